```python
import jax, jax.numpy as jnp
from jax import lax
import numpy as np

D_MODEL = 2048
BATCH = 8
SEQ = 2048
DEPTH = 1

CHUNK = 64
Q_BLOCK = 128
N_MEM = 256
EPS = 1e-6

MLA_HEADS = 8
MLA_NOPE = 128
MLA_ROPE = 64
MLA_QK = MLA_NOPE + MLA_ROPE
MLA_V = 128
MLA_Q_RANK = 512
MLA_KV_RANK = 256
ROPE_THETA = 10000.0

GLA_HEADS = 4
GLA_DK = 128
GLA_DV = 256
GLA_GATE_RANK = 16
GLA_TAU = 16.0

MIX_WIDTH = MLA_HEADS * MLA_V + GLA_HEADS * GLA_DV

MEM_HEADS = 4
MEM_HEAD_DIM = 128
MEM_WIDTH = MEM_HEADS * MEM_HEAD_DIM

D_FF = 5632

IN_SIZES = [
    MLA_Q_RANK,
    MLA_KV_RANK,
    MLA_ROPE,
    GLA_HEADS * GLA_DK,
    GLA_HEADS * GLA_DK,
    GLA_HEADS * GLA_DV,
    GLA_GATE_RANK,
    GLA_HEADS * GLA_DV,
]
IN_WIDTH = int(sum(IN_SIZES))
IN_SPLITS = [int(s) for s in np.cumsum(IN_SIZES)[:-1]]

kernel_name = "hybrid_mla_gla_macaron_memory_block"


def rmsnorm(x, g):
    xf = x.astype(jnp.float32)
    y = xf * lax.rsqrt(jnp.mean(xf * xf, axis=-1, keepdims=True) + EPS)
    return (y * g.astype(jnp.float32)).astype(x.dtype)


def swiglu(h, w_gate, w_up, w_down):
    return (jax.nn.silu(h @ w_gate) * (h @ w_up)) @ w_down


def rope(x, positions):
    half = x.shape[-1] // 2
    inv_freq = ROPE_THETA ** (-jnp.arange(half, dtype=jnp.float32) / half)
    ang = positions.astype(jnp.float32)[..., None] * inv_freq
    cos = jnp.cos(ang)[:, :, None, :]
    sin = jnp.sin(ang)[:, :, None, :]
    xf = x.astype(jnp.float32)
    x1, x2 = xf[..., :half], xf[..., half:]
    return jnp.concatenate([x1 * cos - x2 * sin, x2 * cos + x1 * sin], axis=-1).astype(x.dtype)


def chunk_causal_attention(q, k, v):
    B, S, H, Dk = q.shape
    Dv = v.shape[-1]
    n_blk = S // Q_BLOCK
    scale = Dk ** -0.5
    k_chunk = jnp.arange(S) // CHUNK
    q_blocks = q.reshape(B, n_blk, Q_BLOCK, H, Dk).transpose(1, 0, 2, 3, 4)

    def one_block(args):
        q_blk, blk = args
        s = jnp.einsum('bqhd,bkhd->bhqk', q_blk, k).astype(jnp.float32) * scale
        q_chunk = (blk * Q_BLOCK + jnp.arange(Q_BLOCK)) // CHUNK
        mask = k_chunk[None, :] <= q_chunk[:, None]
        s = jnp.where(mask[None, None], s, -jnp.inf)
        p = jax.nn.softmax(s, axis=-1).astype(v.dtype)
        return jnp.einsum('bhqk,bkhd->bqhd', p, v)

    out = lax.map(one_block, (q_blocks, jnp.arange(n_blk)))
    return out.transpose(1, 0, 2, 3, 4).reshape(B, S, H, Dv)


def gla_chunked(q, k, v, log_a):
    B, S, H, K = q.shape
    V = v.shape[-1]
    n_chunk = S // CHUNK
    f32 = jnp.float32
    qc = q.astype(f32).reshape(B, n_chunk, CHUNK, H, K) * (K ** -0.5)
    kc = k.astype(f32).reshape(B, n_chunk, CHUNK, H, K)
    vc = v.astype(f32).reshape(B, n_chunk, CHUNK, H, V)
    g = log_a.astype(f32).reshape(B, n_chunk, CHUNK, H, K)
    b = jnp.cumsum(g, axis=2)
    b_end = b[:, :, -1]
    k_dec = kc * jnp.exp(b_end[:, :, None] - b)
    u = jnp.einsum('bnchk,bnchv->bnhkv', k_dec, vc)
    decay = jnp.exp(b_end)

    def step(state, inp):
        d, uc = inp
        state = d[..., None] * state + uc
        return state, state

    s0 = jnp.zeros((B, H, K, V), f32)
    _, states = lax.scan(step, s0, (decay.transpose(1, 0, 2, 3), u.transpose(1, 0, 2, 3, 4)))
    states = states.transpose(1, 0, 2, 3, 4)
    o = jnp.einsum('bnchk,bnhkv->bnchv', qc, states)
    return o.reshape(B, S, H, V).astype(v.dtype)


def memory_cross_attention(h, m, w_q, w_k, w_v, w_o, g_q, g_k):
    B, S, _ = h.shape
    M = m.shape[1]
    q = rmsnorm((h @ w_q).reshape(B, S, MEM_HEADS, MEM_HEAD_DIM), g_q)
    k = rmsnorm((m @ w_k).reshape(B, M, MEM_HEADS, MEM_HEAD_DIM), g_k)
    v = (m @ w_v).reshape(B, M, MEM_HEADS, MEM_HEAD_DIM)
    s = jnp.einsum('bqhd,bkhd->bhqk', q, k).astype(jnp.float32) * (MEM_HEAD_DIM ** -0.5)
    p = jax.nn.softmax(s, axis=-1).astype(v.dtype)
    o = jnp.einsum('bhqk,bkhd->bqhd', p, v).reshape(B, S, MEM_WIDTH)
    return o @ w_o


def _fwd_setup_inputs(seed: int = 0) -> dict:
    key = jax.random.key(seed)
    keys = iter(jax.random.split(key, 40))
    f32 = jnp.float32

    def w(fan_in, fan_out):
        return jax.random.normal(next(keys), (DEPTH, fan_in, fan_out), f32) * fan_in ** -0.5

    def g(n):
        return 1.0 + 0.02 * jax.random.normal(next(keys), (DEPTH, n), f32)

    x = jax.random.normal(next(keys), (BATCH, SEQ, D_MODEL), f32)
    mem = jax.random.normal(next(keys), (BATCH, N_MEM, D_MODEL), f32)
    offset = jax.random.randint(next(keys), (BATCH, 1), 0, 64, dtype=jnp.int32) * CHUNK
    positions = (offset + jnp.arange(SEQ, dtype=jnp.int32)[None, :]).astype(jnp.int32)

    return {
        "x": x,
        "mem": mem,
        "positions": positions,
        "ffn1_norm": g(D_MODEL),
        "ffn1_w_gate": w(D_MODEL, D_FF),
        "ffn1_w_up": w(D_MODEL, D_FF),
        "ffn1_w_down": w(D_FF, D_MODEL),
        "mix_norm": g(D_MODEL),
        "w_in": w(D_MODEL, IN_WIDTH),
        "q_a_norm": g(MLA_Q_RANK),
        "w_q_up": w(MLA_Q_RANK, MLA_HEADS * MLA_QK),
        "kv_a_norm": g(MLA_KV_RANK),
        "w_kv_up": w(MLA_KV_RANK, MLA_HEADS * (MLA_NOPE + MLA_V)),
        "mla_q_norm": g(MLA_QK),
        "mla_k_norm": g(MLA_QK),
        "gla_w_gate2": w(GLA_GATE_RANK, GLA_HEADS * GLA_DK),
        "gla_b_gate": 0.1 * jax.random.normal(next(keys), (DEPTH, GLA_HEADS * GLA_DK), f32),
        "gla_out_norm": g(GLA_DV),
        "w_out": w(MIX_WIDTH, D_MODEL),
        "mem_attn_norm": g(D_MODEL),
        "mem_norm": g(D_MODEL),
        "mem_w_q": w(D_MODEL, MEM_WIDTH),
        "mem_w_k": w(D_MODEL, MEM_WIDTH),
        "mem_w_v": w(D_MODEL, MEM_WIDTH),
        "mem_w_o": w(MEM_WIDTH, D_MODEL),
        "mem_q_norm": g(MEM_HEAD_DIM),
        "mem_k_norm": g(MEM_HEAD_DIM),
        "ffn2_norm": g(D_MODEL),
        "ffn2_w_gate": w(D_MODEL, D_FF),
        "ffn2_w_up": w(D_MODEL, D_FF),
        "ffn2_w_down": w(D_FF, D_MODEL),
    }


def _fwd_reference(x, mem, positions, ffn1_norm, ffn1_w_gate, ffn1_w_up, ffn1_w_down,
              mix_norm, w_in, q_a_norm, w_q_up, kv_a_norm, w_kv_up, mla_q_norm,
              mla_k_norm, gla_w_gate2, gla_b_gate, gla_out_norm, w_out,
              mem_attn_norm, mem_norm, mem_w_q, mem_w_k, mem_w_v, mem_w_o,
              mem_q_norm, mem_k_norm, ffn2_norm, ffn2_w_gate, ffn2_w_up, ffn2_w_down):
    B, S, _ = x.shape
    for l in range(DEPTH):
        x = x + 0.5 * swiglu(rmsnorm(x, ffn1_norm[l]), ffn1_w_gate[l], ffn1_w_up[l], ffn1_w_down[l])

        h = rmsnorm(x, mix_norm[l])
        z = h @ w_in[l]
        zq, zkv, zkr, gq, gk, gv, zg, zr = jnp.split(z, IN_SPLITS, axis=-1)

        q = (rmsnorm(zq, q_a_norm[l]) @ w_q_up[l]).reshape(B, S, MLA_HEADS, MLA_QK)
        kv = (rmsnorm(zkv, kv_a_norm[l]) @ w_kv_up[l]).reshape(B, S, MLA_HEADS, MLA_NOPE + MLA_V)
        k_nope, v = kv[..., :MLA_NOPE], kv[..., MLA_NOPE:]
        k_rope = jnp.broadcast_to(zkr[:, :, None, :], (B, S, MLA_HEADS, MLA_ROPE))
        k = jnp.concatenate([k_nope, k_rope], axis=-1)
        q = rmsnorm(q, mla_q_norm[l])
        k = rmsnorm(k, mla_k_norm[l])
        q = jnp.concatenate([q[..., :MLA_NOPE], rope(q[..., MLA_NOPE:], positions)], axis=-1)
        k = jnp.concatenate([k[..., :MLA_NOPE], rope(k[..., MLA_NOPE:], positions)], axis=-1)
        o_mla = chunk_causal_attention(q, k, v).reshape(B, S, MLA_HEADS * MLA_V)

        log_a = jax.nn.log_sigmoid((zg @ gla_w_gate2[l] + gla_b_gate[l]).astype(jnp.float32)) / GLA_TAU
        o_gla = gla_chunked(gq.reshape(B, S, GLA_HEADS, GLA_DK),
                            gk.reshape(B, S, GLA_HEADS, GLA_DK),
                            gv.reshape(B, S, GLA_HEADS, GLA_DV),
                            log_a.reshape(B, S, GLA_HEADS, GLA_DK))
        o_gla = rmsnorm(o_gla, gla_out_norm[l]).reshape(B, S, GLA_HEADS * GLA_DV) * jax.nn.silu(zr)

        x = x + jnp.concatenate([o_mla, o_gla], axis=-1) @ w_out[l]

        x = x + memory_cross_attention(rmsnorm(x, mem_attn_norm[l]), rmsnorm(mem, mem_norm[l]),
                                       mem_w_q[l], mem_w_k[l], mem_w_v[l], mem_w_o[l],
                                       mem_q_norm[l], mem_k_norm[l])

        x = x + 0.5 * swiglu(rmsnorm(x, ffn2_norm[l]), ffn2_w_gate[l], ffn2_w_up[l], ffn2_w_down[l])
    return x


import jax as _jax
import jax.numpy as _jnp

TWIN_FORMAT = 'train_step'
FWD_PARAMS = ['x', 'mem', 'positions', 'ffn1_norm', 'ffn1_w_gate', 'ffn1_w_up', 'ffn1_w_down', 'mix_norm', 'w_in', 'q_a_norm', 'w_q_up', 'kv_a_norm', 'w_kv_up', 'mla_q_norm', 'mla_k_norm', 'gla_w_gate2', 'gla_b_gate', 'gla_out_norm', 'w_out', 'mem_attn_norm', 'mem_norm', 'mem_w_q', 'mem_w_k', 'mem_w_v', 'mem_w_o', 'mem_q_norm', 'mem_k_norm', 'ffn2_norm', 'ffn2_w_gate', 'ffn2_w_up', 'ffn2_w_down']
TWIN_WEIGHTS = ['ffn1_norm', 'ffn1_w_gate', 'ffn1_w_up', 'ffn1_w_down', 'mix_norm', 'w_in', 'q_a_norm', 'w_q_up', 'kv_a_norm', 'w_kv_up', 'mla_q_norm', 'mla_k_norm', 'gla_w_gate2', 'gla_b_gate', 'gla_out_norm', 'w_out', 'mem_attn_norm', 'mem_norm', 'mem_w_q', 'mem_w_k', 'mem_w_v', 'mem_w_o', 'mem_q_norm', 'mem_k_norm', 'ffn2_norm', 'ffn2_w_gate', 'ffn2_w_up', 'ffn2_w_down']
TWIN_DIFF_INPUT = 'x'
TWIN_INPUTS = ['x', 'mem', 'positions', 'ffn1_norm', 'ffn1_w_gate', 'ffn1_w_up', 'ffn1_w_down', 'mix_norm', 'w_in', 'q_a_norm', 'w_q_up', 'kv_a_norm', 'w_kv_up', 'mla_q_norm', 'mla_k_norm', 'gla_w_gate2', 'gla_b_gate', 'gla_out_norm', 'w_out', 'mem_attn_norm', 'mem_norm', 'mem_w_q', 'mem_w_k', 'mem_w_v', 'mem_w_o', 'mem_q_norm', 'mem_k_norm', 'ffn2_norm', 'ffn2_w_gate', 'ffn2_w_up', 'ffn2_w_down', 'loss_target', 'm_ffn1_norm', 'm_ffn1_w_gate', 'm_ffn1_w_up', 'm_ffn1_w_down', 'm_mix_norm', 'm_w_in', 'm_q_a_norm', 'm_w_q_up', 'm_kv_a_norm', 'm_w_kv_up', 'm_mla_q_norm', 'm_mla_k_norm', 'm_gla_w_gate2', 'm_gla_b_gate', 'm_gla_out_norm', 'm_w_out', 'm_mem_attn_norm', 'm_mem_norm', 'm_mem_w_q', 'm_mem_w_k', 'm_mem_w_v', 'm_mem_w_o', 'm_mem_q_norm', 'm_mem_k_norm', 'm_ffn2_norm', 'm_ffn2_w_gate', 'm_ffn2_w_up', 'm_ffn2_w_down', 'v_ffn1_norm', 'v_ffn1_w_gate', 'v_ffn1_w_up', 'v_ffn1_w_down', 'v_mix_norm', 'v_w_in', 'v_q_a_norm', 'v_w_q_up', 'v_kv_a_norm', 'v_w_kv_up', 'v_mla_q_norm', 'v_mla_k_norm', 'v_gla_w_gate2', 'v_gla_b_gate', 'v_gla_out_norm', 'v_w_out', 'v_mem_attn_norm', 'v_mem_norm', 'v_mem_w_q', 'v_mem_w_k', 'v_mem_w_v', 'v_mem_w_o', 'v_mem_q_norm', 'v_mem_k_norm', 'v_ffn2_norm', 'v_ffn2_w_gate', 'v_ffn2_w_up', 'v_ffn2_w_down']
TWIN_OUTPUTS = ['loss', 'grad_x', 'grad_ffn1_norm', 'grad_ffn1_w_gate', 'grad_ffn1_w_up', 'grad_ffn1_w_down', 'grad_mix_norm', 'grad_w_in', 'grad_q_a_norm', 'grad_w_q_up', 'grad_kv_a_norm', 'grad_w_kv_up', 'grad_mla_q_norm', 'grad_mla_k_norm', 'grad_gla_w_gate2', 'grad_gla_b_gate', 'grad_gla_out_norm', 'grad_w_out', 'grad_mem_attn_norm', 'grad_mem_norm', 'grad_mem_w_q', 'grad_mem_w_k', 'grad_mem_w_v', 'grad_mem_w_o', 'grad_mem_q_norm', 'grad_mem_k_norm', 'grad_ffn2_norm', 'grad_ffn2_w_gate', 'grad_ffn2_w_up', 'grad_ffn2_w_down', 'delta_ffn1_norm', 'delta_ffn1_w_gate', 'delta_ffn1_w_up', 'delta_ffn1_w_down', 'delta_mix_norm', 'delta_w_in', 'delta_q_a_norm', 'delta_w_q_up', 'delta_kv_a_norm', 'delta_w_kv_up', 'delta_mla_q_norm', 'delta_mla_k_norm', 'delta_gla_w_gate2', 'delta_gla_b_gate', 'delta_gla_out_norm', 'delta_w_out', 'delta_mem_attn_norm', 'delta_mem_norm', 'delta_mem_w_q', 'delta_mem_w_k', 'delta_mem_w_v', 'delta_mem_w_o', 'delta_mem_q_norm', 'delta_mem_k_norm', 'delta_ffn2_norm', 'delta_ffn2_w_gate', 'delta_ffn2_w_up', 'delta_ffn2_w_down', 'new_m_ffn1_norm', 'new_m_ffn1_w_gate', 'new_m_ffn1_w_up', 'new_m_ffn1_w_down', 'new_m_mix_norm', 'new_m_w_in', 'new_m_q_a_norm', 'new_m_w_q_up', 'new_m_kv_a_norm', 'new_m_w_kv_up', 'new_m_mla_q_norm', 'new_m_mla_k_norm', 'new_m_gla_w_gate2', 'new_m_gla_b_gate', 'new_m_gla_out_norm', 'new_m_w_out', 'new_m_mem_attn_norm', 'new_m_mem_norm', 'new_m_mem_w_q', 'new_m_mem_w_k', 'new_m_mem_w_v', 'new_m_mem_w_o', 'new_m_mem_q_norm', 'new_m_mem_k_norm', 'new_m_ffn2_norm', 'new_m_ffn2_w_gate', 'new_m_ffn2_w_up', 'new_m_ffn2_w_down', 'new_v_ffn1_norm', 'new_v_ffn1_w_gate', 'new_v_ffn1_w_up', 'new_v_ffn1_w_down', 'new_v_mix_norm', 'new_v_w_in', 'new_v_q_a_norm', 'new_v_w_q_up', 'new_v_kv_a_norm', 'new_v_w_kv_up', 'new_v_mla_q_norm', 'new_v_mla_k_norm', 'new_v_gla_w_gate2', 'new_v_gla_b_gate', 'new_v_gla_out_norm', 'new_v_w_out', 'new_v_mem_attn_norm', 'new_v_mem_norm', 'new_v_mem_w_q', 'new_v_mem_w_k', 'new_v_mem_w_v', 'new_v_mem_w_o', 'new_v_mem_q_norm', 'new_v_mem_k_norm', 'new_v_ffn2_norm', 'new_v_ffn2_w_gate', 'new_v_ffn2_w_up', 'new_v_ffn2_w_down']
TWIN_LEAF_KINDS = {'loss': 'loss', 'grad_x': 'grad_x', 'grad_ffn1_norm': 'grad_w', 'grad_ffn1_w_gate': 'grad_w', 'grad_ffn1_w_up': 'grad_w', 'grad_ffn1_w_down': 'grad_w', 'grad_mix_norm': 'grad_w', 'grad_w_in': 'grad_w', 'grad_q_a_norm': 'grad_w', 'grad_w_q_up': 'grad_w', 'grad_kv_a_norm': 'grad_w', 'grad_w_kv_up': 'grad_w', 'grad_mla_q_norm': 'grad_w', 'grad_mla_k_norm': 'grad_w', 'grad_gla_w_gate2': 'grad_w', 'grad_gla_b_gate': 'grad_w', 'grad_gla_out_norm': 'grad_w', 'grad_w_out': 'grad_w', 'grad_mem_attn_norm': 'grad_w', 'grad_mem_norm': 'grad_w', 'grad_mem_w_q': 'grad_w', 'grad_mem_w_k': 'grad_w', 'grad_mem_w_v': 'grad_w', 'grad_mem_w_o': 'grad_w', 'grad_mem_q_norm': 'grad_w', 'grad_mem_k_norm': 'grad_w', 'grad_ffn2_norm': 'grad_w', 'grad_ffn2_w_gate': 'grad_w', 'grad_ffn2_w_up': 'grad_w', 'grad_ffn2_w_down': 'grad_w', 'delta_ffn1_norm': 'delta_w', 'delta_ffn1_w_gate': 'delta_w', 'delta_ffn1_w_up': 'delta_w', 'delta_ffn1_w_down': 'delta_w', 'delta_mix_norm': 'delta_w', 'delta_w_in': 'delta_w', 'delta_q_a_norm': 'delta_w', 'delta_w_q_up': 'delta_w', 'delta_kv_a_norm': 'delta_w', 'delta_w_kv_up': 'delta_w', 'delta_mla_q_norm': 'delta_w', 'delta_mla_k_norm': 'delta_w', 'delta_gla_w_gate2': 'delta_w', 'delta_gla_b_gate': 'delta_w', 'delta_gla_out_norm': 'delta_w', 'delta_w_out': 'delta_w', 'delta_mem_attn_norm': 'delta_w', 'delta_mem_norm': 'delta_w', 'delta_mem_w_q': 'delta_w', 'delta_mem_w_k': 'delta_w', 'delta_mem_w_v': 'delta_w', 'delta_mem_w_o': 'delta_w', 'delta_mem_q_norm': 'delta_w', 'delta_mem_k_norm': 'delta_w', 'delta_ffn2_norm': 'delta_w', 'delta_ffn2_w_gate': 'delta_w', 'delta_ffn2_w_up': 'delta_w', 'delta_ffn2_w_down': 'delta_w', 'new_m_ffn1_norm': 'new_m', 'new_m_ffn1_w_gate': 'new_m', 'new_m_ffn1_w_up': 'new_m', 'new_m_ffn1_w_down': 'new_m', 'new_m_mix_norm': 'new_m', 'new_m_w_in': 'new_m', 'new_m_q_a_norm': 'new_m', 'new_m_w_q_up': 'new_m', 'new_m_kv_a_norm': 'new_m', 'new_m_w_kv_up': 'new_m', 'new_m_mla_q_norm': 'new_m', 'new_m_mla_k_norm': 'new_m', 'new_m_gla_w_gate2': 'new_m', 'new_m_gla_b_gate': 'new_m', 'new_m_gla_out_norm': 'new_m', 'new_m_w_out': 'new_m', 'new_m_mem_attn_norm': 'new_m', 'new_m_mem_norm': 'new_m', 'new_m_mem_w_q': 'new_m', 'new_m_mem_w_k': 'new_m', 'new_m_mem_w_v': 'new_m', 'new_m_mem_w_o': 'new_m', 'new_m_mem_q_norm': 'new_m', 'new_m_mem_k_norm': 'new_m', 'new_m_ffn2_norm': 'new_m', 'new_m_ffn2_w_gate': 'new_m', 'new_m_ffn2_w_up': 'new_m', 'new_m_ffn2_w_down': 'new_m', 'new_v_ffn1_norm': 'new_v', 'new_v_ffn1_w_gate': 'new_v', 'new_v_ffn1_w_up': 'new_v', 'new_v_ffn1_w_down': 'new_v', 'new_v_mix_norm': 'new_v', 'new_v_w_in': 'new_v', 'new_v_q_a_norm': 'new_v', 'new_v_w_q_up': 'new_v', 'new_v_kv_a_norm': 'new_v', 'new_v_w_kv_up': 'new_v', 'new_v_mla_q_norm': 'new_v', 'new_v_mla_k_norm': 'new_v', 'new_v_gla_w_gate2': 'new_v', 'new_v_gla_b_gate': 'new_v', 'new_v_gla_out_norm': 'new_v', 'new_v_w_out': 'new_v', 'new_v_mem_attn_norm': 'new_v', 'new_v_mem_norm': 'new_v', 'new_v_mem_w_q': 'new_v', 'new_v_mem_w_k': 'new_v', 'new_v_mem_w_v': 'new_v', 'new_v_mem_w_o': 'new_v', 'new_v_mem_q_norm': 'new_v', 'new_v_mem_k_norm': 'new_v', 'new_v_ffn2_norm': 'new_v', 'new_v_ffn2_w_gate': 'new_v', 'new_v_ffn2_w_up': 'new_v', 'new_v_ffn2_w_down': 'new_v'}


def _forward(args):
    return _fwd_reference(*[args[k] for k in FWD_PARAMS])


def _output_shape():
    out = _jax.eval_shape(lambda: _forward(_fwd_setup_inputs(0)))
    return out.shape, out.dtype

N_MICROBATCH = 1
ADAM_LR = 0.001
ADAM_B1 = 0.9
ADAM_B2 = 0.999
ADAM_EPS = 1e-08
ADAM_WD = 0.01
ADAM_STEP = 10
PER_EXAMPLE_BATCH_AXIS = {'x': 0, 'mem': 0, 'positions': 0, 'loss_target': 0}
SHARED_INPUTS = []
_WEIGHT_DTYPES = {'ffn1_norm': _jnp.float32, 'ffn1_w_gate': _jnp.float32, 'ffn1_w_up': _jnp.float32, 'ffn1_w_down': _jnp.float32, 'mix_norm': _jnp.float32, 'w_in': _jnp.float32, 'q_a_norm': _jnp.float32, 'w_q_up': _jnp.float32, 'kv_a_norm': _jnp.float32, 'w_kv_up': _jnp.float32, 'mla_q_norm': _jnp.float32, 'mla_k_norm': _jnp.float32, 'gla_w_gate2': _jnp.float32, 'gla_b_gate': _jnp.float32, 'gla_out_norm': _jnp.float32, 'w_out': _jnp.float32, 'mem_attn_norm': _jnp.float32, 'mem_norm': _jnp.float32, 'mem_w_q': _jnp.float32, 'mem_w_k': _jnp.float32, 'mem_w_v': _jnp.float32, 'mem_w_o': _jnp.float32, 'mem_q_norm': _jnp.float32, 'mem_k_norm': _jnp.float32, 'ffn2_norm': _jnp.float32, 'ffn2_w_gate': _jnp.float32, 'ffn2_w_up': _jnp.float32, 'ffn2_w_down': _jnp.float32}
MOMENT_SCALE = {'ffn1_norm': 1.529297e+00, 'ffn1_w_gate': 3.023547e-02, 'ffn1_w_up': 3.014547e-02, 'ffn1_w_down': 4.928675e-02, 'mix_norm': 1.639511e+00, 'w_in': 9.264209e-02, 'q_a_norm': 2.356891e-02, 'w_q_up': 1.377817e-02, 'kv_a_norm': 2.028731e-01, 'w_kv_up': 1.721535e-02, 'mla_q_norm': 2.221310e-01, 'mla_k_norm': 2.221733e-01, 'gla_w_gate2': 1.395596e-02, 'gla_b_gate': 5.191998e-02, 'gla_out_norm': 1.135620e+01, 'w_out': 5.406000e-02, 'mem_attn_norm': 1.167439e-02, 'mem_norm': 8.721452e-02, 'mem_w_q': 2.498351e-02, 'mem_w_k': 2.526837e-02, 'mem_w_v': 6.634739e-02, 'mem_w_o': 1.885690e-02, 'mem_q_norm': 1.213755e+00, 'mem_k_norm': 1.217265e+00, 'ffn2_norm': 1.535475e+00, 'ffn2_w_gate': 1.983776e-02, 'ffn2_w_up': 2.057598e-02, 'ffn2_w_down': 3.284114e-02}


def _to_microbatches(a, axis):
    t = _jnp.moveaxis(a, axis, 0)
    t = t.reshape((N_MICROBATCH, t.shape[0] // N_MICROBATCH) + t.shape[1:])
    return _jnp.moveaxis(t, 1, axis + 1)


def setup_inputs(seed: int = 0) -> dict:
    inp = _fwd_setup_inputs(seed)
    key = _jax.random.fold_in(_jax.random.key(seed), 7919)
    shape, _ = _output_shape()
    out = dict(inp)
    out["loss_target"] = _jax.random.normal(_jax.random.fold_in(key, 0), shape, _jnp.float32)
    for i, name in enumerate(TWIN_WEIGHTS):
        w = inp[name].astype(_jnp.float32)
        if MOMENT_SCALE is None:
            s = _jnp.sqrt(_jnp.mean(_jnp.square(w)) + 1e-30)
        else:
            s = MOMENT_SCALE[name]
        km, kv = _jax.random.split(_jax.random.fold_in(key, i + 1))
        out[name] = w
        out["m_" + name] = s * _jax.random.normal(km, w.shape, _jnp.float32)
        out["v_" + name] = (s * s) * _jax.random.uniform(kv, w.shape, _jnp.float32, 0.5, 1.5)
    if N_MICROBATCH > 1:
        for name, axis in PER_EXAMPLE_BATCH_AXIS.items():
            out[name] = _to_microbatches(out[name], axis)
    return {'x': out['x'], 'mem': out['mem'], 'positions': out['positions'], 'ffn1_norm': out['ffn1_norm'], 'ffn1_w_gate': out['ffn1_w_gate'], 'ffn1_w_up': out['ffn1_w_up'], 'ffn1_w_down': out['ffn1_w_down'], 'mix_norm': out['mix_norm'], 'w_in': out['w_in'], 'q_a_norm': out['q_a_norm'], 'w_q_up': out['w_q_up'], 'kv_a_norm': out['kv_a_norm'], 'w_kv_up': out['w_kv_up'], 'mla_q_norm': out['mla_q_norm'], 'mla_k_norm': out['mla_k_norm'], 'gla_w_gate2': out['gla_w_gate2'], 'gla_b_gate': out['gla_b_gate'], 'gla_out_norm': out['gla_out_norm'], 'w_out': out['w_out'], 'mem_attn_norm': out['mem_attn_norm'], 'mem_norm': out['mem_norm'], 'mem_w_q': out['mem_w_q'], 'mem_w_k': out['mem_w_k'], 'mem_w_v': out['mem_w_v'], 'mem_w_o': out['mem_w_o'], 'mem_q_norm': out['mem_q_norm'], 'mem_k_norm': out['mem_k_norm'], 'ffn2_norm': out['ffn2_norm'], 'ffn2_w_gate': out['ffn2_w_gate'], 'ffn2_w_up': out['ffn2_w_up'], 'ffn2_w_down': out['ffn2_w_down'], 'loss_target': out['loss_target'], 'm_ffn1_norm': out['m_ffn1_norm'], 'm_ffn1_w_gate': out['m_ffn1_w_gate'], 'm_ffn1_w_up': out['m_ffn1_w_up'], 'm_ffn1_w_down': out['m_ffn1_w_down'], 'm_mix_norm': out['m_mix_norm'], 'm_w_in': out['m_w_in'], 'm_q_a_norm': out['m_q_a_norm'], 'm_w_q_up': out['m_w_q_up'], 'm_kv_a_norm': out['m_kv_a_norm'], 'm_w_kv_up': out['m_w_kv_up'], 'm_mla_q_norm': out['m_mla_q_norm'], 'm_mla_k_norm': out['m_mla_k_norm'], 'm_gla_w_gate2': out['m_gla_w_gate2'], 'm_gla_b_gate': out['m_gla_b_gate'], 'm_gla_out_norm': out['m_gla_out_norm'], 'm_w_out': out['m_w_out'], 'm_mem_attn_norm': out['m_mem_attn_norm'], 'm_mem_norm': out['m_mem_norm'], 'm_mem_w_q': out['m_mem_w_q'], 'm_mem_w_k': out['m_mem_w_k'], 'm_mem_w_v': out['m_mem_w_v'], 'm_mem_w_o': out['m_mem_w_o'], 'm_mem_q_norm': out['m_mem_q_norm'], 'm_mem_k_norm': out['m_mem_k_norm'], 'm_ffn2_norm': out['m_ffn2_norm'], 'm_ffn2_w_gate': out['m_ffn2_w_gate'], 'm_ffn2_w_up': out['m_ffn2_w_up'], 'm_ffn2_w_down': out['m_ffn2_w_down'], 'v_ffn1_norm': out['v_ffn1_norm'], 'v_ffn1_w_gate': out['v_ffn1_w_gate'], 'v_ffn1_w_up': out['v_ffn1_w_up'], 'v_ffn1_w_down': out['v_ffn1_w_down'], 'v_mix_norm': out['v_mix_norm'], 'v_w_in': out['v_w_in'], 'v_q_a_norm': out['v_q_a_norm'], 'v_w_q_up': out['v_w_q_up'], 'v_kv_a_norm': out['v_kv_a_norm'], 'v_w_kv_up': out['v_w_kv_up'], 'v_mla_q_norm': out['v_mla_q_norm'], 'v_mla_k_norm': out['v_mla_k_norm'], 'v_gla_w_gate2': out['v_gla_w_gate2'], 'v_gla_b_gate': out['v_gla_b_gate'], 'v_gla_out_norm': out['v_gla_out_norm'], 'v_w_out': out['v_w_out'], 'v_mem_attn_norm': out['v_mem_attn_norm'], 'v_mem_norm': out['v_mem_norm'], 'v_mem_w_q': out['v_mem_w_q'], 'v_mem_w_k': out['v_mem_w_k'], 'v_mem_w_v': out['v_mem_w_v'], 'v_mem_w_o': out['v_mem_w_o'], 'v_mem_q_norm': out['v_mem_q_norm'], 'v_mem_k_norm': out['v_mem_k_norm'], 'v_ffn2_norm': out['v_ffn2_norm'], 'v_ffn2_w_gate': out['v_ffn2_w_gate'], 'v_ffn2_w_up': out['v_ffn2_w_up'], 'v_ffn2_w_down': out['v_ffn2_w_down']}


def _loss(weights, diff, rest, loss_target):
    with _jax.named_scope("forward"):
        args = {**rest, TWIN_DIFF_INPUT: diff, **{k: w.astype(_WEIGHT_DTYPES[k]) for k, w in weights.items()}}
        y = _forward(args)
    with _jax.named_scope("loss_head"):
        err = _jnp.square(y.astype(_jnp.float32) - loss_target)
        return 0.5 * _jnp.sum(_jnp.mean(err, axis=-1)) if err.ndim else 0.5 * err


def _adamw(w, g, m, v):
    m = ADAM_B1 * m + (1.0 - ADAM_B1) * g
    v = ADAM_B2 * v + (1.0 - ADAM_B2) * _jnp.square(g)
    m_hat = m / (1.0 - ADAM_B1 ** ADAM_STEP)
    v_hat = v / (1.0 - ADAM_B2 ** ADAM_STEP)
    delta = -ADAM_LR * (m_hat / (_jnp.sqrt(v_hat) + ADAM_EPS) + ADAM_WD * w)
    return delta, m, v


def reference(x, mem, positions, ffn1_norm, ffn1_w_gate, ffn1_w_up, ffn1_w_down, mix_norm, w_in, q_a_norm, w_q_up, kv_a_norm, w_kv_up, mla_q_norm, mla_k_norm, gla_w_gate2, gla_b_gate, gla_out_norm, w_out, mem_attn_norm, mem_norm, mem_w_q, mem_w_k, mem_w_v, mem_w_o, mem_q_norm, mem_k_norm, ffn2_norm, ffn2_w_gate, ffn2_w_up, ffn2_w_down, loss_target, m_ffn1_norm, m_ffn1_w_gate, m_ffn1_w_up, m_ffn1_w_down, m_mix_norm, m_w_in, m_q_a_norm, m_w_q_up, m_kv_a_norm, m_w_kv_up, m_mla_q_norm, m_mla_k_norm, m_gla_w_gate2, m_gla_b_gate, m_gla_out_norm, m_w_out, m_mem_attn_norm, m_mem_norm, m_mem_w_q, m_mem_w_k, m_mem_w_v, m_mem_w_o, m_mem_q_norm, m_mem_k_norm, m_ffn2_norm, m_ffn2_w_gate, m_ffn2_w_up, m_ffn2_w_down, v_ffn1_norm, v_ffn1_w_gate, v_ffn1_w_up, v_ffn1_w_down, v_mix_norm, v_w_in, v_q_a_norm, v_w_q_up, v_kv_a_norm, v_w_kv_up, v_mla_q_norm, v_mla_k_norm, v_gla_w_gate2, v_gla_b_gate, v_gla_out_norm, v_w_out, v_mem_attn_norm, v_mem_norm, v_mem_w_q, v_mem_w_k, v_mem_w_v, v_mem_w_o, v_mem_q_norm, v_mem_k_norm, v_ffn2_norm, v_ffn2_w_gate, v_ffn2_w_up, v_ffn2_w_down):
    given = dict(x=x, mem=mem, positions=positions, ffn1_norm=ffn1_norm, ffn1_w_gate=ffn1_w_gate, ffn1_w_up=ffn1_w_up, ffn1_w_down=ffn1_w_down, mix_norm=mix_norm, w_in=w_in, q_a_norm=q_a_norm, w_q_up=w_q_up, kv_a_norm=kv_a_norm, w_kv_up=w_kv_up, mla_q_norm=mla_q_norm, mla_k_norm=mla_k_norm, gla_w_gate2=gla_w_gate2, gla_b_gate=gla_b_gate, gla_out_norm=gla_out_norm, w_out=w_out, mem_attn_norm=mem_attn_norm, mem_norm=mem_norm, mem_w_q=mem_w_q, mem_w_k=mem_w_k, mem_w_v=mem_w_v, mem_w_o=mem_w_o, mem_q_norm=mem_q_norm, mem_k_norm=mem_k_norm, ffn2_norm=ffn2_norm, ffn2_w_gate=ffn2_w_gate, ffn2_w_up=ffn2_w_up, ffn2_w_down=ffn2_w_down, loss_target=loss_target, m_ffn1_norm=m_ffn1_norm, m_ffn1_w_gate=m_ffn1_w_gate, m_ffn1_w_up=m_ffn1_w_up, m_ffn1_w_down=m_ffn1_w_down, m_mix_norm=m_mix_norm, m_w_in=m_w_in, m_q_a_norm=m_q_a_norm, m_w_q_up=m_w_q_up, m_kv_a_norm=m_kv_a_norm, m_w_kv_up=m_w_kv_up, m_mla_q_norm=m_mla_q_norm, m_mla_k_norm=m_mla_k_norm, m_gla_w_gate2=m_gla_w_gate2, m_gla_b_gate=m_gla_b_gate, m_gla_out_norm=m_gla_out_norm, m_w_out=m_w_out, m_mem_attn_norm=m_mem_attn_norm, m_mem_norm=m_mem_norm, m_mem_w_q=m_mem_w_q, m_mem_w_k=m_mem_w_k, m_mem_w_v=m_mem_w_v, m_mem_w_o=m_mem_w_o, m_mem_q_norm=m_mem_q_norm, m_mem_k_norm=m_mem_k_norm, m_ffn2_norm=m_ffn2_norm, m_ffn2_w_gate=m_ffn2_w_gate, m_ffn2_w_up=m_ffn2_w_up, m_ffn2_w_down=m_ffn2_w_down, v_ffn1_norm=v_ffn1_norm, v_ffn1_w_gate=v_ffn1_w_gate, v_ffn1_w_up=v_ffn1_w_up, v_ffn1_w_down=v_ffn1_w_down, v_mix_norm=v_mix_norm, v_w_in=v_w_in, v_q_a_norm=v_q_a_norm, v_w_q_up=v_w_q_up, v_kv_a_norm=v_kv_a_norm, v_w_kv_up=v_w_kv_up, v_mla_q_norm=v_mla_q_norm, v_mla_k_norm=v_mla_k_norm, v_gla_w_gate2=v_gla_w_gate2, v_gla_b_gate=v_gla_b_gate, v_gla_out_norm=v_gla_out_norm, v_w_out=v_w_out, v_mem_attn_norm=v_mem_attn_norm, v_mem_norm=v_mem_norm, v_mem_w_q=v_mem_w_q, v_mem_w_k=v_mem_w_k, v_mem_w_v=v_mem_w_v, v_mem_w_o=v_mem_w_o, v_mem_q_norm=v_mem_q_norm, v_mem_k_norm=v_mem_k_norm, v_ffn2_norm=v_ffn2_norm, v_ffn2_w_gate=v_ffn2_w_gate, v_ffn2_w_up=v_ffn2_w_up, v_ffn2_w_down=v_ffn2_w_down)
    weights = {n: given[n] for n in TWIN_WEIGHTS}
    shared = {n: given[n] for n in SHARED_INPUTS}
    per_example = {n: given[n] for n in ['x', 'mem', 'positions']}
    grad_fn = _jax.value_and_grad(_loss, argnums=(0, 1))

    def one_microbatch(ex, loss_target):
        ex = dict(ex)
        diff = ex.pop(TWIN_DIFF_INPUT)
        return grad_fn(weights, diff, {**shared, **ex}, loss_target)

    if N_MICROBATCH == 1:
        loss, (grad_w, grad_x) = one_microbatch(per_example, given["loss_target"])
    else:
        def body(carry, xs):
            loss_sum, grad_sum = carry
            l_k, (gw_k, gx_k) = one_microbatch(xs[0], xs[1])
            with _jax.named_scope("update"):
                return (loss_sum + l_k, _jax.tree.map(_jnp.add, grad_sum, gw_k)), gx_k

        init = (_jnp.zeros((), _jnp.float32), _jax.tree.map(_jnp.zeros_like, weights))
        (loss, grad_w), grad_x = _jax.lax.scan(body, init, (per_example, given["loss_target"]))
    with _jax.named_scope("update"):
        delta_w, new_m, new_v = {}, {}, {}
        for n in TWIN_WEIGHTS:
            delta_w[n], new_m[n], new_v[n] = _adamw(weights[n], grad_w[n], given["m_" + n], given["v_" + n])
    return (loss, grad_x, *[grad_w[n] for n in TWIN_WEIGHTS], *[delta_w[n] for n in TWIN_WEIGHTS],
            *[new_m[n] for n in TWIN_WEIGHTS], *[new_v[n] for n in TWIN_WEIGHTS])
```

```python
import functools
import math

import numpy as np
import jax
import jax.numpy as jnp
from jax import lax
from jax.experimental import pallas as pl
from jax.experimental.pallas import tpu as pltpu

F32 = jnp.float32
BF16 = jnp.bfloat16
MXU_DTYPE = jnp.bfloat16
MESH = pl.DeviceIdType.MESH
ANY = pl.BlockSpec(memory_space=pl.ANY)

LANE = 128
EPS = 1e-6
CHUNK = 64
MLA_HEADS = 8
MLA_NOPE = 128
MLA_ROPE = 64
MLA_QK = MLA_NOPE + MLA_ROPE
MLA_V = 128
MLA_HEAD_PAD = 2 * LANE
ROPE_THETA = 10000.0
GLA_HEADS = 4
GLA_DK = 128
GLA_DV = 256
GLA_GATE_RANK = 16
GLA_TAU = 16.0
MEM_HEADS = 4
MEM_HEAD_DIM = 128
N_CHIPS = 4
N_DEV = 8

ADAM_LR = 0.001
ADAM_B1 = 0.9
ADAM_B2 = 0.999
ADAM_EPS = 1e-08
ADAM_WD = 0.01
ADAM_STEP = 10

VMEM_LIMIT = 56 * 1024 * 1024


def _cparams(sem=None):
    if sem is None:
        return pltpu.CompilerParams(vmem_limit_bytes=VMEM_LIMIT)
    return pltpu.CompilerParams(dimension_semantics=sem, vmem_limit_bytes=VMEM_LIMIT)


def _tile(dim, pref, unit=LANE):
    if dim <= pref:
        return dim
    t = (pref // unit) * unit
    while t > unit and dim % t:
        t -= unit
    assert dim % t == 0, (dim, pref, unit)
    return t


_DIMS = {"nn": (((1,), (0,)), ((), ())), "nt": (((1,), (1,)), ((), ())), "tn": (((0,), (0,)), ((), ()))}


def mm(pairs, mode, out_dtype, *, name, alpha=1.0, res=None, tm=1024, tn=512, tk=512):
    a0, b0 = pairs[0]
    if mode == "nn":
        (M, K), N = a0.shape, b0.shape[1]
    elif mode == "nt":
        (M, K), N = a0.shape, b0.shape[0]
    else:
        (K, M), N = a0.shape, b0.shape[1]
    tm, tn, tk = _tile(M, tm), _tile(N, tn), _tile(K, tk)
    nk = K // tk
    npairs = len(pairs)
    dims = _DIMS[mode]
    if mode == "tn":
        a_spec = pl.BlockSpec((tk, tm), lambda i, j, k: (k, i))
    else:
        a_spec = pl.BlockSpec((tm, tk), lambda i, j, k: (i, k))
    if mode == "nt":
        b_spec = pl.BlockSpec((tn, tk), lambda i, j, k: (j, k))
    else:
        b_spec = pl.BlockSpec((tk, tn), lambda i, j, k: (k, j))
    o_spec = pl.BlockSpec((tm, tn), lambda i, j, k: (i, j))
    has_res = res is not None

    def body(*refs):
        ab = refs[:2 * npairs]
        res_ref = refs[2 * npairs] if has_res else None
        o_ref, acc = refs[-2], refs[-1]
        k = pl.program_id(2)

        @pl.when(k == 0)
        def _():
            acc[...] = jnp.zeros_like(acc)

        for p in range(npairs):
            acc[...] += lax.dot_general(ab[2 * p][...].astype(MXU_DTYPE), ab[2 * p + 1][...].astype(MXU_DTYPE),
                                        dims, preferred_element_type=F32)

        @pl.when(k == nk - 1)
        def _():
            r = acc[...]
            if alpha != 1.0:
                r = r * alpha
            if has_res:
                r = res_ref[...].astype(F32) + r
            o_ref[...] = r.astype(out_dtype)

    ops, specs = [], []
    for a, b in pairs:
        ops += [a, b]
        specs += [a_spec, b_spec]
    if has_res:
        ops.append(res)
        specs.append(o_spec)
    return pl.pallas_call(
        body, name=name, grid=(M // tm, N // tn, nk), in_specs=specs, out_specs=o_spec,
        out_shape=jax.ShapeDtypeStruct((M, N), out_dtype), scratch_shapes=[pltpu.VMEM((tm, tn), F32)],
        compiler_params=_cparams(("parallel", "parallel", "arbitrary")))(*ops)


def _sigmoid(x):
    return 1.0 / (1.0 + jnp.exp(-x))


def ffn_up(n, wg, wu, *, name, tm=1024, tn=512, tk=512):
    M, K = n.shape
    N = wg.shape[1]
    tm, tn, tk = _tile(M, tm), _tile(N, tn), _tile(K, tk)
    nk = K // tk

    def body(n_ref, wg_ref, wu_ref, g_ref, u_ref, a_ref, accg, accu):
        k = pl.program_id(2)

        @pl.when(k == 0)
        def _():
            accg[...] = jnp.zeros_like(accg)
            accu[...] = jnp.zeros_like(accu)

        nv = n_ref[...].astype(MXU_DTYPE)
        accg[...] += jnp.dot(nv, wg_ref[...].astype(MXU_DTYPE), preferred_element_type=F32)
        accu[...] += jnp.dot(nv, wu_ref[...].astype(MXU_DTYPE), preferred_element_type=F32)

        @pl.when(k == nk - 1)
        def _():
            g, u = accg[...], accu[...]
            g_ref[...] = g.astype(g_ref.dtype)
            u_ref[...] = u.astype(u_ref.dtype)
            a_ref[...] = (g * _sigmoid(g) * u).astype(a_ref.dtype)

    o_spec = pl.BlockSpec((tm, tn), lambda i, j, k: (i, j))
    sds = jax.ShapeDtypeStruct((M, N), BF16)
    return pl.pallas_call(
        body, name=name, grid=(M // tm, N // tn, nk),
        in_specs=[pl.BlockSpec((tm, tk), lambda i, j, k: (i, k)), pl.BlockSpec((tk, tn), lambda i, j, k: (k, j)),
                  pl.BlockSpec((tk, tn), lambda i, j, k: (k, j))],
        out_specs=[o_spec, o_spec, o_spec], out_shape=[sds, sds, sds],
        scratch_shapes=[pltpu.VMEM((tm, tn), F32), pltpu.VMEM((tm, tn), F32)],
        compiler_params=_cparams(("parallel", "parallel", "arbitrary")))(n, wg, wu)


def ffn_dact(dy, wd, gate, up, alpha, *, name, tm=1024, tn=512, tk=512):
    M, K = dy.shape
    N = wd.shape[0]
    tm, tn, tk = _tile(M, tm), _tile(N, tn), _tile(K, tk)
    nk = K // tk

    def body(dy_ref, wd_ref, g_ref, u_ref, dg_ref, du_ref, acc):
        k = pl.program_id(2)

        @pl.when(k == 0)
        def _():
            acc[...] = jnp.zeros_like(acc)

        acc[...] += lax.dot_general(dy_ref[...].astype(MXU_DTYPE), wd_ref[...].astype(MXU_DTYPE), _DIMS["nt"],
                                    preferred_element_type=F32)

        @pl.when(k == nk - 1)
        def _():
            da = acc[...] * alpha
            g = g_ref[...].astype(F32)
            u = u_ref[...].astype(F32)
            s = _sigmoid(g)
            du_ref[...] = (da * (g * s)).astype(du_ref.dtype)
            dg_ref[...] = (da * u * (s * (1.0 + g * (1.0 - s)))).astype(dg_ref.dtype)

    o_spec = pl.BlockSpec((tm, tn), lambda i, j, k: (i, j))
    sds = jax.ShapeDtypeStruct((M, N), BF16)
    return pl.pallas_call(
        body, name=name, grid=(M // tm, N // tn, nk),
        in_specs=[pl.BlockSpec((tm, tk), lambda i, j, k: (i, k)), pl.BlockSpec((tn, tk), lambda i, j, k: (j, k)),
                  o_spec, o_spec],
        out_specs=[o_spec, o_spec], out_shape=[sds, sds], scratch_shapes=[pltpu.VMEM((tm, tn), F32)],
        compiler_params=_cparams(("parallel", "parallel", "arbitrary")))(dy, wd, gate, up)


class V:
    def __init__(self, arr, off=0, w=None, hs=0, diff=True):
        self.arr, self.off, self.hs, self.diff = arr, off, hs, diff
        self.w = arr.shape[1] - off if w is None else w

    def window(self, heads, tr):
        width = self.arr.shape[1]
        ext = (heads - 1) * self.hs + self.w
        ww = LANE
        while ww < width:
            if ww >= ext and self.off // ww == (self.off + ext - 1) // ww and width % ww == 0:
                break
            ww *= 2
        else:
            ww = width
        blk = self.off // ww
        return pl.BlockSpec((tr, ww), lambda i, blk=blk: (i, blk)), self.off - blk * ww


def _const_spec(c):
    return pl.BlockSpec(c.shape, lambda i: (0, 0))


def row_fwd(fn, rows, consts, outs, out_map, *, heads=1, tr=256, name):
    S = rows[0].arr.shape[0]
    tr = _tile(S, tr, 8)
    wins = [v.window(heads, tr) for v in rows]
    nr, nc = len(rows), len(consts)

    def body(*refs):
        row_refs, const_refs, out_refs = refs[:nr], refs[nr:nr + nc], refs[nr + nc:]
        cv = [c[...].astype(F32) for c in const_refs]
        for h in range(heads):
            rv = []
            for v, (_, io), r in zip(rows, wins, row_refs):
                lo = io + h * v.hs
                rv.append(r[:, lo:lo + v.w].astype(F32))
            res = fn(*rv, *cv)
            for (ai, off, hs), o in zip(out_map, res):
                lo = off + h * hs
                out_refs[ai][:, lo:lo + o.shape[1]] = o.astype(out_refs[ai].dtype)

    return pl.pallas_call(
        body, name=name, grid=(S // tr,),
        in_specs=[w[0] for w in wins] + [_const_spec(c) for c in consts],
        out_specs=[pl.BlockSpec((tr, w), lambda i: (i, 0)) for w, _ in outs],
        out_shape=[jax.ShapeDtypeStruct((S, w), d) for w, d in outs],
        compiler_params=_cparams(("parallel",)))(*[v.arr for v in rows], *consts)


def row_bwd(fn, rows, consts, cots, *, const_diff, heads=1, tr=256, res=None, row_dtype=F32, name):
    S = rows[0].arr.shape[0]
    tr = _tile(S, tr, 8)
    nr, nc, nct = len(rows), len(consts), len(cots)
    wins = [v.window(heads, tr) for v in rows]
    cwins = [v.window(heads, tr) for v in cots]
    drows = [k for k, v in enumerate(rows) if v.diff]
    dconsts = [k for k in range(nc) if const_diff[k]]
    has_res = res is not None
    ngrid = S // tr

    def body(*refs):
        row_refs = refs[:nr]
        const_refs = refs[nr:nr + nc]
        cot_refs = refs[nr + nc:nr + nc + nct]
        p = nr + nc + nct
        res_ref = refs[p] if has_res else None
        p += int(has_res)
        grow_refs = refs[p:p + len(drows)]
        gconst_refs = refs[p + len(drows):]
        i = pl.program_id(0)
        cv = [c[...].astype(F32) for c in const_refs]
        shared = [None] * len(drows)
        gc_sum = [None] * len(dconsts)
        for h in range(heads):
            rv = []
            for v, (_, io), r in zip(rows, wins, row_refs):
                lo = io + h * v.hs
                rv.append(r[:, lo:lo + v.w].astype(F32))
            ct = []
            for v, (_, io), r in zip(cots, cwins, cot_refs):
                lo = io + h * v.hs
                ct.append(r[:, lo:lo + v.w].astype(F32))

            def closed(*d):
                rr, cc = list(rv), list(cv)
                for k, val in zip(drows, d[:len(drows)]):
                    rr[k] = val
                for k, val in zip(dconsts, d[len(drows):]):
                    cc[k] = val
                return tuple(fn(*rr, *cc))

            _, vjp = jax.vjp(closed, *[rv[k] for k in drows], *[cv[k] for k in dconsts])
            grads = vjp(tuple(ct))
            for n, k in enumerate(drows):
                g = grads[n]
                if rows[k].hs == 0 and heads > 1:
                    shared[n] = g if shared[n] is None else shared[n] + g
                else:
                    if n == 0 and has_res:
                        g = g + res_ref[:, h * rows[k].w:(h + 1) * rows[k].w].astype(F32)
                    grow_refs[n][:, h * rows[k].w:(h + 1) * rows[k].w] = g.astype(row_dtype)
            for n in range(len(dconsts)):
                g = grads[len(drows) + n]
                gc_sum[n] = g if gc_sum[n] is None else gc_sum[n] + g
        for n, k in enumerate(drows):
            if shared[n] is not None:
                g = shared[n]
                if n == 0 and has_res:
                    g = g + res_ref[...].astype(F32)
                grow_refs[n][...] = g.astype(row_dtype)

        @pl.when(i == 0)
        def _():
            for n in range(len(dconsts)):
                gconst_refs[n][...] = gc_sum[n]

        @pl.when(i > 0)
        def _():
            for n in range(len(dconsts)):
                gconst_refs[n][...] += gc_sum[n]

    gw = [rows[k].w * (heads if rows[k].hs else 1) for k in drows]
    in_specs = [w[0] for w in wins] + [_const_spec(c) for c in consts] + [w[0] for w in cwins]
    ops = [v.arr for v in rows] + list(consts) + [v.arr for v in cots]
    if has_res:
        in_specs.append(pl.BlockSpec((tr, gw[0]), lambda i: (i, 0)))
        ops.append(res)
    out_specs = [pl.BlockSpec((tr, w), lambda i: (i, 0)) for w in gw]
    out_shape = [jax.ShapeDtypeStruct((S, w), row_dtype) for w in gw]
    for k in dconsts:
        out_specs.append(_const_spec(consts[k]))
        out_shape.append(jax.ShapeDtypeStruct(consts[k].shape, F32))
    del ngrid
    return pl.pallas_call(body, name=name, grid=(S // tr,), in_specs=in_specs, out_specs=out_specs,
                          out_shape=out_shape, compiler_params=_cparams(("arbitrary",)))(*ops)


def _rms(x, g, n=None):
    n = x.shape[-1] if n is None else n
    ms = jnp.sum(x * x, axis=-1, keepdims=True) * (1.0 / n)
    return x * lax.rsqrt(ms + EPS) * g


def rms_fn(x, g):
    return (_rms(x, g),)


def qk_prep_fn(nope, rope, cos, sin, gn, gr, rot):
    ms = (jnp.sum(nope * nope, axis=-1, keepdims=True) + jnp.sum(rope * rope, axis=-1, keepdims=True)) * (1.0 / MLA_QK)
    r = lax.rsqrt(ms + EPS)
    on = nope * r * gn
    orr = rope * r * gr
    turned = jnp.dot(orr, rot, precision=lax.Precision.HIGHEST, preferred_element_type=F32)
    return on, orr * cos + turned * sin


def gla_out_fn(o, zr, g):
    return (_rms(o, g) * (zr * _sigmoid(zr)),)


def gate_fn(pre, b):
    t = pre + b
    return ((jnp.minimum(t, 0.0) - jnp.log(1.0 + jnp.exp(-jnp.abs(t)))) * (1.0 / GLA_TAU),)


def _attn_probs(q_ref, k_ref, scale, causal, tq):
    s = lax.dot_general(q_ref[...].astype(MXU_DTYPE), k_ref[...].astype(MXU_DTYPE), _DIMS["nt"],
                        preferred_element_type=F32) * scale
    if causal:
        i = pl.program_id(1)
        qc = (i * tq + lax.broadcasted_iota(jnp.int32, s.shape, 0)) // CHUNK
        kc = lax.broadcasted_iota(jnp.int32, s.shape, 1) // CHUNK
        s = jnp.where(kc <= qc, s, -1e30)
    m = jnp.max(s, axis=-1, keepdims=True)
    e = jnp.exp(s - m)
    return e / jnp.sum(e, axis=-1, keepdims=True)


def attn_fwd(q, k, v, *, heads, dk, dv, v_off, v_hs, scale, causal, name, tq=256):
    Sq, Sk = q.shape[0], k.shape[0]
    tq = _tile(Sq, tq, 8)

    def body(q_ref, k_ref, v_ref, o_ref):
        p = _attn_probs(q_ref, k_ref, scale, causal, tq)
        o_ref[...] = jnp.dot(p.astype(MXU_DTYPE), v_ref[...].astype(MXU_DTYPE),
                             preferred_element_type=F32).astype(o_ref.dtype)

    return pl.pallas_call(
        body, name=name, grid=(heads, Sq // tq),
        in_specs=[pl.BlockSpec((tq, dk), lambda h, i: (i, h)), pl.BlockSpec((Sk, dk), lambda h, i: (0, h)),
                  pl.BlockSpec((Sk, dv), lambda h, i: (0, v_off + h * v_hs))],
        out_specs=pl.BlockSpec((tq, dv), lambda h, i: (i, h)),
        out_shape=jax.ShapeDtypeStruct((Sq, heads * dv), BF16),
        compiler_params=_cparams(("parallel", "parallel")))(q, k, v)


def attn_bwd(q, k, v, do, *, heads, dk, dv, v_off, v_hs, scale, causal, name, tq=256):
    Sq, Sk = q.shape[0], k.shape[0]
    tq = _tile(Sq, tq, 8)

    def body(q_ref, k_ref, v_ref, do_ref, dq_ref, dk_ref, dv_ref):
        i = pl.program_id(1)
        p = _attn_probs(q_ref, k_ref, scale, causal, tq)
        dob = do_ref[...].astype(MXU_DTYPE)
        dp = lax.dot_general(dob, v_ref[...].astype(MXU_DTYPE), _DIMS["nt"], preferred_element_type=F32)
        delta = jnp.sum(p * dp, axis=-1, keepdims=True)
        ds = (p * (dp - delta) * scale).astype(MXU_DTYPE)
        dq_ref[...] = jnp.dot(ds, k_ref[...].astype(MXU_DTYPE), preferred_element_type=F32)
        dkc = lax.dot_general(ds, q_ref[...].astype(MXU_DTYPE), _DIMS["tn"], preferred_element_type=F32)
        dvc = lax.dot_general(p.astype(MXU_DTYPE), dob, _DIMS["tn"], preferred_element_type=F32)

        @pl.when(i == 0)
        def _():
            dk_ref[...] = dkc
            dv_ref[...] = dvc

        @pl.when(i > 0)
        def _():
            dk_ref[...] += dkc
            dv_ref[...] += dvc

    return pl.pallas_call(
        body, name=name, grid=(heads, Sq // tq),
        in_specs=[pl.BlockSpec((tq, dk), lambda h, i: (i, h)), pl.BlockSpec((Sk, dk), lambda h, i: (0, h)),
                  pl.BlockSpec((Sk, dv), lambda h, i: (0, v_off + h * v_hs)),
                  pl.BlockSpec((tq, dv), lambda h, i: (i, h))],
        out_specs=[pl.BlockSpec((tq, dk), lambda h, i: (i, h)), pl.BlockSpec((Sk, dk), lambda h, i: (0, h)),
                   pl.BlockSpec((Sk, dv), lambda h, i: (0, h))],
        out_shape=[jax.ShapeDtypeStruct((Sq, heads * dk), F32), jax.ShapeDtypeStruct((Sk, heads * dk), F32),
                   jax.ShapeDtypeStruct((Sk, heads * dv), F32)],
        compiler_params=_cparams(("parallel", "arbitrary")))(q, k, v, do)


def _gla_chunk(k_ref, la_ref, tri_ref):
    g = la_ref[...].astype(F32)
    b = jnp.dot(tri_ref[...], g, precision=lax.Precision.HIGHEST, preferred_element_type=F32)
    b_end = jnp.sum(g, axis=0, keepdims=True)
    e = jnp.exp(b_end - b)
    return k_ref[...].astype(F32) * e, e, jnp.exp(b_end)


def gla_fwd(z, la, tri, *, q_off, k_off, v_off, name):
    S = z.shape[0]
    nchunk = S // CHUNK
    H, DK, DV = GLA_HEADS, GLA_DK, GLA_DV
    qb, kb, vb = q_off // DK, k_off // DK, v_off // DV
    qscale = DK ** -0.5

    def body(q_ref, k_ref, v_ref, la_ref, tri_ref, o_ref, st_ref, state):
        c = pl.program_id(1)

        @pl.when(c == 0)
        def _():
            state[...] = jnp.zeros_like(state)

        kdec, _, decay = _gla_chunk(k_ref, la_ref, tri_ref)
        ut = lax.dot_general(v_ref[...].astype(MXU_DTYPE), kdec.astype(MXU_DTYPE), _DIMS["tn"],
                             preferred_element_type=F32)
        new = state[...] * decay + ut
        state[...] = new
        st_ref[...] = new
        qs = (q_ref[...].astype(F32) * qscale).astype(MXU_DTYPE)
        o_ref[...] = lax.dot_general(qs, new.astype(MXU_DTYPE), _DIMS["nt"], preferred_element_type=F32)

    return pl.pallas_call(
        body, name=name, grid=(H, nchunk),
        in_specs=[pl.BlockSpec((CHUNK, DK), lambda h, c: (c, qb + h)), pl.BlockSpec((CHUNK, DK), lambda h, c: (c, kb + h)),
                  pl.BlockSpec((CHUNK, DV), lambda h, c: (c, vb + h)), pl.BlockSpec((CHUNK, DK), lambda h, c: (c, h)),
                  pl.BlockSpec((CHUNK, CHUNK), lambda h, c: (0, 0))],
        out_specs=[pl.BlockSpec((CHUNK, DV), lambda h, c: (c, h)),
                   pl.BlockSpec((None, None, DV, DK), lambda h, c: (h, c, 0, 0))],
        out_shape=[jax.ShapeDtypeStruct((S, H * DV), F32), jax.ShapeDtypeStruct((H, nchunk, DV, DK), F32)],
        scratch_shapes=[pltpu.VMEM((DV, DK), F32)],
        compiler_params=_cparams(("parallel", "arbitrary")))(z, z, z, la, tri)


def gla_bwd(z, la, tri, trit, states, do, *, q_off, k_off, v_off, name):
    S = z.shape[0]
    nchunk = S // CHUNK
    H, DK, DV = GLA_HEADS, GLA_DK, GLA_DV
    qb, kb, vb = q_off // DK, k_off // DK, v_off // DV
    qscale = DK ** -0.5
    last = nchunk - 1

    def body(q_ref, k_ref, v_ref, la_ref, tri_ref, trit_ref, st_ref, sp_ref, do_ref, dq_ref, dk_ref, dv_ref, dla_ref,
             dstate):
        c = pl.program_id(1)
        cc = last - c

        @pl.when(c == 0)
        def _():
            dstate[...] = jnp.zeros_like(dstate)

        kdec, e, decay = _gla_chunk(k_ref, la_ref, tri_ref)
        kf = k_ref[...].astype(F32)
        dob = do_ref[...].astype(MXU_DTYPE)
        stb = st_ref[...].astype(MXU_DTYPE)
        qs = (q_ref[...].astype(F32) * qscale).astype(MXU_DTYPE)
        dq_ref[...] = jnp.dot(dob, stb, preferred_element_type=F32) * qscale
        dst = dstate[...] + lax.dot_general(dob, qs, _DIMS["tn"], preferred_element_type=F32)
        prev = jnp.where(cc > 0, sp_ref[...], 0.0)
        ddecay = jnp.sum(dst * prev, axis=0, keepdims=True)
        dstate[...] = dst * decay
        dub = dst.astype(MXU_DTYPE)
        vb16 = v_ref[...].astype(MXU_DTYPE)
        dv_ref[...] = lax.dot_general(kdec.astype(MXU_DTYPE), dub, _DIMS["nt"], preferred_element_type=F32)
        dkdec = jnp.dot(vb16, dub, preferred_element_type=F32)
        dk_ref[...] = dkdec * e
        w = dkdec * kf * e
        db_end = jnp.sum(w, axis=0, keepdims=True) + ddecay * decay
        dla_ref[...] = db_end - jnp.dot(trit_ref[...], w, precision=lax.Precision.HIGHEST, preferred_element_type=F32)

    rows = lambda blk: (lambda h, c: (last - c, blk + h))
    return pl.pallas_call(
        body, name=name, grid=(H, nchunk),
        in_specs=[pl.BlockSpec((CHUNK, DK), rows(qb)), pl.BlockSpec((CHUNK, DK), rows(kb)),
                  pl.BlockSpec((CHUNK, DV), rows(vb)), pl.BlockSpec((CHUNK, DK), rows(0)),
                  pl.BlockSpec((CHUNK, CHUNK), lambda h, c: (0, 0)), pl.BlockSpec((CHUNK, CHUNK), lambda h, c: (0, 0)),
                  pl.BlockSpec((None, None, DV, DK), lambda h, c: (h, last - c, 0, 0)),
                  pl.BlockSpec((None, None, DV, DK), lambda h, c: (h, jnp.maximum(last - c - 1, 0), 0, 0)),
                  pl.BlockSpec((CHUNK, DV), rows(0))],
        out_specs=[pl.BlockSpec((CHUNK, DK), rows(0)), pl.BlockSpec((CHUNK, DK), rows(0)),
                   pl.BlockSpec((CHUNK, DV), rows(0)), pl.BlockSpec((CHUNK, DK), rows(0))],
        out_shape=[jax.ShapeDtypeStruct((S, H * DK), F32), jax.ShapeDtypeStruct((S, H * DK), F32),
                   jax.ShapeDtypeStruct((S, H * DV), F32), jax.ShapeDtypeStruct((S, H * DK), F32)],
        scratch_shapes=[pltpu.VMEM((DV, DK), F32)],
        compiler_params=_cparams(("parallel", "arbitrary")))(z, z, z, la, tri, trit, states, states, do)


def loss_head(y, target, *, name, tr=256):
    S, D = y.shape
    tr = _tile(S, tr, 8)

    def body(y_ref, t_ref, dy_ref, loss_ref):
        i = pl.program_id(0)
        err = y_ref[...] - t_ref[...]
        dy_ref[...] = err * (1.0 / D)
        part = jnp.zeros((1, LANE), F32) + 0.5 * jnp.sum(jnp.sum(err * err, axis=-1, keepdims=True) * (1.0 / D))

        @pl.when(i == 0)
        def _():
            loss_ref[...] = part

        @pl.when(i > 0)
        def _():
            loss_ref[...] += part

    spec = pl.BlockSpec((tr, D), lambda i: (i, 0))
    return pl.pallas_call(
        body, name=name, grid=(S // tr,), in_specs=[spec, spec],
        out_specs=[spec, pl.BlockSpec((1, LANE), lambda i: (0, 0))],
        out_shape=[jax.ShapeDtypeStruct((S, D), F32), jax.ShapeDtypeStruct((1, LANE), F32)],
        compiler_params=_cparams(("arbitrary",)))(y, target)


def add_slots(x, out_dtype, *, name, tr=256):
    n, R, C = x.shape
    tr = _tile(R, tr, 8)

    def body(x_ref, o_ref):
        acc = x_ref[0].astype(F32)
        for s in range(1, n):
            acc = acc + x_ref[s].astype(F32)
        o_ref[...] = acc.astype(out_dtype)

    return pl.pallas_call(
        body, name=name, grid=(R // tr,), in_specs=[pl.BlockSpec((n, tr, C), lambda i: (0, i, 0))],
        out_specs=pl.BlockSpec((tr, C), lambda i: (i, 0)), out_shape=jax.ShapeDtypeStruct((R, C), out_dtype),
        compiler_params=_cparams(("parallel",)))(x)


def add_pair(a, b, out_dtype, *, name, tr=256):
    n, R, C = a.shape
    tr = _tile(R, tr, 8)

    def body(a_ref, b_ref, o_ref):
        o_ref[...] = (a_ref[...].astype(F32) + b_ref[...].astype(F32)).astype(out_dtype)

    spec = pl.BlockSpec((None, tr, C), lambda s, i: (s, i, 0))
    return pl.pallas_call(
        body, name=name, grid=(n, R // tr), in_specs=[spec, spec], out_specs=spec,
        out_shape=jax.ShapeDtypeStruct((n, R, C), out_dtype), compiler_params=_cparams(("parallel", "parallel")))(a, b)


def adamw(w, g, m, v, *, name, tr=256):
    R, C = w.shape
    tr = _tile(R, tr, 8)
    c1 = 1.0 / (1.0 - ADAM_B1 ** ADAM_STEP)
    c2 = 1.0 / (1.0 - ADAM_B2 ** ADAM_STEP)

    def body(w_ref, g_ref, m_ref, v_ref, d_ref, nm_ref, nv_ref):
        gv = g_ref[...]
        nm = ADAM_B1 * m_ref[...] + (1.0 - ADAM_B1) * gv
        nv = ADAM_B2 * v_ref[...] + (1.0 - ADAM_B2) * (gv * gv)
        nm_ref[...] = nm
        nv_ref[...] = nv
        d_ref[...] = -ADAM_LR * ((nm * c1) / (jnp.sqrt(nv * c2) + ADAM_EPS) + ADAM_WD * w_ref[...])

    spec = pl.BlockSpec((tr, C), lambda i: (i, 0))
    sds = jax.ShapeDtypeStruct((R, C), F32)
    return pl.pallas_call(body, name=name, grid=(R // tr,), in_specs=[spec] * 4, out_specs=[spec] * 3,
                          out_shape=[sds] * 3, compiler_params=_cparams(("parallel",)))(w, g, m, v)


def _place():
    x, y, c = lax.axis_index("x"), lax.axis_index("y"), lax.axis_index("c")
    chips = [(1 - x, y), (x, 1 - y), (1 - x, 1 - y)]
    return x, y, c, chips


def gather_chips(shards, *, name):
    n = len(shards)

    def body(*refs):
        ins, outs = refs[:n], refs[n:2 * n]
        send, recv, loc = refs[2 * n:]
        x, y, c, chips = _place()
        mine = 2 * x + y
        local = []
        for i in range(n):
            cp = pltpu.make_async_copy(ins[i], outs[i].at[mine], loc.at[i])
            cp.start()
            local.append(cp)
            for k, (px, py) in enumerate(chips):
                pltpu.make_async_remote_copy(src_ref=ins[i], dst_ref=outs[i].at[mine], send_sem=send.at[i, k],
                                             recv_sem=recv.at[i, k], device_id=(px, py, c),
                                             device_id_type=MESH).start()
        for i in range(n):
            for k, (px, py) in enumerate(chips):
                pltpu.make_async_remote_copy(src_ref=ins[i], dst_ref=outs[i].at[2 * px + py], send_sem=send.at[i, k],
                                             recv_sem=recv.at[i, k], device_id=(px, py, c),
                                             device_id_type=MESH).wait()
        for cp in local:
            cp.wait()

    return pl.pallas_call(
        body, name=name, in_specs=[ANY] * n, out_specs=[ANY] * n,
        out_shape=[jax.ShapeDtypeStruct((N_CHIPS,) + s.shape, s.dtype) for s in shards],
        scratch_shapes=[pltpu.SemaphoreType.DMA((n, 3)), pltpu.SemaphoreType.DMA((n, 3)), pltpu.SemaphoreType.DMA((n,))],
    )(*shards)


def swap_halves(gs, *, name):
    n = len(gs)

    def body(*refs):
        ins, kept, got = refs[:n], refs[n:2 * n], refs[2 * n:3 * n]
        send, recv, loc = refs[3 * n:]
        x, y, c, _ = _place()
        sib = (x, y, 1 - c)
        pending = []
        for i in range(n):
            for s in range(N_CHIPS):
                lc = pltpu.make_async_copy(ins[i].at[s, c], kept[i].at[s], loc.at[i, s])
                lc.start()
                rc = pltpu.make_async_remote_copy(src_ref=ins[i].at[s, 1 - c], dst_ref=got[i].at[s],
                                                  send_sem=send.at[i, s], recv_sem=recv.at[i, s], device_id=sib,
                                                  device_id_type=MESH)
                rc.start()
                pending += [lc, rc]
        for cp in pending:
            cp.wait()

    half = [jax.ShapeDtypeStruct((N_CHIPS,) + g.shape[2:], g.dtype) for g in gs]
    res = pl.pallas_call(
        body, name=name, in_specs=[ANY] * n, out_specs=[ANY] * (2 * n), out_shape=half + half,
        scratch_shapes=[pltpu.SemaphoreType.DMA((n, N_CHIPS)), pltpu.SemaphoreType.DMA((n, N_CHIPS)),
                        pltpu.SemaphoreType.DMA((n, N_CHIPS))],
    )(*gs)
    return res[:n], res[n:]


def exchange_chips(ps, *, name):
    n = len(ps)

    def body(*refs):
        ins, outs = refs[:n], refs[n:2 * n]
        send, recv, loc = refs[2 * n:]
        x, y, c, chips = _place()
        mine = 2 * x + y
        local = []
        for i in range(n):
            cp = pltpu.make_async_copy(ins[i].at[mine], outs[i].at[mine], loc.at[i])
            cp.start()
            local.append(cp)
            for k, (px, py) in enumerate(chips):
                pltpu.make_async_remote_copy(src_ref=ins[i].at[2 * px + py], dst_ref=outs[i].at[mine],
                                             send_sem=send.at[i, k], recv_sem=recv.at[i, k], device_id=(px, py, c),
                                             device_id_type=MESH).start()
        for i in range(n):
            for k, (px, py) in enumerate(chips):
                pltpu.make_async_remote_copy(src_ref=ins[i].at[2 * px + py], dst_ref=outs[i].at[2 * px + py],
                                             send_sem=send.at[i, k], recv_sem=recv.at[i, k], device_id=(px, py, c),
                                             device_id_type=MESH).wait()
        for cp in local:
            cp.wait()

    return pl.pallas_call(
        body, name=name, in_specs=[ANY] * n, out_specs=[ANY] * n,
        out_shape=[jax.ShapeDtypeStruct(p.shape, p.dtype) for p in ps],
        scratch_shapes=[pltpu.SemaphoreType.DMA((n, 3)), pltpu.SemaphoreType.DMA((n, 3)), pltpu.SemaphoreType.DMA((n,))],
    )(*ps)


def join_halves(fs, *, name):
    n = len(fs)

    def body(*refs):
        ins, outs = refs[:n], refs[n:2 * n]
        send, recv, loc = refs[2 * n:]
        x, y, c, _ = _place()
        sib = (x, y, 1 - c)
        pending = []
        for i in range(n):
            lc = pltpu.make_async_copy(ins[i], outs[i].at[c], loc.at[i])
            lc.start()
            pltpu.make_async_remote_copy(src_ref=ins[i], dst_ref=outs[i].at[c], send_sem=send.at[i],
                                         recv_sem=recv.at[i], device_id=sib, device_id_type=MESH).start()
            pending.append(lc)
        for i in range(n):
            pltpu.make_async_remote_copy(src_ref=ins[i], dst_ref=outs[i].at[1 - c], send_sem=send.at[i],
                                         recv_sem=recv.at[i], device_id=sib, device_id_type=MESH).wait()
        for cp in pending:
            cp.wait()

    return pl.pallas_call(
        body, name=name, in_specs=[ANY] * n, out_specs=[ANY] * n,
        out_shape=[jax.ShapeDtypeStruct((2,) + f.shape, f.dtype) for f in fs],
        scratch_shapes=[pltpu.SemaphoreType.DMA((n,)), pltpu.SemaphoreType.DMA((n,)), pltpu.SemaphoreType.DMA((n,))],
    )(*fs)


def allreduce_small(v, *, name):
    m_per, n = v.shape

    def body(x_ref, sum_ref, all_ref, send_sems, recv_sems, local_sem):
        x, y, c, chips = _place()
        me, sibling = (x, y, c), (x, y, 1 - c)

        def rows(px, py, pc):
            return all_ref.at[pl.ds((4 * px + 2 * py + pc) * m_per, m_per), :]

        def copy(k, block, to, src=None):
            return pltpu.make_async_remote_copy(src_ref=rows(*block) if src is None else src, dst_ref=rows(*block),
                                                send_sem=send_sems.at[k], recv_sem=recv_sems.at[k], device_id=to,
                                                device_id_type=MESH)

        mine = pltpu.make_async_copy(x_ref, rows(*me), local_sem)
        mine.start()
        first = [copy(0, me, sibling, src=x_ref)]
        first += [copy(1 + j, me, (*chip, c), src=x_ref) for j, chip in enumerate(chips)]
        for cp in first:
            cp.start()
        passed = [copy(4 + j, (*chip, c), sibling) for j, chip in enumerate(chips)]
        for j, chip in enumerate(chips):
            copy(1 + j, (*chip, c), me).wait_recv()
            passed[j].start()
        copy(0, sibling, me).wait_recv()
        for j, chip in enumerate(chips):
            copy(4 + j, (*chip, 1 - c), me).wait_recv()
        for cp in first + passed:
            cp.wait_send()
        mine.wait()
        acc = all_ref[0:m_per, :]
        for d in range(1, N_DEV):
            acc = acc + all_ref[d * m_per:(d + 1) * m_per, :]
        sum_ref[...] = acc

    vm = pl.BlockSpec(memory_space=pltpu.VMEM)
    return pl.pallas_call(
        body, name=name, in_specs=[vm], out_specs=vm, out_shape=jax.ShapeDtypeStruct((m_per, n), F32),
        scratch_shapes=[pltpu.VMEM((N_DEV * m_per, n), F32), pltpu.SemaphoreType.DMA((7,)),
                        pltpu.SemaphoreType.DMA((7,)), pltpu.SemaphoreType.DMA],
    )(v)


def reduce_scatter(grads, *, tag):
    halves = [g.reshape(N_CHIPS, 2, g.shape[1] // 2, g.shape[2]) for g in grads]
    kept, got = swap_halves(halves, name=f"{tag}_swap")
    chip_sum = [add_pair(a, b, F32, name=f"{tag}_add2_{i}") for i, (a, b) in enumerate(zip(kept, got))]
    parts = exchange_chips(chip_sum, name=f"{tag}_xchg")
    total = [add_slots(p, F32, name=f"{tag}_add4_{i}") for i, p in enumerate(parts)]
    both = join_halves(total, name=f"{tag}_join")
    return [b.reshape(g.shape[1], g.shape[2]) for b, g in zip(both, grads)]


def _cols_to_slots(w):
    r, c4 = w.shape
    return w.reshape(r, N_CHIPS, c4 // N_CHIPS).transpose(1, 0, 2)


def _slots_to_cols(w):
    n, r, c = w.shape
    return w.transpose(1, 0, 2).reshape(r, n * c)


def _pad_cols(a, width):
    return jnp.pad(a, ((0, 0), (0, width - a.shape[1])))


class InLayout:
    def __init__(self, q_rank, kv_rank):
        gk = GLA_HEADS * GLA_DK
        gv = GLA_HEADS * GLA_DV
        sizes = [q_rank, kv_rank, MLA_ROPE, gk, gk, gv, GLA_GATE_RANK, gv]
        names = ["zq", "zkv", "zkr", "gq", "gk", "gv", "zg", "zr"]
        starts = np.concatenate([[0], np.cumsum(sizes)[:-1]])
        self.ref = {n: (int(s), int(z)) for n, s, z in zip(names, starts, sizes)}
        self.ref_width = int(sum(sizes))
        self.order = ["gv", "zr", "zq", "gq", "gk", "zkv", "zkr", "zg"]
        self.off, self.size = {}, {}
        pos = 0
        for n in self.order:
            padded = -(-self.ref[n][1] // LANE) * LANE
            self.off[n], self.size[n] = pos, padded
            pos += padded
        self.width = pos

    def pad_weight(self, w):
        return jnp.concatenate([_pad_cols(w[:, self.ref[n][0]:self.ref[n][0] + self.ref[n][1]], self.size[n])
                                for n in self.order], axis=1)

    def unpad_grad(self, g):
        names = sorted(self.ref, key=lambda n: self.ref[n][0])
        return jnp.concatenate([g[:, self.off[n]:self.off[n] + self.ref[n][1]] for n in names], axis=1)


def _pad_q_up(w):
    r = w.shape[0]
    w = w.reshape(r, MLA_HEADS, MLA_QK)
    w = jnp.pad(w, ((0, 0), (0, 0), (0, MLA_HEAD_PAD - MLA_QK)))
    return w.reshape(r, MLA_HEADS * MLA_HEAD_PAD)


def _unpad_q_up(g):
    r = g.shape[0]
    return g.reshape(r, MLA_HEADS, MLA_HEAD_PAD)[:, :, :MLA_QK].reshape(r, MLA_HEADS * MLA_QK)


def _interleave(a, b, heads):
    s = a.shape[0]
    w = a.shape[1] // heads
    return jnp.stack([a.reshape(s, heads, w), b.reshape(s, heads, w)], axis=2).reshape(s, heads * 2 * w)


def _rope_tables(positions):
    half = MLA_ROPE // 2
    inv_freq = ROPE_THETA ** (-jnp.arange(half, dtype=F32) / half)
    ang = positions.astype(F32).reshape(-1, 1) * inv_freq
    cos, sin = jnp.cos(ang), jnp.sin(ang)
    s = ang.shape[0]
    cosf = jnp.concatenate([cos, cos, jnp.ones((s, LANE - MLA_ROPE), F32)], axis=1)
    sinf = jnp.concatenate([sin, sin, jnp.zeros((s, LANE - MLA_ROPE), F32)], axis=1)
    rot = np.zeros((LANE, LANE), np.float32)
    for j in range(half):
        rot[j + half, j] = -1.0
        rot[j, j + half] = 1.0
    return cosf, sinf, jnp.asarray(rot)


SMALL = ["ffn1_norm", "mix_norm", "q_a_norm", "kv_a_norm", "mla_q_norm", "mla_k_norm", "gla_b_gate", "gla_out_norm",
         "mem_attn_norm", "mem_norm", "mem_q_norm", "mem_k_norm", "ffn2_norm"]
BIG = ["ffn1_w_gate", "ffn1_w_up", "ffn1_w_down", "w_in", "w_q_up", "w_kv_up", "w_out", "mem_w_q", "mem_w_k",
       "mem_w_v", "mem_w_o", "ffn2_w_gate", "ffn2_w_up", "ffn2_w_down"]
COL_SHARDED = {"ffn1_w_gate", "ffn1_w_up", "w_in", "w_q_up", "w_kv_up", "gla_w_gate2", "mem_w_o", "ffn2_w_gate", "ffn2_w_up"}
WEIGHTS = ["ffn1_norm", "ffn1_w_gate", "ffn1_w_up", "ffn1_w_down", "mix_norm", "w_in", "q_a_norm", "w_q_up", "kv_a_norm",
           "w_kv_up", "mla_q_norm", "mla_k_norm", "gla_w_gate2", "gla_b_gate", "gla_out_norm", "w_out", "mem_attn_norm",
           "mem_norm", "mem_w_q", "mem_w_k", "mem_w_v", "mem_w_o", "mem_q_norm", "mem_k_norm", "ffn2_norm", "ffn2_w_gate",
           "ffn2_w_up", "ffn2_w_down"]


def _pack_small(vals, rows=8):
    flat = jnp.concatenate([v.reshape(-1).astype(F32) for v in vals])
    n = flat.shape[0]
    per = -(-n // (rows * LANE)) * LANE
    return jnp.pad(flat, (0, rows * per - n)).reshape(rows, per)


def _unpack_small(packed, shapes):
    flat = packed.reshape(-1)
    out, pos = [], 0
    for s in shapes:
        n = int(np.prod(s))
        out.append(flat[pos:pos + n].reshape(s))
        pos += n
    return out


def _ffn_forward(xin, norm_g, wg, wu, wd, tag):
    n = row_fwd(rms_fn, [V(xin)], [norm_g], [(xin.shape[1], BF16)], [(0, 0, 0)], name=f"{tag}_norm")[0]
    gate, up, act = ffn_up(n, wg, wu, name=f"{tag}_up")
    out = mm([(act, wd)], "nn", F32, alpha=0.5, res=xin, name=f"{tag}_down")
    return out, (n, gate, up, act)


def _ffn_backward(dout, xin, norm_g, wg, wu, wd, saved, tag):
    n, gate, up, act = saved
    dgate, dup = ffn_dact(dout, wd, gate, up, 0.5, name=f"{tag}_dact")
    d_wd = mm([(act, dout)], "tn", F32, alpha=0.5, name=f"{tag}_dwd")
    dn = mm([(dgate, wg), (dup, wu)], "nt", F32, name=f"{tag}_dn")
    d_wg = mm([(n, dgate)], "tn", F32, name=f"{tag}_dwg")
    d_wu = mm([(n, dup)], "tn", F32, name=f"{tag}_dwu")
    dx, dg = row_bwd(rms_fn, [V(xin)], [norm_g], [V(dn)], const_diff=[True], res=dout, name=f"{tag}_dnorm")
    return dx, dg, d_wg, d_wu, d_wd


def kernel(x, mem, positions, ffn1_norm, ffn1_w_gate, ffn1_w_up, ffn1_w_down, mix_norm, w_in, q_a_norm, w_q_up, kv_a_norm, w_kv_up, mla_q_norm, mla_k_norm, gla_w_gate2, gla_b_gate, gla_out_norm, w_out, mem_attn_norm, mem_norm, mem_w_q, mem_w_k, mem_w_v, mem_w_o, mem_q_norm, mem_k_norm, ffn2_norm, ffn2_w_gate, ffn2_w_up, ffn2_w_down, loss_target, m_ffn1_norm, m_ffn1_w_gate, m_ffn1_w_up, m_ffn1_w_down, m_mix_norm, m_w_in, m_q_a_norm, m_w_q_up, m_kv_a_norm, m_w_kv_up, m_mla_q_norm, m_mla_k_norm, m_gla_w_gate2, m_gla_b_gate, m_gla_out_norm, m_w_out, m_mem_attn_norm, m_mem_norm, m_mem_w_q, m_mem_w_k, m_mem_w_v, m_mem_w_o, m_mem_q_norm, m_mem_k_norm, m_ffn2_norm, m_ffn2_w_gate, m_ffn2_w_up, m_ffn2_w_down, v_ffn1_norm, v_ffn1_w_gate, v_ffn1_w_up, v_ffn1_w_down, v_mix_norm, v_w_in, v_q_a_norm, v_w_q_up, v_kv_a_norm, v_w_kv_up, v_mla_q_norm, v_mla_k_norm, v_gla_w_gate2, v_gla_b_gate, v_gla_out_norm, v_w_out, v_mem_attn_norm, v_mem_norm, v_mem_w_q, v_mem_w_k, v_mem_w_v, v_mem_w_o, v_mem_q_norm, v_mem_k_norm, v_ffn2_norm, v_ffn2_w_gate, v_ffn2_w_up, v_ffn2_w_down):
    args = dict(locals())
    two_d = lambda a: a[0] if a.ndim == 3 else a
    W = {n: two_d(args[n]) for n in WEIGHTS}
    M1 = {n: two_d(args["m_" + n]) for n in WEIGHTS}
    V2 = {n: two_d(args["v_" + n]) for n in WEIGHTS}
    xs, mems, tgt = x[0], mem[0], loss_target[0]
    S, D = xs.shape
    chip = 2 * lax.axis_index("x") + lax.axis_index("y")

    gathered_names = BIG + ["gla_w_gate2"]
    slots = gather_chips([W[n].astype(BF16) for n in gathered_names], name="gather_weights")
    full = {}
    for n, s in zip(gathered_names, slots):
        full[n] = _slots_to_cols(s) if n in COL_SHARDED else s.reshape(s.shape[0] * s.shape[1], s.shape[2])
    q_rank, kv_rank = full["w_q_up"].shape[0], full["w_kv_up"].shape[0]
    lay = InLayout(q_rank, kv_rank)
    w_in_p = lay.pad_weight(full["w_in"])
    w_q_up_p = _pad_q_up(full["w_q_up"])
    w_gate2_p = jnp.pad(full["gla_w_gate2"], ((0, LANE - GLA_GATE_RANK), (0, 0)))
    off = lay.off
    cosf, sinf, rot = _rope_tables(positions[0])
    tri = jnp.asarray(np.tril(np.ones((CHUNK, CHUNK), np.float32)))
    gqn = W["mla_q_norm"][:, :MLA_NOPE]
    gqr = _pad_cols(W["mla_q_norm"][:, MLA_NOPE:], LANE)
    gkn = W["mla_k_norm"][:, :MLA_NOPE]
    gkr = _pad_cols(W["mla_k_norm"][:, MLA_NOPE:], LANE)
    HP = MLA_HEAD_PAD
    mla_scale = MLA_QK ** -0.5
    mem_scale = MEM_HEAD_DIM ** -0.5
    mla_w = MLA_HEADS * MLA_V
    gla_w = GLA_HEADS * GLA_DV
    mem_w = MEM_HEADS * MEM_HEAD_DIM

    x1, ffn1_saved = _ffn_forward(xs, W["ffn1_norm"], full["ffn1_w_gate"], full["ffn1_w_up"], full["ffn1_w_down"], "ffn1")
    h = row_fwd(rms_fn, [V(x1)], [W["mix_norm"]], [(D, BF16)], [(0, 0, 0)], name="mix_norm")[0]
    z = mm([(h, w_in_p)], "nn", F32, name="in_proj")
    qa = row_fwd(rms_fn, [V(z, off["zq"], q_rank)], [W["q_a_norm"]], [(q_rank, BF16)], [(0, 0, 0)], name="q_a_norm")[0]
    kva = row_fwd(rms_fn, [V(z, off["zkv"], kv_rank)], [W["kv_a_norm"]], [(kv_rank, BF16)], [(0, 0, 0)], name="kv_a_norm")[0]
    qraw = mm([(qa, w_q_up_p)], "nn", F32, name="q_up")
    kvraw = mm([(kva, full["w_kv_up"])], "nn", F32, name="kv_up")
    tabs = [V(cosf, diff=False), V(sinf, diff=False)]
    q_rows = [V(qraw, 0, LANE, HP), V(qraw, LANE, LANE, HP)] + tabs
    k_rows = [V(kvraw, 0, LANE, HP), V(z, off["zkr"], LANE, 0)] + tabs
    qh = row_fwd(qk_prep_fn, q_rows, [gqn, gqr, rot], [(MLA_HEADS * HP, BF16)], [(0, 0, HP), (0, LANE, HP)],
                 heads=MLA_HEADS, name="q_prep")[0]
    kh = row_fwd(qk_prep_fn, k_rows, [gkn, gkr, rot], [(MLA_HEADS * HP, BF16)], [(0, 0, HP), (0, LANE, HP)],
                 heads=MLA_HEADS, name="k_prep")[0]
    mla_kw = dict(heads=MLA_HEADS, dk=HP, dv=MLA_V, v_off=1, v_hs=2, scale=mla_scale, causal=True)
    o_mla = attn_fwd(qh, kh, kvraw, name="mla_attn", **mla_kw)

    zg = z[:, off["zg"]:off["zg"] + LANE]
    pre = mm([(zg, w_gate2_p)], "nn", F32, name="gla_gate")
    la = row_fwd(gate_fn, [V(pre)], [W["gla_b_gate"]], [(pre.shape[1], F32)], [(0, 0, 0)], name="gla_log_decay")[0]
    gla_kw = dict(q_off=off["gq"], k_off=off["gk"], v_off=off["gv"])
    o_raw, states = gla_fwd(z, la, tri, name="gla_scan", **gla_kw)
    gla_rows = [V(o_raw, 0, GLA_DV, GLA_DV), V(z, off["zr"], GLA_DV, GLA_DV)]
    o_gla = row_fwd(gla_out_fn, gla_rows, [W["gla_out_norm"]], [(gla_w, BF16)], [(0, 0, GLA_DV)], heads=GLA_HEADS,
                    name="gla_out")[0]
    o_cat = jnp.concatenate([o_mla, o_gla], axis=1)
    x2 = mm([(o_cat, full["w_out"])], "nn", F32, res=x1, name="out_proj")

    hm = row_fwd(rms_fn, [V(x2)], [W["mem_attn_norm"]], [(D, BF16)], [(0, 0, 0)], name="mem_attn_norm")[0]
    mn = row_fwd(rms_fn, [V(mems)], [W["mem_norm"]], [(D, BF16)], [(0, 0, 0)], name="mem_norm")[0]
    qm_raw = mm([(hm, full["mem_w_q"])], "nn", F32, name="mem_q")
    km_raw = mm([(mn, full["mem_w_k"])], "nn", F32, name="mem_k")
    vm = mm([(mn, full["mem_w_v"])], "nn", F32, name="mem_v")
    hd = MEM_HEAD_DIM
    qm = row_fwd(rms_fn, [V(qm_raw, 0, hd, hd)], [W["mem_q_norm"]], [(mem_w, BF16)], [(0, 0, hd)], heads=MEM_HEADS,
                 name="mem_q_norm")[0]
    km = row_fwd(rms_fn, [V(km_raw, 0, hd, hd)], [W["mem_k_norm"]], [(mem_w, BF16)], [(0, 0, hd)], heads=MEM_HEADS,
                 name="mem_k_norm")[0]
    mem_kw = dict(heads=MEM_HEADS, dk=hd, dv=hd, v_off=0, v_hs=1, scale=mem_scale, causal=False)
    om = attn_fwd(qm, km, vm, name="mem_attn", **mem_kw)
    x3 = mm([(om, full["mem_w_o"])], "nn", F32, res=x2, name="mem_o")

    y, ffn2_saved = _ffn_forward(x3, W["ffn2_norm"], full["ffn2_w_gate"], full["ffn2_w_up"], full["ffn2_w_down"], "ffn2")
    dy, loss_part = loss_head(y, tgt, name="loss_head")
    loss = lax.psum(loss_part[0, 0], ("x", "y", "c"))

    G = {}
    g3, G["ffn2_norm"], G["ffn2_w_gate"], G["ffn2_w_up"], G["ffn2_w_down"] = _ffn_backward(
        dy, x3, W["ffn2_norm"], full["ffn2_w_gate"], full["ffn2_w_up"], full["ffn2_w_down"], ffn2_saved, "ffn2")

    d_om = mm([(g3, full["mem_w_o"])], "nt", F32, name="mem_o_dx")
    G["mem_w_o"] = mm([(om, g3)], "tn", F32, name="mem_o_dw")
    dqm, dkm, dvm = attn_bwd(qm, km, vm, d_om, name="mem_attn_bwd", **mem_kw)
    dqm_raw, G["mem_q_norm"] = row_bwd(rms_fn, [V(qm_raw, 0, hd, hd)], [W["mem_q_norm"]], [V(dqm, 0, hd, hd)],
                                       const_diff=[True], heads=MEM_HEADS, row_dtype=BF16, name="mem_q_norm_bwd")
    dkm_raw, G["mem_k_norm"] = row_bwd(rms_fn, [V(km_raw, 0, hd, hd)], [W["mem_k_norm"]], [V(dkm, 0, hd, hd)],
                                       const_diff=[True], heads=MEM_HEADS, row_dtype=BF16, name="mem_k_norm_bwd")
    dhm = mm([(dqm_raw, full["mem_w_q"])], "nt", F32, name="mem_q_dx")
    G["mem_w_q"] = mm([(hm, dqm_raw)], "tn", F32, name="mem_q_dw")
    dmn = mm([(dkm_raw, full["mem_w_k"]), (dvm, full["mem_w_v"])], "nt", F32, name="mem_kv_dx")
    G["mem_w_k"] = mm([(mn, dkm_raw)], "tn", F32, name="mem_k_dw")
    G["mem_w_v"] = mm([(mn, dvm)], "tn", F32, name="mem_v_dw")
    _, G["mem_norm"] = row_bwd(rms_fn, [V(mems)], [W["mem_norm"]], [V(dmn)], const_diff=[True], row_dtype=BF16,
                               name="mem_norm_bwd")
    g2, G["mem_attn_norm"] = row_bwd(rms_fn, [V(x2)], [W["mem_attn_norm"]], [V(dhm)], const_diff=[True], res=g3,
                                     name="mem_attn_norm_bwd")

    d_ocat = mm([(g2, full["w_out"])], "nt", F32, name="out_proj_dx")
    G["w_out"] = mm([(o_cat, g2)], "tn", F32, name="out_proj_dw")

    d_oraw, d_zr, G["gla_out_norm"] = row_bwd(gla_out_fn, gla_rows, [W["gla_out_norm"]],
                                              [V(d_ocat, mla_w, GLA_DV, GLA_DV)], const_diff=[True], heads=GLA_HEADS,
                                              name="gla_out_bwd")
    d_gq, d_gk, d_gv, d_la = gla_bwd(z, la, tri, tri.T, states, d_oraw, name="gla_scan_bwd", **gla_kw)
    d_pre, G["gla_b_gate"] = row_bwd(gate_fn, [V(pre)], [W["gla_b_gate"]], [V(d_la)], const_diff=[True], row_dtype=BF16,
                                     name="gla_log_decay_bwd")
    d_zg = mm([(d_pre, w_gate2_p)], "nt", BF16, name="gla_gate_dx")
    G["gla_w_gate2"] = mm([(zg, d_pre)], "tn", F32, name="gla_gate_dw")[:GLA_GATE_RANK]

    d_qh, d_kh, d_v = attn_bwd(qh, kh, kvraw, d_ocat, name="mla_attn_bwd", **mla_kw)
    cq = [V(d_qh, 0, LANE, HP), V(d_qh, LANE, LANE, HP)]
    ck = [V(d_kh, 0, LANE, HP), V(d_kh, LANE, LANE, HP)]
    d_qn, d_qr, d_gqn, d_gqr = row_bwd(qk_prep_fn, q_rows, [gqn, gqr, rot], cq, const_diff=[True, True, False],
                                       heads=MLA_HEADS, row_dtype=BF16, name="q_prep_bwd")
    d_kn, d_zkr, d_gkn, d_gkr = row_bwd(qk_prep_fn, k_rows, [gkn, gkr, rot], ck, const_diff=[True, True, False],
                                        heads=MLA_HEADS, row_dtype=BF16, name="k_prep_bwd")
    G["mla_q_norm"] = jnp.concatenate([d_gqn, d_gqr[:, :MLA_ROPE]], axis=1)
    G["mla_k_norm"] = jnp.concatenate([d_gkn, d_gkr[:, :MLA_ROPE]], axis=1)
    d_qraw = _interleave(d_qn, d_qr, MLA_HEADS)
    d_kvraw = _interleave(d_kn, d_v.astype(BF16), MLA_HEADS)
    d_qa = mm([(d_qraw, w_q_up_p)], "nt", F32, name="q_up_dx")
    G["w_q_up"] = _unpad_q_up(mm([(qa, d_qraw)], "tn", F32, name="q_up_dw"))
    d_kva = mm([(d_kvraw, full["w_kv_up"])], "nt", F32, name="kv_up_dx")
    G["w_kv_up"] = mm([(kva, d_kvraw)], "tn", F32, name="kv_up_dw")
    d_zq, G["q_a_norm"] = row_bwd(rms_fn, [V(z, off["zq"], q_rank)], [W["q_a_norm"]], [V(d_qa)], const_diff=[True],
                                  row_dtype=BF16, name="q_a_norm_bwd")
    d_zkv, G["kv_a_norm"] = row_bwd(rms_fn, [V(z, off["zkv"], kv_rank)], [W["kv_a_norm"]], [V(d_kva)], const_diff=[True],
                                    row_dtype=BF16, name="kv_a_norm_bwd")

    seg = {"gv": d_gv, "zr": d_zr, "zq": d_zq, "gq": d_gq, "gk": d_gk, "zkv": d_zkv, "zkr": d_zkr, "zg": d_zg}
    dz = jnp.concatenate([_pad_cols(seg[n].astype(BF16), lay.size[n]) for n in lay.order], axis=1)
    dh = mm([(dz, w_in_p)], "nt", F32, name="in_proj_dx")
    G["w_in"] = lay.unpad_grad(mm([(h, dz)], "tn", F32, name="in_proj_dw"))
    g1, G["mix_norm"] = row_bwd(rms_fn, [V(x1)], [W["mix_norm"]], [V(dh)], const_diff=[True], res=g2, name="mix_norm_bwd")

    gx, G["ffn1_norm"], G["ffn1_w_gate"], G["ffn1_w_up"], G["ffn1_w_down"] = _ffn_backward(
        g1, xs, W["ffn1_norm"], full["ffn1_w_gate"], full["ffn1_w_up"], full["ffn1_w_down"], ffn1_saved, "ffn1")

    big_slots = []
    for n in BIG:
        g = G[n]
        if n in COL_SHARDED:
            big_slots.append(_cols_to_slots(g))
        else:
            big_slots.append(g.reshape(N_CHIPS, g.shape[0] // N_CHIPS, g.shape[1]))
    reduced = dict(zip(BIG, reduce_scatter(big_slots, tag="rs")))
    small_names = SMALL + ["gla_w_gate2"]
    small_sum = allreduce_small(_pack_small([G[n] for n in small_names]), name="allreduce_small")
    small_g = dict(zip(small_names, _unpack_small(small_sum, [G[n].shape for n in small_names])))

    grad, delta, new_m, new_v = {}, {}, {}, {}
    for n in BIG:
        grad[n] = reduced[n]
        delta[n], new_m[n], new_v[n] = adamw(W[n], grad[n], M1[n], V2[n], name=f"adamw_{n}")
    shard_c = W["gla_w_gate2"].shape[1]
    grad["gla_w_gate2"] = lax.dynamic_slice_in_dim(small_g["gla_w_gate2"], chip * shard_c, shard_c, axis=1)
    pw = _pack_small([W[n] for n in SMALL] + [W["gla_w_gate2"]])
    pg = _pack_small([small_g[n] for n in SMALL] + [grad["gla_w_gate2"]])
    pm = _pack_small([M1[n] for n in SMALL] + [M1["gla_w_gate2"]])
    pv = _pack_small([V2[n] for n in SMALL] + [V2["gla_w_gate2"]])
    pd, pnm, pnv = adamw(pw, pg, pm, pv, name="adamw_small")
    shapes = [W[n].shape for n in small_names]
    for n, d_, m_, v_ in zip(small_names, _unpack_small(pd, shapes), _unpack_small(pnm, shapes), _unpack_small(pnv, shapes)):
        delta[n], new_m[n], new_v[n] = d_, m_, v_
        if n != "gla_w_gate2":
            grad[n] = small_g[n]

    lead = lambda d: [d[n].reshape(args[n].shape) for n in WEIGHTS]
    return (loss, gx[None], *lead(grad), *lead(delta), *lead(new_m), *lead(new_v))
```

```python
import functools
import math

import numpy as np
import jax
import jax.numpy as jnp
from jax import lax
from jax.experimental import pallas as pl
from jax.experimental.pallas import tpu as pltpu

F32 = jnp.float32
BF16 = jnp.bfloat16
MXU_DTYPE = jnp.bfloat16
MESH = pl.DeviceIdType.MESH
ANY = pl.BlockSpec(memory_space=pl.ANY)

LANE = 128
EPS = 1e-6
CHUNK = 64
MLA_HEADS = 8
MLA_NOPE = 128
MLA_ROPE = 64
MLA_QK = MLA_NOPE + MLA_ROPE
MLA_V = 128
MLA_HEAD_PAD = 2 * LANE
ROPE_THETA = 10000.0
GLA_HEADS = 4
GLA_DK = 128
GLA_DV = 256
GLA_GATE_RANK = 16
GLA_TAU = 16.0
MEM_HEADS = 4
MEM_HEAD_DIM = 128
N_CHIPS = 4
N_DEV = 8

ADAM_LR = 0.001
ADAM_B1 = 0.9
ADAM_B2 = 0.999
ADAM_EPS = 1e-08
ADAM_WD = 0.01
ADAM_STEP = 10

VMEM_LIMIT = 56 * 1024 * 1024


def _cparams(sem=None):
    if sem is None:
        return pltpu.CompilerParams(vmem_limit_bytes=VMEM_LIMIT)
    return pltpu.CompilerParams(dimension_semantics=sem, vmem_limit_bytes=VMEM_LIMIT)


def _tile(dim, pref, unit=LANE):
    if dim <= pref:
        return dim
    t = (pref // unit) * unit
    while t > unit and dim % t:
        t -= unit
    assert dim % t == 0, (dim, pref, unit)
    return t


_DIMS = {"nn": (((1,), (0,)), ((), ())), "nt": (((1,), (1,)), ((), ())), "tn": (((0,), (0,)), ((), ()))}


def mm(pairs, mode, out_dtype, *, name, alpha=1.0, res=None, tm=1024, tn=512, tk=512):
    a0, b0 = pairs[0]
    if mode == "nn":
        (M, K), N = a0.shape, b0.shape[1]
    elif mode == "nt":
        (M, K), N = a0.shape, b0.shape[0]
    else:
        (K, M), N = a0.shape, b0.shape[1]
    tm, tn, tk = _tile(M, tm), _tile(N, tn), _tile(K, tk)
    nk = K // tk
    npairs = len(pairs)
    dims = _DIMS[mode]
    if mode == "tn":
        a_spec = pl.BlockSpec((tk, tm), lambda i, j, k: (k, i))
    else:
        a_spec = pl.BlockSpec((tm, tk), lambda i, j, k: (i, k))
    if mode == "nt":
        b_spec = pl.BlockSpec((tn, tk), lambda i, j, k: (j, k))
    else:
        b_spec = pl.BlockSpec((tk, tn), lambda i, j, k: (k, j))
    o_spec = pl.BlockSpec((tm, tn), lambda i, j, k: (i, j))
    has_res = res is not None

    def body(*refs):
        ab = refs[:2 * npairs]
        res_ref = refs[2 * npairs] if has_res else None
        o_ref, acc = refs[-2], refs[-1]
        k = pl.program_id(2)

        @pl.when(k == 0)
        def _():
            acc[...] = jnp.zeros_like(acc)

        for p in range(npairs):
            acc[...] += lax.dot_general(ab[2 * p][...].astype(MXU_DTYPE), ab[2 * p + 1][...].astype(MXU_DTYPE),
                                        dims, preferred_element_type=F32)

        @pl.when(k == nk - 1)
        def _():
            r = acc[...]
            if alpha != 1.0:
                r = r * alpha
            if has_res:
                r = res_ref[...].astype(F32) + r
            o_ref[...] = r.astype(out_dtype)

    ops, specs = [], []
    for a, b in pairs:
        ops += [a, b]
        specs += [a_spec, b_spec]
    if has_res:
        ops.append(res)
        specs.append(o_spec)
    return pl.pallas_call(
        body, name=name, grid=(M // tm, N // tn, nk), in_specs=specs, out_specs=o_spec,
        out_shape=jax.ShapeDtypeStruct((M, N), out_dtype), scratch_shapes=[pltpu.VMEM((tm, tn), F32)],
        compiler_params=_cparams(("parallel", "parallel", "arbitrary")))(*ops)


def _sigmoid(x):
    return 1.0 / (1.0 + jnp.exp(-x))


def ffn_up(n, wg, wu, *, name, tm=1024, tn=512, tk=512):
    M, K = n.shape
    N = wg.shape[1]
    tm, tn, tk = _tile(M, tm), _tile(N, tn), _tile(K, tk)
    nk = K // tk

    def body(n_ref, wg_ref, wu_ref, g_ref, u_ref, a_ref, accg, accu):
        k = pl.program_id(2)

        @pl.when(k == 0)
        def _():
            accg[...] = jnp.zeros_like(accg)
            accu[...] = jnp.zeros_like(accu)

        nv = n_ref[...].astype(MXU_DTYPE)
        accg[...] += jnp.dot(nv, wg_ref[...].astype(MXU_DTYPE), preferred_element_type=F32)
        accu[...] += jnp.dot(nv, wu_ref[...].astype(MXU_DTYPE), preferred_element_type=F32)

        @pl.when(k == nk - 1)
        def _():
            g, u = accg[...], accu[...]
            g_ref[...] = g.astype(g_ref.dtype)
            u_ref[...] = u.astype(u_ref.dtype)
            a_ref[...] = (g * _sigmoid(g) * u).astype(a_ref.dtype)

    o_spec = pl.BlockSpec((tm, tn), lambda i, j, k: (i, j))
    sds = jax.ShapeDtypeStruct((M, N), BF16)
    return pl.pallas_call(
        body, name=name, grid=(M // tm, N // tn, nk),
        in_specs=[pl.BlockSpec((tm, tk), lambda i, j, k: (i, k)), pl.BlockSpec((tk, tn), lambda i, j, k: (k, j)),
                  pl.BlockSpec((tk, tn), lambda i, j, k: (k, j))],
        out_specs=[o_spec, o_spec, o_spec], out_shape=[sds, sds, sds],
        scratch_shapes=[pltpu.VMEM((tm, tn), F32), pltpu.VMEM((tm, tn), F32)],
        compiler_params=_cparams(("parallel", "parallel", "arbitrary")))(n, wg, wu)


def ffn_dact(dy, wd, gate, up, alpha, *, name, tm=1024, tn=512, tk=512):
    M, K = dy.shape
    N = wd.shape[0]
    tm, tn, tk = _tile(M, tm), _tile(N, tn), _tile(K, tk)
    nk = K // tk

    def body(dy_ref, wd_ref, g_ref, u_ref, dg_ref, du_ref, acc):
        k = pl.program_id(2)

        @pl.when(k == 0)
        def _():
            acc[...] = jnp.zeros_like(acc)

        acc[...] += lax.dot_general(dy_ref[...].astype(MXU_DTYPE), wd_ref[...].astype(MXU_DTYPE), _DIMS["nt"],
                                    preferred_element_type=F32)

        @pl.when(k == nk - 1)
        def _():
            da = acc[...] * alpha
            g = g_ref[...].astype(F32)
            u = u_ref[...].astype(F32)
            s = _sigmoid(g)
            du_ref[...] = (da * (g * s)).astype(du_ref.dtype)
            dg_ref[...] = (da * u * (s * (1.0 + g * (1.0 - s)))).astype(dg_ref.dtype)

    o_spec = pl.BlockSpec((tm, tn), lambda i, j, k: (i, j))
    sds = jax.ShapeDtypeStruct((M, N), BF16)
    return pl.pallas_call(
        body, name=name, grid=(M // tm, N // tn, nk),
        in_specs=[pl.BlockSpec((tm, tk), lambda i, j, k: (i, k)), pl.BlockSpec((tn, tk), lambda i, j, k: (j, k)),
                  o_spec, o_spec],
        out_specs=[o_spec, o_spec], out_shape=[sds, sds], scratch_shapes=[pltpu.VMEM((tm, tn), F32)],
        compiler_params=_cparams(("parallel", "parallel", "arbitrary")))(dy, wd, gate, up)


class V:
    def __init__(self, arr, off=0, w=None, hs=0, diff=True):
        self.arr, self.off, self.hs, self.diff = arr, off, hs, diff
        self.w = arr.shape[1] - off if w is None else w

    def window(self, heads, tr):
        width = self.arr.shape[1]
        ext = (heads - 1) * self.hs + self.w
        ww = LANE
        while ww < width:
            if ww >= ext and self.off // ww == (self.off + ext - 1) // ww and width % ww == 0:
                break
            ww *= 2
        else:
            ww = width
        blk = self.off // ww
        return pl.BlockSpec((tr, ww), lambda i, blk=blk: (i, blk)), self.off - blk * ww


def _const_spec(c):
    return pl.BlockSpec(c.shape, lambda i: (0, 0))


def row_fwd(fn, rows, consts, outs, out_map, *, heads=1, tr=256, name):
    S = rows[0].arr.shape[0]
    tr = _tile(S, tr, 8)
    wins = [v.window(heads, tr) for v in rows]
    nr, nc = len(rows), len(consts)

    def body(*refs):
        row_refs, const_refs, out_refs = refs[:nr], refs[nr:nr + nc], refs[nr + nc:]
        cv = [c[...].astype(F32) for c in const_refs]
        for h in range(heads):
            rv = []
            for v, (_, io), r in zip(rows, wins, row_refs):
                lo = io + h * v.hs
                rv.append(r[:, lo:lo + v.w].astype(F32))
            res = fn(*rv, *cv)
            for (ai, off, hs), o in zip(out_map, res):
                lo = off + h * hs
                out_refs[ai][:, lo:lo + o.shape[1]] = o.astype(out_refs[ai].dtype)

    return pl.pallas_call(
        body, name=name, grid=(S // tr,),
        in_specs=[w[0] for w in wins] + [_const_spec(c) for c in consts],
        out_specs=[pl.BlockSpec((tr, w), lambda i: (i, 0)) for w, _ in outs],
        out_shape=[jax.ShapeDtypeStruct((S, w), d) for w, d in outs],
        compiler_params=_cparams(("parallel",)))(*[v.arr for v in rows], *consts)


def row_bwd(fn, rows, consts, cots, *, const_diff, heads=1, tr=256, res=None, row_dtype=F32, name):
    S = rows[0].arr.shape[0]
    tr = _tile(S, tr, 8)
    nr, nc, nct = len(rows), len(consts), len(cots)
    wins = [v.window(heads, tr) for v in rows]
    cwins = [v.window(heads, tr) for v in cots]
    drows = [k for k, v in enumerate(rows) if v.diff]
    dconsts = [k for k in range(nc) if const_diff[k]]
    has_res = res is not None
    ngrid = S // tr

    def body(*refs):
        row_refs = refs[:nr]
        const_refs = refs[nr:nr + nc]
        cot_refs = refs[nr + nc:nr + nc + nct]
        p = nr + nc + nct
        res_ref = refs[p] if has_res else None
        p += int(has_res)
        grow_refs = refs[p:p + len(drows)]
        gconst_refs = refs[p + len(drows):]
        i = pl.program_id(0)
        cv = [c[...].astype(F32) for c in const_refs]
        shared = [None] * len(drows)
        gc_sum = [None] * len(dconsts)
        for h in range(heads):
            rv = []
            for v, (_, io), r in zip(rows, wins, row_refs):
                lo = io + h * v.hs
                rv.append(r[:, lo:lo + v.w].astype(F32))
            ct = []
            for v, (_, io), r in zip(cots, cwins, cot_refs):
                lo = io + h * v.hs
                ct.append(r[:, lo:lo + v.w].astype(F32))

            def closed(*d):
                rr, cc = list(rv), list(cv)
                for k, val in zip(drows, d[:len(drows)]):
                    rr[k] = val
                for k, val in zip(dconsts, d[len(drows):]):
                    cc[k] = val
                return tuple(fn(*rr, *cc))

            _, vjp = jax.vjp(closed, *[rv[k] for k in drows], *[cv[k] for k in dconsts])
            grads = vjp(tuple(ct))
            for n, k in enumerate(drows):
                g = grads[n]
                if rows[k].hs == 0 and heads > 1:
                    shared[n] = g if shared[n] is None else shared[n] + g
                else:
                    if n == 0 and has_res:
                        g = g + res_ref[:, h * rows[k].w:(h + 1) * rows[k].w].astype(F32)
                    grow_refs[n][:, h * rows[k].w:(h + 1) * rows[k].w] = g.astype(row_dtype)
            for n in range(len(dconsts)):
                g = grads[len(drows) + n]
                gc_sum[n] = g if gc_sum[n] is None else gc_sum[n] + g
        for n, k in enumerate(drows):
            if shared[n] is not None:
                g = shared[n]
                if n == 0 and has_res:
                    g = g + res_ref[...].astype(F32)
                grow_refs[n][...] = g.astype(row_dtype)

        @pl.when(i == 0)
        def _():
            for n in range(len(dconsts)):
                gconst_refs[n][...] = gc_sum[n]

        @pl.when(i > 0)
        def _():
            for n in range(len(dconsts)):
                gconst_refs[n][...] += gc_sum[n]

    gw = [rows[k].w * (heads if rows[k].hs else 1) for k in drows]
    in_specs = [w[0] for w in wins] + [_const_spec(c) for c in consts] + [w[0] for w in cwins]
    ops = [v.arr for v in rows] + list(consts) + [v.arr for v in cots]
    if has_res:
        in_specs.append(pl.BlockSpec((tr, gw[0]), lambda i: (i, 0)))
        ops.append(res)
    out_specs = [pl.BlockSpec((tr, w), lambda i: (i, 0)) for w in gw]
    out_shape = [jax.ShapeDtypeStruct((S, w), row_dtype) for w in gw]
    for k in dconsts:
        out_specs.append(_const_spec(consts[k]))
        out_shape.append(jax.ShapeDtypeStruct(consts[k].shape, F32))
    del ngrid
    return pl.pallas_call(body, name=name, grid=(S // tr,), in_specs=in_specs, out_specs=out_specs,
                          out_shape=out_shape, compiler_params=_cparams(("arbitrary",)))(*ops)


def _rms(x, g, n=None):
    n = x.shape[-1] if n is None else n
    ms = jnp.sum(x * x, axis=-1, keepdims=True) * (1.0 / n)
    return x * lax.rsqrt(ms + EPS) * g


def rms_fn(x, g):
    return (_rms(x, g),)


def qk_prep_fn(nope, rope, cos, sin, gn, gr, rot):
    ms = (jnp.sum(nope * nope, axis=-1, keepdims=True) + jnp.sum(rope * rope, axis=-1, keepdims=True)) * (1.0 / MLA_QK)
    r = lax.rsqrt(ms + EPS)
    on = nope * r * gn
    orr = rope * r * gr
    turned = jnp.dot(orr, rot, precision=lax.Precision.HIGHEST, preferred_element_type=F32)
    return on, orr * cos + turned * sin


def gla_out_fn(o, zr, g):
    return (_rms(o, g) * (zr * _sigmoid(zr)),)


def gate_fn(pre, b):
    t = pre + b
    return ((jnp.minimum(t, 0.0) - jnp.log(1.0 + jnp.exp(-jnp.abs(t)))) * (1.0 / GLA_TAU),)


def _attn_probs(q_ref, k_ref, scale, causal, tq):
    s = lax.dot_general(q_ref[...].astype(MXU_DTYPE), k_ref[...].astype(MXU_DTYPE), _DIMS["nt"],
                        preferred_element_type=F32) * scale
    if causal:
        i = pl.program_id(1)
        qc = (i * tq + lax.broadcasted_iota(jnp.int32, s.shape, 0)) // CHUNK
        kc = lax.broadcasted_iota(jnp.int32, s.shape, 1) // CHUNK
        s = jnp.where(kc <= qc, s, -1e30)
    m = jnp.max(s, axis=-1, keepdims=True)
    e = jnp.exp(s - m)
    return e / jnp.sum(e, axis=-1, keepdims=True)


def attn_fwd(q, k, v, *, heads, dk, dv, v_off, v_hs, scale, causal, name, tq=256):
    Sq, Sk = q.shape[0], k.shape[0]
    tq = _tile(Sq, tq, 8)

    def body(q_ref, k_ref, v_ref, o_ref):
        p = _attn_probs(q_ref, k_ref, scale, causal, tq)
        o_ref[...] = jnp.dot(p.astype(MXU_DTYPE), v_ref[...].astype(MXU_DTYPE),
                             preferred_element_type=F32).astype(o_ref.dtype)

    return pl.pallas_call(
        body, name=name, grid=(heads, Sq // tq),
        in_specs=[pl.BlockSpec((tq, dk), lambda h, i: (i, h)), pl.BlockSpec((Sk, dk), lambda h, i: (0, h)),
                  pl.BlockSpec((Sk, dv), lambda h, i: (0, v_off + h * v_hs))],
        out_specs=pl.BlockSpec((tq, dv), lambda h, i: (i, h)),
        out_shape=jax.ShapeDtypeStruct((Sq, heads * dv), BF16),
        compiler_params=_cparams(("parallel", "parallel")))(q, k, v)


def attn_bwd(q, k, v, do, *, heads, dk, dv, v_off, v_hs, scale, causal, name, tq=256):
    Sq, Sk = q.shape[0], k.shape[0]
    tq = _tile(Sq, tq, 8)

    def body(q_ref, k_ref, v_ref, do_ref, dq_ref, dk_ref, dv_ref):
        i = pl.program_id(1)
        p = _attn_probs(q_ref, k_ref, scale, causal, tq)
        dob = do_ref[...].astype(MXU_DTYPE)
        dp = lax.dot_general(dob, v_ref[...].astype(MXU_DTYPE), _DIMS["nt"], preferred_element_type=F32)
        delta = jnp.sum(p * dp, axis=-1, keepdims=True)
        ds = (p * (dp - delta) * scale).astype(MXU_DTYPE)
        dq_ref[...] = jnp.dot(ds, k_ref[...].astype(MXU_DTYPE), preferred_element_type=F32)
        dkc = lax.dot_general(ds, q_ref[...].astype(MXU_DTYPE), _DIMS["tn"], preferred_element_type=F32)
        dvc = lax.dot_general(p.astype(MXU_DTYPE), dob, _DIMS["tn"], preferred_element_type=F32)

        @pl.when(i == 0)
        def _():
            dk_ref[...] = dkc
            dv_ref[...] = dvc

        @pl.when(i > 0)
        def _():
            dk_ref[...] += dkc
            dv_ref[...] += dvc

    return pl.pallas_call(
        body, name=name, grid=(heads, Sq // tq),
        in_specs=[pl.BlockSpec((tq, dk), lambda h, i: (i, h)), pl.BlockSpec((Sk, dk), lambda h, i: (0, h)),
                  pl.BlockSpec((Sk, dv), lambda h, i: (0, v_off + h * v_hs)),
                  pl.BlockSpec((tq, dv), lambda h, i: (i, h))],
        out_specs=[pl.BlockSpec((tq, dk), lambda h, i: (i, h)), pl.BlockSpec((Sk, dk), lambda h, i: (0, h)),
                   pl.BlockSpec((Sk, dv), lambda h, i: (0, h))],
        out_shape=[jax.ShapeDtypeStruct((Sq, heads * dk), F32), jax.ShapeDtypeStruct((Sk, heads * dk), F32),
                   jax.ShapeDtypeStruct((Sk, heads * dv), F32)],
        compiler_params=_cparams(("parallel", "arbitrary")))(q, k, v, do)


def _gla_chunk(k_ref, la_ref, tri_ref):
    g = la_ref[...].astype(F32)
    b = jnp.dot(tri_ref[...], g, precision=lax.Precision.HIGHEST, preferred_element_type=F32)
    b_end = jnp.sum(g, axis=0, keepdims=True)
    e = jnp.exp(b_end - b)
    return k_ref[...].astype(F32) * e, e, jnp.exp(b_end)


def gla_fwd(z, la, tri, *, q_off, k_off, v_off, name):
    S = z.shape[0]
    nchunk = S // CHUNK
    H, DK, DV = GLA_HEADS, GLA_DK, GLA_DV
    qb, kb, vb = q_off // DK, k_off // DK, v_off // DV
    qscale = DK ** -0.5

    def body(q_ref, k_ref, v_ref, la_ref, tri_ref, o_ref, st_ref, state):
        c = pl.program_id(1)

        @pl.when(c == 0)
        def _():
            state[...] = jnp.zeros_like(state)

        kdec, _, decay = _gla_chunk(k_ref, la_ref, tri_ref)
        ut = lax.dot_general(v_ref[...].astype(MXU_DTYPE), kdec.astype(MXU_DTYPE), _DIMS["tn"],
                             preferred_element_type=F32)
        new = state[...] * decay + ut
        state[...] = new
        st_ref[...] = new
        qs = (q_ref[...].astype(F32) * qscale).astype(MXU_DTYPE)
        o_ref[...] = lax.dot_general(qs, new.astype(MXU_DTYPE), _DIMS["nt"], preferred_element_type=F32)

    return pl.pallas_call(
        body, name=name, grid=(H, nchunk),
        in_specs=[pl.BlockSpec((CHUNK, DK), lambda h, c: (c, qb + h)), pl.BlockSpec((CHUNK, DK), lambda h, c: (c, kb + h)),
                  pl.BlockSpec((CHUNK, DV), lambda h, c: (c, vb + h)), pl.BlockSpec((CHUNK, DK), lambda h, c: (c, h)),
                  pl.BlockSpec((CHUNK, CHUNK), lambda h, c: (0, 0))],
        out_specs=[pl.BlockSpec((CHUNK, DV), lambda h, c: (c, h)),
                   pl.BlockSpec((None, None, DV, DK), lambda h, c: (h, c, 0, 0))],
        out_shape=[jax.ShapeDtypeStruct((S, H * DV), F32), jax.ShapeDtypeStruct((H, nchunk, DV, DK), F32)],
        scratch_shapes=[pltpu.VMEM((DV, DK), F32)],
        compiler_params=_cparams(("parallel", "arbitrary")))(z, z, z, la, tri)


def gla_bwd(z, la, tri, trit, states, do, *, q_off, k_off, v_off, name):
    S = z.shape[0]
    nchunk = S // CHUNK
    H, DK, DV = GLA_HEADS, GLA_DK, GLA_DV
    qb, kb, vb = q_off // DK, k_off // DK, v_off // DV
    qscale = DK ** -0.5
    last = nchunk - 1

    def body(q_ref, k_ref, v_ref, la_ref, tri_ref, trit_ref, st_ref, sp_ref, do_ref, dq_ref, dk_ref, dv_ref, dla_ref,
             dstate):
        c = pl.program_id(1)
        cc = last - c

        @pl.when(c == 0)
        def _():
            dstate[...] = jnp.zeros_like(dstate)

        kdec, e, decay = _gla_chunk(k_ref, la_ref, tri_ref)
        kf = k_ref[...].astype(F32)
        dob = do_ref[...].astype(MXU_DTYPE)
        stb = st_ref[...].astype(MXU_DTYPE)
        qs = (q_ref[...].astype(F32) * qscale).astype(MXU_DTYPE)
        dq_ref[...] = jnp.dot(dob, stb, preferred_element_type=F32) * qscale
        dst = dstate[...] + lax.dot_general(dob, qs, _DIMS["tn"], preferred_element_type=F32)
        prev = jnp.where(cc > 0, sp_ref[...], 0.0)
        ddecay = jnp.sum(dst * prev, axis=0, keepdims=True)
        dstate[...] = dst * decay
        dub = dst.astype(MXU_DTYPE)
        vb16 = v_ref[...].astype(MXU_DTYPE)
        dv_ref[...] = lax.dot_general(kdec.astype(MXU_DTYPE), dub, _DIMS["nt"], preferred_element_type=F32)
        dkdec = jnp.dot(vb16, dub, preferred_element_type=F32)
        dk_ref[...] = dkdec * e
        w = dkdec * kf * e
        db_end = jnp.sum(w, axis=0, keepdims=True) + ddecay * decay
        dla_ref[...] = db_end - jnp.dot(trit_ref[...], w, precision=lax.Precision.HIGHEST, preferred_element_type=F32)

    rows = lambda blk: (lambda h, c: (last - c, blk + h))
    return pl.pallas_call(
        body, name=name, grid=(H, nchunk),
        in_specs=[pl.BlockSpec((CHUNK, DK), rows(qb)), pl.BlockSpec((CHUNK, DK), rows(kb)),
                  pl.BlockSpec((CHUNK, DV), rows(vb)), pl.BlockSpec((CHUNK, DK), rows(0)),
                  pl.BlockSpec((CHUNK, CHUNK), lambda h, c: (0, 0)), pl.BlockSpec((CHUNK, CHUNK), lambda h, c: (0, 0)),
                  pl.BlockSpec((None, None, DV, DK), lambda h, c: (h, last - c, 0, 0)),
                  pl.BlockSpec((None, None, DV, DK), lambda h, c: (h, jnp.maximum(last - c - 1, 0), 0, 0)),
                  pl.BlockSpec((CHUNK, DV), rows(0))],
        out_specs=[pl.BlockSpec((CHUNK, DK), rows(0)), pl.BlockSpec((CHUNK, DK), rows(0)),
                   pl.BlockSpec((CHUNK, DV), rows(0)), pl.BlockSpec((CHUNK, DK), rows(0))],
        out_shape=[jax.ShapeDtypeStruct((S, H * DK), F32), jax.ShapeDtypeStruct((S, H * DK), F32),
                   jax.ShapeDtypeStruct((S, H * DV), F32), jax.ShapeDtypeStruct((S, H * DK), F32)],
        scratch_shapes=[pltpu.VMEM((DV, DK), F32)],
        compiler_params=_cparams(("parallel", "arbitrary")))(z, z, z, la, tri, trit, states, states, do)


def loss_head(y, target, *, name, tr=256):
    S, D = y.shape
    tr = _tile(S, tr, 8)

    def body(y_ref, t_ref, dy_ref, loss_ref):
        i = pl.program_id(0)
        err = y_ref[...] - t_ref[...]
        dy_ref[...] = err * (1.0 / D)
        part = jnp.zeros((1, LANE), F32) + 0.5 * jnp.sum(jnp.sum(err * err, axis=-1, keepdims=True) * (1.0 / D))

        @pl.when(i == 0)
        def _():
            loss_ref[...] = part

        @pl.when(i > 0)
        def _():
            loss_ref[...] += part

    spec = pl.BlockSpec((tr, D), lambda i: (i, 0))
    return pl.pallas_call(
        body, name=name, grid=(S // tr,), in_specs=[spec, spec],
        out_specs=[spec, pl.BlockSpec((1, LANE), lambda i: (0, 0))],
        out_shape=[jax.ShapeDtypeStruct((S, D), F32), jax.ShapeDtypeStruct((1, LANE), F32)],
        compiler_params=_cparams(("arbitrary",)))(y, target)


def _core_index():
    return lax.axis_index("c").astype(jnp.int32).reshape(1)


def add_slots_into_half(x, *, name, tr=256):
    n, R, C = x.shape
    tr = _tile(R, tr, 8)

    def body(c_ref, x_ref, o_ref):
        acc = x_ref[0].astype(F32)
        for s in range(1, n):
            acc = acc + x_ref[s].astype(F32)
        o_ref[...] = acc

    grid_spec = pltpu.PrefetchScalarGridSpec(
        num_scalar_prefetch=1, grid=(R // tr,), in_specs=[pl.BlockSpec((n, tr, C), lambda i, c: (0, i, 0))],
        out_specs=pl.BlockSpec((None, tr, C), lambda i, c: (c[0], i, 0)))
    return pl.pallas_call(body, name=name, grid_spec=grid_spec, out_shape=jax.ShapeDtypeStruct((2, R, C), F32),
                          compiler_params=_cparams(("parallel",)))(_core_index(), x)


def add_own_half(g, got, out_dtype, *, name, tr=256):
    n, _, R, C = g.shape
    tr = _tile(R, tr, 8)

    def body(c_ref, a_ref, b_ref, o_ref):
        o_ref[...] = (a_ref[...].astype(F32) + b_ref[...].astype(F32)).astype(out_dtype)

    spec = pl.BlockSpec((None, tr, C), lambda s, i, c: (s, i, 0))
    grid_spec = pltpu.PrefetchScalarGridSpec(
        num_scalar_prefetch=1, grid=(n, R // tr),
        in_specs=[pl.BlockSpec((None, None, tr, C), lambda s, i, c: (s, c[0], i, 0)), spec], out_specs=spec)
    return pl.pallas_call(body, name=name, grid_spec=grid_spec, out_shape=jax.ShapeDtypeStruct((n, R, C), out_dtype),
                          compiler_params=_cparams(("parallel", "parallel")))(_core_index(), g, got)


def adamw(w, g, m, v, *, name, tr=256):
    R, C = w.shape
    tr = _tile(R, tr, 8)
    c1 = 1.0 / (1.0 - ADAM_B1 ** ADAM_STEP)
    c2 = 1.0 / (1.0 - ADAM_B2 ** ADAM_STEP)

    def body(w_ref, g_ref, m_ref, v_ref, d_ref, nm_ref, nv_ref):
        gv = g_ref[...]
        nm = ADAM_B1 * m_ref[...] + (1.0 - ADAM_B1) * gv
        nv = ADAM_B2 * v_ref[...] + (1.0 - ADAM_B2) * (gv * gv)
        nm_ref[...] = nm
        nv_ref[...] = nv
        d_ref[...] = -ADAM_LR * ((nm * c1) / (jnp.sqrt(nv * c2) + ADAM_EPS) + ADAM_WD * w_ref[...])

    spec = pl.BlockSpec((tr, C), lambda i: (i, 0))
    sds = jax.ShapeDtypeStruct((R, C), F32)
    return pl.pallas_call(body, name=name, grid=(R // tr,), in_specs=[spec] * 4, out_specs=[spec] * 3,
                          out_shape=[sds] * 3, compiler_params=_cparams(("parallel",)))(w, g, m, v)


def _place():
    x, y, c = lax.axis_index("x"), lax.axis_index("y"), lax.axis_index("c")
    chips = [(1 - x, y), (x, 1 - y), (1 - x, 1 - y)]
    return x, y, c, chips


def gather_chips(shards, *, name):
    n = len(shards)

    def body(*refs):
        ins, outs = refs[:n], refs[n:2 * n]
        send, recv, loc = refs[2 * n:]
        x, y, c, chips = _place()
        mine = 2 * x + y
        local = []
        for i in range(n):
            cp = pltpu.make_async_copy(ins[i], outs[i].at[mine], loc.at[i])
            cp.start()
            local.append(cp)
            for k, (px, py) in enumerate(chips):
                pltpu.make_async_remote_copy(src_ref=ins[i], dst_ref=outs[i].at[mine], send_sem=send.at[i, k],
                                             recv_sem=recv.at[i, k], device_id=(px, py, c),
                                             device_id_type=MESH).start()
        for i in range(n):
            for k, (px, py) in enumerate(chips):
                pltpu.make_async_remote_copy(src_ref=ins[i], dst_ref=outs[i].at[2 * px + py], send_sem=send.at[i, k],
                                             recv_sem=recv.at[i, k], device_id=(px, py, c),
                                             device_id_type=MESH).wait()
        for cp in local:
            cp.wait()

    return pl.pallas_call(
        body, name=name, in_specs=[ANY] * n, out_specs=[ANY] * n,
        out_shape=[jax.ShapeDtypeStruct((N_CHIPS,) + s.shape, s.dtype) for s in shards],
        scratch_shapes=[pltpu.SemaphoreType.DMA((n, 3)), pltpu.SemaphoreType.DMA((n, 3)), pltpu.SemaphoreType.DMA((n,))],
    )(*shards)


def swap_halves(gs, *, name):
    n = len(gs)

    def body(*refs):
        ins, got = refs[:n], refs[n:2 * n]
        send, recv = refs[2 * n:]
        x, y, c, _ = _place()
        sib = (x, y, 1 - c)
        pending = []
        for i in range(n):
            for s in range(N_CHIPS):
                rc = pltpu.make_async_remote_copy(src_ref=ins[i].at[s, 1 - c], dst_ref=got[i].at[s],
                                                  send_sem=send.at[i, s], recv_sem=recv.at[i, s], device_id=sib,
                                                  device_id_type=MESH)
                rc.start()
                pending.append(rc)
        for cp in pending:
            cp.wait()

    return pl.pallas_call(
        body, name=name, in_specs=[ANY] * n, out_specs=[ANY] * n,
        out_shape=[jax.ShapeDtypeStruct((N_CHIPS,) + g.shape[2:], g.dtype) for g in gs],
        scratch_shapes=[pltpu.SemaphoreType.DMA((n, N_CHIPS)), pltpu.SemaphoreType.DMA((n, N_CHIPS))],
    )(*gs)


def exchange_chips(ps, *, name):
    n = len(ps)

    def body(*refs):
        ins, outs = refs[:n], refs[n:2 * n]
        send, recv, loc = refs[2 * n:]
        x, y, c, chips = _place()
        mine = 2 * x + y
        local = []
        for i in range(n):
            cp = pltpu.make_async_copy(ins[i].at[mine], outs[i].at[mine], loc.at[i])
            cp.start()
            local.append(cp)
            for k, (px, py) in enumerate(chips):
                pltpu.make_async_remote_copy(src_ref=ins[i].at[2 * px + py], dst_ref=outs[i].at[mine],
                                             send_sem=send.at[i, k], recv_sem=recv.at[i, k], device_id=(px, py, c),
                                             device_id_type=MESH).start()
        for i in range(n):
            for k, (px, py) in enumerate(chips):
                pltpu.make_async_remote_copy(src_ref=ins[i].at[2 * px + py], dst_ref=outs[i].at[2 * px + py],
                                             send_sem=send.at[i, k], recv_sem=recv.at[i, k], device_id=(px, py, c),
                                             device_id_type=MESH).wait()
        for cp in local:
            cp.wait()

    return pl.pallas_call(
        body, name=name, in_specs=[ANY] * n, out_specs=[ANY] * n,
        out_shape=[jax.ShapeDtypeStruct(p.shape, p.dtype) for p in ps],
        scratch_shapes=[pltpu.SemaphoreType.DMA((n, 3)), pltpu.SemaphoreType.DMA((n, 3)), pltpu.SemaphoreType.DMA((n,))],
    )(*ps)


def join_halves(fs, *, name):
    n = len(fs)

    def body(*refs):
        ins, outs = refs[:n], refs[n:2 * n]
        send, recv = refs[2 * n:]
        x, y, c, _ = _place()
        sib = (x, y, 1 - c)
        for i in range(n):
            pltpu.make_async_remote_copy(src_ref=ins[i].at[c], dst_ref=outs[i].at[c], send_sem=send.at[i],
                                         recv_sem=recv.at[i], device_id=sib, device_id_type=MESH).start()
        for i in range(n):
            pltpu.make_async_remote_copy(src_ref=ins[i].at[c], dst_ref=outs[i].at[1 - c], send_sem=send.at[i],
                                         recv_sem=recv.at[i], device_id=sib, device_id_type=MESH).wait()

    return pl.pallas_call(
        body, name=name, in_specs=[ANY] * n, out_specs=[ANY] * n,
        out_shape=[jax.ShapeDtypeStruct(f.shape, f.dtype) for f in fs],
        input_output_aliases={i: i for i in range(n)},
        scratch_shapes=[pltpu.SemaphoreType.DMA((n,)), pltpu.SemaphoreType.DMA((n,))],
    )(*fs)


def allreduce_small(v, *, name):
    m_per, n = v.shape

    def body(x_ref, sum_ref, all_ref, send_sems, recv_sems, local_sem):
        x, y, c, chips = _place()
        me, sibling = (x, y, c), (x, y, 1 - c)

        def rows(px, py, pc):
            return all_ref.at[pl.ds((4 * px + 2 * py + pc) * m_per, m_per), :]

        def copy(k, block, to, src=None):
            return pltpu.make_async_remote_copy(src_ref=rows(*block) if src is None else src, dst_ref=rows(*block),
                                                send_sem=send_sems.at[k], recv_sem=recv_sems.at[k], device_id=to,
                                                device_id_type=MESH)

        mine = pltpu.make_async_copy(x_ref, rows(*me), local_sem)
        mine.start()
        first = [copy(0, me, sibling, src=x_ref)]
        first += [copy(1 + j, me, (*chip, c), src=x_ref) for j, chip in enumerate(chips)]
        for cp in first:
            cp.start()
        passed = [copy(4 + j, (*chip, c), sibling) for j, chip in enumerate(chips)]
        for j, chip in enumerate(chips):
            copy(1 + j, (*chip, c), me).wait_recv()
            passed[j].start()
        copy(0, sibling, me).wait_recv()
        for j, chip in enumerate(chips):
            copy(4 + j, (*chip, 1 - c), me).wait_recv()
        for cp in first + passed:
            cp.wait_send()
        mine.wait()
        acc = all_ref[0:m_per, :]
        for d in range(1, N_DEV):
            acc = acc + all_ref[d * m_per:(d + 1) * m_per, :]
        sum_ref[...] = acc

    vm = pl.BlockSpec(memory_space=pltpu.VMEM)
    return pl.pallas_call(
        body, name=name, in_specs=[vm], out_specs=vm, out_shape=jax.ShapeDtypeStruct((m_per, n), F32),
        scratch_shapes=[pltpu.VMEM((N_DEV * m_per, n), F32), pltpu.SemaphoreType.DMA((7,)),
                        pltpu.SemaphoreType.DMA((7,)), pltpu.SemaphoreType.DMA],
    )(v)


def reduce_scatter(grads, *, tag):
    halves = [g.reshape(N_CHIPS, 2, g.shape[1] // 2, g.shape[2]) for g in grads]
    got = swap_halves(halves, name=f"{tag}_swap")
    chip_sum = [add_own_half(a, b, BF16, name=f"{tag}_add2_{i}") for i, (a, b) in enumerate(zip(halves, got))]
    parts = exchange_chips(chip_sum, name=f"{tag}_xchg")
    total = [add_slots_into_half(p, name=f"{tag}_add4_{i}") for i, p in enumerate(parts)]
    both = join_halves(total, name=f"{tag}_join")
    return [b.reshape(g.shape[1], g.shape[2]) for b, g in zip(both, grads)]


def _cols_to_slots(w):
    r, c4 = w.shape
    return w.reshape(r, N_CHIPS, c4 // N_CHIPS).transpose(1, 0, 2)


def _slots_to_cols(w):
    n, r, c = w.shape
    return w.transpose(1, 0, 2).reshape(r, n * c)


def _pad_cols(a, width):
    return jnp.pad(a, ((0, 0), (0, width - a.shape[1])))


class InLayout:
    def __init__(self, q_rank, kv_rank):
        gk = GLA_HEADS * GLA_DK
        gv = GLA_HEADS * GLA_DV
        sizes = [q_rank, kv_rank, MLA_ROPE, gk, gk, gv, GLA_GATE_RANK, gv]
        names = ["zq", "zkv", "zkr", "gq", "gk", "gv", "zg", "zr"]
        starts = np.concatenate([[0], np.cumsum(sizes)[:-1]])
        self.ref = {n: (int(s), int(z)) for n, s, z in zip(names, starts, sizes)}
        self.ref_width = int(sum(sizes))
        self.order = ["gv", "zr", "zq", "gq", "gk", "zkv", "zkr", "zg"]
        self.off, self.size = {}, {}
        pos = 0
        for n in self.order:
            padded = -(-self.ref[n][1] // LANE) * LANE
            self.off[n], self.size[n] = pos, padded
            pos += padded
        self.width = pos

    def pad_weight(self, w):
        return jnp.concatenate([_pad_cols(w[:, self.ref[n][0]:self.ref[n][0] + self.ref[n][1]], self.size[n])
                                for n in self.order], axis=1)

    def unpad_grad(self, g):
        names = sorted(self.ref, key=lambda n: self.ref[n][0])
        return jnp.concatenate([g[:, self.off[n]:self.off[n] + self.ref[n][1]] for n in names], axis=1)


def _pad_q_up(w):
    r = w.shape[0]
    w = w.reshape(r, MLA_HEADS, MLA_QK)
    w = jnp.pad(w, ((0, 0), (0, 0), (0, MLA_HEAD_PAD - MLA_QK)))
    return w.reshape(r, MLA_HEADS * MLA_HEAD_PAD)


def _unpad_q_up(g):
    r = g.shape[0]
    return g.reshape(r, MLA_HEADS, MLA_HEAD_PAD)[:, :, :MLA_QK].reshape(r, MLA_HEADS * MLA_QK)


def _interleave(a, b, heads):
    s = a.shape[0]
    w = a.shape[1] // heads
    return jnp.stack([a.reshape(s, heads, w), b.reshape(s, heads, w)], axis=2).reshape(s, heads * 2 * w)


def _rope_tables(positions):
    half = MLA_ROPE // 2
    inv_freq = ROPE_THETA ** (-jnp.arange(half, dtype=F32) / half)
    ang = positions.astype(F32).reshape(-1, 1) * inv_freq
    cos, sin = jnp.cos(ang), jnp.sin(ang)
    s = ang.shape[0]
    cosf = jnp.concatenate([cos, cos, jnp.ones((s, LANE - MLA_ROPE), F32)], axis=1)
    sinf = jnp.concatenate([sin, sin, jnp.zeros((s, LANE - MLA_ROPE), F32)], axis=1)
    rot = np.zeros((LANE, LANE), np.float32)
    for j in range(half):
        rot[j + half, j] = -1.0
        rot[j, j + half] = 1.0
    return cosf, sinf, jnp.asarray(rot)


SMALL = ["ffn1_norm", "mix_norm", "q_a_norm", "kv_a_norm", "mla_q_norm", "mla_k_norm", "gla_b_gate", "gla_out_norm",
         "mem_attn_norm", "mem_norm", "mem_q_norm", "mem_k_norm", "ffn2_norm"]
BIG = ["ffn1_w_gate", "ffn1_w_up", "ffn1_w_down", "w_in", "w_q_up", "w_kv_up", "w_out", "mem_w_q", "mem_w_k",
       "mem_w_v", "mem_w_o", "ffn2_w_gate", "ffn2_w_up", "ffn2_w_down"]
COL_SHARDED = {"ffn1_w_gate", "ffn1_w_up", "w_in", "w_q_up", "w_kv_up", "gla_w_gate2", "mem_w_o", "ffn2_w_gate", "ffn2_w_up"}
WEIGHTS = ["ffn1_norm", "ffn1_w_gate", "ffn1_w_up", "ffn1_w_down", "mix_norm", "w_in", "q_a_norm", "w_q_up", "kv_a_norm",
           "w_kv_up", "mla_q_norm", "mla_k_norm", "gla_w_gate2", "gla_b_gate", "gla_out_norm", "w_out", "mem_attn_norm",
           "mem_norm", "mem_w_q", "mem_w_k", "mem_w_v", "mem_w_o", "mem_q_norm", "mem_k_norm", "ffn2_norm", "ffn2_w_gate",
           "ffn2_w_up", "ffn2_w_down"]


def _pack_small(vals, rows=8):
    flat = jnp.concatenate([v.reshape(-1).astype(F32) for v in vals])
    n = flat.shape[0]
    per = -(-n // (rows * LANE)) * LANE
    return jnp.pad(flat, (0, rows * per - n)).reshape(rows, per)


def _unpack_small(packed, shapes):
    flat = packed.reshape(-1)
    out, pos = [], 0
    for s in shapes:
        n = int(np.prod(s))
        out.append(flat[pos:pos + n].reshape(s))
        pos += n
    return out


def _ffn_forward(xin, norm_g, wg, wu, wd, tag):
    n = row_fwd(rms_fn, [V(xin)], [norm_g], [(xin.shape[1], BF16)], [(0, 0, 0)], name=f"{tag}_norm")[0]
    gate, up, act = ffn_up(n, wg, wu, name=f"{tag}_up")
    out = mm([(act, wd)], "nn", F32, alpha=0.5, res=xin, name=f"{tag}_down")
    return out, (n, gate, up, act)


def _ffn_backward(dout, xin, norm_g, wg, wu, wd, saved, tag):
    n, gate, up, act = saved
    dgate, dup = ffn_dact(dout, wd, gate, up, 0.5, name=f"{tag}_dact")
    d_wd = mm([(act, dout)], "tn", F32, alpha=0.5, name=f"{tag}_dwd")
    dn = mm([(dgate, wg), (dup, wu)], "nt", F32, name=f"{tag}_dn")
    d_wg = mm([(n, dgate)], "tn", F32, name=f"{tag}_dwg")
    d_wu = mm([(n, dup)], "tn", F32, name=f"{tag}_dwu")
    dx, dg = row_bwd(rms_fn, [V(xin)], [norm_g], [V(dn)], const_diff=[True], res=dout, name=f"{tag}_dnorm")
    return dx, dg, d_wg, d_wu, d_wd


def kernel(x, mem, positions, ffn1_norm, ffn1_w_gate, ffn1_w_up, ffn1_w_down, mix_norm, w_in, q_a_norm, w_q_up, kv_a_norm, w_kv_up, mla_q_norm, mla_k_norm, gla_w_gate2, gla_b_gate, gla_out_norm, w_out, mem_attn_norm, mem_norm, mem_w_q, mem_w_k, mem_w_v, mem_w_o, mem_q_norm, mem_k_norm, ffn2_norm, ffn2_w_gate, ffn2_w_up, ffn2_w_down, loss_target, m_ffn1_norm, m_ffn1_w_gate, m_ffn1_w_up, m_ffn1_w_down, m_mix_norm, m_w_in, m_q_a_norm, m_w_q_up, m_kv_a_norm, m_w_kv_up, m_mla_q_norm, m_mla_k_norm, m_gla_w_gate2, m_gla_b_gate, m_gla_out_norm, m_w_out, m_mem_attn_norm, m_mem_norm, m_mem_w_q, m_mem_w_k, m_mem_w_v, m_mem_w_o, m_mem_q_norm, m_mem_k_norm, m_ffn2_norm, m_ffn2_w_gate, m_ffn2_w_up, m_ffn2_w_down, v_ffn1_norm, v_ffn1_w_gate, v_ffn1_w_up, v_ffn1_w_down, v_mix_norm, v_w_in, v_q_a_norm, v_w_q_up, v_kv_a_norm, v_w_kv_up, v_mla_q_norm, v_mla_k_norm, v_gla_w_gate2, v_gla_b_gate, v_gla_out_norm, v_w_out, v_mem_attn_norm, v_mem_norm, v_mem_w_q, v_mem_w_k, v_mem_w_v, v_mem_w_o, v_mem_q_norm, v_mem_k_norm, v_ffn2_norm, v_ffn2_w_gate, v_ffn2_w_up, v_ffn2_w_down):
    args = dict(locals())
    two_d = lambda a: a[0] if a.ndim == 3 else a
    W = {n: two_d(args[n]) for n in WEIGHTS}
    M1 = {n: two_d(args["m_" + n]) for n in WEIGHTS}
    V2 = {n: two_d(args["v_" + n]) for n in WEIGHTS}
    xs, mems, tgt = x[0], mem[0], loss_target[0]
    S, D = xs.shape
    chip = 2 * lax.axis_index("x") + lax.axis_index("y")

    gathered_names = BIG + ["gla_w_gate2"]
    slots = gather_chips([W[n].astype(BF16) for n in gathered_names], name="gather_weights")
    full = {}
    for n, s in zip(gathered_names, slots):
        full[n] = _slots_to_cols(s) if n in COL_SHARDED else s.reshape(s.shape[0] * s.shape[1], s.shape[2])
    q_rank, kv_rank = full["w_q_up"].shape[0], full["w_kv_up"].shape[0]
    lay = InLayout(q_rank, kv_rank)
    w_in_p = lay.pad_weight(full["w_in"])
    w_q_up_p = _pad_q_up(full["w_q_up"])
    w_gate2_p = jnp.pad(full["gla_w_gate2"], ((0, LANE - GLA_GATE_RANK), (0, 0)))
    off = lay.off
    cosf, sinf, rot = _rope_tables(positions[0])
    tri = jnp.asarray(np.tril(np.ones((CHUNK, CHUNK), np.float32)))
    gqn = W["mla_q_norm"][:, :MLA_NOPE]
    gqr = _pad_cols(W["mla_q_norm"][:, MLA_NOPE:], LANE)
    gkn = W["mla_k_norm"][:, :MLA_NOPE]
    gkr = _pad_cols(W["mla_k_norm"][:, MLA_NOPE:], LANE)
    HP = MLA_HEAD_PAD
    mla_scale = MLA_QK ** -0.5
    mem_scale = MEM_HEAD_DIM ** -0.5
    mla_w = MLA_HEADS * MLA_V
    gla_w = GLA_HEADS * GLA_DV
    mem_w = MEM_HEADS * MEM_HEAD_DIM

    x1, ffn1_saved = _ffn_forward(xs, W["ffn1_norm"], full["ffn1_w_gate"], full["ffn1_w_up"], full["ffn1_w_down"], "ffn1")
    h = row_fwd(rms_fn, [V(x1)], [W["mix_norm"]], [(D, BF16)], [(0, 0, 0)], name="mix_norm")[0]
    z = mm([(h, w_in_p)], "nn", F32, name="in_proj")
    qa = row_fwd(rms_fn, [V(z, off["zq"], q_rank)], [W["q_a_norm"]], [(q_rank, BF16)], [(0, 0, 0)], name="q_a_norm")[0]
    kva = row_fwd(rms_fn, [V(z, off["zkv"], kv_rank)], [W["kv_a_norm"]], [(kv_rank, BF16)], [(0, 0, 0)], name="kv_a_norm")[0]
    qraw = mm([(qa, w_q_up_p)], "nn", F32, name="q_up")
    kvraw = mm([(kva, full["w_kv_up"])], "nn", F32, name="kv_up")
    tabs = [V(cosf, diff=False), V(sinf, diff=False)]
    q_rows = [V(qraw, 0, LANE, HP), V(qraw, LANE, LANE, HP)] + tabs
    k_rows = [V(kvraw, 0, LANE, HP), V(z, off["zkr"], LANE, 0)] + tabs
    qh = row_fwd(qk_prep_fn, q_rows, [gqn, gqr, rot], [(MLA_HEADS * HP, BF16)], [(0, 0, HP), (0, LANE, HP)],
                 heads=MLA_HEADS, name="q_prep")[0]
    kh = row_fwd(qk_prep_fn, k_rows, [gkn, gkr, rot], [(MLA_HEADS * HP, BF16)], [(0, 0, HP), (0, LANE, HP)],
                 heads=MLA_HEADS, name="k_prep")[0]
    mla_kw = dict(heads=MLA_HEADS, dk=HP, dv=MLA_V, v_off=1, v_hs=2, scale=mla_scale, causal=True)
    o_mla = attn_fwd(qh, kh, kvraw, name="mla_attn", **mla_kw)

    zg = z[:, off["zg"]:off["zg"] + LANE]
    pre = mm([(zg, w_gate2_p)], "nn", F32, name="gla_gate")
    la = row_fwd(gate_fn, [V(pre)], [W["gla_b_gate"]], [(pre.shape[1], F32)], [(0, 0, 0)], name="gla_log_decay")[0]
    gla_kw = dict(q_off=off["gq"], k_off=off["gk"], v_off=off["gv"])
    o_raw, states = gla_fwd(z, la, tri, name="gla_scan", **gla_kw)
    gla_rows = [V(o_raw, 0, GLA_DV, GLA_DV), V(z, off["zr"], GLA_DV, GLA_DV)]
    o_gla = row_fwd(gla_out_fn, gla_rows, [W["gla_out_norm"]], [(gla_w, BF16)], [(0, 0, GLA_DV)], heads=GLA_HEADS,
                    name="gla_out")[0]
    o_cat = jnp.concatenate([o_mla, o_gla], axis=1)
    x2 = mm([(o_cat, full["w_out"])], "nn", F32, res=x1, name="out_proj")

    hm = row_fwd(rms_fn, [V(x2)], [W["mem_attn_norm"]], [(D, BF16)], [(0, 0, 0)], name="mem_attn_norm")[0]
    mn = row_fwd(rms_fn, [V(mems)], [W["mem_norm"]], [(D, BF16)], [(0, 0, 0)], name="mem_norm")[0]
    qm_raw = mm([(hm, full["mem_w_q"])], "nn", F32, name="mem_q")
    km_raw = mm([(mn, full["mem_w_k"])], "nn", F32, name="mem_k")
    vm = mm([(mn, full["mem_w_v"])], "nn", F32, name="mem_v")
    hd = MEM_HEAD_DIM
    qm = row_fwd(rms_fn, [V(qm_raw, 0, hd, hd)], [W["mem_q_norm"]], [(mem_w, BF16)], [(0, 0, hd)], heads=MEM_HEADS,
                 name="mem_q_norm")[0]
    km = row_fwd(rms_fn, [V(km_raw, 0, hd, hd)], [W["mem_k_norm"]], [(mem_w, BF16)], [(0, 0, hd)], heads=MEM_HEADS,
                 name="mem_k_norm")[0]
    mem_kw = dict(heads=MEM_HEADS, dk=hd, dv=hd, v_off=0, v_hs=1, scale=mem_scale, causal=False)
    om = attn_fwd(qm, km, vm, name="mem_attn", **mem_kw)
    x3 = mm([(om, full["mem_w_o"])], "nn", F32, res=x2, name="mem_o")

    y, ffn2_saved = _ffn_forward(x3, W["ffn2_norm"], full["ffn2_w_gate"], full["ffn2_w_up"], full["ffn2_w_down"], "ffn2")
    dy, loss_part = loss_head(y, tgt, name="loss_head")
    loss = lax.psum(loss_part[0, 0], ("x", "y", "c"))

    G = {}
    g3, G["ffn2_norm"], G["ffn2_w_gate"], G["ffn2_w_up"], G["ffn2_w_down"] = _ffn_backward(
        dy, x3, W["ffn2_norm"], full["ffn2_w_gate"], full["ffn2_w_up"], full["ffn2_w_down"], ffn2_saved, "ffn2")

    d_om = mm([(g3, full["mem_w_o"])], "nt", F32, name="mem_o_dx")
    G["mem_w_o"] = mm([(om, g3)], "tn", F32, name="mem_o_dw")
    dqm, dkm, dvm = attn_bwd(qm, km, vm, d_om, name="mem_attn_bwd", **mem_kw)
    dqm_raw, G["mem_q_norm"] = row_bwd(rms_fn, [V(qm_raw, 0, hd, hd)], [W["mem_q_norm"]], [V(dqm, 0, hd, hd)],
                                       const_diff=[True], heads=MEM_HEADS, row_dtype=BF16, name="mem_q_norm_bwd")
    dkm_raw, G["mem_k_norm"] = row_bwd(rms_fn, [V(km_raw, 0, hd, hd)], [W["mem_k_norm"]], [V(dkm, 0, hd, hd)],
                                       const_diff=[True], heads=MEM_HEADS, row_dtype=BF16, name="mem_k_norm_bwd")
    dhm = mm([(dqm_raw, full["mem_w_q"])], "nt", F32, name="mem_q_dx")
    G["mem_w_q"] = mm([(hm, dqm_raw)], "tn", F32, name="mem_q_dw")
    dmn = mm([(dkm_raw, full["mem_w_k"]), (dvm, full["mem_w_v"])], "nt", F32, name="mem_kv_dx")
    G["mem_w_k"] = mm([(mn, dkm_raw)], "tn", F32, name="mem_k_dw")
    G["mem_w_v"] = mm([(mn, dvm)], "tn", F32, name="mem_v_dw")
    _, G["mem_norm"] = row_bwd(rms_fn, [V(mems)], [W["mem_norm"]], [V(dmn)], const_diff=[True], row_dtype=BF16,
                               name="mem_norm_bwd")
    g2, G["mem_attn_norm"] = row_bwd(rms_fn, [V(x2)], [W["mem_attn_norm"]], [V(dhm)], const_diff=[True], res=g3,
                                     name="mem_attn_norm_bwd")

    d_ocat = mm([(g2, full["w_out"])], "nt", F32, name="out_proj_dx")
    G["w_out"] = mm([(o_cat, g2)], "tn", F32, name="out_proj_dw")

    d_oraw, d_zr, G["gla_out_norm"] = row_bwd(gla_out_fn, gla_rows, [W["gla_out_norm"]],
                                              [V(d_ocat, mla_w, GLA_DV, GLA_DV)], const_diff=[True], heads=GLA_HEADS,
                                              name="gla_out_bwd")
    d_gq, d_gk, d_gv, d_la = gla_bwd(z, la, tri, tri.T, states, d_oraw, name="gla_scan_bwd", **gla_kw)
    d_pre, G["gla_b_gate"] = row_bwd(gate_fn, [V(pre)], [W["gla_b_gate"]], [V(d_la)], const_diff=[True], row_dtype=BF16,
                                     name="gla_log_decay_bwd")
    d_zg = mm([(d_pre, w_gate2_p)], "nt", BF16, name="gla_gate_dx")
    G["gla_w_gate2"] = mm([(zg, d_pre)], "tn", F32, name="gla_gate_dw")[:GLA_GATE_RANK]

    d_qh, d_kh, d_v = attn_bwd(qh, kh, kvraw, d_ocat, name="mla_attn_bwd", **mla_kw)
    cq = [V(d_qh, 0, LANE, HP), V(d_qh, LANE, LANE, HP)]
    ck = [V(d_kh, 0, LANE, HP), V(d_kh, LANE, LANE, HP)]
    d_qn, d_qr, d_gqn, d_gqr = row_bwd(qk_prep_fn, q_rows, [gqn, gqr, rot], cq, const_diff=[True, True, False],
                                       heads=MLA_HEADS, row_dtype=BF16, name="q_prep_bwd")
    d_kn, d_zkr, d_gkn, d_gkr = row_bwd(qk_prep_fn, k_rows, [gkn, gkr, rot], ck, const_diff=[True, True, False],
                                        heads=MLA_HEADS, row_dtype=BF16, name="k_prep_bwd")
    G["mla_q_norm"] = jnp.concatenate([d_gqn, d_gqr[:, :MLA_ROPE]], axis=1)
    G["mla_k_norm"] = jnp.concatenate([d_gkn, d_gkr[:, :MLA_ROPE]], axis=1)
    d_qraw = _interleave(d_qn, d_qr, MLA_HEADS)
    d_kvraw = _interleave(d_kn, d_v.astype(BF16), MLA_HEADS)
    d_qa = mm([(d_qraw, w_q_up_p)], "nt", F32, name="q_up_dx")
    G["w_q_up"] = _unpad_q_up(mm([(qa, d_qraw)], "tn", F32, name="q_up_dw"))
    d_kva = mm([(d_kvraw, full["w_kv_up"])], "nt", F32, name="kv_up_dx")
    G["w_kv_up"] = mm([(kva, d_kvraw)], "tn", F32, name="kv_up_dw")
    d_zq, G["q_a_norm"] = row_bwd(rms_fn, [V(z, off["zq"], q_rank)], [W["q_a_norm"]], [V(d_qa)], const_diff=[True],
                                  row_dtype=BF16, name="q_a_norm_bwd")
    d_zkv, G["kv_a_norm"] = row_bwd(rms_fn, [V(z, off["zkv"], kv_rank)], [W["kv_a_norm"]], [V(d_kva)], const_diff=[True],
                                    row_dtype=BF16, name="kv_a_norm_bwd")

    seg = {"gv": d_gv, "zr": d_zr, "zq": d_zq, "gq": d_gq, "gk": d_gk, "zkv": d_zkv, "zkr": d_zkr, "zg": d_zg}
    dz = jnp.concatenate([_pad_cols(seg[n].astype(BF16), lay.size[n]) for n in lay.order], axis=1)
    dh = mm([(dz, w_in_p)], "nt", F32, name="in_proj_dx")
    G["w_in"] = lay.unpad_grad(mm([(h, dz)], "tn", F32, name="in_proj_dw"))
    g1, G["mix_norm"] = row_bwd(rms_fn, [V(x1)], [W["mix_norm"]], [V(dh)], const_diff=[True], res=g2, name="mix_norm_bwd")

    gx, G["ffn1_norm"], G["ffn1_w_gate"], G["ffn1_w_up"], G["ffn1_w_down"] = _ffn_backward(
        g1, xs, W["ffn1_norm"], full["ffn1_w_gate"], full["ffn1_w_up"], full["ffn1_w_down"], ffn1_saved, "ffn1")

    big_slots = []
    for n in BIG:
        g = G[n]
        if n in COL_SHARDED:
            big_slots.append(_cols_to_slots(g))
        else:
            big_slots.append(g.reshape(N_CHIPS, g.shape[0] // N_CHIPS, g.shape[1]))
    reduced = dict(zip(BIG, reduce_scatter(big_slots, tag="rs")))
    small_names = SMALL + ["gla_w_gate2"]
    small_sum = allreduce_small(_pack_small([G[n] for n in small_names]), name="allreduce_small")
    small_g = dict(zip(small_names, _unpack_small(small_sum, [G[n].shape for n in small_names])))

    grad, delta, new_m, new_v = {}, {}, {}, {}
    for n in BIG:
        grad[n] = reduced[n]
        delta[n], new_m[n], new_v[n] = adamw(W[n], grad[n], M1[n], V2[n], name=f"adamw_{n}")
    shard_c = W["gla_w_gate2"].shape[1]
    grad["gla_w_gate2"] = lax.dynamic_slice_in_dim(small_g["gla_w_gate2"], chip * shard_c, shard_c, axis=1)
    pw = _pack_small([W[n] for n in SMALL] + [W["gla_w_gate2"]])
    pg = _pack_small([small_g[n] for n in SMALL] + [grad["gla_w_gate2"]])
    pm = _pack_small([M1[n] for n in SMALL] + [M1["gla_w_gate2"]])
    pv = _pack_small([V2[n] for n in SMALL] + [V2["gla_w_gate2"]])
    pd, pnm, pnv = adamw(pw, pg, pm, pv, name="adamw_small")
    shapes = [W[n].shape for n in small_names]
    for n, d_, m_, v_ in zip(small_names, _unpack_small(pd, shapes), _unpack_small(pnm, shapes), _unpack_small(pnv, shapes)):
        delta[n], new_m[n], new_v[n] = d_, m_, v_
        if n != "gla_w_gate2":
            grad[n] = small_g[n]

    lead = lambda d: [d[n].reshape(args[n].shape) for n in WEIGHTS]
    return (loss, gx[None], *lead(grad), *lead(delta), *lead(new_m), *lead(new_v))
```

```python
import functools
import math

import numpy as np
import jax
import jax.numpy as jnp
from jax import lax
from jax.experimental import pallas as pl
from jax.experimental.pallas import tpu as pltpu

F32 = jnp.float32
BF16 = jnp.bfloat16
MXU_DTYPE = jnp.bfloat16
MESH = pl.DeviceIdType.MESH
ANY = pl.BlockSpec(memory_space=pl.ANY)

LANE = 128
EPS = 1e-6
CHUNK = 64
MLA_HEADS = 8
MLA_NOPE = 128
MLA_ROPE = 64
MLA_QK = MLA_NOPE + MLA_ROPE
MLA_V = 128
MLA_HEAD_PAD = 2 * LANE
ROPE_THETA = 10000.0
GLA_HEADS = 4
GLA_DK = 128
GLA_DV = 256
GLA_GATE_RANK = 16
GLA_TAU = 16.0
MEM_HEADS = 4
MEM_HEAD_DIM = 128
N_CHIPS = 4
N_DEV = 8

ADAM_LR = 0.001
ADAM_B1 = 0.9
ADAM_B2 = 0.999
ADAM_EPS = 1e-08
ADAM_WD = 0.01
ADAM_STEP = 10

VMEM_LIMIT = 56 * 1024 * 1024


def _cparams(sem=None):
    if sem is None:
        return pltpu.CompilerParams(vmem_limit_bytes=VMEM_LIMIT)
    return pltpu.CompilerParams(dimension_semantics=sem, vmem_limit_bytes=VMEM_LIMIT)


def _tile(dim, pref, unit=LANE):
    if dim <= pref:
        return dim
    t = (pref // unit) * unit
    while t > unit and dim % t:
        t -= unit
    assert dim % t == 0, (dim, pref, unit)
    return t


class Comm:
    def __init__(self, ins, out_shapes, nsem, start, wait, aliases=None):
        self.ins, self.out_shapes, self.nsem = list(ins), list(out_shapes), nsem
        self.start, self.wait, self.aliases = start, wait, dict(aliases or {})


def merge_comms(a, b):
    ai, ao = len(a.ins), len(a.out_shapes)

    def start(ins, outs, send, recv, base):
        a.start(ins[:ai], outs[:ao], send, recv, base)
        b.start(ins[ai:], outs[ao:], send, recv, base + a.nsem)

    def wait(ins, outs, send, recv, base):
        a.wait(ins[:ai], outs[:ao], send, recv, base)
        b.wait(ins[ai:], outs[ao:], send, recv, base + a.nsem)

    aliases = dict(a.aliases)
    aliases.update({ai + i: ao + o for i, o in b.aliases.items()})
    return Comm(a.ins + b.ins, a.out_shapes + b.out_shapes, a.nsem + b.nsem, start, wait, aliases)


def run_comm(comm, *, name):
    ni, no = len(comm.ins), len(comm.out_shapes)

    def body(*refs):
        ins, outs = refs[:ni], refs[ni:ni + no]
        send, recv = refs[ni + no:]
        comm.start(ins, outs, send, recv, 0)
        comm.wait(ins, outs, send, recv, 0)

    return pl.pallas_call(
        body, name=name, in_specs=[ANY] * ni, out_specs=[ANY] * no, out_shape=comm.out_shapes,
        input_output_aliases=comm.aliases,
        scratch_shapes=[pltpu.SemaphoreType.DMA((comm.nsem,)), pltpu.SemaphoreType.DMA((comm.nsem,))])(*comm.ins)


def _pcall(body, ops, *, name, grid, in_specs, out_specs, out_shape, sem, scratch_shapes=(), comm=None):
    if comm is None:
        return pl.pallas_call(body, name=name, grid=grid, in_specs=in_specs, out_specs=out_specs, out_shape=out_shape,
                              scratch_shapes=list(scratch_shapes), compiler_params=_cparams(sem))(*ops)
    multi = isinstance(out_shape, (list, tuple))
    k_out_shape = list(out_shape) if multi else [out_shape]
    k_out_specs = list(out_specs) if multi else [out_specs]
    nki, nko, nks = len(ops), len(k_out_shape), len(scratch_shapes)
    nci, nco = len(comm.ins), len(comm.out_shapes)

    def wrapped(*refs):
        p = 0
        k_in = refs[p:p + nki]; p += nki
        c_in = refs[p:p + nci]; p += nci
        k_out = refs[p:p + nko]; p += nko
        c_out = refs[p:p + nco]; p += nco
        k_scr = refs[p:p + nks]; p += nks
        send, recv = refs[p:]
        first = pl.program_id(0) == 0
        last = pl.program_id(0) == grid[0] - 1
        for a in range(1, len(grid)):
            first = jnp.logical_and(first, pl.program_id(a) == 0)
            last = jnp.logical_and(last, pl.program_id(a) == grid[a] - 1)

        @pl.when(first)
        def _():
            comm.start(c_in, c_out, send, recv, 0)

        body(*k_in, *k_out, *k_scr)

        @pl.when(last)
        def _():
            comm.wait(c_in, c_out, send, recv, 0)

    res = pl.pallas_call(
        wrapped, name=name, grid=grid, in_specs=list(in_specs) + [ANY] * nci, out_specs=k_out_specs + [ANY] * nco,
        out_shape=k_out_shape + comm.out_shapes,
        input_output_aliases={nki + i: nko + o for i, o in comm.aliases.items()},
        scratch_shapes=list(scratch_shapes) + [pltpu.SemaphoreType.DMA((comm.nsem,)), pltpu.SemaphoreType.DMA((comm.nsem,))],
        compiler_params=_cparams(("arbitrary",) * len(grid)))(*ops, *comm.ins)
    k_res = list(res[:nko]) if multi else res[0]
    return k_res, list(res[nko:])


_DIMS = {"nn": (((1,), (0,)), ((), ())), "nt": (((1,), (1,)), ((), ())), "tn": (((0,), (0,)), ((), ()))}


def mm(pairs, mode, out_dtype, *, name, alpha=1.0, res=None, tm=1024, tn=512, tk=512, comm=None):
    a0, b0 = pairs[0]
    if mode == "nn":
        (M, K), N = a0.shape, b0.shape[1]
    elif mode == "nt":
        (M, K), N = a0.shape, b0.shape[0]
    else:
        (K, M), N = a0.shape, b0.shape[1]
    tm, tn, tk = _tile(M, tm), _tile(N, tn), _tile(K, tk)
    nk = K // tk
    npairs = len(pairs)
    dims = _DIMS[mode]
    if mode == "tn":
        a_spec = pl.BlockSpec((tk, tm), lambda i, j, k: (k, i))
    else:
        a_spec = pl.BlockSpec((tm, tk), lambda i, j, k: (i, k))
    if mode == "nt":
        b_spec = pl.BlockSpec((tn, tk), lambda i, j, k: (j, k))
    else:
        b_spec = pl.BlockSpec((tk, tn), lambda i, j, k: (k, j))
    o_spec = pl.BlockSpec((tm, tn), lambda i, j, k: (i, j))
    has_res = res is not None

    def body(*refs):
        ab = refs[:2 * npairs]
        res_ref = refs[2 * npairs] if has_res else None
        o_ref, acc = refs[-2], refs[-1]
        k = pl.program_id(2)

        @pl.when(k == 0)
        def _():
            acc[...] = jnp.zeros_like(acc)

        for p in range(npairs):
            acc[...] += lax.dot_general(ab[2 * p][...].astype(MXU_DTYPE), ab[2 * p + 1][...].astype(MXU_DTYPE),
                                        dims, preferred_element_type=F32)

        @pl.when(k == nk - 1)
        def _():
            r = acc[...]
            if alpha != 1.0:
                r = r * alpha
            if has_res:
                r = res_ref[...].astype(F32) + r
            o_ref[...] = r.astype(out_dtype)

    ops, specs = [], []
    for a, b in pairs:
        ops += [a, b]
        specs += [a_spec, b_spec]
    if has_res:
        ops.append(res)
        specs.append(o_spec)
    return _pcall(body, ops, name=name, grid=(M // tm, N // tn, nk), in_specs=specs, out_specs=o_spec,
                  out_shape=jax.ShapeDtypeStruct((M, N), out_dtype), scratch_shapes=[pltpu.VMEM((tm, tn), F32)],
                  sem=("parallel", "parallel", "arbitrary"), comm=comm)


def _sigmoid(x):
    return 1.0 / (1.0 + jnp.exp(-x))


def ffn_up(n, wg, wu, *, name, tm=1024, tn=512, tk=512, comm=None):
    M, K = n.shape
    N = wg.shape[1]
    tm, tn, tk = _tile(M, tm), _tile(N, tn), _tile(K, tk)
    nk = K // tk

    def body(n_ref, wg_ref, wu_ref, g_ref, u_ref, a_ref, accg, accu):
        k = pl.program_id(2)

        @pl.when(k == 0)
        def _():
            accg[...] = jnp.zeros_like(accg)
            accu[...] = jnp.zeros_like(accu)

        nv = n_ref[...].astype(MXU_DTYPE)
        accg[...] += jnp.dot(nv, wg_ref[...].astype(MXU_DTYPE), preferred_element_type=F32)
        accu[...] += jnp.dot(nv, wu_ref[...].astype(MXU_DTYPE), preferred_element_type=F32)

        @pl.when(k == nk - 1)
        def _():
            g, u = accg[...], accu[...]
            g_ref[...] = g.astype(g_ref.dtype)
            u_ref[...] = u.astype(u_ref.dtype)
            a_ref[...] = (g * _sigmoid(g) * u).astype(a_ref.dtype)

    o_spec = pl.BlockSpec((tm, tn), lambda i, j, k: (i, j))
    sds = jax.ShapeDtypeStruct((M, N), BF16)
    return _pcall(
        body, [n, wg, wu], name=name, grid=(M // tm, N // tn, nk),
        in_specs=[pl.BlockSpec((tm, tk), lambda i, j, k: (i, k)), pl.BlockSpec((tk, tn), lambda i, j, k: (k, j)),
                  pl.BlockSpec((tk, tn), lambda i, j, k: (k, j))],
        out_specs=[o_spec, o_spec, o_spec], out_shape=[sds, sds, sds],
        scratch_shapes=[pltpu.VMEM((tm, tn), F32), pltpu.VMEM((tm, tn), F32)],
        sem=("parallel", "parallel", "arbitrary"), comm=comm)


def ffn_dact(dy, wd, gate, up, alpha, *, name, tm=1024, tn=512, tk=512, comm=None):
    M, K = dy.shape
    N = wd.shape[0]
    tm, tn, tk = _tile(M, tm), _tile(N, tn), _tile(K, tk)
    nk = K // tk

    def body(dy_ref, wd_ref, g_ref, u_ref, dg_ref, du_ref, acc):
        k = pl.program_id(2)

        @pl.when(k == 0)
        def _():
            acc[...] = jnp.zeros_like(acc)

        acc[...] += lax.dot_general(dy_ref[...].astype(MXU_DTYPE), wd_ref[...].astype(MXU_DTYPE), _DIMS["nt"],
                                    preferred_element_type=F32)

        @pl.when(k == nk - 1)
        def _():
            da = acc[...] * alpha
            g = g_ref[...].astype(F32)
            u = u_ref[...].astype(F32)
            s = _sigmoid(g)
            du_ref[...] = (da * (g * s)).astype(du_ref.dtype)
            dg_ref[...] = (da * u * (s * (1.0 + g * (1.0 - s)))).astype(dg_ref.dtype)

    o_spec = pl.BlockSpec((tm, tn), lambda i, j, k: (i, j))
    sds = jax.ShapeDtypeStruct((M, N), BF16)
    return _pcall(
        body, [dy, wd, gate, up], name=name, grid=(M // tm, N // tn, nk),
        in_specs=[pl.BlockSpec((tm, tk), lambda i, j, k: (i, k)), pl.BlockSpec((tn, tk), lambda i, j, k: (j, k)),
                  o_spec, o_spec],
        out_specs=[o_spec, o_spec], out_shape=[sds, sds], scratch_shapes=[pltpu.VMEM((tm, tn), F32)],
        sem=("parallel", "parallel", "arbitrary"), comm=comm)


class V:
    def __init__(self, arr, off=0, w=None, hs=0, diff=True):
        self.arr, self.off, self.hs, self.diff = arr, off, hs, diff
        self.w = arr.shape[1] - off if w is None else w

    def window(self, heads, tr):
        width = self.arr.shape[1]
        ext = (heads - 1) * self.hs + self.w
        ww = LANE
        while ww < width:
            if ww >= ext and self.off // ww == (self.off + ext - 1) // ww and width % ww == 0:
                break
            ww *= 2
        else:
            ww = width
        blk = self.off // ww
        return pl.BlockSpec((tr, ww), lambda i, blk=blk: (i, blk)), self.off - blk * ww


def _const_spec(c):
    return pl.BlockSpec(c.shape, lambda i: (0, 0))


def row_fwd(fn, rows, consts, outs, out_map, *, heads=1, tr=256, name):
    S = rows[0].arr.shape[0]
    tr = _tile(S, tr, 8)
    wins = [v.window(heads, tr) for v in rows]
    nr, nc = len(rows), len(consts)

    def body(*refs):
        row_refs, const_refs, out_refs = refs[:nr], refs[nr:nr + nc], refs[nr + nc:]
        cv = [c[...].astype(F32) for c in const_refs]
        for h in range(heads):
            rv = []
            for v, (_, io), r in zip(rows, wins, row_refs):
                lo = io + h * v.hs
                rv.append(r[:, lo:lo + v.w].astype(F32))
            res = fn(*rv, *cv)
            for (ai, off, hs), o in zip(out_map, res):
                lo = off + h * hs
                out_refs[ai][:, lo:lo + o.shape[1]] = o.astype(out_refs[ai].dtype)

    return pl.pallas_call(
        body, name=name, grid=(S // tr,),
        in_specs=[w[0] for w in wins] + [_const_spec(c) for c in consts],
        out_specs=[pl.BlockSpec((tr, w), lambda i: (i, 0)) for w, _ in outs],
        out_shape=[jax.ShapeDtypeStruct((S, w), d) for w, d in outs],
        compiler_params=_cparams(("parallel",)))(*[v.arr for v in rows], *consts)


def row_bwd(fn, rows, consts, cots, *, const_diff, heads=1, tr=256, res=None, row_dtype=F32, name):
    S = rows[0].arr.shape[0]
    tr = _tile(S, tr, 8)
    nr, nc, nct = len(rows), len(consts), len(cots)
    wins = [v.window(heads, tr) for v in rows]
    cwins = [v.window(heads, tr) for v in cots]
    drows = [k for k, v in enumerate(rows) if v.diff]
    dconsts = [k for k in range(nc) if const_diff[k]]
    has_res = res is not None
    ngrid = S // tr

    def body(*refs):
        row_refs = refs[:nr]
        const_refs = refs[nr:nr + nc]
        cot_refs = refs[nr + nc:nr + nc + nct]
        p = nr + nc + nct
        res_ref = refs[p] if has_res else None
        p += int(has_res)
        grow_refs = refs[p:p + len(drows)]
        gconst_refs = refs[p + len(drows):]
        i = pl.program_id(0)
        cv = [c[...].astype(F32) for c in const_refs]
        shared = [None] * len(drows)
        gc_sum = [None] * len(dconsts)
        for h in range(heads):
            rv = []
            for v, (_, io), r in zip(rows, wins, row_refs):
                lo = io + h * v.hs
                rv.append(r[:, lo:lo + v.w].astype(F32))
            ct = []
            for v, (_, io), r in zip(cots, cwins, cot_refs):
                lo = io + h * v.hs
                ct.append(r[:, lo:lo + v.w].astype(F32))

            def closed(*d):
                rr, cc = list(rv), list(cv)
                for k, val in zip(drows, d[:len(drows)]):
                    rr[k] = val
                for k, val in zip(dconsts, d[len(drows):]):
                    cc[k] = val
                return tuple(fn(*rr, *cc))

            _, vjp = jax.vjp(closed, *[rv[k] for k in drows], *[cv[k] for k in dconsts])
            grads = vjp(tuple(ct))
            for n, k in enumerate(drows):
                g = grads[n]
                if rows[k].hs == 0 and heads > 1:
                    shared[n] = g if shared[n] is None else shared[n] + g
                else:
                    if n == 0 and has_res:
                        g = g + res_ref[:, h * rows[k].w:(h + 1) * rows[k].w].astype(F32)
                    grow_refs[n][:, h * rows[k].w:(h + 1) * rows[k].w] = g.astype(row_dtype)
            for n in range(len(dconsts)):
                g = grads[len(drows) + n]
                gc_sum[n] = g if gc_sum[n] is None else gc_sum[n] + g
        for n, k in enumerate(drows):
            if shared[n] is not None:
                g = shared[n]
                if n == 0 and has_res:
                    g = g + res_ref[...].astype(F32)
                grow_refs[n][...] = g.astype(row_dtype)

        @pl.when(i == 0)
        def _():
            for n in range(len(dconsts)):
                gconst_refs[n][...] = gc_sum[n]

        @pl.when(i > 0)
        def _():
            for n in range(len(dconsts)):
                gconst_refs[n][...] += gc_sum[n]

    gw = [rows[k].w * (heads if rows[k].hs else 1) for k in drows]
    in_specs = [w[0] for w in wins] + [_const_spec(c) for c in consts] + [w[0] for w in cwins]
    ops = [v.arr for v in rows] + list(consts) + [v.arr for v in cots]
    if has_res:
        in_specs.append(pl.BlockSpec((tr, gw[0]), lambda i: (i, 0)))
        ops.append(res)
    out_specs = [pl.BlockSpec((tr, w), lambda i: (i, 0)) for w in gw]
    out_shape = [jax.ShapeDtypeStruct((S, w), row_dtype) for w in gw]
    for k in dconsts:
        out_specs.append(_const_spec(consts[k]))
        out_shape.append(jax.ShapeDtypeStruct(consts[k].shape, F32))
    del ngrid
    return pl.pallas_call(body, name=name, grid=(S // tr,), in_specs=in_specs, out_specs=out_specs,
                          out_shape=out_shape, compiler_params=_cparams(("arbitrary",)))(*ops)


def _rms(x, g, n=None):
    n = x.shape[-1] if n is None else n
    ms = jnp.sum(x * x, axis=-1, keepdims=True) * (1.0 / n)
    return x * lax.rsqrt(ms + EPS) * g


def rms_fn(x, g):
    return (_rms(x, g),)


def qk_prep_fn(nope, rope, cos, sin, gn, gr, rot):
    ms = (jnp.sum(nope * nope, axis=-1, keepdims=True) + jnp.sum(rope * rope, axis=-1, keepdims=True)) * (1.0 / MLA_QK)
    r = lax.rsqrt(ms + EPS)
    on = nope * r * gn
    orr = rope * r * gr
    turned = jnp.dot(orr, rot, precision=lax.Precision.HIGHEST, preferred_element_type=F32)
    return on, orr * cos + turned * sin


def gla_out_fn(o, zr, g):
    return (_rms(o, g) * (zr * _sigmoid(zr)),)


def gate_fn(pre, b):
    t = pre + b
    return ((jnp.minimum(t, 0.0) - jnp.log(1.0 + jnp.exp(-jnp.abs(t)))) * (1.0 / GLA_TAU),)


def _attn_probs(q_ref, k_ref, scale, causal, tq):
    s = lax.dot_general(q_ref[...].astype(MXU_DTYPE), k_ref[...].astype(MXU_DTYPE), _DIMS["nt"],
                        preferred_element_type=F32) * scale
    if causal:
        i = pl.program_id(1)
        qc = (i * tq + lax.broadcasted_iota(jnp.int32, s.shape, 0)) // CHUNK
        kc = lax.broadcasted_iota(jnp.int32, s.shape, 1) // CHUNK
        s = jnp.where(kc <= qc, s, -1e30)
    m = jnp.max(s, axis=-1, keepdims=True)
    e = jnp.exp(s - m)
    return e / jnp.sum(e, axis=-1, keepdims=True)


def attn_fwd(q, k, v, *, heads, dk, dv, v_off, v_hs, scale, causal, name, tq=256, comm=None):
    Sq, Sk = q.shape[0], k.shape[0]
    tq = _tile(Sq, tq, 8)

    def body(q_ref, k_ref, v_ref, o_ref):
        p = _attn_probs(q_ref, k_ref, scale, causal, tq)
        o_ref[...] = jnp.dot(p.astype(MXU_DTYPE), v_ref[...].astype(MXU_DTYPE),
                             preferred_element_type=F32).astype(o_ref.dtype)

    return _pcall(
        body, [q, k, v], name=name, grid=(heads, Sq // tq),
        in_specs=[pl.BlockSpec((tq, dk), lambda h, i: (i, h)), pl.BlockSpec((Sk, dk), lambda h, i: (0, h)),
                  pl.BlockSpec((Sk, dv), lambda h, i: (0, v_off + h * v_hs))],
        out_specs=pl.BlockSpec((tq, dv), lambda h, i: (i, h)),
        out_shape=jax.ShapeDtypeStruct((Sq, heads * dv), BF16), sem=("parallel", "parallel"), comm=comm)


def attn_bwd(q, k, v, do, *, heads, dk, dv, v_off, v_hs, scale, causal, name, tq=256, comm=None):
    Sq, Sk = q.shape[0], k.shape[0]
    tq = _tile(Sq, tq, 8)

    def body(q_ref, k_ref, v_ref, do_ref, dq_ref, dk_ref, dv_ref):
        i = pl.program_id(1)
        p = _attn_probs(q_ref, k_ref, scale, causal, tq)
        dob = do_ref[...].astype(MXU_DTYPE)
        dp = lax.dot_general(dob, v_ref[...].astype(MXU_DTYPE), _DIMS["nt"], preferred_element_type=F32)
        delta = jnp.sum(p * dp, axis=-1, keepdims=True)
        ds = (p * (dp - delta) * scale).astype(MXU_DTYPE)
        dq_ref[...] = jnp.dot(ds, k_ref[...].astype(MXU_DTYPE), preferred_element_type=F32)
        dkc = lax.dot_general(ds, q_ref[...].astype(MXU_DTYPE), _DIMS["tn"], preferred_element_type=F32)
        dvc = lax.dot_general(p.astype(MXU_DTYPE), dob, _DIMS["tn"], preferred_element_type=F32)

        @pl.when(i == 0)
        def _():
            dk_ref[...] = dkc
            dv_ref[...] = dvc

        @pl.when(i > 0)
        def _():
            dk_ref[...] += dkc
            dv_ref[...] += dvc

    return _pcall(
        body, [q, k, v, do], name=name, grid=(heads, Sq // tq),
        in_specs=[pl.BlockSpec((tq, dk), lambda h, i: (i, h)), pl.BlockSpec((Sk, dk), lambda h, i: (0, h)),
                  pl.BlockSpec((Sk, dv), lambda h, i: (0, v_off + h * v_hs)),
                  pl.BlockSpec((tq, dv), lambda h, i: (i, h))],
        out_specs=[pl.BlockSpec((tq, dk), lambda h, i: (i, h)), pl.BlockSpec((Sk, dk), lambda h, i: (0, h)),
                   pl.BlockSpec((Sk, dv), lambda h, i: (0, h))],
        out_shape=[jax.ShapeDtypeStruct((Sq, heads * dk), F32), jax.ShapeDtypeStruct((Sk, heads * dk), F32),
                   jax.ShapeDtypeStruct((Sk, heads * dv), F32)],
        sem=("parallel", "arbitrary"), comm=comm)


def _gla_chunk(k_ref, la_ref, tri_ref):
    g = la_ref[...].astype(F32)
    b = jnp.dot(tri_ref[...], g, precision=lax.Precision.HIGHEST, preferred_element_type=F32)
    b_end = jnp.sum(g, axis=0, keepdims=True)
    e = jnp.exp(b_end - b)
    return k_ref[...].astype(F32) * e, e, jnp.exp(b_end)


def gla_fwd(z, la, tri, *, q_off, k_off, v_off, name, comm=None):
    S = z.shape[0]
    nchunk = S // CHUNK
    H, DK, DV = GLA_HEADS, GLA_DK, GLA_DV
    qb, kb, vb = q_off // DK, k_off // DK, v_off // DV
    qscale = DK ** -0.5

    def body(q_ref, k_ref, v_ref, la_ref, tri_ref, o_ref, st_ref, state):
        c = pl.program_id(1)

        @pl.when(c == 0)
        def _():
            state[...] = jnp.zeros_like(state)

        kdec, _, decay = _gla_chunk(k_ref, la_ref, tri_ref)
        ut = lax.dot_general(v_ref[...].astype(MXU_DTYPE), kdec.astype(MXU_DTYPE), _DIMS["tn"],
                             preferred_element_type=F32)
        new = state[...] * decay + ut
        state[...] = new
        st_ref[...] = new
        qs = (q_ref[...].astype(F32) * qscale).astype(MXU_DTYPE)
        o_ref[...] = lax.dot_general(qs, new.astype(MXU_DTYPE), _DIMS["nt"], preferred_element_type=F32)

    return _pcall(
        body, [z, z, z, la, tri], name=name, grid=(H, nchunk),
        in_specs=[pl.BlockSpec((CHUNK, DK), lambda h, c: (c, qb + h)), pl.BlockSpec((CHUNK, DK), lambda h, c: (c, kb + h)),
                  pl.BlockSpec((CHUNK, DV), lambda h, c: (c, vb + h)), pl.BlockSpec((CHUNK, DK), lambda h, c: (c, h)),
                  pl.BlockSpec((CHUNK, CHUNK), lambda h, c: (0, 0))],
        out_specs=[pl.BlockSpec((CHUNK, DV), lambda h, c: (c, h)),
                   pl.BlockSpec((None, None, DV, DK), lambda h, c: (h, c, 0, 0))],
        out_shape=[jax.ShapeDtypeStruct((S, H * DV), F32), jax.ShapeDtypeStruct((H, nchunk, DV, DK), F32)],
        scratch_shapes=[pltpu.VMEM((DV, DK), F32)], sem=("parallel", "arbitrary"), comm=comm)


def gla_bwd(z, la, tri, trit, states, do, *, q_off, k_off, v_off, name, comm=None):
    S = z.shape[0]
    nchunk = S // CHUNK
    H, DK, DV = GLA_HEADS, GLA_DK, GLA_DV
    qb, kb, vb = q_off // DK, k_off // DK, v_off // DV
    qscale = DK ** -0.5
    last = nchunk - 1

    def body(q_ref, k_ref, v_ref, la_ref, tri_ref, trit_ref, st_ref, sp_ref, do_ref, dq_ref, dk_ref, dv_ref, dla_ref,
             dstate):
        c = pl.program_id(1)
        cc = last - c

        @pl.when(c == 0)
        def _():
            dstate[...] = jnp.zeros_like(dstate)

        kdec, e, decay = _gla_chunk(k_ref, la_ref, tri_ref)
        kf = k_ref[...].astype(F32)
        dob = do_ref[...].astype(MXU_DTYPE)
        stb = st_ref[...].astype(MXU_DTYPE)
        qs = (q_ref[...].astype(F32) * qscale).astype(MXU_DTYPE)
        dq_ref[...] = jnp.dot(dob, stb, preferred_element_type=F32) * qscale
        dst = dstate[...] + lax.dot_general(dob, qs, _DIMS["tn"], preferred_element_type=F32)
        prev = jnp.where(cc > 0, sp_ref[...], 0.0)
        ddecay = jnp.sum(dst * prev, axis=0, keepdims=True)
        dstate[...] = dst * decay
        dub = dst.astype(MXU_DTYPE)
        vb16 = v_ref[...].astype(MXU_DTYPE)
        dv_ref[...] = lax.dot_general(kdec.astype(MXU_DTYPE), dub, _DIMS["nt"], preferred_element_type=F32)
        dkdec = jnp.dot(vb16, dub, preferred_element_type=F32)
        dk_ref[...] = dkdec * e
        w = dkdec * kf * e
        db_end = jnp.sum(w, axis=0, keepdims=True) + ddecay * decay
        dla_ref[...] = db_end - jnp.dot(trit_ref[...], w, precision=lax.Precision.HIGHEST, preferred_element_type=F32)

    rows = lambda blk: (lambda h, c: (last - c, blk + h))
    return _pcall(
        body, [z, z, z, la, tri, trit, states, states, do], name=name, grid=(H, nchunk),
        in_specs=[pl.BlockSpec((CHUNK, DK), rows(qb)), pl.BlockSpec((CHUNK, DK), rows(kb)),
                  pl.BlockSpec((CHUNK, DV), rows(vb)), pl.BlockSpec((CHUNK, DK), rows(0)),
                  pl.BlockSpec((CHUNK, CHUNK), lambda h, c: (0, 0)), pl.BlockSpec((CHUNK, CHUNK), lambda h, c: (0, 0)),
                  pl.BlockSpec((None, None, DV, DK), lambda h, c: (h, last - c, 0, 0)),
                  pl.BlockSpec((None, None, DV, DK), lambda h, c: (h, jnp.maximum(last - c - 1, 0), 0, 0)),
                  pl.BlockSpec((CHUNK, DV), rows(0))],
        out_specs=[pl.BlockSpec((CHUNK, DK), rows(0)), pl.BlockSpec((CHUNK, DK), rows(0)),
                   pl.BlockSpec((CHUNK, DV), rows(0)), pl.BlockSpec((CHUNK, DK), rows(0))],
        out_shape=[jax.ShapeDtypeStruct((S, H * DK), F32), jax.ShapeDtypeStruct((S, H * DK), F32),
                   jax.ShapeDtypeStruct((S, H * DV), F32), jax.ShapeDtypeStruct((S, H * DK), F32)],
        scratch_shapes=[pltpu.VMEM((DV, DK), F32)], sem=("parallel", "arbitrary"), comm=comm)


def loss_head(y, target, *, name, tr=256):
    S, D = y.shape
    tr = _tile(S, tr, 8)

    def body(y_ref, t_ref, dy_ref, loss_ref):
        i = pl.program_id(0)
        err = y_ref[...] - t_ref[...]
        dy_ref[...] = err * (1.0 / D)
        part = jnp.zeros((1, LANE), F32) + 0.5 * jnp.sum(jnp.sum(err * err, axis=-1, keepdims=True) * (1.0 / D))

        @pl.when(i == 0)
        def _():
            loss_ref[...] = part

        @pl.when(i > 0)
        def _():
            loss_ref[...] += part

    spec = pl.BlockSpec((tr, D), lambda i: (i, 0))
    return pl.pallas_call(
        body, name=name, grid=(S // tr,), in_specs=[spec, spec],
        out_specs=[spec, pl.BlockSpec((1, LANE), lambda i: (0, 0))],
        out_shape=[jax.ShapeDtypeStruct((S, D), F32), jax.ShapeDtypeStruct((1, LANE), F32)],
        compiler_params=_cparams(("arbitrary",)))(y, target)


def _core_index():
    return lax.axis_index("c").astype(jnp.int32).reshape(1)


def _chip_slots():
    x, y, c = lax.axis_index("x"), lax.axis_index("y"), lax.axis_index("c")
    return jnp.stack([2 * x + y, 2 * (1 - x) + y, 2 * x + (1 - y), 2 * (1 - x) + (1 - y), c]).astype(jnp.int32)


def sum_chip_parts(own, parts, *, name, tr=256):
    _, R, C = own.shape
    tr = _tile(R, tr, 8)

    def body(idx_ref, o_ref, p0_ref, p1_ref, p2_ref, out_ref):
        acc = o_ref[...].astype(F32) + p0_ref[...].astype(F32)
        acc = acc + p1_ref[...].astype(F32)
        out_ref[...] = acc + p2_ref[...].astype(F32)

    def slot(k):
        return pl.BlockSpec((None, tr, C), lambda i, idx: (idx[k], i, 0))

    grid_spec = pltpu.PrefetchScalarGridSpec(num_scalar_prefetch=1, grid=(R // tr,),
                                             in_specs=[slot(0), slot(1), slot(2), slot(3)], out_specs=slot(4))
    return pl.pallas_call(body, name=name, grid_spec=grid_spec, out_shape=jax.ShapeDtypeStruct((2, R, C), F32),
                          compiler_params=_cparams(("parallel",)))(_chip_slots(), own, parts, parts, parts)


def add_own_half(g, got, out_dtype, *, name, tr=256):
    n, _, R, C = g.shape
    tr = _tile(R, tr, 8)

    def body(c_ref, a_ref, b_ref, o_ref):
        o_ref[...] = (a_ref[...].astype(F32) + b_ref[...].astype(F32)).astype(out_dtype)

    spec = pl.BlockSpec((None, tr, C), lambda s, i, c: (s, i, 0))
    grid_spec = pltpu.PrefetchScalarGridSpec(
        num_scalar_prefetch=1, grid=(n, R // tr),
        in_specs=[pl.BlockSpec((None, None, tr, C), lambda s, i, c: (s, c[0], i, 0)), spec], out_specs=spec)
    return pl.pallas_call(body, name=name, grid_spec=grid_spec, out_shape=jax.ShapeDtypeStruct((n, R, C), out_dtype),
                          compiler_params=_cparams(("parallel", "parallel")))(_core_index(), g, got)


def adamw(items, *, name, max_steps=16, comm=None):
    c1 = 1.0 / (1.0 - ADAM_B1 ** ADAM_STEP)
    c2 = 1.0 / (1.0 - ADAM_B2 ** ADAM_STEP)
    n = len(items)
    steps = max_steps
    while steps > 1 and any(it[0].shape[0] % (8 * steps) for it in items):
        steps //= 2

    def body(*refs):
        for a in range(n):
            w_ref, g_ref, m_ref, v_ref = refs[4 * a:4 * a + 4]
            d_ref, nm_ref, nv_ref = refs[4 * n + 3 * a:4 * n + 3 * a + 3]
            gv = g_ref[...]
            nm = ADAM_B1 * m_ref[...] + (1.0 - ADAM_B1) * gv
            nv = ADAM_B2 * v_ref[...] + (1.0 - ADAM_B2) * (gv * gv)
            nm_ref[...] = nm
            nv_ref[...] = nv
            d_ref[...] = -ADAM_LR * ((nm * c1) / (jnp.sqrt(nv * c2) + ADAM_EPS) + ADAM_WD * w_ref[...])

    ops, in_specs, out_specs, out_shape = [], [], [], []
    for w, g, m, v in items:
        R, C = w.shape
        spec = pl.BlockSpec((R // steps, C), lambda i: (i, 0))
        ops += [w, g, m, v]
        in_specs += [spec] * 4
        out_specs += [spec] * 3
        out_shape += [jax.ShapeDtypeStruct((R, C), F32)] * 3
    res = _pcall(body, ops, name=name, grid=(steps,), in_specs=in_specs, out_specs=out_specs, out_shape=out_shape,
                 sem=("parallel",), comm=comm)
    flat, extra = res if comm is not None else (res, None)
    triples = [tuple(flat[3 * a:3 * a + 3]) for a in range(n)]
    return (triples, extra) if comm is not None else triples


def _place():
    x, y, c = lax.axis_index("x"), lax.axis_index("y"), lax.axis_index("c")
    chips = [(1 - x, y), (x, 1 - y), (1 - x, 1 - y)]
    return x, y, c, chips


def _rcopy(src, dst, send, recv, j, to):
    return pltpu.make_async_remote_copy(src_ref=src, dst_ref=dst, send_sem=send.at[j], recv_sem=recv.at[j], device_id=to,
                                        device_id_type=MESH)


def gather_stage1(shards, split):
    n = len(shards)
    ins = [s.reshape(2, s.shape[0] // 2, s.shape[1]) if sp else s for s, sp in zip(shards, split)]
    outs = [jax.ShapeDtypeStruct((N_CHIPS,) + a.shape, a.dtype) for a in ins]

    def start(in_refs, out_refs, send, recv, base):
        x, y, c, chips = _place()
        mine = 2 * x + y
        for i in range(n):
            src = in_refs[i].at[c] if split[i] else in_refs[i]
            dst = out_refs[i].at[mine, c] if split[i] else out_refs[i].at[mine]
            for k, (px, py) in enumerate(chips):
                _rcopy(src, dst, send, recv, base + 3 * i + k, (px, py, c)).start()

    def wait(in_refs, out_refs, send, recv, base):
        x, y, c, chips = _place()
        for i in range(n):
            src = in_refs[i].at[c] if split[i] else in_refs[i]
            for k, (px, py) in enumerate(chips):
                dst = out_refs[i].at[2 * px + py, c] if split[i] else out_refs[i].at[2 * px + py]
                _rcopy(src, dst, send, recv, base + 3 * i + k, (px, py, c)).wait()

    return Comm(ins, outs, 3 * n, start, wait)


def gather_stage2(slots):
    n = len(slots)

    def copies(in_refs, out_refs, send, recv, base):
        x, y, c, chips = _place()
        for i in range(n):
            for k, (px, py) in enumerate(chips):
                s = 2 * px + py
                yield (_rcopy(in_refs[i].at[s, c], out_refs[i].at[s, c], send, recv, base + 3 * i + k, (x, y, 1 - c)),
                       _rcopy(in_refs[i].at[s, c], out_refs[i].at[s, 1 - c], send, recv, base + 3 * i + k, (x, y, 1 - c)))

    def start(*a):
        for out, _ in copies(*a):
            out.start()

    def wait(*a):
        for _, back in copies(*a):
            back.wait()

    return Comm(slots, [jax.ShapeDtypeStruct(s.shape, s.dtype) for s in slots], 3 * n, start, wait,
                {i: i for i in range(n)})


def swap_halves(gs):
    n = len(gs)

    def copies(in_refs, out_refs, send, recv, base):
        x, y, c, _ = _place()
        return [_rcopy(in_refs[i].at[s, 1 - c], out_refs[i].at[s], send, recv, base + N_CHIPS * i + s, (x, y, 1 - c))
                for i in range(n) for s in range(N_CHIPS)]

    def start(*a):
        for cp in copies(*a):
            cp.start()

    def wait(*a):
        for cp in copies(*a):
            cp.wait()

    return Comm(gs, [jax.ShapeDtypeStruct((N_CHIPS,) + g.shape[2:], g.dtype) for g in gs], N_CHIPS * n, start, wait)


def exchange_chips(ps):
    n = len(ps)

    def start(in_refs, out_refs, send, recv, base):
        x, y, c, chips = _place()
        for i in range(n):
            for k, (px, py) in enumerate(chips):
                _rcopy(in_refs[i].at[2 * px + py], out_refs[i].at[2 * x + y], send, recv, base + 3 * i + k,
                       (px, py, c)).start()

    def wait(in_refs, out_refs, send, recv, base):
        x, y, c, chips = _place()
        for i in range(n):
            for k, (px, py) in enumerate(chips):
                _rcopy(in_refs[i].at[2 * px + py], out_refs[i].at[2 * px + py], send, recv, base + 3 * i + k,
                       (px, py, c)).wait()

    return Comm(ps, [jax.ShapeDtypeStruct(p.shape, p.dtype) for p in ps], 3 * n, start, wait)


def join_halves(fs):
    n = len(fs)

    def start(in_refs, out_refs, send, recv, base):
        x, y, c, _ = _place()
        for i in range(n):
            _rcopy(in_refs[i].at[c], out_refs[i].at[c], send, recv, base + i, (x, y, 1 - c)).start()

    def wait(in_refs, out_refs, send, recv, base):
        x, y, c, _ = _place()
        for i in range(n):
            _rcopy(in_refs[i].at[c], out_refs[i].at[1 - c], send, recv, base + i, (x, y, 1 - c)).wait()

    return Comm(fs, [jax.ShapeDtypeStruct(f.shape, f.dtype) for f in fs], n, start, wait, {i: i for i in range(n)})


def allreduce_small(v, *, name):
    m_per, n = v.shape

    def body(x_ref, sum_ref, all_ref, send_sems, recv_sems, local_sem):
        x, y, c, chips = _place()
        me, sibling = (x, y, c), (x, y, 1 - c)

        def rows(px, py, pc):
            return all_ref.at[pl.ds((4 * px + 2 * py + pc) * m_per, m_per), :]

        def copy(k, block, to, src=None):
            return pltpu.make_async_remote_copy(src_ref=rows(*block) if src is None else src, dst_ref=rows(*block),
                                                send_sem=send_sems.at[k], recv_sem=recv_sems.at[k], device_id=to,
                                                device_id_type=MESH)

        mine = pltpu.make_async_copy(x_ref, rows(*me), local_sem)
        mine.start()
        first = [copy(0, me, sibling, src=x_ref)]
        first += [copy(1 + j, me, (*chip, c), src=x_ref) for j, chip in enumerate(chips)]
        for cp in first:
            cp.start()
        passed = [copy(4 + j, (*chip, c), sibling) for j, chip in enumerate(chips)]
        for j, chip in enumerate(chips):
            copy(1 + j, (*chip, c), me).wait_recv()
            passed[j].start()
        copy(0, sibling, me).wait_recv()
        for j, chip in enumerate(chips):
            copy(4 + j, (*chip, 1 - c), me).wait_recv()
        for cp in first + passed:
            cp.wait_send()
        mine.wait()
        acc = all_ref[0:m_per, :]
        for d in range(1, N_DEV):
            acc = acc + all_ref[d * m_per:(d + 1) * m_per, :]
        sum_ref[...] = acc

    vm = pl.BlockSpec(memory_space=pltpu.VMEM)
    return pl.pallas_call(
        body, name=name, in_specs=[vm], out_specs=vm, out_shape=jax.ShapeDtypeStruct((m_per, n), F32),
        scratch_shapes=[pltpu.VMEM((N_DEV * m_per, n), F32), pltpu.SemaphoreType.DMA((7,)),
                        pltpu.SemaphoreType.DMA((7,)), pltpu.SemaphoreType.DMA],
    )(v)


def _cols_to_slots(w):
    r, c4 = w.shape
    return w.reshape(r, N_CHIPS, c4 // N_CHIPS).transpose(1, 0, 2)


def _slots_to_cols(w):
    n, r, c = w.shape
    return w.transpose(1, 0, 2).reshape(r, n * c)


def _pad_cols(a, width):
    return jnp.pad(a, ((0, 0), (0, width - a.shape[1])))


class InLayout:
    def __init__(self, q_rank, kv_rank):
        gk = GLA_HEADS * GLA_DK
        gv = GLA_HEADS * GLA_DV
        sizes = [q_rank, kv_rank, MLA_ROPE, gk, gk, gv, GLA_GATE_RANK, gv]
        names = ["zq", "zkv", "zkr", "gq", "gk", "gv", "zg", "zr"]
        starts = np.concatenate([[0], np.cumsum(sizes)[:-1]])
        self.ref = {n: (int(s), int(z)) for n, s, z in zip(names, starts, sizes)}
        self.ref_width = int(sum(sizes))
        self.order = ["gv", "zr", "zq", "gq", "gk", "zkv", "zkr", "zg"]
        self.off, self.size = {}, {}
        pos = 0
        for n in self.order:
            padded = -(-self.ref[n][1] // LANE) * LANE
            self.off[n], self.size[n] = pos, padded
            pos += padded
        self.width = pos

    def pad_weight(self, w):
        return jnp.concatenate([_pad_cols(w[:, self.ref[n][0]:self.ref[n][0] + self.ref[n][1]], self.size[n])
                                for n in self.order], axis=1)

    def unpad_grad(self, g):
        names = sorted(self.ref, key=lambda n: self.ref[n][0])
        return jnp.concatenate([g[:, self.off[n]:self.off[n] + self.ref[n][1]] for n in names], axis=1)


def _pad_q_up(w):
    r = w.shape[0]
    w = w.reshape(r, MLA_HEADS, MLA_QK)
    w = jnp.pad(w, ((0, 0), (0, 0), (0, MLA_HEAD_PAD - MLA_QK)))
    return w.reshape(r, MLA_HEADS * MLA_HEAD_PAD)


def _unpad_q_up(g):
    r = g.shape[0]
    return g.reshape(r, MLA_HEADS, MLA_HEAD_PAD)[:, :, :MLA_QK].reshape(r, MLA_HEADS * MLA_QK)


def _interleave(a, b, heads):
    s = a.shape[0]
    w = a.shape[1] // heads
    return jnp.stack([a.reshape(s, heads, w), b.reshape(s, heads, w)], axis=2).reshape(s, heads * 2 * w)


def _rope_tables(positions):
    half = MLA_ROPE // 2
    inv_freq = ROPE_THETA ** (-jnp.arange(half, dtype=F32) / half)
    ang = positions.astype(F32).reshape(-1, 1) * inv_freq
    cos, sin = jnp.cos(ang), jnp.sin(ang)
    s = ang.shape[0]
    cosf = jnp.concatenate([cos, cos, jnp.ones((s, LANE - MLA_ROPE), F32)], axis=1)
    sinf = jnp.concatenate([sin, sin, jnp.zeros((s, LANE - MLA_ROPE), F32)], axis=1)
    rot = np.zeros((LANE, LANE), np.float32)
    for j in range(half):
        rot[j + half, j] = -1.0
        rot[j, j + half] = 1.0
    return cosf, sinf, jnp.asarray(rot)


SMALL = ["ffn1_norm", "mix_norm", "q_a_norm", "kv_a_norm", "mla_q_norm", "mla_k_norm", "gla_b_gate", "gla_out_norm",
         "mem_attn_norm", "mem_norm", "mem_q_norm", "mem_k_norm", "ffn2_norm"]
BIG = ["ffn1_w_gate", "ffn1_w_up", "ffn1_w_down", "w_in", "w_q_up", "w_kv_up", "w_out", "mem_w_q", "mem_w_k",
       "mem_w_v", "mem_w_o", "ffn2_w_gate", "ffn2_w_up", "ffn2_w_down"]
COL_SHARDED = {"ffn1_w_gate", "ffn1_w_up", "w_in", "w_q_up", "w_kv_up", "gla_w_gate2", "mem_w_o", "ffn2_w_gate", "ffn2_w_up"}
WEIGHTS = ["ffn1_norm", "ffn1_w_gate", "ffn1_w_up", "ffn1_w_down", "mix_norm", "w_in", "q_a_norm", "w_q_up", "kv_a_norm",
           "w_kv_up", "mla_q_norm", "mla_k_norm", "gla_w_gate2", "gla_b_gate", "gla_out_norm", "w_out", "mem_attn_norm",
           "mem_norm", "mem_w_q", "mem_w_k", "mem_w_v", "mem_w_o", "mem_q_norm", "mem_k_norm", "ffn2_norm", "ffn2_w_gate",
           "ffn2_w_up", "ffn2_w_down"]


def _pack_small(vals, rows=8):
    flat = jnp.concatenate([v.reshape(-1).astype(F32) for v in vals])
    n = flat.shape[0]
    per = -(-n // (rows * LANE)) * LANE
    return jnp.pad(flat, (0, rows * per - n)).reshape(rows, per)


def _unpack_small(packed, shapes):
    flat = packed.reshape(-1)
    out, pos = [], 0
    for s in shapes:
        n = int(np.prod(s))
        out.append(flat[pos:pos + n].reshape(s))
        pos += n
    return out


FFN1 = ["ffn1_w_gate", "ffn1_w_up", "ffn1_w_down"]
FFN2 = ["ffn2_w_gate", "ffn2_w_up", "ffn2_w_down"]
MID = ["w_in", "w_q_up", "w_kv_up", "w_out", "mem_w_q", "mem_w_k", "mem_w_v", "mem_w_o", "gla_w_gate2"]


def _with(res, comm):
    return res if comm is not None else (res, None)


def kernel(x, mem, positions, ffn1_norm, ffn1_w_gate, ffn1_w_up, ffn1_w_down, mix_norm, w_in, q_a_norm, w_q_up, kv_a_norm, w_kv_up, mla_q_norm, mla_k_norm, gla_w_gate2, gla_b_gate, gla_out_norm, w_out, mem_attn_norm, mem_norm, mem_w_q, mem_w_k, mem_w_v, mem_w_o, mem_q_norm, mem_k_norm, ffn2_norm, ffn2_w_gate, ffn2_w_up, ffn2_w_down, loss_target, m_ffn1_norm, m_ffn1_w_gate, m_ffn1_w_up, m_ffn1_w_down, m_mix_norm, m_w_in, m_q_a_norm, m_w_q_up, m_kv_a_norm, m_w_kv_up, m_mla_q_norm, m_mla_k_norm, m_gla_w_gate2, m_gla_b_gate, m_gla_out_norm, m_w_out, m_mem_attn_norm, m_mem_norm, m_mem_w_q, m_mem_w_k, m_mem_w_v, m_mem_w_o, m_mem_q_norm, m_mem_k_norm, m_ffn2_norm, m_ffn2_w_gate, m_ffn2_w_up, m_ffn2_w_down, v_ffn1_norm, v_ffn1_w_gate, v_ffn1_w_up, v_ffn1_w_down, v_mix_norm, v_w_in, v_q_a_norm, v_w_q_up, v_kv_a_norm, v_w_kv_up, v_mla_q_norm, v_mla_k_norm, v_gla_w_gate2, v_gla_b_gate, v_gla_out_norm, v_w_out, v_mem_attn_norm, v_mem_norm, v_mem_w_q, v_mem_w_k, v_mem_w_v, v_mem_w_o, v_mem_q_norm, v_mem_k_norm, v_ffn2_norm, v_ffn2_w_gate, v_ffn2_w_up, v_ffn2_w_down):
    args = dict(locals())
    two_d = lambda a: a[0] if a.ndim == 3 else a
    W = {n: two_d(args[n]) for n in WEIGHTS}
    M1 = {n: two_d(args["m_" + n]) for n in WEIGHTS}
    V2 = {n: two_d(args["v_" + n]) for n in WEIGHTS}
    xs, mems, tgt = x[0], mem[0], loss_target[0]
    S, D = xs.shape
    chip = 2 * lax.axis_index("x") + lax.axis_index("y")

    shard16 = {n: W[n].astype(BF16) for n in BIG + ["gla_w_gate2"]}
    full = {}

    def stage1(names):
        return gather_stage1([shard16[n] for n in names], [n != "gla_w_gate2" for n in names])

    def stage2(names, slots):
        return gather_stage2([s for n, s in zip(names, slots) if n != "gla_w_gate2"])

    def finish(names, slots1, slots2):
        passed = iter(slots2)
        for n, s1 in zip(names, slots1):
            own = shard16[n]
            s = (s1 if n == "gla_w_gate2" else next(passed)).reshape((N_CHIPS,) + own.shape)
            s = lax.dynamic_update_slice(s, own[None], (chip, 0, 0))
            full[n] = _slots_to_cols(s) if n in COL_SHARDED else s.reshape(-1, own.shape[1])

    ffn1_s1 = run_comm(stage1(FFN1), name="gather_ffn1")
    finish(FFN1, ffn1_s1, run_comm(stage2(FFN1, ffn1_s1), name="pass_ffn1"))
    q_rank, kv_rank = W["w_q_up"].shape[0], W["w_kv_up"].shape[0]
    lay = InLayout(q_rank, kv_rank)
    off = lay.off
    cosf, sinf, rot = _rope_tables(positions[0])
    tri = jnp.asarray(np.tril(np.ones((CHUNK, CHUNK), np.float32)))
    gqn = W["mla_q_norm"][:, :MLA_NOPE]
    gqr = _pad_cols(W["mla_q_norm"][:, MLA_NOPE:], LANE)
    gkn = W["mla_k_norm"][:, :MLA_NOPE]
    gkr = _pad_cols(W["mla_k_norm"][:, MLA_NOPE:], LANE)
    HP = MLA_HEAD_PAD
    mla_scale = MLA_QK ** -0.5
    mem_scale = MEM_HEAD_DIM ** -0.5
    mla_w = MLA_HEADS * MLA_V
    gla_w = GLA_HEADS * GLA_DV
    mem_w = MEM_HEADS * MEM_HEAD_DIM

    n1 = row_fwd(rms_fn, [V(xs)], [W["ffn1_norm"]], [(D, BF16)], [(0, 0, 0)], name="ffn1_norm")[0]
    (gate1, up1, act1), mid_s1 = ffn_up(n1, full["ffn1_w_gate"], full["ffn1_w_up"], name="ffn1_up", comm=stage1(MID))
    x1, mid_s2 = mm([(act1, full["ffn1_w_down"])], "nn", F32, alpha=0.5, res=xs, name="ffn1_down",
                    comm=stage2(MID, mid_s1))
    ffn1_saved = (n1, gate1, up1, act1)
    finish(MID, mid_s1, mid_s2)
    w_in_p = lay.pad_weight(full["w_in"])
    w_q_up_p = _pad_q_up(full["w_q_up"])
    w_gate2_p = jnp.pad(full["gla_w_gate2"], ((0, LANE - GLA_GATE_RANK), (0, 0)))
    h = row_fwd(rms_fn, [V(x1)], [W["mix_norm"]], [(D, BF16)], [(0, 0, 0)], name="mix_norm")[0]
    z, f2_gate = mm([(h, w_in_p)], "nn", F32, name="in_proj", comm=stage1(FFN2[:1]))
    qa = row_fwd(rms_fn, [V(z, off["zq"], q_rank)], [W["q_a_norm"]], [(q_rank, BF16)], [(0, 0, 0)], name="q_a_norm")[0]
    kva = row_fwd(rms_fn, [V(z, off["zkv"], kv_rank)], [W["kv_a_norm"]], [(kv_rank, BF16)], [(0, 0, 0)], name="kv_a_norm")[0]
    qraw = mm([(qa, w_q_up_p)], "nn", F32, name="q_up")
    kvraw = mm([(kva, full["w_kv_up"])], "nn", F32, name="kv_up")
    tabs = [V(cosf, diff=False), V(sinf, diff=False)]
    q_rows = [V(qraw, 0, LANE, HP), V(qraw, LANE, LANE, HP)] + tabs
    k_rows = [V(kvraw, 0, LANE, HP), V(z, off["zkr"], LANE, 0)] + tabs
    qh = row_fwd(qk_prep_fn, q_rows, [gqn, gqr, rot], [(MLA_HEADS * HP, BF16)], [(0, 0, HP), (0, LANE, HP)],
                 heads=MLA_HEADS, name="q_prep")[0]
    kh = row_fwd(qk_prep_fn, k_rows, [gkn, gkr, rot], [(MLA_HEADS * HP, BF16)], [(0, 0, HP), (0, LANE, HP)],
                 heads=MLA_HEADS, name="k_prep")[0]
    mla_kw = dict(heads=MLA_HEADS, dk=HP, dv=MLA_V, v_off=1, v_hs=2, scale=mla_scale, causal=True)
    o_mla, f2_up = attn_fwd(qh, kh, kvraw, name="mla_attn", comm=stage1(FFN2[1:2]), **mla_kw)

    zg = z[:, off["zg"]:off["zg"] + LANE]
    pre = mm([(zg, w_gate2_p)], "nn", F32, name="gla_gate")
    la = row_fwd(gate_fn, [V(pre)], [W["gla_b_gate"]], [(pre.shape[1], F32)], [(0, 0, 0)], name="gla_log_decay")[0]
    gla_kw = dict(q_off=off["gq"], k_off=off["gk"], v_off=off["gv"])
    (o_raw, states), f2_down = gla_fwd(z, la, tri, name="gla_scan", comm=stage1(FFN2[2:]), **gla_kw)
    gla_rows = [V(o_raw, 0, GLA_DV, GLA_DV), V(z, off["zr"], GLA_DV, GLA_DV)]
    o_gla = row_fwd(gla_out_fn, gla_rows, [W["gla_out_norm"]], [(gla_w, BF16)], [(0, 0, GLA_DV)], heads=GLA_HEADS,
                    name="gla_out")[0]
    o_cat = jnp.concatenate([o_mla, o_gla], axis=1)
    ffn2_s1 = f2_gate + f2_up + f2_down
    x2, ffn2_s2 = mm([(o_cat, full["w_out"])], "nn", F32, res=x1, name="out_proj", comm=stage2(FFN2, ffn2_s1))
    finish(FFN2, ffn2_s1, ffn2_s2)

    hm = row_fwd(rms_fn, [V(x2)], [W["mem_attn_norm"]], [(D, BF16)], [(0, 0, 0)], name="mem_attn_norm")[0]
    mn = row_fwd(rms_fn, [V(mems)], [W["mem_norm"]], [(D, BF16)], [(0, 0, 0)], name="mem_norm")[0]
    qm_raw = mm([(hm, full["mem_w_q"])], "nn", F32, name="mem_q")
    km_raw = mm([(mn, full["mem_w_k"])], "nn", F32, name="mem_k")
    vm = mm([(mn, full["mem_w_v"])], "nn", F32, name="mem_v")
    hd = MEM_HEAD_DIM
    qm = row_fwd(rms_fn, [V(qm_raw, 0, hd, hd)], [W["mem_q_norm"]], [(mem_w, BF16)], [(0, 0, hd)], heads=MEM_HEADS,
                 name="mem_q_norm")[0]
    km = row_fwd(rms_fn, [V(km_raw, 0, hd, hd)], [W["mem_k_norm"]], [(mem_w, BF16)], [(0, 0, hd)], heads=MEM_HEADS,
                 name="mem_k_norm")[0]
    mem_kw = dict(heads=MEM_HEADS, dk=hd, dv=hd, v_off=0, v_hs=1, scale=mem_scale, causal=False)
    om = attn_fwd(qm, km, vm, name="mem_attn", **mem_kw)
    x3 = mm([(om, full["mem_w_o"])], "nn", F32, res=x2, name="mem_o")

    n2 = row_fwd(rms_fn, [V(x3)], [W["ffn2_norm"]], [(D, BF16)], [(0, 0, 0)], name="ffn2_norm")[0]
    gate2, up2, act2 = ffn_up(n2, full["ffn2_w_gate"], full["ffn2_w_up"], name="ffn2_up")
    y = mm([(act2, full["ffn2_w_down"])], "nn", F32, alpha=0.5, res=x3, name="ffn2_down")
    dy, loss_part = loss_head(y, tgt, name="loss_head")
    loss = lax.psum(loss_part[0, 0], ("x", "y", "c"))

    G, chip_sum, reduced = {}, {}, {}

    def to_halves(n):
        g = G[n]
        s = _cols_to_slots(g) if n in COL_SHARDED else g.reshape(N_CHIPS, g.shape[0] // N_CHIPS, g.shape[1])
        return s.reshape(N_CHIPS, 2, s.shape[1] // 2, s.shape[2])

    def add2(names, halves, got):
        for n, a, b in zip(names, halves, got):
            chip_sum[n] = add_own_half(a, b, BF16, name=f"rs_add2_{n}")

    def add4_join(names, parts):
        total = [sum_chip_parts(chip_sum[n], p, name=f"rs_add4_{n}") for n, p in zip(names, parts)]
        for n, b in zip(names, run_comm(join_halves(total), name=f"rs_join_{names[0]}")):
            reduced[n] = b.reshape(W[n].shape)

    def ffn_backward(dout, xin, tag, saved, dact_comm=None, after_dact=None, after_dwd=None):
        n_, gate, up, act = saved
        nd, ng, nu = f"{tag}_w_down", f"{tag}_w_gate", f"{tag}_w_up"
        (dgate, dup), got0 = _with(ffn_dact(dout, full[nd], gate, up, 0.5, name=f"{tag}_dact", comm=dact_comm), dact_comm)
        dwd_comm = after_dact(got0) if after_dact else None
        G[nd], got1 = _with(mm([(act, dout)], "tn", F32, alpha=0.5, name=f"{tag}_dwd", comm=dwd_comm), dwd_comm)
        if after_dwd:
            after_dwd(got1)
        hd_ = to_halves(nd)
        G[ng], got_d = mm([(n_, dgate)], "tn", F32, name=f"{tag}_dwg", comm=swap_halves([hd_]))
        add2([nd], [hd_], got_d)
        hg = to_halves(ng)
        G[nu], (parts_d, got_g) = mm([(n_, dup)], "tn", F32, name=f"{tag}_dwu",
                                     comm=merge_comms(exchange_chips([chip_sum[nd]]), swap_halves([hg])))
        add2([ng], [hg], [got_g])
        hu = to_halves(nu)
        dn, (parts_g, got_u) = mm([(dgate, full[ng]), (dup, full[nu])], "nt", F32, name=f"{tag}_dn",
                                  comm=merge_comms(exchange_chips([chip_sum[ng]]), swap_halves([hu])))
        add2([nu], [hu], [got_u])
        dx, G[f"{tag}_norm"] = row_bwd(rms_fn, [V(xin)], [W[f"{tag}_norm"]], [V(dn)], const_diff=[True], res=dout,
                                       name=f"{tag}_dnorm")
        add4_join([nd, ng], [parts_d, parts_g])
        return dx, exchange_chips([chip_sum[nu]])

    g3, ffn2_up_xchg = ffn_backward(dy, x3, "ffn2", (n2, gate2, up2, act2))

    d_om = mm([(g3, full["mem_w_o"])], "nt", F32, name="mem_o_dx")
    G["mem_w_o"] = mm([(om, g3)], "tn", F32, name="mem_o_dw")
    dqm, dkm, dvm = attn_bwd(qm, km, vm, d_om, name="mem_attn_bwd", **mem_kw)
    dqm_raw, G["mem_q_norm"] = row_bwd(rms_fn, [V(qm_raw, 0, hd, hd)], [W["mem_q_norm"]], [V(dqm, 0, hd, hd)],
                                       const_diff=[True], heads=MEM_HEADS, row_dtype=BF16, name="mem_q_norm_bwd")
    dkm_raw, G["mem_k_norm"] = row_bwd(rms_fn, [V(km_raw, 0, hd, hd)], [W["mem_k_norm"]], [V(dkm, 0, hd, hd)],
                                       const_diff=[True], heads=MEM_HEADS, row_dtype=BF16, name="mem_k_norm_bwd")
    dhm = mm([(dqm_raw, full["mem_w_q"])], "nt", F32, name="mem_q_dx")
    G["mem_w_q"] = mm([(hm, dqm_raw)], "tn", F32, name="mem_q_dw")
    dmn = mm([(dkm_raw, full["mem_w_k"]), (dvm, full["mem_w_v"])], "nt", F32, name="mem_kv_dx")
    G["mem_w_k"] = mm([(mn, dkm_raw)], "tn", F32, name="mem_k_dw")
    G["mem_w_v"] = mm([(mn, dvm)], "tn", F32, name="mem_v_dw")
    _, G["mem_norm"] = row_bwd(rms_fn, [V(mems)], [W["mem_norm"]], [V(dmn)], const_diff=[True], row_dtype=BF16,
                               name="mem_norm_bwd")
    g2, G["mem_attn_norm"] = row_bwd(rms_fn, [V(x2)], [W["mem_attn_norm"]], [V(dhm)], const_diff=[True], res=g3,
                                     name="mem_attn_norm_bwd")

    d_ocat = mm([(g2, full["w_out"])], "nt", F32, name="out_proj_dx")
    G["w_out"] = mm([(o_cat, g2)], "tn", F32, name="out_proj_dw")

    d_oraw, d_zr, G["gla_out_norm"] = row_bwd(gla_out_fn, gla_rows, [W["gla_out_norm"]],
                                              [V(d_ocat, mla_w, GLA_DV, GLA_DV)], const_diff=[True], heads=GLA_HEADS,
                                              name="gla_out_bwd")
    d_gq, d_gk, d_gv, d_la = gla_bwd(z, la, tri, tri.T, states, d_oraw, name="gla_scan_bwd", **gla_kw)
    d_pre, G["gla_b_gate"] = row_bwd(gate_fn, [V(pre)], [W["gla_b_gate"]], [V(d_la)], const_diff=[True], row_dtype=BF16,
                                     name="gla_log_decay_bwd")
    d_zg = mm([(d_pre, w_gate2_p)], "nt", BF16, name="gla_gate_dx")
    G["gla_w_gate2"] = mm([(zg, d_pre)], "tn", F32, name="gla_gate_dw")[:GLA_GATE_RANK]

    (d_qh, d_kh, d_v), ffn2_up_parts = attn_bwd(qh, kh, kvraw, d_ocat, name="mla_attn_bwd", comm=ffn2_up_xchg, **mla_kw)
    add4_join(["ffn2_w_up"], ffn2_up_parts)
    cq = [V(d_qh, 0, LANE, HP), V(d_qh, LANE, LANE, HP)]
    ck = [V(d_kh, 0, LANE, HP), V(d_kh, LANE, LANE, HP)]
    d_qn, d_qr, d_gqn, d_gqr = row_bwd(qk_prep_fn, q_rows, [gqn, gqr, rot], cq, const_diff=[True, True, False],
                                       heads=MLA_HEADS, row_dtype=BF16, name="q_prep_bwd")
    d_kn, d_zkr, d_gkn, d_gkr = row_bwd(qk_prep_fn, k_rows, [gkn, gkr, rot], ck, const_diff=[True, True, False],
                                        heads=MLA_HEADS, row_dtype=BF16, name="k_prep_bwd")
    G["mla_q_norm"] = jnp.concatenate([d_gqn, d_gqr[:, :MLA_ROPE]], axis=1)
    G["mla_k_norm"] = jnp.concatenate([d_gkn, d_gkr[:, :MLA_ROPE]], axis=1)
    d_qraw = _interleave(d_qn, d_qr, MLA_HEADS)
    d_kvraw = _interleave(d_kn, d_v.astype(BF16), MLA_HEADS)
    d_qa = mm([(d_qraw, w_q_up_p)], "nt", F32, name="q_up_dx")
    G["w_q_up"] = _unpad_q_up(mm([(qa, d_qraw)], "tn", F32, name="q_up_dw"))
    d_kva = mm([(d_kvraw, full["w_kv_up"])], "nt", F32, name="kv_up_dx")
    G["w_kv_up"] = mm([(kva, d_kvraw)], "tn", F32, name="kv_up_dw")
    d_zq, G["q_a_norm"] = row_bwd(rms_fn, [V(z, off["zq"], q_rank)], [W["q_a_norm"]], [V(d_qa)], const_diff=[True],
                                  row_dtype=BF16, name="q_a_norm_bwd")
    d_zkv, G["kv_a_norm"] = row_bwd(rms_fn, [V(z, off["zkv"], kv_rank)], [W["kv_a_norm"]], [V(d_kva)], const_diff=[True],
                                    row_dtype=BF16, name="kv_a_norm_bwd")

    seg = {"gv": d_gv, "zr": d_zr, "zq": d_zq, "gq": d_gq, "gk": d_gk, "zkv": d_zkv, "zkr": d_zkr, "zg": d_zg}
    dz = jnp.concatenate([_pad_cols(seg[n].astype(BF16), lay.size[n]) for n in lay.order], axis=1)
    dh = mm([(dz, w_in_p)], "nt", F32, name="in_proj_dx")
    G["w_in"] = lay.unpad_grad(mm([(h, dz)], "tn", F32, name="in_proj_dw"))
    g1, G["mix_norm"] = row_bwd(rms_fn, [V(x1)], [W["mix_norm"]], [V(dh)], const_diff=[True], res=g2, name="mix_norm_bwd")

    mid_rs = [n for n in MID if n != "gla_w_gate2"]
    mid_halves = [to_halves(n) for n in mid_rs]

    def mid_sums(got):
        add2(mid_rs, mid_halves, got)
        return exchange_chips([chip_sum[n] for n in mid_rs])

    gx, ffn1_up_xchg = ffn_backward(g1, xs, "ffn1", ffn1_saved, dact_comm=swap_halves(mid_halves), after_dact=mid_sums,
                                    after_dwd=lambda parts: add4_join(mid_rs, parts))

    grad, delta, new_m, new_v = {}, {}, {}, {}

    def adam_group(names, tag, comm=None):
        res, extra = _with(adamw([(W[n], reduced[n], M1[n], V2[n]) for n in names], name=f"adamw_{tag}", comm=comm), comm)
        for n, (d_, m_, v_) in zip(names, res):
            grad[n], delta[n], new_m[n], new_v[n] = reduced[n], d_, m_, v_
        return extra

    add4_join(["ffn1_w_up"], adam_group(FFN2, "ffn2", comm=ffn1_up_xchg))
    adam_group(mid_rs, "mid")
    adam_group(FFN1, "ffn1")

    small_names = SMALL + ["gla_w_gate2"]
    small_sum = allreduce_small(_pack_small([G[n] for n in small_names]), name="allreduce_small")
    small_g = dict(zip(small_names, _unpack_small(small_sum, [G[n].shape for n in small_names])))
    shard_c = W["gla_w_gate2"].shape[1]
    grad["gla_w_gate2"] = lax.dynamic_slice_in_dim(small_g["gla_w_gate2"], chip * shard_c, shard_c, axis=1)
    pw = _pack_small([W[n] for n in SMALL] + [W["gla_w_gate2"]])
    pg = _pack_small([small_g[n] for n in SMALL] + [grad["gla_w_gate2"]])
    pm = _pack_small([M1[n] for n in SMALL] + [M1["gla_w_gate2"]])
    pv = _pack_small([V2[n] for n in SMALL] + [V2["gla_w_gate2"]])
    (pd, pnm, pnv), = adamw([(pw, pg, pm, pv)], name="adamw_small")
    shapes = [W[n].shape for n in small_names]
    for n, d_, m_, v_ in zip(small_names, _unpack_small(pd, shapes), _unpack_small(pnm, shapes), _unpack_small(pnv, shapes)):
        delta[n], new_m[n], new_v[n] = d_, m_, v_
        if n != "gla_w_gate2":
            grad[n] = small_g[n]

    lead = lambda d: [d[n].reshape(args[n].shape) for n in WEIGHTS]
    return (loss, gx[None], *lead(grad), *lead(delta), *lead(new_m), *lead(new_v))
```

```python
import functools
import math

import numpy as np
import jax
import jax.numpy as jnp
from jax import lax
from jax.experimental import pallas as pl
from jax.experimental.pallas import tpu as pltpu

F32 = jnp.float32
BF16 = jnp.bfloat16
MXU_DTYPE = jnp.bfloat16
MESH = pl.DeviceIdType.MESH
ANY = pl.BlockSpec(memory_space=pl.ANY)

LANE = 128
EPS = 1e-6
CHUNK = 64
MLA_HEADS = 8
MLA_NOPE = 128
MLA_ROPE = 64
MLA_QK = MLA_NOPE + MLA_ROPE
MLA_V = 128
MLA_HEAD_PAD = 2 * LANE
ROPE_THETA = 10000.0
GLA_HEADS = 4
GLA_DK = 128
GLA_DV = 256
GLA_GATE_RANK = 16
GLA_TAU = 16.0
MEM_HEADS = 4
MEM_HEAD_DIM = 128
N_CHIPS = 4
N_DEV = 8

ADAM_LR = 0.001
ADAM_B1 = 0.9
ADAM_B2 = 0.999
ADAM_EPS = 1e-08
ADAM_WD = 0.01
ADAM_STEP = 10

VMEM_LIMIT = 56 * 1024 * 1024


def _cparams(sem=None):
    if sem is None:
        return pltpu.CompilerParams(vmem_limit_bytes=VMEM_LIMIT)
    return pltpu.CompilerParams(dimension_semantics=sem, vmem_limit_bytes=VMEM_LIMIT)


def _tile(dim, pref, unit=LANE):
    if dim <= pref:
        return dim
    t = (pref // unit) * unit
    while t > unit and dim % t:
        t -= unit
    assert dim % t == 0, (dim, pref, unit)
    return t


class Comm:
    def __init__(self, ins, out_shapes, nsem, start, wait, aliases=None):
        self.ins, self.out_shapes, self.nsem = list(ins), list(out_shapes), nsem
        self.start, self.wait, self.aliases = start, wait, dict(aliases or {})


def merge_comms(a, b):
    ai, ao = len(a.ins), len(a.out_shapes)

    def start(ins, outs, send, recv, base):
        a.start(ins[:ai], outs[:ao], send, recv, base)
        b.start(ins[ai:], outs[ao:], send, recv, base + a.nsem)

    def wait(ins, outs, send, recv, base):
        a.wait(ins[:ai], outs[:ao], send, recv, base)
        b.wait(ins[ai:], outs[ao:], send, recv, base + a.nsem)

    aliases = dict(a.aliases)
    aliases.update({ai + i: ao + o for i, o in b.aliases.items()})
    return Comm(a.ins + b.ins, a.out_shapes + b.out_shapes, a.nsem + b.nsem, start, wait, aliases)


def run_comm(comm, *, name):
    ni, no = len(comm.ins), len(comm.out_shapes)

    def body(*refs):
        ins, outs = refs[:ni], refs[ni:ni + no]
        send, recv = refs[ni + no:]
        comm.start(ins, outs, send, recv, 0)
        comm.wait(ins, outs, send, recv, 0)

    return pl.pallas_call(
        body, name=name, in_specs=[ANY] * ni, out_specs=[ANY] * no, out_shape=comm.out_shapes,
        input_output_aliases=comm.aliases,
        scratch_shapes=[pltpu.SemaphoreType.DMA((comm.nsem,)), pltpu.SemaphoreType.DMA((comm.nsem,))])(*comm.ins)


def _pcall(body, ops, *, name, grid, in_specs, out_specs, out_shape, sem, scratch_shapes=(), comm=None):
    if comm is None:
        return pl.pallas_call(body, name=name, grid=grid, in_specs=in_specs, out_specs=out_specs, out_shape=out_shape,
                              scratch_shapes=list(scratch_shapes), compiler_params=_cparams(sem))(*ops)
    multi = isinstance(out_shape, (list, tuple))
    k_out_shape = list(out_shape) if multi else [out_shape]
    k_out_specs = list(out_specs) if multi else [out_specs]
    nki, nko, nks = len(ops), len(k_out_shape), len(scratch_shapes)
    nci, nco = len(comm.ins), len(comm.out_shapes)

    def wrapped(*refs):
        p = 0
        k_in = refs[p:p + nki]; p += nki
        c_in = refs[p:p + nci]; p += nci
        k_out = refs[p:p + nko]; p += nko
        c_out = refs[p:p + nco]; p += nco
        k_scr = refs[p:p + nks]; p += nks
        send, recv = refs[p:]
        first = pl.program_id(0) == 0
        last = pl.program_id(0) == grid[0] - 1
        for a in range(1, len(grid)):
            first = jnp.logical_and(first, pl.program_id(a) == 0)
            last = jnp.logical_and(last, pl.program_id(a) == grid[a] - 1)

        @pl.when(first)
        def _():
            comm.start(c_in, c_out, send, recv, 0)

        body(*k_in, *k_out, *k_scr)

        @pl.when(last)
        def _():
            comm.wait(c_in, c_out, send, recv, 0)

    res = pl.pallas_call(
        wrapped, name=name, grid=grid, in_specs=list(in_specs) + [ANY] * nci, out_specs=k_out_specs + [ANY] * nco,
        out_shape=k_out_shape + comm.out_shapes,
        input_output_aliases={nki + i: nko + o for i, o in comm.aliases.items()},
        scratch_shapes=list(scratch_shapes) + [pltpu.SemaphoreType.DMA((comm.nsem,)), pltpu.SemaphoreType.DMA((comm.nsem,))],
        compiler_params=_cparams(("arbitrary",) * len(grid)))(*ops, *comm.ins)
    k_res = list(res[:nko]) if multi else res[0]
    return k_res, list(res[nko:])


_DIMS = {"nn": (((1,), (0,)), ((), ())), "nt": (((1,), (1,)), ((), ())), "tn": (((0,), (0,)), ((), ()))}


def mm(pairs, mode, out_dtype, *, name, alpha=1.0, res=None, tm=1024, tn=512, tk=512, b_slots=False, out_slots=False,
       comm=None):
    a0, b0 = pairs[0]
    if b_slots:
        b_rows, b_cols = b0.shape[1], N_CHIPS * b0.shape[2]
    else:
        b_rows, b_cols = b0.shape
    (M, K) = a0.shape[::-1] if mode == "tn" else a0.shape
    N = b_rows if mode == "nt" else b_cols
    shard = (b_cols if b_slots else N) // N_CHIPS
    tm = _tile(M, tm)
    tn = _tile(shard if (out_slots or (b_slots and mode != "nt")) else N, tn)
    tk = _tile(shard if (b_slots and mode == "nt") else K, tk)
    nk = K // tk
    npairs = len(pairs)
    dims = _DIMS[mode]
    if mode == "tn":
        a_spec = pl.BlockSpec((tk, tm), lambda i, j, k: (k, i))
    else:
        a_spec = pl.BlockSpec((tm, tk), lambda i, j, k: (i, k))
    per = shard // (tk if mode == "nt" else tn)
    if mode == "nt":
        b_spec = (pl.BlockSpec((None, tn, tk), lambda i, j, k: (k // per, j, k % per)) if b_slots else
                  pl.BlockSpec((tn, tk), lambda i, j, k: (j, k)))
    else:
        b_spec = (pl.BlockSpec((None, tk, tn), lambda i, j, k: (j // per, k, j % per)) if b_slots else
                  pl.BlockSpec((tk, tn), lambda i, j, k: (k, j)))
    if out_slots:
        assert res is None and mode != "nt"
        o_spec = pl.BlockSpec((None, tm, tn), lambda i, j, k: (j // per, i, j % per))
        out_sds = jax.ShapeDtypeStruct((N_CHIPS, M, shard), out_dtype)
    else:
        o_spec = pl.BlockSpec((tm, tn), lambda i, j, k: (i, j))
        out_sds = jax.ShapeDtypeStruct((M, N), out_dtype)
    has_res = res is not None

    def body(*refs):
        ab = refs[:2 * npairs]
        res_ref = refs[2 * npairs] if has_res else None
        o_ref, acc = refs[-2], refs[-1]
        k = pl.program_id(2)

        @pl.when(k == 0)
        def _():
            acc[...] = jnp.zeros_like(acc)

        for p in range(npairs):
            acc[...] += lax.dot_general(ab[2 * p][...].astype(MXU_DTYPE), ab[2 * p + 1][...].astype(MXU_DTYPE),
                                        dims, preferred_element_type=F32)

        @pl.when(k == nk - 1)
        def _():
            r = acc[...]
            if alpha != 1.0:
                r = r * alpha
            if has_res:
                r = res_ref[...].astype(F32) + r
            o_ref[...] = r.astype(out_dtype)

    ops, specs = [], []
    for a, b in pairs:
        ops += [a, b]
        specs += [a_spec, b_spec]
    if has_res:
        ops.append(res)
        specs.append(o_spec)
    return _pcall(body, ops, name=name, grid=(M // tm, N // tn, nk), in_specs=specs, out_specs=o_spec,
                  out_shape=out_sds, scratch_shapes=[pltpu.VMEM((tm, tn), F32)],
                  sem=("parallel", "parallel", "arbitrary"), comm=comm)


def _sigmoid(x):
    return 1.0 / (1.0 + jnp.exp(-x))


def ffn_up(n, wg, wu, *, name, tm=1024, tn=1408, tk=512, comm=None):
    M, K = n.shape
    shard = wg.shape[2]
    N = N_CHIPS * shard
    tm, tn, tk = _tile(M, tm), _tile(shard, tn), _tile(K, tk)
    nk = K // tk
    per = shard // tn
    w_spec = pl.BlockSpec((None, tk, tn), lambda i, j, k: (j // per, k, j % per))

    def body(n_ref, wg_ref, wu_ref, g_ref, u_ref, a_ref, accg, accu):
        k = pl.program_id(2)

        @pl.when(k == 0)
        def _():
            accg[...] = jnp.zeros_like(accg)
            accu[...] = jnp.zeros_like(accu)

        nv = n_ref[...].astype(MXU_DTYPE)
        accg[...] += jnp.dot(nv, wg_ref[...].astype(MXU_DTYPE), preferred_element_type=F32)
        accu[...] += jnp.dot(nv, wu_ref[...].astype(MXU_DTYPE), preferred_element_type=F32)

        @pl.when(k == nk - 1)
        def _():
            g, u = accg[...], accu[...]
            g_ref[...] = g.astype(g_ref.dtype)
            u_ref[...] = u.astype(u_ref.dtype)
            a_ref[...] = (g * _sigmoid(g) * u).astype(a_ref.dtype)

    o_spec = pl.BlockSpec((tm, tn), lambda i, j, k: (i, j))
    sds = jax.ShapeDtypeStruct((M, N), BF16)
    return _pcall(
        body, [n, wg, wu], name=name, grid=(M // tm, N // tn, nk),
        in_specs=[pl.BlockSpec((tm, tk), lambda i, j, k: (i, k)), w_spec, w_spec],
        out_specs=[o_spec, o_spec, o_spec], out_shape=[sds, sds, sds],
        scratch_shapes=[pltpu.VMEM((tm, tn), F32), pltpu.VMEM((tm, tn), F32)],
        sem=("parallel", "parallel", "arbitrary"), comm=comm)


def ffn_dact(dy, wd, gate, up, alpha, *, name, tm=1024, tn=512, tk=512, comm=None):
    M, K = dy.shape
    N = wd.shape[0]
    tm, tn, tk = _tile(M, tm), _tile(N, tn), _tile(K, tk)
    nk = K // tk

    def body(dy_ref, wd_ref, g_ref, u_ref, dg_ref, du_ref, acc):
        k = pl.program_id(2)

        @pl.when(k == 0)
        def _():
            acc[...] = jnp.zeros_like(acc)

        acc[...] += lax.dot_general(dy_ref[...].astype(MXU_DTYPE), wd_ref[...].astype(MXU_DTYPE), _DIMS["nt"],
                                    preferred_element_type=F32)

        @pl.when(k == nk - 1)
        def _():
            da = acc[...] * alpha
            g = g_ref[...].astype(F32)
            u = u_ref[...].astype(F32)
            s = _sigmoid(g)
            du_ref[...] = (da * (g * s)).astype(du_ref.dtype)
            dg_ref[...] = (da * u * (s * (1.0 + g * (1.0 - s)))).astype(dg_ref.dtype)

    o_spec = pl.BlockSpec((tm, tn), lambda i, j, k: (i, j))
    sds = jax.ShapeDtypeStruct((M, N), BF16)
    return _pcall(
        body, [dy, wd, gate, up], name=name, grid=(M // tm, N // tn, nk),
        in_specs=[pl.BlockSpec((tm, tk), lambda i, j, k: (i, k)), pl.BlockSpec((tn, tk), lambda i, j, k: (j, k)),
                  o_spec, o_spec],
        out_specs=[o_spec, o_spec], out_shape=[sds, sds], scratch_shapes=[pltpu.VMEM((tm, tn), F32)],
        sem=("parallel", "parallel", "arbitrary"), comm=comm)


class V:
    def __init__(self, arr, off=0, w=None, hs=0, diff=True):
        self.arr, self.off, self.hs, self.diff = arr, off, hs, diff
        self.w = arr.shape[1] - off if w is None else w

    def window(self, heads, tr):
        width = self.arr.shape[1]
        ext = (heads - 1) * self.hs + self.w
        ww = LANE
        while ww < width:
            if ww >= ext and self.off // ww == (self.off + ext - 1) // ww and width % ww == 0:
                break
            ww *= 2
        else:
            ww = width
        blk = self.off // ww
        return pl.BlockSpec((tr, ww), lambda i, blk=blk: (i, blk)), self.off - blk * ww


def _const_spec(c):
    return pl.BlockSpec(c.shape, lambda i: (0, 0))


def row_fwd(fn, rows, consts, outs, out_map, *, heads=1, tr=256, name):
    S = rows[0].arr.shape[0]
    tr = _tile(S, tr, 8)
    wins = [v.window(heads, tr) for v in rows]
    nr, nc = len(rows), len(consts)

    def body(*refs):
        row_refs, const_refs, out_refs = refs[:nr], refs[nr:nr + nc], refs[nr + nc:]
        cv = [c[...].astype(F32) for c in const_refs]
        for h in range(heads):
            rv = []
            for v, (_, io), r in zip(rows, wins, row_refs):
                lo = io + h * v.hs
                rv.append(r[:, lo:lo + v.w].astype(F32))
            res = fn(*rv, *cv)
            for (ai, off, hs), o in zip(out_map, res):
                lo = off + h * hs
                out_refs[ai][:, lo:lo + o.shape[1]] = o.astype(out_refs[ai].dtype)

    return pl.pallas_call(
        body, name=name, grid=(S // tr,),
        in_specs=[w[0] for w in wins] + [_const_spec(c) for c in consts],
        out_specs=[pl.BlockSpec((tr, w), lambda i: (i, 0)) for w, _ in outs],
        out_shape=[jax.ShapeDtypeStruct((S, w), d) for w, d in outs],
        compiler_params=_cparams(("parallel",)))(*[v.arr for v in rows], *consts)


def row_bwd(fn, rows, consts, cots, *, const_diff, heads=1, tr=256, res=None, row_dtype=F32, name):
    S = rows[0].arr.shape[0]
    tr = _tile(S, tr, 8)
    nr, nc, nct = len(rows), len(consts), len(cots)
    wins = [v.window(heads, tr) for v in rows]
    cwins = [v.window(heads, tr) for v in cots]
    drows = [k for k, v in enumerate(rows) if v.diff]
    dconsts = [k for k in range(nc) if const_diff[k]]
    has_res = res is not None
    ngrid = S // tr

    def body(*refs):
        row_refs = refs[:nr]
        const_refs = refs[nr:nr + nc]
        cot_refs = refs[nr + nc:nr + nc + nct]
        p = nr + nc + nct
        res_ref = refs[p] if has_res else None
        p += int(has_res)
        grow_refs = refs[p:p + len(drows)]
        gconst_refs = refs[p + len(drows):]
        i = pl.program_id(0)
        cv = [c[...].astype(F32) for c in const_refs]
        shared = [None] * len(drows)
        gc_sum = [None] * len(dconsts)
        for h in range(heads):
            rv = []
            for v, (_, io), r in zip(rows, wins, row_refs):
                lo = io + h * v.hs
                rv.append(r[:, lo:lo + v.w].astype(F32))
            ct = []
            for v, (_, io), r in zip(cots, cwins, cot_refs):
                lo = io + h * v.hs
                ct.append(r[:, lo:lo + v.w].astype(F32))

            def closed(*d):
                rr, cc = list(rv), list(cv)
                for k, val in zip(drows, d[:len(drows)]):
                    rr[k] = val
                for k, val in zip(dconsts, d[len(drows):]):
                    cc[k] = val
                return tuple(fn(*rr, *cc))

            _, vjp = jax.vjp(closed, *[rv[k] for k in drows], *[cv[k] for k in dconsts])
            grads = vjp(tuple(ct))
            for n, k in enumerate(drows):
                g = grads[n]
                if rows[k].hs == 0 and heads > 1:
                    shared[n] = g if shared[n] is None else shared[n] + g
                else:
                    if n == 0 and has_res:
                        g = g + res_ref[:, h * rows[k].w:(h + 1) * rows[k].w].astype(F32)
                    grow_refs[n][:, h * rows[k].w:(h + 1) * rows[k].w] = g.astype(row_dtype)
            for n in range(len(dconsts)):
                g = grads[len(drows) + n]
                gc_sum[n] = g if gc_sum[n] is None else gc_sum[n] + g
        for n, k in enumerate(drows):
            if shared[n] is not None:
                g = shared[n]
                if n == 0 and has_res:
                    g = g + res_ref[...].astype(F32)
                grow_refs[n][...] = g.astype(row_dtype)

        @pl.when(i == 0)
        def _():
            for n in range(len(dconsts)):
                gconst_refs[n][...] = gc_sum[n]

        @pl.when(i > 0)
        def _():
            for n in range(len(dconsts)):
                gconst_refs[n][...] += gc_sum[n]

    gw = [rows[k].w * (heads if rows[k].hs else 1) for k in drows]
    in_specs = [w[0] for w in wins] + [_const_spec(c) for c in consts] + [w[0] for w in cwins]
    ops = [v.arr for v in rows] + list(consts) + [v.arr for v in cots]
    if has_res:
        in_specs.append(pl.BlockSpec((tr, gw[0]), lambda i: (i, 0)))
        ops.append(res)
    out_specs = [pl.BlockSpec((tr, w), lambda i: (i, 0)) for w in gw]
    out_shape = [jax.ShapeDtypeStruct((S, w), row_dtype) for w in gw]
    for k in dconsts:
        out_specs.append(_const_spec(consts[k]))
        out_shape.append(jax.ShapeDtypeStruct(consts[k].shape, F32))
    del ngrid
    return pl.pallas_call(body, name=name, grid=(S // tr,), in_specs=in_specs, out_specs=out_specs,
                          out_shape=out_shape, compiler_params=_cparams(("arbitrary",)))(*ops)


def _rms(x, g, n=None):
    n = x.shape[-1] if n is None else n
    ms = jnp.sum(x * x, axis=-1, keepdims=True) * (1.0 / n)
    return x * lax.rsqrt(ms + EPS) * g


def rms_fn(x, g):
    return (_rms(x, g),)


def qk_prep_fn(nope, rope, cos, sin, gn, gr, rot):
    ms = (jnp.sum(nope * nope, axis=-1, keepdims=True) + jnp.sum(rope * rope, axis=-1, keepdims=True)) * (1.0 / MLA_QK)
    r = lax.rsqrt(ms + EPS)
    on = nope * r * gn
    orr = rope * r * gr
    turned = jnp.dot(orr, rot, precision=lax.Precision.HIGHEST, preferred_element_type=F32)
    return on, orr * cos + turned * sin


def gla_out_fn(o, zr, g):
    return (_rms(o, g) * (zr * _sigmoid(zr)),)


def gate_fn(pre, b):
    t = pre + b
    return ((jnp.minimum(t, 0.0) - jnp.log(1.0 + jnp.exp(-jnp.abs(t)))) * (1.0 / GLA_TAU),)


def _attn_probs(q_ref, k_ref, scale, causal, tq):
    s = lax.dot_general(q_ref[...].astype(MXU_DTYPE), k_ref[...].astype(MXU_DTYPE), _DIMS["nt"],
                        preferred_element_type=F32) * scale
    if causal:
        i = pl.program_id(1)
        qc = (i * tq + lax.broadcasted_iota(jnp.int32, s.shape, 0)) // CHUNK
        kc = lax.broadcasted_iota(jnp.int32, s.shape, 1) // CHUNK
        s = jnp.where(kc <= qc, s, -1e30)
    m = jnp.max(s, axis=-1, keepdims=True)
    e = jnp.exp(s - m)
    return e / jnp.sum(e, axis=-1, keepdims=True)


def attn_fwd(q, k, v, *, heads, dk, dv, v_off, v_hs, scale, causal, name, tq=256, comm=None):
    Sq, Sk = q.shape[0], k.shape[0]
    tq = _tile(Sq, tq, 8)

    def body(q_ref, k_ref, v_ref, o_ref):
        p = _attn_probs(q_ref, k_ref, scale, causal, tq)
        o_ref[...] = jnp.dot(p.astype(MXU_DTYPE), v_ref[...].astype(MXU_DTYPE),
                             preferred_element_type=F32).astype(o_ref.dtype)

    return _pcall(
        body, [q, k, v], name=name, grid=(heads, Sq // tq),
        in_specs=[pl.BlockSpec((tq, dk), lambda h, i: (i, h)), pl.BlockSpec((Sk, dk), lambda h, i: (0, h)),
                  pl.BlockSpec((Sk, dv), lambda h, i: (0, v_off + h * v_hs))],
        out_specs=pl.BlockSpec((tq, dv), lambda h, i: (i, h)),
        out_shape=jax.ShapeDtypeStruct((Sq, heads * dv), BF16), sem=("parallel", "parallel"), comm=comm)


def attn_bwd(q, k, v, do, *, heads, dk, dv, v_off, v_hs, scale, causal, name, tq=256, comm=None):
    Sq, Sk = q.shape[0], k.shape[0]
    tq = _tile(Sq, tq, 8)

    def body(q_ref, k_ref, v_ref, do_ref, dq_ref, dk_ref, dv_ref):
        i = pl.program_id(1)
        p = _attn_probs(q_ref, k_ref, scale, causal, tq)
        dob = do_ref[...].astype(MXU_DTYPE)
        dp = lax.dot_general(dob, v_ref[...].astype(MXU_DTYPE), _DIMS["nt"], preferred_element_type=F32)
        delta = jnp.sum(p * dp, axis=-1, keepdims=True)
        ds = (p * (dp - delta) * scale).astype(MXU_DTYPE)
        dq_ref[...] = jnp.dot(ds, k_ref[...].astype(MXU_DTYPE), preferred_element_type=F32)
        dkc = lax.dot_general(ds, q_ref[...].astype(MXU_DTYPE), _DIMS["tn"], preferred_element_type=F32)
        dvc = lax.dot_general(p.astype(MXU_DTYPE), dob, _DIMS["tn"], preferred_element_type=F32)

        @pl.when(i == 0)
        def _():
            dk_ref[...] = dkc
            dv_ref[...] = dvc

        @pl.when(i > 0)
        def _():
            dk_ref[...] += dkc
            dv_ref[...] += dvc

    return _pcall(
        body, [q, k, v, do], name=name, grid=(heads, Sq // tq),
        in_specs=[pl.BlockSpec((tq, dk), lambda h, i: (i, h)), pl.BlockSpec((Sk, dk), lambda h, i: (0, h)),
                  pl.BlockSpec((Sk, dv), lambda h, i: (0, v_off + h * v_hs)),
                  pl.BlockSpec((tq, dv), lambda h, i: (i, h))],
        out_specs=[pl.BlockSpec((tq, dk), lambda h, i: (i, h)), pl.BlockSpec((Sk, dk), lambda h, i: (0, h)),
                   pl.BlockSpec((Sk, dv), lambda h, i: (0, h))],
        out_shape=[jax.ShapeDtypeStruct((Sq, heads * dk), F32), jax.ShapeDtypeStruct((Sk, heads * dk), F32),
                   jax.ShapeDtypeStruct((Sk, heads * dv), F32)],
        sem=("parallel", "arbitrary"), comm=comm)


def _gla_chunk(k_ref, la_ref, tri_ref):
    g = la_ref[...].astype(F32)
    b = jnp.dot(tri_ref[...], g, precision=lax.Precision.HIGHEST, preferred_element_type=F32)
    b_end = jnp.sum(g, axis=0, keepdims=True)
    e = jnp.exp(b_end - b)
    return k_ref[...].astype(F32) * e, e, jnp.exp(b_end)


def gla_fwd(z, la, tri, *, q_off, k_off, v_off, name, comm=None):
    S = z.shape[0]
    nchunk = S // CHUNK
    H, DK, DV = GLA_HEADS, GLA_DK, GLA_DV
    qb, kb, vb = q_off // DK, k_off // DK, v_off // DV
    qscale = DK ** -0.5

    def body(q_ref, k_ref, v_ref, la_ref, tri_ref, o_ref, st_ref, state):
        c = pl.program_id(1)

        @pl.when(c == 0)
        def _():
            state[...] = jnp.zeros_like(state)

        kdec, _, decay = _gla_chunk(k_ref, la_ref, tri_ref)
        ut = lax.dot_general(v_ref[...].astype(MXU_DTYPE), kdec.astype(MXU_DTYPE), _DIMS["tn"],
                             preferred_element_type=F32)
        new = state[...] * decay + ut
        state[...] = new
        st_ref[...] = new
        qs = (q_ref[...].astype(F32) * qscale).astype(MXU_DTYPE)
        o_ref[...] = lax.dot_general(qs, new.astype(MXU_DTYPE), _DIMS["nt"], preferred_element_type=F32)

    return _pcall(
        body, [z, z, z, la, tri], name=name, grid=(H, nchunk),
        in_specs=[pl.BlockSpec((CHUNK, DK), lambda h, c: (c, qb + h)), pl.BlockSpec((CHUNK, DK), lambda h, c: (c, kb + h)),
                  pl.BlockSpec((CHUNK, DV), lambda h, c: (c, vb + h)), pl.BlockSpec((CHUNK, DK), lambda h, c: (c, h)),
                  pl.BlockSpec((CHUNK, CHUNK), lambda h, c: (0, 0))],
        out_specs=[pl.BlockSpec((CHUNK, DV), lambda h, c: (c, h)),
                   pl.BlockSpec((None, None, DV, DK), lambda h, c: (h, c, 0, 0))],
        out_shape=[jax.ShapeDtypeStruct((S, H * DV), F32), jax.ShapeDtypeStruct((H, nchunk, DV, DK), F32)],
        scratch_shapes=[pltpu.VMEM((DV, DK), F32)], sem=("parallel", "arbitrary"), comm=comm)


def gla_bwd(z, la, tri, trit, states, do, *, q_off, k_off, v_off, name, comm=None):
    S = z.shape[0]
    nchunk = S // CHUNK
    H, DK, DV = GLA_HEADS, GLA_DK, GLA_DV
    qb, kb, vb = q_off // DK, k_off // DK, v_off // DV
    qscale = DK ** -0.5
    last = nchunk - 1

    def body(q_ref, k_ref, v_ref, la_ref, tri_ref, trit_ref, st_ref, sp_ref, do_ref, dq_ref, dk_ref, dv_ref, dla_ref,
             dstate):
        c = pl.program_id(1)
        cc = last - c

        @pl.when(c == 0)
        def _():
            dstate[...] = jnp.zeros_like(dstate)

        kdec, e, decay = _gla_chunk(k_ref, la_ref, tri_ref)
        kf = k_ref[...].astype(F32)
        dob = do_ref[...].astype(MXU_DTYPE)
        stb = st_ref[...].astype(MXU_DTYPE)
        qs = (q_ref[...].astype(F32) * qscale).astype(MXU_DTYPE)
        dq_ref[...] = jnp.dot(dob, stb, preferred_element_type=F32) * qscale
        dst = dstate[...] + lax.dot_general(dob, qs, _DIMS["tn"], preferred_element_type=F32)
        prev = jnp.where(cc > 0, sp_ref[...], 0.0)
        ddecay = jnp.sum(dst * prev, axis=0, keepdims=True)
        dstate[...] = dst * decay
        dub = dst.astype(MXU_DTYPE)
        vb16 = v_ref[...].astype(MXU_DTYPE)
        dv_ref[...] = lax.dot_general(kdec.astype(MXU_DTYPE), dub, _DIMS["nt"], preferred_element_type=F32)
        dkdec = jnp.dot(vb16, dub, preferred_element_type=F32)
        dk_ref[...] = dkdec * e
        w = dkdec * kf * e
        db_end = jnp.sum(w, axis=0, keepdims=True) + ddecay * decay
        dla_ref[...] = db_end - jnp.dot(trit_ref[...], w, precision=lax.Precision.HIGHEST, preferred_element_type=F32)

    rows = lambda blk: (lambda h, c: (last - c, blk + h))
    return _pcall(
        body, [z, z, z, la, tri, trit, states, states, do], name=name, grid=(H, nchunk),
        in_specs=[pl.BlockSpec((CHUNK, DK), rows(qb)), pl.BlockSpec((CHUNK, DK), rows(kb)),
                  pl.BlockSpec((CHUNK, DV), rows(vb)), pl.BlockSpec((CHUNK, DK), rows(0)),
                  pl.BlockSpec((CHUNK, CHUNK), lambda h, c: (0, 0)), pl.BlockSpec((CHUNK, CHUNK), lambda h, c: (0, 0)),
                  pl.BlockSpec((None, None, DV, DK), lambda h, c: (h, last - c, 0, 0)),
                  pl.BlockSpec((None, None, DV, DK), lambda h, c: (h, jnp.maximum(last - c - 1, 0), 0, 0)),
                  pl.BlockSpec((CHUNK, DV), rows(0))],
        out_specs=[pl.BlockSpec((CHUNK, DK), rows(0)), pl.BlockSpec((CHUNK, DK), rows(0)),
                   pl.BlockSpec((CHUNK, DV), rows(0)), pl.BlockSpec((CHUNK, DK), rows(0))],
        out_shape=[jax.ShapeDtypeStruct((S, H * DK), F32), jax.ShapeDtypeStruct((S, H * DK), F32),
                   jax.ShapeDtypeStruct((S, H * DV), F32), jax.ShapeDtypeStruct((S, H * DK), F32)],
        scratch_shapes=[pltpu.VMEM((DV, DK), F32)], sem=("parallel", "arbitrary"), comm=comm)


def loss_head(y, target, *, name, tr=256):
    S, D = y.shape
    tr = _tile(S, tr, 8)

    def body(y_ref, t_ref, dy_ref, loss_ref):
        i = pl.program_id(0)
        err = y_ref[...] - t_ref[...]
        dy_ref[...] = err * (1.0 / D)
        part = jnp.zeros((1, LANE), F32) + 0.5 * jnp.sum(jnp.sum(err * err, axis=-1, keepdims=True) * (1.0 / D))

        @pl.when(i == 0)
        def _():
            loss_ref[...] = part

        @pl.when(i > 0)
        def _():
            loss_ref[...] += part

    spec = pl.BlockSpec((tr, D), lambda i: (i, 0))
    return pl.pallas_call(
        body, name=name, grid=(S // tr,), in_specs=[spec, spec],
        out_specs=[spec, pl.BlockSpec((1, LANE), lambda i: (0, 0))],
        out_shape=[jax.ShapeDtypeStruct((S, D), F32), jax.ShapeDtypeStruct((1, LANE), F32)],
        compiler_params=_cparams(("arbitrary",)))(y, target)


def _core_index():
    return lax.axis_index("c").astype(jnp.int32).reshape(1)


def _chip_slots():
    x, y, c = lax.axis_index("x"), lax.axis_index("y"), lax.axis_index("c")
    return jnp.stack([2 * x + y, 2 * (1 - x) + y, 2 * x + (1 - y), 2 * (1 - x) + (1 - y), c]).astype(jnp.int32)


def sum_chip_parts(own, parts, *, name, tr=256):
    _, R, C = own.shape
    tr = _tile(R, tr, 8)

    def body(idx_ref, o_ref, p0_ref, p1_ref, p2_ref, out_ref):
        acc = o_ref[...].astype(F32) + p0_ref[...].astype(F32)
        acc = acc + p1_ref[...].astype(F32)
        out_ref[...] = acc + p2_ref[...].astype(F32)

    def slot(k):
        return pl.BlockSpec((None, tr, C), lambda i, idx: (idx[k], i, 0))

    grid_spec = pltpu.PrefetchScalarGridSpec(num_scalar_prefetch=1, grid=(R // tr,),
                                             in_specs=[slot(0), slot(1), slot(2), slot(3)], out_specs=slot(4))
    return pl.pallas_call(body, name=name, grid_spec=grid_spec, out_shape=jax.ShapeDtypeStruct((2, R, C), F32),
                          compiler_params=_cparams(("parallel",)))(_chip_slots(), own, parts, parts, parts)


def add_own_half(g, got, out_dtype, *, name, tr=256):
    n, _, R, C = g.shape
    tr = _tile(R, tr, 8)

    def body(c_ref, a_ref, b_ref, o_ref):
        o_ref[...] = (a_ref[...].astype(F32) + b_ref[...].astype(F32)).astype(out_dtype)

    spec = pl.BlockSpec((None, tr, C), lambda s, i, c: (s, i, 0))
    grid_spec = pltpu.PrefetchScalarGridSpec(
        num_scalar_prefetch=1, grid=(n, R // tr),
        in_specs=[pl.BlockSpec((None, None, tr, C), lambda s, i, c: (s, c[0], i, 0)), spec], out_specs=spec)
    return pl.pallas_call(body, name=name, grid_spec=grid_spec, out_shape=jax.ShapeDtypeStruct((n, R, C), out_dtype),
                          compiler_params=_cparams(("parallel", "parallel")))(_core_index(), g, got)


def adamw(items, *, name, max_steps=16, comm=None):
    c1 = 1.0 / (1.0 - ADAM_B1 ** ADAM_STEP)
    c2 = 1.0 / (1.0 - ADAM_B2 ** ADAM_STEP)
    n = len(items)
    steps = max_steps
    while steps > 1 and any(it[0].shape[0] % (8 * steps) for it in items):
        steps //= 2

    def body(*refs):
        for a in range(n):
            w_ref, g_ref, m_ref, v_ref = refs[4 * a:4 * a + 4]
            d_ref, nm_ref, nv_ref = refs[4 * n + 3 * a:4 * n + 3 * a + 3]
            gv = g_ref[...]
            nm = ADAM_B1 * m_ref[...] + (1.0 - ADAM_B1) * gv
            nv = ADAM_B2 * v_ref[...] + (1.0 - ADAM_B2) * (gv * gv)
            nm_ref[...] = nm
            nv_ref[...] = nv
            d_ref[...] = -ADAM_LR * ((nm * c1) / (jnp.sqrt(nv * c2) + ADAM_EPS) + ADAM_WD * w_ref[...])

    ops, in_specs, out_specs, out_shape = [], [], [], []
    for w, g, m, v in items:
        R, C = w.shape
        spec = pl.BlockSpec((R // steps, C), lambda i: (i, 0))
        ops += [w, g, m, v]
        in_specs += [spec] * 4
        out_specs += [spec] * 3
        out_shape += [jax.ShapeDtypeStruct((R, C), F32)] * 3
    res = _pcall(body, ops, name=name, grid=(steps,), in_specs=in_specs, out_specs=out_specs, out_shape=out_shape,
                 sem=("parallel",), comm=comm)
    flat, extra = res if comm is not None else (res, None)
    triples = [tuple(flat[3 * a:3 * a + 3]) for a in range(n)]
    return (triples, extra) if comm is not None else triples


def _place():
    x, y, c = lax.axis_index("x"), lax.axis_index("y"), lax.axis_index("c")
    chips = [(1 - x, y), (x, 1 - y), (1 - x, 1 - y)]
    return x, y, c, chips


def _rcopy(src, dst, send, recv, j, to):
    return pltpu.make_async_remote_copy(src_ref=src, dst_ref=dst, send_sem=send.at[j], recv_sem=recv.at[j], device_id=to,
                                        device_id_type=MESH)


def gather_stage1(shards, split):
    n = len(shards)
    ins = [s.reshape(2, s.shape[0] // 2, s.shape[1]) if sp else s for s, sp in zip(shards, split)]
    outs = [jax.ShapeDtypeStruct((N_CHIPS,) + a.shape, a.dtype) for a in ins]

    def start(in_refs, out_refs, send, recv, base):
        x, y, c, chips = _place()
        mine = 2 * x + y
        for i in range(n):
            src = in_refs[i].at[c] if split[i] else in_refs[i]
            dst = out_refs[i].at[mine, c] if split[i] else out_refs[i].at[mine]
            for k, (px, py) in enumerate(chips):
                _rcopy(src, dst, send, recv, base + 3 * i + k, (px, py, c)).start()

    def wait(in_refs, out_refs, send, recv, base):
        x, y, c, chips = _place()
        for i in range(n):
            src = in_refs[i].at[c] if split[i] else in_refs[i]
            for k, (px, py) in enumerate(chips):
                dst = out_refs[i].at[2 * px + py, c] if split[i] else out_refs[i].at[2 * px + py]
                _rcopy(src, dst, send, recv, base + 3 * i + k, (px, py, c)).wait()

    return Comm(ins, outs, 3 * n, start, wait)


def gather_stage2(slots, shards, split):
    n = len(slots)
    own = [s.reshape(2, s.shape[0] // 2, s.shape[1]) if sp else s for s, sp in zip(shards, split)]

    def copies(in_refs, out_refs, send, recv, base):
        x, y, c, chips = _place()
        sib = (x, y, 1 - c)
        for i in range(n):
            j = base + 4 * i
            mine = out_refs[i].at[2 * x + y]
            yield _rcopy(in_refs[n + i], mine, send, recv, j + 3, sib), _rcopy(in_refs[n + i], mine, send, recv, j + 3, sib)
            if split[i]:
                for k, (px, py) in enumerate(chips):
                    s = 2 * px + py
                    yield (_rcopy(in_refs[i].at[s, c], out_refs[i].at[s, c], send, recv, j + k, sib),
                           _rcopy(in_refs[i].at[s, c], out_refs[i].at[s, 1 - c], send, recv, j + k, sib))

    def start(*a):
        for out, _ in copies(*a):
            out.start()

    def wait(*a):
        for _, back in copies(*a):
            back.wait()

    return Comm(list(slots) + own, [jax.ShapeDtypeStruct(s.shape, s.dtype) for s in slots], 4 * n, start, wait,
                {i: i for i in range(n)})


def swap_halves(gs):
    n = len(gs)

    def copies(in_refs, out_refs, send, recv, base):
        x, y, c, _ = _place()
        return [_rcopy(in_refs[i].at[s, 1 - c], out_refs[i].at[s], send, recv, base + N_CHIPS * i + s, (x, y, 1 - c))
                for i in range(n) for s in range(N_CHIPS)]

    def start(*a):
        for cp in copies(*a):
            cp.start()

    def wait(*a):
        for cp in copies(*a):
            cp.wait()

    return Comm(gs, [jax.ShapeDtypeStruct((N_CHIPS,) + g.shape[2:], g.dtype) for g in gs], N_CHIPS * n, start, wait)


def exchange_chips(ps):
    n = len(ps)

    def start(in_refs, out_refs, send, recv, base):
        x, y, c, chips = _place()
        for i in range(n):
            for k, (px, py) in enumerate(chips):
                _rcopy(in_refs[i].at[2 * px + py], out_refs[i].at[2 * x + y], send, recv, base + 3 * i + k,
                       (px, py, c)).start()

    def wait(in_refs, out_refs, send, recv, base):
        x, y, c, chips = _place()
        for i in range(n):
            for k, (px, py) in enumerate(chips):
                _rcopy(in_refs[i].at[2 * px + py], out_refs[i].at[2 * px + py], send, recv, base + 3 * i + k,
                       (px, py, c)).wait()

    return Comm(ps, [jax.ShapeDtypeStruct(p.shape, p.dtype) for p in ps], 3 * n, start, wait)


def join_halves(fs):
    n = len(fs)

    def start(in_refs, out_refs, send, recv, base):
        x, y, c, _ = _place()
        for i in range(n):
            _rcopy(in_refs[i].at[c], out_refs[i].at[c], send, recv, base + i, (x, y, 1 - c)).start()

    def wait(in_refs, out_refs, send, recv, base):
        x, y, c, _ = _place()
        for i in range(n):
            _rcopy(in_refs[i].at[c], out_refs[i].at[1 - c], send, recv, base + i, (x, y, 1 - c)).wait()

    return Comm(fs, [jax.ShapeDtypeStruct(f.shape, f.dtype) for f in fs], n, start, wait, {i: i for i in range(n)})


def allreduce_small(v, *, name):
    m_per, n = v.shape

    def body(x_ref, sum_ref, all_ref, send_sems, recv_sems, local_sem):
        x, y, c, chips = _place()
        me, sibling = (x, y, c), (x, y, 1 - c)

        def rows(px, py, pc):
            return all_ref.at[pl.ds((4 * px + 2 * py + pc) * m_per, m_per), :]

        def copy(k, block, to, src=None):
            return pltpu.make_async_remote_copy(src_ref=rows(*block) if src is None else src, dst_ref=rows(*block),
                                                send_sem=send_sems.at[k], recv_sem=recv_sems.at[k], device_id=to,
                                                device_id_type=MESH)

        mine = pltpu.make_async_copy(x_ref, rows(*me), local_sem)
        mine.start()
        first = [copy(0, me, sibling, src=x_ref)]
        first += [copy(1 + j, me, (*chip, c), src=x_ref) for j, chip in enumerate(chips)]
        for cp in first:
            cp.start()
        passed = [copy(4 + j, (*chip, c), sibling) for j, chip in enumerate(chips)]
        for j, chip in enumerate(chips):
            copy(1 + j, (*chip, c), me).wait_recv()
            passed[j].start()
        copy(0, sibling, me).wait_recv()
        for j, chip in enumerate(chips):
            copy(4 + j, (*chip, 1 - c), me).wait_recv()
        for cp in first + passed:
            cp.wait_send()
        mine.wait()
        acc = all_ref[0:m_per, :]
        for d in range(1, N_DEV):
            acc = acc + all_ref[d * m_per:(d + 1) * m_per, :]
        sum_ref[...] = acc

    vm = pl.BlockSpec(memory_space=pltpu.VMEM)
    return pl.pallas_call(
        body, name=name, in_specs=[vm], out_specs=vm, out_shape=jax.ShapeDtypeStruct((m_per, n), F32),
        scratch_shapes=[pltpu.VMEM((N_DEV * m_per, n), F32), pltpu.SemaphoreType.DMA((7,)),
                        pltpu.SemaphoreType.DMA((7,)), pltpu.SemaphoreType.DMA],
    )(v)


def _cols_to_slots(w):
    r, c4 = w.shape
    return w.reshape(r, N_CHIPS, c4 // N_CHIPS).transpose(1, 0, 2)


def _slots_to_cols(w):
    n, r, c = w.shape
    return w.transpose(1, 0, 2).reshape(r, n * c)


def _pad_cols(a, width):
    return jnp.pad(a, ((0, 0), (0, width - a.shape[1])))


class InLayout:
    def __init__(self, q_rank, kv_rank):
        gk = GLA_HEADS * GLA_DK
        gv = GLA_HEADS * GLA_DV
        sizes = [q_rank, kv_rank, MLA_ROPE, gk, gk, gv, GLA_GATE_RANK, gv]
        names = ["zq", "zkv", "zkr", "gq", "gk", "gv", "zg", "zr"]
        starts = np.concatenate([[0], np.cumsum(sizes)[:-1]])
        self.ref = {n: (int(s), int(z)) for n, s, z in zip(names, starts, sizes)}
        self.ref_width = int(sum(sizes))
        self.order = ["gv", "zr", "zq", "gq", "gk", "zkv", "zkr", "zg"]
        self.off, self.size = {}, {}
        pos = 0
        for n in self.order:
            padded = -(-self.ref[n][1] // LANE) * LANE
            self.off[n], self.size[n] = pos, padded
            pos += padded
        self.width = pos

    def pad_weight(self, w):
        return jnp.concatenate([_pad_cols(w[:, self.ref[n][0]:self.ref[n][0] + self.ref[n][1]], self.size[n])
                                for n in self.order], axis=1)

    def unpad_grad(self, g):
        names = sorted(self.ref, key=lambda n: self.ref[n][0])
        return jnp.concatenate([g[:, self.off[n]:self.off[n] + self.ref[n][1]] for n in names], axis=1)


def _pad_q_up(w):
    r = w.shape[0]
    w = w.reshape(r, MLA_HEADS, MLA_QK)
    w = jnp.pad(w, ((0, 0), (0, 0), (0, MLA_HEAD_PAD - MLA_QK)))
    return w.reshape(r, MLA_HEADS * MLA_HEAD_PAD)


def _unpad_q_up(g):
    r = g.shape[0]
    return g.reshape(r, MLA_HEADS, MLA_HEAD_PAD)[:, :, :MLA_QK].reshape(r, MLA_HEADS * MLA_QK)


def _interleave(a, b, heads):
    s = a.shape[0]
    w = a.shape[1] // heads
    return jnp.stack([a.reshape(s, heads, w), b.reshape(s, heads, w)], axis=2).reshape(s, heads * 2 * w)


def _rope_tables(positions):
    half = MLA_ROPE // 2
    inv_freq = ROPE_THETA ** (-jnp.arange(half, dtype=F32) / half)
    ang = positions.astype(F32).reshape(-1, 1) * inv_freq
    cos, sin = jnp.cos(ang), jnp.sin(ang)
    s = ang.shape[0]
    cosf = jnp.concatenate([cos, cos, jnp.ones((s, LANE - MLA_ROPE), F32)], axis=1)
    sinf = jnp.concatenate([sin, sin, jnp.zeros((s, LANE - MLA_ROPE), F32)], axis=1)
    rot = np.zeros((LANE, LANE), np.float32)
    for j in range(half):
        rot[j + half, j] = -1.0
        rot[j, j + half] = 1.0
    return cosf, sinf, jnp.asarray(rot)


SMALL = ["ffn1_norm", "mix_norm", "q_a_norm", "kv_a_norm", "mla_q_norm", "mla_k_norm", "gla_b_gate", "gla_out_norm",
         "mem_attn_norm", "mem_norm", "mem_q_norm", "mem_k_norm", "ffn2_norm"]
BIG = ["ffn1_w_gate", "ffn1_w_up", "ffn1_w_down", "w_in", "w_q_up", "w_kv_up", "w_out", "mem_w_q", "mem_w_k",
       "mem_w_v", "mem_w_o", "ffn2_w_gate", "ffn2_w_up", "ffn2_w_down"]
COL_SHARDED = {"ffn1_w_gate", "ffn1_w_up", "w_in", "w_q_up", "w_kv_up", "gla_w_gate2", "mem_w_o", "ffn2_w_gate", "ffn2_w_up"}
WEIGHTS = ["ffn1_norm", "ffn1_w_gate", "ffn1_w_up", "ffn1_w_down", "mix_norm", "w_in", "q_a_norm", "w_q_up", "kv_a_norm",
           "w_kv_up", "mla_q_norm", "mla_k_norm", "gla_w_gate2", "gla_b_gate", "gla_out_norm", "w_out", "mem_attn_norm",
           "mem_norm", "mem_w_q", "mem_w_k", "mem_w_v", "mem_w_o", "mem_q_norm", "mem_k_norm", "ffn2_norm", "ffn2_w_gate",
           "ffn2_w_up", "ffn2_w_down"]


def _pack_small(vals, rows=8):
    flat = jnp.concatenate([v.reshape(-1).astype(F32) for v in vals])
    n = flat.shape[0]
    per = -(-n // (rows * LANE)) * LANE
    return jnp.pad(flat, (0, rows * per - n)).reshape(rows, per)


def _unpack_small(packed, shapes):
    flat = packed.reshape(-1)
    out, pos = [], 0
    for s in shapes:
        n = int(np.prod(s))
        out.append(flat[pos:pos + n].reshape(s))
        pos += n
    return out


FFN1 = ["ffn1_w_gate", "ffn1_w_up", "ffn1_w_down"]
FFN2 = ["ffn2_w_gate", "ffn2_w_up", "ffn2_w_down"]
SLOT_WEIGHTS = {"ffn1_w_gate", "ffn1_w_up", "ffn2_w_gate", "ffn2_w_up"}
MID = ["w_in", "w_q_up", "w_kv_up", "w_out", "mem_w_q", "mem_w_k", "mem_w_v", "mem_w_o", "gla_w_gate2"]


def _with(res, comm):
    return res if comm is not None else (res, None)


def kernel(x, mem, positions, ffn1_norm, ffn1_w_gate, ffn1_w_up, ffn1_w_down, mix_norm, w_in, q_a_norm, w_q_up, kv_a_norm, w_kv_up, mla_q_norm, mla_k_norm, gla_w_gate2, gla_b_gate, gla_out_norm, w_out, mem_attn_norm, mem_norm, mem_w_q, mem_w_k, mem_w_v, mem_w_o, mem_q_norm, mem_k_norm, ffn2_norm, ffn2_w_gate, ffn2_w_up, ffn2_w_down, loss_target, m_ffn1_norm, m_ffn1_w_gate, m_ffn1_w_up, m_ffn1_w_down, m_mix_norm, m_w_in, m_q_a_norm, m_w_q_up, m_kv_a_norm, m_w_kv_up, m_mla_q_norm, m_mla_k_norm, m_gla_w_gate2, m_gla_b_gate, m_gla_out_norm, m_w_out, m_mem_attn_norm, m_mem_norm, m_mem_w_q, m_mem_w_k, m_mem_w_v, m_mem_w_o, m_mem_q_norm, m_mem_k_norm, m_ffn2_norm, m_ffn2_w_gate, m_ffn2_w_up, m_ffn2_w_down, v_ffn1_norm, v_ffn1_w_gate, v_ffn1_w_up, v_ffn1_w_down, v_mix_norm, v_w_in, v_q_a_norm, v_w_q_up, v_kv_a_norm, v_w_kv_up, v_mla_q_norm, v_mla_k_norm, v_gla_w_gate2, v_gla_b_gate, v_gla_out_norm, v_w_out, v_mem_attn_norm, v_mem_norm, v_mem_w_q, v_mem_w_k, v_mem_w_v, v_mem_w_o, v_mem_q_norm, v_mem_k_norm, v_ffn2_norm, v_ffn2_w_gate, v_ffn2_w_up, v_ffn2_w_down):
    args = dict(locals())
    two_d = lambda a: a[0] if a.ndim == 3 else a
    W = {n: two_d(args[n]) for n in WEIGHTS}
    M1 = {n: two_d(args["m_" + n]) for n in WEIGHTS}
    V2 = {n: two_d(args["v_" + n]) for n in WEIGHTS}
    xs, mems, tgt = x[0], mem[0], loss_target[0]
    S, D = xs.shape
    chip = 2 * lax.axis_index("x") + lax.axis_index("y")

    shard16 = {n: W[n].astype(BF16) for n in BIG + ["gla_w_gate2"]}
    full = {}

    def stage1(names):
        return gather_stage1([shard16[n] for n in names], [n != "gla_w_gate2" for n in names])

    def stage2(names, slots):
        return gather_stage2(slots, [shard16[n] for n in names], [n != "gla_w_gate2" for n in names])

    def finish(names, slots):
        for n, s in zip(names, slots):
            s = s.reshape((N_CHIPS,) + shard16[n].shape)
            if n in SLOT_WEIGHTS:
                full[n] = s
            else:
                full[n] = _slots_to_cols(s) if n in COL_SHARDED else s.reshape(-1, s.shape[2])

    finish(FFN1, run_comm(stage2(FFN1, run_comm(stage1(FFN1), name="gather_ffn1")), name="pass_ffn1"))
    q_rank, kv_rank = W["w_q_up"].shape[0], W["w_kv_up"].shape[0]
    lay = InLayout(q_rank, kv_rank)
    off = lay.off
    cosf, sinf, rot = _rope_tables(positions[0])
    tri = jnp.asarray(np.tril(np.ones((CHUNK, CHUNK), np.float32)))
    gqn = W["mla_q_norm"][:, :MLA_NOPE]
    gqr = _pad_cols(W["mla_q_norm"][:, MLA_NOPE:], LANE)
    gkn = W["mla_k_norm"][:, :MLA_NOPE]
    gkr = _pad_cols(W["mla_k_norm"][:, MLA_NOPE:], LANE)
    HP = MLA_HEAD_PAD
    mla_scale = MLA_QK ** -0.5
    mem_scale = MEM_HEAD_DIM ** -0.5
    mla_w = MLA_HEADS * MLA_V
    gla_w = GLA_HEADS * GLA_DV
    mem_w = MEM_HEADS * MEM_HEAD_DIM

    n1 = row_fwd(rms_fn, [V(xs)], [W["ffn1_norm"]], [(D, BF16)], [(0, 0, 0)], name="ffn1_norm")[0]
    (gate1, up1, act1), mid_s1 = ffn_up(n1, full["ffn1_w_gate"], full["ffn1_w_up"], name="ffn1_up", comm=stage1(MID))
    x1, mid_s2 = mm([(act1, full["ffn1_w_down"])], "nn", F32, alpha=0.5, res=xs, name="ffn1_down",
                    comm=stage2(MID, mid_s1))
    ffn1_saved = (n1, gate1, up1, act1)
    finish(MID, mid_s2)
    w_in_p = lay.pad_weight(full["w_in"])
    w_q_up_p = _pad_q_up(full["w_q_up"])
    w_gate2_p = jnp.pad(full["gla_w_gate2"], ((0, LANE - GLA_GATE_RANK), (0, 0)))
    h = row_fwd(rms_fn, [V(x1)], [W["mix_norm"]], [(D, BF16)], [(0, 0, 0)], name="mix_norm")[0]
    z, f2_gate = mm([(h, w_in_p)], "nn", F32, name="in_proj", comm=stage1(FFN2[:1]))
    qa = row_fwd(rms_fn, [V(z, off["zq"], q_rank)], [W["q_a_norm"]], [(q_rank, BF16)], [(0, 0, 0)], name="q_a_norm")[0]
    kva = row_fwd(rms_fn, [V(z, off["zkv"], kv_rank)], [W["kv_a_norm"]], [(kv_rank, BF16)], [(0, 0, 0)], name="kv_a_norm")[0]
    qraw = mm([(qa, w_q_up_p)], "nn", F32, name="q_up")
    kvraw = mm([(kva, full["w_kv_up"])], "nn", F32, name="kv_up")
    tabs = [V(cosf, diff=False), V(sinf, diff=False)]
    q_rows = [V(qraw, 0, LANE, HP), V(qraw, LANE, LANE, HP)] + tabs
    k_rows = [V(kvraw, 0, LANE, HP), V(z, off["zkr"], LANE, 0)] + tabs
    qh = row_fwd(qk_prep_fn, q_rows, [gqn, gqr, rot], [(MLA_HEADS * HP, BF16)], [(0, 0, HP), (0, LANE, HP)],
                 heads=MLA_HEADS, name="q_prep")[0]
    kh = row_fwd(qk_prep_fn, k_rows, [gkn, gkr, rot], [(MLA_HEADS * HP, BF16)], [(0, 0, HP), (0, LANE, HP)],
                 heads=MLA_HEADS, name="k_prep")[0]
    mla_kw = dict(heads=MLA_HEADS, dk=HP, dv=MLA_V, v_off=1, v_hs=2, scale=mla_scale, causal=True)
    o_mla, f2_up = attn_fwd(qh, kh, kvraw, name="mla_attn", comm=stage1(FFN2[1:2]), **mla_kw)

    zg = z[:, off["zg"]:off["zg"] + LANE]
    pre = mm([(zg, w_gate2_p)], "nn", F32, name="gla_gate")
    la = row_fwd(gate_fn, [V(pre)], [W["gla_b_gate"]], [(pre.shape[1], F32)], [(0, 0, 0)], name="gla_log_decay")[0]
    gla_kw = dict(q_off=off["gq"], k_off=off["gk"], v_off=off["gv"])
    (o_raw, states), f2_down = gla_fwd(z, la, tri, name="gla_scan", comm=stage1(FFN2[2:]), **gla_kw)
    gla_rows = [V(o_raw, 0, GLA_DV, GLA_DV), V(z, off["zr"], GLA_DV, GLA_DV)]
    o_gla = row_fwd(gla_out_fn, gla_rows, [W["gla_out_norm"]], [(gla_w, BF16)], [(0, 0, GLA_DV)], heads=GLA_HEADS,
                    name="gla_out")[0]
    o_cat = jnp.concatenate([o_mla, o_gla], axis=1)
    ffn2_s1 = f2_gate + f2_up + f2_down
    x2, ffn2_s2 = mm([(o_cat, full["w_out"])], "nn", F32, res=x1, name="out_proj", comm=stage2(FFN2, ffn2_s1))
    finish(FFN2, ffn2_s2)

    hm = row_fwd(rms_fn, [V(x2)], [W["mem_attn_norm"]], [(D, BF16)], [(0, 0, 0)], name="mem_attn_norm")[0]
    mn = row_fwd(rms_fn, [V(mems)], [W["mem_norm"]], [(D, BF16)], [(0, 0, 0)], name="mem_norm")[0]
    qm_raw = mm([(hm, full["mem_w_q"])], "nn", F32, name="mem_q")
    km_raw = mm([(mn, full["mem_w_k"])], "nn", F32, name="mem_k")
    vm = mm([(mn, full["mem_w_v"])], "nn", F32, name="mem_v")
    hd = MEM_HEAD_DIM
    qm = row_fwd(rms_fn, [V(qm_raw, 0, hd, hd)], [W["mem_q_norm"]], [(mem_w, BF16)], [(0, 0, hd)], heads=MEM_HEADS,
                 name="mem_q_norm")[0]
    km = row_fwd(rms_fn, [V(km_raw, 0, hd, hd)], [W["mem_k_norm"]], [(mem_w, BF16)], [(0, 0, hd)], heads=MEM_HEADS,
                 name="mem_k_norm")[0]
    mem_kw = dict(heads=MEM_HEADS, dk=hd, dv=hd, v_off=0, v_hs=1, scale=mem_scale, causal=False)
    om = attn_fwd(qm, km, vm, name="mem_attn", **mem_kw)
    x3 = mm([(om, full["mem_w_o"])], "nn", F32, res=x2, name="mem_o")

    n2 = row_fwd(rms_fn, [V(x3)], [W["ffn2_norm"]], [(D, BF16)], [(0, 0, 0)], name="ffn2_norm")[0]
    gate2, up2, act2 = ffn_up(n2, full["ffn2_w_gate"], full["ffn2_w_up"], name="ffn2_up")
    y = mm([(act2, full["ffn2_w_down"])], "nn", F32, alpha=0.5, res=x3, name="ffn2_down")
    dy, loss_part = loss_head(y, tgt, name="loss_head")
    loss = lax.psum(loss_part[0, 0], ("x", "y", "c"))

    G, chip_sum, reduced = {}, {}, {}

    def to_halves(n):
        g = G[n]
        if n in SLOT_WEIGHTS:
            s = g
        else:
            s = _cols_to_slots(g) if n in COL_SHARDED else g.reshape(N_CHIPS, g.shape[0] // N_CHIPS, g.shape[1])
        return s.reshape(N_CHIPS, 2, s.shape[1] // 2, s.shape[2])

    def add2(names, halves, got):
        for n, a, b in zip(names, halves, got):
            chip_sum[n] = add_own_half(a, b, BF16, name=f"rs_add2_{n}")

    def add4_join(names, parts):
        total = [sum_chip_parts(chip_sum[n], p, name=f"rs_add4_{n}") for n, p in zip(names, parts)]
        for n, b in zip(names, run_comm(join_halves(total), name=f"rs_join_{names[0]}")):
            reduced[n] = b.reshape(W[n].shape)

    def ffn_backward(dout, xin, tag, saved, dact_comm=None, after_dact=None, after_dwd=None):
        n_, gate, up, act = saved
        nd, ng, nu = f"{tag}_w_down", f"{tag}_w_gate", f"{tag}_w_up"
        (dgate, dup), got0 = _with(ffn_dact(dout, full[nd], gate, up, 0.5, name=f"{tag}_dact", comm=dact_comm), dact_comm)
        dwd_comm = after_dact(got0) if after_dact else None
        G[nd], got1 = _with(mm([(act, dout)], "tn", F32, alpha=0.5, name=f"{tag}_dwd", comm=dwd_comm), dwd_comm)
        if after_dwd:
            after_dwd(got1)
        hd_ = to_halves(nd)
        G[ng], got_d = mm([(n_, dgate)], "tn", F32, name=f"{tag}_dwg", out_slots=True, tn=1408,
                          comm=swap_halves([hd_]))
        add2([nd], [hd_], got_d)
        hg = to_halves(ng)
        G[nu], (parts_d, got_g) = mm([(n_, dup)], "tn", F32, name=f"{tag}_dwu", out_slots=True, tn=1408,
                                     comm=merge_comms(exchange_chips([chip_sum[nd]]), swap_halves([hg])))
        add2([ng], [hg], [got_g])
        hu = to_halves(nu)
        dn, (parts_g, got_u) = mm([(dgate, full[ng]), (dup, full[nu])], "nt", F32, name=f"{tag}_dn", b_slots=True,
                                  tn=1024, tk=1408,
                                  comm=merge_comms(exchange_chips([chip_sum[ng]]), swap_halves([hu])))
        add2([nu], [hu], [got_u])
        dx, G[f"{tag}_norm"] = row_bwd(rms_fn, [V(xin)], [W[f"{tag}_norm"]], [V(dn)], const_diff=[True], res=dout,
                                       name=f"{tag}_dnorm")
        add4_join([nd, ng], [parts_d, parts_g])
        return dx, exchange_chips([chip_sum[nu]])

    g3, ffn2_up_xchg = ffn_backward(dy, x3, "ffn2", (n2, gate2, up2, act2))

    d_om = mm([(g3, full["mem_w_o"])], "nt", F32, name="mem_o_dx")
    G["mem_w_o"] = mm([(om, g3)], "tn", F32, name="mem_o_dw")
    dqm, dkm, dvm = attn_bwd(qm, km, vm, d_om, name="mem_attn_bwd", **mem_kw)
    dqm_raw, G["mem_q_norm"] = row_bwd(rms_fn, [V(qm_raw, 0, hd, hd)], [W["mem_q_norm"]], [V(dqm, 0, hd, hd)],
                                       const_diff=[True], heads=MEM_HEADS, row_dtype=BF16, name="mem_q_norm_bwd")
    dkm_raw, G["mem_k_norm"] = row_bwd(rms_fn, [V(km_raw, 0, hd, hd)], [W["mem_k_norm"]], [V(dkm, 0, hd, hd)],
                                       const_diff=[True], heads=MEM_HEADS, row_dtype=BF16, name="mem_k_norm_bwd")
    dhm = mm([(dqm_raw, full["mem_w_q"])], "nt", F32, name="mem_q_dx")
    G["mem_w_q"] = mm([(hm, dqm_raw)], "tn", F32, name="mem_q_dw")
    dmn = mm([(dkm_raw, full["mem_w_k"]), (dvm, full["mem_w_v"])], "nt", F32, name="mem_kv_dx")
    G["mem_w_k"] = mm([(mn, dkm_raw)], "tn", F32, name="mem_k_dw")
    G["mem_w_v"] = mm([(mn, dvm)], "tn", F32, name="mem_v_dw")
    _, G["mem_norm"] = row_bwd(rms_fn, [V(mems)], [W["mem_norm"]], [V(dmn)], const_diff=[True], row_dtype=BF16,
                               name="mem_norm_bwd")
    g2, G["mem_attn_norm"] = row_bwd(rms_fn, [V(x2)], [W["mem_attn_norm"]], [V(dhm)], const_diff=[True], res=g3,
                                     name="mem_attn_norm_bwd")

    d_ocat = mm([(g2, full["w_out"])], "nt", F32, name="out_proj_dx")
    G["w_out"] = mm([(o_cat, g2)], "tn", F32, name="out_proj_dw")

    d_oraw, d_zr, G["gla_out_norm"] = row_bwd(gla_out_fn, gla_rows, [W["gla_out_norm"]],
                                              [V(d_ocat, mla_w, GLA_DV, GLA_DV)], const_diff=[True], heads=GLA_HEADS,
                                              name="gla_out_bwd")
    d_gq, d_gk, d_gv, d_la = gla_bwd(z, la, tri, tri.T, states, d_oraw, name="gla_scan_bwd", **gla_kw)
    d_pre, G["gla_b_gate"] = row_bwd(gate_fn, [V(pre)], [W["gla_b_gate"]], [V(d_la)], const_diff=[True], row_dtype=BF16,
                                     name="gla_log_decay_bwd")
    d_zg = mm([(d_pre, w_gate2_p)], "nt", BF16, name="gla_gate_dx")
    G["gla_w_gate2"] = mm([(zg, d_pre)], "tn", F32, name="gla_gate_dw")[:GLA_GATE_RANK]

    (d_qh, d_kh, d_v), ffn2_up_parts = attn_bwd(qh, kh, kvraw, d_ocat, name="mla_attn_bwd", comm=ffn2_up_xchg, **mla_kw)
    add4_join(["ffn2_w_up"], ffn2_up_parts)
    cq = [V(d_qh, 0, LANE, HP), V(d_qh, LANE, LANE, HP)]
    ck = [V(d_kh, 0, LANE, HP), V(d_kh, LANE, LANE, HP)]
    d_qn, d_qr, d_gqn, d_gqr = row_bwd(qk_prep_fn, q_rows, [gqn, gqr, rot], cq, const_diff=[True, True, False],
                                       heads=MLA_HEADS, row_dtype=BF16, name="q_prep_bwd")
    d_kn, d_zkr, d_gkn, d_gkr = row_bwd(qk_prep_fn, k_rows, [gkn, gkr, rot], ck, const_diff=[True, True, False],
                                        heads=MLA_HEADS, row_dtype=BF16, name="k_prep_bwd")
    G["mla_q_norm"] = jnp.concatenate([d_gqn, d_gqr[:, :MLA_ROPE]], axis=1)
    G["mla_k_norm"] = jnp.concatenate([d_gkn, d_gkr[:, :MLA_ROPE]], axis=1)
    d_qraw = _interleave(d_qn, d_qr, MLA_HEADS)
    d_kvraw = _interleave(d_kn, d_v.astype(BF16), MLA_HEADS)
    d_qa = mm([(d_qraw, w_q_up_p)], "nt", F32, name="q_up_dx")
    G["w_q_up"] = _unpad_q_up(mm([(qa, d_qraw)], "tn", F32, name="q_up_dw"))
    d_kva = mm([(d_kvraw, full["w_kv_up"])], "nt", F32, name="kv_up_dx")
    G["w_kv_up"] = mm([(kva, d_kvraw)], "tn", F32, name="kv_up_dw")
    d_zq, G["q_a_norm"] = row_bwd(rms_fn, [V(z, off["zq"], q_rank)], [W["q_a_norm"]], [V(d_qa)], const_diff=[True],
                                  row_dtype=BF16, name="q_a_norm_bwd")
    d_zkv, G["kv_a_norm"] = row_bwd(rms_fn, [V(z, off["zkv"], kv_rank)], [W["kv_a_norm"]], [V(d_kva)], const_diff=[True],
                                    row_dtype=BF16, name="kv_a_norm_bwd")

    seg = {"gv": d_gv, "zr": d_zr, "zq": d_zq, "gq": d_gq, "gk": d_gk, "zkv": d_zkv, "zkr": d_zkr, "zg": d_zg}
    dz = jnp.concatenate([_pad_cols(seg[n].astype(BF16), lay.size[n]) for n in lay.order], axis=1)
    dh = mm([(dz, w_in_p)], "nt", F32, name="in_proj_dx")
    G["w_in"] = lay.unpad_grad(mm([(h, dz)], "tn", F32, name="in_proj_dw"))
    g1, G["mix_norm"] = row_bwd(rms_fn, [V(x1)], [W["mix_norm"]], [V(dh)], const_diff=[True], res=g2, name="mix_norm_bwd")

    mid_rs = [n for n in MID if n != "gla_w_gate2"]
    mid_halves = [to_halves(n) for n in mid_rs]

    def mid_sums(got):
        add2(mid_rs, mid_halves, got)
        return exchange_chips([chip_sum[n] for n in mid_rs])

    gx, ffn1_up_xchg = ffn_backward(g1, xs, "ffn1", ffn1_saved, dact_comm=swap_halves(mid_halves), after_dact=mid_sums,
                                    after_dwd=lambda parts: add4_join(mid_rs, parts))

    grad, delta, new_m, new_v = {}, {}, {}, {}

    def adam_group(names, tag, comm=None):
        res, extra = _with(adamw([(W[n], reduced[n], M1[n], V2[n]) for n in names], name=f"adamw_{tag}", comm=comm), comm)
        for n, (d_, m_, v_) in zip(names, res):
            grad[n], delta[n], new_m[n], new_v[n] = reduced[n], d_, m_, v_
        return extra

    add4_join(["ffn1_w_up"], adam_group(FFN2, "ffn2", comm=ffn1_up_xchg))
    adam_group(mid_rs, "mid")
    adam_group(FFN1, "ffn1")

    small_names = SMALL + ["gla_w_gate2"]
    small_sum = allreduce_small(_pack_small([G[n] for n in small_names]), name="allreduce_small")
    small_g = dict(zip(small_names, _unpack_small(small_sum, [G[n].shape for n in small_names])))
    shard_c = W["gla_w_gate2"].shape[1]
    grad["gla_w_gate2"] = lax.dynamic_slice_in_dim(small_g["gla_w_gate2"], chip * shard_c, shard_c, axis=1)
    pw = _pack_small([W[n] for n in SMALL] + [W["gla_w_gate2"]])
    pg = _pack_small([small_g[n] for n in SMALL] + [grad["gla_w_gate2"]])
    pm = _pack_small([M1[n] for n in SMALL] + [M1["gla_w_gate2"]])
    pv = _pack_small([V2[n] for n in SMALL] + [V2["gla_w_gate2"]])
    (pd, pnm, pnv), = adamw([(pw, pg, pm, pv)], name="adamw_small")
    shapes = [W[n].shape for n in small_names]
    for n, d_, m_, v_ in zip(small_names, _unpack_small(pd, shapes), _unpack_small(pnm, shapes), _unpack_small(pnv, shapes)):
        delta[n], new_m[n], new_v[n] = d_, m_, v_
        if n != "gla_w_gate2":
            grad[n] = small_g[n]

    lead = lambda d: [d[n].reshape(args[n].shape) for n in WEIGHTS]
    return (loss, gx[None], *lead(grad), *lead(delta), *lead(new_m), *lead(new_v))
```

```python
import functools
import math

import numpy as np
import jax
import jax.numpy as jnp
from jax import lax
from jax.experimental import pallas as pl
from jax.experimental.pallas import tpu as pltpu

F32 = jnp.float32
BF16 = jnp.bfloat16
MXU_DTYPE = jnp.bfloat16
MESH = pl.DeviceIdType.MESH
ANY = pl.BlockSpec(memory_space=pl.ANY)

LANE = 128
EPS = 1e-6
CHUNK = 64
MLA_HEADS = 8
MLA_NOPE = 128
MLA_ROPE = 64
MLA_QK = MLA_NOPE + MLA_ROPE
MLA_V = 128
MLA_HEAD_PAD = 2 * LANE
ROPE_THETA = 10000.0
GLA_HEADS = 4
GLA_DK = 128
GLA_DV = 256
GLA_GATE_RANK = 16
GLA_TAU = 16.0
MEM_HEADS = 4
MEM_HEAD_DIM = 128
N_CHIPS = 4
N_DEV = 8

ADAM_LR = 0.001
ADAM_B1 = 0.9
ADAM_B2 = 0.999
ADAM_EPS = 1e-08
ADAM_WD = 0.01
ADAM_STEP = 10

VMEM_LIMIT = 56 * 1024 * 1024


def _cparams(sem=None):
    if sem is None:
        return pltpu.CompilerParams(vmem_limit_bytes=VMEM_LIMIT)
    return pltpu.CompilerParams(dimension_semantics=sem, vmem_limit_bytes=VMEM_LIMIT)


def _tile(dim, pref, unit=LANE):
    if dim <= pref:
        return dim
    t = (pref // unit) * unit
    while t > unit and dim % t:
        t -= unit
    assert dim % t == 0, (dim, pref, unit)
    return t


class Comm:
    def __init__(self, ins, out_shapes, nsem, start, wait, aliases=None):
        self.ins, self.out_shapes, self.nsem = list(ins), list(out_shapes), nsem
        self.start, self.wait, self.aliases = start, wait, dict(aliases or {})


def merge_comms(a, b):
    ai, ao = len(a.ins), len(a.out_shapes)

    def start(ins, outs, send, recv, base):
        a.start(ins[:ai], outs[:ao], send, recv, base)
        b.start(ins[ai:], outs[ao:], send, recv, base + a.nsem)

    def wait(ins, outs, send, recv, base):
        a.wait(ins[:ai], outs[:ao], send, recv, base)
        b.wait(ins[ai:], outs[ao:], send, recv, base + a.nsem)

    aliases = dict(a.aliases)
    aliases.update({ai + i: ao + o for i, o in b.aliases.items()})
    return Comm(a.ins + b.ins, a.out_shapes + b.out_shapes, a.nsem + b.nsem, start, wait, aliases)


def run_comm(comm, *, name):
    ni, no = len(comm.ins), len(comm.out_shapes)

    def body(*refs):
        ins, outs = refs[:ni], refs[ni:ni + no]
        send, recv = refs[ni + no:]
        comm.start(ins, outs, send, recv, 0)
        comm.wait(ins, outs, send, recv, 0)

    return pl.pallas_call(
        body, name=name, in_specs=[ANY] * ni, out_specs=[ANY] * no, out_shape=comm.out_shapes,
        input_output_aliases=comm.aliases,
        scratch_shapes=[pltpu.SemaphoreType.DMA((comm.nsem,)), pltpu.SemaphoreType.DMA((comm.nsem,))])(*comm.ins)


def _pcall(body, ops, *, name, grid, in_specs, out_specs, out_shape, sem, scratch_shapes=(), comm=None):
    if comm is None:
        return pl.pallas_call(body, name=name, grid=grid, in_specs=in_specs, out_specs=out_specs, out_shape=out_shape,
                              scratch_shapes=list(scratch_shapes), compiler_params=_cparams(sem))(*ops)
    multi = isinstance(out_shape, (list, tuple))
    k_out_shape = list(out_shape) if multi else [out_shape]
    k_out_specs = list(out_specs) if multi else [out_specs]
    nki, nko, nks = len(ops), len(k_out_shape), len(scratch_shapes)
    nci, nco = len(comm.ins), len(comm.out_shapes)

    def wrapped(*refs):
        p = 0
        k_in = refs[p:p + nki]; p += nki
        c_in = refs[p:p + nci]; p += nci
        k_out = refs[p:p + nko]; p += nko
        c_out = refs[p:p + nco]; p += nco
        k_scr = refs[p:p + nks]; p += nks
        send, recv = refs[p:]
        first = pl.program_id(0) == 0
        last = pl.program_id(0) == grid[0] - 1
        for a in range(1, len(grid)):
            first = jnp.logical_and(first, pl.program_id(a) == 0)
            last = jnp.logical_and(last, pl.program_id(a) == grid[a] - 1)

        @pl.when(first)
        def _():
            comm.start(c_in, c_out, send, recv, 0)

        body(*k_in, *k_out, *k_scr)

        @pl.when(last)
        def _():
            comm.wait(c_in, c_out, send, recv, 0)

    res = pl.pallas_call(
        wrapped, name=name, grid=grid, in_specs=list(in_specs) + [ANY] * nci, out_specs=k_out_specs + [ANY] * nco,
        out_shape=k_out_shape + comm.out_shapes,
        input_output_aliases={nki + i: nko + o for i, o in comm.aliases.items()},
        scratch_shapes=list(scratch_shapes) + [pltpu.SemaphoreType.DMA((comm.nsem,)), pltpu.SemaphoreType.DMA((comm.nsem,))],
        compiler_params=_cparams(("arbitrary",) * len(grid)))(*ops, *comm.ins)
    k_res = list(res[:nko]) if multi else res[0]
    return k_res, list(res[nko:])


_DIMS = {"nn": (((1,), (0,)), ((), ())), "nt": (((1,), (1,)), ((), ())), "tn": (((0,), (0,)), ((), ()))}


def mm(pairs, mode, out_dtype, *, name, alpha=1.0, res=None, tm=1024, tn=1024, tk=512, b_slots=False, out_slots=False,
       comm=None):
    a0, b0 = pairs[0]
    if b_slots:
        b_rows, b_cols = b0.shape[1], N_CHIPS * b0.shape[2]
    else:
        b_rows, b_cols = b0.shape
    (M, K) = a0.shape[::-1] if mode == "tn" else a0.shape
    N = b_rows if mode == "nt" else b_cols
    shard = (b_cols if b_slots else N) // N_CHIPS
    tm = _tile(M, tm)
    tn = _tile(shard if (out_slots or (b_slots and mode != "nt")) else N, tn)
    tk = _tile(shard if (b_slots and mode == "nt") else K, tk)
    nk = K // tk
    npairs = len(pairs)
    dims = _DIMS[mode]
    if mode == "tn":
        a_spec = pl.BlockSpec((tk, tm), lambda i, j, k: (k, i))
    else:
        a_spec = pl.BlockSpec((tm, tk), lambda i, j, k: (i, k))
    per = shard // (tk if mode == "nt" else tn)
    if mode == "nt":
        b_spec = (pl.BlockSpec((None, tn, tk), lambda i, j, k: (k // per, j, k % per)) if b_slots else
                  pl.BlockSpec((tn, tk), lambda i, j, k: (j, k)))
    else:
        b_spec = (pl.BlockSpec((None, tk, tn), lambda i, j, k: (j // per, k, j % per)) if b_slots else
                  pl.BlockSpec((tk, tn), lambda i, j, k: (k, j)))
    if out_slots:
        assert res is None and mode != "nt"
        o_spec = pl.BlockSpec((None, tm, tn), lambda i, j, k: (j // per, i, j % per))
        out_sds = jax.ShapeDtypeStruct((N_CHIPS, M, shard), out_dtype)
    else:
        o_spec = pl.BlockSpec((tm, tn), lambda i, j, k: (i, j))
        out_sds = jax.ShapeDtypeStruct((M, N), out_dtype)
    has_res = res is not None

    def body(*refs):
        ab = refs[:2 * npairs]
        res_ref = refs[2 * npairs] if has_res else None
        o_ref, acc = refs[-2], refs[-1]
        k = pl.program_id(2)

        @pl.when(k == 0)
        def _():
            acc[...] = jnp.zeros_like(acc)

        for p in range(npairs):
            acc[...] += lax.dot_general(ab[2 * p][...].astype(MXU_DTYPE), ab[2 * p + 1][...].astype(MXU_DTYPE),
                                        dims, preferred_element_type=F32)

        @pl.when(k == nk - 1)
        def _():
            r = acc[...]
            if alpha != 1.0:
                r = r * alpha
            if has_res:
                r = res_ref[...].astype(F32) + r
            o_ref[...] = r.astype(out_dtype)

    ops, specs = [], []
    for a, b in pairs:
        ops += [a, b]
        specs += [a_spec, b_spec]
    if has_res:
        ops.append(res)
        specs.append(o_spec)
    return _pcall(body, ops, name=name, grid=(M // tm, N // tn, nk), in_specs=specs, out_specs=o_spec,
                  out_shape=out_sds, scratch_shapes=[pltpu.VMEM((tm, tn), F32)],
                  sem=("parallel", "parallel", "arbitrary"), comm=comm)


def _sigmoid(x):
    return 1.0 / (1.0 + jnp.exp(-x))


def ffn_up(n, wg, wu, *, name, tm=1024, tn=1408, tk=512, comm=None):
    M, K = n.shape
    shard = wg.shape[2]
    N = N_CHIPS * shard
    tm, tn, tk = _tile(M, tm), _tile(shard, tn), _tile(K, tk)
    nk = K // tk
    per = shard // tn
    w_spec = pl.BlockSpec((None, tk, tn), lambda i, j, k: (j // per, k, j % per))

    def body(n_ref, wg_ref, wu_ref, g_ref, u_ref, a_ref, accg, accu):
        k = pl.program_id(2)

        @pl.when(k == 0)
        def _():
            accg[...] = jnp.zeros_like(accg)
            accu[...] = jnp.zeros_like(accu)

        nv = n_ref[...].astype(MXU_DTYPE)
        accg[...] += jnp.dot(nv, wg_ref[...].astype(MXU_DTYPE), preferred_element_type=F32)
        accu[...] += jnp.dot(nv, wu_ref[...].astype(MXU_DTYPE), preferred_element_type=F32)

        @pl.when(k == nk - 1)
        def _():
            g, u = accg[...], accu[...]
            g_ref[...] = g.astype(g_ref.dtype)
            u_ref[...] = u.astype(u_ref.dtype)
            a_ref[...] = (g * _sigmoid(g) * u).astype(a_ref.dtype)

    o_spec = pl.BlockSpec((tm, tn), lambda i, j, k: (i, j))
    sds = jax.ShapeDtypeStruct((M, N), BF16)
    return _pcall(
        body, [n, wg, wu], name=name, grid=(M // tm, N // tn, nk),
        in_specs=[pl.BlockSpec((tm, tk), lambda i, j, k: (i, k)), w_spec, w_spec],
        out_specs=[o_spec, o_spec, o_spec], out_shape=[sds, sds, sds],
        scratch_shapes=[pltpu.VMEM((tm, tn), F32), pltpu.VMEM((tm, tn), F32)],
        sem=("parallel", "parallel", "arbitrary"), comm=comm)


def ffn_dact(dy, wd, gate, up, alpha, *, name, tm=1024, tn=1408, tk=512, comm=None):
    M, K = dy.shape
    N = wd.shape[0]
    tm, tn, tk = _tile(M, tm), _tile(N, tn), _tile(K, tk)
    nk = K // tk

    def body(dy_ref, wd_ref, g_ref, u_ref, dg_ref, du_ref, acc):
        k = pl.program_id(2)

        @pl.when(k == 0)
        def _():
            acc[...] = jnp.zeros_like(acc)

        acc[...] += lax.dot_general(dy_ref[...].astype(MXU_DTYPE), wd_ref[...].astype(MXU_DTYPE), _DIMS["nt"],
                                    preferred_element_type=F32)

        @pl.when(k == nk - 1)
        def _():
            da = acc[...] * alpha
            g = g_ref[...].astype(F32)
            u = u_ref[...].astype(F32)
            s = _sigmoid(g)
            du_ref[...] = (da * (g * s)).astype(du_ref.dtype)
            dg_ref[...] = (da * u * (s * (1.0 + g * (1.0 - s)))).astype(dg_ref.dtype)

    o_spec = pl.BlockSpec((tm, tn), lambda i, j, k: (i, j))
    sds = jax.ShapeDtypeStruct((M, N), BF16)
    return _pcall(
        body, [dy, wd, gate, up], name=name, grid=(M // tm, N // tn, nk),
        in_specs=[pl.BlockSpec((tm, tk), lambda i, j, k: (i, k)), pl.BlockSpec((tn, tk), lambda i, j, k: (j, k)),
                  o_spec, o_spec],
        out_specs=[o_spec, o_spec], out_shape=[sds, sds], scratch_shapes=[pltpu.VMEM((tm, tn), F32)],
        sem=("parallel", "parallel", "arbitrary"), comm=comm)


class V:
    def __init__(self, arr, off=0, w=None, hs=0, diff=True):
        self.arr, self.off, self.hs, self.diff = arr, off, hs, diff
        self.w = arr.shape[1] - off if w is None else w

    def window(self, heads, tr):
        width = self.arr.shape[1]
        ext = (heads - 1) * self.hs + self.w
        ww = LANE
        while ww < width:
            if ww >= ext and self.off // ww == (self.off + ext - 1) // ww and width % ww == 0:
                break
            ww *= 2
        else:
            ww = width
        blk = self.off // ww
        return pl.BlockSpec((tr, ww), lambda i, blk=blk: (i, blk)), self.off - blk * ww


def _const_spec(c):
    return pl.BlockSpec(c.shape, lambda i: (0, 0))


def row_fwd(fn, rows, consts, outs, out_map, *, heads=1, tr=256, name):
    S = rows[0].arr.shape[0]
    tr = _tile(S, tr, 8)
    wins = [v.window(heads, tr) for v in rows]
    nr, nc = len(rows), len(consts)

    def body(*refs):
        row_refs, const_refs, out_refs = refs[:nr], refs[nr:nr + nc], refs[nr + nc:]
        cv = [c[...].astype(F32) for c in const_refs]
        for h in range(heads):
            rv = []
            for v, (_, io), r in zip(rows, wins, row_refs):
                lo = io + h * v.hs
                rv.append(r[:, lo:lo + v.w].astype(F32))
            res = fn(*rv, *cv)
            for (ai, off, hs), o in zip(out_map, res):
                lo = off + h * hs
                out_refs[ai][:, lo:lo + o.shape[1]] = o.astype(out_refs[ai].dtype)

    return pl.pallas_call(
        body, name=name, grid=(S // tr,),
        in_specs=[w[0] for w in wins] + [_const_spec(c) for c in consts],
        out_specs=[pl.BlockSpec((tr, w), lambda i: (i, 0)) for w, _ in outs],
        out_shape=[jax.ShapeDtypeStruct((S, w), d) for w, d in outs],
        compiler_params=_cparams(("parallel",)))(*[v.arr for v in rows], *consts)


def row_bwd(fn, rows, consts, cots, *, const_diff, heads=1, tr=256, res=None, row_dtype=F32, name):
    S = rows[0].arr.shape[0]
    tr = _tile(S, tr, 8)
    nr, nc, nct = len(rows), len(consts), len(cots)
    wins = [v.window(heads, tr) for v in rows]
    cwins = [v.window(heads, tr) for v in cots]
    drows = [k for k, v in enumerate(rows) if v.diff]
    dconsts = [k for k in range(nc) if const_diff[k]]
    has_res = res is not None
    ngrid = S // tr

    def body(*refs):
        row_refs = refs[:nr]
        const_refs = refs[nr:nr + nc]
        cot_refs = refs[nr + nc:nr + nc + nct]
        p = nr + nc + nct
        res_ref = refs[p] if has_res else None
        p += int(has_res)
        grow_refs = refs[p:p + len(drows)]
        gconst_refs = refs[p + len(drows):]
        i = pl.program_id(0)
        cv = [c[...].astype(F32) for c in const_refs]
        shared = [None] * len(drows)
        gc_sum = [None] * len(dconsts)
        for h in range(heads):
            rv = []
            for v, (_, io), r in zip(rows, wins, row_refs):
                lo = io + h * v.hs
                rv.append(r[:, lo:lo + v.w].astype(F32))
            ct = []
            for v, (_, io), r in zip(cots, cwins, cot_refs):
                lo = io + h * v.hs
                ct.append(r[:, lo:lo + v.w].astype(F32))

            def closed(*d):
                rr, cc = list(rv), list(cv)
                for k, val in zip(drows, d[:len(drows)]):
                    rr[k] = val
                for k, val in zip(dconsts, d[len(drows):]):
                    cc[k] = val
                return tuple(fn(*rr, *cc))

            _, vjp = jax.vjp(closed, *[rv[k] for k in drows], *[cv[k] for k in dconsts])
            grads = vjp(tuple(ct))
            for n, k in enumerate(drows):
                g = grads[n]
                if rows[k].hs == 0 and heads > 1:
                    shared[n] = g if shared[n] is None else shared[n] + g
                else:
                    if n == 0 and has_res:
                        g = g + res_ref[:, h * rows[k].w:(h + 1) * rows[k].w].astype(F32)
                    grow_refs[n][:, h * rows[k].w:(h + 1) * rows[k].w] = g.astype(row_dtype)
            for n in range(len(dconsts)):
                g = grads[len(drows) + n]
                gc_sum[n] = g if gc_sum[n] is None else gc_sum[n] + g
        for n, k in enumerate(drows):
            if shared[n] is not None:
                g = shared[n]
                if n == 0 and has_res:
                    g = g + res_ref[...].astype(F32)
                grow_refs[n][...] = g.astype(row_dtype)

        @pl.when(i == 0)
        def _():
            for n in range(len(dconsts)):
                gconst_refs[n][...] = gc_sum[n]

        @pl.when(i > 0)
        def _():
            for n in range(len(dconsts)):
                gconst_refs[n][...] += gc_sum[n]

    gw = [rows[k].w * (heads if rows[k].hs else 1) for k in drows]
    in_specs = [w[0] for w in wins] + [_const_spec(c) for c in consts] + [w[0] for w in cwins]
    ops = [v.arr for v in rows] + list(consts) + [v.arr for v in cots]
    if has_res:
        in_specs.append(pl.BlockSpec((tr, gw[0]), lambda i: (i, 0)))
        ops.append(res)
    out_specs = [pl.BlockSpec((tr, w), lambda i: (i, 0)) for w in gw]
    out_shape = [jax.ShapeDtypeStruct((S, w), row_dtype) for w in gw]
    for k in dconsts:
        out_specs.append(_const_spec(consts[k]))
        out_shape.append(jax.ShapeDtypeStruct(consts[k].shape, F32))
    del ngrid
    return pl.pallas_call(body, name=name, grid=(S // tr,), in_specs=in_specs, out_specs=out_specs,
                          out_shape=out_shape, compiler_params=_cparams(("arbitrary",)))(*ops)


def _rms(x, g, n=None):
    n = x.shape[-1] if n is None else n
    ms = jnp.sum(x * x, axis=-1, keepdims=True) * (1.0 / n)
    return x * lax.rsqrt(ms + EPS) * g


def rms_fn(x, g):
    return (_rms(x, g),)


def qk_prep_fn(nope, rope, cos, sin, gn, gr, rot):
    ms = (jnp.sum(nope * nope, axis=-1, keepdims=True) + jnp.sum(rope * rope, axis=-1, keepdims=True)) * (1.0 / MLA_QK)
    r = lax.rsqrt(ms + EPS)
    on = nope * r * gn
    orr = rope * r * gr
    turned = jnp.dot(orr, rot, precision=lax.Precision.HIGHEST, preferred_element_type=F32)
    return on, orr * cos + turned * sin


def gla_out_fn(o, zr, g):
    return (_rms(o, g) * (zr * _sigmoid(zr)),)


def gate_fn(pre, b):
    t = pre + b
    return ((jnp.minimum(t, 0.0) - jnp.log(1.0 + jnp.exp(-jnp.abs(t)))) * (1.0 / GLA_TAU),)


def _attn_probs(q_ref, k_ref, scale, q0, kext):
    s = lax.dot_general(q_ref[...].astype(MXU_DTYPE), k_ref[0:kext, :].astype(MXU_DTYPE), _DIMS["nt"],
                        preferred_element_type=F32) * scale
    if q0 is not None:
        qc = (q0 + lax.broadcasted_iota(jnp.int32, s.shape, 0)) // CHUNK
        kc = lax.broadcasted_iota(jnp.int32, s.shape, 1) // CHUNK
        s = jnp.where(kc <= qc, s, -1e30)
    m = jnp.max(s, axis=-1, keepdims=True)
    e = jnp.exp(s - m)
    return e / jnp.sum(e, axis=-1, keepdims=True)


def _per_query_block(one, causal, nq, tq, Sk):
    if not causal:
        one(None, Sk, None)
        return
    assert tq % CHUNK == 0
    for ib in range(nq):
        pl.when(pl.program_id(1) == ib)(functools.partial(one, ib * tq, min(Sk, (ib + 1) * tq), ib))


def attn_fwd(q, k, v, *, heads, dk, dv, v_off, v_hs, scale, causal, name, tq=256, comm=None):
    Sq, Sk = q.shape[0], k.shape[0]
    tq = _tile(Sq, tq, 8)

    def body(q_ref, k_ref, v_ref, o_ref):
        def one(q0, kext, ib):
            p = _attn_probs(q_ref, k_ref, scale, q0, kext)
            o_ref[...] = jnp.dot(p.astype(MXU_DTYPE), v_ref[0:kext, :].astype(MXU_DTYPE),
                                 preferred_element_type=F32).astype(o_ref.dtype)

        _per_query_block(one, causal, Sq // tq, tq, Sk)

    return _pcall(
        body, [q, k, v], name=name, grid=(heads, Sq // tq),
        in_specs=[pl.BlockSpec((tq, dk), lambda h, i: (i, h)), pl.BlockSpec((Sk, dk), lambda h, i: (0, h)),
                  pl.BlockSpec((Sk, dv), lambda h, i: (0, v_off + h * v_hs))],
        out_specs=pl.BlockSpec((tq, dv), lambda h, i: (i, h)),
        out_shape=jax.ShapeDtypeStruct((Sq, heads * dv), BF16), sem=("parallel", "parallel"), comm=comm)


def attn_bwd(q, k, v, do, *, heads, dk, dv, v_off, v_hs, scale, causal, name, tq=256, comm=None):
    Sq, Sk = q.shape[0], k.shape[0]
    tq = _tile(Sq, tq, 8)

    def body(q_ref, k_ref, v_ref, do_ref, dq_ref, dk_ref, dv_ref):
        @pl.when(pl.program_id(1) == 0)
        def _():
            dk_ref[...] = jnp.zeros_like(dk_ref)
            dv_ref[...] = jnp.zeros_like(dv_ref)

        def one(q0, kext, ib):
            p = _attn_probs(q_ref, k_ref, scale, q0, kext)
            dob = do_ref[...].astype(MXU_DTYPE)
            dp = lax.dot_general(dob, v_ref[0:kext, :].astype(MXU_DTYPE), _DIMS["nt"], preferred_element_type=F32)
            delta = jnp.sum(p * dp, axis=-1, keepdims=True)
            ds = (p * (dp - delta) * scale).astype(MXU_DTYPE)
            dq_ref[...] = jnp.dot(ds, k_ref[0:kext, :].astype(MXU_DTYPE), preferred_element_type=F32)
            dk_ref[0:kext, :] += lax.dot_general(ds, q_ref[...].astype(MXU_DTYPE), _DIMS["tn"],
                                                 preferred_element_type=F32)
            dv_ref[0:kext, :] += lax.dot_general(p.astype(MXU_DTYPE), dob, _DIMS["tn"], preferred_element_type=F32)

        _per_query_block(one, causal, Sq // tq, tq, Sk)

    return _pcall(
        body, [q, k, v, do], name=name, grid=(heads, Sq // tq),
        in_specs=[pl.BlockSpec((tq, dk), lambda h, i: (i, h)), pl.BlockSpec((Sk, dk), lambda h, i: (0, h)),
                  pl.BlockSpec((Sk, dv), lambda h, i: (0, v_off + h * v_hs)),
                  pl.BlockSpec((tq, dv), lambda h, i: (i, h))],
        out_specs=[pl.BlockSpec((tq, dk), lambda h, i: (i, h)), pl.BlockSpec((Sk, dk), lambda h, i: (0, h)),
                   pl.BlockSpec((Sk, dv), lambda h, i: (0, h))],
        out_shape=[jax.ShapeDtypeStruct((Sq, heads * dk), F32), jax.ShapeDtypeStruct((Sk, heads * dk), F32),
                   jax.ShapeDtypeStruct((Sk, heads * dv), F32)],
        sem=("parallel", "arbitrary"), comm=comm)


def _gla_chunk(k_ref, la_ref, tri_ref):
    g = la_ref[...].astype(F32)
    b = jnp.dot(tri_ref[...], g, precision=lax.Precision.HIGHEST, preferred_element_type=F32)
    b_end = jnp.sum(g, axis=0, keepdims=True)
    e = jnp.exp(b_end - b)
    return k_ref[...].astype(F32) * e, e, jnp.exp(b_end)


def gla_fwd(z, la, tri, *, q_off, k_off, v_off, name, comm=None):
    S = z.shape[0]
    nchunk = S // CHUNK
    H, DK, DV = GLA_HEADS, GLA_DK, GLA_DV
    qb, kb, vb = q_off // DK, k_off // DK, v_off // DV
    qscale = DK ** -0.5

    def body(q_ref, k_ref, v_ref, la_ref, tri_ref, o_ref, st_ref, state):
        c = pl.program_id(1)

        @pl.when(c == 0)
        def _():
            state[...] = jnp.zeros_like(state)

        kdec, _, decay = _gla_chunk(k_ref, la_ref, tri_ref)
        ut = lax.dot_general(v_ref[...].astype(MXU_DTYPE), kdec.astype(MXU_DTYPE), _DIMS["tn"],
                             preferred_element_type=F32)
        new = state[...] * decay + ut
        state[...] = new
        st_ref[...] = new
        qs = (q_ref[...].astype(F32) * qscale).astype(MXU_DTYPE)
        o_ref[...] = lax.dot_general(qs, new.astype(MXU_DTYPE), _DIMS["nt"], preferred_element_type=F32)

    return _pcall(
        body, [z, z, z, la, tri], name=name, grid=(H, nchunk),
        in_specs=[pl.BlockSpec((CHUNK, DK), lambda h, c: (c, qb + h)), pl.BlockSpec((CHUNK, DK), lambda h, c: (c, kb + h)),
                  pl.BlockSpec((CHUNK, DV), lambda h, c: (c, vb + h)), pl.BlockSpec((CHUNK, DK), lambda h, c: (c, h)),
                  pl.BlockSpec((CHUNK, CHUNK), lambda h, c: (0, 0))],
        out_specs=[pl.BlockSpec((CHUNK, DV), lambda h, c: (c, h)),
                   pl.BlockSpec((None, None, DV, DK), lambda h, c: (h, c, 0, 0))],
        out_shape=[jax.ShapeDtypeStruct((S, H * DV), F32), jax.ShapeDtypeStruct((H, nchunk, DV, DK), F32)],
        scratch_shapes=[pltpu.VMEM((DV, DK), F32)], sem=("parallel", "arbitrary"), comm=comm)


def gla_bwd(z, la, tri, trit, states, do, *, q_off, k_off, v_off, name, comm=None):
    S = z.shape[0]
    nchunk = S // CHUNK
    H, DK, DV = GLA_HEADS, GLA_DK, GLA_DV
    qb, kb, vb = q_off // DK, k_off // DK, v_off // DV
    qscale = DK ** -0.5
    last = nchunk - 1

    def body(q_ref, k_ref, v_ref, la_ref, tri_ref, trit_ref, st_ref, sp_ref, do_ref, dq_ref, dk_ref, dv_ref, dla_ref,
             dstate):
        c = pl.program_id(1)
        cc = last - c

        @pl.when(c == 0)
        def _():
            dstate[...] = jnp.zeros_like(dstate)

        kdec, e, decay = _gla_chunk(k_ref, la_ref, tri_ref)
        kf = k_ref[...].astype(F32)
        dob = do_ref[...].astype(MXU_DTYPE)
        stb = st_ref[...].astype(MXU_DTYPE)
        qs = (q_ref[...].astype(F32) * qscale).astype(MXU_DTYPE)
        dq_ref[...] = jnp.dot(dob, stb, preferred_element_type=F32) * qscale
        dst = dstate[...] + lax.dot_general(dob, qs, _DIMS["tn"], preferred_element_type=F32)
        prev = jnp.where(cc > 0, sp_ref[...], 0.0)
        ddecay = jnp.sum(dst * prev, axis=0, keepdims=True)
        dstate[...] = dst * decay
        dub = dst.astype(MXU_DTYPE)
        vb16 = v_ref[...].astype(MXU_DTYPE)
        dv_ref[...] = lax.dot_general(kdec.astype(MXU_DTYPE), dub, _DIMS["nt"], preferred_element_type=F32)
        dkdec = jnp.dot(vb16, dub, preferred_element_type=F32)
        dk_ref[...] = dkdec * e
        w = dkdec * kf * e
        db_end = jnp.sum(w, axis=0, keepdims=True) + ddecay * decay
        dla_ref[...] = db_end - jnp.dot(trit_ref[...], w, precision=lax.Precision.HIGHEST, preferred_element_type=F32)

    rows = lambda blk: (lambda h, c: (last - c, blk + h))
    return _pcall(
        body, [z, z, z, la, tri, trit, states, states, do], name=name, grid=(H, nchunk),
        in_specs=[pl.BlockSpec((CHUNK, DK), rows(qb)), pl.BlockSpec((CHUNK, DK), rows(kb)),
                  pl.BlockSpec((CHUNK, DV), rows(vb)), pl.BlockSpec((CHUNK, DK), rows(0)),
                  pl.BlockSpec((CHUNK, CHUNK), lambda h, c: (0, 0)), pl.BlockSpec((CHUNK, CHUNK), lambda h, c: (0, 0)),
                  pl.BlockSpec((None, None, DV, DK), lambda h, c: (h, last - c, 0, 0)),
                  pl.BlockSpec((None, None, DV, DK), lambda h, c: (h, jnp.maximum(last - c - 1, 0), 0, 0)),
                  pl.BlockSpec((CHUNK, DV), rows(0))],
        out_specs=[pl.BlockSpec((CHUNK, DK), rows(0)), pl.BlockSpec((CHUNK, DK), rows(0)),
                   pl.BlockSpec((CHUNK, DV), rows(0)), pl.BlockSpec((CHUNK, DK), rows(0))],
        out_shape=[jax.ShapeDtypeStruct((S, H * DK), F32), jax.ShapeDtypeStruct((S, H * DK), F32),
                   jax.ShapeDtypeStruct((S, H * DV), F32), jax.ShapeDtypeStruct((S, H * DK), F32)],
        scratch_shapes=[pltpu.VMEM((DV, DK), F32)], sem=("parallel", "arbitrary"), comm=comm)


def loss_head(y, target, *, name, tr=256):
    S, D = y.shape
    tr = _tile(S, tr, 8)

    def body(y_ref, t_ref, dy_ref, loss_ref):
        i = pl.program_id(0)
        err = y_ref[...] - t_ref[...]
        dy_ref[...] = err * (1.0 / D)
        part = jnp.zeros((1, LANE), F32) + 0.5 * jnp.sum(jnp.sum(err * err, axis=-1, keepdims=True) * (1.0 / D))

        @pl.when(i == 0)
        def _():
            loss_ref[...] = part

        @pl.when(i > 0)
        def _():
            loss_ref[...] += part

    spec = pl.BlockSpec((tr, D), lambda i: (i, 0))
    return pl.pallas_call(
        body, name=name, grid=(S // tr,), in_specs=[spec, spec],
        out_specs=[spec, pl.BlockSpec((1, LANE), lambda i: (0, 0))],
        out_shape=[jax.ShapeDtypeStruct((S, D), F32), jax.ShapeDtypeStruct((1, LANE), F32)],
        compiler_params=_cparams(("arbitrary",)))(y, target)


def _core_index():
    return lax.axis_index("c").astype(jnp.int32).reshape(1)


def _chip_slots():
    x, y, c = lax.axis_index("x"), lax.axis_index("y"), lax.axis_index("c")
    return jnp.stack([2 * x + y, 2 * (1 - x) + y, 2 * x + (1 - y), 2 * (1 - x) + (1 - y), c]).astype(jnp.int32)


def sum_chip_parts(own, parts, *, name, tr=256):
    _, R, C = own.shape
    tr = _tile(R, tr, 8)

    def body(idx_ref, o_ref, p0_ref, p1_ref, p2_ref, out_ref):
        acc = o_ref[...].astype(F32) + p0_ref[...].astype(F32)
        acc = acc + p1_ref[...].astype(F32)
        out_ref[...] = acc + p2_ref[...].astype(F32)

    def slot(k):
        return pl.BlockSpec((None, tr, C), lambda i, idx: (idx[k], i, 0))

    grid_spec = pltpu.PrefetchScalarGridSpec(num_scalar_prefetch=1, grid=(R // tr,),
                                             in_specs=[slot(0), slot(1), slot(2), slot(3)], out_specs=slot(4))
    return pl.pallas_call(body, name=name, grid_spec=grid_spec, out_shape=jax.ShapeDtypeStruct((2, R, C), F32),
                          compiler_params=_cparams(("parallel",)))(_chip_slots(), own, parts, parts, parts)


def add_own_half(g, got, out_dtype, *, name, tr=256):
    n, _, R, C = g.shape
    tr = _tile(R, tr, 8)

    def body(c_ref, a_ref, b_ref, o_ref):
        o_ref[...] = (a_ref[...].astype(F32) + b_ref[...].astype(F32)).astype(out_dtype)

    spec = pl.BlockSpec((None, tr, C), lambda s, i, c: (s, i, 0))
    grid_spec = pltpu.PrefetchScalarGridSpec(
        num_scalar_prefetch=1, grid=(n, R // tr),
        in_specs=[pl.BlockSpec((None, None, tr, C), lambda s, i, c: (s, c[0], i, 0)), spec], out_specs=spec)
    return pl.pallas_call(body, name=name, grid_spec=grid_spec, out_shape=jax.ShapeDtypeStruct((n, R, C), out_dtype),
                          compiler_params=_cparams(("parallel", "parallel")))(_core_index(), g, got)


def adamw(items, *, name, max_steps=16, comm=None):
    c1 = 1.0 / (1.0 - ADAM_B1 ** ADAM_STEP)
    c2 = 1.0 / (1.0 - ADAM_B2 ** ADAM_STEP)
    n = len(items)
    steps = max_steps
    while steps > 1 and any(it[0].shape[0] % (8 * steps) for it in items):
        steps //= 2

    def body(*refs):
        for a in range(n):
            w_ref, g_ref, m_ref, v_ref = refs[4 * a:4 * a + 4]
            d_ref, nm_ref, nv_ref = refs[4 * n + 3 * a:4 * n + 3 * a + 3]
            gv = g_ref[...]
            nm = ADAM_B1 * m_ref[...] + (1.0 - ADAM_B1) * gv
            nv = ADAM_B2 * v_ref[...] + (1.0 - ADAM_B2) * (gv * gv)
            nm_ref[...] = nm
            nv_ref[...] = nv
            d_ref[...] = -ADAM_LR * ((nm * c1) / (jnp.sqrt(nv * c2) + ADAM_EPS) + ADAM_WD * w_ref[...])

    ops, in_specs, out_specs, out_shape = [], [], [], []
    for w, g, m, v in items:
        R, C = w.shape
        spec = pl.BlockSpec((R // steps, C), lambda i: (i, 0))
        ops += [w, g, m, v]
        in_specs += [spec] * 4
        out_specs += [spec] * 3
        out_shape += [jax.ShapeDtypeStruct((R, C), F32)] * 3
    res = _pcall(body, ops, name=name, grid=(steps,), in_specs=in_specs, out_specs=out_specs, out_shape=out_shape,
                 sem=("parallel",), comm=comm)
    flat, extra = res if comm is not None else (res, None)
    triples = [tuple(flat[3 * a:3 * a + 3]) for a in range(n)]
    return (triples, extra) if comm is not None else triples


def _place():
    x, y, c = lax.axis_index("x"), lax.axis_index("y"), lax.axis_index("c")
    chips = [(1 - x, y), (x, 1 - y), (1 - x, 1 - y)]
    return x, y, c, chips


def _rcopy(src, dst, send, recv, j, to):
    return pltpu.make_async_remote_copy(src_ref=src, dst_ref=dst, send_sem=send.at[j], recv_sem=recv.at[j], device_id=to,
                                        device_id_type=MESH)


def gather_stage1(shards, split):
    n = len(shards)
    ins = [s.reshape(2, s.shape[0] // 2, s.shape[1]) if sp else s for s, sp in zip(shards, split)]
    outs = [jax.ShapeDtypeStruct((N_CHIPS,) + a.shape, a.dtype) for a in ins]

    def start(in_refs, out_refs, send, recv, base):
        x, y, c, chips = _place()
        mine = 2 * x + y
        for i in range(n):
            src = in_refs[i].at[c] if split[i] else in_refs[i]
            dst = out_refs[i].at[mine, c] if split[i] else out_refs[i].at[mine]
            for k, (px, py) in enumerate(chips):
                _rcopy(src, dst, send, recv, base + 3 * i + k, (px, py, c)).start()

    def wait(in_refs, out_refs, send, recv, base):
        x, y, c, chips = _place()
        for i in range(n):
            src = in_refs[i].at[c] if split[i] else in_refs[i]
            for k, (px, py) in enumerate(chips):
                dst = out_refs[i].at[2 * px + py, c] if split[i] else out_refs[i].at[2 * px + py]
                _rcopy(src, dst, send, recv, base + 3 * i + k, (px, py, c)).wait()

    return Comm(ins, outs, 3 * n, start, wait)


def gather_stage2(slots, shards, split):
    n = len(slots)
    own = [s.reshape(2, s.shape[0] // 2, s.shape[1]) if sp else s for s, sp in zip(shards, split)]

    def copies(in_refs, out_refs, send, recv, base):
        x, y, c, chips = _place()
        sib = (x, y, 1 - c)
        for i in range(n):
            j = base + 4 * i
            mine = out_refs[i].at[2 * x + y]
            yield _rcopy(in_refs[n + i], mine, send, recv, j + 3, sib), _rcopy(in_refs[n + i], mine, send, recv, j + 3, sib)
            if split[i]:
                for k, (px, py) in enumerate(chips):
                    s = 2 * px + py
                    yield (_rcopy(in_refs[i].at[s, c], out_refs[i].at[s, c], send, recv, j + k, sib),
                           _rcopy(in_refs[i].at[s, c], out_refs[i].at[s, 1 - c], send, recv, j + k, sib))

    def start(*a):
        for out, _ in copies(*a):
            out.start()

    def wait(*a):
        for _, back in copies(*a):
            back.wait()

    return Comm(list(slots) + own, [jax.ShapeDtypeStruct(s.shape, s.dtype) for s in slots], 4 * n, start, wait,
                {i: i for i in range(n)})


def swap_halves(gs):
    n = len(gs)

    def copies(in_refs, out_refs, send, recv, base):
        x, y, c, _ = _place()
        return [_rcopy(in_refs[i].at[s, 1 - c], out_refs[i].at[s], send, recv, base + N_CHIPS * i + s, (x, y, 1 - c))
                for i in range(n) for s in range(N_CHIPS)]

    def start(*a):
        for cp in copies(*a):
            cp.start()

    def wait(*a):
        for cp in copies(*a):
            cp.wait()

    return Comm(gs, [jax.ShapeDtypeStruct((N_CHIPS,) + g.shape[2:], g.dtype) for g in gs], N_CHIPS * n, start, wait)


def exchange_chips(ps):
    n = len(ps)

    def start(in_refs, out_refs, send, recv, base):
        x, y, c, chips = _place()
        for i in range(n):
            for k, (px, py) in enumerate(chips):
                _rcopy(in_refs[i].at[2 * px + py], out_refs[i].at[2 * x + y], send, recv, base + 3 * i + k,
                       (px, py, c)).start()

    def wait(in_refs, out_refs, send, recv, base):
        x, y, c, chips = _place()
        for i in range(n):
            for k, (px, py) in enumerate(chips):
                _rcopy(in_refs[i].at[2 * px + py], out_refs[i].at[2 * px + py], send, recv, base + 3 * i + k,
                       (px, py, c)).wait()

    return Comm(ps, [jax.ShapeDtypeStruct(p.shape, p.dtype) for p in ps], 3 * n, start, wait)


def join_halves(fs):
    n = len(fs)

    def start(in_refs, out_refs, send, recv, base):
        x, y, c, _ = _place()
        for i in range(n):
            _rcopy(in_refs[i].at[c], out_refs[i].at[c], send, recv, base + i, (x, y, 1 - c)).start()

    def wait(in_refs, out_refs, send, recv, base):
        x, y, c, _ = _place()
        for i in range(n):
            _rcopy(in_refs[i].at[c], out_refs[i].at[1 - c], send, recv, base + i, (x, y, 1 - c)).wait()

    return Comm(fs, [jax.ShapeDtypeStruct(f.shape, f.dtype) for f in fs], n, start, wait, {i: i for i in range(n)})


def allreduce_small(v, *, name):
    m_per, n = v.shape

    def body(x_ref, sum_ref, all_ref, send_sems, recv_sems, local_sem):
        x, y, c, chips = _place()
        me, sibling = (x, y, c), (x, y, 1 - c)

        def rows(px, py, pc):
            return all_ref.at[pl.ds((4 * px + 2 * py + pc) * m_per, m_per), :]

        def copy(k, block, to, src=None):
            return pltpu.make_async_remote_copy(src_ref=rows(*block) if src is None else src, dst_ref=rows(*block),
                                                send_sem=send_sems.at[k], recv_sem=recv_sems.at[k], device_id=to,
                                                device_id_type=MESH)

        mine = pltpu.make_async_copy(x_ref, rows(*me), local_sem)
        mine.start()
        first = [copy(0, me, sibling, src=x_ref)]
        first += [copy(1 + j, me, (*chip, c), src=x_ref) for j, chip in enumerate(chips)]
        for cp in first:
            cp.start()
        passed = [copy(4 + j, (*chip, c), sibling) for j, chip in enumerate(chips)]
        for j, chip in enumerate(chips):
            copy(1 + j, (*chip, c), me).wait_recv()
            passed[j].start()
        copy(0, sibling, me).wait_recv()
        for j, chip in enumerate(chips):
            copy(4 + j, (*chip, 1 - c), me).wait_recv()
        for cp in first + passed:
            cp.wait_send()
        mine.wait()
        acc = all_ref[0:m_per, :]
        for d in range(1, N_DEV):
            acc = acc + all_ref[d * m_per:(d + 1) * m_per, :]
        sum_ref[...] = acc

    vm = pl.BlockSpec(memory_space=pltpu.VMEM)
    return pl.pallas_call(
        body, name=name, in_specs=[vm], out_specs=vm, out_shape=jax.ShapeDtypeStruct((m_per, n), F32),
        scratch_shapes=[pltpu.VMEM((N_DEV * m_per, n), F32), pltpu.SemaphoreType.DMA((7,)),
                        pltpu.SemaphoreType.DMA((7,)), pltpu.SemaphoreType.DMA],
    )(v)


def _cols_to_slots(w):
    r, c4 = w.shape
    return w.reshape(r, N_CHIPS, c4 // N_CHIPS).transpose(1, 0, 2)


def _slots_to_cols(w):
    n, r, c = w.shape
    return w.transpose(1, 0, 2).reshape(r, n * c)


def _pad_cols(a, width):
    return jnp.pad(a, ((0, 0), (0, width - a.shape[1])))


class InLayout:
    def __init__(self, q_rank, kv_rank):
        gk = GLA_HEADS * GLA_DK
        gv = GLA_HEADS * GLA_DV
        sizes = [q_rank, kv_rank, MLA_ROPE, gk, gk, gv, GLA_GATE_RANK, gv]
        names = ["zq", "zkv", "zkr", "gq", "gk", "gv", "zg", "zr"]
        starts = np.concatenate([[0], np.cumsum(sizes)[:-1]])
        self.ref = {n: (int(s), int(z)) for n, s, z in zip(names, starts, sizes)}
        self.ref_width = int(sum(sizes))
        self.order = ["gv", "zr", "zq", "gq", "gk", "zkv", "zkr", "zg"]
        self.off, self.size = {}, {}
        pos = 0
        for n in self.order:
            padded = -(-self.ref[n][1] // LANE) * LANE
            self.off[n], self.size[n] = pos, padded
            pos += padded
        self.width = pos

    def pad_weight(self, w):
        return jnp.concatenate([_pad_cols(w[:, self.ref[n][0]:self.ref[n][0] + self.ref[n][1]], self.size[n])
                                for n in self.order], axis=1)

    def unpad_grad(self, g):
        names = sorted(self.ref, key=lambda n: self.ref[n][0])
        return jnp.concatenate([g[:, self.off[n]:self.off[n] + self.ref[n][1]] for n in names], axis=1)


def _pad_q_up(w):
    r = w.shape[0]
    w = w.reshape(r, MLA_HEADS, MLA_QK)
    w = jnp.pad(w, ((0, 0), (0, 0), (0, MLA_HEAD_PAD - MLA_QK)))
    return w.reshape(r, MLA_HEADS * MLA_HEAD_PAD)


def _unpad_q_up(g):
    r = g.shape[0]
    return g.reshape(r, MLA_HEADS, MLA_HEAD_PAD)[:, :, :MLA_QK].reshape(r, MLA_HEADS * MLA_QK)


def _interleave(a, b, heads):
    s = a.shape[0]
    w = a.shape[1] // heads
    return jnp.stack([a.reshape(s, heads, w), b.reshape(s, heads, w)], axis=2).reshape(s, heads * 2 * w)


def _rope_tables(positions):
    half = MLA_ROPE // 2
    inv_freq = ROPE_THETA ** (-jnp.arange(half, dtype=F32) / half)
    ang = positions.astype(F32).reshape(-1, 1) * inv_freq
    cos, sin = jnp.cos(ang), jnp.sin(ang)
    s = ang.shape[0]
    cosf = jnp.concatenate([cos, cos, jnp.ones((s, LANE - MLA_ROPE), F32)], axis=1)
    sinf = jnp.concatenate([sin, sin, jnp.zeros((s, LANE - MLA_ROPE), F32)], axis=1)
    rot = np.zeros((LANE, LANE), np.float32)
    for j in range(half):
        rot[j + half, j] = -1.0
        rot[j, j + half] = 1.0
    return cosf, sinf, jnp.asarray(rot)


SMALL = ["ffn1_norm", "mix_norm", "q_a_norm", "kv_a_norm", "mla_q_norm", "mla_k_norm", "gla_b_gate", "gla_out_norm",
         "mem_attn_norm", "mem_norm", "mem_q_norm", "mem_k_norm", "ffn2_norm"]
BIG = ["ffn1_w_gate", "ffn1_w_up", "ffn1_w_down", "w_in", "w_q_up", "w_kv_up", "w_out", "mem_w_q", "mem_w_k",
       "mem_w_v", "mem_w_o", "ffn2_w_gate", "ffn2_w_up", "ffn2_w_down"]
COL_SHARDED = {"ffn1_w_gate", "ffn1_w_up", "w_in", "w_q_up", "w_kv_up", "gla_w_gate2", "mem_w_o", "ffn2_w_gate", "ffn2_w_up"}
WEIGHTS = ["ffn1_norm", "ffn1_w_gate", "ffn1_w_up", "ffn1_w_down", "mix_norm", "w_in", "q_a_norm", "w_q_up", "kv_a_norm",
           "w_kv_up", "mla_q_norm", "mla_k_norm", "gla_w_gate2", "gla_b_gate", "gla_out_norm", "w_out", "mem_attn_norm",
           "mem_norm", "mem_w_q", "mem_w_k", "mem_w_v", "mem_w_o", "mem_q_norm", "mem_k_norm", "ffn2_norm", "ffn2_w_gate",
           "ffn2_w_up", "ffn2_w_down"]


def _pack_small(vals, rows=8):
    flat = jnp.concatenate([v.reshape(-1).astype(F32) for v in vals])
    n = flat.shape[0]
    per = -(-n // (rows * LANE)) * LANE
    return jnp.pad(flat, (0, rows * per - n)).reshape(rows, per)


def _unpack_small(packed, shapes):
    flat = packed.reshape(-1)
    out, pos = [], 0
    for s in shapes:
        n = int(np.prod(s))
        out.append(flat[pos:pos + n].reshape(s))
        pos += n
    return out


FFN1 = ["ffn1_w_gate", "ffn1_w_up", "ffn1_w_down"]
FFN2 = ["ffn2_w_gate", "ffn2_w_up", "ffn2_w_down"]
SLOT_WEIGHTS = {"ffn1_w_gate", "ffn1_w_up", "ffn2_w_gate", "ffn2_w_up"}
MID = ["w_in", "w_q_up", "w_kv_up", "w_out", "mem_w_q", "mem_w_k", "mem_w_v", "mem_w_o", "gla_w_gate2"]


def _with(res, comm):
    return res if comm is not None else (res, None)


def kernel(x, mem, positions, ffn1_norm, ffn1_w_gate, ffn1_w_up, ffn1_w_down, mix_norm, w_in, q_a_norm, w_q_up, kv_a_norm, w_kv_up, mla_q_norm, mla_k_norm, gla_w_gate2, gla_b_gate, gla_out_norm, w_out, mem_attn_norm, mem_norm, mem_w_q, mem_w_k, mem_w_v, mem_w_o, mem_q_norm, mem_k_norm, ffn2_norm, ffn2_w_gate, ffn2_w_up, ffn2_w_down, loss_target, m_ffn1_norm, m_ffn1_w_gate, m_ffn1_w_up, m_ffn1_w_down, m_mix_norm, m_w_in, m_q_a_norm, m_w_q_up, m_kv_a_norm, m_w_kv_up, m_mla_q_norm, m_mla_k_norm, m_gla_w_gate2, m_gla_b_gate, m_gla_out_norm, m_w_out, m_mem_attn_norm, m_mem_norm, m_mem_w_q, m_mem_w_k, m_mem_w_v, m_mem_w_o, m_mem_q_norm, m_mem_k_norm, m_ffn2_norm, m_ffn2_w_gate, m_ffn2_w_up, m_ffn2_w_down, v_ffn1_norm, v_ffn1_w_gate, v_ffn1_w_up, v_ffn1_w_down, v_mix_norm, v_w_in, v_q_a_norm, v_w_q_up, v_kv_a_norm, v_w_kv_up, v_mla_q_norm, v_mla_k_norm, v_gla_w_gate2, v_gla_b_gate, v_gla_out_norm, v_w_out, v_mem_attn_norm, v_mem_norm, v_mem_w_q, v_mem_w_k, v_mem_w_v, v_mem_w_o, v_mem_q_norm, v_mem_k_norm, v_ffn2_norm, v_ffn2_w_gate, v_ffn2_w_up, v_ffn2_w_down):
    args = dict(locals())
    two_d = lambda a: a[0] if a.ndim == 3 else a
    W = {n: two_d(args[n]) for n in WEIGHTS}
    M1 = {n: two_d(args["m_" + n]) for n in WEIGHTS}
    V2 = {n: two_d(args["v_" + n]) for n in WEIGHTS}
    xs, mems, tgt = x[0], mem[0], loss_target[0]
    S, D = xs.shape
    chip = 2 * lax.axis_index("x") + lax.axis_index("y")

    shard16 = {n: W[n].astype(BF16) for n in BIG + ["gla_w_gate2"]}
    full = {}

    def stage1(names):
        return gather_stage1([shard16[n] for n in names], [n != "gla_w_gate2" for n in names])

    def stage2(names, slots):
        return gather_stage2(slots, [shard16[n] for n in names], [n != "gla_w_gate2" for n in names])

    def finish(names, slots):
        for n, s in zip(names, slots):
            s = s.reshape((N_CHIPS,) + shard16[n].shape)
            if n in SLOT_WEIGHTS:
                full[n] = s
            else:
                full[n] = _slots_to_cols(s) if n in COL_SHARDED else s.reshape(-1, s.shape[2])

    finish(FFN1, run_comm(stage2(FFN1, run_comm(stage1(FFN1), name="gather_ffn1")), name="pass_ffn1"))
    q_rank, kv_rank = W["w_q_up"].shape[0], W["w_kv_up"].shape[0]
    lay = InLayout(q_rank, kv_rank)
    off = lay.off
    cosf, sinf, rot = _rope_tables(positions[0])
    tri = jnp.asarray(np.tril(np.ones((CHUNK, CHUNK), np.float32)))
    gqn = W["mla_q_norm"][:, :MLA_NOPE]
    gqr = _pad_cols(W["mla_q_norm"][:, MLA_NOPE:], LANE)
    gkn = W["mla_k_norm"][:, :MLA_NOPE]
    gkr = _pad_cols(W["mla_k_norm"][:, MLA_NOPE:], LANE)
    HP = MLA_HEAD_PAD
    mla_scale = MLA_QK ** -0.5
    mem_scale = MEM_HEAD_DIM ** -0.5
    mla_w = MLA_HEADS * MLA_V
    gla_w = GLA_HEADS * GLA_DV
    mem_w = MEM_HEADS * MEM_HEAD_DIM

    n1 = row_fwd(rms_fn, [V(xs)], [W["ffn1_norm"]], [(D, BF16)], [(0, 0, 0)], name="ffn1_norm")[0]
    (gate1, up1, act1), mid_s1 = ffn_up(n1, full["ffn1_w_gate"], full["ffn1_w_up"], name="ffn1_up", comm=stage1(MID))
    x1, mid_s2 = mm([(act1, full["ffn1_w_down"])], "nn", F32, alpha=0.5, res=xs, name="ffn1_down",
                    comm=stage2(MID, mid_s1))
    ffn1_saved = (n1, gate1, up1, act1)
    finish(MID, mid_s2)
    w_in_p = lay.pad_weight(full["w_in"])
    w_q_up_p = _pad_q_up(full["w_q_up"])
    w_gate2_p = jnp.pad(full["gla_w_gate2"], ((0, LANE - GLA_GATE_RANK), (0, 0)))
    h = row_fwd(rms_fn, [V(x1)], [W["mix_norm"]], [(D, BF16)], [(0, 0, 0)], name="mix_norm")[0]
    z, f2_gate = mm([(h, w_in_p)], "nn", F32, name="in_proj", comm=stage1(FFN2[:1]))
    qa = row_fwd(rms_fn, [V(z, off["zq"], q_rank)], [W["q_a_norm"]], [(q_rank, BF16)], [(0, 0, 0)], name="q_a_norm")[0]
    kva = row_fwd(rms_fn, [V(z, off["zkv"], kv_rank)], [W["kv_a_norm"]], [(kv_rank, BF16)], [(0, 0, 0)], name="kv_a_norm")[0]
    qraw = mm([(qa, w_q_up_p)], "nn", F32, name="q_up")
    kvraw = mm([(kva, full["w_kv_up"])], "nn", F32, name="kv_up")
    tabs = [V(cosf, diff=False), V(sinf, diff=False)]
    q_rows = [V(qraw, 0, LANE, HP), V(qraw, LANE, LANE, HP)] + tabs
    k_rows = [V(kvraw, 0, LANE, HP), V(z, off["zkr"], LANE, 0)] + tabs
    qh = row_fwd(qk_prep_fn, q_rows, [gqn, gqr, rot], [(MLA_HEADS * HP, BF16)], [(0, 0, HP), (0, LANE, HP)],
                 heads=MLA_HEADS, name="q_prep")[0]
    kh = row_fwd(qk_prep_fn, k_rows, [gkn, gkr, rot], [(MLA_HEADS * HP, BF16)], [(0, 0, HP), (0, LANE, HP)],
                 heads=MLA_HEADS, name="k_prep")[0]
    mla_kw = dict(heads=MLA_HEADS, dk=HP, dv=MLA_V, v_off=1, v_hs=2, scale=mla_scale, causal=True)
    o_mla, f2_up = attn_fwd(qh, kh, kvraw, name="mla_attn", comm=stage1(FFN2[1:2]), **mla_kw)

    zg = z[:, off["zg"]:off["zg"] + LANE]
    pre = mm([(zg, w_gate2_p)], "nn", F32, name="gla_gate")
    la = row_fwd(gate_fn, [V(pre)], [W["gla_b_gate"]], [(pre.shape[1], F32)], [(0, 0, 0)], name="gla_log_decay")[0]
    gla_kw = dict(q_off=off["gq"], k_off=off["gk"], v_off=off["gv"])
    (o_raw, states), f2_down = gla_fwd(z, la, tri, name="gla_scan", comm=stage1(FFN2[2:]), **gla_kw)
    gla_rows = [V(o_raw, 0, GLA_DV, GLA_DV), V(z, off["zr"], GLA_DV, GLA_DV)]
    o_gla = row_fwd(gla_out_fn, gla_rows, [W["gla_out_norm"]], [(gla_w, BF16)], [(0, 0, GLA_DV)], heads=GLA_HEADS,
                    name="gla_out")[0]
    o_cat = jnp.concatenate([o_mla, o_gla], axis=1)
    ffn2_s1 = f2_gate + f2_up + f2_down
    x2, ffn2_s2 = mm([(o_cat, full["w_out"])], "nn", F32, res=x1, name="out_proj", comm=stage2(FFN2, ffn2_s1))
    finish(FFN2, ffn2_s2)

    hm = row_fwd(rms_fn, [V(x2)], [W["mem_attn_norm"]], [(D, BF16)], [(0, 0, 0)], name="mem_attn_norm")[0]
    mn = row_fwd(rms_fn, [V(mems)], [W["mem_norm"]], [(D, BF16)], [(0, 0, 0)], name="mem_norm")[0]
    qm_raw = mm([(hm, full["mem_w_q"])], "nn", F32, name="mem_q")
    km_raw = mm([(mn, full["mem_w_k"])], "nn", F32, name="mem_k")
    vm = mm([(mn, full["mem_w_v"])], "nn", F32, name="mem_v")
    hd = MEM_HEAD_DIM
    qm = row_fwd(rms_fn, [V(qm_raw, 0, hd, hd)], [W["mem_q_norm"]], [(mem_w, BF16)], [(0, 0, hd)], heads=MEM_HEADS,
                 name="mem_q_norm")[0]
    km = row_fwd(rms_fn, [V(km_raw, 0, hd, hd)], [W["mem_k_norm"]], [(mem_w, BF16)], [(0, 0, hd)], heads=MEM_HEADS,
                 name="mem_k_norm")[0]
    mem_kw = dict(heads=MEM_HEADS, dk=hd, dv=hd, v_off=0, v_hs=1, scale=mem_scale, causal=False)
    om = attn_fwd(qm, km, vm, name="mem_attn", **mem_kw)
    x3 = mm([(om, full["mem_w_o"])], "nn", F32, res=x2, name="mem_o")

    n2 = row_fwd(rms_fn, [V(x3)], [W["ffn2_norm"]], [(D, BF16)], [(0, 0, 0)], name="ffn2_norm")[0]
    gate2, up2, act2 = ffn_up(n2, full["ffn2_w_gate"], full["ffn2_w_up"], name="ffn2_up")
    y = mm([(act2, full["ffn2_w_down"])], "nn", F32, alpha=0.5, res=x3, name="ffn2_down")
    dy, loss_part = loss_head(y, tgt, name="loss_head")
    loss = lax.psum(loss_part[0, 0], ("x", "y", "c"))

    G, chip_sum, reduced = {}, {}, {}

    def to_halves(n):
        g = G[n]
        if n in SLOT_WEIGHTS:
            s = g
        else:
            s = _cols_to_slots(g) if n in COL_SHARDED else g.reshape(N_CHIPS, g.shape[0] // N_CHIPS, g.shape[1])
        return s.reshape(N_CHIPS, 2, s.shape[1] // 2, s.shape[2])

    def add2(names, halves, got):
        for n, a, b in zip(names, halves, got):
            chip_sum[n] = add_own_half(a, b, BF16, name=f"rs_add2_{n}")

    def add4_join(names, parts):
        total = [sum_chip_parts(chip_sum[n], p, name=f"rs_add4_{n}") for n, p in zip(names, parts)]
        for n, b in zip(names, run_comm(join_halves(total), name=f"rs_join_{names[0]}")):
            reduced[n] = b.reshape(W[n].shape)

    def ffn_backward(dout, xin, tag, saved, dact_comm=None, after_dact=None, after_dwd=None):
        n_, gate, up, act = saved
        nd, ng, nu = f"{tag}_w_down", f"{tag}_w_gate", f"{tag}_w_up"
        (dgate, dup), got0 = _with(ffn_dact(dout, full[nd], gate, up, 0.5, name=f"{tag}_dact", comm=dact_comm), dact_comm)
        dwd_comm = after_dact(got0) if after_dact else None
        G[nd], got1 = _with(mm([(act, dout)], "tn", F32, alpha=0.5, name=f"{tag}_dwd", tm=1408, tn=2048, tk=256,
                               comm=dwd_comm), dwd_comm)
        if after_dwd:
            after_dwd(got1)
        hd_ = to_halves(nd)
        G[ng], got_d = mm([(n_, dgate)], "tn", F32, name=f"{tag}_dwg", out_slots=True, tm=2048, tn=1408,
                          comm=swap_halves([hd_]))
        add2([nd], [hd_], got_d)
        hg = to_halves(ng)
        G[nu], (parts_d, got_g) = mm([(n_, dup)], "tn", F32, name=f"{tag}_dwu", out_slots=True, tm=2048, tn=1408,
                                     comm=merge_comms(exchange_chips([chip_sum[nd]]), swap_halves([hg])))
        add2([ng], [hg], [got_g])
        hu = to_halves(nu)
        dn, (parts_g, got_u) = mm([(dgate, full[ng]), (dup, full[nu])], "nt", F32, name=f"{tag}_dn", b_slots=True,
                                  tn=1024, tk=1408,
                                  comm=merge_comms(exchange_chips([chip_sum[ng]]), swap_halves([hu])))
        add2([nu], [hu], [got_u])
        dx, G[f"{tag}_norm"] = row_bwd(rms_fn, [V(xin)], [W[f"{tag}_norm"]], [V(dn)], const_diff=[True], res=dout,
                                       name=f"{tag}_dnorm")
        add4_join([nd, ng], [parts_d, parts_g])
        return dx, exchange_chips([chip_sum[nu]])

    g3, ffn2_up_xchg = ffn_backward(dy, x3, "ffn2", (n2, gate2, up2, act2))

    d_om = mm([(g3, full["mem_w_o"])], "nt", F32, name="mem_o_dx")
    G["mem_w_o"] = mm([(om, g3)], "tn", F32, name="mem_o_dw")
    dqm, dkm, dvm = attn_bwd(qm, km, vm, d_om, name="mem_attn_bwd", **mem_kw)
    dqm_raw, G["mem_q_norm"] = row_bwd(rms_fn, [V(qm_raw, 0, hd, hd)], [W["mem_q_norm"]], [V(dqm, 0, hd, hd)],
                                       const_diff=[True], heads=MEM_HEADS, row_dtype=BF16, name="mem_q_norm_bwd")
    dkm_raw, G["mem_k_norm"] = row_bwd(rms_fn, [V(km_raw, 0, hd, hd)], [W["mem_k_norm"]], [V(dkm, 0, hd, hd)],
                                       const_diff=[True], heads=MEM_HEADS, row_dtype=BF16, name="mem_k_norm_bwd")
    dhm = mm([(dqm_raw, full["mem_w_q"])], "nt", F32, name="mem_q_dx")
    G["mem_w_q"] = mm([(hm, dqm_raw)], "tn", F32, name="mem_q_dw")
    dmn = mm([(dkm_raw, full["mem_w_k"]), (dvm, full["mem_w_v"])], "nt", F32, name="mem_kv_dx")
    G["mem_w_k"] = mm([(mn, dkm_raw)], "tn", F32, name="mem_k_dw")
    G["mem_w_v"] = mm([(mn, dvm)], "tn", F32, name="mem_v_dw")
    _, G["mem_norm"] = row_bwd(rms_fn, [V(mems)], [W["mem_norm"]], [V(dmn)], const_diff=[True], row_dtype=BF16,
                               name="mem_norm_bwd")
    g2, G["mem_attn_norm"] = row_bwd(rms_fn, [V(x2)], [W["mem_attn_norm"]], [V(dhm)], const_diff=[True], res=g3,
                                     name="mem_attn_norm_bwd")

    d_ocat = mm([(g2, full["w_out"])], "nt", F32, name="out_proj_dx")
    G["w_out"] = mm([(o_cat, g2)], "tn", F32, name="out_proj_dw")

    d_oraw, d_zr, G["gla_out_norm"] = row_bwd(gla_out_fn, gla_rows, [W["gla_out_norm"]],
                                              [V(d_ocat, mla_w, GLA_DV, GLA_DV)], const_diff=[True], heads=GLA_HEADS,
                                              name="gla_out_bwd")
    d_gq, d_gk, d_gv, d_la = gla_bwd(z, la, tri, tri.T, states, d_oraw, name="gla_scan_bwd", **gla_kw)
    d_pre, G["gla_b_gate"] = row_bwd(gate_fn, [V(pre)], [W["gla_b_gate"]], [V(d_la)], const_diff=[True], row_dtype=BF16,
                                     name="gla_log_decay_bwd")
    d_zg = mm([(d_pre, w_gate2_p)], "nt", BF16, name="gla_gate_dx")
    G["gla_w_gate2"] = mm([(zg, d_pre)], "tn", F32, name="gla_gate_dw")[:GLA_GATE_RANK]

    (d_qh, d_kh, d_v), ffn2_up_parts = attn_bwd(qh, kh, kvraw, d_ocat, name="mla_attn_bwd", comm=ffn2_up_xchg, **mla_kw)
    add4_join(["ffn2_w_up"], ffn2_up_parts)
    cq = [V(d_qh, 0, LANE, HP), V(d_qh, LANE, LANE, HP)]
    ck = [V(d_kh, 0, LANE, HP), V(d_kh, LANE, LANE, HP)]
    d_qn, d_qr, d_gqn, d_gqr = row_bwd(qk_prep_fn, q_rows, [gqn, gqr, rot], cq, const_diff=[True, True, False],
                                       heads=MLA_HEADS, row_dtype=BF16, name="q_prep_bwd")
    d_kn, d_zkr, d_gkn, d_gkr = row_bwd(qk_prep_fn, k_rows, [gkn, gkr, rot], ck, const_diff=[True, True, False],
                                        heads=MLA_HEADS, row_dtype=BF16, name="k_prep_bwd")
    G["mla_q_norm"] = jnp.concatenate([d_gqn, d_gqr[:, :MLA_ROPE]], axis=1)
    G["mla_k_norm"] = jnp.concatenate([d_gkn, d_gkr[:, :MLA_ROPE]], axis=1)
    d_qraw = _interleave(d_qn, d_qr, MLA_HEADS)
    d_kvraw = _interleave(d_kn, d_v.astype(BF16), MLA_HEADS)
    d_qa = mm([(d_qraw, w_q_up_p)], "nt", F32, name="q_up_dx")
    G["w_q_up"] = _unpad_q_up(mm([(qa, d_qraw)], "tn", F32, name="q_up_dw"))
    d_kva = mm([(d_kvraw, full["w_kv_up"])], "nt", F32, name="kv_up_dx")
    G["w_kv_up"] = mm([(kva, d_kvraw)], "tn", F32, name="kv_up_dw")
    d_zq, G["q_a_norm"] = row_bwd(rms_fn, [V(z, off["zq"], q_rank)], [W["q_a_norm"]], [V(d_qa)], const_diff=[True],
                                  row_dtype=BF16, name="q_a_norm_bwd")
    d_zkv, G["kv_a_norm"] = row_bwd(rms_fn, [V(z, off["zkv"], kv_rank)], [W["kv_a_norm"]], [V(d_kva)], const_diff=[True],
                                    row_dtype=BF16, name="kv_a_norm_bwd")

    seg = {"gv": d_gv, "zr": d_zr, "zq": d_zq, "gq": d_gq, "gk": d_gk, "zkv": d_zkv, "zkr": d_zkr, "zg": d_zg}
    dz = jnp.concatenate([_pad_cols(seg[n].astype(BF16), lay.size[n]) for n in lay.order], axis=1)
    dh = mm([(dz, w_in_p)], "nt", F32, name="in_proj_dx")
    G["w_in"] = lay.unpad_grad(mm([(h, dz)], "tn", F32, name="in_proj_dw"))
    g1, G["mix_norm"] = row_bwd(rms_fn, [V(x1)], [W["mix_norm"]], [V(dh)], const_diff=[True], res=g2, name="mix_norm_bwd")

    mid_rs = [n for n in MID if n != "gla_w_gate2"]
    mid_halves = [to_halves(n) for n in mid_rs]

    def mid_sums(got):
        add2(mid_rs, mid_halves, got)
        return exchange_chips([chip_sum[n] for n in mid_rs])

    gx, ffn1_up_xchg = ffn_backward(g1, xs, "ffn1", ffn1_saved, dact_comm=swap_halves(mid_halves), after_dact=mid_sums,
                                    after_dwd=lambda parts: add4_join(mid_rs, parts))

    grad, delta, new_m, new_v = {}, {}, {}, {}

    def adam_group(names, tag, comm=None):
        res, extra = _with(adamw([(W[n], reduced[n], M1[n], V2[n]) for n in names], name=f"adamw_{tag}", comm=comm), comm)
        for n, (d_, m_, v_) in zip(names, res):
            grad[n], delta[n], new_m[n], new_v[n] = reduced[n], d_, m_, v_
        return extra

    add4_join(["ffn1_w_up"], adam_group(FFN2, "ffn2", comm=ffn1_up_xchg))
    adam_group(mid_rs, "mid")
    adam_group(FFN1, "ffn1")

    small_names = SMALL + ["gla_w_gate2"]
    small_sum = allreduce_small(_pack_small([G[n] for n in small_names]), name="allreduce_small")
    small_g = dict(zip(small_names, _unpack_small(small_sum, [G[n].shape for n in small_names])))
    shard_c = W["gla_w_gate2"].shape[1]
    grad["gla_w_gate2"] = lax.dynamic_slice_in_dim(small_g["gla_w_gate2"], chip * shard_c, shard_c, axis=1)
    pw = _pack_small([W[n] for n in SMALL] + [W["gla_w_gate2"]])
    pg = _pack_small([small_g[n] for n in SMALL] + [grad["gla_w_gate2"]])
    pm = _pack_small([M1[n] for n in SMALL] + [M1["gla_w_gate2"]])
    pv = _pack_small([V2[n] for n in SMALL] + [V2["gla_w_gate2"]])
    (pd, pnm, pnv), = adamw([(pw, pg, pm, pv)], name="adamw_small")
    shapes = [W[n].shape for n in small_names]
    for n, d_, m_, v_ in zip(small_names, _unpack_small(pd, shapes), _unpack_small(pnm, shapes), _unpack_small(pnv, shapes)):
        delta[n], new_m[n], new_v[n] = d_, m_, v_
        if n != "gla_w_gate2":
            grad[n] = small_g[n]

    lead = lambda d: [d[n].reshape(args[n].shape) for n in WEIGHTS]
    return (loss, gx[None], *lead(grad), *lead(delta), *lead(new_m), *lead(new_v))
```

```python
import functools
import math

import numpy as np
import jax
import jax.numpy as jnp
from jax import lax
from jax.experimental import pallas as pl
from jax.experimental.pallas import tpu as pltpu

F32 = jnp.float32
BF16 = jnp.bfloat16
MXU_DTYPE = jnp.bfloat16
MESH = pl.DeviceIdType.MESH
ANY = pl.BlockSpec(memory_space=pl.ANY)

LANE = 128
EPS = 1e-6
CHUNK = 64
MLA_HEADS = 8
MLA_NOPE = 128
MLA_ROPE = 64
MLA_QK = MLA_NOPE + MLA_ROPE
MLA_V = 128
MLA_HEAD_PAD = 2 * LANE
ROPE_THETA = 10000.0
GLA_HEADS = 4
GLA_DK = 128
GLA_DV = 256
GLA_GATE_RANK = 16
GLA_TAU = 16.0
MEM_HEADS = 4
MEM_HEAD_DIM = 128
N_CHIPS = 4
N_DEV = 8

ADAM_LR = 0.001
ADAM_B1 = 0.9
ADAM_B2 = 0.999
ADAM_EPS = 1e-08
ADAM_WD = 0.01
ADAM_STEP = 10

VMEM_LIMIT = 56 * 1024 * 1024


def _cparams(sem=None):
    if sem is None:
        return pltpu.CompilerParams(vmem_limit_bytes=VMEM_LIMIT)
    return pltpu.CompilerParams(dimension_semantics=sem, vmem_limit_bytes=VMEM_LIMIT)


def _tile(dim, pref, unit=LANE):
    if dim <= pref:
        return dim
    t = (pref // unit) * unit
    while t > unit and dim % t:
        t -= unit
    assert dim % t == 0, (dim, pref, unit)
    return t


class Comm:
    def __init__(self, ins, out_shapes, nsem, start, wait, aliases=None):
        self.ins, self.out_shapes, self.nsem = list(ins), list(out_shapes), nsem
        self.start, self.wait, self.aliases = start, wait, dict(aliases or {})


def merge_comms(a, b):
    ai, ao = len(a.ins), len(a.out_shapes)

    def start(ins, outs, send, recv, base):
        a.start(ins[:ai], outs[:ao], send, recv, base)
        b.start(ins[ai:], outs[ao:], send, recv, base + a.nsem)

    def wait(ins, outs, send, recv, base):
        a.wait(ins[:ai], outs[:ao], send, recv, base)
        b.wait(ins[ai:], outs[ao:], send, recv, base + a.nsem)

    aliases = dict(a.aliases)
    aliases.update({ai + i: ao + o for i, o in b.aliases.items()})
    return Comm(a.ins + b.ins, a.out_shapes + b.out_shapes, a.nsem + b.nsem, start, wait, aliases)


def run_comm(comm, *, name):
    ni, no = len(comm.ins), len(comm.out_shapes)

    def body(*refs):
        ins, outs = refs[:ni], refs[ni:ni + no]
        send, recv = refs[ni + no:]
        comm.start(ins, outs, send, recv, 0)
        comm.wait(ins, outs, send, recv, 0)

    return pl.pallas_call(
        body, name=name, in_specs=[ANY] * ni, out_specs=[ANY] * no, out_shape=comm.out_shapes,
        input_output_aliases=comm.aliases,
        scratch_shapes=[pltpu.SemaphoreType.DMA((comm.nsem,)), pltpu.SemaphoreType.DMA((comm.nsem,))])(*comm.ins)


def _pcall(body, ops, *, name, grid, in_specs, out_specs, out_shape, sem, scratch_shapes=(), comm=None):
    if comm is None:
        return pl.pallas_call(body, name=name, grid=grid, in_specs=in_specs, out_specs=out_specs, out_shape=out_shape,
                              scratch_shapes=list(scratch_shapes), compiler_params=_cparams(sem))(*ops)
    multi = isinstance(out_shape, (list, tuple))
    k_out_shape = list(out_shape) if multi else [out_shape]
    k_out_specs = list(out_specs) if multi else [out_specs]
    nki, nko, nks = len(ops), len(k_out_shape), len(scratch_shapes)
    nci, nco = len(comm.ins), len(comm.out_shapes)

    def wrapped(*refs):
        p = 0
        k_in = refs[p:p + nki]; p += nki
        c_in = refs[p:p + nci]; p += nci
        k_out = refs[p:p + nko]; p += nko
        c_out = refs[p:p + nco]; p += nco
        k_scr = refs[p:p + nks]; p += nks
        send, recv = refs[p:]
        first = pl.program_id(0) == 0
        last = pl.program_id(0) == grid[0] - 1
        for a in range(1, len(grid)):
            first = jnp.logical_and(first, pl.program_id(a) == 0)
            last = jnp.logical_and(last, pl.program_id(a) == grid[a] - 1)

        @pl.when(first)
        def _():
            comm.start(c_in, c_out, send, recv, 0)

        body(*k_in, *k_out, *k_scr)

        @pl.when(last)
        def _():
            comm.wait(c_in, c_out, send, recv, 0)

    res = pl.pallas_call(
        wrapped, name=name, grid=grid, in_specs=list(in_specs) + [ANY] * nci, out_specs=k_out_specs + [ANY] * nco,
        out_shape=k_out_shape + comm.out_shapes,
        input_output_aliases={nki + i: nko + o for i, o in comm.aliases.items()},
        scratch_shapes=list(scratch_shapes) + [pltpu.SemaphoreType.DMA((comm.nsem,)), pltpu.SemaphoreType.DMA((comm.nsem,))],
        compiler_params=_cparams(("arbitrary",) * len(grid)))(*ops, *comm.ins)
    k_res = list(res[:nko]) if multi else res[0]
    return k_res, list(res[nko:])


_DIMS = {"nn": (((1,), (0,)), ((), ())), "nt": (((1,), (1,)), ((), ())), "tn": (((0,), (0,)), ((), ()))}


def mm(pairs, mode, out_dtype, *, name, alpha=1.0, res=None, tm=1024, tn=1024, tk=512, b_slots=False, out_slots=False,
       comm=None):
    a0, b0 = pairs[0]
    if b_slots:
        b_rows, b_cols = b0.shape[1], N_CHIPS * b0.shape[2]
    else:
        b_rows, b_cols = b0.shape
    (M, K) = a0.shape[::-1] if mode == "tn" else a0.shape
    N = b_rows if mode == "nt" else b_cols
    shard = (b_cols if b_slots else N) // N_CHIPS
    tm = _tile(M, tm)
    tn = _tile(shard if (out_slots or (b_slots and mode != "nt")) else N, tn)
    tk = _tile(shard if (b_slots and mode == "nt") else K, tk)
    nk = K // tk
    npairs = len(pairs)
    dims = _DIMS[mode]
    if mode == "tn":
        a_spec = pl.BlockSpec((tk, tm), lambda i, j, k: (k, i))
    else:
        a_spec = pl.BlockSpec((tm, tk), lambda i, j, k: (i, k))
    per = shard // (tk if mode == "nt" else tn)
    if mode == "nt":
        b_spec = (pl.BlockSpec((None, tn, tk), lambda i, j, k: (k // per, j, k % per)) if b_slots else
                  pl.BlockSpec((tn, tk), lambda i, j, k: (j, k)))
    else:
        b_spec = (pl.BlockSpec((None, tk, tn), lambda i, j, k: (j // per, k, j % per)) if b_slots else
                  pl.BlockSpec((tk, tn), lambda i, j, k: (k, j)))
    if out_slots:
        assert res is None and mode != "nt"
        o_spec = pl.BlockSpec((None, tm, tn), lambda i, j, k: (j // per, i, j % per))
        out_sds = jax.ShapeDtypeStruct((N_CHIPS, M, shard), out_dtype)
    else:
        o_spec = pl.BlockSpec((tm, tn), lambda i, j, k: (i, j))
        out_sds = jax.ShapeDtypeStruct((M, N), out_dtype)
    has_res = res is not None

    def body(*refs):
        ab = refs[:2 * npairs]
        res_ref = refs[2 * npairs] if has_res else None
        o_ref, acc = refs[-2], refs[-1]
        k = pl.program_id(2)

        @pl.when(k == 0)
        def _():
            acc[...] = jnp.zeros_like(acc)

        for p in range(npairs):
            acc[...] += lax.dot_general(ab[2 * p][...].astype(MXU_DTYPE), ab[2 * p + 1][...].astype(MXU_DTYPE),
                                        dims, preferred_element_type=F32)

        @pl.when(k == nk - 1)
        def _():
            r = acc[...]
            if alpha != 1.0:
                r = r * alpha
            if has_res:
                r = res_ref[...].astype(F32) + r
            o_ref[...] = r.astype(out_dtype)

    ops, specs = [], []
    for a, b in pairs:
        ops += [a, b]
        specs += [a_spec, b_spec]
    if has_res:
        ops.append(res)
        specs.append(o_spec)
    return _pcall(body, ops, name=name, grid=(M // tm, N // tn, nk), in_specs=specs, out_specs=o_spec,
                  out_shape=out_sds, scratch_shapes=[pltpu.VMEM((tm, tn), F32)],
                  sem=("parallel", "parallel", "arbitrary"), comm=comm)


def _sigmoid(x):
    return 1.0 / (1.0 + jnp.exp(-x))


def ffn_up(n, wg, wu, *, name, tm=1024, tn=1408, tk=512, comm=None):
    M, K = n.shape
    shard = wg.shape[2]
    N = N_CHIPS * shard
    tm, tn, tk = _tile(M, tm), _tile(shard, tn), _tile(K, tk)
    nk = K // tk
    per = shard // tn
    w_spec = pl.BlockSpec((None, tk, tn), lambda i, j, k: (j // per, k, j % per))

    def body(n_ref, wg_ref, wu_ref, g_ref, u_ref, a_ref, accg, accu):
        k = pl.program_id(2)

        @pl.when(k == 0)
        def _():
            accg[...] = jnp.zeros_like(accg)
            accu[...] = jnp.zeros_like(accu)

        nv = n_ref[...].astype(MXU_DTYPE)
        accg[...] += jnp.dot(nv, wg_ref[...].astype(MXU_DTYPE), preferred_element_type=F32)
        accu[...] += jnp.dot(nv, wu_ref[...].astype(MXU_DTYPE), preferred_element_type=F32)

        @pl.when(k == nk - 1)
        def _():
            g, u = accg[...], accu[...]
            g_ref[...] = g.astype(g_ref.dtype)
            u_ref[...] = u.astype(u_ref.dtype)
            a_ref[...] = (g * _sigmoid(g) * u).astype(a_ref.dtype)

    o_spec = pl.BlockSpec((tm, tn), lambda i, j, k: (i, j))
    sds = jax.ShapeDtypeStruct((M, N), BF16)
    return _pcall(
        body, [n, wg, wu], name=name, grid=(M // tm, N // tn, nk),
        in_specs=[pl.BlockSpec((tm, tk), lambda i, j, k: (i, k)), w_spec, w_spec],
        out_specs=[o_spec, o_spec, o_spec], out_shape=[sds, sds, sds],
        scratch_shapes=[pltpu.VMEM((tm, tn), F32), pltpu.VMEM((tm, tn), F32)],
        sem=("parallel", "parallel", "arbitrary"), comm=comm)


def ffn_dact(dy, wd, gate, up, alpha, *, name, tm=1024, tn=1408, tk=512, comm=None):
    M, K = dy.shape
    N = wd.shape[0]
    tm, tn, tk = _tile(M, tm), _tile(N, tn), _tile(K, tk)
    nk = K // tk

    def body(dy_ref, wd_ref, g_ref, u_ref, dg_ref, du_ref, acc):
        k = pl.program_id(2)

        @pl.when(k == 0)
        def _():
            acc[...] = jnp.zeros_like(acc)

        acc[...] += lax.dot_general(dy_ref[...].astype(MXU_DTYPE), wd_ref[...].astype(MXU_DTYPE), _DIMS["nt"],
                                    preferred_element_type=F32)

        @pl.when(k == nk - 1)
        def _():
            da = acc[...] * alpha
            g = g_ref[...].astype(F32)
            u = u_ref[...].astype(F32)
            s = _sigmoid(g)
            du_ref[...] = (da * (g * s)).astype(du_ref.dtype)
            dg_ref[...] = (da * u * (s * (1.0 + g * (1.0 - s)))).astype(dg_ref.dtype)

    o_spec = pl.BlockSpec((tm, tn), lambda i, j, k: (i, j))
    sds = jax.ShapeDtypeStruct((M, N), BF16)
    return _pcall(
        body, [dy, wd, gate, up], name=name, grid=(M // tm, N // tn, nk),
        in_specs=[pl.BlockSpec((tm, tk), lambda i, j, k: (i, k)), pl.BlockSpec((tn, tk), lambda i, j, k: (j, k)),
                  o_spec, o_spec],
        out_specs=[o_spec, o_spec], out_shape=[sds, sds], scratch_shapes=[pltpu.VMEM((tm, tn), F32)],
        sem=("parallel", "parallel", "arbitrary"), comm=comm)


def _window(width, off, ext):
    ww = LANE
    while ww < width:
        if ww >= ext and off // ww == (off + ext - 1) // ww and width % ww == 0:
            break
        ww *= 2
    else:
        ww = width
    return ww, off // ww, off - (off // ww) * ww


class V:
    def __init__(self, arr, off=0, w=None, hs=0, diff=True):
        self.arr, self.off, self.hs, self.diff = arr, off, hs, diff
        self.w = arr.shape[1] - off if w is None else w

    def window(self, heads, tr):
        ww, blk, inner = _window(self.arr.shape[1], self.off, (heads - 1) * self.hs + self.w)
        return pl.BlockSpec((tr, ww), lambda i, blk=blk: (i, blk)), inner


def _const_spec(c):
    return pl.BlockSpec(c.shape, lambda i: (0, 0))


def row_fwd(fn, rows, consts, outs, out_map, *, heads=1, tr=256, name):
    S = rows[0].arr.shape[0]
    tr = _tile(S, tr, 8)
    wins = [v.window(heads, tr) for v in rows]
    nr, nc = len(rows), len(consts)

    def body(*refs):
        row_refs, const_refs, out_refs = refs[:nr], refs[nr:nr + nc], refs[nr + nc:]
        cv = [c[...].astype(F32) for c in const_refs]
        for h in range(heads):
            rv = []
            for v, (_, io), r in zip(rows, wins, row_refs):
                lo = io + h * v.hs
                rv.append(r[:, lo:lo + v.w].astype(F32))
            res = fn(*rv, *cv)
            for (ai, off, hs), o in zip(out_map, res):
                lo = off + h * hs
                out_refs[ai][:, lo:lo + o.shape[1]] = o.astype(out_refs[ai].dtype)

    return pl.pallas_call(
        body, name=name, grid=(S // tr,),
        in_specs=[w[0] for w in wins] + [_const_spec(c) for c in consts],
        out_specs=[pl.BlockSpec((tr, w), lambda i: (i, 0)) for w, _ in outs],
        out_shape=[jax.ShapeDtypeStruct((S, w), d) for w, d in outs],
        compiler_params=_cparams(("parallel",)))(*[v.arr for v in rows], *consts)


def row_bwd(fn, rows, consts, cots, *, const_diff, heads=1, tr=256, res=None, row_dtype=F32, name):
    S = rows[0].arr.shape[0]
    tr = _tile(S, tr, 8)
    nr, nc, nct = len(rows), len(consts), len(cots)
    wins = [v.window(heads, tr) for v in rows]
    cwins = [v.window(heads, tr) for v in cots]
    drows = [k for k, v in enumerate(rows) if v.diff]
    dconsts = [k for k in range(nc) if const_diff[k]]
    has_res = res is not None
    ngrid = S // tr

    def body(*refs):
        row_refs = refs[:nr]
        const_refs = refs[nr:nr + nc]
        cot_refs = refs[nr + nc:nr + nc + nct]
        p = nr + nc + nct
        res_ref = refs[p] if has_res else None
        p += int(has_res)
        grow_refs = refs[p:p + len(drows)]
        gconst_refs = refs[p + len(drows):]
        i = pl.program_id(0)
        cv = [c[...].astype(F32) for c in const_refs]
        shared = [None] * len(drows)
        gc_sum = [None] * len(dconsts)
        for h in range(heads):
            rv = []
            for v, (_, io), r in zip(rows, wins, row_refs):
                lo = io + h * v.hs
                rv.append(r[:, lo:lo + v.w].astype(F32))
            ct = []
            for v, (_, io), r in zip(cots, cwins, cot_refs):
                lo = io + h * v.hs
                ct.append(r[:, lo:lo + v.w].astype(F32))

            def closed(*d):
                rr, cc = list(rv), list(cv)
                for k, val in zip(drows, d[:len(drows)]):
                    rr[k] = val
                for k, val in zip(dconsts, d[len(drows):]):
                    cc[k] = val
                return tuple(fn(*rr, *cc))

            _, vjp = jax.vjp(closed, *[rv[k] for k in drows], *[cv[k] for k in dconsts])
            grads = vjp(tuple(ct))
            for n, k in enumerate(drows):
                g = grads[n]
                if rows[k].hs == 0 and heads > 1:
                    shared[n] = g if shared[n] is None else shared[n] + g
                else:
                    if n == 0 and has_res:
                        g = g + res_ref[:, h * rows[k].w:(h + 1) * rows[k].w].astype(F32)
                    grow_refs[n][:, h * rows[k].w:(h + 1) * rows[k].w] = g.astype(row_dtype)
            for n in range(len(dconsts)):
                g = grads[len(drows) + n]
                gc_sum[n] = g if gc_sum[n] is None else gc_sum[n] + g
        for n, k in enumerate(drows):
            if shared[n] is not None:
                g = shared[n]
                if n == 0 and has_res:
                    g = g + res_ref[...].astype(F32)
                grow_refs[n][...] = g.astype(row_dtype)

        @pl.when(i == 0)
        def _():
            for n in range(len(dconsts)):
                gconst_refs[n][...] = gc_sum[n]

        @pl.when(i > 0)
        def _():
            for n in range(len(dconsts)):
                gconst_refs[n][...] += gc_sum[n]

    gw = [rows[k].w * (heads if rows[k].hs else 1) for k in drows]
    in_specs = [w[0] for w in wins] + [_const_spec(c) for c in consts] + [w[0] for w in cwins]
    ops = [v.arr for v in rows] + list(consts) + [v.arr for v in cots]
    if has_res:
        in_specs.append(pl.BlockSpec((tr, gw[0]), lambda i: (i, 0)))
        ops.append(res)
    out_specs = [pl.BlockSpec((tr, w), lambda i: (i, 0)) for w in gw]
    out_shape = [jax.ShapeDtypeStruct((S, w), row_dtype) for w in gw]
    for k in dconsts:
        out_specs.append(_const_spec(consts[k]))
        out_shape.append(jax.ShapeDtypeStruct(consts[k].shape, F32))
    del ngrid
    return pl.pallas_call(body, name=name, grid=(S // tr,), in_specs=in_specs, out_specs=out_specs,
                          out_shape=out_shape, compiler_params=_cparams(("arbitrary",)))(*ops)


def _rms(x, g, n=None):
    n = x.shape[-1] if n is None else n
    ms = jnp.sum(x * x, axis=-1, keepdims=True) * (1.0 / n)
    return x * lax.rsqrt(ms + EPS) * g


def rms_fn(x, g):
    return (_rms(x, g),)


def qk_prep_fn(nope, rope, cos, sin, gn, gr, rot):
    ms = (jnp.sum(nope * nope, axis=-1, keepdims=True) + jnp.sum(rope * rope, axis=-1, keepdims=True)) * (1.0 / MLA_QK)
    r = lax.rsqrt(ms + EPS)
    on = nope * r * gn
    orr = rope * r * gr
    turned = jnp.dot(orr, rot, precision=lax.Precision.HIGHEST, preferred_element_type=F32)
    return on, orr * cos + turned * sin


def gla_out_fn(o, zr, g):
    return (_rms(o, g) * (zr * _sigmoid(zr)),)


def gate_fn(pre, b):
    t = pre + b
    return ((jnp.minimum(t, 0.0) - jnp.log(1.0 + jnp.exp(-jnp.abs(t)))) * (1.0 / GLA_TAU),)


def _attn_probs(q_ref, k_ref, scale, q0, kext):
    s = lax.dot_general(q_ref[...].astype(MXU_DTYPE), k_ref[0:kext, :].astype(MXU_DTYPE), _DIMS["nt"],
                        preferred_element_type=F32) * scale
    if q0 is not None:
        qc = (q0 + lax.broadcasted_iota(jnp.int32, s.shape, 0)) // CHUNK
        kc = lax.broadcasted_iota(jnp.int32, s.shape, 1) // CHUNK
        s = jnp.where(kc <= qc, s, -1e30)
    m = jnp.max(s, axis=-1, keepdims=True)
    e = jnp.exp(s - m)
    return e / jnp.sum(e, axis=-1, keepdims=True)


def _per_query_block(one, causal, nq, tq, Sk):
    if not causal:
        one(None, Sk, None)
        return
    assert tq % CHUNK == 0
    for ib in range(nq):
        pl.when(pl.program_id(1) == ib)(functools.partial(one, ib * tq, min(Sk, (ib + 1) * tq), ib))


def attn_fwd(q, k, v, *, heads, dk, dv, v_off, v_hs, scale, causal, name, tq=256, comm=None):
    Sq, Sk = q.shape[0], k.shape[0]
    tq = _tile(Sq, tq, 8)

    def body(q_ref, k_ref, v_ref, o_ref):
        def one(q0, kext, ib):
            p = _attn_probs(q_ref, k_ref, scale, q0, kext)
            o_ref[...] = jnp.dot(p.astype(MXU_DTYPE), v_ref[0:kext, :].astype(MXU_DTYPE),
                                 preferred_element_type=F32).astype(o_ref.dtype)

        _per_query_block(one, causal, Sq // tq, tq, Sk)

    return _pcall(
        body, [q, k, v], name=name, grid=(heads, Sq // tq),
        in_specs=[pl.BlockSpec((tq, dk), lambda h, i: (i, h)), pl.BlockSpec((Sk, dk), lambda h, i: (0, h)),
                  pl.BlockSpec((Sk, dv), lambda h, i: (0, v_off + h * v_hs))],
        out_specs=pl.BlockSpec((tq, dv), lambda h, i: (i, h)),
        out_shape=jax.ShapeDtypeStruct((Sq, heads * dv), BF16), sem=("parallel", "parallel"), comm=comm)


def attn_bwd(q, k, v, do, *, heads, dk, dv, v_off, v_hs, scale, causal, name, tq=256, comm=None):
    Sq, Sk = q.shape[0], k.shape[0]
    tq = _tile(Sq, tq, 8)

    def body(q_ref, k_ref, v_ref, do_ref, dq_ref, dk_ref, dv_ref):
        @pl.when(pl.program_id(1) == 0)
        def _():
            dk_ref[...] = jnp.zeros_like(dk_ref)
            dv_ref[...] = jnp.zeros_like(dv_ref)

        def one(q0, kext, ib):
            p = _attn_probs(q_ref, k_ref, scale, q0, kext)
            dob = do_ref[...].astype(MXU_DTYPE)
            dp = lax.dot_general(dob, v_ref[0:kext, :].astype(MXU_DTYPE), _DIMS["nt"], preferred_element_type=F32)
            delta = jnp.sum(p * dp, axis=-1, keepdims=True)
            ds = (p * (dp - delta) * scale).astype(MXU_DTYPE)
            dq_ref[...] = jnp.dot(ds, k_ref[0:kext, :].astype(MXU_DTYPE), preferred_element_type=F32)
            dk_ref[0:kext, :] += lax.dot_general(ds, q_ref[...].astype(MXU_DTYPE), _DIMS["tn"],
                                                 preferred_element_type=F32)
            dv_ref[0:kext, :] += lax.dot_general(p.astype(MXU_DTYPE), dob, _DIMS["tn"], preferred_element_type=F32)

        _per_query_block(one, causal, Sq // tq, tq, Sk)

    return _pcall(
        body, [q, k, v, do], name=name, grid=(heads, Sq // tq),
        in_specs=[pl.BlockSpec((tq, dk), lambda h, i: (i, h)), pl.BlockSpec((Sk, dk), lambda h, i: (0, h)),
                  pl.BlockSpec((Sk, dv), lambda h, i: (0, v_off + h * v_hs)),
                  pl.BlockSpec((tq, dv), lambda h, i: (i, h))],
        out_specs=[pl.BlockSpec((tq, dk), lambda h, i: (i, h)), pl.BlockSpec((Sk, dk), lambda h, i: (0, h)),
                   pl.BlockSpec((Sk, dv), lambda h, i: (0, h))],
        out_shape=[jax.ShapeDtypeStruct((Sq, heads * dk), F32), jax.ShapeDtypeStruct((Sk, heads * dk), F32),
                   jax.ShapeDtypeStruct((Sk, heads * dv), F32)],
        sem=("parallel", "arbitrary"), comm=comm)


def _gla_chunk(k, g, tri_ref):
    b = jnp.dot(tri_ref[...], g, precision=lax.Precision.HIGHEST, preferred_element_type=F32)
    b_end = jnp.sum(g, axis=0, keepdims=True)
    e = jnp.exp(b_end - b)
    return k * e, e, jnp.exp(b_end)


def _gla_windows(z, q_off, k_off, v_off, rows_of):
    H, DK, DV = GLA_HEADS, GLA_DK, GLA_DV
    specs, inner = [], []
    for off, ext in ((q_off, H * DK), (k_off, H * DK), (v_off, H * DV)):
        ww, blk, io = _window(z.shape[1], off, ext)
        specs.append(pl.BlockSpec((CHUNK, ww), lambda c, blk=blk: (rows_of(c), blk)))
        inner.append(io)
    return specs, inner


def gla_fwd(z, la, tri, *, q_off, k_off, v_off, name, comm=None):
    S = z.shape[0]
    nchunk = S // CHUNK
    H, DK, DV = GLA_HEADS, GLA_DK, GLA_DV
    qscale = DK ** -0.5
    zspecs, (qi, ki, vi) = _gla_windows(z, q_off, k_off, v_off, lambda c: c)

    def body(q_ref, k_ref, v_ref, la_ref, tri_ref, o_ref, st_ref, state):
        @pl.when(pl.program_id(0) == 0)
        def _():
            state[...] = jnp.zeros_like(state)

        for h in range(H):
            dks, dvs = slice(h * DK, (h + 1) * DK), slice(h * DV, (h + 1) * DV)
            k = k_ref[:, ki + h * DK:ki + (h + 1) * DK].astype(F32)
            v = v_ref[:, vi + h * DV:vi + (h + 1) * DV]
            q = q_ref[:, qi + h * DK:qi + (h + 1) * DK].astype(F32)
            kdec, _, decay = _gla_chunk(k, la_ref[:, dks].astype(F32), tri_ref)
            ut = lax.dot_general(v.astype(MXU_DTYPE), kdec.astype(MXU_DTYPE), _DIMS["tn"], preferred_element_type=F32)
            new = state[h] * decay + ut
            state[h] = new
            st_ref[h] = new
            qs = (q * qscale).astype(MXU_DTYPE)
            o_ref[:, dvs] = lax.dot_general(qs, new.astype(MXU_DTYPE), _DIMS["nt"], preferred_element_type=F32)

    return _pcall(
        body, [z, z, z, la, tri], name=name, grid=(nchunk,),
        in_specs=zspecs + [pl.BlockSpec((CHUNK, H * DK), lambda c: (c, 0)), pl.BlockSpec((CHUNK, CHUNK), lambda c: (0, 0))],
        out_specs=[pl.BlockSpec((CHUNK, H * DV), lambda c: (c, 0)),
                   pl.BlockSpec((H, None, DV, DK), lambda c: (0, c, 0, 0))],
        out_shape=[jax.ShapeDtypeStruct((S, H * DV), F32), jax.ShapeDtypeStruct((H, nchunk, DV, DK), F32)],
        scratch_shapes=[pltpu.VMEM((H, DV, DK), F32)], sem=("arbitrary",), comm=comm)


def gla_bwd(z, la, tri, trit, states, do, *, q_off, k_off, v_off, name, comm=None):
    S = z.shape[0]
    nchunk = S // CHUNK
    H, DK, DV = GLA_HEADS, GLA_DK, GLA_DV
    qscale = DK ** -0.5
    last = nchunk - 1
    zspecs, (qi, ki, vi) = _gla_windows(z, q_off, k_off, v_off, lambda c: last - c)

    def body(q_ref, k_ref, v_ref, la_ref, tri_ref, trit_ref, st_ref, sp_ref, do_ref, dq_ref, dk_ref, dv_ref, dla_ref,
             dstate):
        c = pl.program_id(0)
        cc = last - c

        @pl.when(c == 0)
        def _():
            dstate[...] = jnp.zeros_like(dstate)

        for h in range(H):
            dks, dvs = slice(h * DK, (h + 1) * DK), slice(h * DV, (h + 1) * DV)
            kf = k_ref[:, ki + h * DK:ki + (h + 1) * DK].astype(F32)
            vb16 = v_ref[:, vi + h * DV:vi + (h + 1) * DV].astype(MXU_DTYPE)
            q = q_ref[:, qi + h * DK:qi + (h + 1) * DK].astype(F32)
            kdec, e, decay = _gla_chunk(kf, la_ref[:, dks].astype(F32), tri_ref)
            dob = do_ref[:, dvs].astype(MXU_DTYPE)
            stb = st_ref[h].astype(MXU_DTYPE)
            qs = (q * qscale).astype(MXU_DTYPE)
            dq_ref[:, dks] = jnp.dot(dob, stb, preferred_element_type=F32) * qscale
            dst = dstate[h] + lax.dot_general(dob, qs, _DIMS["tn"], preferred_element_type=F32)
            prev = jnp.where(cc > 0, sp_ref[h], 0.0)
            ddecay = jnp.sum(dst * prev, axis=0, keepdims=True)
            dstate[h] = dst * decay
            dub = dst.astype(MXU_DTYPE)
            dv_ref[:, dvs] = lax.dot_general(kdec.astype(MXU_DTYPE), dub, _DIMS["nt"], preferred_element_type=F32)
            dkdec = jnp.dot(vb16, dub, preferred_element_type=F32)
            dk_ref[:, dks] = dkdec * e
            w = dkdec * kf * e
            db_end = jnp.sum(w, axis=0, keepdims=True) + ddecay * decay
            dla_ref[:, dks] = db_end - jnp.dot(trit_ref[...], w, precision=lax.Precision.HIGHEST,
                                               preferred_element_type=F32)

    def rows(width):
        return pl.BlockSpec((CHUNK, width), lambda c: (last - c, 0))

    square = pl.BlockSpec((CHUNK, CHUNK), lambda c: (0, 0))
    return _pcall(
        body, [z, z, z, la, tri, trit, states, states, do], name=name, grid=(nchunk,),
        in_specs=zspecs + [rows(H * DK), square, square,
                           pl.BlockSpec((H, None, DV, DK), lambda c: (0, last - c, 0, 0)),
                           pl.BlockSpec((H, None, DV, DK), lambda c: (0, jnp.maximum(last - c - 1, 0), 0, 0)),
                           rows(H * DV)],
        out_specs=[rows(H * DK), rows(H * DK), rows(H * DV), rows(H * DK)],
        out_shape=[jax.ShapeDtypeStruct((S, H * DK), F32), jax.ShapeDtypeStruct((S, H * DK), F32),
                   jax.ShapeDtypeStruct((S, H * DV), F32), jax.ShapeDtypeStruct((S, H * DK), F32)],
        scratch_shapes=[pltpu.VMEM((H, DV, DK), F32)], sem=("arbitrary",), comm=comm)


def loss_head(y, target, *, name, tr=256):
    S, D = y.shape
    tr = _tile(S, tr, 8)

    def body(y_ref, t_ref, dy_ref, loss_ref):
        i = pl.program_id(0)
        err = y_ref[...] - t_ref[...]
        dy_ref[...] = err * (1.0 / D)
        part = jnp.zeros((1, LANE), F32) + 0.5 * jnp.sum(jnp.sum(err * err, axis=-1, keepdims=True) * (1.0 / D))

        @pl.when(i == 0)
        def _():
            loss_ref[...] = part

        @pl.when(i > 0)
        def _():
            loss_ref[...] += part

    spec = pl.BlockSpec((tr, D), lambda i: (i, 0))
    return pl.pallas_call(
        body, name=name, grid=(S // tr,), in_specs=[spec, spec],
        out_specs=[spec, pl.BlockSpec((1, LANE), lambda i: (0, 0))],
        out_shape=[jax.ShapeDtypeStruct((S, D), F32), jax.ShapeDtypeStruct((1, LANE), F32)],
        compiler_params=_cparams(("arbitrary",)))(y, target)


def _core_index():
    return lax.axis_index("c").astype(jnp.int32).reshape(1)


def _chip_slots():
    x, y, c = lax.axis_index("x"), lax.axis_index("y"), lax.axis_index("c")
    return jnp.stack([2 * x + y, 2 * (1 - x) + y, 2 * x + (1 - y), 2 * (1 - x) + (1 - y), c]).astype(jnp.int32)


def sum_chip_parts(own, parts, *, name, tr=256):
    _, R, C = own.shape
    tr = _tile(R, tr, 8)

    def body(idx_ref, o_ref, p0_ref, p1_ref, p2_ref, out_ref):
        acc = o_ref[...].astype(F32) + p0_ref[...].astype(F32)
        acc = acc + p1_ref[...].astype(F32)
        out_ref[...] = acc + p2_ref[...].astype(F32)

    def slot(k):
        return pl.BlockSpec((None, tr, C), lambda i, idx: (idx[k], i, 0))

    grid_spec = pltpu.PrefetchScalarGridSpec(num_scalar_prefetch=1, grid=(R // tr,),
                                             in_specs=[slot(0), slot(1), slot(2), slot(3)], out_specs=slot(4))
    return pl.pallas_call(body, name=name, grid_spec=grid_spec, out_shape=jax.ShapeDtypeStruct((2, R, C), F32),
                          compiler_params=_cparams(("parallel",)))(_chip_slots(), own, parts, parts, parts)


def add_own_half(g, got, out_dtype, *, name, tr=256):
    n, _, R, C = g.shape
    tr = _tile(R, tr, 8)

    def body(c_ref, a_ref, b_ref, o_ref):
        o_ref[...] = (a_ref[...].astype(F32) + b_ref[...].astype(F32)).astype(out_dtype)

    spec = pl.BlockSpec((None, tr, C), lambda s, i, c: (s, i, 0))
    grid_spec = pltpu.PrefetchScalarGridSpec(
        num_scalar_prefetch=1, grid=(n, R // tr),
        in_specs=[pl.BlockSpec((None, None, tr, C), lambda s, i, c: (s, c[0], i, 0)), spec], out_specs=spec)
    return pl.pallas_call(body, name=name, grid_spec=grid_spec, out_shape=jax.ShapeDtypeStruct((n, R, C), out_dtype),
                          compiler_params=_cparams(("parallel", "parallel")))(_core_index(), g, got)


def adamw(items, *, name, max_steps=16, comm=None):
    c1 = 1.0 / (1.0 - ADAM_B1 ** ADAM_STEP)
    c2 = 1.0 / (1.0 - ADAM_B2 ** ADAM_STEP)
    n = len(items)
    steps = max_steps
    while steps > 1 and any(it[0].shape[0] % (8 * steps) for it in items):
        steps //= 2

    def body(*refs):
        for a in range(n):
            w_ref, g_ref, m_ref, v_ref = refs[4 * a:4 * a + 4]
            d_ref, nm_ref, nv_ref = refs[4 * n + 3 * a:4 * n + 3 * a + 3]
            gv = g_ref[...]
            nm = ADAM_B1 * m_ref[...] + (1.0 - ADAM_B1) * gv
            nv = ADAM_B2 * v_ref[...] + (1.0 - ADAM_B2) * (gv * gv)
            nm_ref[...] = nm
            nv_ref[...] = nv
            d_ref[...] = -ADAM_LR * ((nm * c1) / (jnp.sqrt(nv * c2) + ADAM_EPS) + ADAM_WD * w_ref[...])

    ops, in_specs, out_specs, out_shape = [], [], [], []
    for w, g, m, v in items:
        R, C = w.shape
        spec = pl.BlockSpec((R // steps, C), lambda i: (i, 0))
        ops += [w, g, m, v]
        in_specs += [spec] * 4
        out_specs += [spec] * 3
        out_shape += [jax.ShapeDtypeStruct((R, C), F32)] * 3
    res = _pcall(body, ops, name=name, grid=(steps,), in_specs=in_specs, out_specs=out_specs, out_shape=out_shape,
                 sem=("parallel",), comm=comm)
    flat, extra = res if comm is not None else (res, None)
    triples = [tuple(flat[3 * a:3 * a + 3]) for a in range(n)]
    return (triples, extra) if comm is not None else triples


def _place():
    x, y, c = lax.axis_index("x"), lax.axis_index("y"), lax.axis_index("c")
    chips = [(1 - x, y), (x, 1 - y), (1 - x, 1 - y)]
    return x, y, c, chips


def _rcopy(src, dst, send, recv, j, to):
    return pltpu.make_async_remote_copy(src_ref=src, dst_ref=dst, send_sem=send.at[j], recv_sem=recv.at[j], device_id=to,
                                        device_id_type=MESH)


def gather_stage1(shards, split):
    n = len(shards)
    ins = [s.reshape(2, s.shape[0] // 2, s.shape[1]) if sp else s for s, sp in zip(shards, split)]
    outs = [jax.ShapeDtypeStruct((N_CHIPS,) + a.shape, a.dtype) for a in ins]

    def start(in_refs, out_refs, send, recv, base):
        x, y, c, chips = _place()
        mine = 2 * x + y
        for i in range(n):
            src = in_refs[i].at[c] if split[i] else in_refs[i]
            dst = out_refs[i].at[mine, c] if split[i] else out_refs[i].at[mine]
            for k, (px, py) in enumerate(chips):
                _rcopy(src, dst, send, recv, base + 3 * i + k, (px, py, c)).start()

    def wait(in_refs, out_refs, send, recv, base):
        x, y, c, chips = _place()
        for i in range(n):
            src = in_refs[i].at[c] if split[i] else in_refs[i]
            for k, (px, py) in enumerate(chips):
                dst = out_refs[i].at[2 * px + py, c] if split[i] else out_refs[i].at[2 * px + py]
                _rcopy(src, dst, send, recv, base + 3 * i + k, (px, py, c)).wait()

    return Comm(ins, outs, 3 * n, start, wait)


def gather_stage2(slots, shards, split):
    n = len(slots)
    own = [s.reshape(2, s.shape[0] // 2, s.shape[1]) if sp else s for s, sp in zip(shards, split)]

    def copies(in_refs, out_refs, send, recv, base):
        x, y, c, chips = _place()
        sib = (x, y, 1 - c)
        for i in range(n):
            j = base + 4 * i
            mine = out_refs[i].at[2 * x + y]
            yield _rcopy(in_refs[n + i], mine, send, recv, j + 3, sib), _rcopy(in_refs[n + i], mine, send, recv, j + 3, sib)
            if split[i]:
                for k, (px, py) in enumerate(chips):
                    s = 2 * px + py
                    yield (_rcopy(in_refs[i].at[s, c], out_refs[i].at[s, c], send, recv, j + k, sib),
                           _rcopy(in_refs[i].at[s, c], out_refs[i].at[s, 1 - c], send, recv, j + k, sib))

    def start(*a):
        for out, _ in copies(*a):
            out.start()

    def wait(*a):
        for _, back in copies(*a):
            back.wait()

    return Comm(list(slots) + own, [jax.ShapeDtypeStruct(s.shape, s.dtype) for s in slots], 4 * n, start, wait,
                {i: i for i in range(n)})


def swap_halves(gs):
    n = len(gs)

    def copies(in_refs, out_refs, send, recv, base):
        x, y, c, _ = _place()
        return [_rcopy(in_refs[i].at[s, 1 - c], out_refs[i].at[s], send, recv, base + N_CHIPS * i + s, (x, y, 1 - c))
                for i in range(n) for s in range(N_CHIPS)]

    def start(*a):
        for cp in copies(*a):
            cp.start()

    def wait(*a):
        for cp in copies(*a):
            cp.wait()

    return Comm(gs, [jax.ShapeDtypeStruct((N_CHIPS,) + g.shape[2:], g.dtype) for g in gs], N_CHIPS * n, start, wait)


def exchange_chips(ps):
    n = len(ps)

    def start(in_refs, out_refs, send, recv, base):
        x, y, c, chips = _place()
        for i in range(n):
            for k, (px, py) in enumerate(chips):
                _rcopy(in_refs[i].at[2 * px + py], out_refs[i].at[2 * x + y], send, recv, base + 3 * i + k,
                       (px, py, c)).start()

    def wait(in_refs, out_refs, send, recv, base):
        x, y, c, chips = _place()
        for i in range(n):
            for k, (px, py) in enumerate(chips):
                _rcopy(in_refs[i].at[2 * px + py], out_refs[i].at[2 * px + py], send, recv, base + 3 * i + k,
                       (px, py, c)).wait()

    return Comm(ps, [jax.ShapeDtypeStruct(p.shape, p.dtype) for p in ps], 3 * n, start, wait)


def join_halves(fs):
    n = len(fs)

    def start(in_refs, out_refs, send, recv, base):
        x, y, c, _ = _place()
        for i in range(n):
            _rcopy(in_refs[i].at[c], out_refs[i].at[c], send, recv, base + i, (x, y, 1 - c)).start()

    def wait(in_refs, out_refs, send, recv, base):
        x, y, c, _ = _place()
        for i in range(n):
            _rcopy(in_refs[i].at[c], out_refs[i].at[1 - c], send, recv, base + i, (x, y, 1 - c)).wait()

    return Comm(fs, [jax.ShapeDtypeStruct(f.shape, f.dtype) for f in fs], n, start, wait, {i: i for i in range(n)})


def allreduce_small(v, *, name):
    m_per, n = v.shape

    def body(x_ref, sum_ref, all_ref, send_sems, recv_sems, local_sem):
        x, y, c, chips = _place()
        me, sibling = (x, y, c), (x, y, 1 - c)

        def rows(px, py, pc):
            return all_ref.at[pl.ds((4 * px + 2 * py + pc) * m_per, m_per), :]

        def copy(k, block, to, src=None):
            return pltpu.make_async_remote_copy(src_ref=rows(*block) if src is None else src, dst_ref=rows(*block),
                                                send_sem=send_sems.at[k], recv_sem=recv_sems.at[k], device_id=to,
                                                device_id_type=MESH)

        mine = pltpu.make_async_copy(x_ref, rows(*me), local_sem)
        mine.start()
        first = [copy(0, me, sibling, src=x_ref)]
        first += [copy(1 + j, me, (*chip, c), src=x_ref) for j, chip in enumerate(chips)]
        for cp in first:
            cp.start()
        passed = [copy(4 + j, (*chip, c), sibling) for j, chip in enumerate(chips)]
        for j, chip in enumerate(chips):
            copy(1 + j, (*chip, c), me).wait_recv()
            passed[j].start()
        copy(0, sibling, me).wait_recv()
        for j, chip in enumerate(chips):
            copy(4 + j, (*chip, 1 - c), me).wait_recv()
        for cp in first + passed:
            cp.wait_send()
        mine.wait()
        acc = all_ref[0:m_per, :]
        for d in range(1, N_DEV):
            acc = acc + all_ref[d * m_per:(d + 1) * m_per, :]
        sum_ref[...] = acc

    vm = pl.BlockSpec(memory_space=pltpu.VMEM)
    return pl.pallas_call(
        body, name=name, in_specs=[vm], out_specs=vm, out_shape=jax.ShapeDtypeStruct((m_per, n), F32),
        scratch_shapes=[pltpu.VMEM((N_DEV * m_per, n), F32), pltpu.SemaphoreType.DMA((7,)),
                        pltpu.SemaphoreType.DMA((7,)), pltpu.SemaphoreType.DMA],
    )(v)


def _cols_to_slots(w):
    r, c4 = w.shape
    return w.reshape(r, N_CHIPS, c4 // N_CHIPS).transpose(1, 0, 2)


def _slots_to_cols(w):
    n, r, c = w.shape
    return w.transpose(1, 0, 2).reshape(r, n * c)


def _pad_cols(a, width):
    return jnp.pad(a, ((0, 0), (0, width - a.shape[1])))


class InLayout:
    def __init__(self, q_rank, kv_rank):
        gk = GLA_HEADS * GLA_DK
        gv = GLA_HEADS * GLA_DV
        sizes = [q_rank, kv_rank, MLA_ROPE, gk, gk, gv, GLA_GATE_RANK, gv]
        names = ["zq", "zkv", "zkr", "gq", "gk", "gv", "zg", "zr"]
        starts = np.concatenate([[0], np.cumsum(sizes)[:-1]])
        self.ref = {n: (int(s), int(z)) for n, s, z in zip(names, starts, sizes)}
        self.ref_width = int(sum(sizes))
        self.order = ["gv", "zr", "zq", "gq", "gk", "zkv", "zkr", "zg"]
        self.off, self.size = {}, {}
        pos = 0
        for n in self.order:
            padded = -(-self.ref[n][1] // LANE) * LANE
            self.off[n], self.size[n] = pos, padded
            pos += padded
        self.width = pos

    def pad_weight(self, w):
        return jnp.concatenate([_pad_cols(w[:, self.ref[n][0]:self.ref[n][0] + self.ref[n][1]], self.size[n])
                                for n in self.order], axis=1)

    def unpad_grad(self, g):
        names = sorted(self.ref, key=lambda n: self.ref[n][0])
        return jnp.concatenate([g[:, self.off[n]:self.off[n] + self.ref[n][1]] for n in names], axis=1)


def _pad_q_up(w):
    r = w.shape[0]
    w = w.reshape(r, MLA_HEADS, MLA_QK)
    w = jnp.pad(w, ((0, 0), (0, 0), (0, MLA_HEAD_PAD - MLA_QK)))
    return w.reshape(r, MLA_HEADS * MLA_HEAD_PAD)


def _unpad_q_up(g):
    r = g.shape[0]
    return g.reshape(r, MLA_HEADS, MLA_HEAD_PAD)[:, :, :MLA_QK].reshape(r, MLA_HEADS * MLA_QK)


def _interleave(a, b, heads):
    s = a.shape[0]
    w = a.shape[1] // heads
    return jnp.stack([a.reshape(s, heads, w), b.reshape(s, heads, w)], axis=2).reshape(s, heads * 2 * w)


def _rope_tables(positions):
    half = MLA_ROPE // 2
    inv_freq = ROPE_THETA ** (-jnp.arange(half, dtype=F32) / half)
    ang = positions.astype(F32).reshape(-1, 1) * inv_freq
    cos, sin = jnp.cos(ang), jnp.sin(ang)
    s = ang.shape[0]
    cosf = jnp.concatenate([cos, cos, jnp.ones((s, LANE - MLA_ROPE), F32)], axis=1)
    sinf = jnp.concatenate([sin, sin, jnp.zeros((s, LANE - MLA_ROPE), F32)], axis=1)
    rot = np.zeros((LANE, LANE), np.float32)
    for j in range(half):
        rot[j + half, j] = -1.0
        rot[j, j + half] = 1.0
    return cosf, sinf, jnp.asarray(rot)


SMALL = ["ffn1_norm", "mix_norm", "q_a_norm", "kv_a_norm", "mla_q_norm", "mla_k_norm", "gla_b_gate", "gla_out_norm",
         "mem_attn_norm", "mem_norm", "mem_q_norm", "mem_k_norm", "ffn2_norm"]
BIG = ["ffn1_w_gate", "ffn1_w_up", "ffn1_w_down", "w_in", "w_q_up", "w_kv_up", "w_out", "mem_w_q", "mem_w_k",
       "mem_w_v", "mem_w_o", "ffn2_w_gate", "ffn2_w_up", "ffn2_w_down"]
COL_SHARDED = {"ffn1_w_gate", "ffn1_w_up", "w_in", "w_q_up", "w_kv_up", "gla_w_gate2", "mem_w_o", "ffn2_w_gate", "ffn2_w_up"}
WEIGHTS = ["ffn1_norm", "ffn1_w_gate", "ffn1_w_up", "ffn1_w_down", "mix_norm", "w_in", "q_a_norm", "w_q_up", "kv_a_norm",
           "w_kv_up", "mla_q_norm", "mla_k_norm", "gla_w_gate2", "gla_b_gate", "gla_out_norm", "w_out", "mem_attn_norm",
           "mem_norm", "mem_w_q", "mem_w_k", "mem_w_v", "mem_w_o", "mem_q_norm", "mem_k_norm", "ffn2_norm", "ffn2_w_gate",
           "ffn2_w_up", "ffn2_w_down"]


def _pack_small(vals, rows=8):
    flat = jnp.concatenate([v.reshape(-1).astype(F32) for v in vals])
    n = flat.shape[0]
    per = -(-n // (rows * LANE)) * LANE
    return jnp.pad(flat, (0, rows * per - n)).reshape(rows, per)


def _unpack_small(packed, shapes):
    flat = packed.reshape(-1)
    out, pos = [], 0
    for s in shapes:
        n = int(np.prod(s))
        out.append(flat[pos:pos + n].reshape(s))
        pos += n
    return out


FFN1 = ["ffn1_w_gate", "ffn1_w_up", "ffn1_w_down"]
FFN2 = ["ffn2_w_gate", "ffn2_w_up", "ffn2_w_down"]
SLOT_WEIGHTS = {"ffn1_w_gate", "ffn1_w_up", "ffn2_w_gate", "ffn2_w_up"}
MID_A = ["w_in", "w_q_up", "w_kv_up", "gla_w_gate2"]
MID_B = ["w_out", "mem_w_q", "mem_w_k", "mem_w_v", "mem_w_o"]


def _with(res, comm):
    return res if comm is not None else (res, None)


def kernel(x, mem, positions, ffn1_norm, ffn1_w_gate, ffn1_w_up, ffn1_w_down, mix_norm, w_in, q_a_norm, w_q_up, kv_a_norm, w_kv_up, mla_q_norm, mla_k_norm, gla_w_gate2, gla_b_gate, gla_out_norm, w_out, mem_attn_norm, mem_norm, mem_w_q, mem_w_k, mem_w_v, mem_w_o, mem_q_norm, mem_k_norm, ffn2_norm, ffn2_w_gate, ffn2_w_up, ffn2_w_down, loss_target, m_ffn1_norm, m_ffn1_w_gate, m_ffn1_w_up, m_ffn1_w_down, m_mix_norm, m_w_in, m_q_a_norm, m_w_q_up, m_kv_a_norm, m_w_kv_up, m_mla_q_norm, m_mla_k_norm, m_gla_w_gate2, m_gla_b_gate, m_gla_out_norm, m_w_out, m_mem_attn_norm, m_mem_norm, m_mem_w_q, m_mem_w_k, m_mem_w_v, m_mem_w_o, m_mem_q_norm, m_mem_k_norm, m_ffn2_norm, m_ffn2_w_gate, m_ffn2_w_up, m_ffn2_w_down, v_ffn1_norm, v_ffn1_w_gate, v_ffn1_w_up, v_ffn1_w_down, v_mix_norm, v_w_in, v_q_a_norm, v_w_q_up, v_kv_a_norm, v_w_kv_up, v_mla_q_norm, v_mla_k_norm, v_gla_w_gate2, v_gla_b_gate, v_gla_out_norm, v_w_out, v_mem_attn_norm, v_mem_norm, v_mem_w_q, v_mem_w_k, v_mem_w_v, v_mem_w_o, v_mem_q_norm, v_mem_k_norm, v_ffn2_norm, v_ffn2_w_gate, v_ffn2_w_up, v_ffn2_w_down):
    args = dict(locals())
    two_d = lambda a: a[0] if a.ndim == 3 else a
    W = {n: two_d(args[n]) for n in WEIGHTS}
    M1 = {n: two_d(args["m_" + n]) for n in WEIGHTS}
    V2 = {n: two_d(args["v_" + n]) for n in WEIGHTS}
    xs, mems, tgt = x[0], mem[0], loss_target[0]
    S, D = xs.shape
    chip = 2 * lax.axis_index("x") + lax.axis_index("y")

    shard16 = {n: W[n].astype(BF16) for n in BIG + ["gla_w_gate2"]}
    full = {}

    def stage1(names):
        return gather_stage1([shard16[n] for n in names], [n != "gla_w_gate2" for n in names])

    def stage2(names, slots):
        return gather_stage2(slots, [shard16[n] for n in names], [n != "gla_w_gate2" for n in names])

    def finish(names, slots):
        for n, s in zip(names, slots):
            s = s.reshape((N_CHIPS,) + shard16[n].shape)
            if n in SLOT_WEIGHTS:
                full[n] = s
            else:
                full[n] = _slots_to_cols(s) if n in COL_SHARDED else s.reshape(-1, s.shape[2])

    finish(FFN1, run_comm(stage2(FFN1, run_comm(stage1(FFN1), name="gather_ffn1")), name="pass_ffn1"))
    q_rank, kv_rank = W["w_q_up"].shape[0], W["w_kv_up"].shape[0]
    lay = InLayout(q_rank, kv_rank)
    off = lay.off
    cosf, sinf, rot = _rope_tables(positions[0])
    tri = jnp.asarray(np.tril(np.ones((CHUNK, CHUNK), np.float32)))
    gqn = W["mla_q_norm"][:, :MLA_NOPE]
    gqr = _pad_cols(W["mla_q_norm"][:, MLA_NOPE:], LANE)
    gkn = W["mla_k_norm"][:, :MLA_NOPE]
    gkr = _pad_cols(W["mla_k_norm"][:, MLA_NOPE:], LANE)
    HP = MLA_HEAD_PAD
    mla_scale = MLA_QK ** -0.5
    mem_scale = MEM_HEAD_DIM ** -0.5
    mla_w = MLA_HEADS * MLA_V
    gla_w = GLA_HEADS * GLA_DV
    mem_w = MEM_HEADS * MEM_HEAD_DIM

    n1 = row_fwd(rms_fn, [V(xs)], [W["ffn1_norm"]], [(D, BF16)], [(0, 0, 0)], name="ffn1_norm")[0]
    (gate1, up1, act1), mid_a1 = ffn_up(n1, full["ffn1_w_gate"], full["ffn1_w_up"], name="ffn1_up", comm=stage1(MID_A))
    na = len(MID_A)
    x1, got = mm([(act1, full["ffn1_w_down"])], "nn", F32, alpha=0.5, res=xs, name="ffn1_down",
                 comm=merge_comms(stage2(MID_A, mid_a1), stage1(MID_B)))
    ffn1_saved = (n1, gate1, up1, act1)
    finish(MID_A, got[:na])
    mid_b1 = got[na:]
    w_in_p = lay.pad_weight(full["w_in"])
    w_q_up_p = _pad_q_up(full["w_q_up"])
    w_gate2_p = jnp.pad(full["gla_w_gate2"], ((0, LANE - GLA_GATE_RANK), (0, 0)))
    h = row_fwd(rms_fn, [V(x1)], [W["mix_norm"]], [(D, BF16)], [(0, 0, 0)], name="mix_norm")[0]
    nb = len(MID_B)
    z, got = mm([(h, w_in_p)], "nn", F32, name="in_proj", comm=merge_comms(stage2(MID_B, mid_b1), stage1(FFN2[:1])))
    finish(MID_B, got[:nb])
    f2_gate = got[nb:]
    qa = row_fwd(rms_fn, [V(z, off["zq"], q_rank)], [W["q_a_norm"]], [(q_rank, BF16)], [(0, 0, 0)], name="q_a_norm")[0]
    kva = row_fwd(rms_fn, [V(z, off["zkv"], kv_rank)], [W["kv_a_norm"]], [(kv_rank, BF16)], [(0, 0, 0)], name="kv_a_norm")[0]
    qraw = mm([(qa, w_q_up_p)], "nn", F32, name="q_up")
    kvraw = mm([(kva, full["w_kv_up"])], "nn", F32, name="kv_up")
    tabs = [V(cosf, diff=False), V(sinf, diff=False)]
    q_rows = [V(qraw, 0, LANE, HP), V(qraw, LANE, LANE, HP)] + tabs
    k_rows = [V(kvraw, 0, LANE, HP), V(z, off["zkr"], LANE, 0)] + tabs
    qh = row_fwd(qk_prep_fn, q_rows, [gqn, gqr, rot], [(MLA_HEADS * HP, BF16)], [(0, 0, HP), (0, LANE, HP)],
                 heads=MLA_HEADS, name="q_prep")[0]
    kh = row_fwd(qk_prep_fn, k_rows, [gkn, gkr, rot], [(MLA_HEADS * HP, BF16)], [(0, 0, HP), (0, LANE, HP)],
                 heads=MLA_HEADS, name="k_prep")[0]
    mla_kw = dict(heads=MLA_HEADS, dk=HP, dv=MLA_V, v_off=1, v_hs=2, scale=mla_scale, causal=True)
    o_mla = attn_fwd(qh, kh, kvraw, name="mla_attn", **mla_kw)

    zg = z[:, off["zg"]:off["zg"] + LANE]
    pre = mm([(zg, w_gate2_p)], "nn", F32, name="gla_gate")
    la = row_fwd(gate_fn, [V(pre)], [W["gla_b_gate"]], [(pre.shape[1], F32)], [(0, 0, 0)], name="gla_log_decay")[0]
    gla_kw = dict(q_off=off["gq"], k_off=off["gk"], v_off=off["gv"])
    (o_raw, states), f2_up = gla_fwd(z, la, tri, name="gla_scan", comm=stage1(FFN2[1:2]), **gla_kw)
    gla_rows = [V(o_raw, 0, GLA_DV, GLA_DV), V(z, off["zr"], GLA_DV, GLA_DV)]
    o_gla = row_fwd(gla_out_fn, gla_rows, [W["gla_out_norm"]], [(gla_w, BF16)], [(0, 0, GLA_DV)], heads=GLA_HEADS,
                    name="gla_out")[0]
    o_cat = jnp.concatenate([o_mla, o_gla], axis=1)
    x2, got = mm([(o_cat, full["w_out"])], "nn", F32, res=x1, name="out_proj", comm=stage2(FFN2[:2], f2_gate + f2_up))
    finish(FFN2[:2], got)

    hm = row_fwd(rms_fn, [V(x2)], [W["mem_attn_norm"]], [(D, BF16)], [(0, 0, 0)], name="mem_attn_norm")[0]
    mn = row_fwd(rms_fn, [V(mems)], [W["mem_norm"]], [(D, BF16)], [(0, 0, 0)], name="mem_norm")[0]
    qm_raw = mm([(hm, full["mem_w_q"])], "nn", F32, name="mem_q")
    km_raw = mm([(mn, full["mem_w_k"])], "nn", F32, name="mem_k")
    vm = mm([(mn, full["mem_w_v"])], "nn", F32, name="mem_v")
    hd = MEM_HEAD_DIM
    qm = row_fwd(rms_fn, [V(qm_raw, 0, hd, hd)], [W["mem_q_norm"]], [(mem_w, BF16)], [(0, 0, hd)], heads=MEM_HEADS,
                 name="mem_q_norm")[0]
    km = row_fwd(rms_fn, [V(km_raw, 0, hd, hd)], [W["mem_k_norm"]], [(mem_w, BF16)], [(0, 0, hd)], heads=MEM_HEADS,
                 name="mem_k_norm")[0]
    mem_kw = dict(heads=MEM_HEADS, dk=hd, dv=hd, v_off=0, v_hs=1, scale=mem_scale, causal=False)
    om = attn_fwd(qm, km, vm, name="mem_attn", **mem_kw)
    x3 = mm([(om, full["mem_w_o"])], "nn", F32, res=x2, name="mem_o")

    n2 = row_fwd(rms_fn, [V(x3)], [W["ffn2_norm"]], [(D, BF16)], [(0, 0, 0)], name="ffn2_norm")[0]
    (gate2, up2, act2), f2_down = ffn_up(n2, full["ffn2_w_gate"], full["ffn2_w_up"], name="ffn2_up", comm=stage1(FFN2[2:]))
    finish(FFN2[2:], run_comm(stage2(FFN2[2:], f2_down), name="pass_ffn2_down"))
    y = mm([(act2, full["ffn2_w_down"])], "nn", F32, alpha=0.5, res=x3, name="ffn2_down")
    dy, loss_part = loss_head(y, tgt, name="loss_head")
    loss = lax.psum(loss_part[0, 0], ("x", "y", "c"))

    G, chip_sum, reduced = {}, {}, {}

    def to_halves(n):
        g = G[n]
        if n in SLOT_WEIGHTS:
            s = g
        else:
            s = _cols_to_slots(g) if n in COL_SHARDED else g.reshape(N_CHIPS, g.shape[0] // N_CHIPS, g.shape[1])
        return s.reshape(N_CHIPS, 2, s.shape[1] // 2, s.shape[2])

    def add2(names, halves, got):
        for n, a, b in zip(names, halves, got):
            chip_sum[n] = add_own_half(a, b, BF16, name=f"rs_add2_{n}")

    def add4_join(names, parts):
        total = [sum_chip_parts(chip_sum[n], p, name=f"rs_add4_{n}") for n, p in zip(names, parts)]
        for n, b in zip(names, run_comm(join_halves(total), name=f"rs_join_{names[0]}")):
            reduced[n] = b.reshape(W[n].shape)

    def ffn_backward(dout, xin, tag, saved, dact_comm=None, after_dact=None, after_dwd=None):
        n_, gate, up, act = saved
        nd, ng, nu = f"{tag}_w_down", f"{tag}_w_gate", f"{tag}_w_up"
        (dgate, dup), got0 = _with(ffn_dact(dout, full[nd], gate, up, 0.5, name=f"{tag}_dact", comm=dact_comm), dact_comm)
        dwd_comm = after_dact(got0) if after_dact else None
        G[nd], got1 = _with(mm([(act, dout)], "tn", F32, alpha=0.5, name=f"{tag}_dwd", tm=1408, tn=2048, tk=256,
                               comm=dwd_comm), dwd_comm)
        if after_dwd:
            after_dwd(got1)
        hd_ = to_halves(nd)
        G[ng], got_d = mm([(n_, dgate)], "tn", F32, name=f"{tag}_dwg", out_slots=True, tm=2048, tn=1408,
                          comm=swap_halves([hd_]))
        add2([nd], [hd_], got_d)
        hg = to_halves(ng)
        G[nu], got_g = mm([(n_, dup)], "tn", F32, name=f"{tag}_dwu", out_slots=True, tm=2048, tn=1408,
                          comm=swap_halves([hg]))
        add2([ng], [hg], got_g)
        hu = to_halves(nu)
        dn, (parts_d, got_u) = mm([(dgate, full[ng]), (dup, full[nu])], "nt", F32, name=f"{tag}_dn", b_slots=True,
                                  tn=1024, tk=1408,
                                  comm=merge_comms(exchange_chips([chip_sum[nd]]), swap_halves([hu])))
        add2([nu], [hu], [got_u])
        dx, G[f"{tag}_norm"] = row_bwd(rms_fn, [V(xin)], [W[f"{tag}_norm"]], [V(dn)], const_diff=[True], res=dout,
                                       name=f"{tag}_dnorm")
        add4_join([nd], [parts_d])
        return dx, exchange_chips([chip_sum[ng]]), exchange_chips([chip_sum[nu]])

    g3, ffn2_gate_xchg, ffn2_up_xchg = ffn_backward(dy, x3, "ffn2", (n2, gate2, up2, act2))

    d_om = mm([(g3, full["mem_w_o"])], "nt", F32, name="mem_o_dx")
    G["mem_w_o"] = mm([(om, g3)], "tn", F32, name="mem_o_dw")
    dqm, dkm, dvm = attn_bwd(qm, km, vm, d_om, name="mem_attn_bwd", **mem_kw)
    dqm_raw, G["mem_q_norm"] = row_bwd(rms_fn, [V(qm_raw, 0, hd, hd)], [W["mem_q_norm"]], [V(dqm, 0, hd, hd)],
                                       const_diff=[True], heads=MEM_HEADS, row_dtype=BF16, name="mem_q_norm_bwd")
    dkm_raw, G["mem_k_norm"] = row_bwd(rms_fn, [V(km_raw, 0, hd, hd)], [W["mem_k_norm"]], [V(dkm, 0, hd, hd)],
                                       const_diff=[True], heads=MEM_HEADS, row_dtype=BF16, name="mem_k_norm_bwd")
    dhm = mm([(dqm_raw, full["mem_w_q"])], "nt", F32, name="mem_q_dx")
    G["mem_w_q"] = mm([(hm, dqm_raw)], "tn", F32, name="mem_q_dw")
    dmn = mm([(dkm_raw, full["mem_w_k"]), (dvm, full["mem_w_v"])], "nt", F32, name="mem_kv_dx")
    G["mem_w_k"] = mm([(mn, dkm_raw)], "tn", F32, name="mem_k_dw")
    G["mem_w_v"] = mm([(mn, dvm)], "tn", F32, name="mem_v_dw")
    _, G["mem_norm"] = row_bwd(rms_fn, [V(mems)], [W["mem_norm"]], [V(dmn)], const_diff=[True], row_dtype=BF16,
                               name="mem_norm_bwd")
    g2, G["mem_attn_norm"] = row_bwd(rms_fn, [V(x2)], [W["mem_attn_norm"]], [V(dhm)], const_diff=[True], res=g3,
                                     name="mem_attn_norm_bwd")

    d_ocat = mm([(g2, full["w_out"])], "nt", F32, name="out_proj_dx")
    G["w_out"] = mm([(o_cat, g2)], "tn", F32, name="out_proj_dw")

    d_oraw, d_zr, G["gla_out_norm"] = row_bwd(gla_out_fn, gla_rows, [W["gla_out_norm"]],
                                              [V(d_ocat, mla_w, GLA_DV, GLA_DV)], const_diff=[True], heads=GLA_HEADS,
                                              name="gla_out_bwd")
    mid_b_halves = [to_halves(n) for n in MID_B]
    (d_gq, d_gk, d_gv, d_la), got = gla_bwd(z, la, tri, tri.T, states, d_oraw, name="gla_scan_bwd",
                                            comm=merge_comms(ffn2_gate_xchg, swap_halves(mid_b_halves)), **gla_kw)
    add4_join(["ffn2_w_gate"], got[:1])
    add2(MID_B, mid_b_halves, got[1:])
    d_pre, G["gla_b_gate"] = row_bwd(gate_fn, [V(pre)], [W["gla_b_gate"]], [V(d_la)], const_diff=[True], row_dtype=BF16,
                                     name="gla_log_decay_bwd")
    d_zg = mm([(d_pre, w_gate2_p)], "nt", BF16, name="gla_gate_dx")
    G["gla_w_gate2"] = mm([(zg, d_pre)], "tn", F32, name="gla_gate_dw")[:GLA_GATE_RANK]

    (d_qh, d_kh, d_v), ffn2_up_parts = attn_bwd(qh, kh, kvraw, d_ocat, name="mla_attn_bwd", comm=ffn2_up_xchg, **mla_kw)
    add4_join(["ffn2_w_up"], ffn2_up_parts)
    cq = [V(d_qh, 0, LANE, HP), V(d_qh, LANE, LANE, HP)]
    ck = [V(d_kh, 0, LANE, HP), V(d_kh, LANE, LANE, HP)]
    d_qn, d_qr, d_gqn, d_gqr = row_bwd(qk_prep_fn, q_rows, [gqn, gqr, rot], cq, const_diff=[True, True, False],
                                       heads=MLA_HEADS, row_dtype=BF16, name="q_prep_bwd")
    d_kn, d_zkr, d_gkn, d_gkr = row_bwd(qk_prep_fn, k_rows, [gkn, gkr, rot], ck, const_diff=[True, True, False],
                                        heads=MLA_HEADS, row_dtype=BF16, name="k_prep_bwd")
    G["mla_q_norm"] = jnp.concatenate([d_gqn, d_gqr[:, :MLA_ROPE]], axis=1)
    G["mla_k_norm"] = jnp.concatenate([d_gkn, d_gkr[:, :MLA_ROPE]], axis=1)
    d_qraw = _interleave(d_qn, d_qr, MLA_HEADS)
    d_kvraw = _interleave(d_kn, d_v.astype(BF16), MLA_HEADS)
    d_qa = mm([(d_qraw, w_q_up_p)], "nt", F32, name="q_up_dx")
    G["w_q_up"] = _unpad_q_up(mm([(qa, d_qraw)], "tn", F32, name="q_up_dw"))
    d_kva = mm([(d_kvraw, full["w_kv_up"])], "nt", F32, name="kv_up_dx")
    G["w_kv_up"] = mm([(kva, d_kvraw)], "tn", F32, name="kv_up_dw")
    d_zq, G["q_a_norm"] = row_bwd(rms_fn, [V(z, off["zq"], q_rank)], [W["q_a_norm"]], [V(d_qa)], const_diff=[True],
                                  row_dtype=BF16, name="q_a_norm_bwd")
    d_zkv, G["kv_a_norm"] = row_bwd(rms_fn, [V(z, off["zkv"], kv_rank)], [W["kv_a_norm"]], [V(d_kva)], const_diff=[True],
                                    row_dtype=BF16, name="kv_a_norm_bwd")

    seg = {"gv": d_gv, "zr": d_zr, "zq": d_zq, "gq": d_gq, "gk": d_gk, "zkv": d_zkv, "zkr": d_zkr, "zg": d_zg}
    dz = jnp.concatenate([_pad_cols(seg[n].astype(BF16), lay.size[n]) for n in lay.order], axis=1)
    dh, mid_b_parts = mm([(dz, w_in_p)], "nt", F32, name="in_proj_dx",
                         comm=exchange_chips([chip_sum[n] for n in MID_B]))
    add4_join(MID_B, mid_b_parts)
    G["w_in"] = lay.unpad_grad(mm([(h, dz)], "tn", F32, name="in_proj_dw"))
    g1, G["mix_norm"] = row_bwd(rms_fn, [V(x1)], [W["mix_norm"]], [V(dh)], const_diff=[True], res=g2, name="mix_norm_bwd")

    mid_a = [n for n in MID_A if n != "gla_w_gate2"]
    mid_a_halves = [to_halves(n) for n in mid_a]

    def mid_a_sums(got):
        add2(mid_a, mid_a_halves, got)
        return exchange_chips([chip_sum[n] for n in mid_a])

    gx, ffn1_gate_xchg, ffn1_up_xchg = ffn_backward(
        g1, xs, "ffn1", ffn1_saved, dact_comm=swap_halves(mid_a_halves), after_dact=mid_a_sums,
        after_dwd=lambda parts: add4_join(mid_a, parts))

    grad, delta, new_m, new_v = {}, {}, {}, {}

    def adam_group(names, tag, comm=None):
        res, extra = _with(adamw([(W[n], reduced[n], M1[n], V2[n]) for n in names], name=f"adamw_{tag}", comm=comm), comm)
        for n, (d_, m_, v_) in zip(names, res):
            grad[n], delta[n], new_m[n], new_v[n] = reduced[n], d_, m_, v_
        return extra

    add4_join(["ffn1_w_gate"], adam_group(FFN2, "ffn2", comm=ffn1_gate_xchg))
    add4_join(["ffn1_w_up"], adam_group(mid_a + MID_B, "mid", comm=ffn1_up_xchg))
    adam_group(FFN1, "ffn1")

    small_names = SMALL + ["gla_w_gate2"]
    small_sum = allreduce_small(_pack_small([G[n] for n in small_names]), name="allreduce_small")
    small_g = dict(zip(small_names, _unpack_small(small_sum, [G[n].shape for n in small_names])))
    shard_c = W["gla_w_gate2"].shape[1]
    grad["gla_w_gate2"] = lax.dynamic_slice_in_dim(small_g["gla_w_gate2"], chip * shard_c, shard_c, axis=1)
    pw = _pack_small([W[n] for n in SMALL] + [W["gla_w_gate2"]])
    pg = _pack_small([small_g[n] for n in SMALL] + [grad["gla_w_gate2"]])
    pm = _pack_small([M1[n] for n in SMALL] + [M1["gla_w_gate2"]])
    pv = _pack_small([V2[n] for n in SMALL] + [V2["gla_w_gate2"]])
    (pd, pnm, pnv), = adamw([(pw, pg, pm, pv)], name="adamw_small")
    shapes = [W[n].shape for n in small_names]
    for n, d_, m_, v_ in zip(small_names, _unpack_small(pd, shapes), _unpack_small(pnm, shapes), _unpack_small(pnv, shapes)):
        delta[n], new_m[n], new_v[n] = d_, m_, v_
        if n != "gla_w_gate2":
            grad[n] = small_g[n]

    lead = lambda d: [d[n].reshape(args[n].shape) for n in WEIGHTS]
    return (loss, gx[None], *lead(grad), *lead(delta), *lead(new_m), *lead(new_v))
```

```python
import functools
import math

import numpy as np
import jax
import jax.numpy as jnp
from jax import lax
from jax.experimental import pallas as pl
from jax.experimental.pallas import tpu as pltpu

F32 = jnp.float32
BF16 = jnp.bfloat16
MXU_DTYPE = jnp.bfloat16
MESH = pl.DeviceIdType.MESH
ANY = pl.BlockSpec(memory_space=pl.ANY)

LANE = 128
EPS = 1e-6
CHUNK = 64
MLA_HEADS = 8
MLA_NOPE = 128
MLA_ROPE = 64
MLA_QK = MLA_NOPE + MLA_ROPE
MLA_V = 128
MLA_HEAD_PAD = 2 * LANE
ROPE_THETA = 10000.0
GLA_HEADS = 4
GLA_DK = 128
GLA_DV = 256
GLA_GATE_RANK = 16
GLA_TAU = 16.0
MEM_HEADS = 4
MEM_HEAD_DIM = 128
N_CHIPS = 4
N_DEV = 8

ADAM_LR = 0.001
ADAM_B1 = 0.9
ADAM_B2 = 0.999
ADAM_EPS = 1e-08
ADAM_WD = 0.01
ADAM_STEP = 10

VMEM_LIMIT = 56 * 1024 * 1024


def _cparams(sem=None):
    if sem is None:
        return pltpu.CompilerParams(vmem_limit_bytes=VMEM_LIMIT)
    return pltpu.CompilerParams(dimension_semantics=sem, vmem_limit_bytes=VMEM_LIMIT)


def _tile(dim, pref, unit=LANE):
    if dim <= pref:
        return dim
    t = (pref // unit) * unit
    while t > unit and dim % t:
        t -= unit
    assert dim % t == 0, (dim, pref, unit)
    return t


class Comm:
    def __init__(self, ins, out_shapes, nsem, start, wait, aliases=None):
        self.ins, self.out_shapes, self.nsem = list(ins), list(out_shapes), nsem
        self.start, self.wait, self.aliases = start, wait, dict(aliases or {})


def merge_comms(a, b):
    ai, ao = len(a.ins), len(a.out_shapes)

    def start(ins, outs, send, recv, base):
        a.start(ins[:ai], outs[:ao], send, recv, base)
        b.start(ins[ai:], outs[ao:], send, recv, base + a.nsem)

    def wait(ins, outs, send, recv, base):
        a.wait(ins[:ai], outs[:ao], send, recv, base)
        b.wait(ins[ai:], outs[ao:], send, recv, base + a.nsem)

    aliases = dict(a.aliases)
    aliases.update({ai + i: ao + o for i, o in b.aliases.items()})
    return Comm(a.ins + b.ins, a.out_shapes + b.out_shapes, a.nsem + b.nsem, start, wait, aliases)


def run_comm(comm, *, name):
    ni, no = len(comm.ins), len(comm.out_shapes)

    def body(*refs):
        ins, outs = refs[:ni], refs[ni:ni + no]
        send, recv = refs[ni + no:]
        comm.start(ins, outs, send, recv, 0)
        comm.wait(ins, outs, send, recv, 0)

    return pl.pallas_call(
        body, name=name, in_specs=[ANY] * ni, out_specs=[ANY] * no, out_shape=comm.out_shapes,
        input_output_aliases=comm.aliases,
        scratch_shapes=[pltpu.SemaphoreType.DMA((comm.nsem,)), pltpu.SemaphoreType.DMA((comm.nsem,))])(*comm.ins)


def _pcall(body, ops, *, name, grid, in_specs, out_specs, out_shape, sem, scratch_shapes=(), comm=None):
    if comm is None:
        return pl.pallas_call(body, name=name, grid=grid, in_specs=in_specs, out_specs=out_specs, out_shape=out_shape,
                              scratch_shapes=list(scratch_shapes), compiler_params=_cparams(sem))(*ops)
    multi = isinstance(out_shape, (list, tuple))
    k_out_shape = list(out_shape) if multi else [out_shape]
    k_out_specs = list(out_specs) if multi else [out_specs]
    nki, nko, nks = len(ops), len(k_out_shape), len(scratch_shapes)
    nci, nco = len(comm.ins), len(comm.out_shapes)

    def wrapped(*refs):
        p = 0
        k_in = refs[p:p + nki]; p += nki
        c_in = refs[p:p + nci]; p += nci
        k_out = refs[p:p + nko]; p += nko
        c_out = refs[p:p + nco]; p += nco
        k_scr = refs[p:p + nks]; p += nks
        send, recv = refs[p:]
        first = pl.program_id(0) == 0
        last = pl.program_id(0) == grid[0] - 1
        for a in range(1, len(grid)):
            first = jnp.logical_and(first, pl.program_id(a) == 0)
            last = jnp.logical_and(last, pl.program_id(a) == grid[a] - 1)

        @pl.when(first)
        def _():
            comm.start(c_in, c_out, send, recv, 0)

        body(*k_in, *k_out, *k_scr)

        @pl.when(last)
        def _():
            comm.wait(c_in, c_out, send, recv, 0)

    res = pl.pallas_call(
        wrapped, name=name, grid=grid, in_specs=list(in_specs) + [ANY] * nci, out_specs=k_out_specs + [ANY] * nco,
        out_shape=k_out_shape + comm.out_shapes,
        input_output_aliases={nki + i: nko + o for i, o in comm.aliases.items()},
        scratch_shapes=list(scratch_shapes) + [pltpu.SemaphoreType.DMA((comm.nsem,)), pltpu.SemaphoreType.DMA((comm.nsem,))],
        compiler_params=_cparams(("arbitrary",) * len(grid)))(*ops, *comm.ins)
    k_res = list(res[:nko]) if multi else res[0]
    return k_res, list(res[nko:])


_DIMS = {"nn": (((1,), (0,)), ((), ())), "nt": (((1,), (1,)), ((), ())), "tn": (((0,), (0,)), ((), ()))}


def _blockspec(shape, index, rows_inner):
    return pl.BlockSpec(shape, (lambda j, i, k: index(i, j, k)) if rows_inner else index)


def mm(pairs, mode, out_dtype, *, name, alpha=1.0, res=None, tm=1024, tn=1024, tk=4096, b_slots=False, out_slots=False,
       rows_inner=False, comm=None):
    a0, b0 = pairs[0]
    if b_slots:
        b_rows, b_cols = b0.shape[1], N_CHIPS * b0.shape[2]
    else:
        b_rows, b_cols = b0.shape
    (M, K) = a0.shape[::-1] if mode == "tn" else a0.shape
    N = b_rows if mode == "nt" else b_cols
    shard = (b_cols if b_slots else N) // N_CHIPS
    tm = _tile(M, tm)
    tn = _tile(shard if (out_slots or (b_slots and mode != "nt")) else N, tn)
    tk = _tile(shard if (b_slots and mode == "nt") else K, tk)
    nk = K // tk
    npairs = len(pairs)
    dims = _DIMS[mode]
    spec = functools.partial(_blockspec, rows_inner=rows_inner)
    if mode == "tn":
        a_spec = spec((tk, tm), lambda i, j, k: (k, i))
    else:
        a_spec = spec((tm, tk), lambda i, j, k: (i, k))
    per = shard // (tk if mode == "nt" else tn)
    if mode == "nt":
        b_spec = (spec((None, tn, tk), lambda i, j, k: (k // per, j, k % per)) if b_slots else
                  spec((tn, tk), lambda i, j, k: (j, k)))
    else:
        b_spec = (spec((None, tk, tn), lambda i, j, k: (j // per, k, j % per)) if b_slots else
                  spec((tk, tn), lambda i, j, k: (k, j)))
    if out_slots:
        assert res is None and mode != "nt"
        o_spec = spec((None, tm, tn), lambda i, j, k: (j // per, i, j % per))
        out_sds = jax.ShapeDtypeStruct((N_CHIPS, M, shard), out_dtype)
    else:
        o_spec = spec((tm, tn), lambda i, j, k: (i, j))
        out_sds = jax.ShapeDtypeStruct((M, N), out_dtype)
    has_res = res is not None

    def body(*refs):
        ab = refs[:2 * npairs]
        res_ref = refs[2 * npairs] if has_res else None
        o_ref = refs[2 * npairs + int(has_res)]

        def products():
            r = None
            for p in range(npairs):
                d = lax.dot_general(ab[2 * p][...].astype(MXU_DTYPE), ab[2 * p + 1][...].astype(MXU_DTYPE), dims,
                                    preferred_element_type=F32)
                r = d if r is None else r + d
            return r

        def finish(r):
            if alpha != 1.0:
                r = r * alpha
            if has_res:
                r = res_ref[...].astype(F32) + r
            o_ref[...] = r.astype(out_dtype)

        if nk == 1:
            finish(products())
            return
        acc = refs[-1]
        k = pl.program_id(2)

        @pl.when(k == 0)
        def _():
            acc[...] = jnp.zeros_like(acc)

        acc[...] += products()

        @pl.when(k == nk - 1)
        def _():
            finish(acc[...])

    ops, specs = [], []
    for a, b in pairs:
        ops += [a, b]
        specs += [a_spec, b_spec]
    if has_res:
        ops.append(res)
        specs.append(o_spec)
    blocks = (N // tn, M // tm) if rows_inner else (M // tm, N // tn)
    return _pcall(body, ops, name=name, grid=blocks + (nk,), in_specs=specs, out_specs=o_spec, out_shape=out_sds,
                  scratch_shapes=[pltpu.VMEM((tm, tn), F32)] if nk > 1 else [],
                  sem=("parallel", "parallel", "arbitrary"), comm=comm)


def _sigmoid(x):
    return 1.0 / (1.0 + jnp.exp(-x))


def ffn_up(n, wg, wu, *, name, tm=512, tn=1408, comm=None):
    M, K = n.shape
    shard = wg.shape[2]
    N = N_CHIPS * shard
    tm, tn = _tile(M, tm), _tile(shard, tn)
    per = shard // tn
    w_spec = pl.BlockSpec((None, K, tn), lambda j, i: (j // per, 0, j % per))

    def body(n_ref, wg_ref, wu_ref, g_ref, u_ref, a_ref):
        nv = n_ref[...].astype(MXU_DTYPE)
        g = jnp.dot(nv, wg_ref[...].astype(MXU_DTYPE), preferred_element_type=F32)
        u = jnp.dot(nv, wu_ref[...].astype(MXU_DTYPE), preferred_element_type=F32)
        g_ref[...] = g.astype(g_ref.dtype)
        u_ref[...] = u.astype(u_ref.dtype)
        a_ref[...] = (g * _sigmoid(g) * u).astype(a_ref.dtype)

    o_spec = pl.BlockSpec((tm, tn), lambda j, i: (i, j))
    sds = jax.ShapeDtypeStruct((M, N), BF16)
    return _pcall(
        body, [n, wg, wu], name=name, grid=(N // tn, M // tm),
        in_specs=[pl.BlockSpec((tm, K), lambda j, i: (i, 0)), w_spec, w_spec],
        out_specs=[o_spec, o_spec, o_spec], out_shape=[sds, sds, sds], sem=("parallel", "parallel"), comm=comm)


def ffn_dact(dy, wd, gate, up, alpha, *, name, tm=512, tn=1408, comm=None):
    M, K = dy.shape
    N = wd.shape[0]
    tm, tn = _tile(M, tm), _tile(N, tn)

    def body(dy_ref, wd_ref, g_ref, u_ref, dg_ref, du_ref):
        da = lax.dot_general(dy_ref[...].astype(MXU_DTYPE), wd_ref[...].astype(MXU_DTYPE), _DIMS["nt"],
                             preferred_element_type=F32) * alpha
        g = g_ref[...].astype(F32)
        u = u_ref[...].astype(F32)
        s = _sigmoid(g)
        du_ref[...] = (da * (g * s)).astype(du_ref.dtype)
        dg_ref[...] = (da * u * (s * (1.0 + g * (1.0 - s)))).astype(dg_ref.dtype)

    o_spec = pl.BlockSpec((tm, tn), lambda j, i: (i, j))
    sds = jax.ShapeDtypeStruct((M, N), BF16)
    return _pcall(
        body, [dy, wd, gate, up], name=name, grid=(N // tn, M // tm),
        in_specs=[pl.BlockSpec((tm, K), lambda j, i: (i, 0)), pl.BlockSpec((tn, K), lambda j, i: (j, 0)), o_spec, o_spec],
        out_specs=[o_spec, o_spec], out_shape=[sds, sds], sem=("parallel", "parallel"), comm=comm)


def _window(width, off, ext):
    ww = LANE
    while ww < width:
        if ww >= ext and off // ww == (off + ext - 1) // ww and width % ww == 0:
            break
        ww *= 2
    else:
        ww = width
    return ww, off // ww, off - (off // ww) * ww


class V:
    def __init__(self, arr, off=0, w=None, hs=0, diff=True):
        self.arr, self.off, self.hs, self.diff = arr, off, hs, diff
        self.w = arr.shape[1] - off if w is None else w

    def window(self, heads, tr):
        ww, blk, inner = _window(self.arr.shape[1], self.off, (heads - 1) * self.hs + self.w)
        return pl.BlockSpec((tr, ww), lambda i, blk=blk: (i, blk)), inner


def _const_spec(c):
    return pl.BlockSpec(c.shape, lambda i: (0, 0))


def row_fwd(fn, rows, consts, outs, out_map, *, heads=1, tr=256, name):
    S = rows[0].arr.shape[0]
    tr = _tile(S, tr, 8)
    wins = [v.window(heads, tr) for v in rows]
    nr, nc = len(rows), len(consts)

    def body(*refs):
        row_refs, const_refs, out_refs = refs[:nr], refs[nr:nr + nc], refs[nr + nc:]
        cv = [c[...].astype(F32) for c in const_refs]
        for h in range(heads):
            rv = []
            for v, (_, io), r in zip(rows, wins, row_refs):
                lo = io + h * v.hs
                rv.append(r[:, lo:lo + v.w].astype(F32))
            res = fn(*rv, *cv)
            for (ai, off, hs), o in zip(out_map, res):
                lo = off + h * hs
                out_refs[ai][:, lo:lo + o.shape[1]] = o.astype(out_refs[ai].dtype)

    return pl.pallas_call(
        body, name=name, grid=(S // tr,),
        in_specs=[w[0] for w in wins] + [_const_spec(c) for c in consts],
        out_specs=[pl.BlockSpec((tr, w), lambda i: (i, 0)) for w, _ in outs],
        out_shape=[jax.ShapeDtypeStruct((S, w), d) for w, d in outs],
        compiler_params=_cparams(("parallel",)))(*[v.arr for v in rows], *consts)


def row_bwd(fn, rows, consts, cots, *, const_diff, heads=1, tr=256, res=None, row_dtype=F32, name):
    S = rows[0].arr.shape[0]
    tr = _tile(S, tr, 8)
    nr, nc, nct = len(rows), len(consts), len(cots)
    wins = [v.window(heads, tr) for v in rows]
    cwins = [v.window(heads, tr) for v in cots]
    drows = [k for k, v in enumerate(rows) if v.diff]
    dconsts = [k for k in range(nc) if const_diff[k]]
    has_res = res is not None
    ngrid = S // tr

    def body(*refs):
        row_refs = refs[:nr]
        const_refs = refs[nr:nr + nc]
        cot_refs = refs[nr + nc:nr + nc + nct]
        p = nr + nc + nct
        res_ref = refs[p] if has_res else None
        p += int(has_res)
        grow_refs = refs[p:p + len(drows)]
        gconst_refs = refs[p + len(drows):]
        i = pl.program_id(0)
        cv = [c[...].astype(F32) for c in const_refs]
        shared = [None] * len(drows)
        gc_sum = [None] * len(dconsts)
        for h in range(heads):
            rv = []
            for v, (_, io), r in zip(rows, wins, row_refs):
                lo = io + h * v.hs
                rv.append(r[:, lo:lo + v.w].astype(F32))
            ct = []
            for v, (_, io), r in zip(cots, cwins, cot_refs):
                lo = io + h * v.hs
                ct.append(r[:, lo:lo + v.w].astype(F32))

            def closed(*d):
                rr, cc = list(rv), list(cv)
                for k, val in zip(drows, d[:len(drows)]):
                    rr[k] = val
                for k, val in zip(dconsts, d[len(drows):]):
                    cc[k] = val
                return tuple(fn(*rr, *cc))

            _, vjp = jax.vjp(closed, *[rv[k] for k in drows], *[cv[k] for k in dconsts])
            grads = vjp(tuple(ct))
            for n, k in enumerate(drows):
                g = grads[n]
                if rows[k].hs == 0 and heads > 1:
                    shared[n] = g if shared[n] is None else shared[n] + g
                else:
                    if n == 0 and has_res:
                        g = g + res_ref[:, h * rows[k].w:(h + 1) * rows[k].w].astype(F32)
                    grow_refs[n][:, h * rows[k].w:(h + 1) * rows[k].w] = g.astype(row_dtype)
            for n in range(len(dconsts)):
                g = grads[len(drows) + n]
                gc_sum[n] = g if gc_sum[n] is None else gc_sum[n] + g
        for n, k in enumerate(drows):
            if shared[n] is not None:
                g = shared[n]
                if n == 0 and has_res:
                    g = g + res_ref[...].astype(F32)
                grow_refs[n][...] = g.astype(row_dtype)

        @pl.when(i == 0)
        def _():
            for n in range(len(dconsts)):
                gconst_refs[n][...] = gc_sum[n]

        @pl.when(i > 0)
        def _():
            for n in range(len(dconsts)):
                gconst_refs[n][...] += gc_sum[n]

    gw = [rows[k].w * (heads if rows[k].hs else 1) for k in drows]
    in_specs = [w[0] for w in wins] + [_const_spec(c) for c in consts] + [w[0] for w in cwins]
    ops = [v.arr for v in rows] + list(consts) + [v.arr for v in cots]
    if has_res:
        in_specs.append(pl.BlockSpec((tr, gw[0]), lambda i: (i, 0)))
        ops.append(res)
    out_specs = [pl.BlockSpec((tr, w), lambda i: (i, 0)) for w in gw]
    out_shape = [jax.ShapeDtypeStruct((S, w), row_dtype) for w in gw]
    for k in dconsts:
        out_specs.append(_const_spec(consts[k]))
        out_shape.append(jax.ShapeDtypeStruct(consts[k].shape, F32))
    del ngrid
    return pl.pallas_call(body, name=name, grid=(S // tr,), in_specs=in_specs, out_specs=out_specs,
                          out_shape=out_shape, compiler_params=_cparams(("arbitrary",)))(*ops)


def _rms(x, g, n=None):
    n = x.shape[-1] if n is None else n
    ms = jnp.sum(x * x, axis=-1, keepdims=True) * (1.0 / n)
    return x * lax.rsqrt(ms + EPS) * g


def rms_fn(x, g):
    return (_rms(x, g),)


def qk_prep_fn(nope, rope, cos, sin, gn, gr, rot):
    ms = (jnp.sum(nope * nope, axis=-1, keepdims=True) + jnp.sum(rope * rope, axis=-1, keepdims=True)) * (1.0 / MLA_QK)
    r = lax.rsqrt(ms + EPS)
    on = nope * r * gn
    orr = rope * r * gr
    turned = jnp.dot(orr, rot, precision=lax.Precision.HIGHEST, preferred_element_type=F32)
    return on, orr * cos + turned * sin


def gla_out_fn(o, zr, g):
    return (_rms(o, g) * (zr * _sigmoid(zr)),)


def gate_fn(pre, b):
    t = pre + b
    return ((jnp.minimum(t, 0.0) - jnp.log(1.0 + jnp.exp(-jnp.abs(t)))) * (1.0 / GLA_TAU),)


def _attn_probs(q_ref, k_ref, scale, q0, kext):
    s = lax.dot_general(q_ref[...].astype(MXU_DTYPE), k_ref[0:kext, :].astype(MXU_DTYPE), _DIMS["nt"],
                        preferred_element_type=F32) * scale
    if q0 is not None:
        qc = (q0 + lax.broadcasted_iota(jnp.int32, s.shape, 0)) // CHUNK
        kc = lax.broadcasted_iota(jnp.int32, s.shape, 1) // CHUNK
        s = jnp.where(kc <= qc, s, -1e30)
    m = jnp.max(s, axis=-1, keepdims=True)
    e = jnp.exp(s - m)
    return e / jnp.sum(e, axis=-1, keepdims=True)


def _per_query_block(one, causal, nq, tq, Sk):
    if not causal:
        one(None, Sk, None)
        return
    assert tq % CHUNK == 0
    for ib in range(nq):
        pl.when(pl.program_id(1) == ib)(functools.partial(one, ib * tq, min(Sk, (ib + 1) * tq), ib))


def attn_fwd(q, k, v, *, heads, dk, dv, v_off, v_hs, scale, causal, name, tq=256, comm=None):
    Sq, Sk = q.shape[0], k.shape[0]
    tq = _tile(Sq, tq, 8)

    def body(q_ref, k_ref, v_ref, o_ref):
        def one(q0, kext, ib):
            p = _attn_probs(q_ref, k_ref, scale, q0, kext)
            o_ref[...] = jnp.dot(p.astype(MXU_DTYPE), v_ref[0:kext, :].astype(MXU_DTYPE),
                                 preferred_element_type=F32).astype(o_ref.dtype)

        _per_query_block(one, causal, Sq // tq, tq, Sk)

    return _pcall(
        body, [q, k, v], name=name, grid=(heads, Sq // tq),
        in_specs=[pl.BlockSpec((tq, dk), lambda h, i: (i, h)), pl.BlockSpec((Sk, dk), lambda h, i: (0, h)),
                  pl.BlockSpec((Sk, dv), lambda h, i: (0, v_off + h * v_hs))],
        out_specs=pl.BlockSpec((tq, dv), lambda h, i: (i, h)),
        out_shape=jax.ShapeDtypeStruct((Sq, heads * dv), BF16), sem=("parallel", "parallel"), comm=comm)


def attn_bwd(q, k, v, do, *, heads, dk, dv, v_off, v_hs, scale, causal, name, tq=256, comm=None):
    Sq, Sk = q.shape[0], k.shape[0]
    tq = _tile(Sq, tq, 8)

    def body(q_ref, k_ref, v_ref, do_ref, dq_ref, dk_ref, dv_ref):
        @pl.when(pl.program_id(1) == 0)
        def _():
            dk_ref[...] = jnp.zeros_like(dk_ref)
            dv_ref[...] = jnp.zeros_like(dv_ref)

        def one(q0, kext, ib):
            p = _attn_probs(q_ref, k_ref, scale, q0, kext)
            dob = do_ref[...].astype(MXU_DTYPE)
            dp = lax.dot_general(dob, v_ref[0:kext, :].astype(MXU_DTYPE), _DIMS["nt"], preferred_element_type=F32)
            delta = jnp.sum(p * dp, axis=-1, keepdims=True)
            ds = (p * (dp - delta) * scale).astype(MXU_DTYPE)
            dq_ref[...] = jnp.dot(ds, k_ref[0:kext, :].astype(MXU_DTYPE), preferred_element_type=F32)
            dk_ref[0:kext, :] += lax.dot_general(ds, q_ref[...].astype(MXU_DTYPE), _DIMS["tn"],
                                                 preferred_element_type=F32)
            dv_ref[0:kext, :] += lax.dot_general(p.astype(MXU_DTYPE), dob, _DIMS["tn"], preferred_element_type=F32)

        _per_query_block(one, causal, Sq // tq, tq, Sk)

    return _pcall(
        body, [q, k, v, do], name=name, grid=(heads, Sq // tq),
        in_specs=[pl.BlockSpec((tq, dk), lambda h, i: (i, h)), pl.BlockSpec((Sk, dk), lambda h, i: (0, h)),
                  pl.BlockSpec((Sk, dv), lambda h, i: (0, v_off + h * v_hs)),
                  pl.BlockSpec((tq, dv), lambda h, i: (i, h))],
        out_specs=[pl.BlockSpec((tq, dk), lambda h, i: (i, h)), pl.BlockSpec((Sk, dk), lambda h, i: (0, h)),
                   pl.BlockSpec((Sk, dv), lambda h, i: (0, h))],
        out_shape=[jax.ShapeDtypeStruct((Sq, heads * dk), F32), jax.ShapeDtypeStruct((Sk, heads * dk), F32),
                   jax.ShapeDtypeStruct((Sk, heads * dv), F32)],
        sem=("parallel", "arbitrary"), comm=comm)


def _gla_chunk(k, g, tri_ref):
    b = jnp.dot(tri_ref[...], g, precision=lax.Precision.HIGHEST, preferred_element_type=F32)
    b_end = jnp.sum(g, axis=0, keepdims=True)
    e = jnp.exp(b_end - b)
    return k * e, e, jnp.exp(b_end)


def _gla_windows(z, q_off, k_off, v_off, rows_of):
    H, DK, DV = GLA_HEADS, GLA_DK, GLA_DV
    specs, inner = [], []
    for off, ext in ((q_off, H * DK), (k_off, H * DK), (v_off, H * DV)):
        ww, blk, io = _window(z.shape[1], off, ext)
        specs.append(pl.BlockSpec((CHUNK, ww), lambda c, blk=blk: (rows_of(c), blk)))
        inner.append(io)
    return specs, inner


def gla_fwd(z, la, tri, *, q_off, k_off, v_off, name, comm=None):
    S = z.shape[0]
    nchunk = S // CHUNK
    H, DK, DV = GLA_HEADS, GLA_DK, GLA_DV
    qscale = DK ** -0.5
    zspecs, (qi, ki, vi) = _gla_windows(z, q_off, k_off, v_off, lambda c: c)

    def body(q_ref, k_ref, v_ref, la_ref, tri_ref, o_ref, st_ref, state):
        @pl.when(pl.program_id(0) == 0)
        def _():
            state[...] = jnp.zeros_like(state)

        for h in range(H):
            dks, dvs = slice(h * DK, (h + 1) * DK), slice(h * DV, (h + 1) * DV)
            k = k_ref[:, ki + h * DK:ki + (h + 1) * DK].astype(F32)
            v = v_ref[:, vi + h * DV:vi + (h + 1) * DV]
            q = q_ref[:, qi + h * DK:qi + (h + 1) * DK].astype(F32)
            kdec, _, decay = _gla_chunk(k, la_ref[:, dks].astype(F32), tri_ref)
            ut = lax.dot_general(v.astype(MXU_DTYPE), kdec.astype(MXU_DTYPE), _DIMS["tn"], preferred_element_type=F32)
            new = state[h] * decay + ut
            state[h] = new
            st_ref[h] = new
            qs = (q * qscale).astype(MXU_DTYPE)
            o_ref[:, dvs] = lax.dot_general(qs, new.astype(MXU_DTYPE), _DIMS["nt"], preferred_element_type=F32)

    return _pcall(
        body, [z, z, z, la, tri], name=name, grid=(nchunk,),
        in_specs=zspecs + [pl.BlockSpec((CHUNK, H * DK), lambda c: (c, 0)), pl.BlockSpec((CHUNK, CHUNK), lambda c: (0, 0))],
        out_specs=[pl.BlockSpec((CHUNK, H * DV), lambda c: (c, 0)),
                   pl.BlockSpec((H, None, DV, DK), lambda c: (0, c, 0, 0))],
        out_shape=[jax.ShapeDtypeStruct((S, H * DV), F32), jax.ShapeDtypeStruct((H, nchunk, DV, DK), F32)],
        scratch_shapes=[pltpu.VMEM((H, DV, DK), F32)], sem=("arbitrary",), comm=comm)


def gla_bwd(z, la, tri, trit, states, do, *, q_off, k_off, v_off, name, comm=None):
    S = z.shape[0]
    nchunk = S // CHUNK
    H, DK, DV = GLA_HEADS, GLA_DK, GLA_DV
    qscale = DK ** -0.5
    last = nchunk - 1
    zspecs, (qi, ki, vi) = _gla_windows(z, q_off, k_off, v_off, lambda c: last - c)

    def body(q_ref, k_ref, v_ref, la_ref, tri_ref, trit_ref, st_ref, sp_ref, do_ref, dq_ref, dk_ref, dv_ref, dla_ref,
             dstate):
        c = pl.program_id(0)
        cc = last - c

        @pl.when(c == 0)
        def _():
            dstate[...] = jnp.zeros_like(dstate)

        for h in range(H):
            dks, dvs = slice(h * DK, (h + 1) * DK), slice(h * DV, (h + 1) * DV)
            kf = k_ref[:, ki + h * DK:ki + (h + 1) * DK].astype(F32)
            vb16 = v_ref[:, vi + h * DV:vi + (h + 1) * DV].astype(MXU_DTYPE)
            q = q_ref[:, qi + h * DK:qi + (h + 1) * DK].astype(F32)
            kdec, e, decay = _gla_chunk(kf, la_ref[:, dks].astype(F32), tri_ref)
            dob = do_ref[:, dvs].astype(MXU_DTYPE)
            stb = st_ref[h].astype(MXU_DTYPE)
            qs = (q * qscale).astype(MXU_DTYPE)
            dq_ref[:, dks] = jnp.dot(dob, stb, preferred_element_type=F32) * qscale
            dst = dstate[h] + lax.dot_general(dob, qs, _DIMS["tn"], preferred_element_type=F32)
            prev = jnp.where(cc > 0, sp_ref[h], 0.0)
            ddecay = jnp.sum(dst * prev, axis=0, keepdims=True)
            dstate[h] = dst * decay
            dub = dst.astype(MXU_DTYPE)
            dv_ref[:, dvs] = lax.dot_general(kdec.astype(MXU_DTYPE), dub, _DIMS["nt"], preferred_element_type=F32)
            dkdec = jnp.dot(vb16, dub, preferred_element_type=F32)
            dk_ref[:, dks] = dkdec * e
            w = dkdec * kf * e
            db_end = jnp.sum(w, axis=0, keepdims=True) + ddecay * decay
            dla_ref[:, dks] = db_end - jnp.dot(trit_ref[...], w, precision=lax.Precision.HIGHEST,
                                               preferred_element_type=F32)

    def rows(width):
        return pl.BlockSpec((CHUNK, width), lambda c: (last - c, 0))

    square = pl.BlockSpec((CHUNK, CHUNK), lambda c: (0, 0))
    return _pcall(
        body, [z, z, z, la, tri, trit, states, states, do], name=name, grid=(nchunk,),
        in_specs=zspecs + [rows(H * DK), square, square,
                           pl.BlockSpec((H, None, DV, DK), lambda c: (0, last - c, 0, 0)),
                           pl.BlockSpec((H, None, DV, DK), lambda c: (0, jnp.maximum(last - c - 1, 0), 0, 0)),
                           rows(H * DV)],
        out_specs=[rows(H * DK), rows(H * DK), rows(H * DV), rows(H * DK)],
        out_shape=[jax.ShapeDtypeStruct((S, H * DK), F32), jax.ShapeDtypeStruct((S, H * DK), F32),
                   jax.ShapeDtypeStruct((S, H * DV), F32), jax.ShapeDtypeStruct((S, H * DK), F32)],
        scratch_shapes=[pltpu.VMEM((H, DV, DK), F32)], sem=("arbitrary",), comm=comm)


def loss_head(y, target, *, name, tr=256):
    S, D = y.shape
    tr = _tile(S, tr, 8)

    def body(y_ref, t_ref, dy_ref, loss_ref):
        i = pl.program_id(0)
        err = y_ref[...] - t_ref[...]
        dy_ref[...] = err * (1.0 / D)
        part = jnp.zeros((1, LANE), F32) + 0.5 * jnp.sum(jnp.sum(err * err, axis=-1, keepdims=True) * (1.0 / D))

        @pl.when(i == 0)
        def _():
            loss_ref[...] = part

        @pl.when(i > 0)
        def _():
            loss_ref[...] += part

    spec = pl.BlockSpec((tr, D), lambda i: (i, 0))
    return pl.pallas_call(
        body, name=name, grid=(S // tr,), in_specs=[spec, spec],
        out_specs=[spec, pl.BlockSpec((1, LANE), lambda i: (0, 0))],
        out_shape=[jax.ShapeDtypeStruct((S, D), F32), jax.ShapeDtypeStruct((1, LANE), F32)],
        compiler_params=_cparams(("arbitrary",)))(y, target)


def _core_index():
    return lax.axis_index("c").astype(jnp.int32).reshape(1)


def _chip_slots():
    x, y, c = lax.axis_index("x"), lax.axis_index("y"), lax.axis_index("c")
    return jnp.stack([2 * x + y, 2 * (1 - x) + y, 2 * x + (1 - y), 2 * (1 - x) + (1 - y), c]).astype(jnp.int32)


def sum_chip_parts(own, parts, *, name, tr=256):
    _, R, C = own.shape
    tr = _tile(R, tr, 8)

    def body(idx_ref, o_ref, p0_ref, p1_ref, p2_ref, out_ref):
        acc = o_ref[...].astype(F32) + p0_ref[...].astype(F32)
        acc = acc + p1_ref[...].astype(F32)
        out_ref[...] = acc + p2_ref[...].astype(F32)

    def slot(k):
        return pl.BlockSpec((None, tr, C), lambda i, idx: (idx[k], i, 0))

    grid_spec = pltpu.PrefetchScalarGridSpec(num_scalar_prefetch=1, grid=(R // tr,),
                                             in_specs=[slot(0), slot(1), slot(2), slot(3)], out_specs=slot(4))
    return pl.pallas_call(body, name=name, grid_spec=grid_spec, out_shape=jax.ShapeDtypeStruct((2, R, C), F32),
                          compiler_params=_cparams(("parallel",)))(_chip_slots(), own, parts, parts, parts)


def add_own_half(g, got, out_dtype, *, name, tr=256):
    n, _, R, C = g.shape
    tr = _tile(R, tr, 8)

    def body(c_ref, a_ref, b_ref, o_ref):
        o_ref[...] = (a_ref[...].astype(F32) + b_ref[...].astype(F32)).astype(out_dtype)

    spec = pl.BlockSpec((None, tr, C), lambda s, i, c: (s, i, 0))
    grid_spec = pltpu.PrefetchScalarGridSpec(
        num_scalar_prefetch=1, grid=(n, R // tr),
        in_specs=[pl.BlockSpec((None, None, tr, C), lambda s, i, c: (s, c[0], i, 0)), spec], out_specs=spec)
    return pl.pallas_call(body, name=name, grid_spec=grid_spec, out_shape=jax.ShapeDtypeStruct((n, R, C), out_dtype),
                          compiler_params=_cparams(("parallel", "parallel")))(_core_index(), g, got)


def adamw(items, *, name, max_steps=16, comm=None):
    c1 = 1.0 / (1.0 - ADAM_B1 ** ADAM_STEP)
    c2 = 1.0 / (1.0 - ADAM_B2 ** ADAM_STEP)
    n = len(items)
    steps = max_steps
    while steps > 1 and any(it[0].shape[0] % (8 * steps) for it in items):
        steps //= 2

    def body(*refs):
        for a in range(n):
            w_ref, g_ref, m_ref, v_ref = refs[4 * a:4 * a + 4]
            go_ref, d_ref, nm_ref, nv_ref = refs[4 * n + 4 * a:4 * n + 4 * a + 4]
            gv = g_ref[...]
            go_ref[...] = gv
            nm = ADAM_B1 * m_ref[...] + (1.0 - ADAM_B1) * gv
            nv = ADAM_B2 * v_ref[...] + (1.0 - ADAM_B2) * (gv * gv)
            nm_ref[...] = nm
            nv_ref[...] = nv
            d_ref[...] = -ADAM_LR * ((nm * c1) / (jnp.sqrt(nv * c2) + ADAM_EPS) + ADAM_WD * w_ref[...])

    ops, in_specs, out_specs, out_shape = [], [], [], []
    for w, g, m, v in items:
        R, C = w.shape
        spec = pl.BlockSpec((R // steps, C), lambda i: (i, 0))
        ops += [w, g, m, v]
        in_specs += [spec] * 4
        out_specs += [spec] * 4
        out_shape += [jax.ShapeDtypeStruct((R, C), F32)] * 4
    res = _pcall(body, ops, name=name, grid=(steps,), in_specs=in_specs, out_specs=out_specs, out_shape=out_shape,
                 sem=("parallel",), comm=comm)
    flat, extra = res if comm is not None else (res, None)
    groups = [tuple(flat[4 * a:4 * a + 4]) for a in range(n)]
    return (groups, extra) if comm is not None else groups


def _place():
    x, y, c = lax.axis_index("x"), lax.axis_index("y"), lax.axis_index("c")
    chips = [(1 - x, y), (x, 1 - y), (1 - x, 1 - y)]
    return x, y, c, chips


def _rcopy(src, dst, send, recv, j, to):
    return pltpu.make_async_remote_copy(src_ref=src, dst_ref=dst, send_sem=send.at[j], recv_sem=recv.at[j], device_id=to,
                                        device_id_type=MESH)


def gather_stage1(shards, split):
    n = len(shards)
    ins = [s.reshape(2, s.shape[0] // 2, s.shape[1]) if sp else s for s, sp in zip(shards, split)]
    outs = [jax.ShapeDtypeStruct((N_CHIPS,) + a.shape, a.dtype) for a in ins]

    def start(in_refs, out_refs, send, recv, base):
        x, y, c, chips = _place()
        mine = 2 * x + y
        for i in range(n):
            src = in_refs[i].at[c] if split[i] else in_refs[i]
            dst = out_refs[i].at[mine, c] if split[i] else out_refs[i].at[mine]
            for k, (px, py) in enumerate(chips):
                _rcopy(src, dst, send, recv, base + 3 * i + k, (px, py, c)).start()

    def wait(in_refs, out_refs, send, recv, base):
        x, y, c, chips = _place()
        for i in range(n):
            src = in_refs[i].at[c] if split[i] else in_refs[i]
            for k, (px, py) in enumerate(chips):
                dst = out_refs[i].at[2 * px + py, c] if split[i] else out_refs[i].at[2 * px + py]
                _rcopy(src, dst, send, recv, base + 3 * i + k, (px, py, c)).wait()

    return Comm(ins, outs, 3 * n, start, wait)


def gather_stage2(slots, shards, split):
    n = len(slots)
    own = [s.reshape(2, s.shape[0] // 2, s.shape[1]) if sp else s for s, sp in zip(shards, split)]

    def copies(in_refs, out_refs, send, recv, base):
        x, y, c, chips = _place()
        sib = (x, y, 1 - c)
        for i in range(n):
            j = base + 4 * i
            mine = out_refs[i].at[2 * x + y]
            yield _rcopy(in_refs[n + i], mine, send, recv, j + 3, sib), _rcopy(in_refs[n + i], mine, send, recv, j + 3, sib)
            if split[i]:
                for k, (px, py) in enumerate(chips):
                    s = 2 * px + py
                    yield (_rcopy(in_refs[i].at[s, c], out_refs[i].at[s, c], send, recv, j + k, sib),
                           _rcopy(in_refs[i].at[s, c], out_refs[i].at[s, 1 - c], send, recv, j + k, sib))

    def start(*a):
        for out, _ in copies(*a):
            out.start()

    def wait(*a):
        for _, back in copies(*a):
            back.wait()

    return Comm(list(slots) + own, [jax.ShapeDtypeStruct(s.shape, s.dtype) for s in slots], 4 * n, start, wait,
                {i: i for i in range(n)})


def swap_halves(gs):
    n = len(gs)

    def copies(in_refs, out_refs, send, recv, base):
        x, y, c, _ = _place()
        return [_rcopy(in_refs[i].at[s, 1 - c], out_refs[i].at[s], send, recv, base + N_CHIPS * i + s, (x, y, 1 - c))
                for i in range(n) for s in range(N_CHIPS)]

    def start(*a):
        for cp in copies(*a):
            cp.start()

    def wait(*a):
        for cp in copies(*a):
            cp.wait()

    return Comm(gs, [jax.ShapeDtypeStruct((N_CHIPS,) + g.shape[2:], g.dtype) for g in gs], N_CHIPS * n, start, wait)


def exchange_chips(ps):
    n = len(ps)

    def start(in_refs, out_refs, send, recv, base):
        x, y, c, chips = _place()
        for i in range(n):
            for k, (px, py) in enumerate(chips):
                _rcopy(in_refs[i].at[2 * px + py], out_refs[i].at[2 * x + y], send, recv, base + 3 * i + k,
                       (px, py, c)).start()

    def wait(in_refs, out_refs, send, recv, base):
        x, y, c, chips = _place()
        for i in range(n):
            for k, (px, py) in enumerate(chips):
                _rcopy(in_refs[i].at[2 * px + py], out_refs[i].at[2 * px + py], send, recv, base + 3 * i + k,
                       (px, py, c)).wait()

    return Comm(ps, [jax.ShapeDtypeStruct(p.shape, p.dtype) for p in ps], 3 * n, start, wait)


def join_halves(fs):
    n = len(fs)

    def start(in_refs, out_refs, send, recv, base):
        x, y, c, _ = _place()
        for i in range(n):
            _rcopy(in_refs[i].at[c], out_refs[i].at[c], send, recv, base + i, (x, y, 1 - c)).start()

    def wait(in_refs, out_refs, send, recv, base):
        x, y, c, _ = _place()
        for i in range(n):
            _rcopy(in_refs[i].at[c], out_refs[i].at[1 - c], send, recv, base + i, (x, y, 1 - c)).wait()

    return Comm(fs, [jax.ShapeDtypeStruct(f.shape, f.dtype) for f in fs], n, start, wait, {i: i for i in range(n)})


def allreduce_small(v, *, name):
    m_per, n = v.shape

    def body(x_ref, sum_ref, all_ref, send_sems, recv_sems, local_sem):
        x, y, c, chips = _place()
        me, sibling = (x, y, c), (x, y, 1 - c)

        def rows(px, py, pc):
            return all_ref.at[pl.ds((4 * px + 2 * py + pc) * m_per, m_per), :]

        def copy(k, block, to, src=None):
            return pltpu.make_async_remote_copy(src_ref=rows(*block) if src is None else src, dst_ref=rows(*block),
                                                send_sem=send_sems.at[k], recv_sem=recv_sems.at[k], device_id=to,
                                                device_id_type=MESH)

        mine = pltpu.make_async_copy(x_ref, rows(*me), local_sem)
        mine.start()
        first = [copy(0, me, sibling, src=x_ref)]
        first += [copy(1 + j, me, (*chip, c), src=x_ref) for j, chip in enumerate(chips)]
        for cp in first:
            cp.start()
        passed = [copy(4 + j, (*chip, c), sibling) for j, chip in enumerate(chips)]
        for j, chip in enumerate(chips):
            copy(1 + j, (*chip, c), me).wait_recv()
            passed[j].start()
        copy(0, sibling, me).wait_recv()
        for j, chip in enumerate(chips):
            copy(4 + j, (*chip, 1 - c), me).wait_recv()
        for cp in first + passed:
            cp.wait_send()
        mine.wait()
        acc = all_ref[0:m_per, :]
        for d in range(1, N_DEV):
            acc = acc + all_ref[d * m_per:(d + 1) * m_per, :]
        sum_ref[...] = acc

    vm = pl.BlockSpec(memory_space=pltpu.VMEM)
    return pl.pallas_call(
        body, name=name, in_specs=[vm], out_specs=vm, out_shape=jax.ShapeDtypeStruct((m_per, n), F32),
        scratch_shapes=[pltpu.VMEM((N_DEV * m_per, n), F32), pltpu.SemaphoreType.DMA((7,)),
                        pltpu.SemaphoreType.DMA((7,)), pltpu.SemaphoreType.DMA],
    )(v)


def _cols_to_slots(w):
    r, c4 = w.shape
    return w.reshape(r, N_CHIPS, c4 // N_CHIPS).transpose(1, 0, 2)


def _slots_to_cols(w):
    n, r, c = w.shape
    return w.transpose(1, 0, 2).reshape(r, n * c)


def _pad_cols(a, width):
    return jnp.pad(a, ((0, 0), (0, width - a.shape[1])))


class InLayout:
    def __init__(self, q_rank, kv_rank):
        gk = GLA_HEADS * GLA_DK
        gv = GLA_HEADS * GLA_DV
        sizes = [q_rank, kv_rank, MLA_ROPE, gk, gk, gv, GLA_GATE_RANK, gv]
        names = ["zq", "zkv", "zkr", "gq", "gk", "gv", "zg", "zr"]
        starts = np.concatenate([[0], np.cumsum(sizes)[:-1]])
        self.ref = {n: (int(s), int(z)) for n, s, z in zip(names, starts, sizes)}
        self.ref_width = int(sum(sizes))
        self.order = ["gv", "zr", "zq", "gq", "gk", "zkv", "zkr", "zg"]
        self.off, self.size = {}, {}
        pos = 0
        for n in self.order:
            padded = -(-self.ref[n][1] // LANE) * LANE
            self.off[n], self.size[n] = pos, padded
            pos += padded
        self.width = pos

    def pad_weight(self, w):
        return jnp.concatenate([_pad_cols(w[:, self.ref[n][0]:self.ref[n][0] + self.ref[n][1]], self.size[n])
                                for n in self.order], axis=1)

    def unpad_grad(self, g):
        names = sorted(self.ref, key=lambda n: self.ref[n][0])
        return jnp.concatenate([g[:, self.off[n]:self.off[n] + self.ref[n][1]] for n in names], axis=1)


def _pad_q_up(w):
    r = w.shape[0]
    w = w.reshape(r, MLA_HEADS, MLA_QK)
    w = jnp.pad(w, ((0, 0), (0, 0), (0, MLA_HEAD_PAD - MLA_QK)))
    return w.reshape(r, MLA_HEADS * MLA_HEAD_PAD)


def _unpad_q_up(g):
    r = g.shape[0]
    return g.reshape(r, MLA_HEADS, MLA_HEAD_PAD)[:, :, :MLA_QK].reshape(r, MLA_HEADS * MLA_QK)


def _interleave(a, b, heads):
    s = a.shape[0]
    w = a.shape[1] // heads
    return jnp.stack([a.reshape(s, heads, w), b.reshape(s, heads, w)], axis=2).reshape(s, heads * 2 * w)


def _rope_tables(positions):
    half = MLA_ROPE // 2
    inv_freq = ROPE_THETA ** (-jnp.arange(half, dtype=F32) / half)
    ang = positions.astype(F32).reshape(-1, 1) * inv_freq
    cos, sin = jnp.cos(ang), jnp.sin(ang)
    s = ang.shape[0]
    cosf = jnp.concatenate([cos, cos, jnp.ones((s, LANE - MLA_ROPE), F32)], axis=1)
    sinf = jnp.concatenate([sin, sin, jnp.zeros((s, LANE - MLA_ROPE), F32)], axis=1)
    rot = np.zeros((LANE, LANE), np.float32)
    for j in range(half):
        rot[j + half, j] = -1.0
        rot[j, j + half] = 1.0
    return cosf, sinf, jnp.asarray(rot)


SMALL = ["ffn1_norm", "mix_norm", "q_a_norm", "kv_a_norm", "mla_q_norm", "mla_k_norm", "gla_b_gate", "gla_out_norm",
         "mem_attn_norm", "mem_norm", "mem_q_norm", "mem_k_norm", "ffn2_norm"]
BIG = ["ffn1_w_gate", "ffn1_w_up", "ffn1_w_down", "w_in", "w_q_up", "w_kv_up", "w_out", "mem_w_q", "mem_w_k",
       "mem_w_v", "mem_w_o", "ffn2_w_gate", "ffn2_w_up", "ffn2_w_down"]
COL_SHARDED = {"ffn1_w_gate", "ffn1_w_up", "w_in", "w_q_up", "w_kv_up", "gla_w_gate2", "mem_w_o", "ffn2_w_gate", "ffn2_w_up"}
WEIGHTS = ["ffn1_norm", "ffn1_w_gate", "ffn1_w_up", "ffn1_w_down", "mix_norm", "w_in", "q_a_norm", "w_q_up", "kv_a_norm",
           "w_kv_up", "mla_q_norm", "mla_k_norm", "gla_w_gate2", "gla_b_gate", "gla_out_norm", "w_out", "mem_attn_norm",
           "mem_norm", "mem_w_q", "mem_w_k", "mem_w_v", "mem_w_o", "mem_q_norm", "mem_k_norm", "ffn2_norm", "ffn2_w_gate",
           "ffn2_w_up", "ffn2_w_down"]


def _pack_small(vals, rows=8):
    flat = jnp.concatenate([v.reshape(-1).astype(F32) for v in vals])
    n = flat.shape[0]
    per = -(-n // (rows * LANE)) * LANE
    return jnp.pad(flat, (0, rows * per - n)).reshape(rows, per)


def _unpack_small(packed, shapes):
    flat = packed.reshape(-1)
    out, pos = [], 0
    for s in shapes:
        n = int(np.prod(s))
        out.append(flat[pos:pos + n].reshape(s))
        pos += n
    return out


FFN1 = ["ffn1_w_gate", "ffn1_w_up", "ffn1_w_down"]
FFN2 = ["ffn2_w_gate", "ffn2_w_up", "ffn2_w_down"]
SLOT_WEIGHTS = {"ffn1_w_gate", "ffn1_w_up", "ffn2_w_gate", "ffn2_w_up"}
MID_A = ["w_in", "w_q_up", "w_kv_up", "gla_w_gate2"]
MID_B = ["w_out", "mem_w_q", "mem_w_k", "mem_w_v", "mem_w_o"]


def _with(res, comm):
    return res if comm is not None else (res, None)


def kernel(x, mem, positions, ffn1_norm, ffn1_w_gate, ffn1_w_up, ffn1_w_down, mix_norm, w_in, q_a_norm, w_q_up, kv_a_norm, w_kv_up, mla_q_norm, mla_k_norm, gla_w_gate2, gla_b_gate, gla_out_norm, w_out, mem_attn_norm, mem_norm, mem_w_q, mem_w_k, mem_w_v, mem_w_o, mem_q_norm, mem_k_norm, ffn2_norm, ffn2_w_gate, ffn2_w_up, ffn2_w_down, loss_target, m_ffn1_norm, m_ffn1_w_gate, m_ffn1_w_up, m_ffn1_w_down, m_mix_norm, m_w_in, m_q_a_norm, m_w_q_up, m_kv_a_norm, m_w_kv_up, m_mla_q_norm, m_mla_k_norm, m_gla_w_gate2, m_gla_b_gate, m_gla_out_norm, m_w_out, m_mem_attn_norm, m_mem_norm, m_mem_w_q, m_mem_w_k, m_mem_w_v, m_mem_w_o, m_mem_q_norm, m_mem_k_norm, m_ffn2_norm, m_ffn2_w_gate, m_ffn2_w_up, m_ffn2_w_down, v_ffn1_norm, v_ffn1_w_gate, v_ffn1_w_up, v_ffn1_w_down, v_mix_norm, v_w_in, v_q_a_norm, v_w_q_up, v_kv_a_norm, v_w_kv_up, v_mla_q_norm, v_mla_k_norm, v_gla_w_gate2, v_gla_b_gate, v_gla_out_norm, v_w_out, v_mem_attn_norm, v_mem_norm, v_mem_w_q, v_mem_w_k, v_mem_w_v, v_mem_w_o, v_mem_q_norm, v_mem_k_norm, v_ffn2_norm, v_ffn2_w_gate, v_ffn2_w_up, v_ffn2_w_down):
    args = dict(locals())
    two_d = lambda a: a[0] if a.ndim == 3 else a
    W = {n: two_d(args[n]) for n in WEIGHTS}
    M1 = {n: two_d(args["m_" + n]) for n in WEIGHTS}
    V2 = {n: two_d(args["v_" + n]) for n in WEIGHTS}
    xs, mems, tgt = x[0], mem[0], loss_target[0]
    S, D = xs.shape
    chip = 2 * lax.axis_index("x") + lax.axis_index("y")

    shard16 = {n: W[n].astype(BF16) for n in BIG + ["gla_w_gate2"]}
    full = {}

    def stage1(names):
        return gather_stage1([shard16[n] for n in names], [n != "gla_w_gate2" for n in names])

    def stage2(names, slots):
        return gather_stage2(slots, [shard16[n] for n in names], [n != "gla_w_gate2" for n in names])

    def finish(names, slots):
        for n, s in zip(names, slots):
            s = s.reshape((N_CHIPS,) + shard16[n].shape)
            if n in SLOT_WEIGHTS:
                full[n] = s
            else:
                full[n] = _slots_to_cols(s) if n in COL_SHARDED else s.reshape(-1, s.shape[2])

    finish(FFN1, run_comm(stage2(FFN1, run_comm(stage1(FFN1), name="gather_ffn1")), name="pass_ffn1"))
    q_rank, kv_rank = W["w_q_up"].shape[0], W["w_kv_up"].shape[0]
    lay = InLayout(q_rank, kv_rank)
    off = lay.off
    cosf, sinf, rot = _rope_tables(positions[0])
    tri = jnp.asarray(np.tril(np.ones((CHUNK, CHUNK), np.float32)))
    gqn = W["mla_q_norm"][:, :MLA_NOPE]
    gqr = _pad_cols(W["mla_q_norm"][:, MLA_NOPE:], LANE)
    gkn = W["mla_k_norm"][:, :MLA_NOPE]
    gkr = _pad_cols(W["mla_k_norm"][:, MLA_NOPE:], LANE)
    HP = MLA_HEAD_PAD
    mla_scale = MLA_QK ** -0.5
    mem_scale = MEM_HEAD_DIM ** -0.5
    mla_w = MLA_HEADS * MLA_V
    gla_w = GLA_HEADS * GLA_DV
    mem_w = MEM_HEADS * MEM_HEAD_DIM

    n1 = row_fwd(rms_fn, [V(xs)], [W["ffn1_norm"]], [(D, BF16)], [(0, 0, 0)], name="ffn1_norm")[0]
    (gate1, up1, act1), mid_a1 = ffn_up(n1, full["ffn1_w_gate"], full["ffn1_w_up"], name="ffn1_up", comm=stage1(MID_A))
    na = len(MID_A)
    x1, got = mm([(act1, full["ffn1_w_down"])], "nn", F32, alpha=0.5, res=xs, name="ffn1_down",
                 comm=merge_comms(stage2(MID_A, mid_a1), stage1(MID_B)))
    ffn1_saved = (n1, gate1, up1, act1)
    finish(MID_A, got[:na])
    mid_b1 = got[na:]
    w_in_p = lay.pad_weight(full["w_in"])
    w_q_up_p = _pad_q_up(full["w_q_up"])
    w_gate2_p = jnp.pad(full["gla_w_gate2"], ((0, LANE - GLA_GATE_RANK), (0, 0)))
    h = row_fwd(rms_fn, [V(x1)], [W["mix_norm"]], [(D, BF16)], [(0, 0, 0)], name="mix_norm")[0]
    nb = len(MID_B)
    z, got = mm([(h, w_in_p)], "nn", F32, name="in_proj", comm=merge_comms(stage2(MID_B, mid_b1), stage1(FFN2[:1])))
    finish(MID_B, got[:nb])
    f2_gate = got[nb:]
    qa = row_fwd(rms_fn, [V(z, off["zq"], q_rank)], [W["q_a_norm"]], [(q_rank, BF16)], [(0, 0, 0)], name="q_a_norm")[0]
    kva = row_fwd(rms_fn, [V(z, off["zkv"], kv_rank)], [W["kv_a_norm"]], [(kv_rank, BF16)], [(0, 0, 0)], name="kv_a_norm")[0]
    qraw = mm([(qa, w_q_up_p)], "nn", F32, name="q_up")
    kvraw = mm([(kva, full["w_kv_up"])], "nn", F32, name="kv_up")
    tabs = [V(cosf, diff=False), V(sinf, diff=False)]
    q_rows = [V(qraw, 0, LANE, HP), V(qraw, LANE, LANE, HP)] + tabs
    k_rows = [V(kvraw, 0, LANE, HP), V(z, off["zkr"], LANE, 0)] + tabs
    qh = row_fwd(qk_prep_fn, q_rows, [gqn, gqr, rot], [(MLA_HEADS * HP, BF16)], [(0, 0, HP), (0, LANE, HP)],
                 heads=MLA_HEADS, name="q_prep")[0]
    kh = row_fwd(qk_prep_fn, k_rows, [gkn, gkr, rot], [(MLA_HEADS * HP, BF16)], [(0, 0, HP), (0, LANE, HP)],
                 heads=MLA_HEADS, name="k_prep")[0]
    mla_kw = dict(heads=MLA_HEADS, dk=HP, dv=MLA_V, v_off=1, v_hs=2, scale=mla_scale, causal=True)
    o_mla = attn_fwd(qh, kh, kvraw, name="mla_attn", **mla_kw)

    zg = z[:, off["zg"]:off["zg"] + LANE]
    pre = mm([(zg, w_gate2_p)], "nn", F32, name="gla_gate")
    la = row_fwd(gate_fn, [V(pre)], [W["gla_b_gate"]], [(pre.shape[1], F32)], [(0, 0, 0)], name="gla_log_decay")[0]
    gla_kw = dict(q_off=off["gq"], k_off=off["gk"], v_off=off["gv"])
    (o_raw, states), f2_up = gla_fwd(z, la, tri, name="gla_scan", comm=stage1(FFN2[1:2]), **gla_kw)
    gla_rows = [V(o_raw, 0, GLA_DV, GLA_DV), V(z, off["zr"], GLA_DV, GLA_DV)]
    o_gla = row_fwd(gla_out_fn, gla_rows, [W["gla_out_norm"]], [(gla_w, BF16)], [(0, 0, GLA_DV)], heads=GLA_HEADS,
                    name="gla_out")[0]
    o_cat = jnp.concatenate([o_mla, o_gla], axis=1)
    x2, got = mm([(o_cat, full["w_out"])], "nn", F32, res=x1, name="out_proj", comm=stage2(FFN2[:2], f2_gate + f2_up))
    finish(FFN2[:2], got)

    hm = row_fwd(rms_fn, [V(x2)], [W["mem_attn_norm"]], [(D, BF16)], [(0, 0, 0)], name="mem_attn_norm")[0]
    mn = row_fwd(rms_fn, [V(mems)], [W["mem_norm"]], [(D, BF16)], [(0, 0, 0)], name="mem_norm")[0]
    qm_raw = mm([(hm, full["mem_w_q"])], "nn", F32, name="mem_q")
    km_raw = mm([(mn, full["mem_w_k"])], "nn", F32, name="mem_k")
    vm = mm([(mn, full["mem_w_v"])], "nn", F32, name="mem_v")
    hd = MEM_HEAD_DIM
    qm = row_fwd(rms_fn, [V(qm_raw, 0, hd, hd)], [W["mem_q_norm"]], [(mem_w, BF16)], [(0, 0, hd)], heads=MEM_HEADS,
                 name="mem_q_norm")[0]
    km = row_fwd(rms_fn, [V(km_raw, 0, hd, hd)], [W["mem_k_norm"]], [(mem_w, BF16)], [(0, 0, hd)], heads=MEM_HEADS,
                 name="mem_k_norm")[0]
    mem_kw = dict(heads=MEM_HEADS, dk=hd, dv=hd, v_off=0, v_hs=1, scale=mem_scale, causal=False)
    om = attn_fwd(qm, km, vm, name="mem_attn", **mem_kw)
    x3 = mm([(om, full["mem_w_o"])], "nn", F32, res=x2, name="mem_o")

    n2 = row_fwd(rms_fn, [V(x3)], [W["ffn2_norm"]], [(D, BF16)], [(0, 0, 0)], name="ffn2_norm")[0]
    (gate2, up2, act2), f2_down = ffn_up(n2, full["ffn2_w_gate"], full["ffn2_w_up"], name="ffn2_up", comm=stage1(FFN2[2:]))
    finish(FFN2[2:], run_comm(stage2(FFN2[2:], f2_down), name="pass_ffn2_down"))
    y = mm([(act2, full["ffn2_w_down"])], "nn", F32, alpha=0.5, res=x3, name="ffn2_down")
    dy, loss_part = loss_head(y, tgt, name="loss_head")
    loss = lax.psum(loss_part[0, 0], ("x", "y", "c"))

    G, chip_sum, reduced = {}, {}, {}

    def to_halves(n):
        g = G[n]
        if n in SLOT_WEIGHTS:
            s = g
        else:
            s = _cols_to_slots(g) if n in COL_SHARDED else g.reshape(N_CHIPS, g.shape[0] // N_CHIPS, g.shape[1])
        return s.reshape(N_CHIPS, 2, s.shape[1] // 2, s.shape[2])

    def add2(names, halves, got):
        for n, a, b in zip(names, halves, got):
            chip_sum[n] = add_own_half(a, b, BF16, name=f"rs_add2_{n}")

    to_join = []

    def add4_join(names, parts):
        for n, p in zip(names, parts):
            to_join.append((n, sum_chip_parts(chip_sum[n], p, name=f"rs_add4_{n}")))

    def with_joins(comm):
        names, totals = [n for n, _ in to_join], [t for _, t in to_join]
        to_join.clear()
        if not names:
            return comm, lambda got: got
        own = 0 if comm is None else len(comm.out_shapes)
        joined = join_halves(totals)

        def split(got):
            for n, b in zip(names, got[own:]):
                reduced[n] = b.reshape(W[n].shape)
            return got[:own]

        return (joined if comm is None else merge_comms(comm, joined)), split

    def flush_joins():
        comm, split = with_joins(None)
        if comm is not None:
            split(run_comm(comm, name=f"rs_join_{len(reduced)}"))

    def ffn_backward(dout, xin, tag, saved, dact_comm=None, after_dact=None, after_dwd=None):
        n_, gate, up, act = saved
        nd, ng, nu = f"{tag}_w_down", f"{tag}_w_gate", f"{tag}_w_up"
        (dgate, dup), got0 = _with(ffn_dact(dout, full[nd], gate, up, 0.5, name=f"{tag}_dact", comm=dact_comm), dact_comm)
        dwd_comm = after_dact(got0) if after_dact else None
        G[nd], got1 = _with(mm([(act, dout)], "tn", F32, alpha=0.5, name=f"{tag}_dwd", tm=1408, tn=1024,
                               comm=dwd_comm), dwd_comm)
        if after_dwd:
            after_dwd(got1)
        hd_ = to_halves(nd)
        comm, split = with_joins(swap_halves([hd_]))
        G[ng], got = mm([(n_, dgate)], "tn", F32, name=f"{tag}_dwg", out_slots=True, tm=1024, tn=1408, rows_inner=True,
                        comm=comm)
        add2([nd], [hd_], split(got))
        hg = to_halves(ng)
        G[nu], got_g = mm([(n_, dup)], "tn", F32, name=f"{tag}_dwu", out_slots=True, tm=1024, tn=1408, rows_inner=True,
                          comm=swap_halves([hg]))
        add2([ng], [hg], got_g)
        hu = to_halves(nu)
        dn, (parts_d, got_u) = mm([(dgate, full[ng]), (dup, full[nu])], "nt", F32, name=f"{tag}_dn", b_slots=True,
                                  tn=1024, tk=1408,
                                  comm=merge_comms(exchange_chips([chip_sum[nd]]), swap_halves([hu])))
        add2([nu], [hu], [got_u])
        dx, G[f"{tag}_norm"] = row_bwd(rms_fn, [V(xin)], [W[f"{tag}_norm"]], [V(dn)], const_diff=[True], res=dout,
                                       name=f"{tag}_dnorm")
        add4_join([nd], [parts_d])
        return dx, exchange_chips([chip_sum[ng]]), exchange_chips([chip_sum[nu]])

    g3, ffn2_gate_xchg, ffn2_up_xchg = ffn_backward(dy, x3, "ffn2", (n2, gate2, up2, act2))

    d_om = mm([(g3, full["mem_w_o"])], "nt", F32, name="mem_o_dx")
    G["mem_w_o"] = mm([(om, g3)], "tn", F32, name="mem_o_dw")
    dqm, dkm, dvm = attn_bwd(qm, km, vm, d_om, name="mem_attn_bwd", **mem_kw)
    dqm_raw, G["mem_q_norm"] = row_bwd(rms_fn, [V(qm_raw, 0, hd, hd)], [W["mem_q_norm"]], [V(dqm, 0, hd, hd)],
                                       const_diff=[True], heads=MEM_HEADS, row_dtype=BF16, name="mem_q_norm_bwd")
    dkm_raw, G["mem_k_norm"] = row_bwd(rms_fn, [V(km_raw, 0, hd, hd)], [W["mem_k_norm"]], [V(dkm, 0, hd, hd)],
                                       const_diff=[True], heads=MEM_HEADS, row_dtype=BF16, name="mem_k_norm_bwd")
    dhm = mm([(dqm_raw, full["mem_w_q"])], "nt", F32, name="mem_q_dx")
    G["mem_w_q"] = mm([(hm, dqm_raw)], "tn", F32, name="mem_q_dw")
    dmn = mm([(dkm_raw, full["mem_w_k"]), (dvm, full["mem_w_v"])], "nt", F32, name="mem_kv_dx")
    G["mem_w_k"] = mm([(mn, dkm_raw)], "tn", F32, name="mem_k_dw")
    G["mem_w_v"] = mm([(mn, dvm)], "tn", F32, name="mem_v_dw")
    _, G["mem_norm"] = row_bwd(rms_fn, [V(mems)], [W["mem_norm"]], [V(dmn)], const_diff=[True], row_dtype=BF16,
                               name="mem_norm_bwd")
    g2, G["mem_attn_norm"] = row_bwd(rms_fn, [V(x2)], [W["mem_attn_norm"]], [V(dhm)], const_diff=[True], res=g3,
                                     name="mem_attn_norm_bwd")

    d_ocat = mm([(g2, full["w_out"])], "nt", F32, name="out_proj_dx")
    G["w_out"] = mm([(o_cat, g2)], "tn", F32, name="out_proj_dw")

    d_oraw, d_zr, G["gla_out_norm"] = row_bwd(gla_out_fn, gla_rows, [W["gla_out_norm"]],
                                              [V(d_ocat, mla_w, GLA_DV, GLA_DV)], const_diff=[True], heads=GLA_HEADS,
                                              name="gla_out_bwd")
    mid_b_halves = [to_halves(n) for n in MID_B]
    comm, split = with_joins(merge_comms(ffn2_gate_xchg, swap_halves(mid_b_halves)))
    (d_gq, d_gk, d_gv, d_la), got = gla_bwd(z, la, tri, tri.T, states, d_oraw, name="gla_scan_bwd", comm=comm, **gla_kw)
    got = split(got)
    add4_join(["ffn2_w_gate"], got[:1])
    add2(MID_B, mid_b_halves, got[1:])
    d_pre, G["gla_b_gate"] = row_bwd(gate_fn, [V(pre)], [W["gla_b_gate"]], [V(d_la)], const_diff=[True], row_dtype=BF16,
                                     name="gla_log_decay_bwd")
    d_zg = mm([(d_pre, w_gate2_p)], "nt", BF16, name="gla_gate_dx")
    G["gla_w_gate2"] = mm([(zg, d_pre)], "tn", F32, name="gla_gate_dw")[:GLA_GATE_RANK]

    comm, split = with_joins(ffn2_up_xchg)
    (d_qh, d_kh, d_v), got = attn_bwd(qh, kh, kvraw, d_ocat, name="mla_attn_bwd", comm=comm, **mla_kw)
    add4_join(["ffn2_w_up"], split(got))
    cq = [V(d_qh, 0, LANE, HP), V(d_qh, LANE, LANE, HP)]
    ck = [V(d_kh, 0, LANE, HP), V(d_kh, LANE, LANE, HP)]
    d_qn, d_qr, d_gqn, d_gqr = row_bwd(qk_prep_fn, q_rows, [gqn, gqr, rot], cq, const_diff=[True, True, False],
                                       heads=MLA_HEADS, row_dtype=BF16, name="q_prep_bwd")
    d_kn, d_zkr, d_gkn, d_gkr = row_bwd(qk_prep_fn, k_rows, [gkn, gkr, rot], ck, const_diff=[True, True, False],
                                        heads=MLA_HEADS, row_dtype=BF16, name="k_prep_bwd")
    G["mla_q_norm"] = jnp.concatenate([d_gqn, d_gqr[:, :MLA_ROPE]], axis=1)
    G["mla_k_norm"] = jnp.concatenate([d_gkn, d_gkr[:, :MLA_ROPE]], axis=1)
    d_qraw = _interleave(d_qn, d_qr, MLA_HEADS)
    d_kvraw = _interleave(d_kn, d_v.astype(BF16), MLA_HEADS)
    d_qa = mm([(d_qraw, w_q_up_p)], "nt", F32, name="q_up_dx")
    G["w_q_up"] = _unpad_q_up(mm([(qa, d_qraw)], "tn", F32, name="q_up_dw"))
    d_kva = mm([(d_kvraw, full["w_kv_up"])], "nt", F32, name="kv_up_dx")
    G["w_kv_up"] = mm([(kva, d_kvraw)], "tn", F32, name="kv_up_dw")
    d_zq, G["q_a_norm"] = row_bwd(rms_fn, [V(z, off["zq"], q_rank)], [W["q_a_norm"]], [V(d_qa)], const_diff=[True],
                                  row_dtype=BF16, name="q_a_norm_bwd")
    d_zkv, G["kv_a_norm"] = row_bwd(rms_fn, [V(z, off["zkv"], kv_rank)], [W["kv_a_norm"]], [V(d_kva)], const_diff=[True],
                                    row_dtype=BF16, name="kv_a_norm_bwd")

    seg = {"gv": d_gv, "zr": d_zr, "zq": d_zq, "gq": d_gq, "gk": d_gk, "zkv": d_zkv, "zkr": d_zkr, "zg": d_zg}
    dz = jnp.concatenate([_pad_cols(seg[n].astype(BF16), lay.size[n]) for n in lay.order], axis=1)
    comm, split = with_joins(exchange_chips([chip_sum[n] for n in MID_B]))
    dh, got = mm([(dz, w_in_p)], "nt", F32, name="in_proj_dx", comm=comm)
    add4_join(MID_B, split(got))
    G["w_in"] = lay.unpad_grad(mm([(h, dz)], "tn", F32, name="in_proj_dw"))
    g1, G["mix_norm"] = row_bwd(rms_fn, [V(x1)], [W["mix_norm"]], [V(dh)], const_diff=[True], res=g2, name="mix_norm_bwd")

    mid_a = [n for n in MID_A if n != "gla_w_gate2"]
    mid_a_halves = [to_halves(n) for n in mid_a]

    def mid_a_sums(got):
        add2(mid_a, mid_a_halves, got)
        return exchange_chips([chip_sum[n] for n in mid_a])

    gx, ffn1_gate_xchg, ffn1_up_xchg = ffn_backward(
        g1, xs, "ffn1", ffn1_saved, dact_comm=swap_halves(mid_a_halves), after_dact=mid_a_sums,
        after_dwd=lambda parts: add4_join(mid_a, parts))

    grad, delta, new_m, new_v = {}, {}, {}, {}

    def adam_group(names, tag, comm=None):
        if any(n not in reduced for n in names):
            flush_joins()
        comm, split = with_joins(comm)
        res, extra = _with(adamw([(W[n], reduced[n], M1[n], V2[n]) for n in names], name=f"adamw_{tag}", comm=comm), comm)
        for n, (g_, d_, m_, v_) in zip(names, res):
            grad[n], delta[n], new_m[n], new_v[n] = g_, d_, m_, v_
        return split(extra) if extra is not None else None

    add4_join(["ffn1_w_gate"], adam_group(FFN2, "ffn2", comm=ffn1_gate_xchg))
    add4_join(["ffn1_w_up"], adam_group(mid_a + MID_B, "mid", comm=ffn1_up_xchg))
    adam_group(FFN1, "ffn1")

    small_names = SMALL + ["gla_w_gate2"]
    small_sum = allreduce_small(_pack_small([G[n] for n in small_names]), name="allreduce_small")
    small_g = dict(zip(small_names, _unpack_small(small_sum, [G[n].shape for n in small_names])))
    shard_c = W["gla_w_gate2"].shape[1]
    grad["gla_w_gate2"] = lax.dynamic_slice_in_dim(small_g["gla_w_gate2"], chip * shard_c, shard_c, axis=1)
    pw = _pack_small([W[n] for n in SMALL] + [W["gla_w_gate2"]])
    pg = _pack_small([small_g[n] for n in SMALL] + [grad["gla_w_gate2"]])
    pm = _pack_small([M1[n] for n in SMALL] + [M1["gla_w_gate2"]])
    pv = _pack_small([V2[n] for n in SMALL] + [V2["gla_w_gate2"]])
    (_, pd, pnm, pnv), = adamw([(pw, pg, pm, pv)], name="adamw_small")
    shapes = [W[n].shape for n in small_names]
    for n, d_, m_, v_ in zip(small_names, _unpack_small(pd, shapes), _unpack_small(pnm, shapes), _unpack_small(pnv, shapes)):
        delta[n], new_m[n], new_v[n] = d_, m_, v_
        if n != "gla_w_gate2":
            grad[n] = small_g[n]

    lead = lambda d: [d[n].reshape(args[n].shape) for n in WEIGHTS]
    return (loss, gx[None], *lead(grad), *lead(delta), *lead(new_m), *lead(new_v))
```

```python
import functools
import math

import numpy as np
import jax
import jax.numpy as jnp
from jax import lax
from jax.experimental import pallas as pl
from jax.experimental.pallas import tpu as pltpu

F32 = jnp.float32
BF16 = jnp.bfloat16
MXU_DTYPE = jnp.bfloat16
MESH = pl.DeviceIdType.MESH
ANY = pl.BlockSpec(memory_space=pl.ANY)

LANE = 128
EPS = 1e-6
CHUNK = 64
MLA_HEADS = 8
MLA_NOPE = 128
MLA_ROPE = 64
MLA_QK = MLA_NOPE + MLA_ROPE
MLA_V = 128
MLA_HEAD_PAD = 2 * LANE
ROPE_THETA = 10000.0
GLA_HEADS = 4
GLA_DK = 128
GLA_DV = 256
GLA_GATE_RANK = 16
GLA_TAU = 16.0
MEM_HEADS = 4
MEM_HEAD_DIM = 128
N_CHIPS = 4
N_DEV = 8

ADAM_LR = 0.001
ADAM_B1 = 0.9
ADAM_B2 = 0.999
ADAM_EPS = 1e-08
ADAM_WD = 0.01
ADAM_STEP = 10

VMEM_LIMIT = 56 * 1024 * 1024


def _cparams(sem=None):
    if sem is None:
        return pltpu.CompilerParams(vmem_limit_bytes=VMEM_LIMIT)
    return pltpu.CompilerParams(dimension_semantics=sem, vmem_limit_bytes=VMEM_LIMIT)


def _tile(dim, pref, unit=LANE):
    if dim <= pref:
        return dim
    t = (pref // unit) * unit
    while t > unit and dim % t:
        t -= unit
    assert dim % t == 0, (dim, pref, unit)
    return t


class Comm:
    def __init__(self, ins, out_shapes, nsem, start, wait, aliases=None):
        self.ins, self.out_shapes, self.nsem = list(ins), list(out_shapes), nsem
        self.start, self.wait, self.aliases = start, wait, dict(aliases or {})


def merge_comms(a, b):
    ai, ao = len(a.ins), len(a.out_shapes)

    def start(ins, outs, send, recv, base):
        a.start(ins[:ai], outs[:ao], send, recv, base)
        b.start(ins[ai:], outs[ao:], send, recv, base + a.nsem)

    def wait(ins, outs, send, recv, base):
        a.wait(ins[:ai], outs[:ao], send, recv, base)
        b.wait(ins[ai:], outs[ao:], send, recv, base + a.nsem)

    aliases = dict(a.aliases)
    aliases.update({ai + i: ao + o for i, o in b.aliases.items()})
    return Comm(a.ins + b.ins, a.out_shapes + b.out_shapes, a.nsem + b.nsem, start, wait, aliases)


def run_comm(comm, *, name):
    ni, no = len(comm.ins), len(comm.out_shapes)

    def body(*refs):
        ins, outs = refs[:ni], refs[ni:ni + no]
        send, recv = refs[ni + no:]
        comm.start(ins, outs, send, recv, 0)
        comm.wait(ins, outs, send, recv, 0)

    return pl.pallas_call(
        body, name=name, in_specs=[ANY] * ni, out_specs=[ANY] * no, out_shape=comm.out_shapes,
        input_output_aliases=comm.aliases,
        scratch_shapes=[pltpu.SemaphoreType.DMA((comm.nsem,)), pltpu.SemaphoreType.DMA((comm.nsem,))])(*comm.ins)


def _pcall(body, ops, *, name, grid, in_specs, out_specs, out_shape, sem, scratch_shapes=(), comm=None):
    if comm is None:
        return pl.pallas_call(body, name=name, grid=grid, in_specs=in_specs, out_specs=out_specs, out_shape=out_shape,
                              scratch_shapes=list(scratch_shapes), compiler_params=_cparams(sem))(*ops)
    multi = isinstance(out_shape, (list, tuple))
    k_out_shape = list(out_shape) if multi else [out_shape]
    k_out_specs = list(out_specs) if multi else [out_specs]
    nki, nko, nks = len(ops), len(k_out_shape), len(scratch_shapes)
    nci, nco = len(comm.ins), len(comm.out_shapes)

    def wrapped(*refs):
        p = 0
        k_in = refs[p:p + nki]; p += nki
        c_in = refs[p:p + nci]; p += nci
        k_out = refs[p:p + nko]; p += nko
        c_out = refs[p:p + nco]; p += nco
        k_scr = refs[p:p + nks]; p += nks
        send, recv = refs[p:]
        first = pl.program_id(0) == 0
        last = pl.program_id(0) == grid[0] - 1
        for a in range(1, len(grid)):
            first = jnp.logical_and(first, pl.program_id(a) == 0)
            last = jnp.logical_and(last, pl.program_id(a) == grid[a] - 1)

        @pl.when(first)
        def _():
            comm.start(c_in, c_out, send, recv, 0)

        body(*k_in, *k_out, *k_scr)

        @pl.when(last)
        def _():
            comm.wait(c_in, c_out, send, recv, 0)

    res = pl.pallas_call(
        wrapped, name=name, grid=grid, in_specs=list(in_specs) + [ANY] * nci, out_specs=k_out_specs + [ANY] * nco,
        out_shape=k_out_shape + comm.out_shapes,
        input_output_aliases={nki + i: nko + o for i, o in comm.aliases.items()},
        scratch_shapes=list(scratch_shapes) + [pltpu.SemaphoreType.DMA((comm.nsem,)), pltpu.SemaphoreType.DMA((comm.nsem,))],
        compiler_params=_cparams(("arbitrary",) * len(grid)))(*ops, *comm.ins)
    k_res = list(res[:nko]) if multi else res[0]
    return k_res, list(res[nko:])


_DIMS = {"nn": (((1,), (0,)), ((), ())), "nt": (((1,), (1,)), ((), ())), "tn": (((0,), (0,)), ((), ()))}


def _blockspec(shape, index, rows_inner):
    return pl.BlockSpec(shape, (lambda j, i, k: index(i, j, k)) if rows_inner else index)


def mm(pairs, mode, out_dtype, *, name, alpha=1.0, res=None, tm=1024, tn=1024, tk=4096, b_slots=False, out_slots=False,
       rows_inner=False, comm=None):
    a0, b0 = pairs[0]
    if b_slots:
        b_rows, b_cols = b0.shape[1], N_CHIPS * b0.shape[2]
    else:
        b_rows, b_cols = b0.shape
    (M, K) = a0.shape[::-1] if mode == "tn" else a0.shape
    N = b_rows if mode == "nt" else b_cols
    shard = (b_cols if b_slots else N) // N_CHIPS
    tm = _tile(M, tm)
    tn = _tile(shard if (out_slots or (b_slots and mode != "nt")) else N, tn)
    tk = _tile(shard if (b_slots and mode == "nt") else K, tk)
    nk = K // tk
    npairs = len(pairs)
    dims = _DIMS[mode]
    spec = functools.partial(_blockspec, rows_inner=rows_inner)
    if mode == "tn":
        a_spec = spec((tk, tm), lambda i, j, k: (k, i))
    else:
        a_spec = spec((tm, tk), lambda i, j, k: (i, k))
    per = shard // (tk if mode == "nt" else tn)
    if mode == "nt":
        b_spec = (spec((None, tn, tk), lambda i, j, k: (k // per, j, k % per)) if b_slots else
                  spec((tn, tk), lambda i, j, k: (j, k)))
    else:
        b_spec = (spec((None, tk, tn), lambda i, j, k: (j // per, k, j % per)) if b_slots else
                  spec((tk, tn), lambda i, j, k: (k, j)))
    if out_slots:
        assert res is None and mode != "nt"
        o_spec = spec((None, tm, tn), lambda i, j, k: (j // per, i, j % per))
        out_sds = jax.ShapeDtypeStruct((N_CHIPS, M, shard), out_dtype)
    else:
        o_spec = spec((tm, tn), lambda i, j, k: (i, j))
        out_sds = jax.ShapeDtypeStruct((M, N), out_dtype)
    has_res = res is not None

    def body(*refs):
        ab = refs[:2 * npairs]
        res_ref = refs[2 * npairs] if has_res else None
        o_ref = refs[2 * npairs + int(has_res)]

        def products():
            r = None
            for p in range(npairs):
                d = lax.dot_general(ab[2 * p][...].astype(MXU_DTYPE), ab[2 * p + 1][...].astype(MXU_DTYPE), dims,
                                    preferred_element_type=F32)
                r = d if r is None else r + d
            return r

        def finish(r):
            if alpha != 1.0:
                r = r * alpha
            if has_res:
                r = res_ref[...].astype(F32) + r
            o_ref[...] = r.astype(out_dtype)

        if nk == 1:
            finish(products())
            return
        acc = refs[-1]
        k = pl.program_id(2)

        @pl.when(k == 0)
        def _():
            acc[...] = jnp.zeros_like(acc)

        acc[...] += products()

        @pl.when(k == nk - 1)
        def _():
            finish(acc[...])

    ops, specs = [], []
    for a, b in pairs:
        ops += [a, b]
        specs += [a_spec, b_spec]
    if has_res:
        ops.append(res)
        specs.append(o_spec)
    blocks = (N // tn, M // tm) if rows_inner else (M // tm, N // tn)
    return _pcall(body, ops, name=name, grid=blocks + (nk,), in_specs=specs, out_specs=o_spec, out_shape=out_sds,
                  scratch_shapes=[pltpu.VMEM((tm, tn), F32)] if nk > 1 else [],
                  sem=("parallel", "parallel", "arbitrary"), comm=comm)


def _sigmoid(x):
    return 1.0 / (1.0 + jnp.exp(-x))


def ffn_up(n, wg, wu, *, name, tm=512, tn=1408, comm=None):
    M, K = n.shape
    shard = wg.shape[2]
    N = N_CHIPS * shard
    tm, tn = _tile(M, tm), _tile(shard, tn)
    per = shard // tn
    w_spec = pl.BlockSpec((None, K, tn), lambda j, i: (j // per, 0, j % per))

    def body(n_ref, wg_ref, wu_ref, g_ref, u_ref, a_ref):
        nv = n_ref[...].astype(MXU_DTYPE)
        g = jnp.dot(nv, wg_ref[...].astype(MXU_DTYPE), preferred_element_type=F32)
        u = jnp.dot(nv, wu_ref[...].astype(MXU_DTYPE), preferred_element_type=F32)
        g_ref[...] = g.astype(g_ref.dtype)
        u_ref[...] = u.astype(u_ref.dtype)
        a_ref[...] = (g * _sigmoid(g) * u).astype(a_ref.dtype)

    o_spec = pl.BlockSpec((tm, tn), lambda j, i: (i, j))
    sds = jax.ShapeDtypeStruct((M, N), BF16)
    return _pcall(
        body, [n, wg, wu], name=name, grid=(N // tn, M // tm),
        in_specs=[pl.BlockSpec((tm, K), lambda j, i: (i, 0)), w_spec, w_spec],
        out_specs=[o_spec, o_spec, o_spec], out_shape=[sds, sds, sds], sem=("parallel", "parallel"), comm=comm)


def ffn_dact(dy, wd, gate, up, alpha, *, name, tm=512, tn=1408, comm=None):
    M, K = dy.shape
    N = wd.shape[0]
    tm, tn = _tile(M, tm), _tile(N, tn)

    def body(dy_ref, wd_ref, g_ref, u_ref, dg_ref, du_ref):
        da = lax.dot_general(dy_ref[...].astype(MXU_DTYPE), wd_ref[...].astype(MXU_DTYPE), _DIMS["nt"],
                             preferred_element_type=F32) * alpha
        g = g_ref[...].astype(F32)
        u = u_ref[...].astype(F32)
        s = _sigmoid(g)
        du_ref[...] = (da * (g * s)).astype(du_ref.dtype)
        dg_ref[...] = (da * u * (s * (1.0 + g * (1.0 - s)))).astype(dg_ref.dtype)

    o_spec = pl.BlockSpec((tm, tn), lambda j, i: (i, j))
    sds = jax.ShapeDtypeStruct((M, N), BF16)
    return _pcall(
        body, [dy, wd, gate, up], name=name, grid=(N // tn, M // tm),
        in_specs=[pl.BlockSpec((tm, K), lambda j, i: (i, 0)), pl.BlockSpec((tn, K), lambda j, i: (j, 0)), o_spec, o_spec],
        out_specs=[o_spec, o_spec], out_shape=[sds, sds], sem=("parallel", "parallel"), comm=comm)


def _window(width, off, ext):
    ww = LANE
    while ww < width:
        if ww >= ext and off // ww == (off + ext - 1) // ww and width % ww == 0:
            break
        ww *= 2
    else:
        ww = width
    return ww, off // ww, off - (off // ww) * ww


class V:
    def __init__(self, arr, off=0, w=None, hs=0, diff=True):
        self.arr, self.off, self.hs, self.diff = arr, off, hs, diff
        self.w = arr.shape[1] - off if w is None else w

    def window(self, heads, tr):
        ww, blk, inner = _window(self.arr.shape[1], self.off, (heads - 1) * self.hs + self.w)
        return pl.BlockSpec((tr, ww), lambda i, blk=blk: (i, blk)), inner


def _const_spec(c):
    return pl.BlockSpec(c.shape, lambda i: (0, 0))


def row_fwd(fn, rows, consts, outs, out_map, *, heads=1, tr=256, name):
    S = rows[0].arr.shape[0]
    tr = _tile(S, tr, 8)
    wins = [v.window(heads, tr) for v in rows]
    nr, nc = len(rows), len(consts)

    def body(*refs):
        row_refs, const_refs, out_refs = refs[:nr], refs[nr:nr + nc], refs[nr + nc:]
        cv = [c[...].astype(F32) for c in const_refs]
        for h in range(heads):
            rv = []
            for v, (_, io), r in zip(rows, wins, row_refs):
                lo = io + h * v.hs
                rv.append(r[:, lo:lo + v.w].astype(F32))
            res = fn(*rv, *cv)
            for (ai, off, hs), o in zip(out_map, res):
                lo = off + h * hs
                out_refs[ai][:, lo:lo + o.shape[1]] = o.astype(out_refs[ai].dtype)

    return pl.pallas_call(
        body, name=name, grid=(S // tr,),
        in_specs=[w[0] for w in wins] + [_const_spec(c) for c in consts],
        out_specs=[pl.BlockSpec((tr, w), lambda i: (i, 0)) for w, _ in outs],
        out_shape=[jax.ShapeDtypeStruct((S, w), d) for w, d in outs],
        compiler_params=_cparams(("parallel",)))(*[v.arr for v in rows], *consts)


def row_bwd(fn, rows, consts, cots, *, const_diff, heads=1, tr=256, res=None, row_dtype=F32, pack=None, pack_width=0,
            fills=(), name):
    S = rows[0].arr.shape[0]
    tr = _tile(S, tr, 8)
    pack = dict(pack or {})
    nr, nc, nct, nf = len(rows), len(consts), len(cots), len(fills)
    wins = [v.window(heads, tr) for v in rows]
    cwins = [v.window(heads, tr) for v in cots]
    fwins = [v.window(heads, tr) for v, _, _ in fills]
    drows = [k for k, v in enumerate(rows) if v.diff]
    dconsts = [k for k in range(nc) if const_diff[k]]
    has_res = res is not None
    assert not (has_res and 0 in pack)
    widths = [pack_width] if pack else []
    place = []
    for n, k in enumerate(drows):
        if n in pack:
            place.append((0,) + tuple(pack[n]))
        else:
            place.append((len(widths), 0, rows[k].w))
            widths.append(rows[k].w * (heads if rows[k].hs else 1))

    def body(*refs):
        row_refs = refs[:nr]
        const_refs = refs[nr:nr + nc]
        cot_refs = refs[nr + nc:nr + nc + nct]
        p = nr + nc + nct
        fill_refs = refs[p:p + nf]
        p += nf
        res_ref = refs[p] if has_res else None
        p += int(has_res)
        grow_refs = refs[p:p + len(widths)]
        gconst_refs = refs[p + len(widths):]
        i = pl.program_id(0)
        cv = [c[...].astype(F32) for c in const_refs]
        shared = [None] * len(drows)
        gc_sum = [None] * len(dconsts)
        for h in range(heads):
            rv = []
            for v, (_, io), r in zip(rows, wins, row_refs):
                lo = io + h * v.hs
                rv.append(r[:, lo:lo + v.w].astype(F32))
            ct = []
            for v, (_, io), r in zip(cots, cwins, cot_refs):
                lo = io + h * v.hs
                ct.append(r[:, lo:lo + v.w].astype(F32))

            def closed(*d):
                rr, cc = list(rv), list(cv)
                for k, val in zip(drows, d[:len(drows)]):
                    rr[k] = val
                for k, val in zip(dconsts, d[len(drows):]):
                    cc[k] = val
                return tuple(fn(*rr, *cc))

            _, vjp = jax.vjp(closed, *[rv[k] for k in drows], *[cv[k] for k in dconsts])
            grads = vjp(tuple(ct))
            for n, k in enumerate(drows):
                g = grads[n]
                if rows[k].hs == 0 and heads > 1:
                    shared[n] = g if shared[n] is None else shared[n] + g
                else:
                    if n == 0 and has_res:
                        g = g + res_ref[:, h * rows[k].w:(h + 1) * rows[k].w].astype(F32)
                    out, off, hs = place[n]
                    grow_refs[out][:, off + h * hs:off + h * hs + rows[k].w] = g.astype(row_dtype)
            for (v, off, hs), (_, io), r in zip(fills, fwins, fill_refs):
                lo = io + h * v.hs
                grow_refs[0][:, off + h * hs:off + h * hs + v.w] = r[:, lo:lo + v.w].astype(row_dtype)
            for n in range(len(dconsts)):
                g = grads[len(drows) + n]
                gc_sum[n] = g if gc_sum[n] is None else gc_sum[n] + g
        for n, k in enumerate(drows):
            if shared[n] is not None:
                g = shared[n]
                if n == 0 and has_res:
                    g = g + res_ref[...].astype(F32)
                grow_refs[place[n][0]][...] = g.astype(row_dtype)

        @pl.when(i == 0)
        def _():
            for n in range(len(dconsts)):
                gconst_refs[n][...] = gc_sum[n]

        @pl.when(i > 0)
        def _():
            for n in range(len(dconsts)):
                gconst_refs[n][...] += gc_sum[n]

    in_specs = [w[0] for w in wins] + [_const_spec(c) for c in consts] + [w[0] for w in cwins] + [w[0] for w in fwins]
    ops = [v.arr for v in rows] + list(consts) + [v.arr for v in cots] + [v.arr for v, _, _ in fills]
    if has_res:
        in_specs.append(pl.BlockSpec((tr, widths[0]), lambda i: (i, 0)))
        ops.append(res)
    out_specs = [pl.BlockSpec((tr, w), lambda i: (i, 0)) for w in widths]
    out_shape = [jax.ShapeDtypeStruct((S, w), row_dtype) for w in widths]
    for k in dconsts:
        out_specs.append(_const_spec(consts[k]))
        out_shape.append(jax.ShapeDtypeStruct(consts[k].shape, F32))
    return pl.pallas_call(body, name=name, grid=(S // tr,), in_specs=in_specs, out_specs=out_specs,
                          out_shape=out_shape, compiler_params=_cparams(("arbitrary",)))(*ops)


def _rms(x, g, n=None):
    n = x.shape[-1] if n is None else n
    ms = jnp.sum(x * x, axis=-1, keepdims=True) * (1.0 / n)
    return x * lax.rsqrt(ms + EPS) * g


def rms_fn(x, g):
    return (_rms(x, g),)


def qk_prep_fn(nope, rope, cos, sin, gn, gr, rot):
    ms = (jnp.sum(nope * nope, axis=-1, keepdims=True) + jnp.sum(rope * rope, axis=-1, keepdims=True)) * (1.0 / MLA_QK)
    r = lax.rsqrt(ms + EPS)
    on = nope * r * gn
    orr = rope * r * gr
    turned = jnp.dot(orr, rot, precision=lax.Precision.HIGHEST, preferred_element_type=F32)
    return on, orr * cos + turned * sin


def gla_out_fn(o, zr, g):
    return (_rms(o, g) * (zr * _sigmoid(zr)),)


def gate_fn(pre, b):
    t = pre + b
    return ((jnp.minimum(t, 0.0) - jnp.log(1.0 + jnp.exp(-jnp.abs(t)))) * (1.0 / GLA_TAU),)


def _attn_probs(q_ref, k_ref, scale, q0, kext):
    s = lax.dot_general(q_ref[...].astype(MXU_DTYPE), k_ref[0:kext, :].astype(MXU_DTYPE), _DIMS["nt"],
                        preferred_element_type=F32) * scale
    if q0 is not None:
        qc = (q0 + lax.broadcasted_iota(jnp.int32, s.shape, 0)) // CHUNK
        kc = lax.broadcasted_iota(jnp.int32, s.shape, 1) // CHUNK
        s = jnp.where(kc <= qc, s, -1e30)
    m = jnp.max(s, axis=-1, keepdims=True)
    e = jnp.exp(s - m)
    return e / jnp.sum(e, axis=-1, keepdims=True)


def _per_query_block(one, causal, nq, tq, Sk):
    if not causal:
        one(None, Sk, None)
        return
    assert tq % CHUNK == 0
    for ib in range(nq):
        pl.when(pl.program_id(1) == ib)(functools.partial(one, ib * tq, min(Sk, (ib + 1) * tq), ib))


def attn_fwd(q, k, v, *, heads, dk, dv, v_off, v_hs, scale, causal, name, tq=256, comm=None):
    Sq, Sk = q.shape[0], k.shape[0]
    tq = _tile(Sq, tq, 8)

    def body(q_ref, k_ref, v_ref, o_ref):
        def one(q0, kext, ib):
            p = _attn_probs(q_ref, k_ref, scale, q0, kext)
            o_ref[...] = jnp.dot(p.astype(MXU_DTYPE), v_ref[0:kext, :].astype(MXU_DTYPE),
                                 preferred_element_type=F32).astype(o_ref.dtype)

        _per_query_block(one, causal, Sq // tq, tq, Sk)

    return _pcall(
        body, [q, k, v], name=name, grid=(heads, Sq // tq),
        in_specs=[pl.BlockSpec((tq, dk), lambda h, i: (i, h)), pl.BlockSpec((Sk, dk), lambda h, i: (0, h)),
                  pl.BlockSpec((Sk, dv), lambda h, i: (0, v_off + h * v_hs))],
        out_specs=pl.BlockSpec((tq, dv), lambda h, i: (i, h)),
        out_shape=jax.ShapeDtypeStruct((Sq, heads * dv), BF16), sem=("parallel", "parallel"), comm=comm)


def attn_bwd(q, k, v, do, *, heads, dk, dv, v_off, v_hs, scale, causal, name, tq=256, comm=None):
    Sq, Sk = q.shape[0], k.shape[0]
    tq = _tile(Sq, tq, 8)

    def body(q_ref, k_ref, v_ref, do_ref, dq_ref, dk_ref, dv_ref):
        @pl.when(pl.program_id(1) == 0)
        def _():
            dk_ref[...] = jnp.zeros_like(dk_ref)
            dv_ref[...] = jnp.zeros_like(dv_ref)

        def one(q0, kext, ib):
            p = _attn_probs(q_ref, k_ref, scale, q0, kext)
            dob = do_ref[...].astype(MXU_DTYPE)
            dp = lax.dot_general(dob, v_ref[0:kext, :].astype(MXU_DTYPE), _DIMS["nt"], preferred_element_type=F32)
            delta = jnp.sum(p * dp, axis=-1, keepdims=True)
            ds = (p * (dp - delta) * scale).astype(MXU_DTYPE)
            dq_ref[...] = jnp.dot(ds, k_ref[0:kext, :].astype(MXU_DTYPE), preferred_element_type=F32)
            dk_ref[0:kext, :] += lax.dot_general(ds, q_ref[...].astype(MXU_DTYPE), _DIMS["tn"],
                                                 preferred_element_type=F32)
            dv_ref[0:kext, :] += lax.dot_general(p.astype(MXU_DTYPE), dob, _DIMS["tn"], preferred_element_type=F32)

        _per_query_block(one, causal, Sq // tq, tq, Sk)

    return _pcall(
        body, [q, k, v, do], name=name, grid=(heads, Sq // tq),
        in_specs=[pl.BlockSpec((tq, dk), lambda h, i: (i, h)), pl.BlockSpec((Sk, dk), lambda h, i: (0, h)),
                  pl.BlockSpec((Sk, dv), lambda h, i: (0, v_off + h * v_hs)),
                  pl.BlockSpec((tq, dv), lambda h, i: (i, h))],
        out_specs=[pl.BlockSpec((tq, dk), lambda h, i: (i, h)), pl.BlockSpec((Sk, dk), lambda h, i: (0, h)),
                   pl.BlockSpec((Sk, dv), lambda h, i: (0, h))],
        out_shape=[jax.ShapeDtypeStruct((Sq, heads * dk), F32), jax.ShapeDtypeStruct((Sk, heads * dk), F32),
                   jax.ShapeDtypeStruct((Sk, heads * dv), F32)],
        sem=("parallel", "arbitrary"), comm=comm)


def _gla_chunk(k, g, tri_ref):
    b = jnp.dot(tri_ref[...], g, precision=lax.Precision.HIGHEST, preferred_element_type=F32)
    b_end = jnp.sum(g, axis=0, keepdims=True)
    e = jnp.exp(b_end - b)
    return k * e, e, jnp.exp(b_end)


def _gla_windows(z, q_off, k_off, v_off, rows_of):
    H, DK, DV = GLA_HEADS, GLA_DK, GLA_DV
    specs, inner = [], []
    for off, ext in ((q_off, H * DK), (k_off, H * DK), (v_off, H * DV)):
        ww, blk, io = _window(z.shape[1], off, ext)
        specs.append(pl.BlockSpec((CHUNK, ww), lambda c, blk=blk: (rows_of(c), blk)))
        inner.append(io)
    return specs, inner


def gla_fwd(z, la, tri, *, q_off, k_off, v_off, name, comm=None):
    S = z.shape[0]
    nchunk = S // CHUNK
    H, DK, DV = GLA_HEADS, GLA_DK, GLA_DV
    qscale = DK ** -0.5
    zspecs, (qi, ki, vi) = _gla_windows(z, q_off, k_off, v_off, lambda c: c)

    def body(q_ref, k_ref, v_ref, la_ref, tri_ref, o_ref, st_ref, state):
        @pl.when(pl.program_id(0) == 0)
        def _():
            state[...] = jnp.zeros_like(state)

        for h in range(H):
            dks, dvs = slice(h * DK, (h + 1) * DK), slice(h * DV, (h + 1) * DV)
            k = k_ref[:, ki + h * DK:ki + (h + 1) * DK].astype(F32)
            v = v_ref[:, vi + h * DV:vi + (h + 1) * DV]
            q = q_ref[:, qi + h * DK:qi + (h + 1) * DK].astype(F32)
            kdec, _, decay = _gla_chunk(k, la_ref[:, dks].astype(F32), tri_ref)
            ut = lax.dot_general(v.astype(MXU_DTYPE), kdec.astype(MXU_DTYPE), _DIMS["tn"], preferred_element_type=F32)
            new = state[h] * decay + ut
            state[h] = new
            st_ref[h] = new
            qs = (q * qscale).astype(MXU_DTYPE)
            o_ref[:, dvs] = lax.dot_general(qs, new.astype(MXU_DTYPE), _DIMS["nt"], preferred_element_type=F32)

    return _pcall(
        body, [z, z, z, la, tri], name=name, grid=(nchunk,),
        in_specs=zspecs + [pl.BlockSpec((CHUNK, H * DK), lambda c: (c, 0)), pl.BlockSpec((CHUNK, CHUNK), lambda c: (0, 0))],
        out_specs=[pl.BlockSpec((CHUNK, H * DV), lambda c: (c, 0)),
                   pl.BlockSpec((H, None, DV, DK), lambda c: (0, c, 0, 0))],
        out_shape=[jax.ShapeDtypeStruct((S, H * DV), F32), jax.ShapeDtypeStruct((H, nchunk, DV, DK), F32)],
        scratch_shapes=[pltpu.VMEM((H, DV, DK), F32)], sem=("arbitrary",), comm=comm)


def gla_bwd(z, la, tri, trit, states, do, *, q_off, k_off, v_off, name, comm=None):
    S = z.shape[0]
    nchunk = S // CHUNK
    H, DK, DV = GLA_HEADS, GLA_DK, GLA_DV
    qscale = DK ** -0.5
    last = nchunk - 1
    zspecs, (qi, ki, vi) = _gla_windows(z, q_off, k_off, v_off, lambda c: last - c)

    def body(q_ref, k_ref, v_ref, la_ref, tri_ref, trit_ref, st_ref, sp_ref, do_ref, dq_ref, dk_ref, dv_ref, dla_ref,
             dstate):
        c = pl.program_id(0)
        cc = last - c

        @pl.when(c == 0)
        def _():
            dstate[...] = jnp.zeros_like(dstate)

        for h in range(H):
            dks, dvs = slice(h * DK, (h + 1) * DK), slice(h * DV, (h + 1) * DV)
            kf = k_ref[:, ki + h * DK:ki + (h + 1) * DK].astype(F32)
            vb16 = v_ref[:, vi + h * DV:vi + (h + 1) * DV].astype(MXU_DTYPE)
            q = q_ref[:, qi + h * DK:qi + (h + 1) * DK].astype(F32)
            kdec, e, decay = _gla_chunk(kf, la_ref[:, dks].astype(F32), tri_ref)
            dob = do_ref[:, dvs].astype(MXU_DTYPE)
            stb = st_ref[h].astype(MXU_DTYPE)
            qs = (q * qscale).astype(MXU_DTYPE)
            dq_ref[:, dks] = jnp.dot(dob, stb, preferred_element_type=F32) * qscale
            dst = dstate[h] + lax.dot_general(dob, qs, _DIMS["tn"], preferred_element_type=F32)
            prev = jnp.where(cc > 0, sp_ref[h], 0.0)
            ddecay = jnp.sum(dst * prev, axis=0, keepdims=True)
            dstate[h] = dst * decay
            dub = dst.astype(MXU_DTYPE)
            dv_ref[:, dvs] = lax.dot_general(kdec.astype(MXU_DTYPE), dub, _DIMS["nt"], preferred_element_type=F32)
            dkdec = jnp.dot(vb16, dub, preferred_element_type=F32)
            dk_ref[:, dks] = dkdec * e
            w = dkdec * kf * e
            db_end = jnp.sum(w, axis=0, keepdims=True) + ddecay * decay
            dla_ref[:, dks] = db_end - jnp.dot(trit_ref[...], w, precision=lax.Precision.HIGHEST,
                                               preferred_element_type=F32)

    def rows(width):
        return pl.BlockSpec((CHUNK, width), lambda c: (last - c, 0))

    square = pl.BlockSpec((CHUNK, CHUNK), lambda c: (0, 0))
    return _pcall(
        body, [z, z, z, la, tri, trit, states, states, do], name=name, grid=(nchunk,),
        in_specs=zspecs + [rows(H * DK), square, square,
                           pl.BlockSpec((H, None, DV, DK), lambda c: (0, last - c, 0, 0)),
                           pl.BlockSpec((H, None, DV, DK), lambda c: (0, jnp.maximum(last - c - 1, 0), 0, 0)),
                           rows(H * DV)],
        out_specs=[rows(H * DK), rows(H * DK), rows(H * DV), rows(H * DK)],
        out_shape=[jax.ShapeDtypeStruct((S, H * DK), F32), jax.ShapeDtypeStruct((S, H * DK), F32),
                   jax.ShapeDtypeStruct((S, H * DV), F32), jax.ShapeDtypeStruct((S, H * DK), F32)],
        scratch_shapes=[pltpu.VMEM((H, DV, DK), F32)], sem=("arbitrary",), comm=comm)


def loss_head(y, target, *, name, tr=256):
    S, D = y.shape
    tr = _tile(S, tr, 8)

    def body(y_ref, t_ref, dy_ref, loss_ref):
        i = pl.program_id(0)
        err = y_ref[...] - t_ref[...]
        dy_ref[...] = err * (1.0 / D)
        part = jnp.zeros((1, LANE), F32) + 0.5 * jnp.sum(jnp.sum(err * err, axis=-1, keepdims=True) * (1.0 / D))

        @pl.when(i == 0)
        def _():
            loss_ref[...] = part

        @pl.when(i > 0)
        def _():
            loss_ref[...] += part

    spec = pl.BlockSpec((tr, D), lambda i: (i, 0))
    return pl.pallas_call(
        body, name=name, grid=(S // tr,), in_specs=[spec, spec],
        out_specs=[spec, pl.BlockSpec((1, LANE), lambda i: (0, 0))],
        out_shape=[jax.ShapeDtypeStruct((S, D), F32), jax.ShapeDtypeStruct((1, LANE), F32)],
        compiler_params=_cparams(("arbitrary",)))(y, target)


def _core_index():
    return lax.axis_index("c").astype(jnp.int32).reshape(1)


def _chip_slots():
    x, y, c = lax.axis_index("x"), lax.axis_index("y"), lax.axis_index("c")
    return jnp.stack([2 * x + y, 2 * (1 - x) + y, 2 * x + (1 - y), 2 * (1 - x) + (1 - y), c]).astype(jnp.int32)


def sum_chip_parts(own, parts, *, name, tr=256):
    _, R, C = own.shape
    tr = _tile(R, tr, 8)

    def body(idx_ref, o_ref, p0_ref, p1_ref, p2_ref, out_ref):
        acc = o_ref[...].astype(F32) + p0_ref[...].astype(F32)
        acc = acc + p1_ref[...].astype(F32)
        out_ref[...] = acc + p2_ref[...].astype(F32)

    def slot(k):
        return pl.BlockSpec((None, tr, C), lambda i, idx: (idx[k], i, 0))

    grid_spec = pltpu.PrefetchScalarGridSpec(num_scalar_prefetch=1, grid=(R // tr,),
                                             in_specs=[slot(0), slot(1), slot(2), slot(3)], out_specs=slot(4))
    return pl.pallas_call(body, name=name, grid_spec=grid_spec, out_shape=jax.ShapeDtypeStruct((2, R, C), F32),
                          compiler_params=_cparams(("parallel",)))(_chip_slots(), own, parts, parts, parts)


def add_own_half(g, got, out_dtype, *, name, tr=256):
    n, _, R, C = g.shape
    tr = _tile(R, tr, 8)

    def body(c_ref, a_ref, b_ref, o_ref):
        o_ref[...] = (a_ref[...].astype(F32) + b_ref[...].astype(F32)).astype(out_dtype)

    spec = pl.BlockSpec((None, tr, C), lambda s, i, c: (s, i, 0))
    grid_spec = pltpu.PrefetchScalarGridSpec(
        num_scalar_prefetch=1, grid=(n, R // tr),
        in_specs=[pl.BlockSpec((None, None, tr, C), lambda s, i, c: (s, c[0], i, 0)), spec], out_specs=spec)
    return pl.pallas_call(body, name=name, grid_spec=grid_spec, out_shape=jax.ShapeDtypeStruct((n, R, C), out_dtype),
                          compiler_params=_cparams(("parallel", "parallel")))(_core_index(), g, got)


def adamw(items, *, name, max_steps=16, comm=None):
    c1 = 1.0 / (1.0 - ADAM_B1 ** ADAM_STEP)
    c2 = 1.0 / (1.0 - ADAM_B2 ** ADAM_STEP)
    n = len(items)
    steps = max_steps
    while steps > 1 and any(it[0].shape[0] % (8 * steps) for it in items):
        steps //= 2

    def body(*refs):
        for a in range(n):
            w_ref, g_ref, m_ref, v_ref = refs[4 * a:4 * a + 4]
            go_ref, d_ref, nm_ref, nv_ref = refs[4 * n + 4 * a:4 * n + 4 * a + 4]
            gv = g_ref[...]
            go_ref[...] = gv
            nm = ADAM_B1 * m_ref[...] + (1.0 - ADAM_B1) * gv
            nv = ADAM_B2 * v_ref[...] + (1.0 - ADAM_B2) * (gv * gv)
            nm_ref[...] = nm
            nv_ref[...] = nv
            d_ref[...] = -ADAM_LR * ((nm * c1) / (jnp.sqrt(nv * c2) + ADAM_EPS) + ADAM_WD * w_ref[...])

    ops, in_specs, out_specs, out_shape = [], [], [], []
    for w, g, m, v in items:
        R, C = w.shape
        spec = pl.BlockSpec((R // steps, C), lambda i: (i, 0))
        ops += [w, g, m, v]
        in_specs += [spec] * 4
        out_specs += [spec] * 4
        out_shape += [jax.ShapeDtypeStruct((R, C), F32)] * 4
    res = _pcall(body, ops, name=name, grid=(steps,), in_specs=in_specs, out_specs=out_specs, out_shape=out_shape,
                 sem=("parallel",), comm=comm)
    flat, extra = res if comm is not None else (res, None)
    groups = [tuple(flat[4 * a:4 * a + 4]) for a in range(n)]
    return (groups, extra) if comm is not None else groups


def _place():
    x, y, c = lax.axis_index("x"), lax.axis_index("y"), lax.axis_index("c")
    chips = [(1 - x, y), (x, 1 - y), (1 - x, 1 - y)]
    return x, y, c, chips


def _rcopy(src, dst, send, recv, j, to):
    return pltpu.make_async_remote_copy(src_ref=src, dst_ref=dst, send_sem=send.at[j], recv_sem=recv.at[j], device_id=to,
                                        device_id_type=MESH)


def gather_stage1(shards, split):
    n = len(shards)
    ins = [s.reshape(2, s.shape[0] // 2, s.shape[1]) if sp else s for s, sp in zip(shards, split)]
    outs = [jax.ShapeDtypeStruct((N_CHIPS,) + a.shape, a.dtype) for a in ins]

    def start(in_refs, out_refs, send, recv, base):
        x, y, c, chips = _place()
        mine = 2 * x + y
        for i in range(n):
            src = in_refs[i].at[c] if split[i] else in_refs[i]
            dst = out_refs[i].at[mine, c] if split[i] else out_refs[i].at[mine]
            for k, (px, py) in enumerate(chips):
                _rcopy(src, dst, send, recv, base + 3 * i + k, (px, py, c)).start()

    def wait(in_refs, out_refs, send, recv, base):
        x, y, c, chips = _place()
        for i in range(n):
            src = in_refs[i].at[c] if split[i] else in_refs[i]
            for k, (px, py) in enumerate(chips):
                dst = out_refs[i].at[2 * px + py, c] if split[i] else out_refs[i].at[2 * px + py]
                _rcopy(src, dst, send, recv, base + 3 * i + k, (px, py, c)).wait()

    return Comm(ins, outs, 3 * n, start, wait)


def gather_stage2(slots, shards, split):
    n = len(slots)
    own = [s.reshape(2, s.shape[0] // 2, s.shape[1]) if sp else s for s, sp in zip(shards, split)]

    def copies(in_refs, out_refs, send, recv, base):
        x, y, c, chips = _place()
        sib = (x, y, 1 - c)
        for i in range(n):
            j = base + 4 * i
            mine = out_refs[i].at[2 * x + y]
            yield _rcopy(in_refs[n + i], mine, send, recv, j + 3, sib), _rcopy(in_refs[n + i], mine, send, recv, j + 3, sib)
            if split[i]:
                for k, (px, py) in enumerate(chips):
                    s = 2 * px + py
                    yield (_rcopy(in_refs[i].at[s, c], out_refs[i].at[s, c], send, recv, j + k, sib),
                           _rcopy(in_refs[i].at[s, c], out_refs[i].at[s, 1 - c], send, recv, j + k, sib))

    def start(*a):
        for out, _ in copies(*a):
            out.start()

    def wait(*a):
        for _, back in copies(*a):
            back.wait()

    return Comm(list(slots) + own, [jax.ShapeDtypeStruct(s.shape, s.dtype) for s in slots], 4 * n, start, wait,
                {i: i for i in range(n)})


def swap_halves(gs):
    n = len(gs)

    def copies(in_refs, out_refs, send, recv, base):
        x, y, c, _ = _place()
        return [_rcopy(in_refs[i].at[s, 1 - c], out_refs[i].at[s], send, recv, base + N_CHIPS * i + s, (x, y, 1 - c))
                for i in range(n) for s in range(N_CHIPS)]

    def start(*a):
        for cp in copies(*a):
            cp.start()

    def wait(*a):
        for cp in copies(*a):
            cp.wait()

    return Comm(gs, [jax.ShapeDtypeStruct((N_CHIPS,) + g.shape[2:], g.dtype) for g in gs], N_CHIPS * n, start, wait)


def exchange_chips(ps):
    n = len(ps)

    def start(in_refs, out_refs, send, recv, base):
        x, y, c, chips = _place()
        for i in range(n):
            for k, (px, py) in enumerate(chips):
                _rcopy(in_refs[i].at[2 * px + py], out_refs[i].at[2 * x + y], send, recv, base + 3 * i + k,
                       (px, py, c)).start()

    def wait(in_refs, out_refs, send, recv, base):
        x, y, c, chips = _place()
        for i in range(n):
            for k, (px, py) in enumerate(chips):
                _rcopy(in_refs[i].at[2 * px + py], out_refs[i].at[2 * px + py], send, recv, base + 3 * i + k,
                       (px, py, c)).wait()

    return Comm(ps, [jax.ShapeDtypeStruct(p.shape, p.dtype) for p in ps], 3 * n, start, wait)


def join_halves(fs):
    n = len(fs)

    def start(in_refs, out_refs, send, recv, base):
        x, y, c, _ = _place()
        for i in range(n):
            _rcopy(in_refs[i].at[c], out_refs[i].at[c], send, recv, base + i, (x, y, 1 - c)).start()

    def wait(in_refs, out_refs, send, recv, base):
        x, y, c, _ = _place()
        for i in range(n):
            _rcopy(in_refs[i].at[c], out_refs[i].at[1 - c], send, recv, base + i, (x, y, 1 - c)).wait()

    return Comm(fs, [jax.ShapeDtypeStruct(f.shape, f.dtype) for f in fs], n, start, wait, {i: i for i in range(n)})


def allreduce_small(v, *, name):
    m_per, n = v.shape

    def body(x_ref, sum_ref, all_ref, send_sems, recv_sems, local_sem):
        x, y, c, chips = _place()
        me, sibling = (x, y, c), (x, y, 1 - c)

        def rows(px, py, pc):
            return all_ref.at[pl.ds((4 * px + 2 * py + pc) * m_per, m_per), :]

        def copy(k, block, to, src=None):
            return pltpu.make_async_remote_copy(src_ref=rows(*block) if src is None else src, dst_ref=rows(*block),
                                                send_sem=send_sems.at[k], recv_sem=recv_sems.at[k], device_id=to,
                                                device_id_type=MESH)

        mine = pltpu.make_async_copy(x_ref, rows(*me), local_sem)
        mine.start()
        first = [copy(0, me, sibling, src=x_ref)]
        first += [copy(1 + j, me, (*chip, c), src=x_ref) for j, chip in enumerate(chips)]
        for cp in first:
            cp.start()
        passed = [copy(4 + j, (*chip, c), sibling) for j, chip in enumerate(chips)]
        for j, chip in enumerate(chips):
            copy(1 + j, (*chip, c), me).wait_recv()
            passed[j].start()
        copy(0, sibling, me).wait_recv()
        for j, chip in enumerate(chips):
            copy(4 + j, (*chip, 1 - c), me).wait_recv()
        for cp in first + passed:
            cp.wait_send()
        mine.wait()
        acc = all_ref[0:m_per, :]
        for d in range(1, N_DEV):
            acc = acc + all_ref[d * m_per:(d + 1) * m_per, :]
        sum_ref[...] = acc

    vm = pl.BlockSpec(memory_space=pltpu.VMEM)
    return pl.pallas_call(
        body, name=name, in_specs=[vm], out_specs=vm, out_shape=jax.ShapeDtypeStruct((m_per, n), F32),
        scratch_shapes=[pltpu.VMEM((N_DEV * m_per, n), F32), pltpu.SemaphoreType.DMA((7,)),
                        pltpu.SemaphoreType.DMA((7,)), pltpu.SemaphoreType.DMA],
    )(v)


def _cols_to_slots(w):
    r, c4 = w.shape
    return w.reshape(r, N_CHIPS, c4 // N_CHIPS).transpose(1, 0, 2)


def _slots_to_cols(w):
    n, r, c = w.shape
    return w.transpose(1, 0, 2).reshape(r, n * c)


def _pad_cols(a, width):
    return jnp.pad(a, ((0, 0), (0, width - a.shape[1])))


class InLayout:
    def __init__(self, q_rank, kv_rank):
        gk = GLA_HEADS * GLA_DK
        gv = GLA_HEADS * GLA_DV
        sizes = [q_rank, kv_rank, MLA_ROPE, gk, gk, gv, GLA_GATE_RANK, gv]
        names = ["zq", "zkv", "zkr", "gq", "gk", "gv", "zg", "zr"]
        starts = np.concatenate([[0], np.cumsum(sizes)[:-1]])
        self.ref = {n: (int(s), int(z)) for n, s, z in zip(names, starts, sizes)}
        self.ref_width = int(sum(sizes))
        self.order = ["gv", "zr", "zq", "gq", "gk", "zkv", "zkr", "zg"]
        self.off, self.size = {}, {}
        pos = 0
        for n in self.order:
            padded = -(-self.ref[n][1] // LANE) * LANE
            self.off[n], self.size[n] = pos, padded
            pos += padded
        self.width = pos
        self.shard = self.ref_width // N_CHIPS
        self.shard_pad = -(-self.shard // LANE) * LANE

    def _pieces(self, lo, hi):
        out = []
        while lo < hi:
            s = lo // self.shard
            end = min(hi, (s + 1) * self.shard)
            out.append((s * self.shard_pad + lo - s * self.shard, s * self.shard_pad + end - s * self.shard))
            lo = end
        return out

    def from_shards(self, zs):
        cols = []
        for n in self.order:
            start, size = self.ref[n]
            cols += [zs[:, a:b] for a, b in self._pieces(start, start + size)]
            if self.size[n] > size:
                cols.append(jnp.zeros((zs.shape[0], self.size[n] - size), zs.dtype))
        return jnp.concatenate(cols, axis=1)

    def to_shards(self, dz):
        names = sorted(self.ref, key=lambda n: self.ref[n][0])
        ref = jnp.concatenate([dz[:, self.off[n]:self.off[n] + self.ref[n][1]] for n in names], axis=1)
        ref = ref.reshape(dz.shape[0], N_CHIPS, self.shard)
        return jnp.pad(ref, ((0, 0), (0, 0), (0, self.shard_pad - self.shard))).reshape(dz.shape[0], -1)


def _pad_q_up(w):
    r = w.shape[0]
    w = w.reshape(r, MLA_HEADS, MLA_QK)
    w = jnp.pad(w, ((0, 0), (0, 0), (0, MLA_HEAD_PAD - MLA_QK)))
    return w.reshape(r, MLA_HEADS * MLA_HEAD_PAD)


def _unpad_q_up(g):
    r = g.shape[0]
    return g.reshape(r, MLA_HEADS, MLA_HEAD_PAD)[:, :, :MLA_QK].reshape(r, MLA_HEADS * MLA_QK)


def _rope_tables(positions):
    half = MLA_ROPE // 2
    inv_freq = ROPE_THETA ** (-jnp.arange(half, dtype=F32) / half)
    ang = positions.astype(F32).reshape(-1, 1) * inv_freq
    cos, sin = jnp.cos(ang), jnp.sin(ang)
    s = ang.shape[0]
    cosf = jnp.concatenate([cos, cos, jnp.ones((s, LANE - MLA_ROPE), F32)], axis=1)
    sinf = jnp.concatenate([sin, sin, jnp.zeros((s, LANE - MLA_ROPE), F32)], axis=1)
    rot = np.zeros((LANE, LANE), np.float32)
    for j in range(half):
        rot[j + half, j] = -1.0
        rot[j, j + half] = 1.0
    return cosf, sinf, jnp.asarray(rot)


SMALL = ["ffn1_norm", "mix_norm", "q_a_norm", "kv_a_norm", "mla_q_norm", "mla_k_norm", "gla_b_gate", "gla_out_norm",
         "mem_attn_norm", "mem_norm", "mem_q_norm", "mem_k_norm", "ffn2_norm"]
BIG = ["ffn1_w_gate", "ffn1_w_up", "ffn1_w_down", "w_in", "w_q_up", "w_kv_up", "w_out", "mem_w_q", "mem_w_k",
       "mem_w_v", "mem_w_o", "ffn2_w_gate", "ffn2_w_up", "ffn2_w_down"]
COL_SHARDED = {"ffn1_w_gate", "ffn1_w_up", "w_in", "w_q_up", "w_kv_up", "gla_w_gate2", "mem_w_o", "ffn2_w_gate", "ffn2_w_up"}
WEIGHTS = ["ffn1_norm", "ffn1_w_gate", "ffn1_w_up", "ffn1_w_down", "mix_norm", "w_in", "q_a_norm", "w_q_up", "kv_a_norm",
           "w_kv_up", "mla_q_norm", "mla_k_norm", "gla_w_gate2", "gla_b_gate", "gla_out_norm", "w_out", "mem_attn_norm",
           "mem_norm", "mem_w_q", "mem_w_k", "mem_w_v", "mem_w_o", "mem_q_norm", "mem_k_norm", "ffn2_norm", "ffn2_w_gate",
           "ffn2_w_up", "ffn2_w_down"]


def _pack_small(vals, rows=8):
    flat = jnp.concatenate([v.reshape(-1).astype(F32) for v in vals])
    n = flat.shape[0]
    per = -(-n // (rows * LANE)) * LANE
    return jnp.pad(flat, (0, rows * per - n)).reshape(rows, per)


def _unpack_small(packed, shapes):
    flat = packed.reshape(-1)
    out, pos = [], 0
    for s in shapes:
        n = int(np.prod(s))
        out.append(flat[pos:pos + n].reshape(s))
        pos += n
    return out


FFN1 = ["ffn1_w_gate", "ffn1_w_up", "ffn1_w_down"]
FFN2 = ["ffn2_w_gate", "ffn2_w_up", "ffn2_w_down"]
SLOT_WEIGHTS = {"ffn1_w_gate", "ffn1_w_up", "ffn2_w_gate", "ffn2_w_up", "w_in"}
MID_A = ["w_in", "w_q_up", "w_kv_up", "gla_w_gate2"]
MID_B = ["w_out", "mem_w_q", "mem_w_k", "mem_w_v", "mem_w_o"]


def _with(res, comm):
    return res if comm is not None else (res, None)


def kernel(x, mem, positions, ffn1_norm, ffn1_w_gate, ffn1_w_up, ffn1_w_down, mix_norm, w_in, q_a_norm, w_q_up, kv_a_norm, w_kv_up, mla_q_norm, mla_k_norm, gla_w_gate2, gla_b_gate, gla_out_norm, w_out, mem_attn_norm, mem_norm, mem_w_q, mem_w_k, mem_w_v, mem_w_o, mem_q_norm, mem_k_norm, ffn2_norm, ffn2_w_gate, ffn2_w_up, ffn2_w_down, loss_target, m_ffn1_norm, m_ffn1_w_gate, m_ffn1_w_up, m_ffn1_w_down, m_mix_norm, m_w_in, m_q_a_norm, m_w_q_up, m_kv_a_norm, m_w_kv_up, m_mla_q_norm, m_mla_k_norm, m_gla_w_gate2, m_gla_b_gate, m_gla_out_norm, m_w_out, m_mem_attn_norm, m_mem_norm, m_mem_w_q, m_mem_w_k, m_mem_w_v, m_mem_w_o, m_mem_q_norm, m_mem_k_norm, m_ffn2_norm, m_ffn2_w_gate, m_ffn2_w_up, m_ffn2_w_down, v_ffn1_norm, v_ffn1_w_gate, v_ffn1_w_up, v_ffn1_w_down, v_mix_norm, v_w_in, v_q_a_norm, v_w_q_up, v_kv_a_norm, v_w_kv_up, v_mla_q_norm, v_mla_k_norm, v_gla_w_gate2, v_gla_b_gate, v_gla_out_norm, v_w_out, v_mem_attn_norm, v_mem_norm, v_mem_w_q, v_mem_w_k, v_mem_w_v, v_mem_w_o, v_mem_q_norm, v_mem_k_norm, v_ffn2_norm, v_ffn2_w_gate, v_ffn2_w_up, v_ffn2_w_down):
    args = dict(locals())
    two_d = lambda a: a[0] if a.ndim == 3 else a
    W = {n: two_d(args[n]) for n in WEIGHTS}
    M1 = {n: two_d(args["m_" + n]) for n in WEIGHTS}
    V2 = {n: two_d(args["v_" + n]) for n in WEIGHTS}
    xs, mems, tgt = x[0], mem[0], loss_target[0]
    S, D = xs.shape
    chip = 2 * lax.axis_index("x") + lax.axis_index("y")

    q_rank, kv_rank = W["w_q_up"].shape[0], W["w_kv_up"].shape[0]
    lay = InLayout(q_rank, kv_rank)
    off = lay.off
    shard16 = {n: W[n].astype(BF16) for n in BIG + ["gla_w_gate2"]}
    shard16["w_in"] = _pad_cols(shard16["w_in"], lay.shard_pad)
    full = {}

    def stage1(names):
        return gather_stage1([shard16[n] for n in names], [n != "gla_w_gate2" for n in names])

    def stage2(names, slots):
        return gather_stage2(slots, [shard16[n] for n in names], [n != "gla_w_gate2" for n in names])

    def finish(names, slots):
        for n, s in zip(names, slots):
            s = s.reshape((N_CHIPS,) + shard16[n].shape)
            if n in SLOT_WEIGHTS:
                full[n] = s
            else:
                full[n] = _slots_to_cols(s) if n in COL_SHARDED else s.reshape(-1, s.shape[2])

    finish(FFN1, run_comm(stage2(FFN1, run_comm(stage1(FFN1), name="gather_ffn1")), name="pass_ffn1"))
    cosf, sinf, rot = _rope_tables(positions[0])
    tri = jnp.asarray(np.tril(np.ones((CHUNK, CHUNK), np.float32)))
    gqn = W["mla_q_norm"][:, :MLA_NOPE]
    gqr = _pad_cols(W["mla_q_norm"][:, MLA_NOPE:], LANE)
    gkn = W["mla_k_norm"][:, :MLA_NOPE]
    gkr = _pad_cols(W["mla_k_norm"][:, MLA_NOPE:], LANE)
    HP = MLA_HEAD_PAD
    mla_scale = MLA_QK ** -0.5
    mem_scale = MEM_HEAD_DIM ** -0.5
    mla_w = MLA_HEADS * MLA_V
    gla_w = GLA_HEADS * GLA_DV
    mem_w = MEM_HEADS * MEM_HEAD_DIM

    n1 = row_fwd(rms_fn, [V(xs)], [W["ffn1_norm"]], [(D, BF16)], [(0, 0, 0)], name="ffn1_norm")[0]
    (gate1, up1, act1), mid_a1 = ffn_up(n1, full["ffn1_w_gate"], full["ffn1_w_up"], name="ffn1_up", comm=stage1(MID_A))
    na = len(MID_A)
    x1, got = mm([(act1, full["ffn1_w_down"])], "nn", F32, alpha=0.5, res=xs, name="ffn1_down",
                 comm=merge_comms(stage2(MID_A, mid_a1), stage1(MID_B)))
    ffn1_saved = (n1, gate1, up1, act1)
    finish(MID_A, got[:na])
    mid_b1 = got[na:]
    w_q_up_p = _pad_q_up(full["w_q_up"])
    w_gate2_p = jnp.pad(full["gla_w_gate2"], ((0, LANE - GLA_GATE_RANK), (0, 0)))
    h = row_fwd(rms_fn, [V(x1)], [W["mix_norm"]], [(D, BF16)], [(0, 0, 0)], name="mix_norm")[0]
    nb = len(MID_B)
    z_shards, got = mm([(h, full["w_in"])], "nn", F32, name="in_proj", b_slots=True,
                       comm=merge_comms(stage2(MID_B, mid_b1), stage1(FFN2[:1])))
    z = lay.from_shards(z_shards)
    finish(MID_B, got[:nb])
    f2_gate = got[nb:]
    qa = row_fwd(rms_fn, [V(z, off["zq"], q_rank)], [W["q_a_norm"]], [(q_rank, BF16)], [(0, 0, 0)], name="q_a_norm")[0]
    kva = row_fwd(rms_fn, [V(z, off["zkv"], kv_rank)], [W["kv_a_norm"]], [(kv_rank, BF16)], [(0, 0, 0)], name="kv_a_norm")[0]
    qraw = mm([(qa, w_q_up_p)], "nn", F32, name="q_up")
    kvraw = mm([(kva, full["w_kv_up"])], "nn", F32, name="kv_up")
    tabs = [V(cosf, diff=False), V(sinf, diff=False)]
    q_rows = [V(qraw, 0, LANE, HP), V(qraw, LANE, LANE, HP)] + tabs
    k_rows = [V(kvraw, 0, LANE, HP), V(z, off["zkr"], LANE, 0)] + tabs
    qh = row_fwd(qk_prep_fn, q_rows, [gqn, gqr, rot], [(MLA_HEADS * HP, BF16)], [(0, 0, HP), (0, LANE, HP)],
                 heads=MLA_HEADS, name="q_prep")[0]
    kh = row_fwd(qk_prep_fn, k_rows, [gkn, gkr, rot], [(MLA_HEADS * HP, BF16)], [(0, 0, HP), (0, LANE, HP)],
                 heads=MLA_HEADS, name="k_prep")[0]
    mla_kw = dict(heads=MLA_HEADS, dk=HP, dv=MLA_V, v_off=1, v_hs=2, scale=mla_scale, causal=True)
    o_mla = attn_fwd(qh, kh, kvraw, name="mla_attn", **mla_kw)

    zg = z[:, off["zg"]:off["zg"] + LANE]
    pre = mm([(zg, w_gate2_p)], "nn", F32, name="gla_gate")
    la = row_fwd(gate_fn, [V(pre)], [W["gla_b_gate"]], [(pre.shape[1], F32)], [(0, 0, 0)], name="gla_log_decay")[0]
    gla_kw = dict(q_off=off["gq"], k_off=off["gk"], v_off=off["gv"])
    (o_raw, states), f2_up = gla_fwd(z, la, tri, name="gla_scan", comm=stage1(FFN2[1:2]), **gla_kw)
    gla_rows = [V(o_raw, 0, GLA_DV, GLA_DV), V(z, off["zr"], GLA_DV, GLA_DV)]
    o_gla = row_fwd(gla_out_fn, gla_rows, [W["gla_out_norm"]], [(gla_w, BF16)], [(0, 0, GLA_DV)], heads=GLA_HEADS,
                    name="gla_out")[0]
    o_cat = jnp.concatenate([o_mla, o_gla], axis=1)
    x2, got = mm([(o_cat, full["w_out"])], "nn", F32, res=x1, name="out_proj", comm=stage2(FFN2[:2], f2_gate + f2_up))
    finish(FFN2[:2], got)

    hm = row_fwd(rms_fn, [V(x2)], [W["mem_attn_norm"]], [(D, BF16)], [(0, 0, 0)], name="mem_attn_norm")[0]
    mn = row_fwd(rms_fn, [V(mems)], [W["mem_norm"]], [(D, BF16)], [(0, 0, 0)], name="mem_norm")[0]
    qm_raw = mm([(hm, full["mem_w_q"])], "nn", F32, name="mem_q")
    km_raw = mm([(mn, full["mem_w_k"])], "nn", F32, name="mem_k")
    vm = mm([(mn, full["mem_w_v"])], "nn", F32, name="mem_v")
    hd = MEM_HEAD_DIM
    qm = row_fwd(rms_fn, [V(qm_raw, 0, hd, hd)], [W["mem_q_norm"]], [(mem_w, BF16)], [(0, 0, hd)], heads=MEM_HEADS,
                 name="mem_q_norm")[0]
    km = row_fwd(rms_fn, [V(km_raw, 0, hd, hd)], [W["mem_k_norm"]], [(mem_w, BF16)], [(0, 0, hd)], heads=MEM_HEADS,
                 name="mem_k_norm")[0]
    mem_kw = dict(heads=MEM_HEADS, dk=hd, dv=hd, v_off=0, v_hs=1, scale=mem_scale, causal=False)
    om = attn_fwd(qm, km, vm, name="mem_attn", **mem_kw)
    x3 = mm([(om, full["mem_w_o"])], "nn", F32, res=x2, name="mem_o")

    n2 = row_fwd(rms_fn, [V(x3)], [W["ffn2_norm"]], [(D, BF16)], [(0, 0, 0)], name="ffn2_norm")[0]
    (gate2, up2, act2), f2_down = ffn_up(n2, full["ffn2_w_gate"], full["ffn2_w_up"], name="ffn2_up", comm=stage1(FFN2[2:]))
    finish(FFN2[2:], run_comm(stage2(FFN2[2:], f2_down), name="pass_ffn2_down"))
    y = mm([(act2, full["ffn2_w_down"])], "nn", F32, alpha=0.5, res=x3, name="ffn2_down")
    dy, loss_part = loss_head(y, tgt, name="loss_head")
    loss = lax.psum(loss_part[0, 0], ("x", "y", "c"))

    G, chip_sum, reduced = {}, {}, {}

    def to_halves(n):
        g = G[n]
        if n in SLOT_WEIGHTS:
            s = g
        else:
            s = _cols_to_slots(g) if n in COL_SHARDED else g.reshape(N_CHIPS, g.shape[0] // N_CHIPS, g.shape[1])
        return s.reshape(N_CHIPS, 2, s.shape[1] // 2, s.shape[2])

    def add2(names, halves, got):
        for n, a, b in zip(names, halves, got):
            chip_sum[n] = add_own_half(a, b, BF16, name=f"rs_add2_{n}")

    to_join = []

    def add4_join(names, parts):
        for n, p in zip(names, parts):
            to_join.append((n, sum_chip_parts(chip_sum[n], p, name=f"rs_add4_{n}")))

    def with_joins(comm):
        names, totals = [n for n, _ in to_join], [t for _, t in to_join]
        to_join.clear()
        if not names:
            return comm, lambda got: got
        own = 0 if comm is None else len(comm.out_shapes)
        joined = join_halves(totals)

        def split(got):
            for n, b in zip(names, got[own:]):
                reduced[n] = b.reshape(-1, b.shape[2])[:, :W[n].shape[1]]
            return got[:own]

        return (joined if comm is None else merge_comms(comm, joined)), split

    def flush_joins():
        comm, split = with_joins(None)
        if comm is not None:
            split(run_comm(comm, name=f"rs_join_{len(reduced)}"))

    def ffn_backward(dout, xin, tag, saved, dact_comm=None, after_dact=None, after_dwd=None):
        n_, gate, up, act = saved
        nd, ng, nu = f"{tag}_w_down", f"{tag}_w_gate", f"{tag}_w_up"
        (dgate, dup), got0 = _with(ffn_dact(dout, full[nd], gate, up, 0.5, name=f"{tag}_dact", comm=dact_comm), dact_comm)
        dwd_comm = after_dact(got0) if after_dact else None
        G[nd], got1 = _with(mm([(act, dout)], "tn", F32, alpha=0.5, name=f"{tag}_dwd", tm=1408, tn=1024,
                               comm=dwd_comm), dwd_comm)
        if after_dwd:
            after_dwd(got1)
        hd_ = to_halves(nd)
        comm, split = with_joins(swap_halves([hd_]))
        G[ng], got = mm([(n_, dgate)], "tn", F32, name=f"{tag}_dwg", out_slots=True, tm=1024, tn=1408, rows_inner=True,
                        comm=comm)
        add2([nd], [hd_], split(got))
        hg = to_halves(ng)
        G[nu], got_g = mm([(n_, dup)], "tn", F32, name=f"{tag}_dwu", out_slots=True, tm=1024, tn=1408, rows_inner=True,
                          comm=swap_halves([hg]))
        add2([ng], [hg], got_g)
        hu = to_halves(nu)
        dn, (parts_d, got_u) = mm([(dgate, full[ng]), (dup, full[nu])], "nt", F32, name=f"{tag}_dn", b_slots=True,
                                  tn=1024, tk=1408,
                                  comm=merge_comms(exchange_chips([chip_sum[nd]]), swap_halves([hu])))
        add2([nu], [hu], [got_u])
        dx, G[f"{tag}_norm"] = row_bwd(rms_fn, [V(xin)], [W[f"{tag}_norm"]], [V(dn)], const_diff=[True], res=dout,
                                       name=f"{tag}_dnorm")
        add4_join([nd], [parts_d])
        return dx, exchange_chips([chip_sum[ng]]), exchange_chips([chip_sum[nu]])

    g3, ffn2_gate_xchg, ffn2_up_xchg = ffn_backward(dy, x3, "ffn2", (n2, gate2, up2, act2))

    d_om = mm([(g3, full["mem_w_o"])], "nt", F32, name="mem_o_dx")
    G["mem_w_o"] = mm([(om, g3)], "tn", F32, name="mem_o_dw")
    dqm, dkm, dvm = attn_bwd(qm, km, vm, d_om, name="mem_attn_bwd", **mem_kw)
    dqm_raw, G["mem_q_norm"] = row_bwd(rms_fn, [V(qm_raw, 0, hd, hd)], [W["mem_q_norm"]], [V(dqm, 0, hd, hd)],
                                       const_diff=[True], heads=MEM_HEADS, row_dtype=BF16, name="mem_q_norm_bwd")
    dkm_raw, G["mem_k_norm"] = row_bwd(rms_fn, [V(km_raw, 0, hd, hd)], [W["mem_k_norm"]], [V(dkm, 0, hd, hd)],
                                       const_diff=[True], heads=MEM_HEADS, row_dtype=BF16, name="mem_k_norm_bwd")
    dhm = mm([(dqm_raw, full["mem_w_q"])], "nt", F32, name="mem_q_dx")
    G["mem_w_q"] = mm([(hm, dqm_raw)], "tn", F32, name="mem_q_dw")
    dmn = mm([(dkm_raw, full["mem_w_k"]), (dvm, full["mem_w_v"])], "nt", F32, name="mem_kv_dx")
    G["mem_w_k"] = mm([(mn, dkm_raw)], "tn", F32, name="mem_k_dw")
    G["mem_w_v"] = mm([(mn, dvm)], "tn", F32, name="mem_v_dw")
    _, G["mem_norm"] = row_bwd(rms_fn, [V(mems)], [W["mem_norm"]], [V(dmn)], const_diff=[True], row_dtype=BF16,
                               name="mem_norm_bwd")
    g2, G["mem_attn_norm"] = row_bwd(rms_fn, [V(x2)], [W["mem_attn_norm"]], [V(dhm)], const_diff=[True], res=g3,
                                     name="mem_attn_norm_bwd")

    d_ocat = mm([(g2, full["w_out"])], "nt", F32, name="out_proj_dx")
    G["w_out"] = mm([(o_cat, g2)], "tn", F32, name="out_proj_dw")

    d_oraw, d_zr, G["gla_out_norm"] = row_bwd(gla_out_fn, gla_rows, [W["gla_out_norm"]],
                                              [V(d_ocat, mla_w, GLA_DV, GLA_DV)], const_diff=[True], heads=GLA_HEADS,
                                              name="gla_out_bwd")
    mid_b_halves = [to_halves(n) for n in MID_B]
    comm, split = with_joins(merge_comms(ffn2_gate_xchg, swap_halves(mid_b_halves)))
    (d_gq, d_gk, d_gv, d_la), got = gla_bwd(z, la, tri, tri.T, states, d_oraw, name="gla_scan_bwd", comm=comm, **gla_kw)
    got = split(got)
    add4_join(["ffn2_w_gate"], got[:1])
    add2(MID_B, mid_b_halves, got[1:])
    d_pre, G["gla_b_gate"] = row_bwd(gate_fn, [V(pre)], [W["gla_b_gate"]], [V(d_la)], const_diff=[True], row_dtype=BF16,
                                     name="gla_log_decay_bwd")
    d_zg = mm([(d_pre, w_gate2_p)], "nt", BF16, name="gla_gate_dx")
    G["gla_w_gate2"] = mm([(zg, d_pre)], "tn", F32, name="gla_gate_dw")[:GLA_GATE_RANK]

    comm, split = with_joins(ffn2_up_xchg)
    (d_qh, d_kh, d_v), got = attn_bwd(qh, kh, kvraw, d_ocat, name="mla_attn_bwd", comm=comm, **mla_kw)
    add4_join(["ffn2_w_up"], split(got))
    cq = [V(d_qh, 0, LANE, HP), V(d_qh, LANE, LANE, HP)]
    ck = [V(d_kh, 0, LANE, HP), V(d_kh, LANE, LANE, HP)]
    d_qraw, d_gqn, d_gqr = row_bwd(qk_prep_fn, q_rows, [gqn, gqr, rot], cq, const_diff=[True, True, False],
                                   heads=MLA_HEADS, row_dtype=BF16, pack={0: (0, HP), 1: (LANE, HP)},
                                   pack_width=MLA_HEADS * HP, name="q_prep_bwd")
    d_kvraw, d_zkr, d_gkn, d_gkr = row_bwd(qk_prep_fn, k_rows, [gkn, gkr, rot], ck, const_diff=[True, True, False],
                                           heads=MLA_HEADS, row_dtype=BF16, pack={0: (0, HP)}, pack_width=MLA_HEADS * HP,
                                           fills=[(V(d_v, 0, MLA_V, MLA_V), LANE, HP)], name="k_prep_bwd")
    G["mla_q_norm"] = jnp.concatenate([d_gqn, d_gqr[:, :MLA_ROPE]], axis=1)
    G["mla_k_norm"] = jnp.concatenate([d_gkn, d_gkr[:, :MLA_ROPE]], axis=1)
    d_qa = mm([(d_qraw, w_q_up_p)], "nt", F32, name="q_up_dx")
    G["w_q_up"] = _unpad_q_up(mm([(qa, d_qraw)], "tn", F32, name="q_up_dw"))
    d_kva = mm([(d_kvraw, full["w_kv_up"])], "nt", F32, name="kv_up_dx")
    G["w_kv_up"] = mm([(kva, d_kvraw)], "tn", F32, name="kv_up_dw")
    d_zq, G["q_a_norm"] = row_bwd(rms_fn, [V(z, off["zq"], q_rank)], [W["q_a_norm"]], [V(d_qa)], const_diff=[True],
                                  row_dtype=BF16, name="q_a_norm_bwd")
    d_zkv, G["kv_a_norm"] = row_bwd(rms_fn, [V(z, off["zkv"], kv_rank)], [W["kv_a_norm"]], [V(d_kva)], const_diff=[True],
                                    row_dtype=BF16, name="kv_a_norm_bwd")

    seg = {"gv": d_gv, "zr": d_zr, "zq": d_zq, "gq": d_gq, "gk": d_gk, "zkv": d_zkv, "zkr": d_zkr, "zg": d_zg}
    dz = jnp.concatenate([_pad_cols(seg[n].astype(BF16), lay.size[n]) for n in lay.order], axis=1)
    comm, split = with_joins(exchange_chips([chip_sum[n] for n in MID_B]))
    dz_shards = lay.to_shards(dz)
    dh, got = mm([(dz_shards, full["w_in"])], "nt", F32, name="in_proj_dx", b_slots=True, comm=comm)
    add4_join(MID_B, split(got))
    G["w_in"] = mm([(h, dz_shards)], "tn", F32, name="in_proj_dw", out_slots=True)
    g1, G["mix_norm"] = row_bwd(rms_fn, [V(x1)], [W["mix_norm"]], [V(dh)], const_diff=[True], res=g2, name="mix_norm_bwd")

    mid_a = [n for n in MID_A if n != "gla_w_gate2"]
    mid_a_halves = [to_halves(n) for n in mid_a]

    def mid_a_sums(got):
        add2(mid_a, mid_a_halves, got)
        return exchange_chips([chip_sum[n] for n in mid_a])

    gx, ffn1_gate_xchg, ffn1_up_xchg = ffn_backward(
        g1, xs, "ffn1", ffn1_saved, dact_comm=swap_halves(mid_a_halves), after_dact=mid_a_sums,
        after_dwd=lambda parts: add4_join(mid_a, parts))

    grad, delta, new_m, new_v = {}, {}, {}, {}

    def adam_group(names, tag, comm=None):
        if any(n not in reduced for n in names):
            flush_joins()
        comm, split = with_joins(comm)
        res, extra = _with(adamw([(W[n], reduced[n], M1[n], V2[n]) for n in names], name=f"adamw_{tag}", comm=comm), comm)
        for n, (g_, d_, m_, v_) in zip(names, res):
            grad[n], delta[n], new_m[n], new_v[n] = g_, d_, m_, v_
        return split(extra) if extra is not None else None

    add4_join(["ffn1_w_gate"], adam_group(FFN2, "ffn2", comm=ffn1_gate_xchg))
    add4_join(["ffn1_w_up"], adam_group(mid_a + MID_B, "mid", comm=ffn1_up_xchg))
    adam_group(FFN1, "ffn1")

    small_names = SMALL + ["gla_w_gate2"]
    small_sum = allreduce_small(_pack_small([G[n] for n in small_names]), name="allreduce_small")
    small_g = dict(zip(small_names, _unpack_small(small_sum, [G[n].shape for n in small_names])))
    shard_c = W["gla_w_gate2"].shape[1]
    grad["gla_w_gate2"] = lax.dynamic_slice_in_dim(small_g["gla_w_gate2"], chip * shard_c, shard_c, axis=1)
    pw = _pack_small([W[n] for n in SMALL] + [W["gla_w_gate2"]])
    pg = _pack_small([small_g[n] for n in SMALL] + [grad["gla_w_gate2"]])
    pm = _pack_small([M1[n] for n in SMALL] + [M1["gla_w_gate2"]])
    pv = _pack_small([V2[n] for n in SMALL] + [V2["gla_w_gate2"]])
    (_, pd, pnm, pnv), = adamw([(pw, pg, pm, pv)], name="adamw_small")
    shapes = [W[n].shape for n in small_names]
    for n, d_, m_, v_ in zip(small_names, _unpack_small(pd, shapes), _unpack_small(pnm, shapes), _unpack_small(pnv, shapes)):
        delta[n], new_m[n], new_v[n] = d_, m_, v_
        if n != "gla_w_gate2":
            grad[n] = small_g[n]

    lead = lambda d: [d[n].reshape(args[n].shape) for n in WEIGHTS]
    return (loss, gx[None], *lead(grad), *lead(delta), *lead(new_m), *lead(new_v))
```

```python
import functools
import math

import numpy as np
import jax
import jax.numpy as jnp
from jax import lax
from jax.experimental import pallas as pl
from jax.experimental.pallas import tpu as pltpu

F32 = jnp.float32
BF16 = jnp.bfloat16
MXU_DTYPE = jnp.bfloat16
MESH = pl.DeviceIdType.MESH
ANY = pl.BlockSpec(memory_space=pl.ANY)

LANE = 128
EPS = 1e-6
CHUNK = 64
MLA_HEADS = 8
MLA_NOPE = 128
MLA_ROPE = 64
MLA_QK = MLA_NOPE + MLA_ROPE
MLA_V = 128
MLA_HEAD_PAD = 2 * LANE
ROPE_THETA = 10000.0
GLA_HEADS = 4
GLA_DK = 128
GLA_DV = 256
GLA_GATE_RANK = 16
GLA_TAU = 16.0
MEM_HEADS = 4
MEM_HEAD_DIM = 128
N_CHIPS = 4
N_DEV = 8

ADAM_LR = 0.001
ADAM_B1 = 0.9
ADAM_B2 = 0.999
ADAM_EPS = 1e-08
ADAM_WD = 0.01
ADAM_STEP = 10

VMEM_LIMIT = 56 * 1024 * 1024


def _cparams(sem=None):
    if sem is None:
        return pltpu.CompilerParams(vmem_limit_bytes=VMEM_LIMIT)
    return pltpu.CompilerParams(dimension_semantics=sem, vmem_limit_bytes=VMEM_LIMIT)


def _tile(dim, pref, unit=LANE):
    if dim <= pref:
        return dim
    t = (pref // unit) * unit
    while t > unit and dim % t:
        t -= unit
    assert dim % t == 0, (dim, pref, unit)
    return t


class Comm:
    def __init__(self, ins, out_shapes, nsem, start, wait, aliases=None):
        self.ins, self.out_shapes, self.nsem = list(ins), list(out_shapes), nsem
        self.start, self.wait, self.aliases = start, wait, dict(aliases or {})


def merge_comms(a, b):
    ai, ao = len(a.ins), len(a.out_shapes)

    def start(ins, outs, send, recv, base):
        a.start(ins[:ai], outs[:ao], send, recv, base)
        b.start(ins[ai:], outs[ao:], send, recv, base + a.nsem)

    def wait(ins, outs, send, recv, base):
        a.wait(ins[:ai], outs[:ao], send, recv, base)
        b.wait(ins[ai:], outs[ao:], send, recv, base + a.nsem)

    aliases = dict(a.aliases)
    aliases.update({ai + i: ao + o for i, o in b.aliases.items()})
    return Comm(a.ins + b.ins, a.out_shapes + b.out_shapes, a.nsem + b.nsem, start, wait, aliases)


def run_comm(comm, *, name):
    ni, no = len(comm.ins), len(comm.out_shapes)

    def body(*refs):
        ins, outs = refs[:ni], refs[ni:ni + no]
        send, recv = refs[ni + no:]
        comm.start(ins, outs, send, recv, 0)
        comm.wait(ins, outs, send, recv, 0)

    return pl.pallas_call(
        body, name=name, in_specs=[ANY] * ni, out_specs=[ANY] * no, out_shape=comm.out_shapes,
        input_output_aliases=comm.aliases,
        scratch_shapes=[pltpu.SemaphoreType.DMA((comm.nsem,)), pltpu.SemaphoreType.DMA((comm.nsem,))])(*comm.ins)


HBM = pl.BlockSpec(memory_space=pltpu.HBM)
SEM = pl.BlockSpec(memory_space=pltpu.SEMAPHORE)


def start_comm(comm, *, name):
    assert not comm.aliases
    ni, no = len(comm.ins), len(comm.out_shapes)

    def body(*refs):
        srcs, lands = refs[:ni], refs[ni:ni + no]
        send, recv, token = refs[ni + no], refs[ni + no + 1], refs[-1]
        comm.start(srcs, lands, send, recv, 0)
        token[...] = jnp.zeros_like(token)

    through = [pltpu.HBM(a.shape, a.dtype) for a in comm.ins] + [pltpu.HBM(s.shape, s.dtype) for s in comm.out_shapes]
    ops = [pltpu.with_memory_space_constraint(a, pltpu.HBM) for a in comm.ins]
    ops += [pltpu.with_memory_space_constraint(lax.empty(s.shape, s.dtype), pltpu.HBM) for s in comm.out_shapes]
    res = pl.pallas_call(
        body, name=name, in_specs=[HBM] * (ni + no),
        out_shape=[pltpu.SemaphoreType.DMA((comm.nsem,)), pltpu.SemaphoreType.DMA((comm.nsem,))] + through
        + [jax.ShapeDtypeStruct((8, LANE), F32)],
        out_specs=[SEM, SEM] + [HBM] * (ni + no) + [pl.BlockSpec(memory_space=pltpu.VMEM)],
        input_output_aliases={i: 2 + i for i in range(ni + no)},
        compiler_params=pltpu.CompilerParams(has_side_effects=pltpu.SideEffectType.DATAFLOW_SIDE_EFFECTING))(*ops)
    return res[0], res[1], list(res[2:2 + ni]), list(res[2 + ni:2 + ni + no]), res[-1]


def wait_comm(comm, started, after, *, name):
    send, recv, srcs, lands, _ = started
    ni, no = len(srcs), len(lands)

    def body(*refs):
        comm.wait(refs[:ni], refs[ni:ni + no], refs[ni + no], refs[ni + no + 1], 0)

    res = pl.pallas_call(
        body, name=name, in_specs=[HBM] * (ni + no) + [SEM, SEM, ANY],
        out_shape=[pltpu.HBM(a.shape, a.dtype) for a in srcs + lands], out_specs=[HBM] * (ni + no),
        input_output_aliases={i: i for i in range(ni + no)},
        compiler_params=pltpu.CompilerParams(has_side_effects=pltpu.SideEffectType.DATAFLOW_SIDE_EFFECTING),
    )(*srcs, *lands, send, recv, after)
    return list(res[ni:])


def _pcall(body, ops, *, name, grid, in_specs, out_specs, out_shape, sem, scratch_shapes=(), comm=None):
    if comm is None:
        return pl.pallas_call(body, name=name, grid=grid, in_specs=in_specs, out_specs=out_specs, out_shape=out_shape,
                              scratch_shapes=list(scratch_shapes), compiler_params=_cparams(sem))(*ops)
    multi = isinstance(out_shape, (list, tuple))
    k_out_shape = list(out_shape) if multi else [out_shape]
    k_out_specs = list(out_specs) if multi else [out_specs]
    nki, nko, nks = len(ops), len(k_out_shape), len(scratch_shapes)
    nci, nco = len(comm.ins), len(comm.out_shapes)

    def wrapped(*refs):
        p = 0
        k_in = refs[p:p + nki]; p += nki
        c_in = refs[p:p + nci]; p += nci
        k_out = refs[p:p + nko]; p += nko
        c_out = refs[p:p + nco]; p += nco
        k_scr = refs[p:p + nks]; p += nks
        send, recv = refs[p:]
        first = pl.program_id(0) == 0
        last = pl.program_id(0) == grid[0] - 1
        for a in range(1, len(grid)):
            first = jnp.logical_and(first, pl.program_id(a) == 0)
            last = jnp.logical_and(last, pl.program_id(a) == grid[a] - 1)

        @pl.when(first)
        def _():
            comm.start(c_in, c_out, send, recv, 0)

        body(*k_in, *k_out, *k_scr)

        @pl.when(last)
        def _():
            comm.wait(c_in, c_out, send, recv, 0)

    res = pl.pallas_call(
        wrapped, name=name, grid=grid, in_specs=list(in_specs) + [ANY] * nci, out_specs=k_out_specs + [ANY] * nco,
        out_shape=k_out_shape + comm.out_shapes,
        input_output_aliases={nki + i: nko + o for i, o in comm.aliases.items()},
        scratch_shapes=list(scratch_shapes) + [pltpu.SemaphoreType.DMA((comm.nsem,)), pltpu.SemaphoreType.DMA((comm.nsem,))],
        compiler_params=_cparams(("arbitrary",) * len(grid)))(*ops, *comm.ins)
    k_res = list(res[:nko]) if multi else res[0]
    return k_res, list(res[nko:])


_DIMS = {"nn": (((1,), (0,)), ((), ())), "nt": (((1,), (1,)), ((), ())), "tn": (((0,), (0,)), ((), ()))}


def _blockspec(shape, index, rows_inner):
    return pl.BlockSpec(shape, (lambda j, i, k: index(i, j, k)) if rows_inner else index)


def mm(pairs, mode, out_dtype, *, name, alpha=1.0, res=None, tm=1024, tn=1024, tk=4096, b_slots=False, out_slots=False,
       rows_inner=False, comm=None):
    a0, b0 = pairs[0]
    if b_slots:
        b_rows, b_cols = b0.shape[1], N_CHIPS * b0.shape[2]
    else:
        b_rows, b_cols = b0.shape
    (M, K) = a0.shape[::-1] if mode == "tn" else a0.shape
    N = b_rows if mode == "nt" else b_cols
    shard = (b_cols if b_slots else N) // N_CHIPS
    tm = _tile(M, tm)
    tn = _tile(shard if (out_slots or (b_slots and mode != "nt")) else N, tn)
    tk = _tile(shard if (b_slots and mode == "nt") else K, tk)
    nk = K // tk
    npairs = len(pairs)
    dims = _DIMS[mode]
    spec = functools.partial(_blockspec, rows_inner=rows_inner)
    if mode == "tn":
        a_spec = spec((tk, tm), lambda i, j, k: (k, i))
    else:
        a_spec = spec((tm, tk), lambda i, j, k: (i, k))
    per = shard // (tk if mode == "nt" else tn)
    if mode == "nt":
        b_spec = (spec((None, tn, tk), lambda i, j, k: (k // per, j, k % per)) if b_slots else
                  spec((tn, tk), lambda i, j, k: (j, k)))
    else:
        b_spec = (spec((None, tk, tn), lambda i, j, k: (j // per, k, j % per)) if b_slots else
                  spec((tk, tn), lambda i, j, k: (k, j)))
    if out_slots:
        assert res is None and mode != "nt"
        o_spec = spec((None, tm, tn), lambda i, j, k: (j // per, i, j % per))
        out_sds = jax.ShapeDtypeStruct((N_CHIPS, M, shard), out_dtype)
    else:
        o_spec = spec((tm, tn), lambda i, j, k: (i, j))
        out_sds = jax.ShapeDtypeStruct((M, N), out_dtype)
    has_res = res is not None

    def body(*refs):
        ab = refs[:2 * npairs]
        res_ref = refs[2 * npairs] if has_res else None
        o_ref = refs[2 * npairs + int(has_res)]

        def products():
            r = None
            for p in range(npairs):
                d = lax.dot_general(ab[2 * p][...].astype(MXU_DTYPE), ab[2 * p + 1][...].astype(MXU_DTYPE), dims,
                                    preferred_element_type=F32)
                r = d if r is None else r + d
            return r

        def finish(r):
            if alpha != 1.0:
                r = r * alpha
            if has_res:
                r = res_ref[...].astype(F32) + r
            o_ref[...] = r.astype(out_dtype)

        if nk == 1:
            finish(products())
            return
        acc = refs[-1]
        k = pl.program_id(2)

        @pl.when(k == 0)
        def _():
            acc[...] = jnp.zeros_like(acc)

        acc[...] += products()

        @pl.when(k == nk - 1)
        def _():
            finish(acc[...])

    ops, specs = [], []
    for a, b in pairs:
        ops += [a, b]
        specs += [a_spec, b_spec]
    if has_res:
        ops.append(res)
        specs.append(o_spec)
    blocks = (N // tn, M // tm) if rows_inner else (M // tm, N // tn)
    return _pcall(body, ops, name=name, grid=blocks + (nk,), in_specs=specs, out_specs=o_spec, out_shape=out_sds,
                  scratch_shapes=[pltpu.VMEM((tm, tn), F32)] if nk > 1 else [],
                  sem=("parallel", "parallel", "arbitrary"), comm=comm)


def _sigmoid(x):
    return 1.0 / (1.0 + jnp.exp(-x))


def ffn_up(n, wg, wu, *, name, tm=512, tn=1408, comm=None):
    M, K = n.shape
    shard = wg.shape[2]
    N = N_CHIPS * shard
    tm, tn = _tile(M, tm), _tile(shard, tn)
    per = shard // tn
    w_spec = pl.BlockSpec((None, K, tn), lambda j, i: (j // per, 0, j % per))

    def body(n_ref, wg_ref, wu_ref, g_ref, u_ref, a_ref):
        nv = n_ref[...].astype(MXU_DTYPE)
        g = jnp.dot(nv, wg_ref[...].astype(MXU_DTYPE), preferred_element_type=F32)
        u = jnp.dot(nv, wu_ref[...].astype(MXU_DTYPE), preferred_element_type=F32)
        g_ref[...] = g.astype(g_ref.dtype)
        u_ref[...] = u.astype(u_ref.dtype)
        a_ref[...] = (g * _sigmoid(g) * u).astype(a_ref.dtype)

    o_spec = pl.BlockSpec((tm, tn), lambda j, i: (i, j))
    sds = jax.ShapeDtypeStruct((M, N), BF16)
    return _pcall(
        body, [n, wg, wu], name=name, grid=(N // tn, M // tm),
        in_specs=[pl.BlockSpec((tm, K), lambda j, i: (i, 0)), w_spec, w_spec],
        out_specs=[o_spec, o_spec, o_spec], out_shape=[sds, sds, sds], sem=("parallel", "parallel"), comm=comm)


def ffn_dact(dy, wd, gate, up, alpha, *, name, tm=512, tn=1408, comm=None):
    M, K = dy.shape
    N = wd.shape[0]
    tm, tn = _tile(M, tm), _tile(N, tn)

    def body(dy_ref, wd_ref, g_ref, u_ref, dg_ref, du_ref):
        da = lax.dot_general(dy_ref[...].astype(MXU_DTYPE), wd_ref[...].astype(MXU_DTYPE), _DIMS["nt"],
                             preferred_element_type=F32) * alpha
        g = g_ref[...].astype(F32)
        u = u_ref[...].astype(F32)
        s = _sigmoid(g)
        du_ref[...] = (da * (g * s)).astype(du_ref.dtype)
        dg_ref[...] = (da * u * (s * (1.0 + g * (1.0 - s)))).astype(dg_ref.dtype)

    o_spec = pl.BlockSpec((tm, tn), lambda j, i: (i, j))
    sds = jax.ShapeDtypeStruct((M, N), BF16)
    return _pcall(
        body, [dy, wd, gate, up], name=name, grid=(N // tn, M // tm),
        in_specs=[pl.BlockSpec((tm, K), lambda j, i: (i, 0)), pl.BlockSpec((tn, K), lambda j, i: (j, 0)), o_spec, o_spec],
        out_specs=[o_spec, o_spec], out_shape=[sds, sds], sem=("parallel", "parallel"), comm=comm)


def _window(width, off, ext):
    ww = LANE
    while ww < width:
        if ww >= ext and off // ww == (off + ext - 1) // ww and width % ww == 0:
            break
        ww *= 2
    else:
        ww = width
    return ww, off // ww, off - (off // ww) * ww


class V:
    def __init__(self, arr, off=0, w=None, hs=0, diff=True):
        self.arr, self.off, self.hs, self.diff = arr, off, hs, diff
        self.w = arr.shape[1] - off if w is None else w

    def window(self, heads, tr):
        ww, blk, inner = _window(self.arr.shape[1], self.off, (heads - 1) * self.hs + self.w)
        return pl.BlockSpec((tr, ww), lambda i, blk=blk: (i, blk)), inner


def _const_spec(c):
    return pl.BlockSpec(c.shape, lambda i: (0, 0))


def row_fwd(fn, rows, consts, outs, out_map, *, heads=1, tr=256, name):
    S = rows[0].arr.shape[0]
    tr = _tile(S, tr, 8)
    wins = [v.window(heads, tr) for v in rows]
    nr, nc = len(rows), len(consts)

    def body(*refs):
        row_refs, const_refs, out_refs = refs[:nr], refs[nr:nr + nc], refs[nr + nc:]
        cv = [c[...].astype(F32) for c in const_refs]
        for h in range(heads):
            rv = []
            for v, (_, io), r in zip(rows, wins, row_refs):
                lo = io + h * v.hs
                rv.append(r[:, lo:lo + v.w].astype(F32))
            res = fn(*rv, *cv)
            for (ai, off, hs), o in zip(out_map, res):
                lo = off + h * hs
                out_refs[ai][:, lo:lo + o.shape[1]] = o.astype(out_refs[ai].dtype)

    return pl.pallas_call(
        body, name=name, grid=(S // tr,),
        in_specs=[w[0] for w in wins] + [_const_spec(c) for c in consts],
        out_specs=[pl.BlockSpec((tr, w), lambda i: (i, 0)) for w, _ in outs],
        out_shape=[jax.ShapeDtypeStruct((S, w), d) for w, d in outs],
        compiler_params=_cparams(("parallel",)))(*[v.arr for v in rows], *consts)


def row_bwd(fn, rows, consts, cots, *, const_diff, heads=1, tr=256, res=None, row_dtype=F32, pack=None, pack_width=0,
            fills=(), name):
    S = rows[0].arr.shape[0]
    tr = _tile(S, tr, 8)
    pack = dict(pack or {})
    nr, nc, nct, nf = len(rows), len(consts), len(cots), len(fills)
    wins = [v.window(heads, tr) for v in rows]
    cwins = [v.window(heads, tr) for v in cots]
    fwins = [v.window(heads, tr) for v, _, _ in fills]
    drows = [k for k, v in enumerate(rows) if v.diff]
    dconsts = [k for k in range(nc) if const_diff[k]]
    has_res = res is not None
    assert not (has_res and 0 in pack)
    widths = [pack_width] if pack else []
    place = []
    for n, k in enumerate(drows):
        if n in pack:
            place.append((0,) + tuple(pack[n]))
        else:
            place.append((len(widths), 0, rows[k].w))
            widths.append(rows[k].w * (heads if rows[k].hs else 1))

    def body(*refs):
        row_refs = refs[:nr]
        const_refs = refs[nr:nr + nc]
        cot_refs = refs[nr + nc:nr + nc + nct]
        p = nr + nc + nct
        fill_refs = refs[p:p + nf]
        p += nf
        res_ref = refs[p] if has_res else None
        p += int(has_res)
        grow_refs = refs[p:p + len(widths)]
        gconst_refs = refs[p + len(widths):]
        i = pl.program_id(0)
        cv = [c[...].astype(F32) for c in const_refs]
        shared = [None] * len(drows)
        gc_sum = [None] * len(dconsts)
        for h in range(heads):
            rv = []
            for v, (_, io), r in zip(rows, wins, row_refs):
                lo = io + h * v.hs
                rv.append(r[:, lo:lo + v.w].astype(F32))
            ct = []
            for v, (_, io), r in zip(cots, cwins, cot_refs):
                lo = io + h * v.hs
                ct.append(r[:, lo:lo + v.w].astype(F32))

            def closed(*d):
                rr, cc = list(rv), list(cv)
                for k, val in zip(drows, d[:len(drows)]):
                    rr[k] = val
                for k, val in zip(dconsts, d[len(drows):]):
                    cc[k] = val
                return tuple(fn(*rr, *cc))

            _, vjp = jax.vjp(closed, *[rv[k] for k in drows], *[cv[k] for k in dconsts])
            grads = vjp(tuple(ct))
            for n, k in enumerate(drows):
                g = grads[n]
                if rows[k].hs == 0 and heads > 1:
                    shared[n] = g if shared[n] is None else shared[n] + g
                else:
                    if n == 0 and has_res:
                        g = g + res_ref[:, h * rows[k].w:(h + 1) * rows[k].w].astype(F32)
                    out, off, hs = place[n]
                    grow_refs[out][:, off + h * hs:off + h * hs + rows[k].w] = g.astype(row_dtype)
            for (v, off, hs), (_, io), r in zip(fills, fwins, fill_refs):
                lo = io + h * v.hs
                grow_refs[0][:, off + h * hs:off + h * hs + v.w] = r[:, lo:lo + v.w].astype(row_dtype)
            for n in range(len(dconsts)):
                g = grads[len(drows) + n]
                gc_sum[n] = g if gc_sum[n] is None else gc_sum[n] + g
        for n, k in enumerate(drows):
            if shared[n] is not None:
                g = shared[n]
                if n == 0 and has_res:
                    g = g + res_ref[...].astype(F32)
                grow_refs[place[n][0]][...] = g.astype(row_dtype)

        @pl.when(i == 0)
        def _():
            for n in range(len(dconsts)):
                gconst_refs[n][...] = gc_sum[n]

        @pl.when(i > 0)
        def _():
            for n in range(len(dconsts)):
                gconst_refs[n][...] += gc_sum[n]

    in_specs = [w[0] for w in wins] + [_const_spec(c) for c in consts] + [w[0] for w in cwins] + [w[0] for w in fwins]
    ops = [v.arr for v in rows] + list(consts) + [v.arr for v in cots] + [v.arr for v, _, _ in fills]
    if has_res:
        in_specs.append(pl.BlockSpec((tr, widths[0]), lambda i: (i, 0)))
        ops.append(res)
    out_specs = [pl.BlockSpec((tr, w), lambda i: (i, 0)) for w in widths]
    out_shape = [jax.ShapeDtypeStruct((S, w), row_dtype) for w in widths]
    for k in dconsts:
        out_specs.append(_const_spec(consts[k]))
        out_shape.append(jax.ShapeDtypeStruct(consts[k].shape, F32))
    return pl.pallas_call(body, name=name, grid=(S // tr,), in_specs=in_specs, out_specs=out_specs,
                          out_shape=out_shape, compiler_params=_cparams(("arbitrary",)))(*ops)


def _rms(x, g, n=None):
    n = x.shape[-1] if n is None else n
    ms = jnp.sum(x * x, axis=-1, keepdims=True) * (1.0 / n)
    return x * lax.rsqrt(ms + EPS) * g


def rms_fn(x, g):
    return (_rms(x, g),)


def qk_prep_fn(nope, rope, cos, sin, gn, gr, rot):
    ms = (jnp.sum(nope * nope, axis=-1, keepdims=True) + jnp.sum(rope * rope, axis=-1, keepdims=True)) * (1.0 / MLA_QK)
    r = lax.rsqrt(ms + EPS)
    on = nope * r * gn
    orr = rope * r * gr
    turned = jnp.dot(orr, rot, precision=lax.Precision.HIGHEST, preferred_element_type=F32)
    return on, orr * cos + turned * sin


def gla_out_fn(o, zr, g):
    return (_rms(o, g) * (zr * _sigmoid(zr)),)


def gate_fn(pre, b):
    t = pre + b
    return ((jnp.minimum(t, 0.0) - jnp.log(1.0 + jnp.exp(-jnp.abs(t)))) * (1.0 / GLA_TAU),)


def _attn_probs(q_ref, k_ref, scale, q0, kext):
    s = lax.dot_general(q_ref[...].astype(MXU_DTYPE), k_ref[0:kext, :].astype(MXU_DTYPE), _DIMS["nt"],
                        preferred_element_type=F32) * scale
    if q0 is not None:
        qc = (q0 + lax.broadcasted_iota(jnp.int32, s.shape, 0)) // CHUNK
        kc = lax.broadcasted_iota(jnp.int32, s.shape, 1) // CHUNK
        s = jnp.where(kc <= qc, s, -1e30)
    m = jnp.max(s, axis=-1, keepdims=True)
    e = jnp.exp(s - m)
    return e / jnp.sum(e, axis=-1, keepdims=True)


def _per_query_block(one, causal, nq, tq, Sk):
    if not causal:
        one(None, Sk, None)
        return
    assert tq % CHUNK == 0
    for ib in range(nq):
        pl.when(pl.program_id(1) == ib)(functools.partial(one, ib * tq, min(Sk, (ib + 1) * tq), ib))


def attn_fwd(q, k, v, *, heads, dk, dv, v_off, v_hs, scale, causal, name, tq=256, comm=None):
    Sq, Sk = q.shape[0], k.shape[0]
    tq = _tile(Sq, tq, 8)

    def body(q_ref, k_ref, v_ref, o_ref):
        def one(q0, kext, ib):
            p = _attn_probs(q_ref, k_ref, scale, q0, kext)
            o_ref[...] = jnp.dot(p.astype(MXU_DTYPE), v_ref[0:kext, :].astype(MXU_DTYPE),
                                 preferred_element_type=F32).astype(o_ref.dtype)

        _per_query_block(one, causal, Sq // tq, tq, Sk)

    return _pcall(
        body, [q, k, v], name=name, grid=(heads, Sq // tq),
        in_specs=[pl.BlockSpec((tq, dk), lambda h, i: (i, h)), pl.BlockSpec((Sk, dk), lambda h, i: (0, h)),
                  pl.BlockSpec((Sk, dv), lambda h, i: (0, v_off + h * v_hs))],
        out_specs=pl.BlockSpec((tq, dv), lambda h, i: (i, h)),
        out_shape=jax.ShapeDtypeStruct((Sq, heads * dv), BF16), sem=("parallel", "parallel"), comm=comm)


def attn_bwd(q, k, v, do, *, heads, dk, dv, v_off, v_hs, scale, causal, name, tq=256, comm=None):
    Sq, Sk = q.shape[0], k.shape[0]
    tq = _tile(Sq, tq, 8)

    def body(q_ref, k_ref, v_ref, do_ref, dq_ref, dk_ref, dv_ref):
        @pl.when(pl.program_id(1) == 0)
        def _():
            dk_ref[...] = jnp.zeros_like(dk_ref)
            dv_ref[...] = jnp.zeros_like(dv_ref)

        def one(q0, kext, ib):
            p = _attn_probs(q_ref, k_ref, scale, q0, kext)
            dob = do_ref[...].astype(MXU_DTYPE)
            dp = lax.dot_general(dob, v_ref[0:kext, :].astype(MXU_DTYPE), _DIMS["nt"], preferred_element_type=F32)
            delta = jnp.sum(p * dp, axis=-1, keepdims=True)
            ds = (p * (dp - delta) * scale).astype(MXU_DTYPE)
            dq_ref[...] = jnp.dot(ds, k_ref[0:kext, :].astype(MXU_DTYPE), preferred_element_type=F32)
            dk_ref[0:kext, :] += lax.dot_general(ds, q_ref[...].astype(MXU_DTYPE), _DIMS["tn"],
                                                 preferred_element_type=F32)
            dv_ref[0:kext, :] += lax.dot_general(p.astype(MXU_DTYPE), dob, _DIMS["tn"], preferred_element_type=F32)

        _per_query_block(one, causal, Sq // tq, tq, Sk)

    return _pcall(
        body, [q, k, v, do], name=name, grid=(heads, Sq // tq),
        in_specs=[pl.BlockSpec((tq, dk), lambda h, i: (i, h)), pl.BlockSpec((Sk, dk), lambda h, i: (0, h)),
                  pl.BlockSpec((Sk, dv), lambda h, i: (0, v_off + h * v_hs)),
                  pl.BlockSpec((tq, dv), lambda h, i: (i, h))],
        out_specs=[pl.BlockSpec((tq, dk), lambda h, i: (i, h)), pl.BlockSpec((Sk, dk), lambda h, i: (0, h)),
                   pl.BlockSpec((Sk, dv), lambda h, i: (0, h))],
        out_shape=[jax.ShapeDtypeStruct((Sq, heads * dk), F32), jax.ShapeDtypeStruct((Sk, heads * dk), F32),
                   jax.ShapeDtypeStruct((Sk, heads * dv), F32)],
        sem=("parallel", "arbitrary"), comm=comm)


def _gla_chunk(k, g, tri_ref):
    b = jnp.dot(tri_ref[...], g, precision=lax.Precision.HIGHEST, preferred_element_type=F32)
    b_end = jnp.sum(g, axis=0, keepdims=True)
    e = jnp.exp(b_end - b)
    return k * e, e, jnp.exp(b_end)


def _gla_windows(z, q_off, k_off, v_off, rows_of):
    H, DK, DV = GLA_HEADS, GLA_DK, GLA_DV
    specs, inner = [], []
    for off, ext in ((q_off, H * DK), (k_off, H * DK), (v_off, H * DV)):
        ww, blk, io = _window(z.shape[1], off, ext)
        specs.append(pl.BlockSpec((CHUNK, ww), lambda c, blk=blk: (rows_of(c), blk)))
        inner.append(io)
    return specs, inner


def gla_fwd(z, la, tri, *, q_off, k_off, v_off, name, comm=None):
    S = z.shape[0]
    nchunk = S // CHUNK
    H, DK, DV = GLA_HEADS, GLA_DK, GLA_DV
    qscale = DK ** -0.5
    zspecs, (qi, ki, vi) = _gla_windows(z, q_off, k_off, v_off, lambda c: c)

    def body(q_ref, k_ref, v_ref, la_ref, tri_ref, o_ref, st_ref, state):
        @pl.when(pl.program_id(0) == 0)
        def _():
            state[...] = jnp.zeros_like(state)

        for h in range(H):
            dks, dvs = slice(h * DK, (h + 1) * DK), slice(h * DV, (h + 1) * DV)
            k = k_ref[:, ki + h * DK:ki + (h + 1) * DK].astype(F32)
            v = v_ref[:, vi + h * DV:vi + (h + 1) * DV]
            q = q_ref[:, qi + h * DK:qi + (h + 1) * DK].astype(F32)
            kdec, _, decay = _gla_chunk(k, la_ref[:, dks].astype(F32), tri_ref)
            ut = lax.dot_general(v.astype(MXU_DTYPE), kdec.astype(MXU_DTYPE), _DIMS["tn"], preferred_element_type=F32)
            new = state[h] * decay + ut
            state[h] = new
            st_ref[h] = new
            qs = (q * qscale).astype(MXU_DTYPE)
            o_ref[:, dvs] = lax.dot_general(qs, new.astype(MXU_DTYPE), _DIMS["nt"], preferred_element_type=F32)

    return _pcall(
        body, [z, z, z, la, tri], name=name, grid=(nchunk,),
        in_specs=zspecs + [pl.BlockSpec((CHUNK, H * DK), lambda c: (c, 0)), pl.BlockSpec((CHUNK, CHUNK), lambda c: (0, 0))],
        out_specs=[pl.BlockSpec((CHUNK, H * DV), lambda c: (c, 0)),
                   pl.BlockSpec((H, None, DV, DK), lambda c: (0, c, 0, 0))],
        out_shape=[jax.ShapeDtypeStruct((S, H * DV), F32), jax.ShapeDtypeStruct((H, nchunk, DV, DK), F32)],
        scratch_shapes=[pltpu.VMEM((H, DV, DK), F32)], sem=("arbitrary",), comm=comm)


def gla_bwd(z, la, tri, trit, states, do, *, q_off, k_off, v_off, name, comm=None):
    S = z.shape[0]
    nchunk = S // CHUNK
    H, DK, DV = GLA_HEADS, GLA_DK, GLA_DV
    qscale = DK ** -0.5
    last = nchunk - 1
    zspecs, (qi, ki, vi) = _gla_windows(z, q_off, k_off, v_off, lambda c: last - c)

    def body(q_ref, k_ref, v_ref, la_ref, tri_ref, trit_ref, st_ref, sp_ref, do_ref, dq_ref, dk_ref, dv_ref, dla_ref,
             dstate):
        c = pl.program_id(0)
        cc = last - c

        @pl.when(c == 0)
        def _():
            dstate[...] = jnp.zeros_like(dstate)

        for h in range(H):
            dks, dvs = slice(h * DK, (h + 1) * DK), slice(h * DV, (h + 1) * DV)
            kf = k_ref[:, ki + h * DK:ki + (h + 1) * DK].astype(F32)
            vb16 = v_ref[:, vi + h * DV:vi + (h + 1) * DV].astype(MXU_DTYPE)
            q = q_ref[:, qi + h * DK:qi + (h + 1) * DK].astype(F32)
            kdec, e, decay = _gla_chunk(kf, la_ref[:, dks].astype(F32), tri_ref)
            dob = do_ref[:, dvs].astype(MXU_DTYPE)
            stb = st_ref[h].astype(MXU_DTYPE)
            qs = (q * qscale).astype(MXU_DTYPE)
            dq_ref[:, dks] = jnp.dot(dob, stb, preferred_element_type=F32) * qscale
            dst = dstate[h] + lax.dot_general(dob, qs, _DIMS["tn"], preferred_element_type=F32)
            prev = jnp.where(cc > 0, sp_ref[h], 0.0)
            ddecay = jnp.sum(dst * prev, axis=0, keepdims=True)
            dstate[h] = dst * decay
            dub = dst.astype(MXU_DTYPE)
            dv_ref[:, dvs] = lax.dot_general(kdec.astype(MXU_DTYPE), dub, _DIMS["nt"], preferred_element_type=F32)
            dkdec = jnp.dot(vb16, dub, preferred_element_type=F32)
            dk_ref[:, dks] = dkdec * e
            w = dkdec * kf * e
            db_end = jnp.sum(w, axis=0, keepdims=True) + ddecay * decay
            dla_ref[:, dks] = db_end - jnp.dot(trit_ref[...], w, precision=lax.Precision.HIGHEST,
                                               preferred_element_type=F32)

    def rows(width):
        return pl.BlockSpec((CHUNK, width), lambda c: (last - c, 0))

    square = pl.BlockSpec((CHUNK, CHUNK), lambda c: (0, 0))
    return _pcall(
        body, [z, z, z, la, tri, trit, states, states, do], name=name, grid=(nchunk,),
        in_specs=zspecs + [rows(H * DK), square, square,
                           pl.BlockSpec((H, None, DV, DK), lambda c: (0, last - c, 0, 0)),
                           pl.BlockSpec((H, None, DV, DK), lambda c: (0, jnp.maximum(last - c - 1, 0), 0, 0)),
                           rows(H * DV)],
        out_specs=[rows(H * DK), rows(H * DK), rows(H * DV), rows(H * DK)],
        out_shape=[jax.ShapeDtypeStruct((S, H * DK), F32), jax.ShapeDtypeStruct((S, H * DK), F32),
                   jax.ShapeDtypeStruct((S, H * DV), F32), jax.ShapeDtypeStruct((S, H * DK), F32)],
        scratch_shapes=[pltpu.VMEM((H, DV, DK), F32)], sem=("arbitrary",), comm=comm)


def loss_head(y, target, *, name, tr=256):
    S, D = y.shape
    tr = _tile(S, tr, 8)

    def body(y_ref, t_ref, dy_ref, loss_ref):
        i = pl.program_id(0)
        err = y_ref[...] - t_ref[...]
        dy_ref[...] = err * (1.0 / D)
        part = jnp.zeros((1, LANE), F32) + 0.5 * jnp.sum(jnp.sum(err * err, axis=-1, keepdims=True) * (1.0 / D))

        @pl.when(i == 0)
        def _():
            loss_ref[...] = part

        @pl.when(i > 0)
        def _():
            loss_ref[...] += part

    spec = pl.BlockSpec((tr, D), lambda i: (i, 0))
    return pl.pallas_call(
        body, name=name, grid=(S // tr,), in_specs=[spec, spec],
        out_specs=[spec, pl.BlockSpec((1, LANE), lambda i: (0, 0))],
        out_shape=[jax.ShapeDtypeStruct((S, D), F32), jax.ShapeDtypeStruct((1, LANE), F32)],
        compiler_params=_cparams(("arbitrary",)))(y, target)


def _core_index():
    return lax.axis_index("c").astype(jnp.int32).reshape(1)


def _chip_slots():
    x, y, c = lax.axis_index("x"), lax.axis_index("y"), lax.axis_index("c")
    return jnp.stack([2 * x + y, 2 * (1 - x) + y, 2 * x + (1 - y), 2 * (1 - x) + (1 - y), c]).astype(jnp.int32)


def sum_chip_parts(own, parts, *, name, tr=256):
    _, R, C = own.shape
    tr = _tile(R, tr, 8)

    def body(idx_ref, o_ref, p0_ref, p1_ref, p2_ref, out_ref):
        acc = o_ref[...].astype(F32) + p0_ref[...].astype(F32)
        acc = acc + p1_ref[...].astype(F32)
        out_ref[...] = acc + p2_ref[...].astype(F32)

    def slot(k):
        return pl.BlockSpec((None, tr, C), lambda i, idx: (idx[k], i, 0))

    grid_spec = pltpu.PrefetchScalarGridSpec(num_scalar_prefetch=1, grid=(R // tr,),
                                             in_specs=[slot(0), slot(1), slot(2), slot(3)], out_specs=slot(4))
    return pl.pallas_call(body, name=name, grid_spec=grid_spec, out_shape=jax.ShapeDtypeStruct((2, R, C), F32),
                          compiler_params=_cparams(("parallel",)))(_chip_slots(), own, parts, parts, parts)


def add_own_half(g, got, out_dtype, *, name, tr=256):
    n, _, R, C = g.shape
    tr = _tile(R, tr, 8)

    def body(c_ref, a_ref, b_ref, o_ref):
        o_ref[...] = (a_ref[...].astype(F32) + b_ref[...].astype(F32)).astype(out_dtype)

    spec = pl.BlockSpec((None, tr, C), lambda s, i, c: (s, i, 0))
    grid_spec = pltpu.PrefetchScalarGridSpec(
        num_scalar_prefetch=1, grid=(n, R // tr),
        in_specs=[pl.BlockSpec((None, None, tr, C), lambda s, i, c: (s, c[0], i, 0)), spec], out_specs=spec)
    return pl.pallas_call(body, name=name, grid_spec=grid_spec, out_shape=jax.ShapeDtypeStruct((n, R, C), out_dtype),
                          compiler_params=_cparams(("parallel", "parallel")))(_core_index(), g, got)


def adamw(items, *, name, max_steps=16, behind=None):
    c1 = 1.0 / (1.0 - ADAM_B1 ** ADAM_STEP)
    c2 = 1.0 / (1.0 - ADAM_B2 ** ADAM_STEP)
    n = len(items)
    steps = max_steps
    while steps > 1 and any(it[0].shape[0] % (8 * steps) for it in items):
        steps //= 2
    tail = [] if behind is None else [behind]

    def body(*refs):
        for a in range(n):
            w_ref, g_ref, m_ref, v_ref = refs[4 * a:4 * a + 4]
            go_ref, d_ref, nm_ref, nv_ref = refs[4 * n + len(tail) + 4 * a:4 * n + len(tail) + 4 * a + 4]
            gv = g_ref[...]
            go_ref[...] = gv
            nm = ADAM_B1 * m_ref[...] + (1.0 - ADAM_B1) * gv
            nv = ADAM_B2 * v_ref[...] + (1.0 - ADAM_B2) * (gv * gv)
            nm_ref[...] = nm
            nv_ref[...] = nv
            d_ref[...] = -ADAM_LR * ((nm * c1) / (jnp.sqrt(nv * c2) + ADAM_EPS) + ADAM_WD * w_ref[...])

    ops, in_specs, out_specs, out_shape = [], [], [], []
    for w, g, m, v in items:
        R, C = w.shape
        spec = pl.BlockSpec((R // steps, C), lambda i: (i, 0))
        ops += [w, g, m, v]
        in_specs += [spec] * 4
        out_specs += [spec] * 4
        out_shape += [jax.ShapeDtypeStruct((R, C), F32)] * 4
    flat = _pcall(body, ops + tail, name=name, grid=(steps,), in_specs=in_specs + [ANY] * len(tail), out_specs=out_specs,
                  out_shape=out_shape, sem=("parallel",))
    return [tuple(flat[4 * a:4 * a + 4]) for a in range(n)]


def _place():
    x, y, c = lax.axis_index("x"), lax.axis_index("y"), lax.axis_index("c")
    chips = [(1 - x, y), (x, 1 - y), (1 - x, 1 - y)]
    return x, y, c, chips


def _rcopy(src, dst, send, recv, j, to):
    return pltpu.make_async_remote_copy(src_ref=src, dst_ref=dst, send_sem=send.at[j], recv_sem=recv.at[j], device_id=to,
                                        device_id_type=MESH)


def gather_stage1(shards, split):
    n = len(shards)
    ins = [s.reshape(2, s.shape[0] // 2, s.shape[1]) if sp else s for s, sp in zip(shards, split)]
    outs = [jax.ShapeDtypeStruct((N_CHIPS,) + a.shape, a.dtype) for a in ins]

    def start(in_refs, out_refs, send, recv, base):
        x, y, c, chips = _place()
        mine = 2 * x + y
        for i in range(n):
            src = in_refs[i].at[c] if split[i] else in_refs[i]
            dst = out_refs[i].at[mine, c] if split[i] else out_refs[i].at[mine]
            for k, (px, py) in enumerate(chips):
                _rcopy(src, dst, send, recv, base + 3 * i + k, (px, py, c)).start()

    def wait(in_refs, out_refs, send, recv, base):
        x, y, c, chips = _place()
        for i in range(n):
            src = in_refs[i].at[c] if split[i] else in_refs[i]
            for k, (px, py) in enumerate(chips):
                dst = out_refs[i].at[2 * px + py, c] if split[i] else out_refs[i].at[2 * px + py]
                _rcopy(src, dst, send, recv, base + 3 * i + k, (px, py, c)).wait()

    return Comm(ins, outs, 3 * n, start, wait)


def gather_stage2(slots, shards, split):
    n = len(slots)
    own = [s.reshape(2, s.shape[0] // 2, s.shape[1]) if sp else s for s, sp in zip(shards, split)]

    def copies(in_refs, out_refs, send, recv, base):
        x, y, c, chips = _place()
        sib = (x, y, 1 - c)
        for i in range(n):
            j = base + 4 * i
            mine = out_refs[i].at[2 * x + y]
            yield _rcopy(in_refs[n + i], mine, send, recv, j + 3, sib), _rcopy(in_refs[n + i], mine, send, recv, j + 3, sib)
            if split[i]:
                for k, (px, py) in enumerate(chips):
                    s = 2 * px + py
                    yield (_rcopy(in_refs[i].at[s, c], out_refs[i].at[s, c], send, recv, j + k, sib),
                           _rcopy(in_refs[i].at[s, c], out_refs[i].at[s, 1 - c], send, recv, j + k, sib))

    def start(*a):
        for out, _ in copies(*a):
            out.start()

    def wait(*a):
        for _, back in copies(*a):
            back.wait()

    return Comm(list(slots) + own, [jax.ShapeDtypeStruct(s.shape, s.dtype) for s in slots], 4 * n, start, wait,
                {i: i for i in range(n)})


def swap_halves(gs):
    n = len(gs)

    def copies(in_refs, out_refs, send, recv, base):
        x, y, c, _ = _place()
        return [_rcopy(in_refs[i].at[s, 1 - c], out_refs[i].at[s], send, recv, base + N_CHIPS * i + s, (x, y, 1 - c))
                for i in range(n) for s in range(N_CHIPS)]

    def start(*a):
        for cp in copies(*a):
            cp.start()

    def wait(*a):
        for cp in copies(*a):
            cp.wait()

    return Comm(gs, [jax.ShapeDtypeStruct((N_CHIPS,) + g.shape[2:], g.dtype) for g in gs], N_CHIPS * n, start, wait)


def exchange_chips(ps):
    n = len(ps)

    def start(in_refs, out_refs, send, recv, base):
        x, y, c, chips = _place()
        for i in range(n):
            for k, (px, py) in enumerate(chips):
                _rcopy(in_refs[i].at[2 * px + py], out_refs[i].at[2 * x + y], send, recv, base + 3 * i + k,
                       (px, py, c)).start()

    def wait(in_refs, out_refs, send, recv, base):
        x, y, c, chips = _place()
        for i in range(n):
            for k, (px, py) in enumerate(chips):
                _rcopy(in_refs[i].at[2 * px + py], out_refs[i].at[2 * px + py], send, recv, base + 3 * i + k,
                       (px, py, c)).wait()

    return Comm(ps, [jax.ShapeDtypeStruct(p.shape, p.dtype) for p in ps], 3 * n, start, wait)


def join_halves(fs):
    n = len(fs)

    def start(in_refs, out_refs, send, recv, base):
        x, y, c, _ = _place()
        for i in range(n):
            _rcopy(in_refs[i].at[c], out_refs[i].at[c], send, recv, base + i, (x, y, 1 - c)).start()

    def wait(in_refs, out_refs, send, recv, base):
        x, y, c, _ = _place()
        for i in range(n):
            _rcopy(in_refs[i].at[c], out_refs[i].at[1 - c], send, recv, base + i, (x, y, 1 - c)).wait()

    return Comm(fs, [jax.ShapeDtypeStruct(f.shape, f.dtype) for f in fs], n, start, wait, {i: i for i in range(n)})


def allreduce_small(v, *, name):
    m_per, n = v.shape

    def body(x_ref, sum_ref, all_ref, send_sems, recv_sems, local_sem):
        x, y, c, chips = _place()
        me, sibling = (x, y, c), (x, y, 1 - c)

        def rows(px, py, pc):
            return all_ref.at[pl.ds((4 * px + 2 * py + pc) * m_per, m_per), :]

        def copy(k, block, to, src=None):
            return pltpu.make_async_remote_copy(src_ref=rows(*block) if src is None else src, dst_ref=rows(*block),
                                                send_sem=send_sems.at[k], recv_sem=recv_sems.at[k], device_id=to,
                                                device_id_type=MESH)

        mine = pltpu.make_async_copy(x_ref, rows(*me), local_sem)
        mine.start()
        first = [copy(0, me, sibling, src=x_ref)]
        first += [copy(1 + j, me, (*chip, c), src=x_ref) for j, chip in enumerate(chips)]
        for cp in first:
            cp.start()
        passed = [copy(4 + j, (*chip, c), sibling) for j, chip in enumerate(chips)]
        for j, chip in enumerate(chips):
            copy(1 + j, (*chip, c), me).wait_recv()
            passed[j].start()
        copy(0, sibling, me).wait_recv()
        for j, chip in enumerate(chips):
            copy(4 + j, (*chip, 1 - c), me).wait_recv()
        for cp in first + passed:
            cp.wait_send()
        mine.wait()
        acc = all_ref[0:m_per, :]
        for d in range(1, N_DEV):
            acc = acc + all_ref[d * m_per:(d + 1) * m_per, :]
        sum_ref[...] = acc

    vm = pl.BlockSpec(memory_space=pltpu.VMEM)
    return pl.pallas_call(
        body, name=name, in_specs=[vm], out_specs=vm, out_shape=jax.ShapeDtypeStruct((m_per, n), F32),
        scratch_shapes=[pltpu.VMEM((N_DEV * m_per, n), F32), pltpu.SemaphoreType.DMA((7,)),
                        pltpu.SemaphoreType.DMA((7,)), pltpu.SemaphoreType.DMA],
    )(v)


def _cols_to_slots(w):
    r, c4 = w.shape
    return w.reshape(r, N_CHIPS, c4 // N_CHIPS).transpose(1, 0, 2)


def _slots_to_cols(w):
    n, r, c = w.shape
    return w.transpose(1, 0, 2).reshape(r, n * c)


def _pad_cols(a, width):
    return jnp.pad(a, ((0, 0), (0, width - a.shape[1])))


class InLayout:
    def __init__(self, q_rank, kv_rank):
        gk = GLA_HEADS * GLA_DK
        gv = GLA_HEADS * GLA_DV
        sizes = [q_rank, kv_rank, MLA_ROPE, gk, gk, gv, GLA_GATE_RANK, gv]
        names = ["zq", "zkv", "zkr", "gq", "gk", "gv", "zg", "zr"]
        starts = np.concatenate([[0], np.cumsum(sizes)[:-1]])
        self.ref = {n: (int(s), int(z)) for n, s, z in zip(names, starts, sizes)}
        self.ref_width = int(sum(sizes))
        self.order = ["gv", "zr", "zq", "gq", "gk", "zkv", "zkr", "zg"]
        self.off, self.size = {}, {}
        pos = 0
        for n in self.order:
            padded = -(-self.ref[n][1] // LANE) * LANE
            self.off[n], self.size[n] = pos, padded
            pos += padded
        self.width = pos
        self.shard = self.ref_width // N_CHIPS
        self.shard_pad = -(-self.shard // LANE) * LANE

    def _pieces(self, lo, hi):
        out = []
        while lo < hi:
            s = lo // self.shard
            end = min(hi, (s + 1) * self.shard)
            out.append((s * self.shard_pad + lo - s * self.shard, s * self.shard_pad + end - s * self.shard))
            lo = end
        return out

    def from_shards(self, zs):
        cols = []
        for n in self.order:
            start, size = self.ref[n]
            cols += [zs[:, a:b] for a, b in self._pieces(start, start + size)]
            if self.size[n] > size:
                cols.append(jnp.zeros((zs.shape[0], self.size[n] - size), zs.dtype))
        return jnp.concatenate(cols, axis=1)

    def to_shards(self, dz):
        names = sorted(self.ref, key=lambda n: self.ref[n][0])
        ref = jnp.concatenate([dz[:, self.off[n]:self.off[n] + self.ref[n][1]] for n in names], axis=1)
        ref = ref.reshape(dz.shape[0], N_CHIPS, self.shard)
        return jnp.pad(ref, ((0, 0), (0, 0), (0, self.shard_pad - self.shard))).reshape(dz.shape[0], -1)


def _pad_q_up(w):
    r = w.shape[0]
    w = w.reshape(r, MLA_HEADS, MLA_QK)
    w = jnp.pad(w, ((0, 0), (0, 0), (0, MLA_HEAD_PAD - MLA_QK)))
    return w.reshape(r, MLA_HEADS * MLA_HEAD_PAD)


def _unpad_q_up(g):
    r = g.shape[0]
    return g.reshape(r, MLA_HEADS, MLA_HEAD_PAD)[:, :, :MLA_QK].reshape(r, MLA_HEADS * MLA_QK)


def _rope_tables(positions):
    half = MLA_ROPE // 2
    inv_freq = ROPE_THETA ** (-jnp.arange(half, dtype=F32) / half)
    ang = positions.astype(F32).reshape(-1, 1) * inv_freq
    cos, sin = jnp.cos(ang), jnp.sin(ang)
    s = ang.shape[0]
    cosf = jnp.concatenate([cos, cos, jnp.ones((s, LANE - MLA_ROPE), F32)], axis=1)
    sinf = jnp.concatenate([sin, sin, jnp.zeros((s, LANE - MLA_ROPE), F32)], axis=1)
    rot = np.zeros((LANE, LANE), np.float32)
    for j in range(half):
        rot[j + half, j] = -1.0
        rot[j, j + half] = 1.0
    return cosf, sinf, jnp.asarray(rot)


SMALL = ["ffn1_norm", "mix_norm", "q_a_norm", "kv_a_norm", "mla_q_norm", "mla_k_norm", "gla_b_gate", "gla_out_norm",
         "mem_attn_norm", "mem_norm", "mem_q_norm", "mem_k_norm", "ffn2_norm"]
BIG = ["ffn1_w_gate", "ffn1_w_up", "ffn1_w_down", "w_in", "w_q_up", "w_kv_up", "w_out", "mem_w_q", "mem_w_k",
       "mem_w_v", "mem_w_o", "ffn2_w_gate", "ffn2_w_up", "ffn2_w_down"]
COL_SHARDED = {"ffn1_w_gate", "ffn1_w_up", "w_in", "w_q_up", "w_kv_up", "gla_w_gate2", "mem_w_o", "ffn2_w_gate", "ffn2_w_up"}
WEIGHTS = ["ffn1_norm", "ffn1_w_gate", "ffn1_w_up", "ffn1_w_down", "mix_norm", "w_in", "q_a_norm", "w_q_up", "kv_a_norm",
           "w_kv_up", "mla_q_norm", "mla_k_norm", "gla_w_gate2", "gla_b_gate", "gla_out_norm", "w_out", "mem_attn_norm",
           "mem_norm", "mem_w_q", "mem_w_k", "mem_w_v", "mem_w_o", "mem_q_norm", "mem_k_norm", "ffn2_norm", "ffn2_w_gate",
           "ffn2_w_up", "ffn2_w_down"]


def _pack_small(vals, rows=8):
    flat = jnp.concatenate([v.reshape(-1).astype(F32) for v in vals])
    n = flat.shape[0]
    per = -(-n // (rows * LANE)) * LANE
    return jnp.pad(flat, (0, rows * per - n)).reshape(rows, per)


def _unpack_small(packed, shapes):
    flat = packed.reshape(-1)
    out, pos = [], 0
    for s in shapes:
        n = int(np.prod(s))
        out.append(flat[pos:pos + n].reshape(s))
        pos += n
    return out


FFN1 = ["ffn1_w_gate", "ffn1_w_up", "ffn1_w_down"]
FFN2 = ["ffn2_w_gate", "ffn2_w_up", "ffn2_w_down"]
SLOT_WEIGHTS = {"ffn1_w_gate", "ffn1_w_up", "ffn2_w_gate", "ffn2_w_up", "w_in"}
MID_A = ["w_in", "w_q_up", "w_kv_up", "gla_w_gate2"]
MID_B = ["w_out", "mem_w_q", "mem_w_k", "mem_w_v", "mem_w_o"]


def _with(res, comm):
    return res if comm is not None else (res, None)


def kernel(x, mem, positions, ffn1_norm, ffn1_w_gate, ffn1_w_up, ffn1_w_down, mix_norm, w_in, q_a_norm, w_q_up, kv_a_norm, w_kv_up, mla_q_norm, mla_k_norm, gla_w_gate2, gla_b_gate, gla_out_norm, w_out, mem_attn_norm, mem_norm, mem_w_q, mem_w_k, mem_w_v, mem_w_o, mem_q_norm, mem_k_norm, ffn2_norm, ffn2_w_gate, ffn2_w_up, ffn2_w_down, loss_target, m_ffn1_norm, m_ffn1_w_gate, m_ffn1_w_up, m_ffn1_w_down, m_mix_norm, m_w_in, m_q_a_norm, m_w_q_up, m_kv_a_norm, m_w_kv_up, m_mla_q_norm, m_mla_k_norm, m_gla_w_gate2, m_gla_b_gate, m_gla_out_norm, m_w_out, m_mem_attn_norm, m_mem_norm, m_mem_w_q, m_mem_w_k, m_mem_w_v, m_mem_w_o, m_mem_q_norm, m_mem_k_norm, m_ffn2_norm, m_ffn2_w_gate, m_ffn2_w_up, m_ffn2_w_down, v_ffn1_norm, v_ffn1_w_gate, v_ffn1_w_up, v_ffn1_w_down, v_mix_norm, v_w_in, v_q_a_norm, v_w_q_up, v_kv_a_norm, v_w_kv_up, v_mla_q_norm, v_mla_k_norm, v_gla_w_gate2, v_gla_b_gate, v_gla_out_norm, v_w_out, v_mem_attn_norm, v_mem_norm, v_mem_w_q, v_mem_w_k, v_mem_w_v, v_mem_w_o, v_mem_q_norm, v_mem_k_norm, v_ffn2_norm, v_ffn2_w_gate, v_ffn2_w_up, v_ffn2_w_down):
    args = dict(locals())
    two_d = lambda a: a[0] if a.ndim == 3 else a
    W = {n: two_d(args[n]) for n in WEIGHTS}
    M1 = {n: two_d(args["m_" + n]) for n in WEIGHTS}
    V2 = {n: two_d(args["v_" + n]) for n in WEIGHTS}
    xs, mems, tgt = x[0], mem[0], loss_target[0]
    S, D = xs.shape
    chip = 2 * lax.axis_index("x") + lax.axis_index("y")

    q_rank, kv_rank = W["w_q_up"].shape[0], W["w_kv_up"].shape[0]
    lay = InLayout(q_rank, kv_rank)
    off = lay.off
    shard16 = {n: W[n].astype(BF16) for n in BIG + ["gla_w_gate2"]}
    shard16["w_in"] = _pad_cols(shard16["w_in"], lay.shard_pad)
    full = {}

    def stage1(names):
        return gather_stage1([shard16[n] for n in names], [n != "gla_w_gate2" for n in names])

    def stage2(names, slots):
        return gather_stage2(slots, [shard16[n] for n in names], [n != "gla_w_gate2" for n in names])

    def finish(names, slots):
        for n, s in zip(names, slots):
            s = s.reshape((N_CHIPS,) + shard16[n].shape)
            if n in SLOT_WEIGHTS:
                full[n] = s
            else:
                full[n] = _slots_to_cols(s) if n in COL_SHARDED else s.reshape(-1, s.shape[2])

    finish(FFN1, run_comm(stage2(FFN1, run_comm(stage1(FFN1), name="gather_ffn1")), name="pass_ffn1"))
    cosf, sinf, rot = _rope_tables(positions[0])
    tri = jnp.asarray(np.tril(np.ones((CHUNK, CHUNK), np.float32)))
    gqn = W["mla_q_norm"][:, :MLA_NOPE]
    gqr = _pad_cols(W["mla_q_norm"][:, MLA_NOPE:], LANE)
    gkn = W["mla_k_norm"][:, :MLA_NOPE]
    gkr = _pad_cols(W["mla_k_norm"][:, MLA_NOPE:], LANE)
    HP = MLA_HEAD_PAD
    mla_scale = MLA_QK ** -0.5
    mem_scale = MEM_HEAD_DIM ** -0.5
    mla_w = MLA_HEADS * MLA_V
    gla_w = GLA_HEADS * GLA_DV
    mem_w = MEM_HEADS * MEM_HEAD_DIM

    n1 = row_fwd(rms_fn, [V(xs)], [W["ffn1_norm"]], [(D, BF16)], [(0, 0, 0)], name="ffn1_norm")[0]
    (gate1, up1, act1), mid_a1 = ffn_up(n1, full["ffn1_w_gate"], full["ffn1_w_up"], name="ffn1_up", comm=stage1(MID_A))
    na = len(MID_A)
    x1, got = mm([(act1, full["ffn1_w_down"])], "nn", F32, alpha=0.5, res=xs, name="ffn1_down",
                 comm=merge_comms(stage2(MID_A, mid_a1), stage1(MID_B)))
    ffn1_saved = (n1, gate1, up1, act1)
    finish(MID_A, got[:na])
    mid_b1 = got[na:]
    w_q_up_p = _pad_q_up(full["w_q_up"])
    w_gate2_p = jnp.pad(full["gla_w_gate2"], ((0, LANE - GLA_GATE_RANK), (0, 0)))
    h = row_fwd(rms_fn, [V(x1)], [W["mix_norm"]], [(D, BF16)], [(0, 0, 0)], name="mix_norm")[0]
    nb = len(MID_B)
    z_shards, got = mm([(h, full["w_in"])], "nn", F32, name="in_proj", b_slots=True,
                       comm=merge_comms(stage2(MID_B, mid_b1), stage1(FFN2[:1])))
    z = lay.from_shards(z_shards)
    finish(MID_B, got[:nb])
    f2_gate = got[nb:]
    qa = row_fwd(rms_fn, [V(z, off["zq"], q_rank)], [W["q_a_norm"]], [(q_rank, BF16)], [(0, 0, 0)], name="q_a_norm")[0]
    kva = row_fwd(rms_fn, [V(z, off["zkv"], kv_rank)], [W["kv_a_norm"]], [(kv_rank, BF16)], [(0, 0, 0)], name="kv_a_norm")[0]
    qraw = mm([(qa, w_q_up_p)], "nn", F32, name="q_up")
    kvraw = mm([(kva, full["w_kv_up"])], "nn", F32, name="kv_up")
    tabs = [V(cosf, diff=False), V(sinf, diff=False)]
    q_rows = [V(qraw, 0, LANE, HP), V(qraw, LANE, LANE, HP)] + tabs
    k_rows = [V(kvraw, 0, LANE, HP), V(z, off["zkr"], LANE, 0)] + tabs
    qh = row_fwd(qk_prep_fn, q_rows, [gqn, gqr, rot], [(MLA_HEADS * HP, BF16)], [(0, 0, HP), (0, LANE, HP)],
                 heads=MLA_HEADS, name="q_prep")[0]
    kh = row_fwd(qk_prep_fn, k_rows, [gkn, gkr, rot], [(MLA_HEADS * HP, BF16)], [(0, 0, HP), (0, LANE, HP)],
                 heads=MLA_HEADS, name="k_prep")[0]
    mla_kw = dict(heads=MLA_HEADS, dk=HP, dv=MLA_V, v_off=1, v_hs=2, scale=mla_scale, causal=True)
    o_mla = attn_fwd(qh, kh, kvraw, name="mla_attn", **mla_kw)

    zg = z[:, off["zg"]:off["zg"] + LANE]
    pre = mm([(zg, w_gate2_p)], "nn", F32, name="gla_gate")
    la = row_fwd(gate_fn, [V(pre)], [W["gla_b_gate"]], [(pre.shape[1], F32)], [(0, 0, 0)], name="gla_log_decay")[0]
    gla_kw = dict(q_off=off["gq"], k_off=off["gk"], v_off=off["gv"])
    (o_raw, states), f2_up = gla_fwd(z, la, tri, name="gla_scan", comm=stage1(FFN2[1:2]), **gla_kw)
    gla_rows = [V(o_raw, 0, GLA_DV, GLA_DV), V(z, off["zr"], GLA_DV, GLA_DV)]
    o_gla = row_fwd(gla_out_fn, gla_rows, [W["gla_out_norm"]], [(gla_w, BF16)], [(0, 0, GLA_DV)], heads=GLA_HEADS,
                    name="gla_out")[0]
    o_cat = jnp.concatenate([o_mla, o_gla], axis=1)
    x2, got = mm([(o_cat, full["w_out"])], "nn", F32, res=x1, name="out_proj", comm=stage2(FFN2[:2], f2_gate + f2_up))
    finish(FFN2[:2], got)

    hm = row_fwd(rms_fn, [V(x2)], [W["mem_attn_norm"]], [(D, BF16)], [(0, 0, 0)], name="mem_attn_norm")[0]
    mn = row_fwd(rms_fn, [V(mems)], [W["mem_norm"]], [(D, BF16)], [(0, 0, 0)], name="mem_norm")[0]
    qm_raw = mm([(hm, full["mem_w_q"])], "nn", F32, name="mem_q")
    km_raw = mm([(mn, full["mem_w_k"])], "nn", F32, name="mem_k")
    vm = mm([(mn, full["mem_w_v"])], "nn", F32, name="mem_v")
    hd = MEM_HEAD_DIM
    qm = row_fwd(rms_fn, [V(qm_raw, 0, hd, hd)], [W["mem_q_norm"]], [(mem_w, BF16)], [(0, 0, hd)], heads=MEM_HEADS,
                 name="mem_q_norm")[0]
    km = row_fwd(rms_fn, [V(km_raw, 0, hd, hd)], [W["mem_k_norm"]], [(mem_w, BF16)], [(0, 0, hd)], heads=MEM_HEADS,
                 name="mem_k_norm")[0]
    mem_kw = dict(heads=MEM_HEADS, dk=hd, dv=hd, v_off=0, v_hs=1, scale=mem_scale, causal=False)
    om = attn_fwd(qm, km, vm, name="mem_attn", **mem_kw)
    x3 = mm([(om, full["mem_w_o"])], "nn", F32, res=x2, name="mem_o")

    n2 = row_fwd(rms_fn, [V(x3)], [W["ffn2_norm"]], [(D, BF16)], [(0, 0, 0)], name="ffn2_norm")[0]
    (gate2, up2, act2), f2_down = ffn_up(n2, full["ffn2_w_gate"], full["ffn2_w_up"], name="ffn2_up", comm=stage1(FFN2[2:]))
    finish(FFN2[2:], run_comm(stage2(FFN2[2:], f2_down), name="pass_ffn2_down"))
    y = mm([(act2, full["ffn2_w_down"])], "nn", F32, alpha=0.5, res=x3, name="ffn2_down")
    dy, loss_part = loss_head(y, tgt, name="loss_head")
    loss = lax.psum(loss_part[0, 0], ("x", "y", "c"))

    G, chip_sum, reduced = {}, {}, {}

    def to_halves(n):
        g = G[n]
        if n in SLOT_WEIGHTS:
            s = g
        else:
            s = _cols_to_slots(g) if n in COL_SHARDED else g.reshape(N_CHIPS, g.shape[0] // N_CHIPS, g.shape[1])
        return s.reshape(N_CHIPS, 2, s.shape[1] // 2, s.shape[2])

    def add2(names, halves, got):
        for n, a, b in zip(names, halves, got):
            chip_sum[n] = add_own_half(a, b, BF16, name=f"rs_add2_{n}")

    to_join = []

    def add4_join(names, parts):
        for n, p in zip(names, parts):
            to_join.append((n, sum_chip_parts(chip_sum[n], p, name=f"rs_add4_{n}")))

    def with_joins(comm):
        names, totals = [n for n, _ in to_join], [t for _, t in to_join]
        to_join.clear()
        if not names:
            return comm, lambda got: got
        own = 0 if comm is None else len(comm.out_shapes)
        joined = join_halves(totals)

        def split(got):
            for n, b in zip(names, got[own:]):
                reduced[n] = b.reshape(-1, b.shape[2])[:, :W[n].shape[1]]
            return got[:own]

        return (joined if comm is None else merge_comms(comm, joined)), split

    def flush_joins():
        comm, split = with_joins(None)
        if comm is not None:
            split(run_comm(comm, name=f"rs_join_{len(reduced)}"))

    def ffn_backward(dout, xin, tag, saved, dact_comm=None, after_dact=None, after_dwd=None):
        n_, gate, up, act = saved
        nd, ng, nu = f"{tag}_w_down", f"{tag}_w_gate", f"{tag}_w_up"
        (dgate, dup), got0 = _with(ffn_dact(dout, full[nd], gate, up, 0.5, name=f"{tag}_dact", comm=dact_comm), dact_comm)
        dwd_comm = after_dact(got0) if after_dact else None
        G[nd], got1 = _with(mm([(act, dout)], "tn", F32, alpha=0.5, name=f"{tag}_dwd", tm=1408, tn=1024,
                               comm=dwd_comm), dwd_comm)
        if after_dwd:
            after_dwd(got1)
        hd_ = to_halves(nd)
        comm, split = with_joins(swap_halves([hd_]))
        G[ng], got = mm([(n_, dgate)], "tn", F32, name=f"{tag}_dwg", out_slots=True, tm=1024, tn=1408, rows_inner=True,
                        comm=comm)
        add2([nd], [hd_], split(got))
        hg = to_halves(ng)
        G[nu], got_g = mm([(n_, dup)], "tn", F32, name=f"{tag}_dwu", out_slots=True, tm=1024, tn=1408, rows_inner=True,
                          comm=swap_halves([hg]))
        add2([ng], [hg], got_g)
        hu = to_halves(nu)
        dn, (parts_d, got_u) = mm([(dgate, full[ng]), (dup, full[nu])], "nt", F32, name=f"{tag}_dn", b_slots=True,
                                  tn=1024, tk=1408,
                                  comm=merge_comms(exchange_chips([chip_sum[nd]]), swap_halves([hu])))
        add2([nu], [hu], [got_u])
        dx, G[f"{tag}_norm"] = row_bwd(rms_fn, [V(xin)], [W[f"{tag}_norm"]], [V(dn)], const_diff=[True], res=dout,
                                       name=f"{tag}_dnorm")
        add4_join([nd], [parts_d])
        return dx, exchange_chips([chip_sum[ng]]), exchange_chips([chip_sum[nu]])

    g3, ffn2_gate_xchg, ffn2_up_xchg = ffn_backward(dy, x3, "ffn2", (n2, gate2, up2, act2))

    d_om = mm([(g3, full["mem_w_o"])], "nt", F32, name="mem_o_dx")
    G["mem_w_o"] = mm([(om, g3)], "tn", F32, name="mem_o_dw")
    dqm, dkm, dvm = attn_bwd(qm, km, vm, d_om, name="mem_attn_bwd", **mem_kw)
    dqm_raw, G["mem_q_norm"] = row_bwd(rms_fn, [V(qm_raw, 0, hd, hd)], [W["mem_q_norm"]], [V(dqm, 0, hd, hd)],
                                       const_diff=[True], heads=MEM_HEADS, row_dtype=BF16, name="mem_q_norm_bwd")
    dkm_raw, G["mem_k_norm"] = row_bwd(rms_fn, [V(km_raw, 0, hd, hd)], [W["mem_k_norm"]], [V(dkm, 0, hd, hd)],
                                       const_diff=[True], heads=MEM_HEADS, row_dtype=BF16, name="mem_k_norm_bwd")
    dhm = mm([(dqm_raw, full["mem_w_q"])], "nt", F32, name="mem_q_dx")
    G["mem_w_q"] = mm([(hm, dqm_raw)], "tn", F32, name="mem_q_dw")
    dmn = mm([(dkm_raw, full["mem_w_k"]), (dvm, full["mem_w_v"])], "nt", F32, name="mem_kv_dx")
    G["mem_w_k"] = mm([(mn, dkm_raw)], "tn", F32, name="mem_k_dw")
    G["mem_w_v"] = mm([(mn, dvm)], "tn", F32, name="mem_v_dw")
    _, G["mem_norm"] = row_bwd(rms_fn, [V(mems)], [W["mem_norm"]], [V(dmn)], const_diff=[True], row_dtype=BF16,
                               name="mem_norm_bwd")
    g2, G["mem_attn_norm"] = row_bwd(rms_fn, [V(x2)], [W["mem_attn_norm"]], [V(dhm)], const_diff=[True], res=g3,
                                     name="mem_attn_norm_bwd")

    d_ocat = mm([(g2, full["w_out"])], "nt", F32, name="out_proj_dx")
    G["w_out"] = mm([(o_cat, g2)], "tn", F32, name="out_proj_dw")

    d_oraw, d_zr, G["gla_out_norm"] = row_bwd(gla_out_fn, gla_rows, [W["gla_out_norm"]],
                                              [V(d_ocat, mla_w, GLA_DV, GLA_DV)], const_diff=[True], heads=GLA_HEADS,
                                              name="gla_out_bwd")
    mid_b_halves = [to_halves(n) for n in MID_B]
    comm, split = with_joins(merge_comms(ffn2_gate_xchg, swap_halves(mid_b_halves)))
    (d_gq, d_gk, d_gv, d_la), got = gla_bwd(z, la, tri, tri.T, states, d_oraw, name="gla_scan_bwd", comm=comm, **gla_kw)
    got = split(got)
    add4_join(["ffn2_w_gate"], got[:1])
    add2(MID_B, mid_b_halves, got[1:])
    d_pre, G["gla_b_gate"] = row_bwd(gate_fn, [V(pre)], [W["gla_b_gate"]], [V(d_la)], const_diff=[True], row_dtype=BF16,
                                     name="gla_log_decay_bwd")
    d_zg = mm([(d_pre, w_gate2_p)], "nt", BF16, name="gla_gate_dx")
    G["gla_w_gate2"] = mm([(zg, d_pre)], "tn", F32, name="gla_gate_dw")[:GLA_GATE_RANK]

    comm, split = with_joins(ffn2_up_xchg)
    (d_qh, d_kh, d_v), got = attn_bwd(qh, kh, kvraw, d_ocat, name="mla_attn_bwd", comm=comm, **mla_kw)
    add4_join(["ffn2_w_up"], split(got))
    cq = [V(d_qh, 0, LANE, HP), V(d_qh, LANE, LANE, HP)]
    ck = [V(d_kh, 0, LANE, HP), V(d_kh, LANE, LANE, HP)]
    d_qraw, d_gqn, d_gqr = row_bwd(qk_prep_fn, q_rows, [gqn, gqr, rot], cq, const_diff=[True, True, False],
                                   heads=MLA_HEADS, row_dtype=BF16, pack={0: (0, HP), 1: (LANE, HP)},
                                   pack_width=MLA_HEADS * HP, name="q_prep_bwd")
    d_kvraw, d_zkr, d_gkn, d_gkr = row_bwd(qk_prep_fn, k_rows, [gkn, gkr, rot], ck, const_diff=[True, True, False],
                                           heads=MLA_HEADS, row_dtype=BF16, pack={0: (0, HP)}, pack_width=MLA_HEADS * HP,
                                           fills=[(V(d_v, 0, MLA_V, MLA_V), LANE, HP)], name="k_prep_bwd")
    G["mla_q_norm"] = jnp.concatenate([d_gqn, d_gqr[:, :MLA_ROPE]], axis=1)
    G["mla_k_norm"] = jnp.concatenate([d_gkn, d_gkr[:, :MLA_ROPE]], axis=1)
    d_qa = mm([(d_qraw, w_q_up_p)], "nt", F32, name="q_up_dx")
    G["w_q_up"] = _unpad_q_up(mm([(qa, d_qraw)], "tn", F32, name="q_up_dw"))
    d_kva = mm([(d_kvraw, full["w_kv_up"])], "nt", F32, name="kv_up_dx")
    G["w_kv_up"] = mm([(kva, d_kvraw)], "tn", F32, name="kv_up_dw")
    d_zq, G["q_a_norm"] = row_bwd(rms_fn, [V(z, off["zq"], q_rank)], [W["q_a_norm"]], [V(d_qa)], const_diff=[True],
                                  row_dtype=BF16, name="q_a_norm_bwd")
    d_zkv, G["kv_a_norm"] = row_bwd(rms_fn, [V(z, off["zkv"], kv_rank)], [W["kv_a_norm"]], [V(d_kva)], const_diff=[True],
                                    row_dtype=BF16, name="kv_a_norm_bwd")

    seg = {"gv": d_gv, "zr": d_zr, "zq": d_zq, "gq": d_gq, "gk": d_gk, "zkv": d_zkv, "zkr": d_zkr, "zg": d_zg}
    dz = jnp.concatenate([_pad_cols(seg[n].astype(BF16), lay.size[n]) for n in lay.order], axis=1)
    comm, split = with_joins(exchange_chips([chip_sum[n] for n in MID_B]))
    dz_shards = lay.to_shards(dz)
    dh, got = mm([(dz_shards, full["w_in"])], "nt", F32, name="in_proj_dx", b_slots=True, comm=comm)
    add4_join(MID_B, split(got))
    G["w_in"] = mm([(h, dz_shards)], "tn", F32, name="in_proj_dw", out_slots=True)
    g1, G["mix_norm"] = row_bwd(rms_fn, [V(x1)], [W["mix_norm"]], [V(dh)], const_diff=[True], res=g2, name="mix_norm_bwd")

    mid_a = [n for n in MID_A if n != "gla_w_gate2"]
    mid_a_halves = [to_halves(n) for n in mid_a]

    def mid_a_sums(got):
        add2(mid_a, mid_a_halves, got)
        return exchange_chips([chip_sum[n] for n in mid_a])

    gx, ffn1_gate_xchg, ffn1_up_xchg = ffn_backward(
        g1, xs, "ffn1", ffn1_saved, dact_comm=swap_halves(mid_a_halves), after_dact=mid_a_sums,
        after_dwd=lambda parts: add4_join(mid_a, parts))

    grad, delta, new_m, new_v = {}, {}, {}, {}

    def adam_group(names, tag, behind=None):
        if any(n not in reduced for n in names):
            flush_joins()
        res = adamw([(W[n], reduced[n], M1[n], V2[n]) for n in names], name=f"adamw_{tag}", behind=behind)
        for n, (g_, d_, m_, v_) in zip(names, res):
            grad[n], delta[n], new_m[n], new_v[n] = g_, d_, m_, v_

    tail_xchg = merge_comms(ffn1_gate_xchg, ffn1_up_xchg)
    started = start_comm(tail_xchg, name="tail_xchg_start")
    adam_group(FFN2, "ffn2", behind=started[-1])
    adam_group(mid_a + MID_B, "mid", behind=started[-1])
    add4_join(["ffn1_w_gate", "ffn1_w_up"], wait_comm(tail_xchg, started, delta[MID_B[-1]], name="tail_xchg_wait"))
    adam_group(FFN1, "ffn1")

    small_names = SMALL + ["gla_w_gate2"]
    small_sum = allreduce_small(_pack_small([G[n] for n in small_names]), name="allreduce_small")
    small_g = dict(zip(small_names, _unpack_small(small_sum, [G[n].shape for n in small_names])))
    shard_c = W["gla_w_gate2"].shape[1]
    grad["gla_w_gate2"] = lax.dynamic_slice_in_dim(small_g["gla_w_gate2"], chip * shard_c, shard_c, axis=1)
    pw = _pack_small([W[n] for n in SMALL] + [W["gla_w_gate2"]])
    pg = _pack_small([small_g[n] for n in SMALL] + [grad["gla_w_gate2"]])
    pm = _pack_small([M1[n] for n in SMALL] + [M1["gla_w_gate2"]])
    pv = _pack_small([V2[n] for n in SMALL] + [V2["gla_w_gate2"]])
    (_, pd, pnm, pnv), = adamw([(pw, pg, pm, pv)], name="adamw_small")
    shapes = [W[n].shape for n in small_names]
    for n, d_, m_, v_ in zip(small_names, _unpack_small(pd, shapes), _unpack_small(pnm, shapes), _unpack_small(pnv, shapes)):
        delta[n], new_m[n], new_v[n] = d_, m_, v_
        if n != "gla_w_gate2":
            grad[n] = small_g[n]

    lead = lambda d: [d[n].reshape(args[n].shape) for n in WEIGHTS]
    return (loss, gx[None], *lead(grad), *lead(delta), *lead(new_m), *lead(new_v))
```

```python
import functools
import math

import numpy as np
import jax
import jax.numpy as jnp
from jax import lax
from jax.experimental import pallas as pl
from jax.experimental.pallas import tpu as pltpu

F32 = jnp.float32
BF16 = jnp.bfloat16
MXU_DTYPE = jnp.bfloat16
MESH = pl.DeviceIdType.MESH
ANY = pl.BlockSpec(memory_space=pl.ANY)

LANE = 128
EPS = 1e-6
CHUNK = 64
MLA_HEADS = 8
MLA_NOPE = 128
MLA_ROPE = 64
MLA_QK = MLA_NOPE + MLA_ROPE
MLA_V = 128
MLA_HEAD_PAD = 2 * LANE
ROPE_THETA = 10000.0
GLA_HEADS = 4
GLA_DK = 128
GLA_DV = 256
GLA_GATE_RANK = 16
GLA_TAU = 16.0
MEM_HEADS = 4
MEM_HEAD_DIM = 128
N_CHIPS = 4
N_DEV = 8

ADAM_LR = 0.001
ADAM_B1 = 0.9
ADAM_B2 = 0.999
ADAM_EPS = 1e-08
ADAM_WD = 0.01
ADAM_STEP = 10

VMEM_LIMIT = 56 * 1024 * 1024


def _cparams(sem=None):
    if sem is None:
        return pltpu.CompilerParams(vmem_limit_bytes=VMEM_LIMIT)
    return pltpu.CompilerParams(dimension_semantics=sem, vmem_limit_bytes=VMEM_LIMIT)


def _tile(dim, pref, unit=LANE):
    if dim <= pref:
        return dim
    t = (pref // unit) * unit
    while t > unit and dim % t:
        t -= unit
    assert dim % t == 0, (dim, pref, unit)
    return t


class Comm:
    def __init__(self, ins, out_shapes, nsem, start, wait, aliases=None):
        self.ins, self.out_shapes, self.nsem = list(ins), list(out_shapes), nsem
        self.start, self.wait, self.aliases = start, wait, dict(aliases or {})


def merge_comms(a, b):
    ai, ao = len(a.ins), len(a.out_shapes)

    def start(ins, outs, send, recv, base):
        a.start(ins[:ai], outs[:ao], send, recv, base)
        b.start(ins[ai:], outs[ao:], send, recv, base + a.nsem)

    def wait(ins, outs, send, recv, base):
        a.wait(ins[:ai], outs[:ao], send, recv, base)
        b.wait(ins[ai:], outs[ao:], send, recv, base + a.nsem)

    aliases = dict(a.aliases)
    aliases.update({ai + i: ao + o for i, o in b.aliases.items()})
    return Comm(a.ins + b.ins, a.out_shapes + b.out_shapes, a.nsem + b.nsem, start, wait, aliases)


def run_comm(comm, *, name):
    ni, no = len(comm.ins), len(comm.out_shapes)

    def body(*refs):
        ins, outs = refs[:ni], refs[ni:ni + no]
        send, recv = refs[ni + no:]
        comm.start(ins, outs, send, recv, 0)
        comm.wait(ins, outs, send, recv, 0)

    return pl.pallas_call(
        body, name=name, in_specs=[ANY] * ni, out_specs=[ANY] * no, out_shape=comm.out_shapes,
        input_output_aliases=comm.aliases,
        scratch_shapes=[pltpu.SemaphoreType.DMA((comm.nsem,)), pltpu.SemaphoreType.DMA((comm.nsem,))])(*comm.ins)


HBM = pl.BlockSpec(memory_space=pltpu.HBM)
SEM = pl.BlockSpec(memory_space=pltpu.SEMAPHORE)


def start_comm(comm, *, name, after=None):
    assert not comm.aliases
    ni, no = len(comm.ins), len(comm.out_shapes)
    tail = [] if after is None else [after]

    def body(*refs):
        srcs, lands = refs[:ni], refs[ni:ni + no]
        send, recv = refs[ni + no + len(tail)], refs[ni + no + len(tail) + 1]
        token = refs[-1]
        comm.start(srcs, lands, send, recv, 0)
        token[...] = jnp.zeros_like(token)

    through = [pltpu.HBM(a.shape, a.dtype) for a in comm.ins] + [pltpu.HBM(s.shape, s.dtype) for s in comm.out_shapes]
    ops = [pltpu.with_memory_space_constraint(a, pltpu.HBM) for a in comm.ins]
    ops += [pltpu.with_memory_space_constraint(lax.empty(s.shape, s.dtype), pltpu.HBM) for s in comm.out_shapes]
    ops += tail
    res = pl.pallas_call(
        body, name=name, in_specs=[HBM] * (ni + no) + [ANY] * len(tail),
        out_shape=[pltpu.SemaphoreType.DMA((comm.nsem,)), pltpu.SemaphoreType.DMA((comm.nsem,))] + through
        + [jax.ShapeDtypeStruct((8, LANE), F32)],
        out_specs=[SEM, SEM] + [HBM] * (ni + no) + [pl.BlockSpec(memory_space=pltpu.VMEM)],
        input_output_aliases={i: 2 + i for i in range(ni + no)},
        compiler_params=pltpu.CompilerParams(has_side_effects=pltpu.SideEffectType.DATAFLOW_SIDE_EFFECTING))(*ops)
    return res[0], res[1], list(res[2:2 + ni]), list(res[2 + ni:2 + ni + no]), res[-1]


def wait_comm(comm, started, after, *, name):
    send, recv, srcs, lands, _ = started
    ni, no = len(srcs), len(lands)

    def body(*refs):
        comm.wait(refs[:ni], refs[ni:ni + no], refs[ni + no], refs[ni + no + 1], 0)

    res = pl.pallas_call(
        body, name=name, in_specs=[HBM] * (ni + no) + [SEM, SEM, ANY],
        out_shape=[pltpu.HBM(a.shape, a.dtype) for a in srcs + lands], out_specs=[HBM] * (ni + no),
        input_output_aliases={i: i for i in range(ni + no)},
        compiler_params=pltpu.CompilerParams(has_side_effects=pltpu.SideEffectType.DATAFLOW_SIDE_EFFECTING),
    )(*srcs, *lands, send, recv, after)
    return list(res[ni:])


def _pcall(body, ops, *, name, grid, in_specs, out_specs, out_shape, sem, scratch_shapes=(), comm=None, behind=None):
    if behind is not None:
        n_real, inner = len(ops), body
        ops, in_specs = list(ops) + [behind], list(in_specs) + [ANY]

        def body(*refs):
            inner(*refs[:n_real], *refs[n_real + 1:])

    if comm is None:
        return pl.pallas_call(body, name=name, grid=grid, in_specs=in_specs, out_specs=out_specs, out_shape=out_shape,
                              scratch_shapes=list(scratch_shapes), compiler_params=_cparams(sem))(*ops)
    multi = isinstance(out_shape, (list, tuple))
    k_out_shape = list(out_shape) if multi else [out_shape]
    k_out_specs = list(out_specs) if multi else [out_specs]
    nki, nko, nks = len(ops), len(k_out_shape), len(scratch_shapes)
    nci, nco = len(comm.ins), len(comm.out_shapes)

    def wrapped(*refs):
        p = 0
        k_in = refs[p:p + nki]; p += nki
        c_in = refs[p:p + nci]; p += nci
        k_out = refs[p:p + nko]; p += nko
        c_out = refs[p:p + nco]; p += nco
        k_scr = refs[p:p + nks]; p += nks
        send, recv = refs[p:]
        first = pl.program_id(0) == 0
        last = pl.program_id(0) == grid[0] - 1
        for a in range(1, len(grid)):
            first = jnp.logical_and(first, pl.program_id(a) == 0)
            last = jnp.logical_and(last, pl.program_id(a) == grid[a] - 1)

        @pl.when(first)
        def _():
            comm.start(c_in, c_out, send, recv, 0)

        body(*k_in, *k_out, *k_scr)

        @pl.when(last)
        def _():
            comm.wait(c_in, c_out, send, recv, 0)

    res = pl.pallas_call(
        wrapped, name=name, grid=grid, in_specs=list(in_specs) + [ANY] * nci, out_specs=k_out_specs + [ANY] * nco,
        out_shape=k_out_shape + comm.out_shapes,
        input_output_aliases={nki + i: nko + o for i, o in comm.aliases.items()},
        scratch_shapes=list(scratch_shapes) + [pltpu.SemaphoreType.DMA((comm.nsem,)), pltpu.SemaphoreType.DMA((comm.nsem,))],
        compiler_params=_cparams(("arbitrary",) * len(grid)))(*ops, *comm.ins)
    k_res = list(res[:nko]) if multi else res[0]
    return k_res, list(res[nko:])


_DIMS = {"nn": (((1,), (0,)), ((), ())), "nt": (((1,), (1,)), ((), ())), "tn": (((0,), (0,)), ((), ()))}


def _blockspec(shape, index, rows_inner):
    return pl.BlockSpec(shape, (lambda j, i, k: index(i, j, k)) if rows_inner else index)


def mm(pairs, mode, out_dtype, *, name, alpha=1.0, res=None, tm=1024, tn=1024, tk=4096, b_slots=False, out_slots=False,
       rows_inner=False, comm=None, behind=None):
    a0, b0 = pairs[0]
    if b_slots:
        b_rows, b_cols = b0.shape[1], N_CHIPS * b0.shape[2]
    else:
        b_rows, b_cols = b0.shape
    (M, K) = a0.shape[::-1] if mode == "tn" else a0.shape
    N = b_rows if mode == "nt" else b_cols
    shard = (b_cols if b_slots else N) // N_CHIPS
    tm = _tile(M, tm)
    tn = _tile(shard if (out_slots or (b_slots and mode != "nt")) else N, tn)
    tk = _tile(shard if (b_slots and mode == "nt") else K, tk)
    nk = K // tk
    npairs = len(pairs)
    dims = _DIMS[mode]
    spec = functools.partial(_blockspec, rows_inner=rows_inner)
    if mode == "tn":
        a_spec = spec((tk, tm), lambda i, j, k: (k, i))
    else:
        a_spec = spec((tm, tk), lambda i, j, k: (i, k))
    per = shard // (tk if mode == "nt" else tn)
    if mode == "nt":
        b_spec = (spec((None, tn, tk), lambda i, j, k: (k // per, j, k % per)) if b_slots else
                  spec((tn, tk), lambda i, j, k: (j, k)))
    else:
        b_spec = (spec((None, tk, tn), lambda i, j, k: (j // per, k, j % per)) if b_slots else
                  spec((tk, tn), lambda i, j, k: (k, j)))
    if out_slots:
        assert res is None and mode != "nt"
        o_spec = spec((None, tm, tn), lambda i, j, k: (j // per, i, j % per))
        out_sds = jax.ShapeDtypeStruct((N_CHIPS, M, shard), out_dtype)
    else:
        o_spec = spec((tm, tn), lambda i, j, k: (i, j))
        out_sds = jax.ShapeDtypeStruct((M, N), out_dtype)
    has_res = res is not None

    def body(*refs):
        ab = refs[:2 * npairs]
        res_ref = refs[2 * npairs] if has_res else None
        o_ref = refs[2 * npairs + int(has_res)]

        def products():
            r = None
            for p in range(npairs):
                d = lax.dot_general(ab[2 * p][...].astype(MXU_DTYPE), ab[2 * p + 1][...].astype(MXU_DTYPE), dims,
                                    preferred_element_type=F32)
                r = d if r is None else r + d
            return r

        def finish(r):
            if alpha != 1.0:
                r = r * alpha
            if has_res:
                r = res_ref[...].astype(F32) + r
            o_ref[...] = r.astype(out_dtype)

        if nk == 1:
            finish(products())
            return
        acc = refs[-1]
        k = pl.program_id(2)

        @pl.when(k == 0)
        def _():
            acc[...] = jnp.zeros_like(acc)

        acc[...] += products()

        @pl.when(k == nk - 1)
        def _():
            finish(acc[...])

    ops, specs = [], []
    for a, b in pairs:
        ops += [a, b]
        specs += [a_spec, b_spec]
    if has_res:
        ops.append(res)
        specs.append(o_spec)
    blocks = (N // tn, M // tm) if rows_inner else (M // tm, N // tn)
    return _pcall(body, ops, name=name, grid=blocks + (nk,), in_specs=specs, out_specs=o_spec, out_shape=out_sds,
                  scratch_shapes=[pltpu.VMEM((tm, tn), F32)] if nk > 1 else [],
                  sem=("parallel", "parallel", "arbitrary"), comm=comm, behind=behind)


def _sigmoid(x):
    return 1.0 / (1.0 + jnp.exp(-x))


def ffn_up(n, wg, wu, *, name, tm=512, tn=1408, comm=None):
    M, K = n.shape
    shard = wg.shape[2]
    N = N_CHIPS * shard
    tm, tn = _tile(M, tm), _tile(shard, tn)
    per = shard // tn
    w_spec = pl.BlockSpec((None, K, tn), lambda j, i: (j // per, 0, j % per))

    def body(n_ref, wg_ref, wu_ref, g_ref, u_ref, a_ref):
        nv = n_ref[...].astype(MXU_DTYPE)
        g = jnp.dot(nv, wg_ref[...].astype(MXU_DTYPE), preferred_element_type=F32)
        u = jnp.dot(nv, wu_ref[...].astype(MXU_DTYPE), preferred_element_type=F32)
        g_ref[...] = g.astype(g_ref.dtype)
        u_ref[...] = u.astype(u_ref.dtype)
        a_ref[...] = (g * _sigmoid(g) * u).astype(a_ref.dtype)

    o_spec = pl.BlockSpec((tm, tn), lambda j, i: (i, j))
    sds = jax.ShapeDtypeStruct((M, N), BF16)
    return _pcall(
        body, [n, wg, wu], name=name, grid=(N // tn, M // tm),
        in_specs=[pl.BlockSpec((tm, K), lambda j, i: (i, 0)), w_spec, w_spec],
        out_specs=[o_spec, o_spec, o_spec], out_shape=[sds, sds, sds], sem=("parallel", "parallel"), comm=comm)


def ffn_dact(dy, wd, gate, up, alpha, *, name, tm=512, tn=1408, comm=None, behind=None):
    M, K = dy.shape
    N = wd.shape[0]
    tm, tn = _tile(M, tm), _tile(N, tn)

    def body(dy_ref, wd_ref, g_ref, u_ref, dg_ref, du_ref):
        da = lax.dot_general(dy_ref[...].astype(MXU_DTYPE), wd_ref[...].astype(MXU_DTYPE), _DIMS["nt"],
                             preferred_element_type=F32) * alpha
        g = g_ref[...].astype(F32)
        u = u_ref[...].astype(F32)
        s = _sigmoid(g)
        du_ref[...] = (da * (g * s)).astype(du_ref.dtype)
        dg_ref[...] = (da * u * (s * (1.0 + g * (1.0 - s)))).astype(dg_ref.dtype)

    o_spec = pl.BlockSpec((tm, tn), lambda j, i: (i, j))
    sds = jax.ShapeDtypeStruct((M, N), BF16)
    return _pcall(
        body, [dy, wd, gate, up], name=name, grid=(N // tn, M // tm),
        in_specs=[pl.BlockSpec((tm, K), lambda j, i: (i, 0)), pl.BlockSpec((tn, K), lambda j, i: (j, 0)), o_spec, o_spec],
        out_specs=[o_spec, o_spec], out_shape=[sds, sds], sem=("parallel", "parallel"), comm=comm, behind=behind)


def _window(width, off, ext):
    ww = LANE
    while ww < width:
        if ww >= ext and off // ww == (off + ext - 1) // ww and width % ww == 0:
            break
        ww *= 2
    else:
        ww = width
    return ww, off // ww, off - (off // ww) * ww


class V:
    def __init__(self, arr, off=0, w=None, hs=0, diff=True):
        self.arr, self.off, self.hs, self.diff = arr, off, hs, diff
        self.w = arr.shape[1] - off if w is None else w

    def window(self, heads, tr):
        ww, blk, inner = _window(self.arr.shape[1], self.off, (heads - 1) * self.hs + self.w)
        return pl.BlockSpec((tr, ww), lambda i, blk=blk: (i, blk)), inner


def _const_spec(c):
    return pl.BlockSpec(c.shape, lambda i: (0, 0))


def row_fwd(fn, rows, consts, outs, out_map, *, heads=1, tr=256, name):
    S = rows[0].arr.shape[0]
    tr = _tile(S, tr, 8)
    wins = [v.window(heads, tr) for v in rows]
    nr, nc = len(rows), len(consts)

    def body(*refs):
        row_refs, const_refs, out_refs = refs[:nr], refs[nr:nr + nc], refs[nr + nc:]
        cv = [c[...].astype(F32) for c in const_refs]
        for h in range(heads):
            rv = []
            for v, (_, io), r in zip(rows, wins, row_refs):
                lo = io + h * v.hs
                rv.append(r[:, lo:lo + v.w].astype(F32))
            res = fn(*rv, *cv)
            for (ai, off, hs), o in zip(out_map, res):
                lo = off + h * hs
                out_refs[ai][:, lo:lo + o.shape[1]] = o.astype(out_refs[ai].dtype)

    return pl.pallas_call(
        body, name=name, grid=(S // tr,),
        in_specs=[w[0] for w in wins] + [_const_spec(c) for c in consts],
        out_specs=[pl.BlockSpec((tr, w), lambda i: (i, 0)) for w, _ in outs],
        out_shape=[jax.ShapeDtypeStruct((S, w), d) for w, d in outs],
        compiler_params=_cparams(("parallel",)))(*[v.arr for v in rows], *consts)


def row_bwd(fn, rows, consts, cots, *, const_diff, heads=1, tr=256, res=None, row_dtype=F32, pack=None, pack_width=0,
            fills=(), name):
    S = rows[0].arr.shape[0]
    tr = _tile(S, tr, 8)
    pack = dict(pack or {})
    nr, nc, nct, nf = len(rows), len(consts), len(cots), len(fills)
    wins = [v.window(heads, tr) for v in rows]
    cwins = [v.window(heads, tr) for v in cots]
    fwins = [v.window(heads, tr) for v, _, _ in fills]
    drows = [k for k, v in enumerate(rows) if v.diff]
    dconsts = [k for k in range(nc) if const_diff[k]]
    has_res = res is not None
    assert not (has_res and 0 in pack)
    widths = [pack_width] if pack else []
    place = []
    for n, k in enumerate(drows):
        if n in pack:
            place.append((0,) + tuple(pack[n]))
        else:
            place.append((len(widths), 0, rows[k].w))
            widths.append(rows[k].w * (heads if rows[k].hs else 1))

    def body(*refs):
        row_refs = refs[:nr]
        const_refs = refs[nr:nr + nc]
        cot_refs = refs[nr + nc:nr + nc + nct]
        p = nr + nc + nct
        fill_refs = refs[p:p + nf]
        p += nf
        res_ref = refs[p] if has_res else None
        p += int(has_res)
        grow_refs = refs[p:p + len(widths)]
        gconst_refs = refs[p + len(widths):]
        i = pl.program_id(0)
        cv = [c[...].astype(F32) for c in const_refs]
        shared = [None] * len(drows)
        gc_sum = [None] * len(dconsts)
        for h in range(heads):
            rv = []
            for v, (_, io), r in zip(rows, wins, row_refs):
                lo = io + h * v.hs
                rv.append(r[:, lo:lo + v.w].astype(F32))
            ct = []
            for v, (_, io), r in zip(cots, cwins, cot_refs):
                lo = io + h * v.hs
                ct.append(r[:, lo:lo + v.w].astype(F32))

            def closed(*d):
                rr, cc = list(rv), list(cv)
                for k, val in zip(drows, d[:len(drows)]):
                    rr[k] = val
                for k, val in zip(dconsts, d[len(drows):]):
                    cc[k] = val
                return tuple(fn(*rr, *cc))

            _, vjp = jax.vjp(closed, *[rv[k] for k in drows], *[cv[k] for k in dconsts])
            grads = vjp(tuple(ct))
            for n, k in enumerate(drows):
                g = grads[n]
                if rows[k].hs == 0 and heads > 1:
                    shared[n] = g if shared[n] is None else shared[n] + g
                else:
                    if n == 0 and has_res:
                        g = g + res_ref[:, h * rows[k].w:(h + 1) * rows[k].w].astype(F32)
                    out, off, hs = place[n]
                    grow_refs[out][:, off + h * hs:off + h * hs + rows[k].w] = g.astype(row_dtype)
            for (v, off, hs), (_, io), r in zip(fills, fwins, fill_refs):
                lo = io + h * v.hs
                grow_refs[0][:, off + h * hs:off + h * hs + v.w] = r[:, lo:lo + v.w].astype(row_dtype)
            for n in range(len(dconsts)):
                g = grads[len(drows) + n]
                gc_sum[n] = g if gc_sum[n] is None else gc_sum[n] + g
        for n, k in enumerate(drows):
            if shared[n] is not None:
                g = shared[n]
                if n == 0 and has_res:
                    g = g + res_ref[...].astype(F32)
                grow_refs[place[n][0]][...] = g.astype(row_dtype)

        @pl.when(i == 0)
        def _():
            for n in range(len(dconsts)):
                gconst_refs[n][...] = gc_sum[n]

        @pl.when(i > 0)
        def _():
            for n in range(len(dconsts)):
                gconst_refs[n][...] += gc_sum[n]

    in_specs = [w[0] for w in wins] + [_const_spec(c) for c in consts] + [w[0] for w in cwins] + [w[0] for w in fwins]
    ops = [v.arr for v in rows] + list(consts) + [v.arr for v in cots] + [v.arr for v, _, _ in fills]
    if has_res:
        in_specs.append(pl.BlockSpec((tr, widths[0]), lambda i: (i, 0)))
        ops.append(res)
    out_specs = [pl.BlockSpec((tr, w), lambda i: (i, 0)) for w in widths]
    out_shape = [jax.ShapeDtypeStruct((S, w), row_dtype) for w in widths]
    for k in dconsts:
        out_specs.append(_const_spec(consts[k]))
        out_shape.append(jax.ShapeDtypeStruct(consts[k].shape, F32))
    return pl.pallas_call(body, name=name, grid=(S // tr,), in_specs=in_specs, out_specs=out_specs,
                          out_shape=out_shape, compiler_params=_cparams(("arbitrary",)))(*ops)


def _rms(x, g, n=None):
    n = x.shape[-1] if n is None else n
    ms = jnp.sum(x * x, axis=-1, keepdims=True) * (1.0 / n)
    return x * lax.rsqrt(ms + EPS) * g


def rms_fn(x, g):
    return (_rms(x, g),)


def qk_prep_fn(nope, rope, cos, sin, gn, gr, rot):
    ms = (jnp.sum(nope * nope, axis=-1, keepdims=True) + jnp.sum(rope * rope, axis=-1, keepdims=True)) * (1.0 / MLA_QK)
    r = lax.rsqrt(ms + EPS)
    on = nope * r * gn
    orr = rope * r * gr
    turned = jnp.dot(orr, rot, precision=lax.Precision.HIGHEST, preferred_element_type=F32)
    return on, orr * cos + turned * sin


def gla_out_fn(o, zr, g):
    return (_rms(o, g) * (zr * _sigmoid(zr)),)


def gate_fn(pre, b):
    t = pre + b
    return ((jnp.minimum(t, 0.0) - jnp.log(1.0 + jnp.exp(-jnp.abs(t)))) * (1.0 / GLA_TAU),)


def _attn_probs(q_ref, k_ref, scale, q0, kext):
    s = lax.dot_general(q_ref[...].astype(MXU_DTYPE), k_ref[0:kext, :].astype(MXU_DTYPE), _DIMS["nt"],
                        preferred_element_type=F32) * scale
    if q0 is not None:
        qc = (q0 + lax.broadcasted_iota(jnp.int32, s.shape, 0)) // CHUNK
        kc = lax.broadcasted_iota(jnp.int32, s.shape, 1) // CHUNK
        s = jnp.where(kc <= qc, s, -1e30)
    m = jnp.max(s, axis=-1, keepdims=True)
    e = jnp.exp(s - m)
    return e / jnp.sum(e, axis=-1, keepdims=True)


def _per_query_block(one, causal, nq, tq, Sk):
    if not causal:
        one(None, Sk, None)
        return
    assert tq % CHUNK == 0
    for ib in range(nq):
        pl.when(pl.program_id(1) == ib)(functools.partial(one, ib * tq, min(Sk, (ib + 1) * tq), ib))


def attn_fwd(q, k, v, *, heads, dk, dv, v_off, v_hs, scale, causal, name, tq=256, comm=None):
    Sq, Sk = q.shape[0], k.shape[0]
    tq = _tile(Sq, tq, 8)

    def body(q_ref, k_ref, v_ref, o_ref):
        def one(q0, kext, ib):
            p = _attn_probs(q_ref, k_ref, scale, q0, kext)
            o_ref[...] = jnp.dot(p.astype(MXU_DTYPE), v_ref[0:kext, :].astype(MXU_DTYPE),
                                 preferred_element_type=F32).astype(o_ref.dtype)

        _per_query_block(one, causal, Sq // tq, tq, Sk)

    return _pcall(
        body, [q, k, v], name=name, grid=(heads, Sq // tq),
        in_specs=[pl.BlockSpec((tq, dk), lambda h, i: (i, h)), pl.BlockSpec((Sk, dk), lambda h, i: (0, h)),
                  pl.BlockSpec((Sk, dv), lambda h, i: (0, v_off + h * v_hs))],
        out_specs=pl.BlockSpec((tq, dv), lambda h, i: (i, h)),
        out_shape=jax.ShapeDtypeStruct((Sq, heads * dv), BF16), sem=("parallel", "parallel"), comm=comm)


def attn_bwd(q, k, v, do, *, heads, dk, dv, v_off, v_hs, scale, causal, name, tq=256, comm=None):
    Sq, Sk = q.shape[0], k.shape[0]
    tq = _tile(Sq, tq, 8)

    def body(q_ref, k_ref, v_ref, do_ref, dq_ref, dk_ref, dv_ref):
        @pl.when(pl.program_id(1) == 0)
        def _():
            dk_ref[...] = jnp.zeros_like(dk_ref)
            dv_ref[...] = jnp.zeros_like(dv_ref)

        def one(q0, kext, ib):
            p = _attn_probs(q_ref, k_ref, scale, q0, kext)
            dob = do_ref[...].astype(MXU_DTYPE)
            dp = lax.dot_general(dob, v_ref[0:kext, :].astype(MXU_DTYPE), _DIMS["nt"], preferred_element_type=F32)
            delta = jnp.sum(p * dp, axis=-1, keepdims=True)
            ds = (p * (dp - delta) * scale).astype(MXU_DTYPE)
            dq_ref[...] = jnp.dot(ds, k_ref[0:kext, :].astype(MXU_DTYPE), preferred_element_type=F32)
            dk_ref[0:kext, :] += lax.dot_general(ds, q_ref[...].astype(MXU_DTYPE), _DIMS["tn"],
                                                 preferred_element_type=F32)
            dv_ref[0:kext, :] += lax.dot_general(p.astype(MXU_DTYPE), dob, _DIMS["tn"], preferred_element_type=F32)

        _per_query_block(one, causal, Sq // tq, tq, Sk)

    return _pcall(
        body, [q, k, v, do], name=name, grid=(heads, Sq // tq),
        in_specs=[pl.BlockSpec((tq, dk), lambda h, i: (i, h)), pl.BlockSpec((Sk, dk), lambda h, i: (0, h)),
                  pl.BlockSpec((Sk, dv), lambda h, i: (0, v_off + h * v_hs)),
                  pl.BlockSpec((tq, dv), lambda h, i: (i, h))],
        out_specs=[pl.BlockSpec((tq, dk), lambda h, i: (i, h)), pl.BlockSpec((Sk, dk), lambda h, i: (0, h)),
                   pl.BlockSpec((Sk, dv), lambda h, i: (0, h))],
        out_shape=[jax.ShapeDtypeStruct((Sq, heads * dk), F32), jax.ShapeDtypeStruct((Sk, heads * dk), F32),
                   jax.ShapeDtypeStruct((Sk, heads * dv), F32)],
        sem=("parallel", "arbitrary"), comm=comm)


def _gla_chunk(k, g, tri_ref):
    b = jnp.dot(tri_ref[...], g, precision=lax.Precision.HIGHEST, preferred_element_type=F32)
    b_end = jnp.sum(g, axis=0, keepdims=True)
    e = jnp.exp(b_end - b)
    return k * e, e, jnp.exp(b_end)


def _gla_windows(z, q_off, k_off, v_off, rows_of):
    H, DK, DV = GLA_HEADS, GLA_DK, GLA_DV
    specs, inner = [], []
    for off, ext in ((q_off, H * DK), (k_off, H * DK), (v_off, H * DV)):
        ww, blk, io = _window(z.shape[1], off, ext)
        specs.append(pl.BlockSpec((CHUNK, ww), lambda c, blk=blk: (rows_of(c), blk)))
        inner.append(io)
    return specs, inner


def gla_fwd(z, la, tri, *, q_off, k_off, v_off, name, comm=None):
    S = z.shape[0]
    nchunk = S // CHUNK
    H, DK, DV = GLA_HEADS, GLA_DK, GLA_DV
    qscale = DK ** -0.5
    zspecs, (qi, ki, vi) = _gla_windows(z, q_off, k_off, v_off, lambda c: c)

    def body(q_ref, k_ref, v_ref, la_ref, tri_ref, o_ref, st_ref, state):
        @pl.when(pl.program_id(0) == 0)
        def _():
            state[...] = jnp.zeros_like(state)

        for h in range(H):
            dks, dvs = slice(h * DK, (h + 1) * DK), slice(h * DV, (h + 1) * DV)
            k = k_ref[:, ki + h * DK:ki + (h + 1) * DK].astype(F32)
            v = v_ref[:, vi + h * DV:vi + (h + 1) * DV]
            q = q_ref[:, qi + h * DK:qi + (h + 1) * DK].astype(F32)
            kdec, _, decay = _gla_chunk(k, la_ref[:, dks].astype(F32), tri_ref)
            ut = lax.dot_general(v.astype(MXU_DTYPE), kdec.astype(MXU_DTYPE), _DIMS["tn"], preferred_element_type=F32)
            new = state[h] * decay + ut
            state[h] = new
            st_ref[h] = new
            qs = (q * qscale).astype(MXU_DTYPE)
            o_ref[:, dvs] = lax.dot_general(qs, new.astype(MXU_DTYPE), _DIMS["nt"], preferred_element_type=F32)

    return _pcall(
        body, [z, z, z, la, tri], name=name, grid=(nchunk,),
        in_specs=zspecs + [pl.BlockSpec((CHUNK, H * DK), lambda c: (c, 0)), pl.BlockSpec((CHUNK, CHUNK), lambda c: (0, 0))],
        out_specs=[pl.BlockSpec((CHUNK, H * DV), lambda c: (c, 0)),
                   pl.BlockSpec((H, None, DV, DK), lambda c: (0, c, 0, 0))],
        out_shape=[jax.ShapeDtypeStruct((S, H * DV), F32), jax.ShapeDtypeStruct((H, nchunk, DV, DK), F32)],
        scratch_shapes=[pltpu.VMEM((H, DV, DK), F32)], sem=("arbitrary",), comm=comm)


def gla_bwd(z, la, tri, trit, states, do, *, q_off, k_off, v_off, name, comm=None):
    S = z.shape[0]
    nchunk = S // CHUNK
    H, DK, DV = GLA_HEADS, GLA_DK, GLA_DV
    qscale = DK ** -0.5
    last = nchunk - 1
    zspecs, (qi, ki, vi) = _gla_windows(z, q_off, k_off, v_off, lambda c: last - c)

    def body(q_ref, k_ref, v_ref, la_ref, tri_ref, trit_ref, st_ref, sp_ref, do_ref, dq_ref, dk_ref, dv_ref, dla_ref,
             dstate):
        c = pl.program_id(0)
        cc = last - c

        @pl.when(c == 0)
        def _():
            dstate[...] = jnp.zeros_like(dstate)

        for h in range(H):
            dks, dvs = slice(h * DK, (h + 1) * DK), slice(h * DV, (h + 1) * DV)
            kf = k_ref[:, ki + h * DK:ki + (h + 1) * DK].astype(F32)
            vb16 = v_ref[:, vi + h * DV:vi + (h + 1) * DV].astype(MXU_DTYPE)
            q = q_ref[:, qi + h * DK:qi + (h + 1) * DK].astype(F32)
            kdec, e, decay = _gla_chunk(kf, la_ref[:, dks].astype(F32), tri_ref)
            dob = do_ref[:, dvs].astype(MXU_DTYPE)
            stb = st_ref[h].astype(MXU_DTYPE)
            qs = (q * qscale).astype(MXU_DTYPE)
            dq_ref[:, dks] = jnp.dot(dob, stb, preferred_element_type=F32) * qscale
            dst = dstate[h] + lax.dot_general(dob, qs, _DIMS["tn"], preferred_element_type=F32)
            prev = jnp.where(cc > 0, sp_ref[h], 0.0)
            ddecay = jnp.sum(dst * prev, axis=0, keepdims=True)
            dstate[h] = dst * decay
            dub = dst.astype(MXU_DTYPE)
            dv_ref[:, dvs] = lax.dot_general(kdec.astype(MXU_DTYPE), dub, _DIMS["nt"], preferred_element_type=F32)
            dkdec = jnp.dot(vb16, dub, preferred_element_type=F32)
            dk_ref[:, dks] = dkdec * e
            w = dkdec * kf * e
            db_end = jnp.sum(w, axis=0, keepdims=True) + ddecay * decay
            dla_ref[:, dks] = db_end - jnp.dot(trit_ref[...], w, precision=lax.Precision.HIGHEST,
                                               preferred_element_type=F32)

    def rows(width):
        return pl.BlockSpec((CHUNK, width), lambda c: (last - c, 0))

    square = pl.BlockSpec((CHUNK, CHUNK), lambda c: (0, 0))
    return _pcall(
        body, [z, z, z, la, tri, trit, states, states, do], name=name, grid=(nchunk,),
        in_specs=zspecs + [rows(H * DK), square, square,
                           pl.BlockSpec((H, None, DV, DK), lambda c: (0, last - c, 0, 0)),
                           pl.BlockSpec((H, None, DV, DK), lambda c: (0, jnp.maximum(last - c - 1, 0), 0, 0)),
                           rows(H * DV)],
        out_specs=[rows(H * DK), rows(H * DK), rows(H * DV), rows(H * DK)],
        out_shape=[jax.ShapeDtypeStruct((S, H * DK), F32), jax.ShapeDtypeStruct((S, H * DK), F32),
                   jax.ShapeDtypeStruct((S, H * DV), F32), jax.ShapeDtypeStruct((S, H * DK), F32)],
        scratch_shapes=[pltpu.VMEM((H, DV, DK), F32)], sem=("arbitrary",), comm=comm)


def loss_head(y, target, *, name, tr=256):
    S, D = y.shape
    tr = _tile(S, tr, 8)

    def body(y_ref, t_ref, dy_ref, loss_ref):
        i = pl.program_id(0)
        err = y_ref[...] - t_ref[...]
        dy_ref[...] = err * (1.0 / D)
        part = jnp.zeros((1, LANE), F32) + 0.5 * jnp.sum(jnp.sum(err * err, axis=-1, keepdims=True) * (1.0 / D))

        @pl.when(i == 0)
        def _():
            loss_ref[...] = part

        @pl.when(i > 0)
        def _():
            loss_ref[...] += part

    spec = pl.BlockSpec((tr, D), lambda i: (i, 0))
    return pl.pallas_call(
        body, name=name, grid=(S // tr,), in_specs=[spec, spec],
        out_specs=[spec, pl.BlockSpec((1, LANE), lambda i: (0, 0))],
        out_shape=[jax.ShapeDtypeStruct((S, D), F32), jax.ShapeDtypeStruct((1, LANE), F32)],
        compiler_params=_cparams(("arbitrary",)))(y, target)


def _core_index():
    return lax.axis_index("c").astype(jnp.int32).reshape(1)


def _chip_slots():
    x, y, c = lax.axis_index("x"), lax.axis_index("y"), lax.axis_index("c")
    return jnp.stack([2 * x + y, 2 * (1 - x) + y, 2 * x + (1 - y), 2 * (1 - x) + (1 - y), c]).astype(jnp.int32)


def sum_chip_parts(own, parts, *, name, tr=256):
    _, R, C = own.shape
    tr = _tile(R, tr, 8)

    def body(idx_ref, o_ref, p0_ref, p1_ref, p2_ref, out_ref):
        acc = o_ref[...].astype(F32) + p0_ref[...].astype(F32)
        acc = acc + p1_ref[...].astype(F32)
        out_ref[...] = acc + p2_ref[...].astype(F32)

    def slot(k):
        return pl.BlockSpec((None, tr, C), lambda i, idx: (idx[k], i, 0))

    grid_spec = pltpu.PrefetchScalarGridSpec(num_scalar_prefetch=1, grid=(R // tr,),
                                             in_specs=[slot(0), slot(1), slot(2), slot(3)], out_specs=slot(4))
    return pl.pallas_call(body, name=name, grid_spec=grid_spec, out_shape=jax.ShapeDtypeStruct((2, R, C), F32),
                          compiler_params=_cparams(("parallel",)))(_chip_slots(), own, parts, parts, parts)


def add_own_half(g, got, out_dtype, *, name, tr=256):
    n, _, R, C = g.shape
    tr = _tile(R, tr, 8)

    def body(c_ref, a_ref, b_ref, o_ref):
        o_ref[...] = (a_ref[...].astype(F32) + b_ref[...].astype(F32)).astype(out_dtype)

    spec = pl.BlockSpec((None, tr, C), lambda s, i, c: (s, i, 0))
    grid_spec = pltpu.PrefetchScalarGridSpec(
        num_scalar_prefetch=1, grid=(n, R // tr),
        in_specs=[pl.BlockSpec((None, None, tr, C), lambda s, i, c: (s, c[0], i, 0)), spec], out_specs=spec)
    return pl.pallas_call(body, name=name, grid_spec=grid_spec, out_shape=jax.ShapeDtypeStruct((n, R, C), out_dtype),
                          compiler_params=_cparams(("parallel", "parallel")))(_core_index(), g, got)


def adamw(items, *, name, max_steps=16, behind=None):
    c1 = 1.0 / (1.0 - ADAM_B1 ** ADAM_STEP)
    c2 = 1.0 / (1.0 - ADAM_B2 ** ADAM_STEP)
    n = len(items)
    steps = max_steps
    while steps > 1 and any(it[0].shape[0] % (8 * steps) for it in items):
        steps //= 2
    tail = [] if behind is None else [behind]

    def body(*refs):
        for a in range(n):
            w_ref, g_ref, m_ref, v_ref = refs[4 * a:4 * a + 4]
            go_ref, d_ref, nm_ref, nv_ref = refs[4 * n + len(tail) + 4 * a:4 * n + len(tail) + 4 * a + 4]
            gv = g_ref[...]
            go_ref[...] = gv
            nm = ADAM_B1 * m_ref[...] + (1.0 - ADAM_B1) * gv
            nv = ADAM_B2 * v_ref[...] + (1.0 - ADAM_B2) * (gv * gv)
            nm_ref[...] = nm
            nv_ref[...] = nv
            d_ref[...] = -ADAM_LR * ((nm * c1) / (jnp.sqrt(nv * c2) + ADAM_EPS) + ADAM_WD * w_ref[...])

    ops, in_specs, out_specs, out_shape = [], [], [], []
    for w, g, m, v in items:
        R, C = w.shape
        spec = pl.BlockSpec((R // steps, C), lambda i: (i, 0))
        ops += [w, g, m, v]
        in_specs += [spec] * 4
        out_specs += [spec] * 4
        out_shape += [jax.ShapeDtypeStruct((R, C), F32)] * 4
    flat = _pcall(body, ops + tail, name=name, grid=(steps,), in_specs=in_specs + [ANY] * len(tail), out_specs=out_specs,
                  out_shape=out_shape, sem=("parallel",))
    return [tuple(flat[4 * a:4 * a + 4]) for a in range(n)]


def _place():
    x, y, c = lax.axis_index("x"), lax.axis_index("y"), lax.axis_index("c")
    chips = [(1 - x, y), (x, 1 - y), (1 - x, 1 - y)]
    return x, y, c, chips


def _rcopy(src, dst, send, recv, j, to):
    return pltpu.make_async_remote_copy(src_ref=src, dst_ref=dst, send_sem=send.at[j], recv_sem=recv.at[j], device_id=to,
                                        device_id_type=MESH)


def gather_stage1(shards, split):
    n = len(shards)
    ins = [s.reshape(2, s.shape[0] // 2, s.shape[1]) if sp else s for s, sp in zip(shards, split)]
    outs = [jax.ShapeDtypeStruct((N_CHIPS,) + a.shape, a.dtype) for a in ins]

    def start(in_refs, out_refs, send, recv, base):
        x, y, c, chips = _place()
        mine = 2 * x + y
        for i in range(n):
            src = in_refs[i].at[c] if split[i] else in_refs[i]
            dst = out_refs[i].at[mine, c] if split[i] else out_refs[i].at[mine]
            for k, (px, py) in enumerate(chips):
                _rcopy(src, dst, send, recv, base + 3 * i + k, (px, py, c)).start()

    def wait(in_refs, out_refs, send, recv, base):
        x, y, c, chips = _place()
        for i in range(n):
            src = in_refs[i].at[c] if split[i] else in_refs[i]
            for k, (px, py) in enumerate(chips):
                dst = out_refs[i].at[2 * px + py, c] if split[i] else out_refs[i].at[2 * px + py]
                _rcopy(src, dst, send, recv, base + 3 * i + k, (px, py, c)).wait()

    return Comm(ins, outs, 3 * n, start, wait)


def gather_stage2(slots, shards, split):
    n = len(slots)
    own = [s.reshape(2, s.shape[0] // 2, s.shape[1]) if sp else s for s, sp in zip(shards, split)]

    def copies(in_refs, out_refs, send, recv, base):
        x, y, c, chips = _place()
        sib = (x, y, 1 - c)
        for i in range(n):
            j = base + 4 * i
            mine = out_refs[i].at[2 * x + y]
            yield _rcopy(in_refs[n + i], mine, send, recv, j + 3, sib), _rcopy(in_refs[n + i], mine, send, recv, j + 3, sib)
            if split[i]:
                for k, (px, py) in enumerate(chips):
                    s = 2 * px + py
                    yield (_rcopy(in_refs[i].at[s, c], out_refs[i].at[s, c], send, recv, j + k, sib),
                           _rcopy(in_refs[i].at[s, c], out_refs[i].at[s, 1 - c], send, recv, j + k, sib))

    def start(*a):
        for out, _ in copies(*a):
            out.start()

    def wait(*a):
        for _, back in copies(*a):
            back.wait()

    return Comm(list(slots) + own, [jax.ShapeDtypeStruct(s.shape, s.dtype) for s in slots], 4 * n, start, wait,
                {i: i for i in range(n)})


def swap_halves(gs):
    n = len(gs)

    def copies(in_refs, out_refs, send, recv, base):
        x, y, c, _ = _place()
        return [_rcopy(in_refs[i].at[s, 1 - c], out_refs[i].at[s], send, recv, base + N_CHIPS * i + s, (x, y, 1 - c))
                for i in range(n) for s in range(N_CHIPS)]

    def start(*a):
        for cp in copies(*a):
            cp.start()

    def wait(*a):
        for cp in copies(*a):
            cp.wait()

    return Comm(gs, [jax.ShapeDtypeStruct((N_CHIPS,) + g.shape[2:], g.dtype) for g in gs], N_CHIPS * n, start, wait)


def exchange_chips(ps):
    n = len(ps)

    def start(in_refs, out_refs, send, recv, base):
        x, y, c, chips = _place()
        for i in range(n):
            for k, (px, py) in enumerate(chips):
                _rcopy(in_refs[i].at[2 * px + py], out_refs[i].at[2 * x + y], send, recv, base + 3 * i + k,
                       (px, py, c)).start()

    def wait(in_refs, out_refs, send, recv, base):
        x, y, c, chips = _place()
        for i in range(n):
            for k, (px, py) in enumerate(chips):
                _rcopy(in_refs[i].at[2 * px + py], out_refs[i].at[2 * px + py], send, recv, base + 3 * i + k,
                       (px, py, c)).wait()

    return Comm(ps, [jax.ShapeDtypeStruct(p.shape, p.dtype) for p in ps], 3 * n, start, wait)


def join_halves(fs):
    n = len(fs)

    def start(in_refs, out_refs, send, recv, base):
        x, y, c, _ = _place()
        for i in range(n):
            _rcopy(in_refs[i].at[c], out_refs[i].at[c], send, recv, base + i, (x, y, 1 - c)).start()

    def wait(in_refs, out_refs, send, recv, base):
        x, y, c, _ = _place()
        for i in range(n):
            _rcopy(in_refs[i].at[c], out_refs[i].at[1 - c], send, recv, base + i, (x, y, 1 - c)).wait()

    return Comm(fs, [jax.ShapeDtypeStruct(f.shape, f.dtype) for f in fs], n, start, wait, {i: i for i in range(n)})


def allreduce_small(v, *, name):
    m_per, n = v.shape

    def body(x_ref, sum_ref, all_ref, send_sems, recv_sems, local_sem):
        x, y, c, chips = _place()
        me, sibling = (x, y, c), (x, y, 1 - c)

        def rows(px, py, pc):
            return all_ref.at[pl.ds((4 * px + 2 * py + pc) * m_per, m_per), :]

        def copy(k, block, to, src=None):
            return pltpu.make_async_remote_copy(src_ref=rows(*block) if src is None else src, dst_ref=rows(*block),
                                                send_sem=send_sems.at[k], recv_sem=recv_sems.at[k], device_id=to,
                                                device_id_type=MESH)

        mine = pltpu.make_async_copy(x_ref, rows(*me), local_sem)
        mine.start()
        first = [copy(0, me, sibling, src=x_ref)]
        first += [copy(1 + j, me, (*chip, c), src=x_ref) for j, chip in enumerate(chips)]
        for cp in first:
            cp.start()
        passed = [copy(4 + j, (*chip, c), sibling) for j, chip in enumerate(chips)]
        for j, chip in enumerate(chips):
            copy(1 + j, (*chip, c), me).wait_recv()
            passed[j].start()
        copy(0, sibling, me).wait_recv()
        for j, chip in enumerate(chips):
            copy(4 + j, (*chip, 1 - c), me).wait_recv()
        for cp in first + passed:
            cp.wait_send()
        mine.wait()
        acc = all_ref[0:m_per, :]
        for d in range(1, N_DEV):
            acc = acc + all_ref[d * m_per:(d + 1) * m_per, :]
        sum_ref[...] = acc

    vm = pl.BlockSpec(memory_space=pltpu.VMEM)
    return pl.pallas_call(
        body, name=name, in_specs=[vm], out_specs=vm, out_shape=jax.ShapeDtypeStruct((m_per, n), F32),
        scratch_shapes=[pltpu.VMEM((N_DEV * m_per, n), F32), pltpu.SemaphoreType.DMA((7,)),
                        pltpu.SemaphoreType.DMA((7,)), pltpu.SemaphoreType.DMA],
    )(v)


def _cols_to_slots(w):
    r, c4 = w.shape
    return w.reshape(r, N_CHIPS, c4 // N_CHIPS).transpose(1, 0, 2)


def _slots_to_cols(w):
    n, r, c = w.shape
    return w.transpose(1, 0, 2).reshape(r, n * c)


def _pad_cols(a, width):
    return jnp.pad(a, ((0, 0), (0, width - a.shape[1])))


class InLayout:
    def __init__(self, q_rank, kv_rank):
        gk = GLA_HEADS * GLA_DK
        gv = GLA_HEADS * GLA_DV
        sizes = [q_rank, kv_rank, MLA_ROPE, gk, gk, gv, GLA_GATE_RANK, gv]
        names = ["zq", "zkv", "zkr", "gq", "gk", "gv", "zg", "zr"]
        starts = np.concatenate([[0], np.cumsum(sizes)[:-1]])
        self.ref = {n: (int(s), int(z)) for n, s, z in zip(names, starts, sizes)}
        self.ref_width = int(sum(sizes))
        self.order = ["gv", "zr", "zq", "gq", "gk", "zkv", "zkr", "zg"]
        self.off, self.size = {}, {}
        pos = 0
        for n in self.order:
            padded = -(-self.ref[n][1] // LANE) * LANE
            self.off[n], self.size[n] = pos, padded
            pos += padded
        self.width = pos
        self.shard = self.ref_width // N_CHIPS
        self.shard_pad = -(-self.shard // LANE) * LANE

    def _pieces(self, lo, hi):
        out = []
        while lo < hi:
            s = lo // self.shard
            end = min(hi, (s + 1) * self.shard)
            out.append((s * self.shard_pad + lo - s * self.shard, s * self.shard_pad + end - s * self.shard))
            lo = end
        return out

    def from_shards(self, zs):
        cols = []
        for n in self.order:
            start, size = self.ref[n]
            cols += [zs[:, a:b] for a, b in self._pieces(start, start + size)]
            if self.size[n] > size:
                cols.append(jnp.zeros((zs.shape[0], self.size[n] - size), zs.dtype))
        return jnp.concatenate(cols, axis=1)

    def to_shards(self, dz):
        names = sorted(self.ref, key=lambda n: self.ref[n][0])
        ref = jnp.concatenate([dz[:, self.off[n]:self.off[n] + self.ref[n][1]] for n in names], axis=1)
        ref = ref.reshape(dz.shape[0], N_CHIPS, self.shard)
        return jnp.pad(ref, ((0, 0), (0, 0), (0, self.shard_pad - self.shard))).reshape(dz.shape[0], -1)


def _pad_q_up(w):
    r = w.shape[0]
    w = w.reshape(r, MLA_HEADS, MLA_QK)
    w = jnp.pad(w, ((0, 0), (0, 0), (0, MLA_HEAD_PAD - MLA_QK)))
    return w.reshape(r, MLA_HEADS * MLA_HEAD_PAD)


def _unpad_q_up(g):
    r = g.shape[0]
    return g.reshape(r, MLA_HEADS, MLA_HEAD_PAD)[:, :, :MLA_QK].reshape(r, MLA_HEADS * MLA_QK)


def _rope_tables(positions):
    half = MLA_ROPE // 2
    inv_freq = ROPE_THETA ** (-jnp.arange(half, dtype=F32) / half)
    ang = positions.astype(F32).reshape(-1, 1) * inv_freq
    cos, sin = jnp.cos(ang), jnp.sin(ang)
    s = ang.shape[0]
    cosf = jnp.concatenate([cos, cos, jnp.ones((s, LANE - MLA_ROPE), F32)], axis=1)
    sinf = jnp.concatenate([sin, sin, jnp.zeros((s, LANE - MLA_ROPE), F32)], axis=1)
    rot = np.zeros((LANE, LANE), np.float32)
    for j in range(half):
        rot[j + half, j] = -1.0
        rot[j, j + half] = 1.0
    return cosf, sinf, jnp.asarray(rot)


SMALL = ["ffn1_norm", "mix_norm", "q_a_norm", "kv_a_norm", "mla_q_norm", "mla_k_norm", "gla_b_gate", "gla_out_norm",
         "mem_attn_norm", "mem_norm", "mem_q_norm", "mem_k_norm", "ffn2_norm"]
BIG = ["ffn1_w_gate", "ffn1_w_up", "ffn1_w_down", "w_in", "w_q_up", "w_kv_up", "w_out", "mem_w_q", "mem_w_k",
       "mem_w_v", "mem_w_o", "ffn2_w_gate", "ffn2_w_up", "ffn2_w_down"]
COL_SHARDED = {"ffn1_w_gate", "ffn1_w_up", "w_in", "w_q_up", "w_kv_up", "gla_w_gate2", "mem_w_o", "ffn2_w_gate", "ffn2_w_up"}
WEIGHTS = ["ffn1_norm", "ffn1_w_gate", "ffn1_w_up", "ffn1_w_down", "mix_norm", "w_in", "q_a_norm", "w_q_up", "kv_a_norm",
           "w_kv_up", "mla_q_norm", "mla_k_norm", "gla_w_gate2", "gla_b_gate", "gla_out_norm", "w_out", "mem_attn_norm",
           "mem_norm", "mem_w_q", "mem_w_k", "mem_w_v", "mem_w_o", "mem_q_norm", "mem_k_norm", "ffn2_norm", "ffn2_w_gate",
           "ffn2_w_up", "ffn2_w_down"]


def _pack_small(vals, rows=8):
    flat = jnp.concatenate([v.reshape(-1).astype(F32) for v in vals])
    n = flat.shape[0]
    per = -(-n // (rows * LANE)) * LANE
    return jnp.pad(flat, (0, rows * per - n)).reshape(rows, per)


def _unpack_small(packed, shapes):
    flat = packed.reshape(-1)
    out, pos = [], 0
    for s in shapes:
        n = int(np.prod(s))
        out.append(flat[pos:pos + n].reshape(s))
        pos += n
    return out


FFN1 = ["ffn1_w_gate", "ffn1_w_up", "ffn1_w_down"]
FFN2 = ["ffn2_w_gate", "ffn2_w_up", "ffn2_w_down"]
SLOT_WEIGHTS = {"ffn1_w_gate", "ffn1_w_up", "ffn2_w_gate", "ffn2_w_up", "w_in"}
MID_A = ["w_in", "w_q_up", "w_kv_up", "gla_w_gate2"]
MID_B = ["w_out", "mem_w_q", "mem_w_k", "mem_w_v", "mem_w_o"]


def _with(res, comm):
    return res if comm is not None else (res, None)


def kernel(x, mem, positions, ffn1_norm, ffn1_w_gate, ffn1_w_up, ffn1_w_down, mix_norm, w_in, q_a_norm, w_q_up, kv_a_norm, w_kv_up, mla_q_norm, mla_k_norm, gla_w_gate2, gla_b_gate, gla_out_norm, w_out, mem_attn_norm, mem_norm, mem_w_q, mem_w_k, mem_w_v, mem_w_o, mem_q_norm, mem_k_norm, ffn2_norm, ffn2_w_gate, ffn2_w_up, ffn2_w_down, loss_target, m_ffn1_norm, m_ffn1_w_gate, m_ffn1_w_up, m_ffn1_w_down, m_mix_norm, m_w_in, m_q_a_norm, m_w_q_up, m_kv_a_norm, m_w_kv_up, m_mla_q_norm, m_mla_k_norm, m_gla_w_gate2, m_gla_b_gate, m_gla_out_norm, m_w_out, m_mem_attn_norm, m_mem_norm, m_mem_w_q, m_mem_w_k, m_mem_w_v, m_mem_w_o, m_mem_q_norm, m_mem_k_norm, m_ffn2_norm, m_ffn2_w_gate, m_ffn2_w_up, m_ffn2_w_down, v_ffn1_norm, v_ffn1_w_gate, v_ffn1_w_up, v_ffn1_w_down, v_mix_norm, v_w_in, v_q_a_norm, v_w_q_up, v_kv_a_norm, v_w_kv_up, v_mla_q_norm, v_mla_k_norm, v_gla_w_gate2, v_gla_b_gate, v_gla_out_norm, v_w_out, v_mem_attn_norm, v_mem_norm, v_mem_w_q, v_mem_w_k, v_mem_w_v, v_mem_w_o, v_mem_q_norm, v_mem_k_norm, v_ffn2_norm, v_ffn2_w_gate, v_ffn2_w_up, v_ffn2_w_down):
    args = dict(locals())
    two_d = lambda a: a[0] if a.ndim == 3 else a
    W = {n: two_d(args[n]) for n in WEIGHTS}
    M1 = {n: two_d(args["m_" + n]) for n in WEIGHTS}
    V2 = {n: two_d(args["v_" + n]) for n in WEIGHTS}
    xs, mems, tgt = x[0], mem[0], loss_target[0]
    S, D = xs.shape
    chip = 2 * lax.axis_index("x") + lax.axis_index("y")

    q_rank, kv_rank = W["w_q_up"].shape[0], W["w_kv_up"].shape[0]
    lay = InLayout(q_rank, kv_rank)
    off = lay.off
    shard16 = {n: W[n].astype(BF16) for n in BIG + ["gla_w_gate2"]}
    shard16["w_in"] = _pad_cols(shard16["w_in"], lay.shard_pad)
    full = {}

    def stage1(names):
        return gather_stage1([shard16[n] for n in names], [n != "gla_w_gate2" for n in names])

    def stage2(names, slots):
        return gather_stage2(slots, [shard16[n] for n in names], [n != "gla_w_gate2" for n in names])

    def finish(names, slots):
        for n, s in zip(names, slots):
            s = s.reshape((N_CHIPS,) + shard16[n].shape)
            if n in SLOT_WEIGHTS:
                full[n] = s
            else:
                full[n] = _slots_to_cols(s) if n in COL_SHARDED else s.reshape(-1, s.shape[2])

    finish(FFN1, run_comm(stage2(FFN1, run_comm(stage1(FFN1), name="gather_ffn1")), name="pass_ffn1"))
    cosf, sinf, rot = _rope_tables(positions[0])
    tri = jnp.asarray(np.tril(np.ones((CHUNK, CHUNK), np.float32)))
    gqn = W["mla_q_norm"][:, :MLA_NOPE]
    gqr = _pad_cols(W["mla_q_norm"][:, MLA_NOPE:], LANE)
    gkn = W["mla_k_norm"][:, :MLA_NOPE]
    gkr = _pad_cols(W["mla_k_norm"][:, MLA_NOPE:], LANE)
    HP = MLA_HEAD_PAD
    mla_scale = MLA_QK ** -0.5
    mem_scale = MEM_HEAD_DIM ** -0.5
    mla_w = MLA_HEADS * MLA_V
    gla_w = GLA_HEADS * GLA_DV
    mem_w = MEM_HEADS * MEM_HEAD_DIM

    n1 = row_fwd(rms_fn, [V(xs)], [W["ffn1_norm"]], [(D, BF16)], [(0, 0, 0)], name="ffn1_norm")[0]
    (gate1, up1, act1), mid_a1 = ffn_up(n1, full["ffn1_w_gate"], full["ffn1_w_up"], name="ffn1_up", comm=stage1(MID_A))
    token = {"last": None}

    def begin(comm, name, after=None):
        started = start_comm(comm, name=name, after=token["last"] if after is None else after)
        token["last"] = started[-1]
        return comm, started

    mid_b = begin(stage1(MID_B), "gather_start_mid_b", after=act1)
    x1, got = mm([(act1, full["ffn1_w_down"])], "nn", F32, alpha=0.5, res=xs, name="ffn1_down",
                 comm=stage2(MID_A, mid_a1), behind=token["last"])
    ffn1_saved = (n1, gate1, up1, act1)
    finish(MID_A, got)
    ffn2_s1 = [begin(stage1([n]), f"gather_start_{n}", after=x1 if n == FFN2[0] else None) for n in FFN2]
    w_q_up_p = _pad_q_up(full["w_q_up"])
    w_gate2_p = jnp.pad(full["gla_w_gate2"], ((0, LANE - GLA_GATE_RANK), (0, 0)))
    h = row_fwd(rms_fn, [V(x1)], [W["mix_norm"]], [(D, BF16)], [(0, 0, 0)], name="mix_norm")[0]
    mid_b1 = wait_comm(*mid_b, h, name="gather_wait_mid_b")
    z_shards, got = mm([(h, full["w_in"])], "nn", F32, name="in_proj", b_slots=True, comm=stage2(MID_B, mid_b1),
                       behind=token["last"])
    z = lay.from_shards(z_shards)
    finish(MID_B, got)
    qa = row_fwd(rms_fn, [V(z, off["zq"], q_rank)], [W["q_a_norm"]], [(q_rank, BF16)], [(0, 0, 0)], name="q_a_norm")[0]
    kva = row_fwd(rms_fn, [V(z, off["zkv"], kv_rank)], [W["kv_a_norm"]], [(kv_rank, BF16)], [(0, 0, 0)], name="kv_a_norm")[0]
    qraw = mm([(qa, w_q_up_p)], "nn", F32, name="q_up")
    kvraw = mm([(kva, full["w_kv_up"])], "nn", F32, name="kv_up")
    tabs = [V(cosf, diff=False), V(sinf, diff=False)]
    q_rows = [V(qraw, 0, LANE, HP), V(qraw, LANE, LANE, HP)] + tabs
    k_rows = [V(kvraw, 0, LANE, HP), V(z, off["zkr"], LANE, 0)] + tabs
    qh = row_fwd(qk_prep_fn, q_rows, [gqn, gqr, rot], [(MLA_HEADS * HP, BF16)], [(0, 0, HP), (0, LANE, HP)],
                 heads=MLA_HEADS, name="q_prep")[0]
    kh = row_fwd(qk_prep_fn, k_rows, [gkn, gkr, rot], [(MLA_HEADS * HP, BF16)], [(0, 0, HP), (0, LANE, HP)],
                 heads=MLA_HEADS, name="k_prep")[0]
    mla_kw = dict(heads=MLA_HEADS, dk=HP, dv=MLA_V, v_off=1, v_hs=2, scale=mla_scale, causal=True)
    o_mla = attn_fwd(qh, kh, kvraw, name="mla_attn", **mla_kw)

    zg = z[:, off["zg"]:off["zg"] + LANE]
    pre = mm([(zg, w_gate2_p)], "nn", F32, name="gla_gate")
    la = row_fwd(gate_fn, [V(pre)], [W["gla_b_gate"]], [(pre.shape[1], F32)], [(0, 0, 0)], name="gla_log_decay")[0]
    gla_kw = dict(q_off=off["gq"], k_off=off["gk"], v_off=off["gv"])
    o_raw, states = gla_fwd(z, la, tri, name="gla_scan", **gla_kw)
    gla_rows = [V(o_raw, 0, GLA_DV, GLA_DV), V(z, off["zr"], GLA_DV, GLA_DV)]
    o_gla = row_fwd(gla_out_fn, gla_rows, [W["gla_out_norm"]], [(gla_w, BF16)], [(0, 0, GLA_DV)], heads=GLA_HEADS,
                    name="gla_out")[0]
    o_cat = jnp.concatenate([o_mla, o_gla], axis=1)
    f2 = [wait_comm(*ffn2_s1[k], o_cat, name=f"gather_wait_{FFN2[k]}")[0] for k in range(2)]
    x2, got = mm([(o_cat, full["w_out"])], "nn", F32, res=x1, name="out_proj", comm=stage2(FFN2[:2], f2))
    finish(FFN2[:2], got)

    hm = row_fwd(rms_fn, [V(x2)], [W["mem_attn_norm"]], [(D, BF16)], [(0, 0, 0)], name="mem_attn_norm")[0]
    mn = row_fwd(rms_fn, [V(mems)], [W["mem_norm"]], [(D, BF16)], [(0, 0, 0)], name="mem_norm")[0]
    qm_raw = mm([(hm, full["mem_w_q"])], "nn", F32, name="mem_q")
    km_raw = mm([(mn, full["mem_w_k"])], "nn", F32, name="mem_k")
    vm = mm([(mn, full["mem_w_v"])], "nn", F32, name="mem_v")
    hd = MEM_HEAD_DIM
    qm = row_fwd(rms_fn, [V(qm_raw, 0, hd, hd)], [W["mem_q_norm"]], [(mem_w, BF16)], [(0, 0, hd)], heads=MEM_HEADS,
                 name="mem_q_norm")[0]
    km = row_fwd(rms_fn, [V(km_raw, 0, hd, hd)], [W["mem_k_norm"]], [(mem_w, BF16)], [(0, 0, hd)], heads=MEM_HEADS,
                 name="mem_k_norm")[0]
    mem_kw = dict(heads=MEM_HEADS, dk=hd, dv=hd, v_off=0, v_hs=1, scale=mem_scale, causal=False)
    om = attn_fwd(qm, km, vm, name="mem_attn", **mem_kw)
    x3 = mm([(om, full["mem_w_o"])], "nn", F32, res=x2, name="mem_o")

    n2 = row_fwd(rms_fn, [V(x3)], [W["ffn2_norm"]], [(D, BF16)], [(0, 0, 0)], name="ffn2_norm")[0]
    f2_down = wait_comm(*ffn2_s1[2], n2, name=f"gather_wait_{FFN2[2]}")
    (gate2, up2, act2), got = ffn_up(n2, full["ffn2_w_gate"], full["ffn2_w_up"], name="ffn2_up",
                                     comm=stage2(FFN2[2:], f2_down))
    finish(FFN2[2:], got)
    y = mm([(act2, full["ffn2_w_down"])], "nn", F32, alpha=0.5, res=x3, name="ffn2_down")
    dy, loss_part = loss_head(y, tgt, name="loss_head")
    loss = lax.psum(loss_part[0, 0], ("x", "y", "c"))

    G, chip_sum, reduced = {}, {}, {}

    def to_halves(n):
        g = G[n]
        if n in SLOT_WEIGHTS:
            s = g
        else:
            s = _cols_to_slots(g) if n in COL_SHARDED else g.reshape(N_CHIPS, g.shape[0] // N_CHIPS, g.shape[1])
        return s.reshape(N_CHIPS, 2, s.shape[1] // 2, s.shape[2])

    def add2(names, halves, got):
        for n, a, b in zip(names, halves, got):
            chip_sum[n] = add_own_half(a, b, BF16, name=f"rs_add2_{n}")

    to_join = []

    def add4_join(names, parts):
        for n, p in zip(names, parts):
            to_join.append((n, sum_chip_parts(chip_sum[n], p, name=f"rs_add4_{n}")))

    def with_joins(comm):
        names, totals = [n for n, _ in to_join], [t for _, t in to_join]
        to_join.clear()
        if not names:
            return comm, lambda got: got
        own = 0 if comm is None else len(comm.out_shapes)
        joined = join_halves(totals)

        def split(got):
            for n, b in zip(names, got[own:]):
                reduced[n] = b.reshape(-1, b.shape[2])[:, :W[n].shape[1]]
            return got[:own]

        return (joined if comm is None else merge_comms(comm, joined)), split

    def flush_joins():
        comm, split = with_joins(None)
        if comm is not None:
            split(run_comm(comm, name=f"rs_join_{len(reduced)}"))

    in_flight = []

    def xchg_start(names):
        in_flight.append((names,) + begin(exchange_chips([chip_sum[n] for n in names]), f"xchg_start_{names[0]}"))

    def xchg_wait(after, count=1):
        for _ in range(count):
            names, comm, started = in_flight.pop(0)
            add4_join(names, wait_comm(comm, started, after, name=f"xchg_wait_{names[0]}"))

    def ffn_backward(dout, xin, tag, saved, dact_comm=None, after_dact=None):
        n_, gate, up, act = saved
        nd, ng, nu = f"{tag}_w_down", f"{tag}_w_gate", f"{tag}_w_up"
        (dgate, dup), got0 = _with(ffn_dact(dout, full[nd], gate, up, 0.5, name=f"{tag}_dact", comm=dact_comm,
                                            behind=token["last"]), dact_comm)
        if after_dact:
            after_dact(got0)
        G[nd] = mm([(act, dout)], "tn", F32, alpha=0.5, name=f"{tag}_dwd", tm=1408, tn=1024, behind=token["last"])
        hd_ = to_halves(nd)
        comm, split = with_joins(swap_halves([hd_]))
        G[ng], got = mm([(n_, dgate)], "tn", F32, name=f"{tag}_dwg", out_slots=True, tm=1024, tn=1408, rows_inner=True,
                        comm=comm)
        add2([nd], [hd_], split(got))
        xchg_start([nd])
        hg = to_halves(ng)
        G[nu], got_g = mm([(n_, dup)], "tn", F32, name=f"{tag}_dwu", out_slots=True, tm=1024, tn=1408, rows_inner=True,
                          comm=swap_halves([hg]), behind=token["last"])
        add2([ng], [hg], got_g)
        xchg_start([ng])
        hu = to_halves(nu)
        dn, got_u = mm([(dgate, full[ng]), (dup, full[nu])], "nt", F32, name=f"{tag}_dn", b_slots=True, tn=1024, tk=1408,
                       comm=swap_halves([hu]), behind=token["last"])
        add2([nu], [hu], got_u)
        xchg_start([nu])
        dx, G[f"{tag}_norm"] = row_bwd(rms_fn, [V(xin)], [W[f"{tag}_norm"]], [V(dn)], const_diff=[True], res=dout,
                                       name=f"{tag}_dnorm")
        return dx

    g3 = ffn_backward(dy, x3, "ffn2", (n2, gate2, up2, act2))
    xchg_wait(g3)

    d_om = mm([(g3, full["mem_w_o"])], "nt", F32, name="mem_o_dx", behind=token["last"])
    G["mem_w_o"] = mm([(om, g3)], "tn", F32, name="mem_o_dw")
    dqm, dkm, dvm = attn_bwd(qm, km, vm, d_om, name="mem_attn_bwd", **mem_kw)
    dqm_raw, G["mem_q_norm"] = row_bwd(rms_fn, [V(qm_raw, 0, hd, hd)], [W["mem_q_norm"]], [V(dqm, 0, hd, hd)],
                                       const_diff=[True], heads=MEM_HEADS, row_dtype=BF16, name="mem_q_norm_bwd")
    dkm_raw, G["mem_k_norm"] = row_bwd(rms_fn, [V(km_raw, 0, hd, hd)], [W["mem_k_norm"]], [V(dkm, 0, hd, hd)],
                                       const_diff=[True], heads=MEM_HEADS, row_dtype=BF16, name="mem_k_norm_bwd")
    dhm = mm([(dqm_raw, full["mem_w_q"])], "nt", F32, name="mem_q_dx")
    G["mem_w_q"] = mm([(hm, dqm_raw)], "tn", F32, name="mem_q_dw")
    dmn = mm([(dkm_raw, full["mem_w_k"]), (dvm, full["mem_w_v"])], "nt", F32, name="mem_kv_dx")
    G["mem_w_k"] = mm([(mn, dkm_raw)], "tn", F32, name="mem_k_dw")
    G["mem_w_v"] = mm([(mn, dvm)], "tn", F32, name="mem_v_dw")
    _, G["mem_norm"] = row_bwd(rms_fn, [V(mems)], [W["mem_norm"]], [V(dmn)], const_diff=[True], row_dtype=BF16,
                               name="mem_norm_bwd")
    g2, G["mem_attn_norm"] = row_bwd(rms_fn, [V(x2)], [W["mem_attn_norm"]], [V(dhm)], const_diff=[True], res=g3,
                                     name="mem_attn_norm_bwd")

    xchg_wait(g2, 2)

    d_ocat = mm([(g2, full["w_out"])], "nt", F32, name="out_proj_dx")
    G["w_out"] = mm([(o_cat, g2)], "tn", F32, name="out_proj_dw")

    d_oraw, d_zr, G["gla_out_norm"] = row_bwd(gla_out_fn, gla_rows, [W["gla_out_norm"]],
                                              [V(d_ocat, mla_w, GLA_DV, GLA_DV)], const_diff=[True], heads=GLA_HEADS,
                                              name="gla_out_bwd")
    mid_b_halves = [to_halves(n) for n in MID_B]
    comm, split = with_joins(swap_halves(mid_b_halves))
    (d_gq, d_gk, d_gv, d_la), got = gla_bwd(z, la, tri, tri.T, states, d_oraw, name="gla_scan_bwd", comm=comm, **gla_kw)
    add2(MID_B, mid_b_halves, split(got))
    xchg_start(MID_B)
    d_pre, G["gla_b_gate"] = row_bwd(gate_fn, [V(pre)], [W["gla_b_gate"]], [V(d_la)], const_diff=[True], row_dtype=BF16,
                                     name="gla_log_decay_bwd")
    d_zg = mm([(d_pre, w_gate2_p)], "nt", BF16, name="gla_gate_dx")
    G["gla_w_gate2"] = mm([(zg, d_pre)], "tn", F32, name="gla_gate_dw")[:GLA_GATE_RANK]

    comm, split = with_joins(None)
    (d_qh, d_kh, d_v), got = _with(attn_bwd(qh, kh, kvraw, d_ocat, name="mla_attn_bwd", comm=comm, **mla_kw), comm)
    split(got)
    cq = [V(d_qh, 0, LANE, HP), V(d_qh, LANE, LANE, HP)]
    ck = [V(d_kh, 0, LANE, HP), V(d_kh, LANE, LANE, HP)]
    d_qraw, d_gqn, d_gqr = row_bwd(qk_prep_fn, q_rows, [gqn, gqr, rot], cq, const_diff=[True, True, False],
                                   heads=MLA_HEADS, row_dtype=BF16, pack={0: (0, HP), 1: (LANE, HP)},
                                   pack_width=MLA_HEADS * HP, name="q_prep_bwd")
    d_kvraw, d_zkr, d_gkn, d_gkr = row_bwd(qk_prep_fn, k_rows, [gkn, gkr, rot], ck, const_diff=[True, True, False],
                                           heads=MLA_HEADS, row_dtype=BF16, pack={0: (0, HP)}, pack_width=MLA_HEADS * HP,
                                           fills=[(V(d_v, 0, MLA_V, MLA_V), LANE, HP)], name="k_prep_bwd")
    G["mla_q_norm"] = jnp.concatenate([d_gqn, d_gqr[:, :MLA_ROPE]], axis=1)
    G["mla_k_norm"] = jnp.concatenate([d_gkn, d_gkr[:, :MLA_ROPE]], axis=1)
    d_qa = mm([(d_qraw, w_q_up_p)], "nt", F32, name="q_up_dx")
    G["w_q_up"] = _unpad_q_up(mm([(qa, d_qraw)], "tn", F32, name="q_up_dw"))
    d_kva = mm([(d_kvraw, full["w_kv_up"])], "nt", F32, name="kv_up_dx")
    G["w_kv_up"] = mm([(kva, d_kvraw)], "tn", F32, name="kv_up_dw")
    d_zq, G["q_a_norm"] = row_bwd(rms_fn, [V(z, off["zq"], q_rank)], [W["q_a_norm"]], [V(d_qa)], const_diff=[True],
                                  row_dtype=BF16, name="q_a_norm_bwd")
    d_zkv, G["kv_a_norm"] = row_bwd(rms_fn, [V(z, off["zkv"], kv_rank)], [W["kv_a_norm"]], [V(d_kva)], const_diff=[True],
                                    row_dtype=BF16, name="kv_a_norm_bwd")

    seg = {"gv": d_gv, "zr": d_zr, "zq": d_zq, "gq": d_gq, "gk": d_gk, "zkv": d_zkv, "zkr": d_zkr, "zg": d_zg}
    dz = jnp.concatenate([_pad_cols(seg[n].astype(BF16), lay.size[n]) for n in lay.order], axis=1)
    xchg_wait(dz)
    comm, split = with_joins(None)
    dz_shards = lay.to_shards(dz)
    dh, got = _with(mm([(dz_shards, full["w_in"])], "nt", F32, name="in_proj_dx", b_slots=True, comm=comm), comm)
    split(got)
    G["w_in"] = mm([(h, dz_shards)], "tn", F32, name="in_proj_dw", out_slots=True)
    g1, G["mix_norm"] = row_bwd(rms_fn, [V(x1)], [W["mix_norm"]], [V(dh)], const_diff=[True], res=g2, name="mix_norm_bwd")

    mid_a = [n for n in MID_A if n != "gla_w_gate2"]
    mid_a_halves = [to_halves(n) for n in mid_a]

    def mid_a_sums(got):
        add2(mid_a, mid_a_halves, got)
        xchg_start(mid_a)

    gx = ffn_backward(g1, xs, "ffn1", ffn1_saved, dact_comm=swap_halves(mid_a_halves), after_dact=mid_a_sums)
    xchg_wait(gx, 2)

    grad, delta, new_m, new_v = {}, {}, {}, {}

    def adam_group(names, tag, behind=None):
        if any(n not in reduced for n in names):
            flush_joins()
        res = adamw([(W[n], reduced[n], M1[n], V2[n]) for n in names], name=f"adamw_{tag}", behind=behind)
        for n, (g_, d_, m_, v_) in zip(names, res):
            grad[n], delta[n], new_m[n], new_v[n] = g_, d_, m_, v_

    adam_group(FFN2, "ffn2", behind=token["last"])
    adam_group(mid_a + MID_B, "mid", behind=token["last"])
    xchg_wait(delta[MID_B[-1]], 2)
    adam_group(FFN1, "ffn1")

    small_names = SMALL + ["gla_w_gate2"]
    small_sum = allreduce_small(_pack_small([G[n] for n in small_names]), name="allreduce_small")
    small_g = dict(zip(small_names, _unpack_small(small_sum, [G[n].shape for n in small_names])))
    shard_c = W["gla_w_gate2"].shape[1]
    grad["gla_w_gate2"] = lax.dynamic_slice_in_dim(small_g["gla_w_gate2"], chip * shard_c, shard_c, axis=1)
    pw = _pack_small([W[n] for n in SMALL] + [W["gla_w_gate2"]])
    pg = _pack_small([small_g[n] for n in SMALL] + [grad["gla_w_gate2"]])
    pm = _pack_small([M1[n] for n in SMALL] + [M1["gla_w_gate2"]])
    pv = _pack_small([V2[n] for n in SMALL] + [V2["gla_w_gate2"]])
    (_, pd, pnm, pnv), = adamw([(pw, pg, pm, pv)], name="adamw_small")
    shapes = [W[n].shape for n in small_names]
    for n, d_, m_, v_ in zip(small_names, _unpack_small(pd, shapes), _unpack_small(pnm, shapes), _unpack_small(pnv, shapes)):
        delta[n], new_m[n], new_v[n] = d_, m_, v_
        if n != "gla_w_gate2":
            grad[n] = small_g[n]

    lead = lambda d: [d[n].reshape(args[n].shape) for n in WEIGHTS]
    return (loss, gx[None], *lead(grad), *lead(delta), *lead(new_m), *lead(new_v))
```

```python
import functools
import math

import numpy as np
import jax
import jax.numpy as jnp
from jax import lax
from jax.experimental import pallas as pl
from jax.experimental.pallas import tpu as pltpu

F32 = jnp.float32
BF16 = jnp.bfloat16
MXU_DTYPE = jnp.bfloat16
MESH = pl.DeviceIdType.MESH
ANY = pl.BlockSpec(memory_space=pl.ANY)

LANE = 128
EPS = 1e-6
CHUNK = 64
MLA_HEADS = 8
MLA_NOPE = 128
MLA_ROPE = 64
MLA_QK = MLA_NOPE + MLA_ROPE
MLA_V = 128
MLA_HEAD_PAD = 2 * LANE
ROPE_THETA = 10000.0
GLA_HEADS = 4
GLA_DK = 128
GLA_DV = 256
GLA_GATE_RANK = 16
GLA_TAU = 16.0
MEM_HEADS = 4
MEM_HEAD_DIM = 128
N_CHIPS = 4
N_DEV = 8

ADAM_LR = 0.001
ADAM_B1 = 0.9
ADAM_B2 = 0.999
ADAM_EPS = 1e-08
ADAM_WD = 0.01
ADAM_STEP = 10

VMEM_LIMIT = 56 * 1024 * 1024


def _cparams(sem=None):
    if sem is None:
        return pltpu.CompilerParams(vmem_limit_bytes=VMEM_LIMIT)
    return pltpu.CompilerParams(dimension_semantics=sem, vmem_limit_bytes=VMEM_LIMIT)


def _tile(dim, pref, unit=LANE):
    if dim <= pref:
        return dim
    t = (pref // unit) * unit
    while t > unit and dim % t:
        t -= unit
    assert dim % t == 0, (dim, pref, unit)
    return t


class Comm:
    def __init__(self, ins, out_shapes, nsem, start, wait, aliases=None):
        self.ins, self.out_shapes, self.nsem = list(ins), list(out_shapes), nsem
        self.start, self.wait, self.aliases = start, wait, dict(aliases or {})


def merge_comms(a, b):
    ai, ao = len(a.ins), len(a.out_shapes)

    def start(ins, outs, send, recv, base):
        a.start(ins[:ai], outs[:ao], send, recv, base)
        b.start(ins[ai:], outs[ao:], send, recv, base + a.nsem)

    def wait(ins, outs, send, recv, base):
        a.wait(ins[:ai], outs[:ao], send, recv, base)
        b.wait(ins[ai:], outs[ao:], send, recv, base + a.nsem)

    aliases = dict(a.aliases)
    aliases.update({ai + i: ao + o for i, o in b.aliases.items()})
    return Comm(a.ins + b.ins, a.out_shapes + b.out_shapes, a.nsem + b.nsem, start, wait, aliases)


def run_comm(comm, *, name):
    ni, no = len(comm.ins), len(comm.out_shapes)

    def body(*refs):
        ins, outs = refs[:ni], refs[ni:ni + no]
        send, recv = refs[ni + no:]
        comm.start(ins, outs, send, recv, 0)
        comm.wait(ins, outs, send, recv, 0)

    return pl.pallas_call(
        body, name=name, in_specs=[ANY] * ni, out_specs=[ANY] * no, out_shape=comm.out_shapes,
        input_output_aliases=comm.aliases,
        scratch_shapes=[pltpu.SemaphoreType.DMA((comm.nsem,)), pltpu.SemaphoreType.DMA((comm.nsem,))])(*comm.ins)


HBM = pl.BlockSpec(memory_space=pltpu.HBM)
SEM = pl.BlockSpec(memory_space=pltpu.SEMAPHORE)


def start_comm(comm, *, name, after=None):
    assert not comm.aliases
    ni, no = len(comm.ins), len(comm.out_shapes)
    tail = [] if after is None else [after]

    def body(*refs):
        srcs, lands = refs[:ni], refs[ni:ni + no]
        send, recv = refs[ni + no + len(tail)], refs[ni + no + len(tail) + 1]
        token = refs[-1]
        comm.start(srcs, lands, send, recv, 0)
        token[...] = jnp.zeros_like(token)

    through = [pltpu.HBM(a.shape, a.dtype) for a in comm.ins] + [pltpu.HBM(s.shape, s.dtype) for s in comm.out_shapes]
    ops = [pltpu.with_memory_space_constraint(a, pltpu.HBM) for a in comm.ins]
    ops += [pltpu.with_memory_space_constraint(lax.empty(s.shape, s.dtype), pltpu.HBM) for s in comm.out_shapes]
    ops += tail
    res = pl.pallas_call(
        body, name=name, in_specs=[HBM] * (ni + no) + [ANY] * len(tail),
        out_shape=[pltpu.SemaphoreType.DMA((comm.nsem,)), pltpu.SemaphoreType.DMA((comm.nsem,))] + through
        + [jax.ShapeDtypeStruct((8, LANE), F32)],
        out_specs=[SEM, SEM] + [HBM] * (ni + no) + [pl.BlockSpec(memory_space=pltpu.VMEM)],
        input_output_aliases={i: 2 + i for i in range(ni + no)},
        compiler_params=pltpu.CompilerParams(has_side_effects=pltpu.SideEffectType.DATAFLOW_SIDE_EFFECTING))(*ops)
    return res[0], res[1], list(res[2:2 + ni]), list(res[2 + ni:2 + ni + no]), res[-1]


def wait_comm(comm, started, after, *, name):
    send, recv, srcs, lands, _ = started
    ni, no = len(srcs), len(lands)

    def body(*refs):
        comm.wait(refs[:ni], refs[ni:ni + no], refs[ni + no], refs[ni + no + 1], 0)

    res = pl.pallas_call(
        body, name=name, in_specs=[HBM] * (ni + no) + [SEM, SEM, ANY],
        out_shape=[pltpu.HBM(a.shape, a.dtype) for a in srcs + lands], out_specs=[HBM] * (ni + no),
        input_output_aliases={i: i for i in range(ni + no)},
        compiler_params=pltpu.CompilerParams(has_side_effects=pltpu.SideEffectType.DATAFLOW_SIDE_EFFECTING),
    )(*srcs, *lands, send, recv, after)
    return list(res[ni:])


def _pcall(body, ops, *, name, grid, in_specs, out_specs, out_shape, sem, scratch_shapes=(), comm=None, behind=None):
    if behind is not None:
        n_real, inner = len(ops), body
        ops, in_specs = list(ops) + [behind], list(in_specs) + [ANY]

        def body(*refs):
            inner(*refs[:n_real], *refs[n_real + 1:])

    if comm is None:
        return pl.pallas_call(body, name=name, grid=grid, in_specs=in_specs, out_specs=out_specs, out_shape=out_shape,
                              scratch_shapes=list(scratch_shapes), compiler_params=_cparams(sem))(*ops)
    multi = isinstance(out_shape, (list, tuple))
    k_out_shape = list(out_shape) if multi else [out_shape]
    k_out_specs = list(out_specs) if multi else [out_specs]
    nki, nko, nks = len(ops), len(k_out_shape), len(scratch_shapes)
    nci, nco = len(comm.ins), len(comm.out_shapes)

    def wrapped(*refs):
        p = 0
        k_in = refs[p:p + nki]; p += nki
        c_in = refs[p:p + nci]; p += nci
        k_out = refs[p:p + nko]; p += nko
        c_out = refs[p:p + nco]; p += nco
        k_scr = refs[p:p + nks]; p += nks
        send, recv = refs[p:]
        first = pl.program_id(0) == 0
        last = pl.program_id(0) == grid[0] - 1
        for a in range(1, len(grid)):
            first = jnp.logical_and(first, pl.program_id(a) == 0)
            last = jnp.logical_and(last, pl.program_id(a) == grid[a] - 1)

        @pl.when(first)
        def _():
            comm.start(c_in, c_out, send, recv, 0)

        body(*k_in, *k_out, *k_scr)

        @pl.when(last)
        def _():
            comm.wait(c_in, c_out, send, recv, 0)

    res = pl.pallas_call(
        wrapped, name=name, grid=grid, in_specs=list(in_specs) + [ANY] * nci, out_specs=k_out_specs + [ANY] * nco,
        out_shape=k_out_shape + comm.out_shapes,
        input_output_aliases={nki + i: nko + o for i, o in comm.aliases.items()},
        scratch_shapes=list(scratch_shapes) + [pltpu.SemaphoreType.DMA((comm.nsem,)), pltpu.SemaphoreType.DMA((comm.nsem,))],
        compiler_params=_cparams(("arbitrary",) * len(grid)))(*ops, *comm.ins)
    k_res = list(res[:nko]) if multi else res[0]
    return k_res, list(res[nko:])


_DIMS = {"nn": (((1,), (0,)), ((), ())), "nt": (((1,), (1,)), ((), ())), "tn": (((0,), (0,)), ((), ()))}


def _blockspec(shape, index, rows_inner):
    return pl.BlockSpec(shape, (lambda j, i, k: index(i, j, k)) if rows_inner else index)


def mm(pairs, mode, out_dtype, *, name, alpha=1.0, res=None, tm=1024, tn=1024, tk=4096, b_slots=False, out_slots=False,
       rows_inner=False, comm=None, behind=None):
    a0, b0 = pairs[0]
    if b_slots:
        b_rows, b_cols = b0.shape[1], N_CHIPS * b0.shape[2]
    else:
        b_rows, b_cols = b0.shape
    (M, K) = a0.shape[::-1] if mode == "tn" else a0.shape
    N = b_rows if mode == "nt" else b_cols
    shard = (b_cols if b_slots else N) // N_CHIPS
    tm = _tile(M, tm)
    tn = _tile(shard if (out_slots or (b_slots and mode != "nt")) else N, tn)
    tk = _tile(shard if (b_slots and mode == "nt") else K, tk)
    nk = K // tk
    npairs = len(pairs)
    dims = _DIMS[mode]
    spec = functools.partial(_blockspec, rows_inner=rows_inner)
    if mode == "tn":
        a_spec = spec((tk, tm), lambda i, j, k: (k, i))
    else:
        a_spec = spec((tm, tk), lambda i, j, k: (i, k))
    per = shard // (tk if mode == "nt" else tn)
    if mode == "nt":
        b_spec = (spec((None, tn, tk), lambda i, j, k: (k // per, j, k % per)) if b_slots else
                  spec((tn, tk), lambda i, j, k: (j, k)))
    else:
        b_spec = (spec((None, tk, tn), lambda i, j, k: (j // per, k, j % per)) if b_slots else
                  spec((tk, tn), lambda i, j, k: (k, j)))
    if out_slots:
        assert res is None and mode != "nt"
        o_spec = spec((None, tm, tn), lambda i, j, k: (j // per, i, j % per))
        out_sds = jax.ShapeDtypeStruct((N_CHIPS, M, shard), out_dtype)
    else:
        o_spec = spec((tm, tn), lambda i, j, k: (i, j))
        out_sds = jax.ShapeDtypeStruct((M, N), out_dtype)
    has_res = res is not None

    def body(*refs):
        ab = refs[:2 * npairs]
        res_ref = refs[2 * npairs] if has_res else None
        o_ref = refs[2 * npairs + int(has_res)]

        def products():
            r = None
            for p in range(npairs):
                d = lax.dot_general(ab[2 * p][...].astype(MXU_DTYPE), ab[2 * p + 1][...].astype(MXU_DTYPE), dims,
                                    preferred_element_type=F32)
                r = d if r is None else r + d
            return r

        def finish(r):
            if alpha != 1.0:
                r = r * alpha
            if has_res:
                r = res_ref[...].astype(F32) + r
            o_ref[...] = r.astype(out_dtype)

        if nk == 1:
            finish(products())
            return
        acc = refs[-1]
        k = pl.program_id(2)

        @pl.when(k == 0)
        def _():
            acc[...] = jnp.zeros_like(acc)

        acc[...] += products()

        @pl.when(k == nk - 1)
        def _():
            finish(acc[...])

    ops, specs = [], []
    for a, b in pairs:
        ops += [a, b]
        specs += [a_spec, b_spec]
    if has_res:
        ops.append(res)
        specs.append(o_spec)
    blocks = (N // tn, M // tm) if rows_inner else (M // tm, N // tn)
    return _pcall(body, ops, name=name, grid=blocks + (nk,), in_specs=specs, out_specs=o_spec, out_shape=out_sds,
                  scratch_shapes=[pltpu.VMEM((tm, tn), F32)] if nk > 1 else [],
                  sem=("parallel", "parallel", "arbitrary"), comm=comm, behind=behind)


def _sigmoid(x):
    return 1.0 / (1.0 + jnp.exp(-x))


def ffn_up(n, wg, wu, *, name, tm=512, tn=1408, comm=None, behind=None):
    M, K = n.shape
    shard = wg.shape[2]
    N = N_CHIPS * shard
    tm, tn = _tile(M, tm), _tile(shard, tn)
    per = shard // tn
    w_spec = pl.BlockSpec((None, K, tn), lambda j, i: (j // per, 0, j % per))

    def body(n_ref, wg_ref, wu_ref, g_ref, u_ref, a_ref):
        nv = n_ref[...].astype(MXU_DTYPE)
        g = jnp.dot(nv, wg_ref[...].astype(MXU_DTYPE), preferred_element_type=F32)
        u = jnp.dot(nv, wu_ref[...].astype(MXU_DTYPE), preferred_element_type=F32)
        g_ref[...] = g.astype(g_ref.dtype)
        u_ref[...] = u.astype(u_ref.dtype)
        a_ref[...] = (g * _sigmoid(g) * u).astype(a_ref.dtype)

    o_spec = pl.BlockSpec((tm, tn), lambda j, i: (i, j))
    sds = jax.ShapeDtypeStruct((M, N), BF16)
    return _pcall(
        body, [n, wg, wu], name=name, grid=(N // tn, M // tm),
        in_specs=[pl.BlockSpec((tm, K), lambda j, i: (i, 0)), w_spec, w_spec],
        out_specs=[o_spec, o_spec, o_spec], out_shape=[sds, sds, sds], sem=("parallel", "parallel"), comm=comm,
        behind=behind)


def ffn_dact(dy, wd, gate, up, alpha, *, name, tm=512, tn=1408, comm=None, behind=None):
    M, K = dy.shape
    N = wd.shape[0]
    tm, tn = _tile(M, tm), _tile(N, tn)

    def body(dy_ref, wd_ref, g_ref, u_ref, dg_ref, du_ref):
        da = lax.dot_general(dy_ref[...].astype(MXU_DTYPE), wd_ref[...].astype(MXU_DTYPE), _DIMS["nt"],
                             preferred_element_type=F32) * alpha
        g = g_ref[...].astype(F32)
        u = u_ref[...].astype(F32)
        s = _sigmoid(g)
        du_ref[...] = (da * (g * s)).astype(du_ref.dtype)
        dg_ref[...] = (da * u * (s * (1.0 + g * (1.0 - s)))).astype(dg_ref.dtype)

    o_spec = pl.BlockSpec((tm, tn), lambda j, i: (i, j))
    sds = jax.ShapeDtypeStruct((M, N), BF16)
    return _pcall(
        body, [dy, wd, gate, up], name=name, grid=(N // tn, M // tm),
        in_specs=[pl.BlockSpec((tm, K), lambda j, i: (i, 0)), pl.BlockSpec((tn, K), lambda j, i: (j, 0)), o_spec, o_spec],
        out_specs=[o_spec, o_spec], out_shape=[sds, sds], sem=("parallel", "parallel"), comm=comm, behind=behind)


def _window(width, off, ext):
    ww = LANE
    while ww < width:
        if ww >= ext and off // ww == (off + ext - 1) // ww and width % ww == 0:
            break
        ww *= 2
    else:
        ww = width
    return ww, off // ww, off - (off // ww) * ww


class V:
    def __init__(self, arr, off=0, w=None, hs=0, diff=True):
        self.arr, self.off, self.hs, self.diff = arr, off, hs, diff
        self.w = arr.shape[1] - off if w is None else w

    def window(self, heads, tr):
        ww, blk, inner = _window(self.arr.shape[1], self.off, (heads - 1) * self.hs + self.w)
        return pl.BlockSpec((tr, ww), lambda i, blk=blk: (i, blk)), inner


def _const_spec(c):
    return pl.BlockSpec(c.shape, lambda i: (0, 0))


def row_fwd(fn, rows, consts, outs, out_map, *, heads=1, tr=256, name):
    S = rows[0].arr.shape[0]
    tr = _tile(S, tr, 8)
    wins = [v.window(heads, tr) for v in rows]
    nr, nc = len(rows), len(consts)

    def body(*refs):
        row_refs, const_refs, out_refs = refs[:nr], refs[nr:nr + nc], refs[nr + nc:]
        cv = [c[...].astype(F32) for c in const_refs]
        for h in range(heads):
            rv = []
            for v, (_, io), r in zip(rows, wins, row_refs):
                lo = io + h * v.hs
                rv.append(r[:, lo:lo + v.w].astype(F32))
            res = fn(*rv, *cv)
            for (ai, off, hs), o in zip(out_map, res):
                lo = off + h * hs
                out_refs[ai][:, lo:lo + o.shape[1]] = o.astype(out_refs[ai].dtype)

    return pl.pallas_call(
        body, name=name, grid=(S // tr,),
        in_specs=[w[0] for w in wins] + [_const_spec(c) for c in consts],
        out_specs=[pl.BlockSpec((tr, w), lambda i: (i, 0)) for w, _ in outs],
        out_shape=[jax.ShapeDtypeStruct((S, w), d) for w, d in outs],
        compiler_params=_cparams(("parallel",)))(*[v.arr for v in rows], *consts)


def row_bwd(fn, rows, consts, cots, *, const_diff, heads=1, tr=256, res=None, row_dtype=F32, pack=None, pack_width=0,
            fills=(), name):
    S = rows[0].arr.shape[0]
    tr = _tile(S, tr, 8)
    pack = dict(pack or {})
    nr, nc, nct, nf = len(rows), len(consts), len(cots), len(fills)
    wins = [v.window(heads, tr) for v in rows]
    cwins = [v.window(heads, tr) for v in cots]
    fwins = [v.window(heads, tr) for v, _, _ in fills]
    drows = [k for k, v in enumerate(rows) if v.diff]
    dconsts = [k for k in range(nc) if const_diff[k]]
    has_res = res is not None
    assert not (has_res and 0 in pack)
    widths = [pack_width] if pack else []
    place = []
    for n, k in enumerate(drows):
        if n in pack:
            place.append((0,) + tuple(pack[n]))
        else:
            place.append((len(widths), 0, rows[k].w))
            widths.append(rows[k].w * (heads if rows[k].hs else 1))

    def body(*refs):
        row_refs = refs[:nr]
        const_refs = refs[nr:nr + nc]
        cot_refs = refs[nr + nc:nr + nc + nct]
        p = nr + nc + nct
        fill_refs = refs[p:p + nf]
        p += nf
        res_ref = refs[p] if has_res else None
        p += int(has_res)
        grow_refs = refs[p:p + len(widths)]
        gconst_refs = refs[p + len(widths):]
        i = pl.program_id(0)
        cv = [c[...].astype(F32) for c in const_refs]
        shared = [None] * len(drows)
        gc_sum = [None] * len(dconsts)
        for h in range(heads):
            rv = []
            for v, (_, io), r in zip(rows, wins, row_refs):
                lo = io + h * v.hs
                rv.append(r[:, lo:lo + v.w].astype(F32))
            ct = []
            for v, (_, io), r in zip(cots, cwins, cot_refs):
                lo = io + h * v.hs
                ct.append(r[:, lo:lo + v.w].astype(F32))

            def closed(*d):
                rr, cc = list(rv), list(cv)
                for k, val in zip(drows, d[:len(drows)]):
                    rr[k] = val
                for k, val in zip(dconsts, d[len(drows):]):
                    cc[k] = val
                return tuple(fn(*rr, *cc))

            _, vjp = jax.vjp(closed, *[rv[k] for k in drows], *[cv[k] for k in dconsts])
            grads = vjp(tuple(ct))
            for n, k in enumerate(drows):
                g = grads[n]
                if rows[k].hs == 0 and heads > 1:
                    shared[n] = g if shared[n] is None else shared[n] + g
                else:
                    if n == 0 and has_res:
                        g = g + res_ref[:, h * rows[k].w:(h + 1) * rows[k].w].astype(F32)
                    out, off, hs = place[n]
                    grow_refs[out][:, off + h * hs:off + h * hs + rows[k].w] = g.astype(row_dtype)
            for (v, off, hs), (_, io), r in zip(fills, fwins, fill_refs):
                lo = io + h * v.hs
                grow_refs[0][:, off + h * hs:off + h * hs + v.w] = r[:, lo:lo + v.w].astype(row_dtype)
            for n in range(len(dconsts)):
                g = grads[len(drows) + n]
                gc_sum[n] = g if gc_sum[n] is None else gc_sum[n] + g
        for n, k in enumerate(drows):
            if shared[n] is not None:
                g = shared[n]
                if n == 0 and has_res:
                    g = g + res_ref[...].astype(F32)
                grow_refs[place[n][0]][...] = g.astype(row_dtype)

        @pl.when(i == 0)
        def _():
            for n in range(len(dconsts)):
                gconst_refs[n][...] = gc_sum[n]

        @pl.when(i > 0)
        def _():
            for n in range(len(dconsts)):
                gconst_refs[n][...] += gc_sum[n]

    in_specs = [w[0] for w in wins] + [_const_spec(c) for c in consts] + [w[0] for w in cwins] + [w[0] for w in fwins]
    ops = [v.arr for v in rows] + list(consts) + [v.arr for v in cots] + [v.arr for v, _, _ in fills]
    if has_res:
        in_specs.append(pl.BlockSpec((tr, widths[0]), lambda i: (i, 0)))
        ops.append(res)
    out_specs = [pl.BlockSpec((tr, w), lambda i: (i, 0)) for w in widths]
    out_shape = [jax.ShapeDtypeStruct((S, w), row_dtype) for w in widths]
    for k in dconsts:
        out_specs.append(_const_spec(consts[k]))
        out_shape.append(jax.ShapeDtypeStruct(consts[k].shape, F32))
    return pl.pallas_call(body, name=name, grid=(S // tr,), in_specs=in_specs, out_specs=out_specs,
                          out_shape=out_shape, compiler_params=_cparams(("arbitrary",)))(*ops)


def _rms(x, g, n=None):
    n = x.shape[-1] if n is None else n
    ms = jnp.sum(x * x, axis=-1, keepdims=True) * (1.0 / n)
    return x * lax.rsqrt(ms + EPS) * g


def rms_fn(x, g):
    return (_rms(x, g),)


def qk_prep_fn(nope, rope, cos, sin, gn, gr, rot):
    ms = (jnp.sum(nope * nope, axis=-1, keepdims=True) + jnp.sum(rope * rope, axis=-1, keepdims=True)) * (1.0 / MLA_QK)
    r = lax.rsqrt(ms + EPS)
    on = nope * r * gn
    orr = rope * r * gr
    turned = jnp.dot(orr, rot, precision=lax.Precision.HIGHEST, preferred_element_type=F32)
    return on, orr * cos + turned * sin


def gla_out_fn(o, zr, g):
    return (_rms(o, g) * (zr * _sigmoid(zr)),)


def gate_fn(pre, b):
    t = pre + b
    return ((jnp.minimum(t, 0.0) - jnp.log(1.0 + jnp.exp(-jnp.abs(t)))) * (1.0 / GLA_TAU),)


def _attn_probs(q_ref, k_ref, scale, q0, kext):
    s = lax.dot_general(q_ref[...].astype(MXU_DTYPE), k_ref[0:kext, :].astype(MXU_DTYPE), _DIMS["nt"],
                        preferred_element_type=F32) * scale
    if q0 is not None:
        qc = (q0 + lax.broadcasted_iota(jnp.int32, s.shape, 0)) // CHUNK
        kc = lax.broadcasted_iota(jnp.int32, s.shape, 1) // CHUNK
        s = jnp.where(kc <= qc, s, -1e30)
    m = jnp.max(s, axis=-1, keepdims=True)
    e = jnp.exp(s - m)
    return e / jnp.sum(e, axis=-1, keepdims=True)


def _per_query_block(one, causal, nq, tq, Sk):
    if not causal:
        one(None, Sk, None)
        return
    assert tq % CHUNK == 0
    for ib in range(nq):
        pl.when(pl.program_id(1) == ib)(functools.partial(one, ib * tq, min(Sk, (ib + 1) * tq), ib))


def attn_fwd(q, k, v, *, heads, dk, dv, v_off, v_hs, scale, causal, name, tq=256, comm=None):
    Sq, Sk = q.shape[0], k.shape[0]
    tq = _tile(Sq, tq, 8)

    def body(q_ref, k_ref, v_ref, o_ref):
        def one(q0, kext, ib):
            p = _attn_probs(q_ref, k_ref, scale, q0, kext)
            o_ref[...] = jnp.dot(p.astype(MXU_DTYPE), v_ref[0:kext, :].astype(MXU_DTYPE),
                                 preferred_element_type=F32).astype(o_ref.dtype)

        _per_query_block(one, causal, Sq // tq, tq, Sk)

    return _pcall(
        body, [q, k, v], name=name, grid=(heads, Sq // tq),
        in_specs=[pl.BlockSpec((tq, dk), lambda h, i: (i, h)), pl.BlockSpec((Sk, dk), lambda h, i: (0, h)),
                  pl.BlockSpec((Sk, dv), lambda h, i: (0, v_off + h * v_hs))],
        out_specs=pl.BlockSpec((tq, dv), lambda h, i: (i, h)),
        out_shape=jax.ShapeDtypeStruct((Sq, heads * dv), BF16), sem=("parallel", "parallel"), comm=comm)


def attn_bwd(q, k, v, do, *, heads, dk, dv, v_off, v_hs, scale, causal, name, tq=256, comm=None):
    Sq, Sk = q.shape[0], k.shape[0]
    tq = _tile(Sq, tq, 8)

    def body(q_ref, k_ref, v_ref, do_ref, dq_ref, dk_ref, dv_ref):
        @pl.when(pl.program_id(1) == 0)
        def _():
            dk_ref[...] = jnp.zeros_like(dk_ref)
            dv_ref[...] = jnp.zeros_like(dv_ref)

        def one(q0, kext, ib):
            p = _attn_probs(q_ref, k_ref, scale, q0, kext)
            dob = do_ref[...].astype(MXU_DTYPE)
            dp = lax.dot_general(dob, v_ref[0:kext, :].astype(MXU_DTYPE), _DIMS["nt"], preferred_element_type=F32)
            delta = jnp.sum(p * dp, axis=-1, keepdims=True)
            ds = (p * (dp - delta) * scale).astype(MXU_DTYPE)
            dq_ref[...] = jnp.dot(ds, k_ref[0:kext, :].astype(MXU_DTYPE), preferred_element_type=F32)
            dk_ref[0:kext, :] += lax.dot_general(ds, q_ref[...].astype(MXU_DTYPE), _DIMS["tn"],
                                                 preferred_element_type=F32)
            dv_ref[0:kext, :] += lax.dot_general(p.astype(MXU_DTYPE), dob, _DIMS["tn"], preferred_element_type=F32)

        _per_query_block(one, causal, Sq // tq, tq, Sk)

    return _pcall(
        body, [q, k, v, do], name=name, grid=(heads, Sq // tq),
        in_specs=[pl.BlockSpec((tq, dk), lambda h, i: (i, h)), pl.BlockSpec((Sk, dk), lambda h, i: (0, h)),
                  pl.BlockSpec((Sk, dv), lambda h, i: (0, v_off + h * v_hs)),
                  pl.BlockSpec((tq, dv), lambda h, i: (i, h))],
        out_specs=[pl.BlockSpec((tq, dk), lambda h, i: (i, h)), pl.BlockSpec((Sk, dk), lambda h, i: (0, h)),
                   pl.BlockSpec((Sk, dv), lambda h, i: (0, h))],
        out_shape=[jax.ShapeDtypeStruct((Sq, heads * dk), F32), jax.ShapeDtypeStruct((Sk, heads * dk), F32),
                   jax.ShapeDtypeStruct((Sk, heads * dv), F32)],
        sem=("parallel", "arbitrary"), comm=comm)


def _gla_chunk(k, g, tri_ref):
    b = jnp.dot(tri_ref[...], g, precision=lax.Precision.HIGHEST, preferred_element_type=F32)
    b_end = jnp.sum(g, axis=0, keepdims=True)
    e = jnp.exp(b_end - b)
    return k * e, e, jnp.exp(b_end)


def _gla_windows(z, q_off, k_off, v_off, rows_of):
    H, DK, DV = GLA_HEADS, GLA_DK, GLA_DV
    specs, inner = [], []
    for off, ext in ((q_off, H * DK), (k_off, H * DK), (v_off, H * DV)):
        ww, blk, io = _window(z.shape[1], off, ext)
        specs.append(pl.BlockSpec((CHUNK, ww), lambda c, blk=blk: (rows_of(c), blk)))
        inner.append(io)
    return specs, inner


def gla_fwd(z, la, tri, *, q_off, k_off, v_off, name, comm=None):
    S = z.shape[0]
    nchunk = S // CHUNK
    H, DK, DV = GLA_HEADS, GLA_DK, GLA_DV
    qscale = DK ** -0.5
    zspecs, (qi, ki, vi) = _gla_windows(z, q_off, k_off, v_off, lambda c: c)

    def body(q_ref, k_ref, v_ref, la_ref, tri_ref, o_ref, st_ref, state):
        @pl.when(pl.program_id(0) == 0)
        def _():
            state[...] = jnp.zeros_like(state)

        for h in range(H):
            dks, dvs = slice(h * DK, (h + 1) * DK), slice(h * DV, (h + 1) * DV)
            k = k_ref[:, ki + h * DK:ki + (h + 1) * DK].astype(F32)
            v = v_ref[:, vi + h * DV:vi + (h + 1) * DV]
            q = q_ref[:, qi + h * DK:qi + (h + 1) * DK].astype(F32)
            kdec, _, decay = _gla_chunk(k, la_ref[:, dks].astype(F32), tri_ref)
            ut = lax.dot_general(v.astype(MXU_DTYPE), kdec.astype(MXU_DTYPE), _DIMS["tn"], preferred_element_type=F32)
            new = state[h] * decay + ut
            state[h] = new
            st_ref[h] = new
            qs = (q * qscale).astype(MXU_DTYPE)
            o_ref[:, dvs] = lax.dot_general(qs, new.astype(MXU_DTYPE), _DIMS["nt"], preferred_element_type=F32)

    return _pcall(
        body, [z, z, z, la, tri], name=name, grid=(nchunk,),
        in_specs=zspecs + [pl.BlockSpec((CHUNK, H * DK), lambda c: (c, 0)), pl.BlockSpec((CHUNK, CHUNK), lambda c: (0, 0))],
        out_specs=[pl.BlockSpec((CHUNK, H * DV), lambda c: (c, 0)),
                   pl.BlockSpec((H, None, DV, DK), lambda c: (0, c, 0, 0))],
        out_shape=[jax.ShapeDtypeStruct((S, H * DV), F32), jax.ShapeDtypeStruct((H, nchunk, DV, DK), F32)],
        scratch_shapes=[pltpu.VMEM((H, DV, DK), F32)], sem=("arbitrary",), comm=comm)


def gla_bwd(z, la, tri, trit, states, do, *, q_off, k_off, v_off, name, comm=None):
    S = z.shape[0]
    nchunk = S // CHUNK
    H, DK, DV = GLA_HEADS, GLA_DK, GLA_DV
    qscale = DK ** -0.5
    last = nchunk - 1
    zspecs, (qi, ki, vi) = _gla_windows(z, q_off, k_off, v_off, lambda c: last - c)

    def body(q_ref, k_ref, v_ref, la_ref, tri_ref, trit_ref, st_ref, sp_ref, do_ref, dq_ref, dk_ref, dv_ref, dla_ref,
             dstate):
        c = pl.program_id(0)
        cc = last - c

        @pl.when(c == 0)
        def _():
            dstate[...] = jnp.zeros_like(dstate)

        for h in range(H):
            dks, dvs = slice(h * DK, (h + 1) * DK), slice(h * DV, (h + 1) * DV)
            kf = k_ref[:, ki + h * DK:ki + (h + 1) * DK].astype(F32)
            vb16 = v_ref[:, vi + h * DV:vi + (h + 1) * DV].astype(MXU_DTYPE)
            q = q_ref[:, qi + h * DK:qi + (h + 1) * DK].astype(F32)
            kdec, e, decay = _gla_chunk(kf, la_ref[:, dks].astype(F32), tri_ref)
            dob = do_ref[:, dvs].astype(MXU_DTYPE)
            stb = st_ref[h].astype(MXU_DTYPE)
            qs = (q * qscale).astype(MXU_DTYPE)
            dq_ref[:, dks] = jnp.dot(dob, stb, preferred_element_type=F32) * qscale
            dst = dstate[h] + lax.dot_general(dob, qs, _DIMS["tn"], preferred_element_type=F32)
            prev = jnp.where(cc > 0, sp_ref[h], 0.0)
            ddecay = jnp.sum(dst * prev, axis=0, keepdims=True)
            dstate[h] = dst * decay
            dub = dst.astype(MXU_DTYPE)
            dv_ref[:, dvs] = lax.dot_general(kdec.astype(MXU_DTYPE), dub, _DIMS["nt"], preferred_element_type=F32)
            dkdec = jnp.dot(vb16, dub, preferred_element_type=F32)
            dk_ref[:, dks] = dkdec * e
            w = dkdec * kf * e
            db_end = jnp.sum(w, axis=0, keepdims=True) + ddecay * decay
            dla_ref[:, dks] = db_end - jnp.dot(trit_ref[...], w, precision=lax.Precision.HIGHEST,
                                               preferred_element_type=F32)

    def rows(width):
        return pl.BlockSpec((CHUNK, width), lambda c: (last - c, 0))

    square = pl.BlockSpec((CHUNK, CHUNK), lambda c: (0, 0))
    return _pcall(
        body, [z, z, z, la, tri, trit, states, states, do], name=name, grid=(nchunk,),
        in_specs=zspecs + [rows(H * DK), square, square,
                           pl.BlockSpec((H, None, DV, DK), lambda c: (0, last - c, 0, 0)),
                           pl.BlockSpec((H, None, DV, DK), lambda c: (0, jnp.maximum(last - c - 1, 0), 0, 0)),
                           rows(H * DV)],
        out_specs=[rows(H * DK), rows(H * DK), rows(H * DV), rows(H * DK)],
        out_shape=[jax.ShapeDtypeStruct((S, H * DK), F32), jax.ShapeDtypeStruct((S, H * DK), F32),
                   jax.ShapeDtypeStruct((S, H * DV), F32), jax.ShapeDtypeStruct((S, H * DK), F32)],
        scratch_shapes=[pltpu.VMEM((H, DV, DK), F32)], sem=("arbitrary",), comm=comm)


def loss_head(y, target, *, name, tr=256):
    S, D = y.shape
    tr = _tile(S, tr, 8)

    def body(y_ref, t_ref, dy_ref, loss_ref):
        i = pl.program_id(0)
        err = y_ref[...] - t_ref[...]
        dy_ref[...] = err * (1.0 / D)
        part = jnp.zeros((1, LANE), F32) + 0.5 * jnp.sum(jnp.sum(err * err, axis=-1, keepdims=True) * (1.0 / D))

        @pl.when(i == 0)
        def _():
            loss_ref[...] = part

        @pl.when(i > 0)
        def _():
            loss_ref[...] += part

    spec = pl.BlockSpec((tr, D), lambda i: (i, 0))
    return pl.pallas_call(
        body, name=name, grid=(S // tr,), in_specs=[spec, spec],
        out_specs=[spec, pl.BlockSpec((1, LANE), lambda i: (0, 0))],
        out_shape=[jax.ShapeDtypeStruct((S, D), F32), jax.ShapeDtypeStruct((1, LANE), F32)],
        compiler_params=_cparams(("arbitrary",)))(y, target)


def _core_index():
    return lax.axis_index("c").astype(jnp.int32).reshape(1)


def _chip_slots():
    x, y, c = lax.axis_index("x"), lax.axis_index("y"), lax.axis_index("c")
    return jnp.stack([2 * x + y, 2 * (1 - x) + y, 2 * x + (1 - y), 2 * (1 - x) + (1 - y), c]).astype(jnp.int32)


def sum_chip_parts(own, parts, *, name, tr=256):
    _, R, C = own.shape
    tr = _tile(R, tr, 8)

    def body(idx_ref, o_ref, p0_ref, p1_ref, p2_ref, out_ref):
        acc = o_ref[...].astype(F32) + p0_ref[...].astype(F32)
        acc = acc + p1_ref[...].astype(F32)
        out_ref[...] = acc + p2_ref[...].astype(F32)

    def slot(k):
        return pl.BlockSpec((None, tr, C), lambda i, idx: (idx[k], i, 0))

    grid_spec = pltpu.PrefetchScalarGridSpec(num_scalar_prefetch=1, grid=(R // tr,),
                                             in_specs=[slot(0), slot(1), slot(2), slot(3)], out_specs=slot(4))
    return pl.pallas_call(body, name=name, grid_spec=grid_spec, out_shape=jax.ShapeDtypeStruct((2, R, C), F32),
                          compiler_params=_cparams(("parallel",)))(_chip_slots(), own, parts, parts, parts)


def add_own_half(g, got, out_dtype, *, name, tr=256):
    n, _, R, C = g.shape
    tr = _tile(R, tr, 8)

    def body(c_ref, a_ref, b_ref, o_ref):
        o_ref[...] = (a_ref[...].astype(F32) + b_ref[...].astype(F32)).astype(out_dtype)

    spec = pl.BlockSpec((None, tr, C), lambda s, i, c: (s, i, 0))
    grid_spec = pltpu.PrefetchScalarGridSpec(
        num_scalar_prefetch=1, grid=(n, R // tr),
        in_specs=[pl.BlockSpec((None, None, tr, C), lambda s, i, c: (s, c[0], i, 0)), spec], out_specs=spec)
    return pl.pallas_call(body, name=name, grid_spec=grid_spec, out_shape=jax.ShapeDtypeStruct((n, R, C), out_dtype),
                          compiler_params=_cparams(("parallel", "parallel")))(_core_index(), g, got)


def adamw(items, *, name, max_steps=16, behind=None):
    c1 = 1.0 / (1.0 - ADAM_B1 ** ADAM_STEP)
    c2 = 1.0 / (1.0 - ADAM_B2 ** ADAM_STEP)
    n = len(items)
    steps = max_steps
    while steps > 1 and any(it[0].shape[0] % (8 * steps) for it in items):
        steps //= 2
    tail = [] if behind is None else [behind]

    def body(*refs):
        for a in range(n):
            w_ref, g_ref, m_ref, v_ref = refs[4 * a:4 * a + 4]
            go_ref, d_ref, nm_ref, nv_ref = refs[4 * n + len(tail) + 4 * a:4 * n + len(tail) + 4 * a + 4]
            gv = g_ref[...]
            go_ref[...] = gv
            nm = ADAM_B1 * m_ref[...] + (1.0 - ADAM_B1) * gv
            nv = ADAM_B2 * v_ref[...] + (1.0 - ADAM_B2) * (gv * gv)
            nm_ref[...] = nm
            nv_ref[...] = nv
            d_ref[...] = -ADAM_LR * ((nm * c1) / (jnp.sqrt(nv * c2) + ADAM_EPS) + ADAM_WD * w_ref[...])

    ops, in_specs, out_specs, out_shape = [], [], [], []
    for w, g, m, v in items:
        R, C = w.shape
        spec = pl.BlockSpec((R // steps, C), lambda i: (i, 0))
        ops += [w, g, m, v]
        in_specs += [spec] * 4
        out_specs += [spec] * 4
        out_shape += [jax.ShapeDtypeStruct((R, C), F32)] * 4
    flat = _pcall(body, ops + tail, name=name, grid=(steps,), in_specs=in_specs + [ANY] * len(tail), out_specs=out_specs,
                  out_shape=out_shape, sem=("parallel",))
    return [tuple(flat[4 * a:4 * a + 4]) for a in range(n)]


def _place():
    x, y, c = lax.axis_index("x"), lax.axis_index("y"), lax.axis_index("c")
    chips = [(1 - x, y), (x, 1 - y), (1 - x, 1 - y)]
    return x, y, c, chips


def _rcopy(src, dst, send, recv, j, to):
    return pltpu.make_async_remote_copy(src_ref=src, dst_ref=dst, send_sem=send.at[j], recv_sem=recv.at[j], device_id=to,
                                        device_id_type=MESH)


def gather_stage1(shards, split):
    n = len(shards)
    ins = [s.reshape(2, s.shape[0] // 2, s.shape[1]) if sp else s for s, sp in zip(shards, split)]
    outs = [jax.ShapeDtypeStruct((N_CHIPS,) + a.shape, a.dtype) for a in ins]

    def start(in_refs, out_refs, send, recv, base):
        x, y, c, chips = _place()
        mine = 2 * x + y
        for i in range(n):
            src = in_refs[i].at[c] if split[i] else in_refs[i]
            dst = out_refs[i].at[mine, c] if split[i] else out_refs[i].at[mine]
            for k, (px, py) in enumerate(chips):
                _rcopy(src, dst, send, recv, base + 3 * i + k, (px, py, c)).start()

    def wait(in_refs, out_refs, send, recv, base):
        x, y, c, chips = _place()
        for i in range(n):
            src = in_refs[i].at[c] if split[i] else in_refs[i]
            for k, (px, py) in enumerate(chips):
                dst = out_refs[i].at[2 * px + py, c] if split[i] else out_refs[i].at[2 * px + py]
                _rcopy(src, dst, send, recv, base + 3 * i + k, (px, py, c)).wait()

    return Comm(ins, outs, 3 * n, start, wait)


def gather_stage2(slots, shards, split):
    n = len(slots)
    own = [s.reshape(2, s.shape[0] // 2, s.shape[1]) if sp else s for s, sp in zip(shards, split)]

    def copies(in_refs, out_refs, send, recv, base):
        x, y, c, chips = _place()
        sib = (x, y, 1 - c)
        for i in range(n):
            j = base + 4 * i
            mine = out_refs[i].at[2 * x + y]
            yield _rcopy(in_refs[n + i], mine, send, recv, j + 3, sib), _rcopy(in_refs[n + i], mine, send, recv, j + 3, sib)
            if split[i]:
                for k, (px, py) in enumerate(chips):
                    s = 2 * px + py
                    yield (_rcopy(in_refs[i].at[s, c], out_refs[i].at[s, c], send, recv, j + k, sib),
                           _rcopy(in_refs[i].at[s, c], out_refs[i].at[s, 1 - c], send, recv, j + k, sib))

    def start(*a):
        for out, _ in copies(*a):
            out.start()

    def wait(*a):
        for _, back in copies(*a):
            back.wait()

    return Comm(list(slots) + own, [jax.ShapeDtypeStruct(s.shape, s.dtype) for s in slots], 4 * n, start, wait,
                {i: i for i in range(n)})


def swap_halves(gs):
    n = len(gs)

    def copies(in_refs, out_refs, send, recv, base):
        x, y, c, _ = _place()
        return [_rcopy(in_refs[i].at[s, 1 - c], out_refs[i].at[s], send, recv, base + N_CHIPS * i + s, (x, y, 1 - c))
                for i in range(n) for s in range(N_CHIPS)]

    def start(*a):
        for cp in copies(*a):
            cp.start()

    def wait(*a):
        for cp in copies(*a):
            cp.wait()

    return Comm(gs, [jax.ShapeDtypeStruct((N_CHIPS,) + g.shape[2:], g.dtype) for g in gs], N_CHIPS * n, start, wait)


def exchange_chips(ps):
    n = len(ps)

    def start(in_refs, out_refs, send, recv, base):
        x, y, c, chips = _place()
        for i in range(n):
            for k, (px, py) in enumerate(chips):
                _rcopy(in_refs[i].at[2 * px + py], out_refs[i].at[2 * x + y], send, recv, base + 3 * i + k,
                       (px, py, c)).start()

    def wait(in_refs, out_refs, send, recv, base):
        x, y, c, chips = _place()
        for i in range(n):
            for k, (px, py) in enumerate(chips):
                _rcopy(in_refs[i].at[2 * px + py], out_refs[i].at[2 * px + py], send, recv, base + 3 * i + k,
                       (px, py, c)).wait()

    return Comm(ps, [jax.ShapeDtypeStruct(p.shape, p.dtype) for p in ps], 3 * n, start, wait)


def join_halves(fs):
    n = len(fs)

    def start(in_refs, out_refs, send, recv, base):
        x, y, c, _ = _place()
        for i in range(n):
            _rcopy(in_refs[i].at[c], out_refs[i].at[c], send, recv, base + i, (x, y, 1 - c)).start()

    def wait(in_refs, out_refs, send, recv, base):
        x, y, c, _ = _place()
        for i in range(n):
            _rcopy(in_refs[i].at[c], out_refs[i].at[1 - c], send, recv, base + i, (x, y, 1 - c)).wait()

    return Comm(fs, [jax.ShapeDtypeStruct(f.shape, f.dtype) for f in fs], n, start, wait, {i: i for i in range(n)})


def allreduce_small(v, *, name):
    m_per, n = v.shape

    def body(x_ref, sum_ref, all_ref, send_sems, recv_sems, local_sem):
        x, y, c, chips = _place()
        me, sibling = (x, y, c), (x, y, 1 - c)

        def rows(px, py, pc):
            return all_ref.at[pl.ds((4 * px + 2 * py + pc) * m_per, m_per), :]

        def copy(k, block, to, src=None):
            return pltpu.make_async_remote_copy(src_ref=rows(*block) if src is None else src, dst_ref=rows(*block),
                                                send_sem=send_sems.at[k], recv_sem=recv_sems.at[k], device_id=to,
                                                device_id_type=MESH)

        mine = pltpu.make_async_copy(x_ref, rows(*me), local_sem)
        mine.start()
        first = [copy(0, me, sibling, src=x_ref)]
        first += [copy(1 + j, me, (*chip, c), src=x_ref) for j, chip in enumerate(chips)]
        for cp in first:
            cp.start()
        passed = [copy(4 + j, (*chip, c), sibling) for j, chip in enumerate(chips)]
        for j, chip in enumerate(chips):
            copy(1 + j, (*chip, c), me).wait_recv()
            passed[j].start()
        copy(0, sibling, me).wait_recv()
        for j, chip in enumerate(chips):
            copy(4 + j, (*chip, 1 - c), me).wait_recv()
        for cp in first + passed:
            cp.wait_send()
        mine.wait()
        acc = all_ref[0:m_per, :]
        for d in range(1, N_DEV):
            acc = acc + all_ref[d * m_per:(d + 1) * m_per, :]
        sum_ref[...] = acc

    vm = pl.BlockSpec(memory_space=pltpu.VMEM)
    return pl.pallas_call(
        body, name=name, in_specs=[vm], out_specs=vm, out_shape=jax.ShapeDtypeStruct((m_per, n), F32),
        scratch_shapes=[pltpu.VMEM((N_DEV * m_per, n), F32), pltpu.SemaphoreType.DMA((7,)),
                        pltpu.SemaphoreType.DMA((7,)), pltpu.SemaphoreType.DMA],
    )(v)


def _cols_to_slots(w):
    r, c4 = w.shape
    return w.reshape(r, N_CHIPS, c4 // N_CHIPS).transpose(1, 0, 2)


def _slots_to_cols(w):
    n, r, c = w.shape
    return w.transpose(1, 0, 2).reshape(r, n * c)


def _pad_cols(a, width):
    return jnp.pad(a, ((0, 0), (0, width - a.shape[1])))


class InLayout:
    def __init__(self, q_rank, kv_rank):
        gk = GLA_HEADS * GLA_DK
        gv = GLA_HEADS * GLA_DV
        sizes = [q_rank, kv_rank, MLA_ROPE, gk, gk, gv, GLA_GATE_RANK, gv]
        names = ["zq", "zkv", "zkr", "gq", "gk", "gv", "zg", "zr"]
        starts = np.concatenate([[0], np.cumsum(sizes)[:-1]])
        self.ref = {n: (int(s), int(z)) for n, s, z in zip(names, starts, sizes)}
        self.ref_width = int(sum(sizes))
        self.order = ["gv", "zr", "zq", "gq", "gk", "zkv", "zkr", "zg"]
        self.off, self.size = {}, {}
        pos = 0
        for n in self.order:
            padded = -(-self.ref[n][1] // LANE) * LANE
            self.off[n], self.size[n] = pos, padded
            pos += padded
        self.width = pos
        self.shard = self.ref_width // N_CHIPS
        self.shard_pad = -(-self.shard // LANE) * LANE

    def _pieces(self, lo, hi):
        out = []
        while lo < hi:
            s = lo // self.shard
            end = min(hi, (s + 1) * self.shard)
            out.append((s * self.shard_pad + lo - s * self.shard, s * self.shard_pad + end - s * self.shard))
            lo = end
        return out

    def from_shards(self, zs):
        cols = []
        for n in self.order:
            start, size = self.ref[n]
            cols += [zs[:, a:b] for a, b in self._pieces(start, start + size)]
            if self.size[n] > size:
                cols.append(jnp.zeros((zs.shape[0], self.size[n] - size), zs.dtype))
        return jnp.concatenate(cols, axis=1)

    def to_shards(self, dz):
        names = sorted(self.ref, key=lambda n: self.ref[n][0])
        ref = jnp.concatenate([dz[:, self.off[n]:self.off[n] + self.ref[n][1]] for n in names], axis=1)
        ref = ref.reshape(dz.shape[0], N_CHIPS, self.shard)
        return jnp.pad(ref, ((0, 0), (0, 0), (0, self.shard_pad - self.shard))).reshape(dz.shape[0], -1)


def _pad_q_up(w):
    r = w.shape[0]
    w = w.reshape(r, MLA_HEADS, MLA_QK)
    w = jnp.pad(w, ((0, 0), (0, 0), (0, MLA_HEAD_PAD - MLA_QK)))
    return w.reshape(r, MLA_HEADS * MLA_HEAD_PAD)


def _unpad_q_up(g):
    r = g.shape[0]
    return g.reshape(r, MLA_HEADS, MLA_HEAD_PAD)[:, :, :MLA_QK].reshape(r, MLA_HEADS * MLA_QK)


def _rope_tables(positions):
    half = MLA_ROPE // 2
    inv_freq = ROPE_THETA ** (-jnp.arange(half, dtype=F32) / half)
    ang = positions.astype(F32).reshape(-1, 1) * inv_freq
    cos, sin = jnp.cos(ang), jnp.sin(ang)
    s = ang.shape[0]
    cosf = jnp.concatenate([cos, cos, jnp.ones((s, LANE - MLA_ROPE), F32)], axis=1)
    sinf = jnp.concatenate([sin, sin, jnp.zeros((s, LANE - MLA_ROPE), F32)], axis=1)
    rot = np.zeros((LANE, LANE), np.float32)
    for j in range(half):
        rot[j + half, j] = -1.0
        rot[j, j + half] = 1.0
    return cosf, sinf, jnp.asarray(rot)


SMALL = ["ffn1_norm", "mix_norm", "q_a_norm", "kv_a_norm", "mla_q_norm", "mla_k_norm", "gla_b_gate", "gla_out_norm",
         "mem_attn_norm", "mem_norm", "mem_q_norm", "mem_k_norm", "ffn2_norm"]
BIG = ["ffn1_w_gate", "ffn1_w_up", "ffn1_w_down", "w_in", "w_q_up", "w_kv_up", "w_out", "mem_w_q", "mem_w_k",
       "mem_w_v", "mem_w_o", "ffn2_w_gate", "ffn2_w_up", "ffn2_w_down"]
COL_SHARDED = {"ffn1_w_gate", "ffn1_w_up", "w_in", "w_q_up", "w_kv_up", "gla_w_gate2", "mem_w_o", "ffn2_w_gate", "ffn2_w_up"}
WEIGHTS = ["ffn1_norm", "ffn1_w_gate", "ffn1_w_up", "ffn1_w_down", "mix_norm", "w_in", "q_a_norm", "w_q_up", "kv_a_norm",
           "w_kv_up", "mla_q_norm", "mla_k_norm", "gla_w_gate2", "gla_b_gate", "gla_out_norm", "w_out", "mem_attn_norm",
           "mem_norm", "mem_w_q", "mem_w_k", "mem_w_v", "mem_w_o", "mem_q_norm", "mem_k_norm", "ffn2_norm", "ffn2_w_gate",
           "ffn2_w_up", "ffn2_w_down"]


def _pack_small(vals, rows=8):
    flat = jnp.concatenate([v.reshape(-1).astype(F32) for v in vals])
    n = flat.shape[0]
    per = -(-n // (rows * LANE)) * LANE
    return jnp.pad(flat, (0, rows * per - n)).reshape(rows, per)


def _unpack_small(packed, shapes):
    flat = packed.reshape(-1)
    out, pos = [], 0
    for s in shapes:
        n = int(np.prod(s))
        out.append(flat[pos:pos + n].reshape(s))
        pos += n
    return out


FFN1 = ["ffn1_w_gate", "ffn1_w_up", "ffn1_w_down"]
FFN2 = ["ffn2_w_gate", "ffn2_w_up", "ffn2_w_down"]
SLOT_WEIGHTS = {"ffn1_w_gate", "ffn1_w_up", "ffn2_w_gate", "ffn2_w_up", "w_in"}
MID_A = ["w_in", "w_q_up", "w_kv_up", "gla_w_gate2"]
MID_B = ["w_out", "mem_w_q", "mem_w_k", "mem_w_v", "mem_w_o"]


def _with(res, comm):
    return res if comm is not None else (res, None)


def kernel(x, mem, positions, ffn1_norm, ffn1_w_gate, ffn1_w_up, ffn1_w_down, mix_norm, w_in, q_a_norm, w_q_up, kv_a_norm, w_kv_up, mla_q_norm, mla_k_norm, gla_w_gate2, gla_b_gate, gla_out_norm, w_out, mem_attn_norm, mem_norm, mem_w_q, mem_w_k, mem_w_v, mem_w_o, mem_q_norm, mem_k_norm, ffn2_norm, ffn2_w_gate, ffn2_w_up, ffn2_w_down, loss_target, m_ffn1_norm, m_ffn1_w_gate, m_ffn1_w_up, m_ffn1_w_down, m_mix_norm, m_w_in, m_q_a_norm, m_w_q_up, m_kv_a_norm, m_w_kv_up, m_mla_q_norm, m_mla_k_norm, m_gla_w_gate2, m_gla_b_gate, m_gla_out_norm, m_w_out, m_mem_attn_norm, m_mem_norm, m_mem_w_q, m_mem_w_k, m_mem_w_v, m_mem_w_o, m_mem_q_norm, m_mem_k_norm, m_ffn2_norm, m_ffn2_w_gate, m_ffn2_w_up, m_ffn2_w_down, v_ffn1_norm, v_ffn1_w_gate, v_ffn1_w_up, v_ffn1_w_down, v_mix_norm, v_w_in, v_q_a_norm, v_w_q_up, v_kv_a_norm, v_w_kv_up, v_mla_q_norm, v_mla_k_norm, v_gla_w_gate2, v_gla_b_gate, v_gla_out_norm, v_w_out, v_mem_attn_norm, v_mem_norm, v_mem_w_q, v_mem_w_k, v_mem_w_v, v_mem_w_o, v_mem_q_norm, v_mem_k_norm, v_ffn2_norm, v_ffn2_w_gate, v_ffn2_w_up, v_ffn2_w_down):
    args = dict(locals())
    two_d = lambda a: a[0] if a.ndim == 3 else a
    W = {n: two_d(args[n]) for n in WEIGHTS}
    M1 = {n: two_d(args["m_" + n]) for n in WEIGHTS}
    V2 = {n: two_d(args["v_" + n]) for n in WEIGHTS}
    xs, mems, tgt = x[0], mem[0], loss_target[0]
    S, D = xs.shape
    chip = 2 * lax.axis_index("x") + lax.axis_index("y")

    q_rank, kv_rank = W["w_q_up"].shape[0], W["w_kv_up"].shape[0]
    lay = InLayout(q_rank, kv_rank)
    off = lay.off
    shard16 = {n: W[n].astype(BF16) for n in BIG + ["gla_w_gate2"]}
    shard16["w_in"] = _pad_cols(shard16["w_in"], lay.shard_pad)
    full = {}

    def stage1(names):
        return gather_stage1([shard16[n] for n in names], [n != "gla_w_gate2" for n in names])

    def stage2(names, slots):
        return gather_stage2(slots, [shard16[n] for n in names], [n != "gla_w_gate2" for n in names])

    def finish(names, slots):
        for n, s in zip(names, slots):
            s = s.reshape((N_CHIPS,) + shard16[n].shape)
            if n in SLOT_WEIGHTS:
                full[n] = s
            else:
                full[n] = _slots_to_cols(s) if n in COL_SHARDED else s.reshape(-1, s.shape[2])

    token = {"last": None}

    def begin(comm, name, after=None):
        started = start_comm(comm, name=name, after=token["last"] if after is None else after)
        token["last"] = started[-1]
        return comm, started

    up_names, down_names = FFN1[:2], FFN1[2:]
    first = begin(stage1(up_names), "gather_start_ffn1_up")
    first_s1 = wait_comm(*first, token["last"], name="gather_wait_ffn1_up")
    down1 = begin(stage1(down_names), "gather_start_ffn1_down", after=first_s1[0])
    mid_a_s1 = begin(stage1(MID_A), "gather_start_mid_a")
    finish(up_names, run_comm(stage2(up_names, first_s1), name="pass_ffn1_up"))
    cosf, sinf, rot = _rope_tables(positions[0])
    tri = jnp.asarray(np.tril(np.ones((CHUNK, CHUNK), np.float32)))
    gqn = W["mla_q_norm"][:, :MLA_NOPE]
    gqr = _pad_cols(W["mla_q_norm"][:, MLA_NOPE:], LANE)
    gkn = W["mla_k_norm"][:, :MLA_NOPE]
    gkr = _pad_cols(W["mla_k_norm"][:, MLA_NOPE:], LANE)
    HP = MLA_HEAD_PAD
    mla_scale = MLA_QK ** -0.5
    mem_scale = MEM_HEAD_DIM ** -0.5
    mla_w = MLA_HEADS * MLA_V
    gla_w = GLA_HEADS * GLA_DV
    mem_w = MEM_HEADS * MEM_HEAD_DIM

    n1 = row_fwd(rms_fn, [V(xs)], [W["ffn1_norm"]], [(D, BF16)], [(0, 0, 0)], name="ffn1_norm")[0]
    gate1, up1, act1 = ffn_up(n1, full["ffn1_w_gate"], full["ffn1_w_up"], name="ffn1_up", behind=token["last"])
    finish(down_names, run_comm(stage2(down_names, wait_comm(*down1, act1, name="gather_wait_ffn1_down")),
                                name="pass_ffn1_down"))
    mid_a1 = wait_comm(*mid_a_s1, act1, name="gather_wait_mid_a")
    mid_b = begin(stage1(MID_B), "gather_start_mid_b", after=mid_a1[0])
    x1, got = mm([(act1, full["ffn1_w_down"])], "nn", F32, alpha=0.5, res=xs, name="ffn1_down",
                 comm=stage2(MID_A, mid_a1), behind=token["last"])
    ffn1_saved = (n1, gate1, up1, act1)
    finish(MID_A, got)
    ffn2_s1 = [begin(stage1([n]), f"gather_start_{n}", after=x1 if n == FFN2[0] else None) for n in FFN2]
    w_q_up_p = _pad_q_up(full["w_q_up"])
    w_gate2_p = jnp.pad(full["gla_w_gate2"], ((0, LANE - GLA_GATE_RANK), (0, 0)))
    h = row_fwd(rms_fn, [V(x1)], [W["mix_norm"]], [(D, BF16)], [(0, 0, 0)], name="mix_norm")[0]
    mid_b1 = wait_comm(*mid_b, h, name="gather_wait_mid_b")
    z_shards, got = mm([(h, full["w_in"])], "nn", F32, name="in_proj", b_slots=True, comm=stage2(MID_B, mid_b1),
                       behind=token["last"])
    z = lay.from_shards(z_shards)
    finish(MID_B, got)
    qa = row_fwd(rms_fn, [V(z, off["zq"], q_rank)], [W["q_a_norm"]], [(q_rank, BF16)], [(0, 0, 0)], name="q_a_norm")[0]
    kva = row_fwd(rms_fn, [V(z, off["zkv"], kv_rank)], [W["kv_a_norm"]], [(kv_rank, BF16)], [(0, 0, 0)], name="kv_a_norm")[0]
    qraw = mm([(qa, w_q_up_p)], "nn", F32, name="q_up")
    kvraw = mm([(kva, full["w_kv_up"])], "nn", F32, name="kv_up")
    tabs = [V(cosf, diff=False), V(sinf, diff=False)]
    q_rows = [V(qraw, 0, LANE, HP), V(qraw, LANE, LANE, HP)] + tabs
    k_rows = [V(kvraw, 0, LANE, HP), V(z, off["zkr"], LANE, 0)] + tabs
    qh = row_fwd(qk_prep_fn, q_rows, [gqn, gqr, rot], [(MLA_HEADS * HP, BF16)], [(0, 0, HP), (0, LANE, HP)],
                 heads=MLA_HEADS, name="q_prep")[0]
    kh = row_fwd(qk_prep_fn, k_rows, [gkn, gkr, rot], [(MLA_HEADS * HP, BF16)], [(0, 0, HP), (0, LANE, HP)],
                 heads=MLA_HEADS, name="k_prep")[0]
    mla_kw = dict(heads=MLA_HEADS, dk=HP, dv=MLA_V, v_off=1, v_hs=2, scale=mla_scale, causal=True)
    o_mla = attn_fwd(qh, kh, kvraw, name="mla_attn", **mla_kw)

    zg = z[:, off["zg"]:off["zg"] + LANE]
    pre = mm([(zg, w_gate2_p)], "nn", F32, name="gla_gate")
    la = row_fwd(gate_fn, [V(pre)], [W["gla_b_gate"]], [(pre.shape[1], F32)], [(0, 0, 0)], name="gla_log_decay")[0]
    gla_kw = dict(q_off=off["gq"], k_off=off["gk"], v_off=off["gv"])
    o_raw, states = gla_fwd(z, la, tri, name="gla_scan", **gla_kw)
    gla_rows = [V(o_raw, 0, GLA_DV, GLA_DV), V(z, off["zr"], GLA_DV, GLA_DV)]
    o_gla = row_fwd(gla_out_fn, gla_rows, [W["gla_out_norm"]], [(gla_w, BF16)], [(0, 0, GLA_DV)], heads=GLA_HEADS,
                    name="gla_out")[0]
    o_cat = jnp.concatenate([o_mla, o_gla], axis=1)
    f2 = [wait_comm(*ffn2_s1[k], o_cat, name=f"gather_wait_{FFN2[k]}")[0] for k in range(2)]
    x2, got = mm([(o_cat, full["w_out"])], "nn", F32, res=x1, name="out_proj", comm=stage2(FFN2[:2], f2))
    finish(FFN2[:2], got)

    hm = row_fwd(rms_fn, [V(x2)], [W["mem_attn_norm"]], [(D, BF16)], [(0, 0, 0)], name="mem_attn_norm")[0]
    mn = row_fwd(rms_fn, [V(mems)], [W["mem_norm"]], [(D, BF16)], [(0, 0, 0)], name="mem_norm")[0]
    qm_raw = mm([(hm, full["mem_w_q"])], "nn", F32, name="mem_q")
    km_raw = mm([(mn, full["mem_w_k"])], "nn", F32, name="mem_k")
    vm = mm([(mn, full["mem_w_v"])], "nn", F32, name="mem_v")
    hd = MEM_HEAD_DIM
    qm = row_fwd(rms_fn, [V(qm_raw, 0, hd, hd)], [W["mem_q_norm"]], [(mem_w, BF16)], [(0, 0, hd)], heads=MEM_HEADS,
                 name="mem_q_norm")[0]
    km = row_fwd(rms_fn, [V(km_raw, 0, hd, hd)], [W["mem_k_norm"]], [(mem_w, BF16)], [(0, 0, hd)], heads=MEM_HEADS,
                 name="mem_k_norm")[0]
    mem_kw = dict(heads=MEM_HEADS, dk=hd, dv=hd, v_off=0, v_hs=1, scale=mem_scale, causal=False)
    om = attn_fwd(qm, km, vm, name="mem_attn", **mem_kw)
    x3 = mm([(om, full["mem_w_o"])], "nn", F32, res=x2, name="mem_o")

    n2 = row_fwd(rms_fn, [V(x3)], [W["ffn2_norm"]], [(D, BF16)], [(0, 0, 0)], name="ffn2_norm")[0]
    f2_down = wait_comm(*ffn2_s1[2], n2, name=f"gather_wait_{FFN2[2]}")
    (gate2, up2, act2), got = ffn_up(n2, full["ffn2_w_gate"], full["ffn2_w_up"], name="ffn2_up",
                                     comm=stage2(FFN2[2:], f2_down))
    finish(FFN2[2:], got)
    y = mm([(act2, full["ffn2_w_down"])], "nn", F32, alpha=0.5, res=x3, name="ffn2_down")
    dy, loss_part = loss_head(y, tgt, name="loss_head")
    loss = lax.psum(loss_part[0, 0], ("x", "y", "c"))

    G, chip_sum, reduced = {}, {}, {}

    def to_halves(n):
        g = G[n]
        if n in SLOT_WEIGHTS:
            s = g
        else:
            s = _cols_to_slots(g) if n in COL_SHARDED else g.reshape(N_CHIPS, g.shape[0] // N_CHIPS, g.shape[1])
        return s.reshape(N_CHIPS, 2, s.shape[1] // 2, s.shape[2])

    def add2(names, halves, got):
        for n, a, b in zip(names, halves, got):
            chip_sum[n] = add_own_half(a, b, BF16, name=f"rs_add2_{n}")

    to_join = []

    def add4_join(names, parts):
        for n, p in zip(names, parts):
            to_join.append((n, sum_chip_parts(chip_sum[n], p, name=f"rs_add4_{n}")))

    def with_joins(comm):
        names, totals = [n for n, _ in to_join], [t for _, t in to_join]
        to_join.clear()
        if not names:
            return comm, lambda got: got
        own = 0 if comm is None else len(comm.out_shapes)
        joined = join_halves(totals)

        def split(got):
            for n, b in zip(names, got[own:]):
                reduced[n] = b.reshape(-1, b.shape[2])[:, :W[n].shape[1]]
            return got[:own]

        return (joined if comm is None else merge_comms(comm, joined)), split

    def flush_joins():
        comm, split = with_joins(None)
        if comm is not None:
            split(run_comm(comm, name=f"rs_join_{len(reduced)}"))

    in_flight = []

    def xchg_start(names):
        in_flight.append((names,) + begin(exchange_chips([chip_sum[n] for n in names]), f"xchg_start_{names[0]}"))

    def xchg_wait(after, count=1):
        for _ in range(count):
            names, comm, started = in_flight.pop(0)
            add4_join(names, wait_comm(comm, started, after, name=f"xchg_wait_{names[0]}"))

    def ffn_backward(dout, xin, tag, saved, dact_comm=None, after_dact=None):
        n_, gate, up, act = saved
        nd, ng, nu = f"{tag}_w_down", f"{tag}_w_gate", f"{tag}_w_up"
        (dgate, dup), got0 = _with(ffn_dact(dout, full[nd], gate, up, 0.5, name=f"{tag}_dact", comm=dact_comm,
                                            behind=token["last"]), dact_comm)
        if after_dact:
            after_dact(got0)
        G[nd] = mm([(act, dout)], "tn", F32, alpha=0.5, name=f"{tag}_dwd", tm=1408, tn=1024, behind=token["last"])
        hd_ = to_halves(nd)
        comm, split = with_joins(swap_halves([hd_]))
        G[ng], got = mm([(n_, dgate)], "tn", F32, name=f"{tag}_dwg", out_slots=True, tm=1024, tn=1408, rows_inner=True,
                        comm=comm)
        add2([nd], [hd_], split(got))
        xchg_start([nd])
        hg = to_halves(ng)
        G[nu], got_g = mm([(n_, dup)], "tn", F32, name=f"{tag}_dwu", out_slots=True, tm=1024, tn=1408, rows_inner=True,
                          comm=swap_halves([hg]), behind=token["last"])
        add2([ng], [hg], got_g)
        xchg_start([ng])
        hu = to_halves(nu)
        dn, got_u = mm([(dgate, full[ng]), (dup, full[nu])], "nt", F32, name=f"{tag}_dn", b_slots=True, tn=1024, tk=1408,
                       comm=swap_halves([hu]), behind=token["last"])
        add2([nu], [hu], got_u)
        xchg_start([nu])
        dx, G[f"{tag}_norm"] = row_bwd(rms_fn, [V(xin)], [W[f"{tag}_norm"]], [V(dn)], const_diff=[True], res=dout,
                                       name=f"{tag}_dnorm")
        return dx

    g3 = ffn_backward(dy, x3, "ffn2", (n2, gate2, up2, act2))
    xchg_wait(g3)

    d_om = mm([(g3, full["mem_w_o"])], "nt", F32, name="mem_o_dx", behind=token["last"])
    G["mem_w_o"] = mm([(om, g3)], "tn", F32, name="mem_o_dw")
    dqm, dkm, dvm = attn_bwd(qm, km, vm, d_om, name="mem_attn_bwd", **mem_kw)
    dqm_raw, G["mem_q_norm"] = row_bwd(rms_fn, [V(qm_raw, 0, hd, hd)], [W["mem_q_norm"]], [V(dqm, 0, hd, hd)],
                                       const_diff=[True], heads=MEM_HEADS, row_dtype=BF16, name="mem_q_norm_bwd")
    dkm_raw, G["mem_k_norm"] = row_bwd(rms_fn, [V(km_raw, 0, hd, hd)], [W["mem_k_norm"]], [V(dkm, 0, hd, hd)],
                                       const_diff=[True], heads=MEM_HEADS, row_dtype=BF16, name="mem_k_norm_bwd")
    dhm = mm([(dqm_raw, full["mem_w_q"])], "nt", F32, name="mem_q_dx")
    G["mem_w_q"] = mm([(hm, dqm_raw)], "tn", F32, name="mem_q_dw")
    dmn = mm([(dkm_raw, full["mem_w_k"]), (dvm, full["mem_w_v"])], "nt", F32, name="mem_kv_dx")
    G["mem_w_k"] = mm([(mn, dkm_raw)], "tn", F32, name="mem_k_dw")
    G["mem_w_v"] = mm([(mn, dvm)], "tn", F32, name="mem_v_dw")
    _, G["mem_norm"] = row_bwd(rms_fn, [V(mems)], [W["mem_norm"]], [V(dmn)], const_diff=[True], row_dtype=BF16,
                               name="mem_norm_bwd")
    g2, G["mem_attn_norm"] = row_bwd(rms_fn, [V(x2)], [W["mem_attn_norm"]], [V(dhm)], const_diff=[True], res=g3,
                                     name="mem_attn_norm_bwd")

    xchg_wait(g2, 2)

    d_ocat = mm([(g2, full["w_out"])], "nt", F32, name="out_proj_dx")
    G["w_out"] = mm([(o_cat, g2)], "tn", F32, name="out_proj_dw")

    d_oraw, d_zr, G["gla_out_norm"] = row_bwd(gla_out_fn, gla_rows, [W["gla_out_norm"]],
                                              [V(d_ocat, mla_w, GLA_DV, GLA_DV)], const_diff=[True], heads=GLA_HEADS,
                                              name="gla_out_bwd")
    mid_b_halves = [to_halves(n) for n in MID_B]
    comm, split = with_joins(swap_halves(mid_b_halves))
    (d_gq, d_gk, d_gv, d_la), got = gla_bwd(z, la, tri, tri.T, states, d_oraw, name="gla_scan_bwd", comm=comm, **gla_kw)
    add2(MID_B, mid_b_halves, split(got))
    xchg_start(MID_B)
    d_pre, G["gla_b_gate"] = row_bwd(gate_fn, [V(pre)], [W["gla_b_gate"]], [V(d_la)], const_diff=[True], row_dtype=BF16,
                                     name="gla_log_decay_bwd")
    d_zg = mm([(d_pre, w_gate2_p)], "nt", BF16, name="gla_gate_dx", behind=token["last"])
    G["gla_w_gate2"] = mm([(zg, d_pre)], "tn", F32, name="gla_gate_dw")[:GLA_GATE_RANK]

    comm, split = with_joins(None)
    (d_qh, d_kh, d_v), got = _with(attn_bwd(qh, kh, kvraw, d_ocat, name="mla_attn_bwd", comm=comm, **mla_kw), comm)
    split(got)
    cq = [V(d_qh, 0, LANE, HP), V(d_qh, LANE, LANE, HP)]
    ck = [V(d_kh, 0, LANE, HP), V(d_kh, LANE, LANE, HP)]
    d_qraw, d_gqn, d_gqr = row_bwd(qk_prep_fn, q_rows, [gqn, gqr, rot], cq, const_diff=[True, True, False],
                                   heads=MLA_HEADS, row_dtype=BF16, pack={0: (0, HP), 1: (LANE, HP)},
                                   pack_width=MLA_HEADS * HP, name="q_prep_bwd")
    d_kvraw, d_zkr, d_gkn, d_gkr = row_bwd(qk_prep_fn, k_rows, [gkn, gkr, rot], ck, const_diff=[True, True, False],
                                           heads=MLA_HEADS, row_dtype=BF16, pack={0: (0, HP)}, pack_width=MLA_HEADS * HP,
                                           fills=[(V(d_v, 0, MLA_V, MLA_V), LANE, HP)], name="k_prep_bwd")
    G["mla_q_norm"] = jnp.concatenate([d_gqn, d_gqr[:, :MLA_ROPE]], axis=1)
    G["mla_k_norm"] = jnp.concatenate([d_gkn, d_gkr[:, :MLA_ROPE]], axis=1)
    d_qa = mm([(d_qraw, w_q_up_p)], "nt", F32, name="q_up_dx")
    G["w_q_up"] = _unpad_q_up(mm([(qa, d_qraw)], "tn", F32, name="q_up_dw"))
    d_kva = mm([(d_kvraw, full["w_kv_up"])], "nt", F32, name="kv_up_dx")
    G["w_kv_up"] = mm([(kva, d_kvraw)], "tn", F32, name="kv_up_dw")
    d_zq, G["q_a_norm"] = row_bwd(rms_fn, [V(z, off["zq"], q_rank)], [W["q_a_norm"]], [V(d_qa)], const_diff=[True],
                                  row_dtype=BF16, name="q_a_norm_bwd")
    d_zkv, G["kv_a_norm"] = row_bwd(rms_fn, [V(z, off["zkv"], kv_rank)], [W["kv_a_norm"]], [V(d_kva)], const_diff=[True],
                                    row_dtype=BF16, name="kv_a_norm_bwd")

    seg = {"gv": d_gv, "zr": d_zr, "zq": d_zq, "gq": d_gq, "gk": d_gk, "zkv": d_zkv, "zkr": d_zkr, "zg": d_zg}
    dz = jnp.concatenate([_pad_cols(seg[n].astype(BF16), lay.size[n]) for n in lay.order], axis=1)
    xchg_wait(dz)
    comm, split = with_joins(None)
    dz_shards = lay.to_shards(dz)
    dh, got = _with(mm([(dz_shards, full["w_in"])], "nt", F32, name="in_proj_dx", b_slots=True, comm=comm), comm)
    split(got)
    G["w_in"] = mm([(h, dz_shards)], "tn", F32, name="in_proj_dw", out_slots=True)
    g1, G["mix_norm"] = row_bwd(rms_fn, [V(x1)], [W["mix_norm"]], [V(dh)], const_diff=[True], res=g2, name="mix_norm_bwd")

    mid_a = [n for n in MID_A if n != "gla_w_gate2"]
    mid_a_halves = [to_halves(n) for n in mid_a]

    def mid_a_sums(got):
        add2(mid_a, mid_a_halves, got)
        xchg_start(mid_a)

    gx = ffn_backward(g1, xs, "ffn1", ffn1_saved, dact_comm=swap_halves(mid_a_halves), after_dact=mid_a_sums)
    xchg_wait(gx, 2)

    grad, delta, new_m, new_v = {}, {}, {}, {}

    def adam_group(names, tag, behind=None):
        if any(n not in reduced for n in names):
            flush_joins()
        res = adamw([(W[n], reduced[n], M1[n], V2[n]) for n in names], name=f"adamw_{tag}", behind=behind)
        for n, (g_, d_, m_, v_) in zip(names, res):
            grad[n], delta[n], new_m[n], new_v[n] = g_, d_, m_, v_

    adam_group(FFN2, "ffn2", behind=token["last"])
    adam_group(mid_a + MID_B, "mid", behind=token["last"])
    xchg_wait(delta[MID_B[-1]], 2)
    adam_group(FFN1, "ffn1")

    small_names = SMALL + ["gla_w_gate2"]
    small_sum = allreduce_small(_pack_small([G[n] for n in small_names]), name="allreduce_small")
    small_g = dict(zip(small_names, _unpack_small(small_sum, [G[n].shape for n in small_names])))
    shard_c = W["gla_w_gate2"].shape[1]
    grad["gla_w_gate2"] = lax.dynamic_slice_in_dim(small_g["gla_w_gate2"], chip * shard_c, shard_c, axis=1)
    pw = _pack_small([W[n] for n in SMALL] + [W["gla_w_gate2"]])
    pg = _pack_small([small_g[n] for n in SMALL] + [grad["gla_w_gate2"]])
    pm = _pack_small([M1[n] for n in SMALL] + [M1["gla_w_gate2"]])
    pv = _pack_small([V2[n] for n in SMALL] + [V2["gla_w_gate2"]])
    (_, pd, pnm, pnv), = adamw([(pw, pg, pm, pv)], name="adamw_small")
    shapes = [W[n].shape for n in small_names]
    for n, d_, m_, v_ in zip(small_names, _unpack_small(pd, shapes), _unpack_small(pnm, shapes), _unpack_small(pnv, shapes)):
        delta[n], new_m[n], new_v[n] = d_, m_, v_
        if n != "gla_w_gate2":
            grad[n] = small_g[n]

    lead = lambda d: [d[n].reshape(args[n].shape) for n in WEIGHTS]
    return (loss, gx[None], *lead(grad), *lead(delta), *lead(new_m), *lead(new_v))
```

```python
import functools
import math

import numpy as np
import jax
import jax.numpy as jnp
from jax import lax
from jax.experimental import pallas as pl
from jax.experimental.pallas import tpu as pltpu

F32 = jnp.float32
BF16 = jnp.bfloat16
MXU_DTYPE = jnp.bfloat16
MESH = pl.DeviceIdType.MESH
ANY = pl.BlockSpec(memory_space=pl.ANY)

LANE = 128
EPS = 1e-6
CHUNK = 64
MLA_HEADS = 8
MLA_NOPE = 128
MLA_ROPE = 64
MLA_QK = MLA_NOPE + MLA_ROPE
MLA_V = 128
MLA_HEAD_PAD = 2 * LANE
ROPE_THETA = 10000.0
GLA_HEADS = 4
GLA_DK = 128
GLA_DV = 256
GLA_GATE_RANK = 16
GLA_TAU = 16.0
MEM_HEADS = 4
MEM_HEAD_DIM = 128
N_CHIPS = 4
N_DEV = 8

ADAM_LR = 0.001
ADAM_B1 = 0.9
ADAM_B2 = 0.999
ADAM_EPS = 1e-08
ADAM_WD = 0.01
ADAM_STEP = 10

VMEM_LIMIT = 56 * 1024 * 1024


def _cparams(sem=None):
    if sem is None:
        return pltpu.CompilerParams(vmem_limit_bytes=VMEM_LIMIT)
    return pltpu.CompilerParams(dimension_semantics=sem, vmem_limit_bytes=VMEM_LIMIT)


def _tile(dim, pref, unit=LANE):
    if dim <= pref:
        return dim
    t = (pref // unit) * unit
    while t > unit and dim % t:
        t -= unit
    assert dim % t == 0, (dim, pref, unit)
    return t


class Comm:
    def __init__(self, ins, out_shapes, nsem, start, wait, aliases=None):
        self.ins, self.out_shapes, self.nsem = list(ins), list(out_shapes), nsem
        self.start, self.wait, self.aliases = start, wait, dict(aliases or {})


def merge_comms(a, b):
    ai, ao = len(a.ins), len(a.out_shapes)

    def start(ins, outs, send, recv, base):
        a.start(ins[:ai], outs[:ao], send, recv, base)
        b.start(ins[ai:], outs[ao:], send, recv, base + a.nsem)

    def wait(ins, outs, send, recv, base):
        a.wait(ins[:ai], outs[:ao], send, recv, base)
        b.wait(ins[ai:], outs[ao:], send, recv, base + a.nsem)

    aliases = dict(a.aliases)
    aliases.update({ai + i: ao + o for i, o in b.aliases.items()})
    return Comm(a.ins + b.ins, a.out_shapes + b.out_shapes, a.nsem + b.nsem, start, wait, aliases)


def run_comm(comm, *, name):
    ni, no = len(comm.ins), len(comm.out_shapes)

    def body(*refs):
        ins, outs = refs[:ni], refs[ni:ni + no]
        send, recv = refs[ni + no:]
        comm.start(ins, outs, send, recv, 0)
        comm.wait(ins, outs, send, recv, 0)

    return pl.pallas_call(
        body, name=name, in_specs=[ANY] * ni, out_specs=[ANY] * no, out_shape=comm.out_shapes,
        input_output_aliases=comm.aliases,
        scratch_shapes=[pltpu.SemaphoreType.DMA((comm.nsem,)), pltpu.SemaphoreType.DMA((comm.nsem,))])(*comm.ins)


HBM = pl.BlockSpec(memory_space=pltpu.HBM)
SEM = pl.BlockSpec(memory_space=pltpu.SEMAPHORE)


def start_comm(comm, *, name, after=None):
    assert not comm.aliases
    ni, no = len(comm.ins), len(comm.out_shapes)
    tail = [] if after is None else [after]

    def body(*refs):
        srcs, lands = refs[:ni], refs[ni:ni + no]
        send, recv = refs[ni + no + len(tail)], refs[ni + no + len(tail) + 1]
        token = refs[-1]
        comm.start(srcs, lands, send, recv, 0)
        token[...] = jnp.zeros_like(token)

    through = [pltpu.HBM(a.shape, a.dtype) for a in comm.ins] + [pltpu.HBM(s.shape, s.dtype) for s in comm.out_shapes]
    ops = [pltpu.with_memory_space_constraint(a, pltpu.HBM) for a in comm.ins]
    ops += [pltpu.with_memory_space_constraint(lax.empty(s.shape, s.dtype), pltpu.HBM) for s in comm.out_shapes]
    ops += tail
    res = pl.pallas_call(
        body, name=name, in_specs=[HBM] * (ni + no) + [ANY] * len(tail),
        out_shape=[pltpu.SemaphoreType.DMA((comm.nsem,)), pltpu.SemaphoreType.DMA((comm.nsem,))] + through
        + [jax.ShapeDtypeStruct((8, LANE), F32)],
        out_specs=[SEM, SEM] + [HBM] * (ni + no) + [pl.BlockSpec(memory_space=pltpu.VMEM)],
        input_output_aliases={i: 2 + i for i in range(ni + no)},
        compiler_params=pltpu.CompilerParams(has_side_effects=pltpu.SideEffectType.DATAFLOW_SIDE_EFFECTING))(*ops)
    return res[0], res[1], list(res[2:2 + ni]), list(res[2 + ni:2 + ni + no]), res[-1]


def wait_comm(comm, started, after, *, name):
    send, recv, srcs, lands, _ = started
    ni, no = len(srcs), len(lands)

    def body(*refs):
        comm.wait(refs[:ni], refs[ni:ni + no], refs[ni + no], refs[ni + no + 1], 0)

    res = pl.pallas_call(
        body, name=name, in_specs=[HBM] * (ni + no) + [SEM, SEM, ANY],
        out_shape=[pltpu.HBM(a.shape, a.dtype) for a in srcs + lands], out_specs=[HBM] * (ni + no),
        input_output_aliases={i: i for i in range(ni + no)},
        compiler_params=pltpu.CompilerParams(has_side_effects=pltpu.SideEffectType.DATAFLOW_SIDE_EFFECTING),
    )(*srcs, *lands, send, recv, after)
    return list(res[ni:])


def _pcall(body, ops, *, name, grid, in_specs, out_specs, out_shape, sem, scratch_shapes=(), comm=None, behind=None):
    if behind is not None:
        n_real, inner = len(ops), body
        ops, in_specs = list(ops) + [behind], list(in_specs) + [ANY]

        def body(*refs):
            inner(*refs[:n_real], *refs[n_real + 1:])

    if comm is None:
        return pl.pallas_call(body, name=name, grid=grid, in_specs=in_specs, out_specs=out_specs, out_shape=out_shape,
                              scratch_shapes=list(scratch_shapes), compiler_params=_cparams(sem))(*ops)
    multi = isinstance(out_shape, (list, tuple))
    k_out_shape = list(out_shape) if multi else [out_shape]
    k_out_specs = list(out_specs) if multi else [out_specs]
    nki, nko, nks = len(ops), len(k_out_shape), len(scratch_shapes)
    nci, nco = len(comm.ins), len(comm.out_shapes)

    def wrapped(*refs):
        p = 0
        k_in = refs[p:p + nki]; p += nki
        c_in = refs[p:p + nci]; p += nci
        k_out = refs[p:p + nko]; p += nko
        c_out = refs[p:p + nco]; p += nco
        k_scr = refs[p:p + nks]; p += nks
        send, recv = refs[p:]
        first = pl.program_id(0) == 0
        last = pl.program_id(0) == grid[0] - 1
        for a in range(1, len(grid)):
            first = jnp.logical_and(first, pl.program_id(a) == 0)
            last = jnp.logical_and(last, pl.program_id(a) == grid[a] - 1)

        @pl.when(first)
        def _():
            comm.start(c_in, c_out, send, recv, 0)

        body(*k_in, *k_out, *k_scr)

        @pl.when(last)
        def _():
            comm.wait(c_in, c_out, send, recv, 0)

    res = pl.pallas_call(
        wrapped, name=name, grid=grid, in_specs=list(in_specs) + [ANY] * nci, out_specs=k_out_specs + [ANY] * nco,
        out_shape=k_out_shape + comm.out_shapes,
        input_output_aliases={nki + i: nko + o for i, o in comm.aliases.items()},
        scratch_shapes=list(scratch_shapes) + [pltpu.SemaphoreType.DMA((comm.nsem,)), pltpu.SemaphoreType.DMA((comm.nsem,))],
        compiler_params=_cparams(("arbitrary",) * len(grid)))(*ops, *comm.ins)
    k_res = list(res[:nko]) if multi else res[0]
    return k_res, list(res[nko:])


_DIMS = {"nn": (((1,), (0,)), ((), ())), "nt": (((1,), (1,)), ((), ())), "tn": (((0,), (0,)), ((), ()))}


def _blockspec(shape, index, rows_inner):
    return pl.BlockSpec(shape, (lambda j, i, k: index(i, j, k)) if rows_inner else index)


def mm(pairs, mode, out_dtype, *, name, alpha=1.0, res=None, tm=1024, tn=1024, tk=4096, b_slots=False, out_slots=False,
       rows_inner=False, comm=None, behind=None):
    a0, b0 = pairs[0]
    if b_slots:
        b_rows, b_cols = b0.shape[1], N_CHIPS * b0.shape[2]
    else:
        b_rows, b_cols = b0.shape
    (M, K) = a0.shape[::-1] if mode == "tn" else a0.shape
    N = b_rows if mode == "nt" else b_cols
    shard = (b_cols if b_slots else N) // N_CHIPS
    tm = _tile(M, tm)
    tn = _tile(shard if (out_slots or (b_slots and mode != "nt")) else N, tn)
    tk = _tile(shard if (b_slots and mode == "nt") else K, tk)
    nk = K // tk
    npairs = len(pairs)
    dims = _DIMS[mode]
    spec = functools.partial(_blockspec, rows_inner=rows_inner)
    if mode == "tn":
        a_spec = spec((tk, tm), lambda i, j, k: (k, i))
    else:
        a_spec = spec((tm, tk), lambda i, j, k: (i, k))
    per = shard // (tk if mode == "nt" else tn)
    if mode == "nt":
        b_spec = (spec((None, tn, tk), lambda i, j, k: (k // per, j, k % per)) if b_slots else
                  spec((tn, tk), lambda i, j, k: (j, k)))
    else:
        b_spec = (spec((None, tk, tn), lambda i, j, k: (j // per, k, j % per)) if b_slots else
                  spec((tk, tn), lambda i, j, k: (k, j)))
    if out_slots:
        assert res is None and mode != "nt"
        o_spec = spec((None, tm, tn), lambda i, j, k: (j // per, i, j % per))
        out_sds = jax.ShapeDtypeStruct((N_CHIPS, M, shard), out_dtype)
    else:
        o_spec = spec((tm, tn), lambda i, j, k: (i, j))
        out_sds = jax.ShapeDtypeStruct((M, N), out_dtype)
    has_res = res is not None

    def body(*refs):
        ab = refs[:2 * npairs]
        res_ref = refs[2 * npairs] if has_res else None
        o_ref = refs[2 * npairs + int(has_res)]

        def products():
            r = None
            for p in range(npairs):
                d = lax.dot_general(ab[2 * p][...].astype(MXU_DTYPE), ab[2 * p + 1][...].astype(MXU_DTYPE), dims,
                                    preferred_element_type=F32)
                r = d if r is None else r + d
            return r

        def finish(r):
            if alpha != 1.0:
                r = r * alpha
            if has_res:
                r = res_ref[...].astype(F32) + r
            o_ref[...] = r.astype(out_dtype)

        if nk == 1:
            finish(products())
            return
        acc = refs[-1]
        k = pl.program_id(2)

        @pl.when(k == 0)
        def _():
            acc[...] = jnp.zeros_like(acc)

        acc[...] += products()

        @pl.when(k == nk - 1)
        def _():
            finish(acc[...])

    ops, specs = [], []
    for a, b in pairs:
        ops += [a, b]
        specs += [a_spec, b_spec]
    if has_res:
        ops.append(res)
        specs.append(o_spec)
    blocks = (N // tn, M // tm) if rows_inner else (M // tm, N // tn)
    return _pcall(body, ops, name=name, grid=blocks + (nk,), in_specs=specs, out_specs=o_spec, out_shape=out_sds,
                  scratch_shapes=[pltpu.VMEM((tm, tn), F32)] if nk > 1 else [],
                  sem=("parallel", "parallel", "arbitrary"), comm=comm, behind=behind)


def _sigmoid(x):
    return 1.0 / (1.0 + jnp.exp(-x))


def ffn_up(n, wg, wu, *, name, tm=512, tn=1408, comm=None, behind=None):
    M, K = n.shape
    shard = wg.shape[2]
    N = N_CHIPS * shard
    tm, tn = _tile(M, tm), _tile(shard, tn)
    per = shard // tn
    w_spec = pl.BlockSpec((None, K, tn), lambda j, i: (j // per, 0, j % per))

    def body(n_ref, wg_ref, wu_ref, g_ref, u_ref, a_ref):
        nv = n_ref[...].astype(MXU_DTYPE)
        g = jnp.dot(nv, wg_ref[...].astype(MXU_DTYPE), preferred_element_type=F32)
        u = jnp.dot(nv, wu_ref[...].astype(MXU_DTYPE), preferred_element_type=F32)
        g_ref[...] = g.astype(g_ref.dtype)
        u_ref[...] = u.astype(u_ref.dtype)
        a_ref[...] = (g * _sigmoid(g) * u).astype(a_ref.dtype)

    o_spec = pl.BlockSpec((tm, tn), lambda j, i: (i, j))
    sds = jax.ShapeDtypeStruct((M, N), BF16)
    return _pcall(
        body, [n, wg, wu], name=name, grid=(N // tn, M // tm),
        in_specs=[pl.BlockSpec((tm, K), lambda j, i: (i, 0)), w_spec, w_spec],
        out_specs=[o_spec, o_spec, o_spec], out_shape=[sds, sds, sds], sem=("parallel", "parallel"), comm=comm,
        behind=behind)


def ffn_dact(dy, wd, gate, up, alpha, *, name, tm=512, tn=1408, comm=None, behind=None):
    M, K = dy.shape
    N = wd.shape[0]
    tm, tn = _tile(M, tm), _tile(N, tn)

    def body(dy_ref, wd_ref, g_ref, u_ref, dg_ref, du_ref):
        da = lax.dot_general(dy_ref[...].astype(MXU_DTYPE), wd_ref[...].astype(MXU_DTYPE), _DIMS["nt"],
                             preferred_element_type=F32) * alpha
        g = g_ref[...].astype(F32)
        u = u_ref[...].astype(F32)
        s = _sigmoid(g)
        du_ref[...] = (da * (g * s)).astype(du_ref.dtype)
        dg_ref[...] = (da * u * (s * (1.0 + g * (1.0 - s)))).astype(dg_ref.dtype)

    o_spec = pl.BlockSpec((tm, tn), lambda j, i: (i, j))
    sds = jax.ShapeDtypeStruct((M, N), BF16)
    return _pcall(
        body, [dy, wd, gate, up], name=name, grid=(N // tn, M // tm),
        in_specs=[pl.BlockSpec((tm, K), lambda j, i: (i, 0)), pl.BlockSpec((tn, K), lambda j, i: (j, 0)), o_spec, o_spec],
        out_specs=[o_spec, o_spec], out_shape=[sds, sds], sem=("parallel", "parallel"), comm=comm, behind=behind)


def _window(width, off, ext):
    ww = LANE
    while ww < width:
        if ww >= ext and off // ww == (off + ext - 1) // ww and width % ww == 0:
            break
        ww *= 2
    else:
        ww = width
    return ww, off // ww, off - (off // ww) * ww


class V:
    def __init__(self, arr, off=0, w=None, hs=0, diff=True):
        self.arr, self.off, self.hs, self.diff = arr, off, hs, diff
        self.w = arr.shape[1] - off if w is None else w

    def window(self, heads, tr):
        ww, blk, inner = _window(self.arr.shape[1], self.off, (heads - 1) * self.hs + self.w)
        return pl.BlockSpec((tr, ww), lambda i, blk=blk: (i, blk)), inner


def _const_spec(c):
    return pl.BlockSpec(c.shape, lambda i: (0, 0))


def row_fwd(fn, rows, consts, outs, out_map, *, heads=1, tr=256, name):
    S = rows[0].arr.shape[0]
    tr = _tile(S, tr, 8)
    wins = [v.window(heads, tr) for v in rows]
    nr, nc = len(rows), len(consts)

    def body(*refs):
        row_refs, const_refs, out_refs = refs[:nr], refs[nr:nr + nc], refs[nr + nc:]
        cv = [c[...].astype(F32) for c in const_refs]
        for h in range(heads):
            rv = []
            for v, (_, io), r in zip(rows, wins, row_refs):
                lo = io + h * v.hs
                rv.append(r[:, lo:lo + v.w].astype(F32))
            res = fn(*rv, *cv)
            for (ai, off, hs), o in zip(out_map, res):
                lo = off + h * hs
                out_refs[ai][:, lo:lo + o.shape[1]] = o.astype(out_refs[ai].dtype)

    return pl.pallas_call(
        body, name=name, grid=(S // tr,),
        in_specs=[w[0] for w in wins] + [_const_spec(c) for c in consts],
        out_specs=[pl.BlockSpec((tr, w), lambda i: (i, 0)) for w, _ in outs],
        out_shape=[jax.ShapeDtypeStruct((S, w), d) for w, d in outs],
        compiler_params=_cparams(("parallel",)))(*[v.arr for v in rows], *consts)


def row_bwd(fn, rows, consts, cots, *, const_diff, heads=1, tr=256, res=None, row_dtype=F32, pack=None, pack_width=0,
            fills=(), name):
    S = rows[0].arr.shape[0]
    tr = _tile(S, tr, 8)
    pack = dict(pack or {})
    nr, nc, nct, nf = len(rows), len(consts), len(cots), len(fills)
    wins = [v.window(heads, tr) for v in rows]
    cwins = [v.window(heads, tr) for v in cots]
    fwins = [v.window(heads, tr) for v, _, _ in fills]
    drows = [k for k, v in enumerate(rows) if v.diff]
    dconsts = [k for k in range(nc) if const_diff[k]]
    has_res = res is not None
    assert not (has_res and 0 in pack)
    widths = [pack_width] if pack else []
    place = []
    for n, k in enumerate(drows):
        if n in pack:
            place.append((0,) + tuple(pack[n]))
        else:
            place.append((len(widths), 0, rows[k].w))
            widths.append(rows[k].w * (heads if rows[k].hs else 1))

    def body(*refs):
        row_refs = refs[:nr]
        const_refs = refs[nr:nr + nc]
        cot_refs = refs[nr + nc:nr + nc + nct]
        p = nr + nc + nct
        fill_refs = refs[p:p + nf]
        p += nf
        res_ref = refs[p] if has_res else None
        p += int(has_res)
        grow_refs = refs[p:p + len(widths)]
        gconst_refs = refs[p + len(widths):]
        i = pl.program_id(0)
        cv = [c[...].astype(F32) for c in const_refs]
        shared = [None] * len(drows)
        gc_sum = [None] * len(dconsts)
        for h in range(heads):
            rv = []
            for v, (_, io), r in zip(rows, wins, row_refs):
                lo = io + h * v.hs
                rv.append(r[:, lo:lo + v.w].astype(F32))
            ct = []
            for v, (_, io), r in zip(cots, cwins, cot_refs):
                lo = io + h * v.hs
                ct.append(r[:, lo:lo + v.w].astype(F32))

            def closed(*d):
                rr, cc = list(rv), list(cv)
                for k, val in zip(drows, d[:len(drows)]):
                    rr[k] = val
                for k, val in zip(dconsts, d[len(drows):]):
                    cc[k] = val
                return tuple(fn(*rr, *cc))

            _, vjp = jax.vjp(closed, *[rv[k] for k in drows], *[cv[k] for k in dconsts])
            grads = vjp(tuple(ct))
            for n, k in enumerate(drows):
                g = grads[n]
                if rows[k].hs == 0 and heads > 1:
                    shared[n] = g if shared[n] is None else shared[n] + g
                else:
                    if n == 0 and has_res:
                        g = g + res_ref[:, h * rows[k].w:(h + 1) * rows[k].w].astype(F32)
                    out, off, hs = place[n]
                    grow_refs[out][:, off + h * hs:off + h * hs + rows[k].w] = g.astype(row_dtype)
            for (v, off, hs), (_, io), r in zip(fills, fwins, fill_refs):
                lo = io + h * v.hs
                grow_refs[0][:, off + h * hs:off + h * hs + v.w] = r[:, lo:lo + v.w].astype(row_dtype)
            for n in range(len(dconsts)):
                g = grads[len(drows) + n]
                gc_sum[n] = g if gc_sum[n] is None else gc_sum[n] + g
        for n, k in enumerate(drows):
            if shared[n] is not None:
                g = shared[n]
                if n == 0 and has_res:
                    g = g + res_ref[...].astype(F32)
                grow_refs[place[n][0]][...] = g.astype(row_dtype)

        @pl.when(i == 0)
        def _():
            for n in range(len(dconsts)):
                gconst_refs[n][...] = gc_sum[n]

        @pl.when(i > 0)
        def _():
            for n in range(len(dconsts)):
                gconst_refs[n][...] += gc_sum[n]

    in_specs = [w[0] for w in wins] + [_const_spec(c) for c in consts] + [w[0] for w in cwins] + [w[0] for w in fwins]
    ops = [v.arr for v in rows] + list(consts) + [v.arr for v in cots] + [v.arr for v, _, _ in fills]
    if has_res:
        in_specs.append(pl.BlockSpec((tr, widths[0]), lambda i: (i, 0)))
        ops.append(res)
    out_specs = [pl.BlockSpec((tr, w), lambda i: (i, 0)) for w in widths]
    out_shape = [jax.ShapeDtypeStruct((S, w), row_dtype) for w in widths]
    for k in dconsts:
        out_specs.append(_const_spec(consts[k]))
        out_shape.append(jax.ShapeDtypeStruct(consts[k].shape, F32))
    return pl.pallas_call(body, name=name, grid=(S // tr,), in_specs=in_specs, out_specs=out_specs,
                          out_shape=out_shape, compiler_params=_cparams(("arbitrary",)))(*ops)


def _rms(x, g, n=None):
    n = x.shape[-1] if n is None else n
    ms = jnp.sum(x * x, axis=-1, keepdims=True) * (1.0 / n)
    return x * lax.rsqrt(ms + EPS) * g


def rms_fn(x, g):
    return (_rms(x, g),)


def qk_prep_fn(nope, rope, cos, sin, gn, gr, rot):
    ms = (jnp.sum(nope * nope, axis=-1, keepdims=True) + jnp.sum(rope * rope, axis=-1, keepdims=True)) * (1.0 / MLA_QK)
    r = lax.rsqrt(ms + EPS)
    on = nope * r * gn
    orr = rope * r * gr
    turned = jnp.dot(orr, rot, precision=lax.Precision.HIGHEST, preferred_element_type=F32)
    return on, orr * cos + turned * sin


def gla_out_fn(o, zr, g):
    return (_rms(o, g) * (zr * _sigmoid(zr)),)


def gate_fn(pre, b):
    t = pre + b
    return ((jnp.minimum(t, 0.0) - jnp.log(1.0 + jnp.exp(-jnp.abs(t)))) * (1.0 / GLA_TAU),)


def _attn_probs(q_ref, k_ref, scale, q0, kext):
    s = lax.dot_general(q_ref[...].astype(MXU_DTYPE), k_ref[0:kext, :].astype(MXU_DTYPE), _DIMS["nt"],
                        preferred_element_type=F32) * scale
    if q0 is not None:
        qc = (q0 + lax.broadcasted_iota(jnp.int32, s.shape, 0)) // CHUNK
        kc = lax.broadcasted_iota(jnp.int32, s.shape, 1) // CHUNK
        s = jnp.where(kc <= qc, s, -1e30)
    m = jnp.max(s, axis=-1, keepdims=True)
    e = jnp.exp(s - m)
    return e / jnp.sum(e, axis=-1, keepdims=True)


def _per_query_block(one, causal, nq, tq, Sk):
    if not causal:
        one(None, Sk, None)
        return
    assert tq % CHUNK == 0
    for ib in range(nq):
        pl.when(pl.program_id(1) == ib)(functools.partial(one, ib * tq, min(Sk, (ib + 1) * tq), ib))


def attn_fwd(q, k, v, *, heads, dk, dv, v_off, v_hs, scale, causal, name, tq=256, comm=None):
    Sq, Sk = q.shape[0], k.shape[0]
    tq = _tile(Sq, tq, 8)

    def body(q_ref, k_ref, v_ref, o_ref):
        def one(q0, kext, ib):
            p = _attn_probs(q_ref, k_ref, scale, q0, kext)
            o_ref[...] = jnp.dot(p.astype(MXU_DTYPE), v_ref[0:kext, :].astype(MXU_DTYPE),
                                 preferred_element_type=F32).astype(o_ref.dtype)

        _per_query_block(one, causal, Sq // tq, tq, Sk)

    return _pcall(
        body, [q, k, v], name=name, grid=(heads, Sq // tq),
        in_specs=[pl.BlockSpec((tq, dk), lambda h, i: (i, h)), pl.BlockSpec((Sk, dk), lambda h, i: (0, h)),
                  pl.BlockSpec((Sk, dv), lambda h, i: (0, v_off + h * v_hs))],
        out_specs=pl.BlockSpec((tq, dv), lambda h, i: (i, h)),
        out_shape=jax.ShapeDtypeStruct((Sq, heads * dv), BF16), sem=("parallel", "parallel"), comm=comm)


def attn_bwd(q, k, v, do, *, heads, dk, dv, v_off, v_hs, scale, causal, name, tq=256, comm=None):
    Sq, Sk = q.shape[0], k.shape[0]
    tq = _tile(Sq, tq, 8)

    def body(q_ref, k_ref, v_ref, do_ref, dq_ref, dk_ref, dv_ref):
        @pl.when(pl.program_id(1) == 0)
        def _():
            dk_ref[...] = jnp.zeros_like(dk_ref)
            dv_ref[...] = jnp.zeros_like(dv_ref)

        def one(q0, kext, ib):
            p = _attn_probs(q_ref, k_ref, scale, q0, kext)
            dob = do_ref[...].astype(MXU_DTYPE)
            dp = lax.dot_general(dob, v_ref[0:kext, :].astype(MXU_DTYPE), _DIMS["nt"], preferred_element_type=F32)
            delta = jnp.sum(p * dp, axis=-1, keepdims=True)
            ds = (p * (dp - delta) * scale).astype(MXU_DTYPE)
            dq_ref[...] = jnp.dot(ds, k_ref[0:kext, :].astype(MXU_DTYPE), preferred_element_type=F32)
            dk_ref[0:kext, :] += lax.dot_general(ds, q_ref[...].astype(MXU_DTYPE), _DIMS["tn"],
                                                 preferred_element_type=F32)
            dv_ref[0:kext, :] += lax.dot_general(p.astype(MXU_DTYPE), dob, _DIMS["tn"], preferred_element_type=F32)

        _per_query_block(one, causal, Sq // tq, tq, Sk)

    return _pcall(
        body, [q, k, v, do], name=name, grid=(heads, Sq // tq),
        in_specs=[pl.BlockSpec((tq, dk), lambda h, i: (i, h)), pl.BlockSpec((Sk, dk), lambda h, i: (0, h)),
                  pl.BlockSpec((Sk, dv), lambda h, i: (0, v_off + h * v_hs)),
                  pl.BlockSpec((tq, dv), lambda h, i: (i, h))],
        out_specs=[pl.BlockSpec((tq, dk), lambda h, i: (i, h)), pl.BlockSpec((Sk, dk), lambda h, i: (0, h)),
                   pl.BlockSpec((Sk, dv), lambda h, i: (0, h))],
        out_shape=[jax.ShapeDtypeStruct((Sq, heads * dk), F32), jax.ShapeDtypeStruct((Sk, heads * dk), F32),
                   jax.ShapeDtypeStruct((Sk, heads * dv), F32)],
        sem=("parallel", "arbitrary"), comm=comm)


def _gla_chunk(k, g, tri_ref):
    b = jnp.dot(tri_ref[...], g, precision=lax.Precision.HIGHEST, preferred_element_type=F32)
    b_end = jnp.sum(g, axis=0, keepdims=True)
    e = jnp.exp(b_end - b)
    return k * e, e, jnp.exp(b_end)


def _gla_windows(z, q_off, k_off, v_off, rows_of):
    H, DK, DV = GLA_HEADS, GLA_DK, GLA_DV
    specs, inner = [], []
    for off, ext in ((q_off, H * DK), (k_off, H * DK), (v_off, H * DV)):
        ww, blk, io = _window(z.shape[1], off, ext)
        specs.append(pl.BlockSpec((CHUNK, ww), lambda c, blk=blk: (rows_of(c), blk)))
        inner.append(io)
    return specs, inner


def gla_fwd(z, la, tri, *, q_off, k_off, v_off, name, comm=None):
    S = z.shape[0]
    nchunk = S // CHUNK
    H, DK, DV = GLA_HEADS, GLA_DK, GLA_DV
    qscale = DK ** -0.5
    zspecs, (qi, ki, vi) = _gla_windows(z, q_off, k_off, v_off, lambda c: c)

    def body(q_ref, k_ref, v_ref, la_ref, tri_ref, o_ref, st_ref, state):
        @pl.when(pl.program_id(0) == 0)
        def _():
            state[...] = jnp.zeros_like(state)

        for h in range(H):
            dks, dvs = slice(h * DK, (h + 1) * DK), slice(h * DV, (h + 1) * DV)
            k = k_ref[:, ki + h * DK:ki + (h + 1) * DK].astype(F32)
            v = v_ref[:, vi + h * DV:vi + (h + 1) * DV]
            q = q_ref[:, qi + h * DK:qi + (h + 1) * DK].astype(F32)
            kdec, _, decay = _gla_chunk(k, la_ref[:, dks].astype(F32), tri_ref)
            ut = lax.dot_general(v.astype(MXU_DTYPE), kdec.astype(MXU_DTYPE), _DIMS["tn"], preferred_element_type=F32)
            new = state[h] * decay + ut
            state[h] = new
            st_ref[h] = new
            qs = (q * qscale).astype(MXU_DTYPE)
            o_ref[:, dvs] = lax.dot_general(qs, new.astype(MXU_DTYPE), _DIMS["nt"], preferred_element_type=F32)

    return _pcall(
        body, [z, z, z, la, tri], name=name, grid=(nchunk,),
        in_specs=zspecs + [pl.BlockSpec((CHUNK, H * DK), lambda c: (c, 0)), pl.BlockSpec((CHUNK, CHUNK), lambda c: (0, 0))],
        out_specs=[pl.BlockSpec((CHUNK, H * DV), lambda c: (c, 0)),
                   pl.BlockSpec((H, None, DV, DK), lambda c: (0, c, 0, 0))],
        out_shape=[jax.ShapeDtypeStruct((S, H * DV), F32), jax.ShapeDtypeStruct((H, nchunk, DV, DK), F32)],
        scratch_shapes=[pltpu.VMEM((H, DV, DK), F32)], sem=("arbitrary",), comm=comm)


def gla_bwd(z, la, tri, trit, states, do, *, q_off, k_off, v_off, name, comm=None):
    S = z.shape[0]
    nchunk = S // CHUNK
    H, DK, DV = GLA_HEADS, GLA_DK, GLA_DV
    qscale = DK ** -0.5
    last = nchunk - 1
    zspecs, (qi, ki, vi) = _gla_windows(z, q_off, k_off, v_off, lambda c: last - c)

    def body(q_ref, k_ref, v_ref, la_ref, tri_ref, trit_ref, st_ref, sp_ref, do_ref, dq_ref, dk_ref, dv_ref, dla_ref,
             dstate):
        c = pl.program_id(0)
        cc = last - c

        @pl.when(c == 0)
        def _():
            dstate[...] = jnp.zeros_like(dstate)

        for h in range(H):
            dks, dvs = slice(h * DK, (h + 1) * DK), slice(h * DV, (h + 1) * DV)
            kf = k_ref[:, ki + h * DK:ki + (h + 1) * DK].astype(F32)
            vb16 = v_ref[:, vi + h * DV:vi + (h + 1) * DV].astype(MXU_DTYPE)
            q = q_ref[:, qi + h * DK:qi + (h + 1) * DK].astype(F32)
            kdec, e, decay = _gla_chunk(kf, la_ref[:, dks].astype(F32), tri_ref)
            dob = do_ref[:, dvs].astype(MXU_DTYPE)
            stb = st_ref[h].astype(MXU_DTYPE)
            qs = (q * qscale).astype(MXU_DTYPE)
            dq_ref[:, dks] = jnp.dot(dob, stb, preferred_element_type=F32) * qscale
            dst = dstate[h] + lax.dot_general(dob, qs, _DIMS["tn"], preferred_element_type=F32)
            prev = jnp.where(cc > 0, sp_ref[h], 0.0)
            ddecay = jnp.sum(dst * prev, axis=0, keepdims=True)
            dstate[h] = dst * decay
            dub = dst.astype(MXU_DTYPE)
            dv_ref[:, dvs] = lax.dot_general(kdec.astype(MXU_DTYPE), dub, _DIMS["nt"], preferred_element_type=F32)
            dkdec = jnp.dot(vb16, dub, preferred_element_type=F32)
            dk_ref[:, dks] = dkdec * e
            w = dkdec * kf * e
            db_end = jnp.sum(w, axis=0, keepdims=True) + ddecay * decay
            dla_ref[:, dks] = db_end - jnp.dot(trit_ref[...], w, precision=lax.Precision.HIGHEST,
                                               preferred_element_type=F32)

    def rows(width):
        return pl.BlockSpec((CHUNK, width), lambda c: (last - c, 0))

    square = pl.BlockSpec((CHUNK, CHUNK), lambda c: (0, 0))
    return _pcall(
        body, [z, z, z, la, tri, trit, states, states, do], name=name, grid=(nchunk,),
        in_specs=zspecs + [rows(H * DK), square, square,
                           pl.BlockSpec((H, None, DV, DK), lambda c: (0, last - c, 0, 0)),
                           pl.BlockSpec((H, None, DV, DK), lambda c: (0, jnp.maximum(last - c - 1, 0), 0, 0)),
                           rows(H * DV)],
        out_specs=[rows(H * DK), rows(H * DK), rows(H * DV), rows(H * DK)],
        out_shape=[jax.ShapeDtypeStruct((S, H * DK), F32), jax.ShapeDtypeStruct((S, H * DK), F32),
                   jax.ShapeDtypeStruct((S, H * DV), F32), jax.ShapeDtypeStruct((S, H * DK), F32)],
        scratch_shapes=[pltpu.VMEM((H, DV, DK), F32)], sem=("arbitrary",), comm=comm)


def loss_head(y, target, *, name, tr=256):
    S, D = y.shape
    tr = _tile(S, tr, 8)

    def body(y_ref, t_ref, dy_ref, loss_ref):
        i = pl.program_id(0)
        err = y_ref[...] - t_ref[...]
        dy_ref[...] = err * (1.0 / D)
        part = jnp.zeros((1, LANE), F32) + 0.5 * jnp.sum(jnp.sum(err * err, axis=-1, keepdims=True) * (1.0 / D))

        @pl.when(i == 0)
        def _():
            loss_ref[...] = part

        @pl.when(i > 0)
        def _():
            loss_ref[...] += part

    spec = pl.BlockSpec((tr, D), lambda i: (i, 0))
    return pl.pallas_call(
        body, name=name, grid=(S // tr,), in_specs=[spec, spec],
        out_specs=[spec, pl.BlockSpec((1, LANE), lambda i: (0, 0))],
        out_shape=[jax.ShapeDtypeStruct((S, D), F32), jax.ShapeDtypeStruct((1, LANE), F32)],
        compiler_params=_cparams(("arbitrary",)))(y, target)


def _core_index():
    return lax.axis_index("c").astype(jnp.int32).reshape(1)


def _chip_slots():
    x, y, c = lax.axis_index("x"), lax.axis_index("y"), lax.axis_index("c")
    return jnp.stack([2 * x + y, 2 * (1 - x) + y, 2 * x + (1 - y), 2 * (1 - x) + (1 - y), c]).astype(jnp.int32)


def sum_chip_parts(own, parts, *, name, tr=1024):
    _, R, C = own.shape
    tr = _tile(R, tr, 8)

    def body(idx_ref, o_ref, p0_ref, p1_ref, p2_ref, out_ref):
        acc = o_ref[...].astype(F32) + p0_ref[...].astype(F32)
        acc = acc + p1_ref[...].astype(F32)
        out_ref[...] = acc + p2_ref[...].astype(F32)

    def slot(k):
        return pl.BlockSpec((None, tr, C), lambda i, idx: (idx[k], i, 0))

    grid_spec = pltpu.PrefetchScalarGridSpec(num_scalar_prefetch=1, grid=(R // tr,),
                                             in_specs=[slot(0), slot(1), slot(2), slot(3)], out_specs=slot(4))
    return pl.pallas_call(body, name=name, grid_spec=grid_spec, out_shape=jax.ShapeDtypeStruct((2, R, C), F32),
                          compiler_params=_cparams(("parallel",)))(_chip_slots(), own, parts, parts, parts)


def add_own_half(g, got, out_dtype, *, name, tr=1024):
    n, _, R, C = g.shape
    tr = _tile(R, tr, 8)

    def body(c_ref, a_ref, b_ref, o_ref):
        o_ref[...] = (a_ref[...].astype(F32) + b_ref[...].astype(F32)).astype(out_dtype)

    spec = pl.BlockSpec((None, tr, C), lambda s, i, c: (s, i, 0))
    grid_spec = pltpu.PrefetchScalarGridSpec(
        num_scalar_prefetch=1, grid=(n, R // tr),
        in_specs=[pl.BlockSpec((None, None, tr, C), lambda s, i, c: (s, c[0], i, 0)), spec], out_specs=spec)
    return pl.pallas_call(body, name=name, grid_spec=grid_spec, out_shape=jax.ShapeDtypeStruct((n, R, C), out_dtype),
                          compiler_params=_cparams(("parallel", "parallel")))(_core_index(), g, got)


def adamw(items, *, name, max_steps=16, behind=None):
    c1 = 1.0 / (1.0 - ADAM_B1 ** ADAM_STEP)
    c2 = 1.0 / (1.0 - ADAM_B2 ** ADAM_STEP)
    n = len(items)
    steps = max_steps
    while steps > 1 and any(it[0].shape[0] % (8 * steps) for it in items):
        steps //= 2
    tail = [] if behind is None else [behind]

    def body(*refs):
        for a in range(n):
            w_ref, g_ref, m_ref, v_ref = refs[4 * a:4 * a + 4]
            go_ref, d_ref, nm_ref, nv_ref = refs[4 * n + len(tail) + 4 * a:4 * n + len(tail) + 4 * a + 4]
            gv = g_ref[...]
            go_ref[...] = gv
            nm = ADAM_B1 * m_ref[...] + (1.0 - ADAM_B1) * gv
            nv = ADAM_B2 * v_ref[...] + (1.0 - ADAM_B2) * (gv * gv)
            nm_ref[...] = nm
            nv_ref[...] = nv
            d_ref[...] = -ADAM_LR * ((nm * c1) / (jnp.sqrt(nv * c2) + ADAM_EPS) + ADAM_WD * w_ref[...])

    ops, in_specs, out_specs, out_shape = [], [], [], []
    for w, g, m, v in items:
        R, C = w.shape
        spec = pl.BlockSpec((R // steps, C), lambda i: (i, 0))
        ops += [w, g, m, v]
        in_specs += [spec] * 4
        out_specs += [spec] * 4
        out_shape += [jax.ShapeDtypeStruct((R, C), F32)] * 4
    flat = _pcall(body, ops + tail, name=name, grid=(steps,), in_specs=in_specs + [ANY] * len(tail), out_specs=out_specs,
                  out_shape=out_shape, sem=("parallel",))
    return [tuple(flat[4 * a:4 * a + 4]) for a in range(n)]


def _place():
    x, y, c = lax.axis_index("x"), lax.axis_index("y"), lax.axis_index("c")
    chips = [(1 - x, y), (x, 1 - y), (1 - x, 1 - y)]
    return x, y, c, chips


def _rcopy(src, dst, send, recv, j, to):
    return pltpu.make_async_remote_copy(src_ref=src, dst_ref=dst, send_sem=send.at[j], recv_sem=recv.at[j], device_id=to,
                                        device_id_type=MESH)


def gather_stage1(shards, split):
    n = len(shards)
    ins = [s.reshape(2, s.shape[0] // 2, s.shape[1]) if sp else s for s, sp in zip(shards, split)]
    outs = [jax.ShapeDtypeStruct((N_CHIPS,) + a.shape, a.dtype) for a in ins]

    def start(in_refs, out_refs, send, recv, base):
        x, y, c, chips = _place()
        mine = 2 * x + y
        for i in range(n):
            src = in_refs[i].at[c] if split[i] else in_refs[i]
            dst = out_refs[i].at[mine, c] if split[i] else out_refs[i].at[mine]
            for k, (px, py) in enumerate(chips):
                _rcopy(src, dst, send, recv, base + 3 * i + k, (px, py, c)).start()

    def wait(in_refs, out_refs, send, recv, base):
        x, y, c, chips = _place()
        for i in range(n):
            src = in_refs[i].at[c] if split[i] else in_refs[i]
            for k, (px, py) in enumerate(chips):
                dst = out_refs[i].at[2 * px + py, c] if split[i] else out_refs[i].at[2 * px + py]
                _rcopy(src, dst, send, recv, base + 3 * i + k, (px, py, c)).wait()

    return Comm(ins, outs, 3 * n, start, wait)


def gather_stage2(slots, shards, split):
    n = len(slots)
    own = [s.reshape(2, s.shape[0] // 2, s.shape[1]) if sp else s for s, sp in zip(shards, split)]

    def copies(in_refs, out_refs, send, recv, base):
        x, y, c, chips = _place()
        sib = (x, y, 1 - c)
        for i in range(n):
            j = base + 4 * i
            mine = out_refs[i].at[2 * x + y]
            yield _rcopy(in_refs[n + i], mine, send, recv, j + 3, sib), _rcopy(in_refs[n + i], mine, send, recv, j + 3, sib)
            if split[i]:
                for k, (px, py) in enumerate(chips):
                    s = 2 * px + py
                    yield (_rcopy(in_refs[i].at[s, c], out_refs[i].at[s, c], send, recv, j + k, sib),
                           _rcopy(in_refs[i].at[s, c], out_refs[i].at[s, 1 - c], send, recv, j + k, sib))

    def start(*a):
        for out, _ in copies(*a):
            out.start()

    def wait(*a):
        for _, back in copies(*a):
            back.wait()

    return Comm(list(slots) + own, [jax.ShapeDtypeStruct(s.shape, s.dtype) for s in slots], 4 * n, start, wait,
                {i: i for i in range(n)})


def swap_halves(gs):
    n = len(gs)

    def copies(in_refs, out_refs, send, recv, base):
        x, y, c, _ = _place()
        return [_rcopy(in_refs[i].at[s, 1 - c], out_refs[i].at[s], send, recv, base + N_CHIPS * i + s, (x, y, 1 - c))
                for i in range(n) for s in range(N_CHIPS)]

    def start(*a):
        for cp in copies(*a):
            cp.start()

    def wait(*a):
        for cp in copies(*a):
            cp.wait()

    return Comm(gs, [jax.ShapeDtypeStruct((N_CHIPS,) + g.shape[2:], g.dtype) for g in gs], N_CHIPS * n, start, wait)


def exchange_chips(ps):
    n = len(ps)

    def start(in_refs, out_refs, send, recv, base):
        x, y, c, chips = _place()
        for i in range(n):
            for k, (px, py) in enumerate(chips):
                _rcopy(in_refs[i].at[2 * px + py], out_refs[i].at[2 * x + y], send, recv, base + 3 * i + k,
                       (px, py, c)).start()

    def wait(in_refs, out_refs, send, recv, base):
        x, y, c, chips = _place()
        for i in range(n):
            for k, (px, py) in enumerate(chips):
                _rcopy(in_refs[i].at[2 * px + py], out_refs[i].at[2 * px + py], send, recv, base + 3 * i + k,
                       (px, py, c)).wait()

    return Comm(ps, [jax.ShapeDtypeStruct(p.shape, p.dtype) for p in ps], 3 * n, start, wait)


def join_halves(fs):
    n = len(fs)

    def start(in_refs, out_refs, send, recv, base):
        x, y, c, _ = _place()
        for i in range(n):
            _rcopy(in_refs[i].at[c], out_refs[i].at[c], send, recv, base + i, (x, y, 1 - c)).start()

    def wait(in_refs, out_refs, send, recv, base):
        x, y, c, _ = _place()
        for i in range(n):
            _rcopy(in_refs[i].at[c], out_refs[i].at[1 - c], send, recv, base + i, (x, y, 1 - c)).wait()

    return Comm(fs, [jax.ShapeDtypeStruct(f.shape, f.dtype) for f in fs], n, start, wait, {i: i for i in range(n)})


def allreduce_small(v, *, name):
    m_per, n = v.shape

    def body(x_ref, sum_ref, all_ref, send_sems, recv_sems, local_sem):
        x, y, c, chips = _place()
        me, sibling = (x, y, c), (x, y, 1 - c)

        def rows(px, py, pc):
            return all_ref.at[pl.ds((4 * px + 2 * py + pc) * m_per, m_per), :]

        def copy(k, block, to, src=None):
            return pltpu.make_async_remote_copy(src_ref=rows(*block) if src is None else src, dst_ref=rows(*block),
                                                send_sem=send_sems.at[k], recv_sem=recv_sems.at[k], device_id=to,
                                                device_id_type=MESH)

        mine = pltpu.make_async_copy(x_ref, rows(*me), local_sem)
        mine.start()
        first = [copy(0, me, sibling, src=x_ref)]
        first += [copy(1 + j, me, (*chip, c), src=x_ref) for j, chip in enumerate(chips)]
        for cp in first:
            cp.start()
        passed = [copy(4 + j, (*chip, c), sibling) for j, chip in enumerate(chips)]
        for j, chip in enumerate(chips):
            copy(1 + j, (*chip, c), me).wait_recv()
            passed[j].start()
        copy(0, sibling, me).wait_recv()
        for j, chip in enumerate(chips):
            copy(4 + j, (*chip, 1 - c), me).wait_recv()
        for cp in first + passed:
            cp.wait_send()
        mine.wait()
        acc = all_ref[0:m_per, :]
        for d in range(1, N_DEV):
            acc = acc + all_ref[d * m_per:(d + 1) * m_per, :]
        sum_ref[...] = acc

    vm = pl.BlockSpec(memory_space=pltpu.VMEM)
    return pl.pallas_call(
        body, name=name, in_specs=[vm], out_specs=vm, out_shape=jax.ShapeDtypeStruct((m_per, n), F32),
        scratch_shapes=[pltpu.VMEM((N_DEV * m_per, n), F32), pltpu.SemaphoreType.DMA((7,)),
                        pltpu.SemaphoreType.DMA((7,)), pltpu.SemaphoreType.DMA],
    )(v)


def _cols_to_slots(w):
    r, c4 = w.shape
    return w.reshape(r, N_CHIPS, c4 // N_CHIPS).transpose(1, 0, 2)


def _slots_to_cols(w):
    n, r, c = w.shape
    return w.transpose(1, 0, 2).reshape(r, n * c)


def _pad_cols(a, width):
    return jnp.pad(a, ((0, 0), (0, width - a.shape[1])))


class InLayout:
    def __init__(self, q_rank, kv_rank):
        gk = GLA_HEADS * GLA_DK
        gv = GLA_HEADS * GLA_DV
        sizes = [q_rank, kv_rank, MLA_ROPE, gk, gk, gv, GLA_GATE_RANK, gv]
        names = ["zq", "zkv", "zkr", "gq", "gk", "gv", "zg", "zr"]
        starts = np.concatenate([[0], np.cumsum(sizes)[:-1]])
        self.ref = {n: (int(s), int(z)) for n, s, z in zip(names, starts, sizes)}
        self.ref_width = int(sum(sizes))
        self.order = ["gv", "zr", "zq", "gq", "gk", "zkv", "zkr", "zg"]
        self.off, self.size = {}, {}
        pos = 0
        for n in self.order:
            padded = -(-self.ref[n][1] // LANE) * LANE
            self.off[n], self.size[n] = pos, padded
            pos += padded
        self.width = pos
        self.shard = self.ref_width // N_CHIPS
        self.shard_pad = -(-self.shard // LANE) * LANE

    def _pieces(self, lo, hi):
        out = []
        while lo < hi:
            s = lo // self.shard
            end = min(hi, (s + 1) * self.shard)
            out.append((s * self.shard_pad + lo - s * self.shard, s * self.shard_pad + end - s * self.shard))
            lo = end
        return out

    def from_shards(self, zs):
        cols = []
        for n in self.order:
            start, size = self.ref[n]
            cols += [zs[:, a:b] for a, b in self._pieces(start, start + size)]
            if self.size[n] > size:
                cols.append(jnp.zeros((zs.shape[0], self.size[n] - size), zs.dtype))
        return jnp.concatenate(cols, axis=1)

    def to_shards(self, dz):
        names = sorted(self.ref, key=lambda n: self.ref[n][0])
        ref = jnp.concatenate([dz[:, self.off[n]:self.off[n] + self.ref[n][1]] for n in names], axis=1)
        ref = ref.reshape(dz.shape[0], N_CHIPS, self.shard)
        return jnp.pad(ref, ((0, 0), (0, 0), (0, self.shard_pad - self.shard))).reshape(dz.shape[0], -1)


def _pad_q_up(w):
    r = w.shape[0]
    w = w.reshape(r, MLA_HEADS, MLA_QK)
    w = jnp.pad(w, ((0, 0), (0, 0), (0, MLA_HEAD_PAD - MLA_QK)))
    return w.reshape(r, MLA_HEADS * MLA_HEAD_PAD)


def _unpad_q_up(g):
    r = g.shape[0]
    return g.reshape(r, MLA_HEADS, MLA_HEAD_PAD)[:, :, :MLA_QK].reshape(r, MLA_HEADS * MLA_QK)


def _rope_tables(positions):
    half = MLA_ROPE // 2
    inv_freq = ROPE_THETA ** (-jnp.arange(half, dtype=F32) / half)
    ang = positions.astype(F32).reshape(-1, 1) * inv_freq
    cos, sin = jnp.cos(ang), jnp.sin(ang)
    s = ang.shape[0]
    cosf = jnp.concatenate([cos, cos, jnp.ones((s, LANE - MLA_ROPE), F32)], axis=1)
    sinf = jnp.concatenate([sin, sin, jnp.zeros((s, LANE - MLA_ROPE), F32)], axis=1)
    rot = np.zeros((LANE, LANE), np.float32)
    for j in range(half):
        rot[j + half, j] = -1.0
        rot[j, j + half] = 1.0
    return cosf, sinf, jnp.asarray(rot)


SMALL = ["ffn1_norm", "mix_norm", "q_a_norm", "kv_a_norm", "mla_q_norm", "mla_k_norm", "gla_b_gate", "gla_out_norm",
         "mem_attn_norm", "mem_norm", "mem_q_norm", "mem_k_norm", "ffn2_norm"]
BIG = ["ffn1_w_gate", "ffn1_w_up", "ffn1_w_down", "w_in", "w_q_up", "w_kv_up", "w_out", "mem_w_q", "mem_w_k",
       "mem_w_v", "mem_w_o", "ffn2_w_gate", "ffn2_w_up", "ffn2_w_down"]
COL_SHARDED = {"ffn1_w_gate", "ffn1_w_up", "w_in", "w_q_up", "w_kv_up", "gla_w_gate2", "mem_w_o", "ffn2_w_gate", "ffn2_w_up"}
WEIGHTS = ["ffn1_norm", "ffn1_w_gate", "ffn1_w_up", "ffn1_w_down", "mix_norm", "w_in", "q_a_norm", "w_q_up", "kv_a_norm",
           "w_kv_up", "mla_q_norm", "mla_k_norm", "gla_w_gate2", "gla_b_gate", "gla_out_norm", "w_out", "mem_attn_norm",
           "mem_norm", "mem_w_q", "mem_w_k", "mem_w_v", "mem_w_o", "mem_q_norm", "mem_k_norm", "ffn2_norm", "ffn2_w_gate",
           "ffn2_w_up", "ffn2_w_down"]


def _pack_small(vals, rows=8):
    flat = jnp.concatenate([v.reshape(-1).astype(F32) for v in vals])
    n = flat.shape[0]
    per = -(-n // (rows * LANE)) * LANE
    return jnp.pad(flat, (0, rows * per - n)).reshape(rows, per)


def _unpack_small(packed, shapes):
    flat = packed.reshape(-1)
    out, pos = [], 0
    for s in shapes:
        n = int(np.prod(s))
        out.append(flat[pos:pos + n].reshape(s))
        pos += n
    return out


FFN1 = ["ffn1_w_gate", "ffn1_w_up", "ffn1_w_down"]
FFN2 = ["ffn2_w_gate", "ffn2_w_up", "ffn2_w_down"]
SLOT_WEIGHTS = {"ffn1_w_gate", "ffn1_w_up", "ffn2_w_gate", "ffn2_w_up", "w_in"}
MID_A = ["w_in", "w_q_up", "w_kv_up", "gla_w_gate2"]
MID_B = ["w_out", "mem_w_q", "mem_w_k", "mem_w_v", "mem_w_o"]


def _with(res, comm):
    return res if comm is not None else (res, None)


def kernel(x, mem, positions, ffn1_norm, ffn1_w_gate, ffn1_w_up, ffn1_w_down, mix_norm, w_in, q_a_norm, w_q_up, kv_a_norm, w_kv_up, mla_q_norm, mla_k_norm, gla_w_gate2, gla_b_gate, gla_out_norm, w_out, mem_attn_norm, mem_norm, mem_w_q, mem_w_k, mem_w_v, mem_w_o, mem_q_norm, mem_k_norm, ffn2_norm, ffn2_w_gate, ffn2_w_up, ffn2_w_down, loss_target, m_ffn1_norm, m_ffn1_w_gate, m_ffn1_w_up, m_ffn1_w_down, m_mix_norm, m_w_in, m_q_a_norm, m_w_q_up, m_kv_a_norm, m_w_kv_up, m_mla_q_norm, m_mla_k_norm, m_gla_w_gate2, m_gla_b_gate, m_gla_out_norm, m_w_out, m_mem_attn_norm, m_mem_norm, m_mem_w_q, m_mem_w_k, m_mem_w_v, m_mem_w_o, m_mem_q_norm, m_mem_k_norm, m_ffn2_norm, m_ffn2_w_gate, m_ffn2_w_up, m_ffn2_w_down, v_ffn1_norm, v_ffn1_w_gate, v_ffn1_w_up, v_ffn1_w_down, v_mix_norm, v_w_in, v_q_a_norm, v_w_q_up, v_kv_a_norm, v_w_kv_up, v_mla_q_norm, v_mla_k_norm, v_gla_w_gate2, v_gla_b_gate, v_gla_out_norm, v_w_out, v_mem_attn_norm, v_mem_norm, v_mem_w_q, v_mem_w_k, v_mem_w_v, v_mem_w_o, v_mem_q_norm, v_mem_k_norm, v_ffn2_norm, v_ffn2_w_gate, v_ffn2_w_up, v_ffn2_w_down):
    args = dict(locals())
    two_d = lambda a: a[0] if a.ndim == 3 else a
    W = {n: two_d(args[n]) for n in WEIGHTS}
    M1 = {n: two_d(args["m_" + n]) for n in WEIGHTS}
    V2 = {n: two_d(args["v_" + n]) for n in WEIGHTS}
    xs, mems, tgt = x[0], mem[0], loss_target[0]
    S, D = xs.shape
    chip = 2 * lax.axis_index("x") + lax.axis_index("y")

    q_rank, kv_rank = W["w_q_up"].shape[0], W["w_kv_up"].shape[0]
    lay = InLayout(q_rank, kv_rank)
    off = lay.off
    shard16 = {n: W[n].astype(BF16) for n in BIG + ["gla_w_gate2"]}
    shard16["w_in"] = _pad_cols(shard16["w_in"], lay.shard_pad)
    full = {}

    def stage1(names):
        return gather_stage1([shard16[n] for n in names], [n != "gla_w_gate2" for n in names])

    def stage2(names, slots):
        return gather_stage2(slots, [shard16[n] for n in names], [n != "gla_w_gate2" for n in names])

    def finish(names, slots):
        for n, s in zip(names, slots):
            s = s.reshape((N_CHIPS,) + shard16[n].shape)
            if n in SLOT_WEIGHTS:
                full[n] = s
            else:
                full[n] = _slots_to_cols(s) if n in COL_SHARDED else s.reshape(-1, s.shape[2])

    token = {"last": None}

    def begin(comm, name, after=None):
        started = start_comm(comm, name=name, after=token["last"] if after is None else after)
        token["last"] = started[-1]
        return comm, started

    up_names, down_names = FFN1[:2], FFN1[2:]
    first = [begin(stage1([n]), f"gather_start_{n}") for n in up_names]
    gate_s1 = wait_comm(*first[0], token["last"], name=f"gather_wait_{up_names[0]}")
    finish(up_names[:1], run_comm(stage2(up_names[:1], gate_s1), name="pass_ffn1_gate"))
    up_s1 = wait_comm(*first[1], full[up_names[0]], name=f"gather_wait_{up_names[1]}")
    down1 = begin(stage1(down_names), "gather_start_ffn1_down", after=up_s1[0])
    mid_a_s1 = begin(stage1(MID_A), "gather_start_mid_a")
    finish(up_names[1:], run_comm(stage2(up_names[1:], up_s1), name="pass_ffn1_up"))
    cosf, sinf, rot = _rope_tables(positions[0])
    tri = jnp.asarray(np.tril(np.ones((CHUNK, CHUNK), np.float32)))
    gqn = W["mla_q_norm"][:, :MLA_NOPE]
    gqr = _pad_cols(W["mla_q_norm"][:, MLA_NOPE:], LANE)
    gkn = W["mla_k_norm"][:, :MLA_NOPE]
    gkr = _pad_cols(W["mla_k_norm"][:, MLA_NOPE:], LANE)
    HP = MLA_HEAD_PAD
    mla_scale = MLA_QK ** -0.5
    mem_scale = MEM_HEAD_DIM ** -0.5
    mla_w = MLA_HEADS * MLA_V
    gla_w = GLA_HEADS * GLA_DV
    mem_w = MEM_HEADS * MEM_HEAD_DIM

    n1 = row_fwd(rms_fn, [V(xs)], [W["ffn1_norm"]], [(D, BF16)], [(0, 0, 0)], name="ffn1_norm")[0]
    gate1, up1, act1 = ffn_up(n1, full["ffn1_w_gate"], full["ffn1_w_up"], name="ffn1_up", behind=token["last"])
    finish(down_names, run_comm(stage2(down_names, wait_comm(*down1, act1, name="gather_wait_ffn1_down")),
                                name="pass_ffn1_down"))
    mid_a1 = wait_comm(*mid_a_s1, act1, name="gather_wait_mid_a")
    mid_b = begin(stage1(MID_B), "gather_start_mid_b", after=mid_a1[0])
    x1, got = mm([(act1, full["ffn1_w_down"])], "nn", F32, alpha=0.5, res=xs, name="ffn1_down",
                 comm=stage2(MID_A, mid_a1), behind=token["last"])
    ffn1_saved = (n1, gate1, up1, act1)
    finish(MID_A, got)
    ffn2_s1 = [begin(stage1([n]), f"gather_start_{n}", after=x1 if n == FFN2[0] else None) for n in FFN2]
    w_q_up_p = _pad_q_up(full["w_q_up"])
    w_gate2_p = jnp.pad(full["gla_w_gate2"], ((0, LANE - GLA_GATE_RANK), (0, 0)))
    h = row_fwd(rms_fn, [V(x1)], [W["mix_norm"]], [(D, BF16)], [(0, 0, 0)], name="mix_norm")[0]
    mid_b1 = wait_comm(*mid_b, h, name="gather_wait_mid_b")
    z_shards, got = mm([(h, full["w_in"])], "nn", F32, name="in_proj", b_slots=True, comm=stage2(MID_B, mid_b1),
                       behind=token["last"])
    z = lay.from_shards(z_shards)
    finish(MID_B, got)
    qa = row_fwd(rms_fn, [V(z, off["zq"], q_rank)], [W["q_a_norm"]], [(q_rank, BF16)], [(0, 0, 0)], name="q_a_norm")[0]
    kva = row_fwd(rms_fn, [V(z, off["zkv"], kv_rank)], [W["kv_a_norm"]], [(kv_rank, BF16)], [(0, 0, 0)], name="kv_a_norm")[0]
    qraw = mm([(qa, w_q_up_p)], "nn", F32, name="q_up")
    kvraw = mm([(kva, full["w_kv_up"])], "nn", F32, name="kv_up")
    tabs = [V(cosf, diff=False), V(sinf, diff=False)]
    q_rows = [V(qraw, 0, LANE, HP), V(qraw, LANE, LANE, HP)] + tabs
    k_rows = [V(kvraw, 0, LANE, HP), V(z, off["zkr"], LANE, 0)] + tabs
    qh = row_fwd(qk_prep_fn, q_rows, [gqn, gqr, rot], [(MLA_HEADS * HP, BF16)], [(0, 0, HP), (0, LANE, HP)],
                 heads=MLA_HEADS, name="q_prep")[0]
    kh = row_fwd(qk_prep_fn, k_rows, [gkn, gkr, rot], [(MLA_HEADS * HP, BF16)], [(0, 0, HP), (0, LANE, HP)],
                 heads=MLA_HEADS, name="k_prep")[0]
    mla_kw = dict(heads=MLA_HEADS, dk=HP, dv=MLA_V, v_off=1, v_hs=2, scale=mla_scale, causal=True)
    o_mla = attn_fwd(qh, kh, kvraw, name="mla_attn", **mla_kw)

    zg = z[:, off["zg"]:off["zg"] + LANE]
    pre = mm([(zg, w_gate2_p)], "nn", F32, name="gla_gate")
    la = row_fwd(gate_fn, [V(pre)], [W["gla_b_gate"]], [(pre.shape[1], F32)], [(0, 0, 0)], name="gla_log_decay")[0]
    gla_kw = dict(q_off=off["gq"], k_off=off["gk"], v_off=off["gv"])
    o_raw, states = gla_fwd(z, la, tri, name="gla_scan", **gla_kw)
    gla_rows = [V(o_raw, 0, GLA_DV, GLA_DV), V(z, off["zr"], GLA_DV, GLA_DV)]
    o_gla = row_fwd(gla_out_fn, gla_rows, [W["gla_out_norm"]], [(gla_w, BF16)], [(0, 0, GLA_DV)], heads=GLA_HEADS,
                    name="gla_out")[0]
    o_cat = jnp.concatenate([o_mla, o_gla], axis=1)
    f2 = [wait_comm(*ffn2_s1[k], o_cat, name=f"gather_wait_{FFN2[k]}")[0] for k in range(2)]
    x2, got = mm([(o_cat, full["w_out"])], "nn", F32, res=x1, name="out_proj", comm=stage2(FFN2[:2], f2))
    finish(FFN2[:2], got)

    hm = row_fwd(rms_fn, [V(x2)], [W["mem_attn_norm"]], [(D, BF16)], [(0, 0, 0)], name="mem_attn_norm")[0]
    mn = row_fwd(rms_fn, [V(mems)], [W["mem_norm"]], [(D, BF16)], [(0, 0, 0)], name="mem_norm")[0]
    qm_raw = mm([(hm, full["mem_w_q"])], "nn", F32, name="mem_q")
    km_raw = mm([(mn, full["mem_w_k"])], "nn", F32, name="mem_k")
    vm = mm([(mn, full["mem_w_v"])], "nn", F32, name="mem_v")
    hd = MEM_HEAD_DIM
    qm = row_fwd(rms_fn, [V(qm_raw, 0, hd, hd)], [W["mem_q_norm"]], [(mem_w, BF16)], [(0, 0, hd)], heads=MEM_HEADS,
                 name="mem_q_norm")[0]
    km = row_fwd(rms_fn, [V(km_raw, 0, hd, hd)], [W["mem_k_norm"]], [(mem_w, BF16)], [(0, 0, hd)], heads=MEM_HEADS,
                 name="mem_k_norm")[0]
    mem_kw = dict(heads=MEM_HEADS, dk=hd, dv=hd, v_off=0, v_hs=1, scale=mem_scale, causal=False)
    om = attn_fwd(qm, km, vm, name="mem_attn", **mem_kw)
    x3 = mm([(om, full["mem_w_o"])], "nn", F32, res=x2, name="mem_o")

    n2 = row_fwd(rms_fn, [V(x3)], [W["ffn2_norm"]], [(D, BF16)], [(0, 0, 0)], name="ffn2_norm")[0]
    f2_down = wait_comm(*ffn2_s1[2], n2, name=f"gather_wait_{FFN2[2]}")
    (gate2, up2, act2), got = ffn_up(n2, full["ffn2_w_gate"], full["ffn2_w_up"], name="ffn2_up",
                                     comm=stage2(FFN2[2:], f2_down))
    finish(FFN2[2:], got)
    y = mm([(act2, full["ffn2_w_down"])], "nn", F32, alpha=0.5, res=x3, name="ffn2_down")
    dy, loss_part = loss_head(y, tgt, name="loss_head")
    G = {"loss": loss_part[:, :1]}

    chip_sum, reduced = {}, {}

    def to_halves(n):
        g = G[n]
        if n in SLOT_WEIGHTS:
            s = g
        else:
            s = _cols_to_slots(g) if n in COL_SHARDED else g.reshape(N_CHIPS, g.shape[0] // N_CHIPS, g.shape[1])
        return s.reshape(N_CHIPS, 2, s.shape[1] // 2, s.shape[2])

    def add2(names, halves, got):
        for n, a, b in zip(names, halves, got):
            chip_sum[n] = add_own_half(a, b, BF16, name=f"rs_add2_{n}")

    to_join = []

    def add4_join(names, parts):
        for n, p in zip(names, parts):
            to_join.append((n, sum_chip_parts(chip_sum[n], p, name=f"rs_add4_{n}")))

    def with_joins(comm):
        names, totals = [n for n, _ in to_join], [t for _, t in to_join]
        to_join.clear()
        if not names:
            return comm, lambda got: got
        own = 0 if comm is None else len(comm.out_shapes)
        joined = join_halves(totals)

        def split(got):
            for n, b in zip(names, got[own:]):
                reduced[n] = b.reshape(-1, b.shape[2])[:, :W[n].shape[1]]
            return got[:own]

        return (joined if comm is None else merge_comms(comm, joined)), split

    def flush_joins():
        comm, split = with_joins(None)
        if comm is not None:
            split(run_comm(comm, name=f"rs_join_{len(reduced)}"))

    in_flight = []

    def xchg_start(names):
        in_flight.append((names,) + begin(exchange_chips([chip_sum[n] for n in names]), f"xchg_start_{names[0]}"))

    def xchg_wait(after, count=1):
        for _ in range(count):
            names, comm, started = in_flight.pop(0)
            add4_join(names, wait_comm(comm, started, after, name=f"xchg_wait_{names[0]}"))

    def ffn_backward(dout, xin, tag, saved, dact_comm=None, after_dact=None):
        n_, gate, up, act = saved
        nd, ng, nu = f"{tag}_w_down", f"{tag}_w_gate", f"{tag}_w_up"
        (dgate, dup), got0 = _with(ffn_dact(dout, full[nd], gate, up, 0.5, name=f"{tag}_dact", comm=dact_comm,
                                            behind=token["last"]), dact_comm)
        if after_dact:
            after_dact(got0)
        G[nd] = mm([(act, dout)], "tn", F32, alpha=0.5, name=f"{tag}_dwd", tm=1408, tn=1024, behind=token["last"])
        hd_ = to_halves(nd)
        comm, split = with_joins(swap_halves([hd_]))
        G[ng], got = mm([(n_, dgate)], "tn", F32, name=f"{tag}_dwg", out_slots=True, tm=1024, tn=1408, rows_inner=True,
                        comm=comm)
        add2([nd], [hd_], split(got))
        xchg_start([nd])
        hg = to_halves(ng)
        G[nu], got_g = mm([(n_, dup)], "tn", F32, name=f"{tag}_dwu", out_slots=True, tm=1024, tn=1408, rows_inner=True,
                          comm=swap_halves([hg]), behind=token["last"])
        add2([ng], [hg], got_g)
        xchg_start([ng])
        hu = to_halves(nu)
        dn, got_u = mm([(dgate, full[ng]), (dup, full[nu])], "nt", F32, name=f"{tag}_dn", b_slots=True, tn=1024, tk=1408,
                       comm=swap_halves([hu]), behind=token["last"])
        add2([nu], [hu], got_u)
        xchg_start([nu])
        dx, G[f"{tag}_norm"] = row_bwd(rms_fn, [V(xin)], [W[f"{tag}_norm"]], [V(dn)], const_diff=[True], res=dout,
                                       name=f"{tag}_dnorm")
        return dx

    g3 = ffn_backward(dy, x3, "ffn2", (n2, gate2, up2, act2))
    xchg_wait(g3)

    d_om = mm([(g3, full["mem_w_o"])], "nt", F32, name="mem_o_dx", behind=token["last"])
    G["mem_w_o"] = mm([(om, g3)], "tn", F32, name="mem_o_dw")
    dqm, dkm, dvm = attn_bwd(qm, km, vm, d_om, name="mem_attn_bwd", **mem_kw)
    dqm_raw, G["mem_q_norm"] = row_bwd(rms_fn, [V(qm_raw, 0, hd, hd)], [W["mem_q_norm"]], [V(dqm, 0, hd, hd)],
                                       const_diff=[True], heads=MEM_HEADS, row_dtype=BF16, name="mem_q_norm_bwd")
    dkm_raw, G["mem_k_norm"] = row_bwd(rms_fn, [V(km_raw, 0, hd, hd)], [W["mem_k_norm"]], [V(dkm, 0, hd, hd)],
                                       const_diff=[True], heads=MEM_HEADS, row_dtype=BF16, name="mem_k_norm_bwd")
    dhm = mm([(dqm_raw, full["mem_w_q"])], "nt", F32, name="mem_q_dx")
    G["mem_w_q"] = mm([(hm, dqm_raw)], "tn", F32, name="mem_q_dw")
    dmn = mm([(dkm_raw, full["mem_w_k"]), (dvm, full["mem_w_v"])], "nt", F32, name="mem_kv_dx")
    G["mem_w_k"] = mm([(mn, dkm_raw)], "tn", F32, name="mem_k_dw")
    G["mem_w_v"] = mm([(mn, dvm)], "tn", F32, name="mem_v_dw")
    _, G["mem_norm"] = row_bwd(rms_fn, [V(mems)], [W["mem_norm"]], [V(dmn)], const_diff=[True], row_dtype=BF16,
                               name="mem_norm_bwd")
    g2, G["mem_attn_norm"] = row_bwd(rms_fn, [V(x2)], [W["mem_attn_norm"]], [V(dhm)], const_diff=[True], res=g3,
                                     name="mem_attn_norm_bwd")

    xchg_wait(g2, 2)

    d_ocat = mm([(g2, full["w_out"])], "nt", F32, name="out_proj_dx")
    G["w_out"] = mm([(o_cat, g2)], "tn", F32, name="out_proj_dw")

    d_oraw, d_zr, G["gla_out_norm"] = row_bwd(gla_out_fn, gla_rows, [W["gla_out_norm"]],
                                              [V(d_ocat, mla_w, GLA_DV, GLA_DV)], const_diff=[True], heads=GLA_HEADS,
                                              name="gla_out_bwd")
    mid_b_halves = [to_halves(n) for n in MID_B]
    comm, split = with_joins(swap_halves(mid_b_halves))
    (d_gq, d_gk, d_gv, d_la), got = gla_bwd(z, la, tri, tri.T, states, d_oraw, name="gla_scan_bwd", comm=comm, **gla_kw)
    add2(MID_B, mid_b_halves, split(got))
    xchg_start(MID_B)
    d_pre, G["gla_b_gate"] = row_bwd(gate_fn, [V(pre)], [W["gla_b_gate"]], [V(d_la)], const_diff=[True], row_dtype=BF16,
                                     name="gla_log_decay_bwd")
    d_zg = mm([(d_pre, w_gate2_p)], "nt", BF16, name="gla_gate_dx", behind=token["last"])
    G["gla_w_gate2"] = mm([(zg, d_pre)], "tn", F32, name="gla_gate_dw")[:GLA_GATE_RANK]

    comm, split = with_joins(None)
    (d_qh, d_kh, d_v), got = _with(attn_bwd(qh, kh, kvraw, d_ocat, name="mla_attn_bwd", comm=comm, **mla_kw), comm)
    split(got)
    cq = [V(d_qh, 0, LANE, HP), V(d_qh, LANE, LANE, HP)]
    ck = [V(d_kh, 0, LANE, HP), V(d_kh, LANE, LANE, HP)]
    d_qraw, d_gqn, d_gqr = row_bwd(qk_prep_fn, q_rows, [gqn, gqr, rot], cq, const_diff=[True, True, False],
                                   heads=MLA_HEADS, row_dtype=BF16, pack={0: (0, HP), 1: (LANE, HP)},
                                   pack_width=MLA_HEADS * HP, name="q_prep_bwd")
    d_kvraw, d_zkr, d_gkn, d_gkr = row_bwd(qk_prep_fn, k_rows, [gkn, gkr, rot], ck, const_diff=[True, True, False],
                                           heads=MLA_HEADS, row_dtype=BF16, pack={0: (0, HP)}, pack_width=MLA_HEADS * HP,
                                           fills=[(V(d_v, 0, MLA_V, MLA_V), LANE, HP)], name="k_prep_bwd")
    G["mla_q_norm"] = jnp.concatenate([d_gqn, d_gqr[:, :MLA_ROPE]], axis=1)
    G["mla_k_norm"] = jnp.concatenate([d_gkn, d_gkr[:, :MLA_ROPE]], axis=1)
    d_qa = mm([(d_qraw, w_q_up_p)], "nt", F32, name="q_up_dx")
    G["w_q_up"] = _unpad_q_up(mm([(qa, d_qraw)], "tn", F32, name="q_up_dw"))
    d_kva = mm([(d_kvraw, full["w_kv_up"])], "nt", F32, name="kv_up_dx")
    G["w_kv_up"] = mm([(kva, d_kvraw)], "tn", F32, name="kv_up_dw")
    d_zq, G["q_a_norm"] = row_bwd(rms_fn, [V(z, off["zq"], q_rank)], [W["q_a_norm"]], [V(d_qa)], const_diff=[True],
                                  row_dtype=BF16, name="q_a_norm_bwd")
    d_zkv, G["kv_a_norm"] = row_bwd(rms_fn, [V(z, off["zkv"], kv_rank)], [W["kv_a_norm"]], [V(d_kva)], const_diff=[True],
                                    row_dtype=BF16, name="kv_a_norm_bwd")

    seg = {"gv": d_gv, "zr": d_zr, "zq": d_zq, "gq": d_gq, "gk": d_gk, "zkv": d_zkv, "zkr": d_zkr, "zg": d_zg}
    dz = jnp.concatenate([_pad_cols(seg[n].astype(BF16), lay.size[n]) for n in lay.order], axis=1)
    xchg_wait(dz)
    comm, split = with_joins(None)
    dz_shards = lay.to_shards(dz)
    dh, got = _with(mm([(dz_shards, full["w_in"])], "nt", F32, name="in_proj_dx", b_slots=True, comm=comm), comm)
    split(got)
    G["w_in"] = mm([(h, dz_shards)], "tn", F32, name="in_proj_dw", out_slots=True)
    g1, G["mix_norm"] = row_bwd(rms_fn, [V(x1)], [W["mix_norm"]], [V(dh)], const_diff=[True], res=g2, name="mix_norm_bwd")

    mid_a = [n for n in MID_A if n != "gla_w_gate2"]
    mid_a_halves = [to_halves(n) for n in mid_a]

    def mid_a_sums(got):
        add2(mid_a, mid_a_halves, got)
        xchg_start(mid_a)

    gx = ffn_backward(g1, xs, "ffn1", ffn1_saved, dact_comm=swap_halves(mid_a_halves), after_dact=mid_a_sums)
    xchg_wait(gx, 2)

    grad, delta, new_m, new_v = {}, {}, {}, {}

    def adam_group(names, tag, behind=None):
        if any(n not in reduced for n in names):
            flush_joins()
        res = adamw([(W[n], reduced[n], M1[n], V2[n]) for n in names], name=f"adamw_{tag}", behind=behind)
        for n, (g_, d_, m_, v_) in zip(names, res):
            grad[n], delta[n], new_m[n], new_v[n] = g_, d_, m_, v_

    adam_group(FFN2, "ffn2", behind=token["last"])
    adam_group(mid_a + MID_B, "mid", behind=token["last"])
    xchg_wait(delta[MID_B[-1]], 2)
    adam_group(FFN1, "ffn1")

    small_names = SMALL + ["gla_w_gate2"]
    packed = small_names + ["loss"]
    small_sum = allreduce_small(_pack_small([G[n] for n in packed]), name="allreduce_small")
    small_g = dict(zip(packed, _unpack_small(small_sum, [G[n].shape for n in packed])))
    loss = small_g["loss"][0, 0]
    shard_c = W["gla_w_gate2"].shape[1]
    grad["gla_w_gate2"] = lax.dynamic_slice_in_dim(small_g["gla_w_gate2"], chip * shard_c, shard_c, axis=1)
    pw = _pack_small([W[n] for n in SMALL] + [W["gla_w_gate2"]])
    pg = _pack_small([small_g[n] for n in SMALL] + [grad["gla_w_gate2"]])
    pm = _pack_small([M1[n] for n in SMALL] + [M1["gla_w_gate2"]])
    pv = _pack_small([V2[n] for n in SMALL] + [V2["gla_w_gate2"]])
    (_, pd, pnm, pnv), = adamw([(pw, pg, pm, pv)], name="adamw_small")
    shapes = [W[n].shape for n in small_names]
    for n, d_, m_, v_ in zip(small_names, _unpack_small(pd, shapes), _unpack_small(pnm, shapes), _unpack_small(pnv, shapes)):
        delta[n], new_m[n], new_v[n] = d_, m_, v_
        if n != "gla_w_gate2":
            grad[n] = small_g[n]

    lead = lambda d: [d[n].reshape(args[n].shape) for n in WEIGHTS]
    return (loss, gx[None], *lead(grad), *lead(delta), *lead(new_m), *lead(new_v))
```

```python
import functools
import math

import numpy as np
import jax
import jax.numpy as jnp
from jax import lax
from jax.experimental import pallas as pl
from jax.experimental.pallas import tpu as pltpu

F32 = jnp.float32
BF16 = jnp.bfloat16
MXU_DTYPE = jnp.bfloat16
MESH = pl.DeviceIdType.MESH
ANY = pl.BlockSpec(memory_space=pl.ANY)

LANE = 128
EPS = 1e-6
CHUNK = 64
MLA_HEADS = 8
MLA_NOPE = 128
MLA_ROPE = 64
MLA_QK = MLA_NOPE + MLA_ROPE
MLA_V = 128
MLA_HEAD_PAD = 2 * LANE
ROPE_THETA = 10000.0
GLA_HEADS = 4
GLA_DK = 128
GLA_DV = 256
GLA_GATE_RANK = 16
GLA_TAU = 16.0
MEM_HEADS = 4
MEM_HEAD_DIM = 128
N_CHIPS = 4
N_DEV = 8

ADAM_LR = 0.001
ADAM_B1 = 0.9
ADAM_B2 = 0.999
ADAM_EPS = 1e-08
ADAM_WD = 0.01
ADAM_STEP = 10

VMEM_LIMIT = 56 * 1024 * 1024


def _cparams(sem=None):
    if sem is None:
        return pltpu.CompilerParams(vmem_limit_bytes=VMEM_LIMIT)
    return pltpu.CompilerParams(dimension_semantics=sem, vmem_limit_bytes=VMEM_LIMIT)


def _tile(dim, pref, unit=LANE):
    if dim <= pref:
        return dim
    t = (pref // unit) * unit
    while t > unit and dim % t:
        t -= unit
    assert dim % t == 0, (dim, pref, unit)
    return t


class Comm:
    def __init__(self, ins, out_shapes, nsem, start, wait, aliases=None):
        self.ins, self.out_shapes, self.nsem = list(ins), list(out_shapes), nsem
        self.start, self.wait, self.aliases = start, wait, dict(aliases or {})


def merge_comms(a, b):
    ai, ao = len(a.ins), len(a.out_shapes)

    def start(ins, outs, send, recv, base):
        a.start(ins[:ai], outs[:ao], send, recv, base)
        b.start(ins[ai:], outs[ao:], send, recv, base + a.nsem)

    def wait(ins, outs, send, recv, base):
        a.wait(ins[:ai], outs[:ao], send, recv, base)
        b.wait(ins[ai:], outs[ao:], send, recv, base + a.nsem)

    aliases = dict(a.aliases)
    aliases.update({ai + i: ao + o for i, o in b.aliases.items()})
    return Comm(a.ins + b.ins, a.out_shapes + b.out_shapes, a.nsem + b.nsem, start, wait, aliases)


def run_comm(comm, *, name):
    ni, no = len(comm.ins), len(comm.out_shapes)

    def body(*refs):
        ins, outs = refs[:ni], refs[ni:ni + no]
        send, recv = refs[ni + no:]
        comm.start(ins, outs, send, recv, 0)
        comm.wait(ins, outs, send, recv, 0)

    return pl.pallas_call(
        body, name=name, in_specs=[ANY] * ni, out_specs=[ANY] * no, out_shape=comm.out_shapes,
        input_output_aliases=comm.aliases,
        scratch_shapes=[pltpu.SemaphoreType.DMA((comm.nsem,)), pltpu.SemaphoreType.DMA((comm.nsem,))])(*comm.ins)


HBM = pl.BlockSpec(memory_space=pltpu.HBM)
SEM = pl.BlockSpec(memory_space=pltpu.SEMAPHORE)


def start_comm(comm, *, name, after=None):
    assert not comm.aliases
    ni, no = len(comm.ins), len(comm.out_shapes)
    tail = [] if after is None else [after]

    def body(*refs):
        srcs, lands = refs[:ni], refs[ni:ni + no]
        send, recv = refs[ni + no + len(tail)], refs[ni + no + len(tail) + 1]
        token = refs[-1]
        comm.start(srcs, lands, send, recv, 0)
        token[...] = jnp.zeros_like(token)

    through = [pltpu.HBM(a.shape, a.dtype) for a in comm.ins] + [pltpu.HBM(s.shape, s.dtype) for s in comm.out_shapes]
    ops = [pltpu.with_memory_space_constraint(a, pltpu.HBM) for a in comm.ins]
    ops += [pltpu.with_memory_space_constraint(lax.empty(s.shape, s.dtype), pltpu.HBM) for s in comm.out_shapes]
    ops += tail
    res = pl.pallas_call(
        body, name=name, in_specs=[HBM] * (ni + no) + [ANY] * len(tail),
        out_shape=[pltpu.SemaphoreType.DMA((comm.nsem,)), pltpu.SemaphoreType.DMA((comm.nsem,))] + through
        + [jax.ShapeDtypeStruct((8, LANE), F32)],
        out_specs=[SEM, SEM] + [HBM] * (ni + no) + [pl.BlockSpec(memory_space=pltpu.VMEM)],
        input_output_aliases={i: 2 + i for i in range(ni + no)},
        compiler_params=pltpu.CompilerParams(has_side_effects=pltpu.SideEffectType.DATAFLOW_SIDE_EFFECTING))(*ops)
    return res[0], res[1], list(res[2:2 + ni]), list(res[2 + ni:2 + ni + no]), res[-1]


def wait_comm(comm, started, after, *, name):
    send, recv, srcs, lands, _ = started
    ni, no = len(srcs), len(lands)

    def body(*refs):
        comm.wait(refs[:ni], refs[ni:ni + no], refs[ni + no], refs[ni + no + 1], 0)

    res = pl.pallas_call(
        body, name=name, in_specs=[HBM] * (ni + no) + [SEM, SEM, ANY],
        out_shape=[pltpu.HBM(a.shape, a.dtype) for a in srcs + lands], out_specs=[HBM] * (ni + no),
        input_output_aliases={i: i for i in range(ni + no)},
        compiler_params=pltpu.CompilerParams(has_side_effects=pltpu.SideEffectType.DATAFLOW_SIDE_EFFECTING),
    )(*srcs, *lands, send, recv, after)
    return list(res[ni:])


def _pcall(body, ops, *, name, grid, in_specs, out_specs, out_shape, sem, scratch_shapes=(), comm=None, behind=None):
    if behind is not None:
        n_real, inner = len(ops), body
        ops, in_specs = list(ops) + [behind], list(in_specs) + [ANY]

        def body(*refs):
            inner(*refs[:n_real], *refs[n_real + 1:])

    if comm is None:
        return pl.pallas_call(body, name=name, grid=grid, in_specs=in_specs, out_specs=out_specs, out_shape=out_shape,
                              scratch_shapes=list(scratch_shapes), compiler_params=_cparams(sem))(*ops)
    multi = isinstance(out_shape, (list, tuple))
    k_out_shape = list(out_shape) if multi else [out_shape]
    k_out_specs = list(out_specs) if multi else [out_specs]
    nki, nko, nks = len(ops), len(k_out_shape), len(scratch_shapes)
    nci, nco = len(comm.ins), len(comm.out_shapes)

    def wrapped(*refs):
        p = 0
        k_in = refs[p:p + nki]; p += nki
        c_in = refs[p:p + nci]; p += nci
        k_out = refs[p:p + nko]; p += nko
        c_out = refs[p:p + nco]; p += nco
        k_scr = refs[p:p + nks]; p += nks
        send, recv = refs[p:]
        first = pl.program_id(0) == 0
        last = pl.program_id(0) == grid[0] - 1
        for a in range(1, len(grid)):
            first = jnp.logical_and(first, pl.program_id(a) == 0)
            last = jnp.logical_and(last, pl.program_id(a) == grid[a] - 1)

        @pl.when(first)
        def _():
            comm.start(c_in, c_out, send, recv, 0)

        body(*k_in, *k_out, *k_scr)

        @pl.when(last)
        def _():
            comm.wait(c_in, c_out, send, recv, 0)

    res = pl.pallas_call(
        wrapped, name=name, grid=grid, in_specs=list(in_specs) + [ANY] * nci, out_specs=k_out_specs + [ANY] * nco,
        out_shape=k_out_shape + comm.out_shapes,
        input_output_aliases={nki + i: nko + o for i, o in comm.aliases.items()},
        scratch_shapes=list(scratch_shapes) + [pltpu.SemaphoreType.DMA((comm.nsem,)), pltpu.SemaphoreType.DMA((comm.nsem,))],
        compiler_params=_cparams(("arbitrary",) * len(grid)))(*ops, *comm.ins)
    k_res = list(res[:nko]) if multi else res[0]
    return k_res, list(res[nko:])


_DIMS = {"nn": (((1,), (0,)), ((), ())), "nt": (((1,), (1,)), ((), ())), "tn": (((0,), (0,)), ((), ()))}


def _blockspec(shape, index, rows_inner):
    return pl.BlockSpec(shape, (lambda j, i, k: index(i, j, k)) if rows_inner else index)


def mm(pairs, mode, out_dtype, *, name, alpha=1.0, res=None, tm=1024, tn=1024, tk=4096, b_slots=False, out_slots=False,
       rows_inner=False, comm=None, behind=None):
    a0, b0 = pairs[0]
    if b_slots:
        b_rows, b_cols = b0.shape[1], N_CHIPS * b0.shape[2]
    else:
        b_rows, b_cols = b0.shape
    (M, K) = a0.shape[::-1] if mode == "tn" else a0.shape
    N = b_rows if mode == "nt" else b_cols
    shard = (b_cols if b_slots else N) // N_CHIPS
    tm = _tile(M, tm)
    tn = _tile(shard if (out_slots or (b_slots and mode != "nt")) else N, tn)
    tk = _tile(shard if (b_slots and mode == "nt") else K, tk)
    nk = K // tk
    npairs = len(pairs)
    dims = _DIMS[mode]
    spec = functools.partial(_blockspec, rows_inner=rows_inner)
    if mode == "tn":
        a_spec = spec((tk, tm), lambda i, j, k: (k, i))
    else:
        a_spec = spec((tm, tk), lambda i, j, k: (i, k))
    per = shard // (tk if mode == "nt" else tn)
    if mode == "nt":
        b_spec = (spec((None, tn, tk), lambda i, j, k: (k // per, j, k % per)) if b_slots else
                  spec((tn, tk), lambda i, j, k: (j, k)))
    else:
        b_spec = (spec((None, tk, tn), lambda i, j, k: (j // per, k, j % per)) if b_slots else
                  spec((tk, tn), lambda i, j, k: (k, j)))
    if out_slots:
        assert res is None and mode != "nt"
        o_spec = spec((None, tm, tn), lambda i, j, k: (j // per, i, j % per))
        out_sds = jax.ShapeDtypeStruct((N_CHIPS, M, shard), out_dtype)
    else:
        o_spec = spec((tm, tn), lambda i, j, k: (i, j))
        out_sds = jax.ShapeDtypeStruct((M, N), out_dtype)
    has_res = res is not None

    def body(*refs):
        ab = refs[:2 * npairs]
        res_ref = refs[2 * npairs] if has_res else None
        o_ref = refs[2 * npairs + int(has_res)]

        def products():
            r = None
            for p in range(npairs):
                d = lax.dot_general(ab[2 * p][...].astype(MXU_DTYPE), ab[2 * p + 1][...].astype(MXU_DTYPE), dims,
                                    preferred_element_type=F32)
                r = d if r is None else r + d
            return r

        def finish(r):
            if alpha != 1.0:
                r = r * alpha
            if has_res:
                r = res_ref[...].astype(F32) + r
            o_ref[...] = r.astype(out_dtype)

        if nk == 1:
            finish(products())
            return
        acc = refs[-1]
        k = pl.program_id(2)

        @pl.when(k == 0)
        def _():
            acc[...] = jnp.zeros_like(acc)

        acc[...] += products()

        @pl.when(k == nk - 1)
        def _():
            finish(acc[...])

    ops, specs = [], []
    for a, b in pairs:
        ops += [a, b]
        specs += [a_spec, b_spec]
    if has_res:
        ops.append(res)
        specs.append(o_spec)
    blocks = (N // tn, M // tm) if rows_inner else (M // tm, N // tn)
    return _pcall(body, ops, name=name, grid=blocks + (nk,), in_specs=specs, out_specs=o_spec, out_shape=out_sds,
                  scratch_shapes=[pltpu.VMEM((tm, tn), F32)] if nk > 1 else [],
                  sem=("parallel", "parallel", "arbitrary"), comm=comm, behind=behind)


def _sigmoid(x):
    return 1.0 / (1.0 + jnp.exp(-x))


def ffn_up(n, wg, wu, *, name, tm=512, tn=1408, comm=None, behind=None):
    M, K = n.shape
    shard = wg.shape[2]
    N = N_CHIPS * shard
    tm, tn = _tile(M, tm), _tile(shard, tn)
    per = shard // tn
    w_spec = pl.BlockSpec((None, K, tn), lambda j, i: (j // per, 0, j % per))

    def body(n_ref, wg_ref, wu_ref, g_ref, u_ref, a_ref):
        nv = n_ref[...].astype(MXU_DTYPE)
        g = jnp.dot(nv, wg_ref[...].astype(MXU_DTYPE), preferred_element_type=F32)
        u = jnp.dot(nv, wu_ref[...].astype(MXU_DTYPE), preferred_element_type=F32)
        g_ref[...] = g.astype(g_ref.dtype)
        u_ref[...] = u.astype(u_ref.dtype)
        a_ref[...] = (g * _sigmoid(g) * u).astype(a_ref.dtype)

    o_spec = pl.BlockSpec((tm, tn), lambda j, i: (i, j))
    sds = jax.ShapeDtypeStruct((M, N), BF16)
    return _pcall(
        body, [n, wg, wu], name=name, grid=(N // tn, M // tm),
        in_specs=[pl.BlockSpec((tm, K), lambda j, i: (i, 0)), w_spec, w_spec],
        out_specs=[o_spec, o_spec, o_spec], out_shape=[sds, sds, sds], sem=("parallel", "parallel"), comm=comm,
        behind=behind)


def ffn_dact(dy, wd, gate, up, alpha, *, name, tm=512, tn=1408, comm=None, behind=None):
    M, K = dy.shape
    N = wd.shape[0]
    tm, tn = _tile(M, tm), _tile(N, tn)

    def body(dy_ref, wd_ref, g_ref, u_ref, dg_ref, du_ref):
        da = lax.dot_general(dy_ref[...].astype(MXU_DTYPE), wd_ref[...].astype(MXU_DTYPE), _DIMS["nt"],
                             preferred_element_type=F32) * alpha
        g = g_ref[...].astype(F32)
        u = u_ref[...].astype(F32)
        s = _sigmoid(g)
        du_ref[...] = (da * (g * s)).astype(du_ref.dtype)
        dg_ref[...] = (da * u * (s * (1.0 + g * (1.0 - s)))).astype(dg_ref.dtype)

    o_spec = pl.BlockSpec((tm, tn), lambda j, i: (i, j))
    sds = jax.ShapeDtypeStruct((M, N), BF16)
    return _pcall(
        body, [dy, wd, gate, up], name=name, grid=(N // tn, M // tm),
        in_specs=[pl.BlockSpec((tm, K), lambda j, i: (i, 0)), pl.BlockSpec((tn, K), lambda j, i: (j, 0)), o_spec, o_spec],
        out_specs=[o_spec, o_spec], out_shape=[sds, sds], sem=("parallel", "parallel"), comm=comm, behind=behind)


def _window(width, off, ext):
    ww = LANE
    while ww < width:
        if ww >= ext and off // ww == (off + ext - 1) // ww and width % ww == 0:
            break
        ww *= 2
    else:
        ww = width
    return ww, off // ww, off - (off // ww) * ww


class V:
    def __init__(self, arr, off=0, w=None, hs=0, diff=True):
        self.arr, self.off, self.hs, self.diff = arr, off, hs, diff
        self.w = arr.shape[1] - off if w is None else w

    def window(self, heads, tr):
        ww, blk, inner = _window(self.arr.shape[1], self.off, (heads - 1) * self.hs + self.w)
        return pl.BlockSpec((tr, ww), lambda i, blk=blk: (i, blk)), inner


def _const_spec(c):
    return pl.BlockSpec(c.shape, lambda i: (0, 0))


def row_fwd(fn, rows, consts, outs, out_map, *, heads=1, tr=256, name):
    S = rows[0].arr.shape[0]
    tr = _tile(S, tr, 8)
    wins = [v.window(heads, tr) for v in rows]
    nr, nc = len(rows), len(consts)

    def body(*refs):
        row_refs, const_refs, out_refs = refs[:nr], refs[nr:nr + nc], refs[nr + nc:]
        cv = [c[...].astype(F32) for c in const_refs]
        for h in range(heads):
            rv = []
            for v, (_, io), r in zip(rows, wins, row_refs):
                lo = io + h * v.hs
                rv.append(r[:, lo:lo + v.w].astype(F32))
            res = fn(*rv, *cv)
            for (ai, off, hs), o in zip(out_map, res):
                lo = off + h * hs
                out_refs[ai][:, lo:lo + o.shape[1]] = o.astype(out_refs[ai].dtype)

    return pl.pallas_call(
        body, name=name, grid=(S // tr,),
        in_specs=[w[0] for w in wins] + [_const_spec(c) for c in consts],
        out_specs=[pl.BlockSpec((tr, w), lambda i: (i, 0)) for w, _ in outs],
        out_shape=[jax.ShapeDtypeStruct((S, w), d) for w, d in outs],
        compiler_params=_cparams(("parallel",)))(*[v.arr for v in rows], *consts)


def row_bwd(fn, rows, consts, cots, *, const_diff, heads=1, tr=256, res=None, row_dtype=F32, pack=None, pack_width=0,
            fills=(), name):
    S = rows[0].arr.shape[0]
    tr = _tile(S, tr, 8)
    pack = dict(pack or {})
    nr, nc, nct, nf = len(rows), len(consts), len(cots), len(fills)
    wins = [v.window(heads, tr) for v in rows]
    cwins = [v.window(heads, tr) for v in cots]
    fwins = [v.window(heads, tr) for v, _, _ in fills]
    drows = [k for k, v in enumerate(rows) if v.diff]
    dconsts = [k for k in range(nc) if const_diff[k]]
    has_res = res is not None
    assert not (has_res and 0 in pack)
    widths = [pack_width] if pack else []
    place = []
    for n, k in enumerate(drows):
        if n in pack:
            place.append((0,) + tuple(pack[n]))
        else:
            place.append((len(widths), 0, rows[k].w))
            widths.append(rows[k].w * (heads if rows[k].hs else 1))

    def body(*refs):
        row_refs = refs[:nr]
        const_refs = refs[nr:nr + nc]
        cot_refs = refs[nr + nc:nr + nc + nct]
        p = nr + nc + nct
        fill_refs = refs[p:p + nf]
        p += nf
        res_ref = refs[p] if has_res else None
        p += int(has_res)
        grow_refs = refs[p:p + len(widths)]
        gconst_refs = refs[p + len(widths):]
        i = pl.program_id(0)
        cv = [c[...].astype(F32) for c in const_refs]
        shared = [None] * len(drows)
        gc_sum = [None] * len(dconsts)
        for h in range(heads):
            rv = []
            for v, (_, io), r in zip(rows, wins, row_refs):
                lo = io + h * v.hs
                rv.append(r[:, lo:lo + v.w].astype(F32))
            ct = []
            for v, (_, io), r in zip(cots, cwins, cot_refs):
                lo = io + h * v.hs
                ct.append(r[:, lo:lo + v.w].astype(F32))

            def closed(*d):
                rr, cc = list(rv), list(cv)
                for k, val in zip(drows, d[:len(drows)]):
                    rr[k] = val
                for k, val in zip(dconsts, d[len(drows):]):
                    cc[k] = val
                return tuple(fn(*rr, *cc))

            _, vjp = jax.vjp(closed, *[rv[k] for k in drows], *[cv[k] for k in dconsts])
            grads = vjp(tuple(ct))
            for n, k in enumerate(drows):
                g = grads[n]
                if rows[k].hs == 0 and heads > 1:
                    shared[n] = g if shared[n] is None else shared[n] + g
                else:
                    if n == 0 and has_res:
                        g = g + res_ref[:, h * rows[k].w:(h + 1) * rows[k].w].astype(F32)
                    out, off, hs = place[n]
                    grow_refs[out][:, off + h * hs:off + h * hs + rows[k].w] = g.astype(row_dtype)
            for (v, off, hs), (_, io), r in zip(fills, fwins, fill_refs):
                lo = io + h * v.hs
                grow_refs[0][:, off + h * hs:off + h * hs + v.w] = r[:, lo:lo + v.w].astype(row_dtype)
            for n in range(len(dconsts)):
                g = grads[len(drows) + n]
                gc_sum[n] = g if gc_sum[n] is None else gc_sum[n] + g
        for n, k in enumerate(drows):
            if shared[n] is not None:
                g = shared[n]
                if n == 0 and has_res:
                    g = g + res_ref[...].astype(F32)
                grow_refs[place[n][0]][...] = g.astype(row_dtype)

        @pl.when(i == 0)
        def _():
            for n in range(len(dconsts)):
                gconst_refs[n][...] = gc_sum[n]

        @pl.when(i > 0)
        def _():
            for n in range(len(dconsts)):
                gconst_refs[n][...] += gc_sum[n]

    in_specs = [w[0] for w in wins] + [_const_spec(c) for c in consts] + [w[0] for w in cwins] + [w[0] for w in fwins]
    ops = [v.arr for v in rows] + list(consts) + [v.arr for v in cots] + [v.arr for v, _, _ in fills]
    if has_res:
        in_specs.append(pl.BlockSpec((tr, widths[0]), lambda i: (i, 0)))
        ops.append(res)
    out_specs = [pl.BlockSpec((tr, w), lambda i: (i, 0)) for w in widths]
    out_shape = [jax.ShapeDtypeStruct((S, w), row_dtype) for w in widths]
    for k in dconsts:
        out_specs.append(_const_spec(consts[k]))
        out_shape.append(jax.ShapeDtypeStruct(consts[k].shape, F32))
    return pl.pallas_call(body, name=name, grid=(S // tr,), in_specs=in_specs, out_specs=out_specs,
                          out_shape=out_shape, compiler_params=_cparams(("arbitrary",)))(*ops)


def _rms(x, g, n=None):
    n = x.shape[-1] if n is None else n
    ms = jnp.sum(x * x, axis=-1, keepdims=True) * (1.0 / n)
    return x * lax.rsqrt(ms + EPS) * g


def rms_fn(x, g):
    return (_rms(x, g),)


def qk_prep_fn(nope, rope, cos, sin, gn, gr, rot):
    ms = (jnp.sum(nope * nope, axis=-1, keepdims=True) + jnp.sum(rope * rope, axis=-1, keepdims=True)) * (1.0 / MLA_QK)
    r = lax.rsqrt(ms + EPS)
    on = nope * r * gn
    orr = rope * r * gr
    turned = jnp.dot(orr, rot, precision=lax.Precision.HIGHEST, preferred_element_type=F32)
    return on, orr * cos + turned * sin


def gla_out_fn(o, zr, g):
    return (_rms(o, g) * (zr * _sigmoid(zr)),)


def gate_fn(pre, b):
    t = pre + b
    return ((jnp.minimum(t, 0.0) - jnp.log(1.0 + jnp.exp(-jnp.abs(t)))) * (1.0 / GLA_TAU),)


def _attn_probs(q_ref, k_ref, scale, q0, kext):
    s = lax.dot_general(q_ref[...].astype(MXU_DTYPE), k_ref[0:kext, :].astype(MXU_DTYPE), _DIMS["nt"],
                        preferred_element_type=F32) * scale
    if q0 is not None:
        qc = (q0 + lax.broadcasted_iota(jnp.int32, s.shape, 0)) // CHUNK
        kc = lax.broadcasted_iota(jnp.int32, s.shape, 1) // CHUNK
        s = jnp.where(kc <= qc, s, -1e30)
    m = jnp.max(s, axis=-1, keepdims=True)
    e = jnp.exp(s - m)
    return e / jnp.sum(e, axis=-1, keepdims=True)


def _per_query_block(one, causal, nq, tq, Sk):
    if not causal:
        one(None, Sk, None)
        return
    assert tq % CHUNK == 0
    for ib in range(nq):
        pl.when(pl.program_id(1) == ib)(functools.partial(one, ib * tq, min(Sk, (ib + 1) * tq), ib))


def attn_fwd(q, k, v, *, heads, dk, dv, v_off, v_hs, scale, causal, name, tq=256, comm=None):
    Sq, Sk = q.shape[0], k.shape[0]
    tq = _tile(Sq, tq, 8)

    def body(q_ref, k_ref, v_ref, o_ref):
        def one(q0, kext, ib):
            p = _attn_probs(q_ref, k_ref, scale, q0, kext)
            o_ref[...] = jnp.dot(p.astype(MXU_DTYPE), v_ref[0:kext, :].astype(MXU_DTYPE),
                                 preferred_element_type=F32).astype(o_ref.dtype)

        _per_query_block(one, causal, Sq // tq, tq, Sk)

    return _pcall(
        body, [q, k, v], name=name, grid=(heads, Sq // tq),
        in_specs=[pl.BlockSpec((tq, dk), lambda h, i: (i, h)), pl.BlockSpec((Sk, dk), lambda h, i: (0, h)),
                  pl.BlockSpec((Sk, dv), lambda h, i: (0, v_off + h * v_hs))],
        out_specs=pl.BlockSpec((tq, dv), lambda h, i: (i, h)),
        out_shape=jax.ShapeDtypeStruct((Sq, heads * dv), BF16), sem=("parallel", "parallel"), comm=comm)


def attn_bwd(q, k, v, do, *, heads, dk, dv, v_off, v_hs, scale, causal, name, tq=256, comm=None):
    Sq, Sk = q.shape[0], k.shape[0]
    tq = _tile(Sq, tq, 8)

    def body(q_ref, k_ref, v_ref, do_ref, dq_ref, dk_ref, dv_ref):
        @pl.when(pl.program_id(1) == 0)
        def _():
            dk_ref[...] = jnp.zeros_like(dk_ref)
            dv_ref[...] = jnp.zeros_like(dv_ref)

        def one(q0, kext, ib):
            p = _attn_probs(q_ref, k_ref, scale, q0, kext)
            dob = do_ref[...].astype(MXU_DTYPE)
            dp = lax.dot_general(dob, v_ref[0:kext, :].astype(MXU_DTYPE), _DIMS["nt"], preferred_element_type=F32)
            delta = jnp.sum(p * dp, axis=-1, keepdims=True)
            ds = (p * (dp - delta) * scale).astype(MXU_DTYPE)
            dq_ref[...] = jnp.dot(ds, k_ref[0:kext, :].astype(MXU_DTYPE), preferred_element_type=F32)
            dk_ref[0:kext, :] += lax.dot_general(ds, q_ref[...].astype(MXU_DTYPE), _DIMS["tn"],
                                                 preferred_element_type=F32)
            dv_ref[0:kext, :] += lax.dot_general(p.astype(MXU_DTYPE), dob, _DIMS["tn"], preferred_element_type=F32)

        _per_query_block(one, causal, Sq // tq, tq, Sk)

    return _pcall(
        body, [q, k, v, do], name=name, grid=(heads, Sq // tq),
        in_specs=[pl.BlockSpec((tq, dk), lambda h, i: (i, h)), pl.BlockSpec((Sk, dk), lambda h, i: (0, h)),
                  pl.BlockSpec((Sk, dv), lambda h, i: (0, v_off + h * v_hs)),
                  pl.BlockSpec((tq, dv), lambda h, i: (i, h))],
        out_specs=[pl.BlockSpec((tq, dk), lambda h, i: (i, h)), pl.BlockSpec((Sk, dk), lambda h, i: (0, h)),
                   pl.BlockSpec((Sk, dv), lambda h, i: (0, h))],
        out_shape=[jax.ShapeDtypeStruct((Sq, heads * dk), F32), jax.ShapeDtypeStruct((Sk, heads * dk), F32),
                   jax.ShapeDtypeStruct((Sk, heads * dv), F32)],
        sem=("parallel", "arbitrary"), comm=comm)


def _gla_chunk(k, g, tri_ref):
    b = jnp.dot(tri_ref[...], g, precision=lax.Precision.HIGHEST, preferred_element_type=F32)
    b_end = jnp.sum(g, axis=0, keepdims=True)
    e = jnp.exp(b_end - b)
    return k * e, e, jnp.exp(b_end)


def _gla_windows(z, q_off, k_off, v_off, rows_of):
    H, DK, DV = GLA_HEADS, GLA_DK, GLA_DV
    specs, inner = [], []
    for off, ext in ((q_off, H * DK), (k_off, H * DK), (v_off, H * DV)):
        ww, blk, io = _window(z.shape[1], off, ext)
        specs.append(pl.BlockSpec((CHUNK, ww), lambda c, blk=blk: (rows_of(c), blk)))
        inner.append(io)
    return specs, inner


def gla_fwd(z, la, tri, *, q_off, k_off, v_off, name, comm=None):
    S = z.shape[0]
    nchunk = S // CHUNK
    H, DK, DV = GLA_HEADS, GLA_DK, GLA_DV
    qscale = DK ** -0.5
    zspecs, (qi, ki, vi) = _gla_windows(z, q_off, k_off, v_off, lambda c: c)

    def body(q_ref, k_ref, v_ref, la_ref, tri_ref, o_ref, st_ref, state):
        @pl.when(pl.program_id(0) == 0)
        def _():
            state[...] = jnp.zeros_like(state)

        for h in range(H):
            dks, dvs = slice(h * DK, (h + 1) * DK), slice(h * DV, (h + 1) * DV)
            k = k_ref[:, ki + h * DK:ki + (h + 1) * DK].astype(F32)
            v = v_ref[:, vi + h * DV:vi + (h + 1) * DV]
            q = q_ref[:, qi + h * DK:qi + (h + 1) * DK].astype(F32)
            kdec, _, decay = _gla_chunk(k, la_ref[:, dks].astype(F32), tri_ref)
            ut = lax.dot_general(v.astype(MXU_DTYPE), kdec.astype(MXU_DTYPE), _DIMS["tn"], preferred_element_type=F32)
            new = state[h] * decay + ut
            state[h] = new
            st_ref[h] = new
            qs = (q * qscale).astype(MXU_DTYPE)
            o_ref[:, dvs] = lax.dot_general(qs, new.astype(MXU_DTYPE), _DIMS["nt"], preferred_element_type=F32)

    return _pcall(
        body, [z, z, z, la, tri], name=name, grid=(nchunk,),
        in_specs=zspecs + [pl.BlockSpec((CHUNK, H * DK), lambda c: (c, 0)), pl.BlockSpec((CHUNK, CHUNK), lambda c: (0, 0))],
        out_specs=[pl.BlockSpec((CHUNK, H * DV), lambda c: (c, 0)),
                   pl.BlockSpec((H, None, DV, DK), lambda c: (0, c, 0, 0))],
        out_shape=[jax.ShapeDtypeStruct((S, H * DV), F32), jax.ShapeDtypeStruct((H, nchunk, DV, DK), F32)],
        scratch_shapes=[pltpu.VMEM((H, DV, DK), F32)], sem=("arbitrary",), comm=comm)


def gla_bwd(z, la, tri, trit, states, do, *, q_off, k_off, v_off, name, comm=None):
    S = z.shape[0]
    nchunk = S // CHUNK
    H, DK, DV = GLA_HEADS, GLA_DK, GLA_DV
    qscale = DK ** -0.5
    last = nchunk - 1
    zspecs, (qi, ki, vi) = _gla_windows(z, q_off, k_off, v_off, lambda c: last - c)

    def body(q_ref, k_ref, v_ref, la_ref, tri_ref, trit_ref, st_ref, sp_ref, do_ref, dq_ref, dk_ref, dv_ref, dla_ref,
             dstate):
        c = pl.program_id(0)
        cc = last - c

        @pl.when(c == 0)
        def _():
            dstate[...] = jnp.zeros_like(dstate)

        for h in range(H):
            dks, dvs = slice(h * DK, (h + 1) * DK), slice(h * DV, (h + 1) * DV)
            kf = k_ref[:, ki + h * DK:ki + (h + 1) * DK].astype(F32)
            vb16 = v_ref[:, vi + h * DV:vi + (h + 1) * DV].astype(MXU_DTYPE)
            q = q_ref[:, qi + h * DK:qi + (h + 1) * DK].astype(F32)
            kdec, e, decay = _gla_chunk(kf, la_ref[:, dks].astype(F32), tri_ref)
            dob = do_ref[:, dvs].astype(MXU_DTYPE)
            stb = st_ref[h].astype(MXU_DTYPE)
            qs = (q * qscale).astype(MXU_DTYPE)
            dq_ref[:, dks] = jnp.dot(dob, stb, preferred_element_type=F32) * qscale
            dst = dstate[h] + lax.dot_general(dob, qs, _DIMS["tn"], preferred_element_type=F32)
            prev = jnp.where(cc > 0, sp_ref[h], 0.0)
            ddecay = jnp.sum(dst * prev, axis=0, keepdims=True)
            dstate[h] = dst * decay
            dub = dst.astype(MXU_DTYPE)
            dv_ref[:, dvs] = lax.dot_general(kdec.astype(MXU_DTYPE), dub, _DIMS["nt"], preferred_element_type=F32)
            dkdec = jnp.dot(vb16, dub, preferred_element_type=F32)
            dk_ref[:, dks] = dkdec * e
            w = dkdec * kf * e
            db_end = jnp.sum(w, axis=0, keepdims=True) + ddecay * decay
            dla_ref[:, dks] = db_end - jnp.dot(trit_ref[...], w, precision=lax.Precision.HIGHEST,
                                               preferred_element_type=F32)

    def rows(width):
        return pl.BlockSpec((CHUNK, width), lambda c: (last - c, 0))

    square = pl.BlockSpec((CHUNK, CHUNK), lambda c: (0, 0))
    return _pcall(
        body, [z, z, z, la, tri, trit, states, states, do], name=name, grid=(nchunk,),
        in_specs=zspecs + [rows(H * DK), square, square,
                           pl.BlockSpec((H, None, DV, DK), lambda c: (0, last - c, 0, 0)),
                           pl.BlockSpec((H, None, DV, DK), lambda c: (0, jnp.maximum(last - c - 1, 0), 0, 0)),
                           rows(H * DV)],
        out_specs=[rows(H * DK), rows(H * DK), rows(H * DV), rows(H * DK)],
        out_shape=[jax.ShapeDtypeStruct((S, H * DK), F32), jax.ShapeDtypeStruct((S, H * DK), F32),
                   jax.ShapeDtypeStruct((S, H * DV), F32), jax.ShapeDtypeStruct((S, H * DK), F32)],
        scratch_shapes=[pltpu.VMEM((H, DV, DK), F32)], sem=("arbitrary",), comm=comm)


def loss_head(y, target, *, name, tr=256):
    S, D = y.shape
    tr = _tile(S, tr, 8)

    def body(y_ref, t_ref, dy_ref, loss_ref):
        i = pl.program_id(0)
        err = y_ref[...] - t_ref[...]
        dy_ref[...] = err * (1.0 / D)
        part = jnp.zeros((1, LANE), F32) + 0.5 * jnp.sum(jnp.sum(err * err, axis=-1, keepdims=True) * (1.0 / D))

        @pl.when(i == 0)
        def _():
            loss_ref[...] = part

        @pl.when(i > 0)
        def _():
            loss_ref[...] += part

    spec = pl.BlockSpec((tr, D), lambda i: (i, 0))
    return pl.pallas_call(
        body, name=name, grid=(S // tr,), in_specs=[spec, spec],
        out_specs=[spec, pl.BlockSpec((1, LANE), lambda i: (0, 0))],
        out_shape=[jax.ShapeDtypeStruct((S, D), F32), jax.ShapeDtypeStruct((1, LANE), F32)],
        compiler_params=_cparams(("arbitrary",)))(y, target)


def _core_index():
    return lax.axis_index("c").astype(jnp.int32).reshape(1)


def _chip_slots():
    x, y, c = lax.axis_index("x"), lax.axis_index("y"), lax.axis_index("c")
    return jnp.stack([2 * x + y, 2 * (1 - x) + y, 2 * x + (1 - y), 2 * (1 - x) + (1 - y), c]).astype(jnp.int32)


def sum_chip_parts(own, parts, *, name, tr=1024):
    _, R, C = own.shape
    tr = _tile(R, tr, 8)

    def body(idx_ref, o_ref, p0_ref, p1_ref, p2_ref, out_ref):
        acc = o_ref[...].astype(F32) + p0_ref[...].astype(F32)
        acc = acc + p1_ref[...].astype(F32)
        out_ref[...] = acc + p2_ref[...].astype(F32)

    def slot(k):
        return pl.BlockSpec((None, tr, C), lambda i, idx: (idx[k], i, 0))

    grid_spec = pltpu.PrefetchScalarGridSpec(num_scalar_prefetch=1, grid=(R // tr,),
                                             in_specs=[slot(0), slot(1), slot(2), slot(3)], out_specs=slot(4))
    return pl.pallas_call(body, name=name, grid_spec=grid_spec, out_shape=jax.ShapeDtypeStruct((2, R, C), F32),
                          compiler_params=_cparams(("parallel",)))(_chip_slots(), own, parts, parts, parts)


def add_own_half(g, got, out_dtype, *, name, tr=1024):
    n, _, R, C = g.shape
    tr = _tile(R, tr, 8)

    def body(c_ref, a_ref, b_ref, o_ref):
        o_ref[...] = (a_ref[...].astype(F32) + b_ref[...].astype(F32)).astype(out_dtype)

    spec = pl.BlockSpec((None, tr, C), lambda s, i, c: (s, i, 0))
    grid_spec = pltpu.PrefetchScalarGridSpec(
        num_scalar_prefetch=1, grid=(n, R // tr),
        in_specs=[pl.BlockSpec((None, None, tr, C), lambda s, i, c: (s, c[0], i, 0)), spec], out_specs=spec)
    return pl.pallas_call(body, name=name, grid_spec=grid_spec, out_shape=jax.ShapeDtypeStruct((n, R, C), out_dtype),
                          compiler_params=_cparams(("parallel", "parallel")))(_core_index(), g, got)


def adamw(items, *, name, max_steps=16, behind=None):
    c1 = 1.0 / (1.0 - ADAM_B1 ** ADAM_STEP)
    c2 = 1.0 / (1.0 - ADAM_B2 ** ADAM_STEP)
    n = len(items)
    steps = max_steps
    while steps > 1 and any(it[0].shape[0] % (8 * steps) for it in items):
        steps //= 2
    tail = [] if behind is None else [behind]

    def body(*refs):
        for a in range(n):
            w_ref, g_ref, m_ref, v_ref = refs[4 * a:4 * a + 4]
            go_ref, d_ref, nm_ref, nv_ref = refs[4 * n + len(tail) + 4 * a:4 * n + len(tail) + 4 * a + 4]
            gv = g_ref[...]
            go_ref[...] = gv
            nm = ADAM_B1 * m_ref[...] + (1.0 - ADAM_B1) * gv
            nv = ADAM_B2 * v_ref[...] + (1.0 - ADAM_B2) * (gv * gv)
            nm_ref[...] = nm
            nv_ref[...] = nv
            d_ref[...] = -ADAM_LR * ((nm * c1) / (jnp.sqrt(nv * c2) + ADAM_EPS) + ADAM_WD * w_ref[...])

    ops, in_specs, out_specs, out_shape = [], [], [], []
    for w, g, m, v in items:
        R, C = w.shape
        spec = pl.BlockSpec((R // steps, C), lambda i: (i, 0))
        ops += [w, g, m, v]
        in_specs += [spec] * 4
        out_specs += [spec] * 4
        out_shape += [jax.ShapeDtypeStruct((R, C), F32)] * 4
    flat = _pcall(body, ops + tail, name=name, grid=(steps,), in_specs=in_specs + [ANY] * len(tail), out_specs=out_specs,
                  out_shape=out_shape, sem=("parallel",))
    return [tuple(flat[4 * a:4 * a + 4]) for a in range(n)]


def _place():
    x, y, c = lax.axis_index("x"), lax.axis_index("y"), lax.axis_index("c")
    chips = [(1 - x, y), (x, 1 - y), (1 - x, 1 - y)]
    return x, y, c, chips


def _rcopy(src, dst, send, recv, j, to):
    return pltpu.make_async_remote_copy(src_ref=src, dst_ref=dst, send_sem=send.at[j], recv_sem=recv.at[j], device_id=to,
                                        device_id_type=MESH)


def gather_stage1(shards, split):
    n = len(shards)
    ins = [s.reshape(2, s.shape[0] // 2, s.shape[1]) if sp else s for s, sp in zip(shards, split)]
    outs = [jax.ShapeDtypeStruct((N_CHIPS,) + a.shape, a.dtype) for a in ins]

    def start(in_refs, out_refs, send, recv, base):
        x, y, c, chips = _place()
        mine = 2 * x + y
        for i in range(n):
            src = in_refs[i].at[c] if split[i] else in_refs[i]
            dst = out_refs[i].at[mine, c] if split[i] else out_refs[i].at[mine]
            for k, (px, py) in enumerate(chips):
                _rcopy(src, dst, send, recv, base + 3 * i + k, (px, py, c)).start()

    def wait(in_refs, out_refs, send, recv, base):
        x, y, c, chips = _place()
        for i in range(n):
            src = in_refs[i].at[c] if split[i] else in_refs[i]
            for k, (px, py) in enumerate(chips):
                dst = out_refs[i].at[2 * px + py, c] if split[i] else out_refs[i].at[2 * px + py]
                _rcopy(src, dst, send, recv, base + 3 * i + k, (px, py, c)).wait()

    return Comm(ins, outs, 3 * n, start, wait)


def gather_stage2(slots, shards, split):
    n = len(slots)
    own = [s.reshape(2, s.shape[0] // 2, s.shape[1]) if sp else s for s, sp in zip(shards, split)]

    def copies(in_refs, out_refs, send, recv, base):
        x, y, c, chips = _place()
        sib = (x, y, 1 - c)
        for i in range(n):
            j = base + 4 * i
            mine = out_refs[i].at[2 * x + y]
            yield _rcopy(in_refs[n + i], mine, send, recv, j + 3, sib), _rcopy(in_refs[n + i], mine, send, recv, j + 3, sib)
            if split[i]:
                for k, (px, py) in enumerate(chips):
                    s = 2 * px + py
                    yield (_rcopy(in_refs[i].at[s, c], out_refs[i].at[s, c], send, recv, j + k, sib),
                           _rcopy(in_refs[i].at[s, c], out_refs[i].at[s, 1 - c], send, recv, j + k, sib))

    def start(*a):
        for out, _ in copies(*a):
            out.start()

    def wait(*a):
        for _, back in copies(*a):
            back.wait()

    return Comm(list(slots) + own, [jax.ShapeDtypeStruct(s.shape, s.dtype) for s in slots], 4 * n, start, wait,
                {i: i for i in range(n)})


def swap_halves(gs):
    n = len(gs)

    def copies(in_refs, out_refs, send, recv, base):
        x, y, c, _ = _place()
        return [_rcopy(in_refs[i].at[s, 1 - c], out_refs[i].at[s], send, recv, base + N_CHIPS * i + s, (x, y, 1 - c))
                for i in range(n) for s in range(N_CHIPS)]

    def start(*a):
        for cp in copies(*a):
            cp.start()

    def wait(*a):
        for cp in copies(*a):
            cp.wait()

    return Comm(gs, [jax.ShapeDtypeStruct((N_CHIPS,) + g.shape[2:], g.dtype) for g in gs], N_CHIPS * n, start, wait)


def exchange_chips(ps):
    n = len(ps)

    def start(in_refs, out_refs, send, recv, base):
        x, y, c, chips = _place()
        for i in range(n):
            for k, (px, py) in enumerate(chips):
                _rcopy(in_refs[i].at[2 * px + py], out_refs[i].at[2 * x + y], send, recv, base + 3 * i + k,
                       (px, py, c)).start()

    def wait(in_refs, out_refs, send, recv, base):
        x, y, c, chips = _place()
        for i in range(n):
            for k, (px, py) in enumerate(chips):
                _rcopy(in_refs[i].at[2 * px + py], out_refs[i].at[2 * px + py], send, recv, base + 3 * i + k,
                       (px, py, c)).wait()

    return Comm(ps, [jax.ShapeDtypeStruct(p.shape, p.dtype) for p in ps], 3 * n, start, wait)


def join_halves(fs):
    n = len(fs)

    def start(in_refs, out_refs, send, recv, base):
        x, y, c, _ = _place()
        for i in range(n):
            _rcopy(in_refs[i].at[c], out_refs[i].at[c], send, recv, base + i, (x, y, 1 - c)).start()

    def wait(in_refs, out_refs, send, recv, base):
        x, y, c, _ = _place()
        for i in range(n):
            _rcopy(in_refs[i].at[c], out_refs[i].at[1 - c], send, recv, base + i, (x, y, 1 - c)).wait()

    return Comm(fs, [jax.ShapeDtypeStruct(f.shape, f.dtype) for f in fs], n, start, wait, {i: i for i in range(n)})


def allreduce_small(v, *, name):
    m_per, n = v.shape

    def body(x_ref, sum_ref, all_ref, send_sems, recv_sems, local_sem):
        x, y, c, chips = _place()
        me, sibling = (x, y, c), (x, y, 1 - c)

        def rows(px, py, pc):
            return all_ref.at[pl.ds((4 * px + 2 * py + pc) * m_per, m_per), :]

        def copy(k, block, to, src=None):
            return pltpu.make_async_remote_copy(src_ref=rows(*block) if src is None else src, dst_ref=rows(*block),
                                                send_sem=send_sems.at[k], recv_sem=recv_sems.at[k], device_id=to,
                                                device_id_type=MESH)

        mine = pltpu.make_async_copy(x_ref, rows(*me), local_sem)
        mine.start()
        first = [copy(0, me, sibling, src=x_ref)]
        first += [copy(1 + j, me, (*chip, c), src=x_ref) for j, chip in enumerate(chips)]
        for cp in first:
            cp.start()
        passed = [copy(4 + j, (*chip, c), sibling) for j, chip in enumerate(chips)]
        for j, chip in enumerate(chips):
            copy(1 + j, (*chip, c), me).wait_recv()
            passed[j].start()
        copy(0, sibling, me).wait_recv()
        for j, chip in enumerate(chips):
            copy(4 + j, (*chip, 1 - c), me).wait_recv()
        for cp in first + passed:
            cp.wait_send()
        mine.wait()
        acc = all_ref[0:m_per, :]
        for d in range(1, N_DEV):
            acc = acc + all_ref[d * m_per:(d + 1) * m_per, :]
        sum_ref[...] = acc

    vm = pl.BlockSpec(memory_space=pltpu.VMEM)
    return pl.pallas_call(
        body, name=name, in_specs=[vm], out_specs=vm, out_shape=jax.ShapeDtypeStruct((m_per, n), F32),
        scratch_shapes=[pltpu.VMEM((N_DEV * m_per, n), F32), pltpu.SemaphoreType.DMA((7,)),
                        pltpu.SemaphoreType.DMA((7,)), pltpu.SemaphoreType.DMA],
    )(v)


def _cols_to_slots(w):
    r, c4 = w.shape
    return w.reshape(r, N_CHIPS, c4 // N_CHIPS).transpose(1, 0, 2)


def _slots_to_cols(w):
    n, r, c = w.shape
    return w.transpose(1, 0, 2).reshape(r, n * c)


def _pad_cols(a, width):
    return jnp.pad(a, ((0, 0), (0, width - a.shape[1])))


class InLayout:
    def __init__(self, q_rank, kv_rank):
        gk = GLA_HEADS * GLA_DK
        gv = GLA_HEADS * GLA_DV
        sizes = [q_rank, kv_rank, MLA_ROPE, gk, gk, gv, GLA_GATE_RANK, gv]
        names = ["zq", "zkv", "zkr", "gq", "gk", "gv", "zg", "zr"]
        starts = np.concatenate([[0], np.cumsum(sizes)[:-1]])
        self.ref = {n: (int(s), int(z)) for n, s, z in zip(names, starts, sizes)}
        self.ref_width = int(sum(sizes))
        self.order = ["gv", "zr", "zq", "gq", "gk", "zkv", "zkr", "zg"]
        self.off, self.size = {}, {}
        pos = 0
        for n in self.order:
            padded = -(-self.ref[n][1] // LANE) * LANE
            self.off[n], self.size[n] = pos, padded
            pos += padded
        self.width = pos
        self.shard = self.ref_width // N_CHIPS
        self.shard_pad = -(-self.shard // LANE) * LANE

    def _pieces(self, lo, hi):
        out = []
        while lo < hi:
            s = lo // self.shard
            end = min(hi, (s + 1) * self.shard)
            out.append((s * self.shard_pad + lo - s * self.shard, s * self.shard_pad + end - s * self.shard))
            lo = end
        return out

    def from_shards(self, zs):
        cols = []
        for n in self.order:
            start, size = self.ref[n]
            cols += [zs[:, a:b] for a, b in self._pieces(start, start + size)]
            if self.size[n] > size:
                cols.append(jnp.zeros((zs.shape[0], self.size[n] - size), zs.dtype))
        return jnp.concatenate(cols, axis=1)

    def to_shards(self, dz):
        names = sorted(self.ref, key=lambda n: self.ref[n][0])
        ref = jnp.concatenate([dz[:, self.off[n]:self.off[n] + self.ref[n][1]] for n in names], axis=1)
        ref = ref.reshape(dz.shape[0], N_CHIPS, self.shard)
        return jnp.pad(ref, ((0, 0), (0, 0), (0, self.shard_pad - self.shard))).reshape(dz.shape[0], -1)


def _pad_q_up(w):
    r = w.shape[0]
    w = w.reshape(r, MLA_HEADS, MLA_QK)
    w = jnp.pad(w, ((0, 0), (0, 0), (0, MLA_HEAD_PAD - MLA_QK)))
    return w.reshape(r, MLA_HEADS * MLA_HEAD_PAD)


def _unpad_q_up(g):
    r = g.shape[0]
    return g.reshape(r, MLA_HEADS, MLA_HEAD_PAD)[:, :, :MLA_QK].reshape(r, MLA_HEADS * MLA_QK)


def _rope_tables(positions):
    half = MLA_ROPE // 2
    inv_freq = ROPE_THETA ** (-jnp.arange(half, dtype=F32) / half)
    ang = positions.astype(F32).reshape(-1, 1) * inv_freq
    cos, sin = jnp.cos(ang), jnp.sin(ang)
    s = ang.shape[0]
    cosf = jnp.concatenate([cos, cos, jnp.ones((s, LANE - MLA_ROPE), F32)], axis=1)
    sinf = jnp.concatenate([sin, sin, jnp.zeros((s, LANE - MLA_ROPE), F32)], axis=1)
    rot = np.zeros((LANE, LANE), np.float32)
    for j in range(half):
        rot[j + half, j] = -1.0
        rot[j, j + half] = 1.0
    return cosf, sinf, jnp.asarray(rot)


SMALL = ["ffn1_norm", "mix_norm", "q_a_norm", "kv_a_norm", "mla_q_norm", "mla_k_norm", "gla_b_gate", "gla_out_norm",
         "mem_attn_norm", "mem_norm", "mem_q_norm", "mem_k_norm", "ffn2_norm"]
BIG = ["ffn1_w_gate", "ffn1_w_up", "ffn1_w_down", "w_in", "w_q_up", "w_kv_up", "w_out", "mem_w_q", "mem_w_k",
       "mem_w_v", "mem_w_o", "ffn2_w_gate", "ffn2_w_up", "ffn2_w_down"]
COL_SHARDED = {"ffn1_w_gate", "ffn1_w_up", "w_in", "w_q_up", "w_kv_up", "gla_w_gate2", "mem_w_o", "ffn2_w_gate", "ffn2_w_up"}
WEIGHTS = ["ffn1_norm", "ffn1_w_gate", "ffn1_w_up", "ffn1_w_down", "mix_norm", "w_in", "q_a_norm", "w_q_up", "kv_a_norm",
           "w_kv_up", "mla_q_norm", "mla_k_norm", "gla_w_gate2", "gla_b_gate", "gla_out_norm", "w_out", "mem_attn_norm",
           "mem_norm", "mem_w_q", "mem_w_k", "mem_w_v", "mem_w_o", "mem_q_norm", "mem_k_norm", "ffn2_norm", "ffn2_w_gate",
           "ffn2_w_up", "ffn2_w_down"]


def _pack_small(vals, rows=8):
    flat = jnp.concatenate([v.reshape(-1).astype(F32) for v in vals])
    n = flat.shape[0]
    per = -(-n // (rows * LANE)) * LANE
    return jnp.pad(flat, (0, rows * per - n)).reshape(rows, per)


def _unpack_small(packed, shapes):
    flat = packed.reshape(-1)
    out, pos = [], 0
    for s in shapes:
        n = int(np.prod(s))
        out.append(flat[pos:pos + n].reshape(s))
        pos += n
    return out


FFN1 = ["ffn1_w_gate", "ffn1_w_up", "ffn1_w_down"]
FFN2 = ["ffn2_w_gate", "ffn2_w_up", "ffn2_w_down"]
SLOT_WEIGHTS = {"ffn1_w_gate", "ffn1_w_up", "ffn2_w_gate", "ffn2_w_up", "w_in"}
MID_A = ["w_in", "w_q_up", "w_kv_up", "gla_w_gate2"]
MID_B = ["w_out", "mem_w_q", "mem_w_k", "mem_w_v", "mem_w_o"]


def _with(res, comm):
    return res if comm is not None else (res, None)


def kernel(x, mem, positions, ffn1_norm, ffn1_w_gate, ffn1_w_up, ffn1_w_down, mix_norm, w_in, q_a_norm, w_q_up, kv_a_norm, w_kv_up, mla_q_norm, mla_k_norm, gla_w_gate2, gla_b_gate, gla_out_norm, w_out, mem_attn_norm, mem_norm, mem_w_q, mem_w_k, mem_w_v, mem_w_o, mem_q_norm, mem_k_norm, ffn2_norm, ffn2_w_gate, ffn2_w_up, ffn2_w_down, loss_target, m_ffn1_norm, m_ffn1_w_gate, m_ffn1_w_up, m_ffn1_w_down, m_mix_norm, m_w_in, m_q_a_norm, m_w_q_up, m_kv_a_norm, m_w_kv_up, m_mla_q_norm, m_mla_k_norm, m_gla_w_gate2, m_gla_b_gate, m_gla_out_norm, m_w_out, m_mem_attn_norm, m_mem_norm, m_mem_w_q, m_mem_w_k, m_mem_w_v, m_mem_w_o, m_mem_q_norm, m_mem_k_norm, m_ffn2_norm, m_ffn2_w_gate, m_ffn2_w_up, m_ffn2_w_down, v_ffn1_norm, v_ffn1_w_gate, v_ffn1_w_up, v_ffn1_w_down, v_mix_norm, v_w_in, v_q_a_norm, v_w_q_up, v_kv_a_norm, v_w_kv_up, v_mla_q_norm, v_mla_k_norm, v_gla_w_gate2, v_gla_b_gate, v_gla_out_norm, v_w_out, v_mem_attn_norm, v_mem_norm, v_mem_w_q, v_mem_w_k, v_mem_w_v, v_mem_w_o, v_mem_q_norm, v_mem_k_norm, v_ffn2_norm, v_ffn2_w_gate, v_ffn2_w_up, v_ffn2_w_down):
    args = dict(locals())
    two_d = lambda a: a[0] if a.ndim == 3 else a
    W = {n: two_d(args[n]) for n in WEIGHTS}
    M1 = {n: two_d(args["m_" + n]) for n in WEIGHTS}
    V2 = {n: two_d(args["v_" + n]) for n in WEIGHTS}
    xs, mems, tgt = x[0], mem[0], loss_target[0]
    S, D = xs.shape
    chip = 2 * lax.axis_index("x") + lax.axis_index("y")

    q_rank, kv_rank = W["w_q_up"].shape[0], W["w_kv_up"].shape[0]
    lay = InLayout(q_rank, kv_rank)
    off = lay.off
    shard16 = {n: W[n].astype(BF16) for n in BIG + ["gla_w_gate2"]}
    shard16["w_in"] = _pad_cols(shard16["w_in"], lay.shard_pad)
    full = {}

    def stage1(names):
        return gather_stage1([shard16[n] for n in names], [n != "gla_w_gate2" for n in names])

    def stage2(names, slots):
        return gather_stage2(slots, [shard16[n] for n in names], [n != "gla_w_gate2" for n in names])

    def finish(names, slots):
        for n, s in zip(names, slots):
            s = s.reshape((N_CHIPS,) + shard16[n].shape)
            if n in SLOT_WEIGHTS:
                full[n] = s
            else:
                full[n] = _slots_to_cols(s) if n in COL_SHARDED else s.reshape(-1, s.shape[2])

    token = {"last": None}

    def begin(comm, name, after=None):
        started = start_comm(comm, name=name, after=token["last"] if after is None else after)
        token["last"] = started[-1]
        return comm, started

    up_names, down_names = FFN1[:2], FFN1[2:]
    first = [begin(stage1([n]), f"gather_start_{n}") for n in up_names]
    n1 = row_fwd(rms_fn, [V(xs)], [W["ffn1_norm"]], [(D, BF16)], [(0, 0, 0)], name="ffn1_norm")[0]
    gate_s1 = wait_comm(*first[0], n1, name=f"gather_wait_{up_names[0]}")
    finish(up_names[:1], run_comm(stage2(up_names[:1], gate_s1), name="pass_ffn1_gate"))
    up_s1 = wait_comm(*first[1], full[up_names[0]], name=f"gather_wait_{up_names[1]}")
    down1 = begin(stage1(down_names), "gather_start_ffn1_down", after=up_s1[0])
    mid_a_s1 = begin(stage1(MID_A), "gather_start_mid_a")
    finish(up_names[1:], run_comm(stage2(up_names[1:], up_s1), name="pass_ffn1_up"))
    cosf, sinf, rot = _rope_tables(positions[0])
    tri = jnp.asarray(np.tril(np.ones((CHUNK, CHUNK), np.float32)))
    gqn = W["mla_q_norm"][:, :MLA_NOPE]
    gqr = _pad_cols(W["mla_q_norm"][:, MLA_NOPE:], LANE)
    gkn = W["mla_k_norm"][:, :MLA_NOPE]
    gkr = _pad_cols(W["mla_k_norm"][:, MLA_NOPE:], LANE)
    HP = MLA_HEAD_PAD
    mla_scale = MLA_QK ** -0.5
    mem_scale = MEM_HEAD_DIM ** -0.5
    mla_w = MLA_HEADS * MLA_V
    gla_w = GLA_HEADS * GLA_DV
    mem_w = MEM_HEADS * MEM_HEAD_DIM

    gate1, up1, act1 = ffn_up(n1, full["ffn1_w_gate"], full["ffn1_w_up"], name="ffn1_up", behind=token["last"])
    finish(down_names, run_comm(stage2(down_names, wait_comm(*down1, act1, name="gather_wait_ffn1_down")),
                                name="pass_ffn1_down"))
    mid_a1 = wait_comm(*mid_a_s1, act1, name="gather_wait_mid_a")
    mid_b = begin(stage1(MID_B), "gather_start_mid_b", after=mid_a1[0])
    x1, got = mm([(act1, full["ffn1_w_down"])], "nn", F32, alpha=0.5, res=xs, name="ffn1_down",
                 comm=stage2(MID_A, mid_a1), behind=token["last"])
    ffn1_saved = (n1, gate1, up1, act1)
    finish(MID_A, got)
    ffn2_s1 = [begin(stage1([n]), f"gather_start_{n}", after=x1 if n == FFN2[0] else None) for n in FFN2]
    w_q_up_p = _pad_q_up(full["w_q_up"])
    w_gate2_p = jnp.pad(full["gla_w_gate2"], ((0, LANE - GLA_GATE_RANK), (0, 0)))
    h = row_fwd(rms_fn, [V(x1)], [W["mix_norm"]], [(D, BF16)], [(0, 0, 0)], name="mix_norm")[0]
    mid_b1 = wait_comm(*mid_b, h, name="gather_wait_mid_b")
    z_shards, got = mm([(h, full["w_in"])], "nn", F32, name="in_proj", b_slots=True, comm=stage2(MID_B, mid_b1),
                       behind=token["last"])
    z = lay.from_shards(z_shards)
    finish(MID_B, got)
    qa = row_fwd(rms_fn, [V(z, off["zq"], q_rank)], [W["q_a_norm"]], [(q_rank, BF16)], [(0, 0, 0)], name="q_a_norm")[0]
    kva = row_fwd(rms_fn, [V(z, off["zkv"], kv_rank)], [W["kv_a_norm"]], [(kv_rank, BF16)], [(0, 0, 0)], name="kv_a_norm")[0]
    qraw = mm([(qa, w_q_up_p)], "nn", F32, name="q_up")
    kvraw = mm([(kva, full["w_kv_up"])], "nn", F32, name="kv_up")
    tabs = [V(cosf, diff=False), V(sinf, diff=False)]
    q_rows = [V(qraw, 0, LANE, HP), V(qraw, LANE, LANE, HP)] + tabs
    k_rows = [V(kvraw, 0, LANE, HP), V(z, off["zkr"], LANE, 0)] + tabs
    qh = row_fwd(qk_prep_fn, q_rows, [gqn, gqr, rot], [(MLA_HEADS * HP, BF16)], [(0, 0, HP), (0, LANE, HP)],
                 heads=MLA_HEADS, name="q_prep")[0]
    kh = row_fwd(qk_prep_fn, k_rows, [gkn, gkr, rot], [(MLA_HEADS * HP, BF16)], [(0, 0, HP), (0, LANE, HP)],
                 heads=MLA_HEADS, name="k_prep")[0]
    mla_kw = dict(heads=MLA_HEADS, dk=HP, dv=MLA_V, v_off=1, v_hs=2, scale=mla_scale, causal=True, tq=512)
    o_mla = attn_fwd(qh, kh, kvraw, name="mla_attn", **mla_kw)

    zg = z[:, off["zg"]:off["zg"] + LANE]
    pre = mm([(zg, w_gate2_p)], "nn", F32, name="gla_gate")
    la = row_fwd(gate_fn, [V(pre)], [W["gla_b_gate"]], [(pre.shape[1], F32)], [(0, 0, 0)], name="gla_log_decay")[0]
    gla_kw = dict(q_off=off["gq"], k_off=off["gk"], v_off=off["gv"])
    o_raw, states = gla_fwd(z, la, tri, name="gla_scan", **gla_kw)
    gla_rows = [V(o_raw, 0, GLA_DV, GLA_DV), V(z, off["zr"], GLA_DV, GLA_DV)]
    o_gla = row_fwd(gla_out_fn, gla_rows, [W["gla_out_norm"]], [(gla_w, BF16)], [(0, 0, GLA_DV)], heads=GLA_HEADS,
                    name="gla_out")[0]
    o_cat = jnp.concatenate([o_mla, o_gla], axis=1)
    f2 = [wait_comm(*ffn2_s1[k], o_cat, name=f"gather_wait_{FFN2[k]}")[0] for k in range(2)]
    x2, got = mm([(o_cat, full["w_out"])], "nn", F32, res=x1, name="out_proj", comm=stage2(FFN2[:2], f2))
    finish(FFN2[:2], got)

    hm = row_fwd(rms_fn, [V(x2)], [W["mem_attn_norm"]], [(D, BF16)], [(0, 0, 0)], name="mem_attn_norm")[0]
    mn = row_fwd(rms_fn, [V(mems)], [W["mem_norm"]], [(D, BF16)], [(0, 0, 0)], name="mem_norm")[0]
    qm_raw = mm([(hm, full["mem_w_q"])], "nn", F32, name="mem_q")
    km_raw = mm([(mn, full["mem_w_k"])], "nn", F32, name="mem_k")
    vm = mm([(mn, full["mem_w_v"])], "nn", F32, name="mem_v")
    hd = MEM_HEAD_DIM
    qm = row_fwd(rms_fn, [V(qm_raw, 0, hd, hd)], [W["mem_q_norm"]], [(mem_w, BF16)], [(0, 0, hd)], heads=MEM_HEADS,
                 name="mem_q_norm")[0]
    km = row_fwd(rms_fn, [V(km_raw, 0, hd, hd)], [W["mem_k_norm"]], [(mem_w, BF16)], [(0, 0, hd)], heads=MEM_HEADS,
                 name="mem_k_norm")[0]
    mem_kw = dict(heads=MEM_HEADS, dk=hd, dv=hd, v_off=0, v_hs=1, scale=mem_scale, causal=False)
    om = attn_fwd(qm, km, vm, name="mem_attn", **mem_kw)
    x3 = mm([(om, full["mem_w_o"])], "nn", F32, res=x2, name="mem_o")

    n2 = row_fwd(rms_fn, [V(x3)], [W["ffn2_norm"]], [(D, BF16)], [(0, 0, 0)], name="ffn2_norm")[0]
    f2_down = wait_comm(*ffn2_s1[2], n2, name=f"gather_wait_{FFN2[2]}")
    (gate2, up2, act2), got = ffn_up(n2, full["ffn2_w_gate"], full["ffn2_w_up"], name="ffn2_up",
                                     comm=stage2(FFN2[2:], f2_down))
    finish(FFN2[2:], got)
    y = mm([(act2, full["ffn2_w_down"])], "nn", F32, alpha=0.5, res=x3, name="ffn2_down")
    dy, loss_part = loss_head(y, tgt, name="loss_head")
    G = {"loss": loss_part[:, :1]}

    chip_sum, reduced = {}, {}

    def to_halves(n):
        g = G[n]
        if n in SLOT_WEIGHTS:
            s = g
        else:
            s = _cols_to_slots(g) if n in COL_SHARDED else g.reshape(N_CHIPS, g.shape[0] // N_CHIPS, g.shape[1])
        return s.reshape(N_CHIPS, 2, s.shape[1] // 2, s.shape[2])

    def add2(names, halves, got):
        for n, a, b in zip(names, halves, got):
            chip_sum[n] = add_own_half(a, b, BF16, name=f"rs_add2_{n}")

    to_join = []

    def add4_join(names, parts):
        for n, p in zip(names, parts):
            to_join.append((n, sum_chip_parts(chip_sum[n], p, name=f"rs_add4_{n}")))

    def with_joins(comm):
        names, totals = [n for n, _ in to_join], [t for _, t in to_join]
        to_join.clear()
        if not names:
            return comm, lambda got: got
        own = 0 if comm is None else len(comm.out_shapes)
        joined = join_halves(totals)

        def split(got):
            for n, b in zip(names, got[own:]):
                reduced[n] = b.reshape(-1, b.shape[2])[:, :W[n].shape[1]]
            return got[:own]

        return (joined if comm is None else merge_comms(comm, joined)), split

    def flush_joins():
        comm, split = with_joins(None)
        if comm is not None:
            split(run_comm(comm, name=f"rs_join_{len(reduced)}"))

    in_flight = []

    def xchg_start(names):
        in_flight.append((names,) + begin(exchange_chips([chip_sum[n] for n in names]), f"xchg_start_{names[0]}"))

    def xchg_wait(after, count=1):
        for _ in range(count):
            names, comm, started = in_flight.pop(0)
            add4_join(names, wait_comm(comm, started, after, name=f"xchg_wait_{names[0]}"))

    def ffn_backward(dout, xin, tag, saved, dact_comm=None, after_dact=None):
        n_, gate, up, act = saved
        nd, ng, nu = f"{tag}_w_down", f"{tag}_w_gate", f"{tag}_w_up"
        (dgate, dup), got0 = _with(ffn_dact(dout, full[nd], gate, up, 0.5, name=f"{tag}_dact", comm=dact_comm,
                                            behind=token["last"]), dact_comm)
        if after_dact:
            after_dact(got0)
        G[nd] = mm([(act, dout)], "tn", F32, alpha=0.5, name=f"{tag}_dwd", tm=1408, tn=1024, behind=token["last"])
        hd_ = to_halves(nd)
        comm, split = with_joins(swap_halves([hd_]))
        G[ng], got = mm([(n_, dgate)], "tn", F32, name=f"{tag}_dwg", out_slots=True, tm=1024, tn=1408, rows_inner=True,
                        comm=comm)
        add2([nd], [hd_], split(got))
        xchg_start([nd])
        hg = to_halves(ng)
        G[nu], got_g = mm([(n_, dup)], "tn", F32, name=f"{tag}_dwu", out_slots=True, tm=1024, tn=1408, rows_inner=True,
                          comm=swap_halves([hg]), behind=token["last"])
        add2([ng], [hg], got_g)
        xchg_start([ng])
        hu = to_halves(nu)
        dn, got_u = mm([(dgate, full[ng]), (dup, full[nu])], "nt", F32, name=f"{tag}_dn", b_slots=True, tn=1024, tk=1408,
                       comm=swap_halves([hu]), behind=token["last"])
        add2([nu], [hu], got_u)
        xchg_start([nu])
        dx, G[f"{tag}_norm"] = row_bwd(rms_fn, [V(xin)], [W[f"{tag}_norm"]], [V(dn)], const_diff=[True], res=dout,
                                       tr=512, name=f"{tag}_dnorm")
        return dx

    g3 = ffn_backward(dy, x3, "ffn2", (n2, gate2, up2, act2))
    xchg_wait(g3)

    d_om = mm([(g3, full["mem_w_o"])], "nt", F32, name="mem_o_dx", behind=token["last"])
    G["mem_w_o"] = mm([(om, g3)], "tn", F32, name="mem_o_dw")
    dqm, dkm, dvm = attn_bwd(qm, km, vm, d_om, name="mem_attn_bwd", **mem_kw)
    dqm_raw, G["mem_q_norm"] = row_bwd(rms_fn, [V(qm_raw, 0, hd, hd)], [W["mem_q_norm"]], [V(dqm, 0, hd, hd)],
                                       const_diff=[True], heads=MEM_HEADS, row_dtype=BF16, name="mem_q_norm_bwd")
    dkm_raw, G["mem_k_norm"] = row_bwd(rms_fn, [V(km_raw, 0, hd, hd)], [W["mem_k_norm"]], [V(dkm, 0, hd, hd)],
                                       const_diff=[True], heads=MEM_HEADS, row_dtype=BF16, name="mem_k_norm_bwd")
    dhm = mm([(dqm_raw, full["mem_w_q"])], "nt", F32, name="mem_q_dx")
    G["mem_w_q"] = mm([(hm, dqm_raw)], "tn", F32, name="mem_q_dw")
    dmn = mm([(dkm_raw, full["mem_w_k"]), (dvm, full["mem_w_v"])], "nt", F32, name="mem_kv_dx")
    G["mem_w_k"] = mm([(mn, dkm_raw)], "tn", F32, name="mem_k_dw")
    G["mem_w_v"] = mm([(mn, dvm)], "tn", F32, name="mem_v_dw")
    _, G["mem_norm"] = row_bwd(rms_fn, [V(mems)], [W["mem_norm"]], [V(dmn)], const_diff=[True], row_dtype=BF16,
                               name="mem_norm_bwd")
    g2, G["mem_attn_norm"] = row_bwd(rms_fn, [V(x2)], [W["mem_attn_norm"]], [V(dhm)], const_diff=[True], res=g3,
                                     tr=512, name="mem_attn_norm_bwd")

    xchg_wait(g2, 2)

    d_ocat = mm([(g2, full["w_out"])], "nt", F32, name="out_proj_dx")
    G["w_out"] = mm([(o_cat, g2)], "tn", F32, name="out_proj_dw")

    d_oraw, d_zr, G["gla_out_norm"] = row_bwd(gla_out_fn, gla_rows, [W["gla_out_norm"]],
                                              [V(d_ocat, mla_w, GLA_DV, GLA_DV)], const_diff=[True], heads=GLA_HEADS,
                                              name="gla_out_bwd")
    mid_b_halves = [to_halves(n) for n in MID_B]
    comm, split = with_joins(swap_halves(mid_b_halves))
    (d_gq, d_gk, d_gv, d_la), got = gla_bwd(z, la, tri, tri.T, states, d_oraw, name="gla_scan_bwd", comm=comm, **gla_kw)
    add2(MID_B, mid_b_halves, split(got))
    xchg_start(MID_B)
    d_pre, G["gla_b_gate"] = row_bwd(gate_fn, [V(pre)], [W["gla_b_gate"]], [V(d_la)], const_diff=[True], row_dtype=BF16,
                                     name="gla_log_decay_bwd")
    d_zg = mm([(d_pre, w_gate2_p)], "nt", BF16, name="gla_gate_dx", behind=token["last"])
    G["gla_w_gate2"] = mm([(zg, d_pre)], "tn", F32, name="gla_gate_dw")[:GLA_GATE_RANK]

    comm, split = with_joins(None)
    (d_qh, d_kh, d_v), got = _with(attn_bwd(qh, kh, kvraw, d_ocat, name="mla_attn_bwd", comm=comm, **mla_kw), comm)
    split(got)
    cq = [V(d_qh, 0, LANE, HP), V(d_qh, LANE, LANE, HP)]
    ck = [V(d_kh, 0, LANE, HP), V(d_kh, LANE, LANE, HP)]
    d_qraw, d_gqn, d_gqr = row_bwd(qk_prep_fn, q_rows, [gqn, gqr, rot], cq, const_diff=[True, True, False],
                                   heads=MLA_HEADS, row_dtype=BF16, pack={0: (0, HP), 1: (LANE, HP)},
                                   pack_width=MLA_HEADS * HP, name="q_prep_bwd")
    d_kvraw, d_zkr, d_gkn, d_gkr = row_bwd(qk_prep_fn, k_rows, [gkn, gkr, rot], ck, const_diff=[True, True, False],
                                           heads=MLA_HEADS, row_dtype=BF16, pack={0: (0, HP)}, pack_width=MLA_HEADS * HP,
                                           fills=[(V(d_v, 0, MLA_V, MLA_V), LANE, HP)], name="k_prep_bwd")
    G["mla_q_norm"] = jnp.concatenate([d_gqn, d_gqr[:, :MLA_ROPE]], axis=1)
    G["mla_k_norm"] = jnp.concatenate([d_gkn, d_gkr[:, :MLA_ROPE]], axis=1)
    d_qa = mm([(d_qraw, w_q_up_p)], "nt", F32, name="q_up_dx")
    G["w_q_up"] = _unpad_q_up(mm([(qa, d_qraw)], "tn", F32, name="q_up_dw"))
    d_kva = mm([(d_kvraw, full["w_kv_up"])], "nt", F32, name="kv_up_dx")
    G["w_kv_up"] = mm([(kva, d_kvraw)], "tn", F32, name="kv_up_dw")
    d_zq, G["q_a_norm"] = row_bwd(rms_fn, [V(z, off["zq"], q_rank)], [W["q_a_norm"]], [V(d_qa)], const_diff=[True],
                                  row_dtype=BF16, name="q_a_norm_bwd")
    d_zkv, G["kv_a_norm"] = row_bwd(rms_fn, [V(z, off["zkv"], kv_rank)], [W["kv_a_norm"]], [V(d_kva)], const_diff=[True],
                                    row_dtype=BF16, name="kv_a_norm_bwd")

    seg = {"gv": d_gv, "zr": d_zr, "zq": d_zq, "gq": d_gq, "gk": d_gk, "zkv": d_zkv, "zkr": d_zkr, "zg": d_zg}
    dz = jnp.concatenate([_pad_cols(seg[n].astype(BF16), lay.size[n]) for n in lay.order], axis=1)
    xchg_wait(dz)
    comm, split = with_joins(None)
    dz_shards = lay.to_shards(dz)
    dh, got = _with(mm([(dz_shards, full["w_in"])], "nt", F32, name="in_proj_dx", b_slots=True, comm=comm), comm)
    split(got)
    G["w_in"] = mm([(h, dz_shards)], "tn", F32, name="in_proj_dw", out_slots=True)
    g1, G["mix_norm"] = row_bwd(rms_fn, [V(x1)], [W["mix_norm"]], [V(dh)], const_diff=[True], res=g2, tr=512,
                                name="mix_norm_bwd")

    mid_a = [n for n in MID_A if n != "gla_w_gate2"]
    mid_a_halves = [to_halves(n) for n in mid_a]

    def mid_a_sums(got):
        add2(mid_a, mid_a_halves, got)
        xchg_start(mid_a)

    gx = ffn_backward(g1, xs, "ffn1", ffn1_saved, dact_comm=swap_halves(mid_a_halves), after_dact=mid_a_sums)
    xchg_wait(gx, 2)

    grad, delta, new_m, new_v = {}, {}, {}, {}

    def adam_group(names, tag, behind=None):
        if any(n not in reduced for n in names):
            flush_joins()
        res = adamw([(W[n], reduced[n], M1[n], V2[n]) for n in names], name=f"adamw_{tag}", behind=behind)
        for n, (g_, d_, m_, v_) in zip(names, res):
            grad[n], delta[n], new_m[n], new_v[n] = g_, d_, m_, v_

    adam_group(FFN2, "ffn2", behind=token["last"])
    adam_group(mid_a + MID_B, "mid", behind=token["last"])
    xchg_wait(delta[MID_B[-1]], 2)
    adam_group(FFN1, "ffn1")

    small_names = SMALL + ["gla_w_gate2"]
    packed = small_names + ["loss"]
    small_sum = allreduce_small(_pack_small([G[n] for n in packed]), name="allreduce_small")
    small_g = dict(zip(packed, _unpack_small(small_sum, [G[n].shape for n in packed])))
    loss = small_g["loss"][0, 0]
    shard_c = W["gla_w_gate2"].shape[1]
    grad["gla_w_gate2"] = lax.dynamic_slice_in_dim(small_g["gla_w_gate2"], chip * shard_c, shard_c, axis=1)
    pw = _pack_small([W[n] for n in SMALL] + [W["gla_w_gate2"]])
    pg = _pack_small([small_g[n] for n in SMALL] + [grad["gla_w_gate2"]])
    pm = _pack_small([M1[n] for n in SMALL] + [M1["gla_w_gate2"]])
    pv = _pack_small([V2[n] for n in SMALL] + [V2["gla_w_gate2"]])
    (_, pd, pnm, pnv), = adamw([(pw, pg, pm, pv)], name="adamw_small")
    shapes = [W[n].shape for n in small_names]
    for n, d_, m_, v_ in zip(small_names, _unpack_small(pd, shapes), _unpack_small(pnm, shapes), _unpack_small(pnv, shapes)):
        delta[n], new_m[n], new_v[n] = d_, m_, v_
        if n != "gla_w_gate2":
            grad[n] = small_g[n]

    lead = lambda d: [d[n].reshape(args[n].shape) for n in WEIGHTS]
    return (loss, gx[None], *lead(grad), *lead(delta), *lead(new_m), *lead(new_v))
```

```python
import functools
import math

import numpy as np
import jax
import jax.numpy as jnp
from jax import lax
from jax.experimental import pallas as pl
from jax.experimental.pallas import tpu as pltpu

F32 = jnp.float32
BF16 = jnp.bfloat16
MXU_DTYPE = jnp.bfloat16
MESH = pl.DeviceIdType.MESH
ANY = pl.BlockSpec(memory_space=pl.ANY)

LANE = 128
EPS = 1e-6
CHUNK = 64
MLA_HEADS = 8
MLA_NOPE = 128
MLA_ROPE = 64
MLA_QK = MLA_NOPE + MLA_ROPE
MLA_V = 128
MLA_HEAD_PAD = 2 * LANE
ROPE_THETA = 10000.0
GLA_HEADS = 4
GLA_DK = 128
GLA_DV = 256
GLA_GATE_RANK = 16
GLA_TAU = 16.0
MEM_HEADS = 4
MEM_HEAD_DIM = 128
N_CHIPS = 4
N_DEV = 8

ADAM_LR = 0.001
ADAM_B1 = 0.9
ADAM_B2 = 0.999
ADAM_EPS = 1e-08
ADAM_WD = 0.01
ADAM_STEP = 10

VMEM_LIMIT = 56 * 1024 * 1024


def _cparams(sem=None):
    if sem is None:
        return pltpu.CompilerParams(vmem_limit_bytes=VMEM_LIMIT)
    return pltpu.CompilerParams(dimension_semantics=sem, vmem_limit_bytes=VMEM_LIMIT)


def _tile(dim, pref, unit=LANE):
    if dim <= pref:
        return dim
    t = (pref // unit) * unit
    while t > unit and dim % t:
        t -= unit
    assert dim % t == 0, (dim, pref, unit)
    return t


class Comm:
    def __init__(self, ins, out_shapes, nsem, start, wait, aliases=None):
        self.ins, self.out_shapes, self.nsem = list(ins), list(out_shapes), nsem
        self.start, self.wait, self.aliases = start, wait, dict(aliases or {})


def merge_comms(a, b):
    ai, ao = len(a.ins), len(a.out_shapes)

    def start(ins, outs, send, recv, base):
        a.start(ins[:ai], outs[:ao], send, recv, base)
        b.start(ins[ai:], outs[ao:], send, recv, base + a.nsem)

    def wait(ins, outs, send, recv, base):
        a.wait(ins[:ai], outs[:ao], send, recv, base)
        b.wait(ins[ai:], outs[ao:], send, recv, base + a.nsem)

    aliases = dict(a.aliases)
    aliases.update({ai + i: ao + o for i, o in b.aliases.items()})
    return Comm(a.ins + b.ins, a.out_shapes + b.out_shapes, a.nsem + b.nsem, start, wait, aliases)


def run_comm(comm, *, name):
    ni, no = len(comm.ins), len(comm.out_shapes)

    def body(*refs):
        ins, outs = refs[:ni], refs[ni:ni + no]
        send, recv = refs[ni + no:]
        comm.start(ins, outs, send, recv, 0)
        comm.wait(ins, outs, send, recv, 0)

    return pl.pallas_call(
        body, name=name, in_specs=[ANY] * ni, out_specs=[ANY] * no, out_shape=comm.out_shapes,
        input_output_aliases=comm.aliases,
        scratch_shapes=[pltpu.SemaphoreType.DMA((comm.nsem,)), pltpu.SemaphoreType.DMA((comm.nsem,))])(*comm.ins)


HBM = pl.BlockSpec(memory_space=pltpu.HBM)
SEM = pl.BlockSpec(memory_space=pltpu.SEMAPHORE)


def start_comm(comm, *, name, after=None):
    assert not comm.aliases
    ni, no = len(comm.ins), len(comm.out_shapes)
    tail = [] if after is None else [after]

    def body(*refs):
        srcs, lands = refs[:ni], refs[ni:ni + no]
        send, recv = refs[ni + no + len(tail)], refs[ni + no + len(tail) + 1]
        token = refs[-1]
        comm.start(srcs, lands, send, recv, 0)
        token[...] = jnp.zeros_like(token)

    through = [pltpu.HBM(a.shape, a.dtype) for a in comm.ins] + [pltpu.HBM(s.shape, s.dtype) for s in comm.out_shapes]
    ops = [pltpu.with_memory_space_constraint(a, pltpu.HBM) for a in comm.ins]
    ops += [pltpu.with_memory_space_constraint(lax.empty(s.shape, s.dtype), pltpu.HBM) for s in comm.out_shapes]
    ops += tail
    res = pl.pallas_call(
        body, name=name, in_specs=[HBM] * (ni + no) + [ANY] * len(tail),
        out_shape=[pltpu.SemaphoreType.DMA((comm.nsem,)), pltpu.SemaphoreType.DMA((comm.nsem,))] + through
        + [jax.ShapeDtypeStruct((8, LANE), F32)],
        out_specs=[SEM, SEM] + [HBM] * (ni + no) + [pl.BlockSpec(memory_space=pltpu.VMEM)],
        input_output_aliases={i: 2 + i for i in range(ni + no)},
        compiler_params=pltpu.CompilerParams(has_side_effects=pltpu.SideEffectType.DATAFLOW_SIDE_EFFECTING))(*ops)
    return res[0], res[1], list(res[2:2 + ni]), list(res[2 + ni:2 + ni + no]), res[-1]


def wait_comm(comm, started, after, *, name):
    send, recv, srcs, lands, _ = started
    ni, no = len(srcs), len(lands)
    after = list(after) if isinstance(after, (list, tuple)) else [after]

    def body(*refs):
        comm.wait(refs[:ni], refs[ni:ni + no], refs[ni + no], refs[ni + no + 1], 0)

    res = pl.pallas_call(
        body, name=name, in_specs=[HBM] * (ni + no) + [SEM, SEM] + [ANY] * len(after),
        out_shape=[pltpu.HBM(a.shape, a.dtype) for a in srcs + lands], out_specs=[HBM] * (ni + no),
        input_output_aliases={i: i for i in range(ni + no)},
        compiler_params=pltpu.CompilerParams(has_side_effects=pltpu.SideEffectType.DATAFLOW_SIDE_EFFECTING),
    )(*srcs, *lands, send, recv, *after)
    return list(res[ni:])


def _pcall(body, ops, *, name, grid, in_specs, out_specs, out_shape, sem, scratch_shapes=(), comm=None, behind=None):
    if behind is not None:
        n_real, inner = len(ops), body
        ops, in_specs = list(ops) + [behind], list(in_specs) + [ANY]

        def body(*refs):
            inner(*refs[:n_real], *refs[n_real + 1:])

    if comm is None:
        return pl.pallas_call(body, name=name, grid=grid, in_specs=in_specs, out_specs=out_specs, out_shape=out_shape,
                              scratch_shapes=list(scratch_shapes), compiler_params=_cparams(sem))(*ops)
    multi = isinstance(out_shape, (list, tuple))
    k_out_shape = list(out_shape) if multi else [out_shape]
    k_out_specs = list(out_specs) if multi else [out_specs]
    nki, nko, nks = len(ops), len(k_out_shape), len(scratch_shapes)
    nci, nco = len(comm.ins), len(comm.out_shapes)

    def wrapped(*refs):
        p = 0
        k_in = refs[p:p + nki]; p += nki
        c_in = refs[p:p + nci]; p += nci
        k_out = refs[p:p + nko]; p += nko
        c_out = refs[p:p + nco]; p += nco
        k_scr = refs[p:p + nks]; p += nks
        send, recv = refs[p:]
        first = pl.program_id(0) == 0
        last = pl.program_id(0) == grid[0] - 1
        for a in range(1, len(grid)):
            first = jnp.logical_and(first, pl.program_id(a) == 0)
            last = jnp.logical_and(last, pl.program_id(a) == grid[a] - 1)

        @pl.when(first)
        def _():
            comm.start(c_in, c_out, send, recv, 0)

        body(*k_in, *k_out, *k_scr)

        @pl.when(last)
        def _():
            comm.wait(c_in, c_out, send, recv, 0)

    res = pl.pallas_call(
        wrapped, name=name, grid=grid, in_specs=list(in_specs) + [ANY] * nci, out_specs=k_out_specs + [ANY] * nco,
        out_shape=k_out_shape + comm.out_shapes,
        input_output_aliases={nki + i: nko + o for i, o in comm.aliases.items()},
        scratch_shapes=list(scratch_shapes) + [pltpu.SemaphoreType.DMA((comm.nsem,)), pltpu.SemaphoreType.DMA((comm.nsem,))],
        compiler_params=_cparams(("arbitrary",) * len(grid)))(*ops, *comm.ins)
    k_res = list(res[:nko]) if multi else res[0]
    return k_res, list(res[nko:])


_DIMS = {"nn": (((1,), (0,)), ((), ())), "nt": (((1,), (1,)), ((), ())), "tn": (((0,), (0,)), ((), ()))}


def _blockspec(shape, index, rows_inner):
    return pl.BlockSpec(shape, (lambda j, i, k: index(i, j, k)) if rows_inner else index)


def mm(pairs, mode, out_dtype, *, name, alpha=1.0, res=None, tm=1024, tn=1024, tk=4096, b_slots=False, out_slots=False,
       rows_inner=False, comm=None, behind=None):
    a0, b0 = pairs[0]
    if b_slots:
        b_rows, b_cols = b0.shape[1], N_CHIPS * b0.shape[2]
    else:
        b_rows, b_cols = b0.shape
    (M, K) = a0.shape[::-1] if mode == "tn" else a0.shape
    N = b_rows if mode == "nt" else b_cols
    shard = (b_cols if b_slots else N) // N_CHIPS
    tm = _tile(M, tm)
    tn = _tile(shard if (out_slots or (b_slots and mode != "nt")) else N, tn)
    tk = _tile(shard if (b_slots and mode == "nt") else K, tk)
    nk = K // tk
    npairs = len(pairs)
    dims = _DIMS[mode]
    spec = functools.partial(_blockspec, rows_inner=rows_inner)
    if mode == "tn":
        a_spec = spec((tk, tm), lambda i, j, k: (k, i))
    else:
        a_spec = spec((tm, tk), lambda i, j, k: (i, k))
    per = shard // (tk if mode == "nt" else tn)
    if mode == "nt":
        b_spec = (spec((None, tn, tk), lambda i, j, k: (k // per, j, k % per)) if b_slots else
                  spec((tn, tk), lambda i, j, k: (j, k)))
    else:
        b_spec = (spec((None, tk, tn), lambda i, j, k: (j // per, k, j % per)) if b_slots else
                  spec((tk, tn), lambda i, j, k: (k, j)))
    if out_slots:
        assert res is None and mode != "nt"
        o_spec = spec((None, tm, tn), lambda i, j, k: (j // per, i, j % per))
        out_sds = jax.ShapeDtypeStruct((N_CHIPS, M, shard), out_dtype)
    else:
        o_spec = spec((tm, tn), lambda i, j, k: (i, j))
        out_sds = jax.ShapeDtypeStruct((M, N), out_dtype)
    has_res = res is not None

    def body(*refs):
        ab = refs[:2 * npairs]
        res_ref = refs[2 * npairs] if has_res else None
        o_ref = refs[2 * npairs + int(has_res)]

        def products():
            r = None
            for p in range(npairs):
                d = lax.dot_general(ab[2 * p][...].astype(MXU_DTYPE), ab[2 * p + 1][...].astype(MXU_DTYPE), dims,
                                    preferred_element_type=F32)
                r = d if r is None else r + d
            return r

        def finish(r):
            if alpha != 1.0:
                r = r * alpha
            if has_res:
                r = res_ref[...].astype(F32) + r
            o_ref[...] = r.astype(out_dtype)

        if nk == 1:
            finish(products())
            return
        acc = refs[-1]
        k = pl.program_id(2)

        @pl.when(k == 0)
        def _():
            acc[...] = jnp.zeros_like(acc)

        acc[...] += products()

        @pl.when(k == nk - 1)
        def _():
            finish(acc[...])

    ops, specs = [], []
    for a, b in pairs:
        ops += [a, b]
        specs += [a_spec, b_spec]
    if has_res:
        ops.append(res)
        specs.append(o_spec)
    blocks = (N // tn, M // tm) if rows_inner else (M // tm, N // tn)
    return _pcall(body, ops, name=name, grid=blocks + (nk,), in_specs=specs, out_specs=o_spec, out_shape=out_sds,
                  scratch_shapes=[pltpu.VMEM((tm, tn), F32)] if nk > 1 else [],
                  sem=("parallel", "parallel", "arbitrary"), comm=comm, behind=behind)


def _sigmoid(x):
    return 1.0 / (1.0 + jnp.exp(-x))


def ffn_up(n, wg, wu, *, name, tm=512, tn=1408, comm=None, behind=None):
    M, K = n.shape
    shard = wg.shape[2]
    N = N_CHIPS * shard
    tm, tn = _tile(M, tm), _tile(shard, tn)
    per = shard // tn
    w_spec = pl.BlockSpec((None, K, tn), lambda j, i: (j // per, 0, j % per))

    def body(n_ref, wg_ref, wu_ref, g_ref, u_ref, a_ref):
        nv = n_ref[...].astype(MXU_DTYPE)
        g = jnp.dot(nv, wg_ref[...].astype(MXU_DTYPE), preferred_element_type=F32)
        u = jnp.dot(nv, wu_ref[...].astype(MXU_DTYPE), preferred_element_type=F32)
        g_ref[...] = g.astype(g_ref.dtype)
        u_ref[...] = u.astype(u_ref.dtype)
        a_ref[...] = (g * _sigmoid(g) * u).astype(a_ref.dtype)

    o_spec = pl.BlockSpec((tm, tn), lambda j, i: (i, j))
    sds = jax.ShapeDtypeStruct((M, N), BF16)
    return _pcall(
        body, [n, wg, wu], name=name, grid=(N // tn, M // tm),
        in_specs=[pl.BlockSpec((tm, K), lambda j, i: (i, 0)), w_spec, w_spec],
        out_specs=[o_spec, o_spec, o_spec], out_shape=[sds, sds, sds], sem=("parallel", "parallel"), comm=comm,
        behind=behind)


def ffn_dact(dy, wd, gate, up, alpha, *, name, tm=512, tn=1408, comm=None, behind=None):
    M, K = dy.shape
    N = wd.shape[0]
    tm, tn = _tile(M, tm), _tile(N, tn)

    def body(dy_ref, wd_ref, g_ref, u_ref, dg_ref, du_ref):
        da = lax.dot_general(dy_ref[...].astype(MXU_DTYPE), wd_ref[...].astype(MXU_DTYPE), _DIMS["nt"],
                             preferred_element_type=F32) * alpha
        g = g_ref[...].astype(F32)
        u = u_ref[...].astype(F32)
        s = _sigmoid(g)
        du_ref[...] = (da * (g * s)).astype(du_ref.dtype)
        dg_ref[...] = (da * u * (s * (1.0 + g * (1.0 - s)))).astype(dg_ref.dtype)

    o_spec = pl.BlockSpec((tm, tn), lambda j, i: (i, j))
    sds = jax.ShapeDtypeStruct((M, N), BF16)
    return _pcall(
        body, [dy, wd, gate, up], name=name, grid=(N // tn, M // tm),
        in_specs=[pl.BlockSpec((tm, K), lambda j, i: (i, 0)), pl.BlockSpec((tn, K), lambda j, i: (j, 0)), o_spec, o_spec],
        out_specs=[o_spec, o_spec], out_shape=[sds, sds], sem=("parallel", "parallel"), comm=comm, behind=behind)


def _window(width, off, ext):
    ww = LANE
    while ww < width:
        if ww >= ext and off // ww == (off + ext - 1) // ww and width % ww == 0:
            break
        ww *= 2
    else:
        ww = width
    return ww, off // ww, off - (off // ww) * ww


class V:
    def __init__(self, arr, off=0, w=None, hs=0, diff=True):
        self.arr, self.off, self.hs, self.diff = arr, off, hs, diff
        self.w = arr.shape[1] - off if w is None else w

    def window(self, heads, tr):
        ww, blk, inner = _window(self.arr.shape[1], self.off, (heads - 1) * self.hs + self.w)
        return pl.BlockSpec((tr, ww), lambda i, blk=blk: (i, blk)), inner


def _const_spec(c):
    return pl.BlockSpec(c.shape, lambda i: (0, 0))


def row_fwd(fn, rows, consts, outs, out_map, *, heads=1, tr=256, name):
    S = rows[0].arr.shape[0]
    tr = _tile(S, tr, 8)
    wins = [v.window(heads, tr) for v in rows]
    nr, nc = len(rows), len(consts)

    def body(*refs):
        row_refs, const_refs, out_refs = refs[:nr], refs[nr:nr + nc], refs[nr + nc:]
        cv = [c[...].astype(F32) for c in const_refs]
        for h in range(heads):
            rv = []
            for v, (_, io), r in zip(rows, wins, row_refs):
                lo = io + h * v.hs
                rv.append(r[:, lo:lo + v.w].astype(F32))
            res = fn(*rv, *cv)
            for (ai, off, hs), o in zip(out_map, res):
                lo = off + h * hs
                out_refs[ai][:, lo:lo + o.shape[1]] = o.astype(out_refs[ai].dtype)

    return pl.pallas_call(
        body, name=name, grid=(S // tr,),
        in_specs=[w[0] for w in wins] + [_const_spec(c) for c in consts],
        out_specs=[pl.BlockSpec((tr, w), lambda i: (i, 0)) for w, _ in outs],
        out_shape=[jax.ShapeDtypeStruct((S, w), d) for w, d in outs],
        compiler_params=_cparams(("parallel",)))(*[v.arr for v in rows], *consts)


def row_bwd(fn, rows, consts, cots, *, const_diff, heads=1, tr=256, res=None, row_dtype=F32, pack=None, pack_width=0,
            fills=(), name):
    S = rows[0].arr.shape[0]
    tr = _tile(S, tr, 8)
    pack = dict(pack or {})
    nr, nc, nct, nf = len(rows), len(consts), len(cots), len(fills)
    wins = [v.window(heads, tr) for v in rows]
    cwins = [v.window(heads, tr) for v in cots]
    fwins = [v.window(heads, tr) for v, _, _ in fills]
    drows = [k for k, v in enumerate(rows) if v.diff]
    dconsts = [k for k in range(nc) if const_diff[k]]
    has_res = res is not None
    assert not (has_res and 0 in pack)
    widths = [pack_width] if pack else []
    place = []
    for n, k in enumerate(drows):
        if n in pack:
            place.append((0,) + tuple(pack[n]))
        else:
            place.append((len(widths), 0, rows[k].w))
            widths.append(rows[k].w * (heads if rows[k].hs else 1))

    def body(*refs):
        row_refs = refs[:nr]
        const_refs = refs[nr:nr + nc]
        cot_refs = refs[nr + nc:nr + nc + nct]
        p = nr + nc + nct
        fill_refs = refs[p:p + nf]
        p += nf
        res_ref = refs[p] if has_res else None
        p += int(has_res)
        grow_refs = refs[p:p + len(widths)]
        gconst_refs = refs[p + len(widths):]
        i = pl.program_id(0)
        cv = [c[...].astype(F32) for c in const_refs]
        shared = [None] * len(drows)
        gc_sum = [None] * len(dconsts)
        for h in range(heads):
            rv = []
            for v, (_, io), r in zip(rows, wins, row_refs):
                lo = io + h * v.hs
                rv.append(r[:, lo:lo + v.w].astype(F32))
            ct = []
            for v, (_, io), r in zip(cots, cwins, cot_refs):
                lo = io + h * v.hs
                ct.append(r[:, lo:lo + v.w].astype(F32))

            def closed(*d):
                rr, cc = list(rv), list(cv)
                for k, val in zip(drows, d[:len(drows)]):
                    rr[k] = val
                for k, val in zip(dconsts, d[len(drows):]):
                    cc[k] = val
                return tuple(fn(*rr, *cc))

            _, vjp = jax.vjp(closed, *[rv[k] for k in drows], *[cv[k] for k in dconsts])
            grads = vjp(tuple(ct))
            for n, k in enumerate(drows):
                g = grads[n]
                if rows[k].hs == 0 and heads > 1:
                    shared[n] = g if shared[n] is None else shared[n] + g
                else:
                    if n == 0 and has_res:
                        g = g + res_ref[:, h * rows[k].w:(h + 1) * rows[k].w].astype(F32)
                    out, off, hs = place[n]
                    grow_refs[out][:, off + h * hs:off + h * hs + rows[k].w] = g.astype(row_dtype)
            for (v, off, hs), (_, io), r in zip(fills, fwins, fill_refs):
                lo = io + h * v.hs
                grow_refs[0][:, off + h * hs:off + h * hs + v.w] = r[:, lo:lo + v.w].astype(row_dtype)
            for n in range(len(dconsts)):
                g = grads[len(drows) + n]
                gc_sum[n] = g if gc_sum[n] is None else gc_sum[n] + g
        for n, k in enumerate(drows):
            if shared[n] is not None:
                g = shared[n]
                if n == 0 and has_res:
                    g = g + res_ref[...].astype(F32)
                grow_refs[place[n][0]][...] = g.astype(row_dtype)

        @pl.when(i == 0)
        def _():
            for n in range(len(dconsts)):
                gconst_refs[n][...] = gc_sum[n]

        @pl.when(i > 0)
        def _():
            for n in range(len(dconsts)):
                gconst_refs[n][...] += gc_sum[n]

    in_specs = [w[0] for w in wins] + [_const_spec(c) for c in consts] + [w[0] for w in cwins] + [w[0] for w in fwins]
    ops = [v.arr for v in rows] + list(consts) + [v.arr for v in cots] + [v.arr for v, _, _ in fills]
    if has_res:
        in_specs.append(pl.BlockSpec((tr, widths[0]), lambda i: (i, 0)))
        ops.append(res)
    out_specs = [pl.BlockSpec((tr, w), lambda i: (i, 0)) for w in widths]
    out_shape = [jax.ShapeDtypeStruct((S, w), row_dtype) for w in widths]
    for k in dconsts:
        out_specs.append(_const_spec(consts[k]))
        out_shape.append(jax.ShapeDtypeStruct(consts[k].shape, F32))
    return pl.pallas_call(body, name=name, grid=(S // tr,), in_specs=in_specs, out_specs=out_specs,
                          out_shape=out_shape, compiler_params=_cparams(("arbitrary",)))(*ops)


def _rms(x, g, n=None):
    n = x.shape[-1] if n is None else n
    ms = jnp.sum(x * x, axis=-1, keepdims=True) * (1.0 / n)
    return x * lax.rsqrt(ms + EPS) * g


def rms_fn(x, g):
    return (_rms(x, g),)


def qk_prep_fn(nope, rope, cos, sin, gn, gr, rot):
    ms = (jnp.sum(nope * nope, axis=-1, keepdims=True) + jnp.sum(rope * rope, axis=-1, keepdims=True)) * (1.0 / MLA_QK)
    r = lax.rsqrt(ms + EPS)
    on = nope * r * gn
    orr = rope * r * gr
    turned = jnp.dot(orr, rot, precision=lax.Precision.HIGHEST, preferred_element_type=F32)
    return on, orr * cos + turned * sin


def gla_out_fn(o, zr, g):
    return (_rms(o, g) * (zr * _sigmoid(zr)),)


def gate_fn(pre, b):
    t = pre + b
    return ((jnp.minimum(t, 0.0) - jnp.log(1.0 + jnp.exp(-jnp.abs(t)))) * (1.0 / GLA_TAU),)


def _attn_probs(q_ref, k_ref, scale, q0, kext):
    s = lax.dot_general(q_ref[...].astype(MXU_DTYPE), k_ref[0:kext, :].astype(MXU_DTYPE), _DIMS["nt"],
                        preferred_element_type=F32) * scale
    if q0 is not None:
        qc = (q0 + lax.broadcasted_iota(jnp.int32, s.shape, 0)) // CHUNK
        kc = lax.broadcasted_iota(jnp.int32, s.shape, 1) // CHUNK
        s = jnp.where(kc <= qc, s, -1e30)
    m = jnp.max(s, axis=-1, keepdims=True)
    e = jnp.exp(s - m)
    return e / jnp.sum(e, axis=-1, keepdims=True)


def _per_query_block(one, causal, nq, tq, Sk):
    if not causal:
        one(None, Sk, None)
        return
    assert tq % CHUNK == 0
    for ib in range(nq):
        pl.when(pl.program_id(1) == ib)(functools.partial(one, ib * tq, min(Sk, (ib + 1) * tq), ib))


def attn_fwd(q, k, v, *, heads, dk, dv, v_off, v_hs, scale, causal, name, tq=256, comm=None):
    Sq, Sk = q.shape[0], k.shape[0]
    tq = _tile(Sq, tq, 8)

    def body(q_ref, k_ref, v_ref, o_ref):
        def one(q0, kext, ib):
            p = _attn_probs(q_ref, k_ref, scale, q0, kext)
            o_ref[...] = jnp.dot(p.astype(MXU_DTYPE), v_ref[0:kext, :].astype(MXU_DTYPE),
                                 preferred_element_type=F32).astype(o_ref.dtype)

        _per_query_block(one, causal, Sq // tq, tq, Sk)

    return _pcall(
        body, [q, k, v], name=name, grid=(heads, Sq // tq),
        in_specs=[pl.BlockSpec((tq, dk), lambda h, i: (i, h)), pl.BlockSpec((Sk, dk), lambda h, i: (0, h)),
                  pl.BlockSpec((Sk, dv), lambda h, i: (0, v_off + h * v_hs))],
        out_specs=pl.BlockSpec((tq, dv), lambda h, i: (i, h)),
        out_shape=jax.ShapeDtypeStruct((Sq, heads * dv), BF16), sem=("parallel", "parallel"), comm=comm)


def attn_bwd(q, k, v, do, *, heads, dk, dv, v_off, v_hs, scale, causal, name, tq=256, comm=None):
    Sq, Sk = q.shape[0], k.shape[0]
    tq = _tile(Sq, tq, 8)

    def body(q_ref, k_ref, v_ref, do_ref, dq_ref, dk_ref, dv_ref):
        @pl.when(pl.program_id(1) == 0)
        def _():
            dk_ref[...] = jnp.zeros_like(dk_ref)
            dv_ref[...] = jnp.zeros_like(dv_ref)

        def one(q0, kext, ib):
            p = _attn_probs(q_ref, k_ref, scale, q0, kext)
            dob = do_ref[...].astype(MXU_DTYPE)
            dp = lax.dot_general(dob, v_ref[0:kext, :].astype(MXU_DTYPE), _DIMS["nt"], preferred_element_type=F32)
            delta = jnp.sum(p * dp, axis=-1, keepdims=True)
            ds = (p * (dp - delta) * scale).astype(MXU_DTYPE)
            dq_ref[...] = jnp.dot(ds, k_ref[0:kext, :].astype(MXU_DTYPE), preferred_element_type=F32)
            dk_ref[0:kext, :] += lax.dot_general(ds, q_ref[...].astype(MXU_DTYPE), _DIMS["tn"],
                                                 preferred_element_type=F32)
            dv_ref[0:kext, :] += lax.dot_general(p.astype(MXU_DTYPE), dob, _DIMS["tn"], preferred_element_type=F32)

        _per_query_block(one, causal, Sq // tq, tq, Sk)

    return _pcall(
        body, [q, k, v, do], name=name, grid=(heads, Sq // tq),
        in_specs=[pl.BlockSpec((tq, dk), lambda h, i: (i, h)), pl.BlockSpec((Sk, dk), lambda h, i: (0, h)),
                  pl.BlockSpec((Sk, dv), lambda h, i: (0, v_off + h * v_hs)),
                  pl.BlockSpec((tq, dv), lambda h, i: (i, h))],
        out_specs=[pl.BlockSpec((tq, dk), lambda h, i: (i, h)), pl.BlockSpec((Sk, dk), lambda h, i: (0, h)),
                   pl.BlockSpec((Sk, dv), lambda h, i: (0, h))],
        out_shape=[jax.ShapeDtypeStruct((Sq, heads * dk), F32), jax.ShapeDtypeStruct((Sk, heads * dk), F32),
                   jax.ShapeDtypeStruct((Sk, heads * dv), F32)],
        sem=("parallel", "arbitrary"), comm=comm)


def _gla_chunk(k, g, tri_ref):
    b = jnp.dot(tri_ref[...], g, precision=lax.Precision.HIGHEST, preferred_element_type=F32)
    b_end = jnp.sum(g, axis=0, keepdims=True)
    e = jnp.exp(b_end - b)
    return k * e, e, jnp.exp(b_end)


def _gla_windows(z, q_off, k_off, v_off, rows_of):
    H, DK, DV = GLA_HEADS, GLA_DK, GLA_DV
    specs, inner = [], []
    for off, ext in ((q_off, H * DK), (k_off, H * DK), (v_off, H * DV)):
        ww, blk, io = _window(z.shape[1], off, ext)
        specs.append(pl.BlockSpec((CHUNK, ww), lambda c, blk=blk: (rows_of(c), blk)))
        inner.append(io)
    return specs, inner


def gla_fwd(z, la, tri, *, q_off, k_off, v_off, name, comm=None):
    S = z.shape[0]
    nchunk = S // CHUNK
    H, DK, DV = GLA_HEADS, GLA_DK, GLA_DV
    qscale = DK ** -0.5
    zspecs, (qi, ki, vi) = _gla_windows(z, q_off, k_off, v_off, lambda c: c)

    def body(q_ref, k_ref, v_ref, la_ref, tri_ref, o_ref, st_ref, state):
        @pl.when(pl.program_id(0) == 0)
        def _():
            state[...] = jnp.zeros_like(state)

        for h in range(H):
            dks, dvs = slice(h * DK, (h + 1) * DK), slice(h * DV, (h + 1) * DV)
            k = k_ref[:, ki + h * DK:ki + (h + 1) * DK].astype(F32)
            v = v_ref[:, vi + h * DV:vi + (h + 1) * DV]
            q = q_ref[:, qi + h * DK:qi + (h + 1) * DK].astype(F32)
            kdec, _, decay = _gla_chunk(k, la_ref[:, dks].astype(F32), tri_ref)
            ut = lax.dot_general(v.astype(MXU_DTYPE), kdec.astype(MXU_DTYPE), _DIMS["tn"], preferred_element_type=F32)
            new = state[h] * decay + ut
            state[h] = new
            st_ref[h] = new
            qs = (q * qscale).astype(MXU_DTYPE)
            o_ref[:, dvs] = lax.dot_general(qs, new.astype(MXU_DTYPE), _DIMS["nt"], preferred_element_type=F32)

    return _pcall(
        body, [z, z, z, la, tri], name=name, grid=(nchunk,),
        in_specs=zspecs + [pl.BlockSpec((CHUNK, H * DK), lambda c: (c, 0)), pl.BlockSpec((CHUNK, CHUNK), lambda c: (0, 0))],
        out_specs=[pl.BlockSpec((CHUNK, H * DV), lambda c: (c, 0)),
                   pl.BlockSpec((H, None, DV, DK), lambda c: (0, c, 0, 0))],
        out_shape=[jax.ShapeDtypeStruct((S, H * DV), F32), jax.ShapeDtypeStruct((H, nchunk, DV, DK), F32)],
        scratch_shapes=[pltpu.VMEM((H, DV, DK), F32)], sem=("arbitrary",), comm=comm)


def gla_bwd(z, la, tri, trit, states, do, *, q_off, k_off, v_off, name, comm=None):
    S = z.shape[0]
    nchunk = S // CHUNK
    H, DK, DV = GLA_HEADS, GLA_DK, GLA_DV
    qscale = DK ** -0.5
    last = nchunk - 1
    zspecs, (qi, ki, vi) = _gla_windows(z, q_off, k_off, v_off, lambda c: last - c)

    def body(q_ref, k_ref, v_ref, la_ref, tri_ref, trit_ref, st_ref, sp_ref, do_ref, dq_ref, dk_ref, dv_ref, dla_ref,
             dstate):
        c = pl.program_id(0)
        cc = last - c

        @pl.when(c == 0)
        def _():
            dstate[...] = jnp.zeros_like(dstate)

        for h in range(H):
            dks, dvs = slice(h * DK, (h + 1) * DK), slice(h * DV, (h + 1) * DV)
            kf = k_ref[:, ki + h * DK:ki + (h + 1) * DK].astype(F32)
            vb16 = v_ref[:, vi + h * DV:vi + (h + 1) * DV].astype(MXU_DTYPE)
            q = q_ref[:, qi + h * DK:qi + (h + 1) * DK].astype(F32)
            kdec, e, decay = _gla_chunk(kf, la_ref[:, dks].astype(F32), tri_ref)
            dob = do_ref[:, dvs].astype(MXU_DTYPE)
            stb = st_ref[h].astype(MXU_DTYPE)
            qs = (q * qscale).astype(MXU_DTYPE)
            dq_ref[:, dks] = jnp.dot(dob, stb, preferred_element_type=F32) * qscale
            dst = dstate[h] + lax.dot_general(dob, qs, _DIMS["tn"], preferred_element_type=F32)
            prev = jnp.where(cc > 0, sp_ref[h], 0.0)
            ddecay = jnp.sum(dst * prev, axis=0, keepdims=True)
            dstate[h] = dst * decay
            dub = dst.astype(MXU_DTYPE)
            dv_ref[:, dvs] = lax.dot_general(kdec.astype(MXU_DTYPE), dub, _DIMS["nt"], preferred_element_type=F32)
            dkdec = jnp.dot(vb16, dub, preferred_element_type=F32)
            dk_ref[:, dks] = dkdec * e
            w = dkdec * kf * e
            db_end = jnp.sum(w, axis=0, keepdims=True) + ddecay * decay
            dla_ref[:, dks] = db_end - jnp.dot(trit_ref[...], w, precision=lax.Precision.HIGHEST,
                                               preferred_element_type=F32)

    def rows(width):
        return pl.BlockSpec((CHUNK, width), lambda c: (last - c, 0))

    square = pl.BlockSpec((CHUNK, CHUNK), lambda c: (0, 0))
    return _pcall(
        body, [z, z, z, la, tri, trit, states, states, do], name=name, grid=(nchunk,),
        in_specs=zspecs + [rows(H * DK), square, square,
                           pl.BlockSpec((H, None, DV, DK), lambda c: (0, last - c, 0, 0)),
                           pl.BlockSpec((H, None, DV, DK), lambda c: (0, jnp.maximum(last - c - 1, 0), 0, 0)),
                           rows(H * DV)],
        out_specs=[rows(H * DK), rows(H * DK), rows(H * DV), rows(H * DK)],
        out_shape=[jax.ShapeDtypeStruct((S, H * DK), F32), jax.ShapeDtypeStruct((S, H * DK), F32),
                   jax.ShapeDtypeStruct((S, H * DV), F32), jax.ShapeDtypeStruct((S, H * DK), F32)],
        scratch_shapes=[pltpu.VMEM((H, DV, DK), F32)], sem=("arbitrary",), comm=comm)


def loss_head(y, target, *, name, tr=256):
    S, D = y.shape
    tr = _tile(S, tr, 8)

    def body(y_ref, t_ref, dy_ref, loss_ref):
        i = pl.program_id(0)
        err = y_ref[...] - t_ref[...]
        dy_ref[...] = err * (1.0 / D)
        part = jnp.zeros((1, LANE), F32) + 0.5 * jnp.sum(jnp.sum(err * err, axis=-1, keepdims=True) * (1.0 / D))

        @pl.when(i == 0)
        def _():
            loss_ref[...] = part

        @pl.when(i > 0)
        def _():
            loss_ref[...] += part

    spec = pl.BlockSpec((tr, D), lambda i: (i, 0))
    return pl.pallas_call(
        body, name=name, grid=(S // tr,), in_specs=[spec, spec],
        out_specs=[spec, pl.BlockSpec((1, LANE), lambda i: (0, 0))],
        out_shape=[jax.ShapeDtypeStruct((S, D), F32), jax.ShapeDtypeStruct((1, LANE), F32)],
        compiler_params=_cparams(("arbitrary",)))(y, target)


def _core_index():
    return lax.axis_index("c").astype(jnp.int32).reshape(1)


def _chip_slots():
    x, y, c = lax.axis_index("x"), lax.axis_index("y"), lax.axis_index("c")
    return jnp.stack([2 * x + y, 2 * (1 - x) + y, 2 * x + (1 - y), 2 * (1 - x) + (1 - y), c]).astype(jnp.int32)


def sum_chip_parts(own, parts, *, name, tr=1024):
    _, R, C = own.shape
    tr = _tile(R, tr, 8)

    def body(idx_ref, o_ref, p0_ref, p1_ref, p2_ref, out_ref):
        acc = o_ref[...].astype(F32) + p0_ref[...].astype(F32)
        acc = acc + p1_ref[...].astype(F32)
        out_ref[...] = acc + p2_ref[...].astype(F32)

    def slot(k):
        return pl.BlockSpec((None, tr, C), lambda i, idx: (idx[k], i, 0))

    grid_spec = pltpu.PrefetchScalarGridSpec(num_scalar_prefetch=1, grid=(R // tr,),
                                             in_specs=[slot(0), slot(1), slot(2), slot(3)], out_specs=slot(4))
    return pl.pallas_call(body, name=name, grid_spec=grid_spec, out_shape=jax.ShapeDtypeStruct((2, R, C), F32),
                          compiler_params=_cparams(("parallel",)))(_chip_slots(), own, parts, parts, parts)


def add_own_half(g, got, out_dtype, *, name, tr=1024):
    n, _, R, C = g.shape
    tr = _tile(R, tr, 8)

    def body(c_ref, a_ref, b_ref, o_ref):
        o_ref[...] = (a_ref[...].astype(F32) + b_ref[...].astype(F32)).astype(out_dtype)

    spec = pl.BlockSpec((None, tr, C), lambda s, i, c: (s, i, 0))
    grid_spec = pltpu.PrefetchScalarGridSpec(
        num_scalar_prefetch=1, grid=(n, R // tr),
        in_specs=[pl.BlockSpec((None, None, tr, C), lambda s, i, c: (s, c[0], i, 0)), spec], out_specs=spec)
    return pl.pallas_call(body, name=name, grid_spec=grid_spec, out_shape=jax.ShapeDtypeStruct((n, R, C), out_dtype),
                          compiler_params=_cparams(("parallel", "parallel")))(_core_index(), g, got)


def adamw(items, *, name, max_steps=16, behind=None):
    c1 = 1.0 / (1.0 - ADAM_B1 ** ADAM_STEP)
    c2 = 1.0 / (1.0 - ADAM_B2 ** ADAM_STEP)
    n = len(items)
    steps = max_steps
    while steps > 1 and any(it[0].shape[0] % (8 * steps) for it in items):
        steps //= 2
    tail = [] if behind is None else [behind]

    def body(*refs):
        for a in range(n):
            w_ref, g_ref, m_ref, v_ref = refs[4 * a:4 * a + 4]
            go_ref, d_ref, nm_ref, nv_ref = refs[4 * n + len(tail) + 4 * a:4 * n + len(tail) + 4 * a + 4]
            gv = g_ref[...]
            go_ref[...] = gv
            nm = ADAM_B1 * m_ref[...] + (1.0 - ADAM_B1) * gv
            nv = ADAM_B2 * v_ref[...] + (1.0 - ADAM_B2) * (gv * gv)
            nm_ref[...] = nm
            nv_ref[...] = nv
            d_ref[...] = -ADAM_LR * ((nm * c1) / (jnp.sqrt(nv * c2) + ADAM_EPS) + ADAM_WD * w_ref[...])

    ops, in_specs, out_specs, out_shape = [], [], [], []
    for w, g, m, v in items:
        R, C = w.shape
        spec = pl.BlockSpec((R // steps, C), lambda i: (i, 0))
        ops += [w, g, m, v]
        in_specs += [spec] * 4
        out_specs += [spec] * 4
        out_shape += [jax.ShapeDtypeStruct((R, C), F32)] * 4
    flat = _pcall(body, ops + tail, name=name, grid=(steps,), in_specs=in_specs + [ANY] * len(tail), out_specs=out_specs,
                  out_shape=out_shape, sem=("parallel",))
    return [tuple(flat[4 * a:4 * a + 4]) for a in range(n)]


def _place():
    x, y, c = lax.axis_index("x"), lax.axis_index("y"), lax.axis_index("c")
    chips = [(1 - x, y), (x, 1 - y), (1 - x, 1 - y)]
    return x, y, c, chips


def _rcopy(src, dst, send, recv, j, to):
    return pltpu.make_async_remote_copy(src_ref=src, dst_ref=dst, send_sem=send.at[j], recv_sem=recv.at[j], device_id=to,
                                        device_id_type=MESH)


def gather_stage1(shards, split):
    n = len(shards)
    ins = [s.reshape(2, s.shape[0] // 2, s.shape[1]) if sp else s for s, sp in zip(shards, split)]
    outs = [jax.ShapeDtypeStruct((N_CHIPS,) + a.shape, a.dtype) for a in ins]

    def start(in_refs, out_refs, send, recv, base):
        x, y, c, chips = _place()
        mine = 2 * x + y
        for i in range(n):
            src = in_refs[i].at[c] if split[i] else in_refs[i]
            dst = out_refs[i].at[mine, c] if split[i] else out_refs[i].at[mine]
            for k, (px, py) in enumerate(chips):
                _rcopy(src, dst, send, recv, base + 3 * i + k, (px, py, c)).start()

    def wait(in_refs, out_refs, send, recv, base):
        x, y, c, chips = _place()
        for i in range(n):
            src = in_refs[i].at[c] if split[i] else in_refs[i]
            for k, (px, py) in enumerate(chips):
                dst = out_refs[i].at[2 * px + py, c] if split[i] else out_refs[i].at[2 * px + py]
                _rcopy(src, dst, send, recv, base + 3 * i + k, (px, py, c)).wait()

    return Comm(ins, outs, 3 * n, start, wait)


def gather_stage2(slots, shards, split):
    n = len(slots)
    own = [s.reshape(2, s.shape[0] // 2, s.shape[1]) if sp else s for s, sp in zip(shards, split)]

    def copies(in_refs, out_refs, send, recv, base):
        x, y, c, chips = _place()
        sib = (x, y, 1 - c)
        for i in range(n):
            j = base + 4 * i
            mine = out_refs[i].at[2 * x + y]
            yield _rcopy(in_refs[n + i], mine, send, recv, j + 3, sib), _rcopy(in_refs[n + i], mine, send, recv, j + 3, sib)
            if split[i]:
                for k, (px, py) in enumerate(chips):
                    s = 2 * px + py
                    yield (_rcopy(in_refs[i].at[s, c], out_refs[i].at[s, c], send, recv, j + k, sib),
                           _rcopy(in_refs[i].at[s, c], out_refs[i].at[s, 1 - c], send, recv, j + k, sib))

    def start(*a):
        for out, _ in copies(*a):
            out.start()

    def wait(*a):
        for _, back in copies(*a):
            back.wait()

    return Comm(list(slots) + own, [jax.ShapeDtypeStruct(s.shape, s.dtype) for s in slots], 4 * n, start, wait,
                {i: i for i in range(n)})


def swap_halves(gs):
    n = len(gs)

    def copies(in_refs, out_refs, send, recv, base):
        x, y, c, _ = _place()
        return [_rcopy(in_refs[i].at[s, 1 - c], out_refs[i].at[s], send, recv, base + N_CHIPS * i + s, (x, y, 1 - c))
                for i in range(n) for s in range(N_CHIPS)]

    def start(*a):
        for cp in copies(*a):
            cp.start()

    def wait(*a):
        for cp in copies(*a):
            cp.wait()

    return Comm(gs, [jax.ShapeDtypeStruct((N_CHIPS,) + g.shape[2:], g.dtype) for g in gs], N_CHIPS * n, start, wait)


def exchange_chips(ps):
    n = len(ps)

    def start(in_refs, out_refs, send, recv, base):
        x, y, c, chips = _place()
        for i in range(n):
            for k, (px, py) in enumerate(chips):
                _rcopy(in_refs[i].at[2 * px + py], out_refs[i].at[2 * x + y], send, recv, base + 3 * i + k,
                       (px, py, c)).start()

    def wait(in_refs, out_refs, send, recv, base):
        x, y, c, chips = _place()
        for i in range(n):
            for k, (px, py) in enumerate(chips):
                _rcopy(in_refs[i].at[2 * px + py], out_refs[i].at[2 * px + py], send, recv, base + 3 * i + k,
                       (px, py, c)).wait()

    return Comm(ps, [jax.ShapeDtypeStruct(p.shape, p.dtype) for p in ps], 3 * n, start, wait)


def join_halves(fs):
    n = len(fs)

    def start(in_refs, out_refs, send, recv, base):
        x, y, c, _ = _place()
        for i in range(n):
            _rcopy(in_refs[i].at[c], out_refs[i].at[c], send, recv, base + i, (x, y, 1 - c)).start()

    def wait(in_refs, out_refs, send, recv, base):
        x, y, c, _ = _place()
        for i in range(n):
            _rcopy(in_refs[i].at[c], out_refs[i].at[1 - c], send, recv, base + i, (x, y, 1 - c)).wait()

    return Comm(fs, [jax.ShapeDtypeStruct(f.shape, f.dtype) for f in fs], n, start, wait, {i: i for i in range(n)})


def allreduce_small(v, *, name):
    m_per, n = v.shape

    def body(x_ref, sum_ref, all_ref, send_sems, recv_sems, local_sem):
        x, y, c, chips = _place()
        me, sibling = (x, y, c), (x, y, 1 - c)

        def rows(px, py, pc):
            return all_ref.at[pl.ds((4 * px + 2 * py + pc) * m_per, m_per), :]

        def copy(k, block, to, src=None):
            return pltpu.make_async_remote_copy(src_ref=rows(*block) if src is None else src, dst_ref=rows(*block),
                                                send_sem=send_sems.at[k], recv_sem=recv_sems.at[k], device_id=to,
                                                device_id_type=MESH)

        mine = pltpu.make_async_copy(x_ref, rows(*me), local_sem)
        mine.start()
        first = [copy(0, me, sibling, src=x_ref)]
        first += [copy(1 + j, me, (*chip, c), src=x_ref) for j, chip in enumerate(chips)]
        for cp in first:
            cp.start()
        passed = [copy(4 + j, (*chip, c), sibling) for j, chip in enumerate(chips)]
        for j, chip in enumerate(chips):
            copy(1 + j, (*chip, c), me).wait_recv()
            passed[j].start()
        copy(0, sibling, me).wait_recv()
        for j, chip in enumerate(chips):
            copy(4 + j, (*chip, 1 - c), me).wait_recv()
        for cp in first + passed:
            cp.wait_send()
        mine.wait()
        acc = all_ref[0:m_per, :]
        for d in range(1, N_DEV):
            acc = acc + all_ref[d * m_per:(d + 1) * m_per, :]
        sum_ref[...] = acc

    vm = pl.BlockSpec(memory_space=pltpu.VMEM)
    return pl.pallas_call(
        body, name=name, in_specs=[vm], out_specs=vm, out_shape=jax.ShapeDtypeStruct((m_per, n), F32),
        scratch_shapes=[pltpu.VMEM((N_DEV * m_per, n), F32), pltpu.SemaphoreType.DMA((7,)),
                        pltpu.SemaphoreType.DMA((7,)), pltpu.SemaphoreType.DMA],
    )(v)


def _cols_to_slots(w):
    r, c4 = w.shape
    return w.reshape(r, N_CHIPS, c4 // N_CHIPS).transpose(1, 0, 2)


def _slots_to_cols(w):
    n, r, c = w.shape
    return w.transpose(1, 0, 2).reshape(r, n * c)


def _pad_cols(a, width):
    return jnp.pad(a, ((0, 0), (0, width - a.shape[1])))


class InLayout:
    def __init__(self, q_rank, kv_rank):
        gk = GLA_HEADS * GLA_DK
        gv = GLA_HEADS * GLA_DV
        sizes = [q_rank, kv_rank, MLA_ROPE, gk, gk, gv, GLA_GATE_RANK, gv]
        names = ["zq", "zkv", "zkr", "gq", "gk", "gv", "zg", "zr"]
        starts = np.concatenate([[0], np.cumsum(sizes)[:-1]])
        self.ref = {n: (int(s), int(z)) for n, s, z in zip(names, starts, sizes)}
        self.ref_width = int(sum(sizes))
        self.order = ["gv", "zr", "zq", "gq", "gk", "zkv", "zkr", "zg"]
        self.off, self.size = {}, {}
        pos = 0
        for n in self.order:
            padded = -(-self.ref[n][1] // LANE) * LANE
            self.off[n], self.size[n] = pos, padded
            pos += padded
        self.width = pos
        self.shard = self.ref_width // N_CHIPS
        self.shard_pad = -(-self.shard // LANE) * LANE

    def _pieces(self, lo, hi):
        out = []
        while lo < hi:
            s = lo // self.shard
            end = min(hi, (s + 1) * self.shard)
            out.append((s * self.shard_pad + lo - s * self.shard, s * self.shard_pad + end - s * self.shard))
            lo = end
        return out

    def from_shards(self, zs):
        cols = []
        for n in self.order:
            start, size = self.ref[n]
            cols += [zs[:, a:b] for a, b in self._pieces(start, start + size)]
            if self.size[n] > size:
                cols.append(jnp.zeros((zs.shape[0], self.size[n] - size), zs.dtype))
        return jnp.concatenate(cols, axis=1)

    def to_shards(self, dz):
        names = sorted(self.ref, key=lambda n: self.ref[n][0])
        ref = jnp.concatenate([dz[:, self.off[n]:self.off[n] + self.ref[n][1]] for n in names], axis=1)
        ref = ref.reshape(dz.shape[0], N_CHIPS, self.shard)
        return jnp.pad(ref, ((0, 0), (0, 0), (0, self.shard_pad - self.shard))).reshape(dz.shape[0], -1)


def _pad_q_up(w):
    r = w.shape[0]
    w = w.reshape(r, MLA_HEADS, MLA_QK)
    w = jnp.pad(w, ((0, 0), (0, 0), (0, MLA_HEAD_PAD - MLA_QK)))
    return w.reshape(r, MLA_HEADS * MLA_HEAD_PAD)


def _unpad_q_up(g):
    r = g.shape[0]
    return g.reshape(r, MLA_HEADS, MLA_HEAD_PAD)[:, :, :MLA_QK].reshape(r, MLA_HEADS * MLA_QK)


def _rope_tables(positions):
    half = MLA_ROPE // 2
    inv_freq = ROPE_THETA ** (-jnp.arange(half, dtype=F32) / half)
    ang = positions.astype(F32).reshape(-1, 1) * inv_freq
    cos, sin = jnp.cos(ang), jnp.sin(ang)
    s = ang.shape[0]
    cosf = jnp.concatenate([cos, cos, jnp.ones((s, LANE - MLA_ROPE), F32)], axis=1)
    sinf = jnp.concatenate([sin, sin, jnp.zeros((s, LANE - MLA_ROPE), F32)], axis=1)
    rot = np.zeros((LANE, LANE), np.float32)
    for j in range(half):
        rot[j + half, j] = -1.0
        rot[j, j + half] = 1.0
    return cosf, sinf, jnp.asarray(rot)


SMALL = ["ffn1_norm", "mix_norm", "q_a_norm", "kv_a_norm", "mla_q_norm", "mla_k_norm", "gla_b_gate", "gla_out_norm",
         "mem_attn_norm", "mem_norm", "mem_q_norm", "mem_k_norm", "ffn2_norm"]
BIG = ["ffn1_w_gate", "ffn1_w_up", "ffn1_w_down", "w_in", "w_q_up", "w_kv_up", "w_out", "mem_w_q", "mem_w_k",
       "mem_w_v", "mem_w_o", "ffn2_w_gate", "ffn2_w_up", "ffn2_w_down"]
COL_SHARDED = {"ffn1_w_gate", "ffn1_w_up", "w_in", "w_q_up", "w_kv_up", "gla_w_gate2", "mem_w_o", "ffn2_w_gate", "ffn2_w_up"}
WEIGHTS = ["ffn1_norm", "ffn1_w_gate", "ffn1_w_up", "ffn1_w_down", "mix_norm", "w_in", "q_a_norm", "w_q_up", "kv_a_norm",
           "w_kv_up", "mla_q_norm", "mla_k_norm", "gla_w_gate2", "gla_b_gate", "gla_out_norm", "w_out", "mem_attn_norm",
           "mem_norm", "mem_w_q", "mem_w_k", "mem_w_v", "mem_w_o", "mem_q_norm", "mem_k_norm", "ffn2_norm", "ffn2_w_gate",
           "ffn2_w_up", "ffn2_w_down"]


def _pack_small(vals, rows=8):
    flat = jnp.concatenate([v.reshape(-1).astype(F32) for v in vals])
    n = flat.shape[0]
    per = -(-n // (rows * LANE)) * LANE
    return jnp.pad(flat, (0, rows * per - n)).reshape(rows, per)


def _unpack_small(packed, shapes):
    flat = packed.reshape(-1)
    out, pos = [], 0
    for s in shapes:
        n = int(np.prod(s))
        out.append(flat[pos:pos + n].reshape(s))
        pos += n
    return out


FFN1 = ["ffn1_w_gate", "ffn1_w_up", "ffn1_w_down"]
FFN2 = ["ffn2_w_gate", "ffn2_w_up", "ffn2_w_down"]
SLOT_WEIGHTS = {"ffn1_w_gate", "ffn1_w_up", "ffn2_w_gate", "ffn2_w_up", "w_in"}
MID_A = ["w_in", "w_q_up", "w_kv_up", "gla_w_gate2"]
MID_B = ["w_out", "mem_w_q", "mem_w_k", "mem_w_v", "mem_w_o"]


def _with(res, comm):
    return res if comm is not None else (res, None)


def kernel(x, mem, positions, ffn1_norm, ffn1_w_gate, ffn1_w_up, ffn1_w_down, mix_norm, w_in, q_a_norm, w_q_up, kv_a_norm, w_kv_up, mla_q_norm, mla_k_norm, gla_w_gate2, gla_b_gate, gla_out_norm, w_out, mem_attn_norm, mem_norm, mem_w_q, mem_w_k, mem_w_v, mem_w_o, mem_q_norm, mem_k_norm, ffn2_norm, ffn2_w_gate, ffn2_w_up, ffn2_w_down, loss_target, m_ffn1_norm, m_ffn1_w_gate, m_ffn1_w_up, m_ffn1_w_down, m_mix_norm, m_w_in, m_q_a_norm, m_w_q_up, m_kv_a_norm, m_w_kv_up, m_mla_q_norm, m_mla_k_norm, m_gla_w_gate2, m_gla_b_gate, m_gla_out_norm, m_w_out, m_mem_attn_norm, m_mem_norm, m_mem_w_q, m_mem_w_k, m_mem_w_v, m_mem_w_o, m_mem_q_norm, m_mem_k_norm, m_ffn2_norm, m_ffn2_w_gate, m_ffn2_w_up, m_ffn2_w_down, v_ffn1_norm, v_ffn1_w_gate, v_ffn1_w_up, v_ffn1_w_down, v_mix_norm, v_w_in, v_q_a_norm, v_w_q_up, v_kv_a_norm, v_w_kv_up, v_mla_q_norm, v_mla_k_norm, v_gla_w_gate2, v_gla_b_gate, v_gla_out_norm, v_w_out, v_mem_attn_norm, v_mem_norm, v_mem_w_q, v_mem_w_k, v_mem_w_v, v_mem_w_o, v_mem_q_norm, v_mem_k_norm, v_ffn2_norm, v_ffn2_w_gate, v_ffn2_w_up, v_ffn2_w_down):
    args = dict(locals())
    two_d = lambda a: a[0] if a.ndim == 3 else a
    W = {n: two_d(args[n]) for n in WEIGHTS}
    M1 = {n: two_d(args["m_" + n]) for n in WEIGHTS}
    V2 = {n: two_d(args["v_" + n]) for n in WEIGHTS}
    xs, mems, tgt = x[0], mem[0], loss_target[0]
    S, D = xs.shape
    chip = 2 * lax.axis_index("x") + lax.axis_index("y")

    q_rank, kv_rank = W["w_q_up"].shape[0], W["w_kv_up"].shape[0]
    lay = InLayout(q_rank, kv_rank)
    off = lay.off
    up_names, down_names = FFN1[:2], FFN1[2:]
    shard16 = {n: W[n].astype(BF16) for n in up_names}
    full = {}

    def stage1(names):
        return gather_stage1([shard16[n] for n in names], [n != "gla_w_gate2" for n in names])

    def stage2(names, slots):
        return gather_stage2(slots, [shard16[n] for n in names], [n != "gla_w_gate2" for n in names])

    def finish(names, slots):
        for n, s in zip(names, slots):
            s = s.reshape((N_CHIPS,) + shard16[n].shape)
            if n in SLOT_WEIGHTS:
                full[n] = s
            else:
                full[n] = _slots_to_cols(s) if n in COL_SHARDED else s.reshape(-1, s.shape[2])

    token = {"last": None}

    def begin(comm, name, after=None):
        started = start_comm(comm, name=name, after=token["last"] if after is None else after)
        token["last"] = started[-1]
        return comm, started

    first = [begin(stage1([n]), f"gather_start_{n}") for n in up_names]
    zero = token["last"][0, 0]
    later = [n for n in BIG + ["gla_w_gate2"] if n not in shard16]
    for n in later:
        shard16[n] = (W[n] + zero).astype(BF16)
    shard16["w_in"] = _pad_cols(shard16["w_in"], lay.shard_pad)
    n1 = row_fwd(rms_fn, [V(xs)], [W["ffn1_norm"]], [(D, BF16)], [(0, 0, 0)], name="ffn1_norm")[0]
    gate_s1 = wait_comm(*first[0], [n1] + [shard16[n] for n in later], name=f"gather_wait_{up_names[0]}")
    finish(up_names[:1], run_comm(stage2(up_names[:1], gate_s1), name="pass_ffn1_gate"))
    up_s1 = wait_comm(*first[1], full[up_names[0]], name=f"gather_wait_{up_names[1]}")
    down1 = begin(stage1(down_names), "gather_start_ffn1_down", after=up_s1[0])
    mid_a_s1 = begin(stage1(MID_A), "gather_start_mid_a")
    finish(up_names[1:], run_comm(stage2(up_names[1:], up_s1), name="pass_ffn1_up"))
    cosf, sinf, rot = _rope_tables(positions[0])
    tri = jnp.asarray(np.tril(np.ones((CHUNK, CHUNK), np.float32)))
    gqn = W["mla_q_norm"][:, :MLA_NOPE]
    gqr = _pad_cols(W["mla_q_norm"][:, MLA_NOPE:], LANE)
    gkn = W["mla_k_norm"][:, :MLA_NOPE]
    gkr = _pad_cols(W["mla_k_norm"][:, MLA_NOPE:], LANE)
    HP = MLA_HEAD_PAD
    mla_scale = MLA_QK ** -0.5
    mem_scale = MEM_HEAD_DIM ** -0.5
    mla_w = MLA_HEADS * MLA_V
    gla_w = GLA_HEADS * GLA_DV
    mem_w = MEM_HEADS * MEM_HEAD_DIM

    gate1, up1, act1 = ffn_up(n1, full["ffn1_w_gate"], full["ffn1_w_up"], name="ffn1_up", behind=token["last"])
    finish(down_names, run_comm(stage2(down_names, wait_comm(*down1, act1, name="gather_wait_ffn1_down")),
                                name="pass_ffn1_down"))
    mid_a1 = wait_comm(*mid_a_s1, act1, name="gather_wait_mid_a")
    mid_b = begin(stage1(MID_B), "gather_start_mid_b", after=mid_a1[0])
    x1, got = mm([(act1, full["ffn1_w_down"])], "nn", F32, alpha=0.5, res=xs, name="ffn1_down",
                 comm=stage2(MID_A, mid_a1), behind=token["last"])
    ffn1_saved = (n1, gate1, up1, act1)
    finish(MID_A, got)
    ffn2_s1 = [begin(stage1([n]), f"gather_start_{n}", after=x1 if n == FFN2[0] else None) for n in FFN2]
    w_q_up_p = _pad_q_up(full["w_q_up"])
    w_gate2_p = jnp.pad(full["gla_w_gate2"], ((0, LANE - GLA_GATE_RANK), (0, 0)))
    h = row_fwd(rms_fn, [V(x1)], [W["mix_norm"]], [(D, BF16)], [(0, 0, 0)], name="mix_norm")[0]
    mid_b1 = wait_comm(*mid_b, h, name="gather_wait_mid_b")
    z_shards, got = mm([(h, full["w_in"])], "nn", F32, name="in_proj", b_slots=True, comm=stage2(MID_B, mid_b1),
                       behind=token["last"])
    z = lay.from_shards(z_shards)
    finish(MID_B, got)
    qa = row_fwd(rms_fn, [V(z, off["zq"], q_rank)], [W["q_a_norm"]], [(q_rank, BF16)], [(0, 0, 0)], name="q_a_norm")[0]
    kva = row_fwd(rms_fn, [V(z, off["zkv"], kv_rank)], [W["kv_a_norm"]], [(kv_rank, BF16)], [(0, 0, 0)], name="kv_a_norm")[0]
    qraw = mm([(qa, w_q_up_p)], "nn", F32, name="q_up")
    kvraw = mm([(kva, full["w_kv_up"])], "nn", F32, name="kv_up")
    tabs = [V(cosf, diff=False), V(sinf, diff=False)]
    q_rows = [V(qraw, 0, LANE, HP), V(qraw, LANE, LANE, HP)] + tabs
    k_rows = [V(kvraw, 0, LANE, HP), V(z, off["zkr"], LANE, 0)] + tabs
    qh = row_fwd(qk_prep_fn, q_rows, [gqn, gqr, rot], [(MLA_HEADS * HP, BF16)], [(0, 0, HP), (0, LANE, HP)],
                 heads=MLA_HEADS, name="q_prep")[0]
    kh = row_fwd(qk_prep_fn, k_rows, [gkn, gkr, rot], [(MLA_HEADS * HP, BF16)], [(0, 0, HP), (0, LANE, HP)],
                 heads=MLA_HEADS, name="k_prep")[0]
    mla_kw = dict(heads=MLA_HEADS, dk=HP, dv=MLA_V, v_off=1, v_hs=2, scale=mla_scale, causal=True, tq=512)
    o_mla = attn_fwd(qh, kh, kvraw, name="mla_attn", **mla_kw)

    zg = z[:, off["zg"]:off["zg"] + LANE]
    pre = mm([(zg, w_gate2_p)], "nn", F32, name="gla_gate")
    la = row_fwd(gate_fn, [V(pre)], [W["gla_b_gate"]], [(pre.shape[1], F32)], [(0, 0, 0)], name="gla_log_decay")[0]
    gla_kw = dict(q_off=off["gq"], k_off=off["gk"], v_off=off["gv"])
    o_raw, states = gla_fwd(z, la, tri, name="gla_scan", **gla_kw)
    gla_rows = [V(o_raw, 0, GLA_DV, GLA_DV), V(z, off["zr"], GLA_DV, GLA_DV)]
    o_gla = row_fwd(gla_out_fn, gla_rows, [W["gla_out_norm"]], [(gla_w, BF16)], [(0, 0, GLA_DV)], heads=GLA_HEADS,
                    name="gla_out")[0]
    o_cat = jnp.concatenate([o_mla, o_gla], axis=1)
    f2 = [wait_comm(*ffn2_s1[k], o_cat, name=f"gather_wait_{FFN2[k]}")[0] for k in range(2)]
    x2, got = mm([(o_cat, full["w_out"])], "nn", F32, res=x1, name="out_proj", comm=stage2(FFN2[:2], f2))
    finish(FFN2[:2], got)

    hm = row_fwd(rms_fn, [V(x2)], [W["mem_attn_norm"]], [(D, BF16)], [(0, 0, 0)], name="mem_attn_norm")[0]
    mn = row_fwd(rms_fn, [V(mems)], [W["mem_norm"]], [(D, BF16)], [(0, 0, 0)], name="mem_norm")[0]
    qm_raw = mm([(hm, full["mem_w_q"])], "nn", F32, name="mem_q")
    km_raw = mm([(mn, full["mem_w_k"])], "nn", F32, name="mem_k")
    vm = mm([(mn, full["mem_w_v"])], "nn", F32, name="mem_v")
    hd = MEM_HEAD_DIM
    qm = row_fwd(rms_fn, [V(qm_raw, 0, hd, hd)], [W["mem_q_norm"]], [(mem_w, BF16)], [(0, 0, hd)], heads=MEM_HEADS,
                 name="mem_q_norm")[0]
    km = row_fwd(rms_fn, [V(km_raw, 0, hd, hd)], [W["mem_k_norm"]], [(mem_w, BF16)], [(0, 0, hd)], heads=MEM_HEADS,
                 name="mem_k_norm")[0]
    mem_kw = dict(heads=MEM_HEADS, dk=hd, dv=hd, v_off=0, v_hs=1, scale=mem_scale, causal=False)
    om = attn_fwd(qm, km, vm, name="mem_attn", **mem_kw)
    x3 = mm([(om, full["mem_w_o"])], "nn", F32, res=x2, name="mem_o")

    n2 = row_fwd(rms_fn, [V(x3)], [W["ffn2_norm"]], [(D, BF16)], [(0, 0, 0)], name="ffn2_norm")[0]
    f2_down = wait_comm(*ffn2_s1[2], n2, name=f"gather_wait_{FFN2[2]}")
    (gate2, up2, act2), got = ffn_up(n2, full["ffn2_w_gate"], full["ffn2_w_up"], name="ffn2_up",
                                     comm=stage2(FFN2[2:], f2_down))
    finish(FFN2[2:], got)
    y = mm([(act2, full["ffn2_w_down"])], "nn", F32, alpha=0.5, res=x3, name="ffn2_down")
    dy, loss_part = loss_head(y, tgt, name="loss_head")
    G = {"loss": loss_part[:, :1]}

    chip_sum, reduced = {}, {}

    def to_halves(n):
        g = G[n]
        if n in SLOT_WEIGHTS:
            s = g
        else:
            s = _cols_to_slots(g) if n in COL_SHARDED else g.reshape(N_CHIPS, g.shape[0] // N_CHIPS, g.shape[1])
        return s.reshape(N_CHIPS, 2, s.shape[1] // 2, s.shape[2])

    def add2(names, halves, got):
        for n, a, b in zip(names, halves, got):
            chip_sum[n] = add_own_half(a, b, BF16, name=f"rs_add2_{n}")

    to_join = []

    def add4_join(names, parts):
        for n, p in zip(names, parts):
            to_join.append((n, sum_chip_parts(chip_sum[n], p, name=f"rs_add4_{n}")))

    def with_joins(comm):
        names, totals = [n for n, _ in to_join], [t for _, t in to_join]
        to_join.clear()
        if not names:
            return comm, lambda got: got
        own = 0 if comm is None else len(comm.out_shapes)
        joined = join_halves(totals)

        def split(got):
            for n, b in zip(names, got[own:]):
                reduced[n] = b.reshape(-1, b.shape[2])[:, :W[n].shape[1]]
            return got[:own]

        return (joined if comm is None else merge_comms(comm, joined)), split

    def flush_joins():
        comm, split = with_joins(None)
        if comm is not None:
            split(run_comm(comm, name=f"rs_join_{len(reduced)}"))

    in_flight = []

    def xchg_start(names):
        in_flight.append((names,) + begin(exchange_chips([chip_sum[n] for n in names]), f"xchg_start_{names[0]}"))

    def xchg_wait(after, count=1):
        for _ in range(count):
            names, comm, started = in_flight.pop(0)
            add4_join(names, wait_comm(comm, started, after, name=f"xchg_wait_{names[0]}"))

    def ffn_backward(dout, xin, tag, saved, dact_comm=None, after_dact=None):
        n_, gate, up, act = saved
        nd, ng, nu = f"{tag}_w_down", f"{tag}_w_gate", f"{tag}_w_up"
        (dgate, dup), got0 = _with(ffn_dact(dout, full[nd], gate, up, 0.5, name=f"{tag}_dact", comm=dact_comm,
                                            behind=token["last"]), dact_comm)
        if after_dact:
            after_dact(got0)
        G[nd] = mm([(act, dout)], "tn", F32, alpha=0.5, name=f"{tag}_dwd", tm=1408, tn=1024, behind=token["last"])
        hd_ = to_halves(nd)
        comm, split = with_joins(swap_halves([hd_]))
        G[ng], got = mm([(n_, dgate)], "tn", F32, name=f"{tag}_dwg", out_slots=True, tm=1024, tn=1408, rows_inner=True,
                        comm=comm)
        add2([nd], [hd_], split(got))
        xchg_start([nd])
        hg = to_halves(ng)
        G[nu], got_g = mm([(n_, dup)], "tn", F32, name=f"{tag}_dwu", out_slots=True, tm=1024, tn=1408, rows_inner=True,
                          comm=swap_halves([hg]), behind=token["last"])
        add2([ng], [hg], got_g)
        xchg_start([ng])
        hu = to_halves(nu)
        dn, got_u = mm([(dgate, full[ng]), (dup, full[nu])], "nt", F32, name=f"{tag}_dn", b_slots=True, tn=1024, tk=1408,
                       comm=swap_halves([hu]), behind=token["last"])
        add2([nu], [hu], got_u)
        xchg_start([nu])
        dx, G[f"{tag}_norm"] = row_bwd(rms_fn, [V(xin)], [W[f"{tag}_norm"]], [V(dn)], const_diff=[True], res=dout,
                                       name=f"{tag}_dnorm")
        return dx

    g3 = ffn_backward(dy, x3, "ffn2", (n2, gate2, up2, act2))
    xchg_wait(g3)

    d_om = mm([(g3, full["mem_w_o"])], "nt", F32, name="mem_o_dx", behind=token["last"])
    G["mem_w_o"] = mm([(om, g3)], "tn", F32, name="mem_o_dw")
    dqm, dkm, dvm = attn_bwd(qm, km, vm, d_om, name="mem_attn_bwd", **mem_kw)
    dqm_raw, G["mem_q_norm"] = row_bwd(rms_fn, [V(qm_raw, 0, hd, hd)], [W["mem_q_norm"]], [V(dqm, 0, hd, hd)],
                                       const_diff=[True], heads=MEM_HEADS, row_dtype=BF16, name="mem_q_norm_bwd")
    dkm_raw, G["mem_k_norm"] = row_bwd(rms_fn, [V(km_raw, 0, hd, hd)], [W["mem_k_norm"]], [V(dkm, 0, hd, hd)],
                                       const_diff=[True], heads=MEM_HEADS, row_dtype=BF16, name="mem_k_norm_bwd")
    dhm = mm([(dqm_raw, full["mem_w_q"])], "nt", F32, name="mem_q_dx")
    G["mem_w_q"] = mm([(hm, dqm_raw)], "tn", F32, name="mem_q_dw")
    dmn = mm([(dkm_raw, full["mem_w_k"]), (dvm, full["mem_w_v"])], "nt", F32, name="mem_kv_dx")
    G["mem_w_k"] = mm([(mn, dkm_raw)], "tn", F32, name="mem_k_dw")
    G["mem_w_v"] = mm([(mn, dvm)], "tn", F32, name="mem_v_dw")
    _, G["mem_norm"] = row_bwd(rms_fn, [V(mems)], [W["mem_norm"]], [V(dmn)], const_diff=[True], row_dtype=BF16,
                               name="mem_norm_bwd")
    g2, G["mem_attn_norm"] = row_bwd(rms_fn, [V(x2)], [W["mem_attn_norm"]], [V(dhm)], const_diff=[True], res=g3,
                                     name="mem_attn_norm_bwd")

    xchg_wait(g2, 2)

    d_ocat = mm([(g2, full["w_out"])], "nt", F32, name="out_proj_dx")
    G["w_out"] = mm([(o_cat, g2)], "tn", F32, name="out_proj_dw")

    d_oraw, d_zr, G["gla_out_norm"] = row_bwd(gla_out_fn, gla_rows, [W["gla_out_norm"]],
                                              [V(d_ocat, mla_w, GLA_DV, GLA_DV)], const_diff=[True], heads=GLA_HEADS,
                                              name="gla_out_bwd")
    mid_b_halves = [to_halves(n) for n in MID_B]
    comm, split = with_joins(swap_halves(mid_b_halves))
    (d_gq, d_gk, d_gv, d_la), got = gla_bwd(z, la, tri, tri.T, states, d_oraw, name="gla_scan_bwd", comm=comm, **gla_kw)
    add2(MID_B, mid_b_halves, split(got))
    xchg_start(MID_B)
    d_pre, G["gla_b_gate"] = row_bwd(gate_fn, [V(pre)], [W["gla_b_gate"]], [V(d_la)], const_diff=[True], row_dtype=BF16,
                                     name="gla_log_decay_bwd")
    d_zg = mm([(d_pre, w_gate2_p)], "nt", BF16, name="gla_gate_dx", behind=token["last"])
    G["gla_w_gate2"] = mm([(zg, d_pre)], "tn", F32, name="gla_gate_dw")[:GLA_GATE_RANK]

    comm, split = with_joins(None)
    (d_qh, d_kh, d_v), got = _with(attn_bwd(qh, kh, kvraw, d_ocat, name="mla_attn_bwd", comm=comm, **mla_kw), comm)
    split(got)
    cq = [V(d_qh, 0, LANE, HP), V(d_qh, LANE, LANE, HP)]
    ck = [V(d_kh, 0, LANE, HP), V(d_kh, LANE, LANE, HP)]
    d_qraw, d_gqn, d_gqr = row_bwd(qk_prep_fn, q_rows, [gqn, gqr, rot], cq, const_diff=[True, True, False],
                                   heads=MLA_HEADS, row_dtype=BF16, pack={0: (0, HP), 1: (LANE, HP)},
                                   pack_width=MLA_HEADS * HP, name="q_prep_bwd")
    d_kvraw, d_zkr, d_gkn, d_gkr = row_bwd(qk_prep_fn, k_rows, [gkn, gkr, rot], ck, const_diff=[True, True, False],
                                           heads=MLA_HEADS, row_dtype=BF16, pack={0: (0, HP)}, pack_width=MLA_HEADS * HP,
                                           fills=[(V(d_v, 0, MLA_V, MLA_V), LANE, HP)], name="k_prep_bwd")
    G["mla_q_norm"] = jnp.concatenate([d_gqn, d_gqr[:, :MLA_ROPE]], axis=1)
    G["mla_k_norm"] = jnp.concatenate([d_gkn, d_gkr[:, :MLA_ROPE]], axis=1)
    d_qa = mm([(d_qraw, w_q_up_p)], "nt", F32, name="q_up_dx")
    G["w_q_up"] = _unpad_q_up(mm([(qa, d_qraw)], "tn", F32, name="q_up_dw"))
    d_kva = mm([(d_kvraw, full["w_kv_up"])], "nt", F32, name="kv_up_dx")
    G["w_kv_up"] = mm([(kva, d_kvraw)], "tn", F32, name="kv_up_dw")
    d_zq, G["q_a_norm"] = row_bwd(rms_fn, [V(z, off["zq"], q_rank)], [W["q_a_norm"]], [V(d_qa)], const_diff=[True],
                                  row_dtype=BF16, name="q_a_norm_bwd")
    d_zkv, G["kv_a_norm"] = row_bwd(rms_fn, [V(z, off["zkv"], kv_rank)], [W["kv_a_norm"]], [V(d_kva)], const_diff=[True],
                                    row_dtype=BF16, name="kv_a_norm_bwd")

    seg = {"gv": d_gv, "zr": d_zr, "zq": d_zq, "gq": d_gq, "gk": d_gk, "zkv": d_zkv, "zkr": d_zkr, "zg": d_zg}
    dz = jnp.concatenate([_pad_cols(seg[n].astype(BF16), lay.size[n]) for n in lay.order], axis=1)
    xchg_wait(dz)
    comm, split = with_joins(None)
    dz_shards = lay.to_shards(dz)
    dh, got = _with(mm([(dz_shards, full["w_in"])], "nt", F32, name="in_proj_dx", b_slots=True, comm=comm), comm)
    split(got)
    G["w_in"] = mm([(h, dz_shards)], "tn", F32, name="in_proj_dw", out_slots=True)
    g1, G["mix_norm"] = row_bwd(rms_fn, [V(x1)], [W["mix_norm"]], [V(dh)], const_diff=[True], res=g2,
                                name="mix_norm_bwd")

    mid_a = [n for n in MID_A if n != "gla_w_gate2"]
    mid_a_halves = [to_halves(n) for n in mid_a]

    def mid_a_sums(got):
        add2(mid_a, mid_a_halves, got)
        xchg_start(mid_a)

    gx = ffn_backward(g1, xs, "ffn1", ffn1_saved, dact_comm=swap_halves(mid_a_halves), after_dact=mid_a_sums)
    xchg_wait(gx, 2)

    grad, delta, new_m, new_v = {}, {}, {}, {}

    def adam_group(names, tag, behind=None):
        if any(n not in reduced for n in names):
            flush_joins()
        res = adamw([(W[n], reduced[n], M1[n], V2[n]) for n in names], name=f"adamw_{tag}", behind=behind)
        for n, (g_, d_, m_, v_) in zip(names, res):
            grad[n], delta[n], new_m[n], new_v[n] = g_, d_, m_, v_

    adam_group(FFN2, "ffn2", behind=token["last"])
    adam_group(mid_a + MID_B, "mid", behind=token["last"])
    xchg_wait(delta[MID_B[-1]], 2)
    adam_group(FFN1, "ffn1")

    small_names = SMALL + ["gla_w_gate2"]
    packed = small_names + ["loss"]
    small_sum = allreduce_small(_pack_small([G[n] for n in packed]), name="allreduce_small")
    small_g = dict(zip(packed, _unpack_small(small_sum, [G[n].shape for n in packed])))
    loss = small_g["loss"][0, 0]
    shard_c = W["gla_w_gate2"].shape[1]
    grad["gla_w_gate2"] = lax.dynamic_slice_in_dim(small_g["gla_w_gate2"], chip * shard_c, shard_c, axis=1)
    pw = _pack_small([W[n] for n in SMALL] + [W["gla_w_gate2"]])
    pg = _pack_small([small_g[n] for n in SMALL] + [grad["gla_w_gate2"]])
    pm = _pack_small([M1[n] for n in SMALL] + [M1["gla_w_gate2"]])
    pv = _pack_small([V2[n] for n in SMALL] + [V2["gla_w_gate2"]])
    (_, pd, pnm, pnv), = adamw([(pw, pg, pm, pv)], name="adamw_small")
    shapes = [W[n].shape for n in small_names]
    for n, d_, m_, v_ in zip(small_names, _unpack_small(pd, shapes), _unpack_small(pnm, shapes), _unpack_small(pnv, shapes)):
        delta[n], new_m[n], new_v[n] = d_, m_, v_
        if n != "gla_w_gate2":
            grad[n] = small_g[n]

    lead = lambda d: [d[n].reshape(args[n].shape) for n in WEIGHTS]
    return (loss, gx[None], *lead(grad), *lead(delta), *lead(new_m), *lead(new_v))
```

```python
import functools

import numpy as np
import jax
import jax.numpy as jnp
from jax import lax
from jax.experimental import pallas as pl
from jax.experimental.pallas import tpu as pltpu

F32 = jnp.float32
BF16 = jnp.bfloat16
MXU_DTYPE = jnp.bfloat16
MESH = pl.DeviceIdType.MESH
ANY = pl.BlockSpec(memory_space=pl.ANY)

LANE = 128
EPS = 1e-6
CHUNK = 64
MLA_HEADS = 8
MLA_NOPE = 128
MLA_ROPE = 64
MLA_QK = MLA_NOPE + MLA_ROPE
MLA_V = 128
MLA_HEAD_PAD = 2 * LANE
ROPE_THETA = 10000.0
GLA_HEADS = 4
GLA_DK = 128
GLA_DV = 256
GLA_GATE_RANK = 16
GLA_TAU = 16.0
MEM_HEADS = 4
MEM_HEAD_DIM = 128
N_CHIPS = 4
N_DEV = 8

ADAM_LR = 0.001
ADAM_B1 = 0.9
ADAM_B2 = 0.999
ADAM_EPS = 1e-08
ADAM_WD = 0.01
ADAM_STEP = 10

VMEM_LIMIT = 56 * 1024 * 1024


def _cparams(sem=None):
    if sem is None:
        return pltpu.CompilerParams(vmem_limit_bytes=VMEM_LIMIT)
    return pltpu.CompilerParams(dimension_semantics=sem, vmem_limit_bytes=VMEM_LIMIT)


def _tile(dim, pref, unit=LANE):
    if dim <= pref:
        return dim
    t = (pref // unit) * unit
    while t > unit and dim % t:
        t -= unit
    assert dim % t == 0, (dim, pref, unit)
    return t


class Comm:
    def __init__(self, ins, out_shapes, nsem, start, wait, aliases=None):
        self.ins, self.out_shapes, self.nsem = list(ins), list(out_shapes), nsem
        self.start, self.wait, self.aliases = start, wait, dict(aliases or {})


def merge_comms(a, b):
    ai, ao = len(a.ins), len(a.out_shapes)

    def start(ins, outs, send, recv, base):
        a.start(ins[:ai], outs[:ao], send, recv, base)
        b.start(ins[ai:], outs[ao:], send, recv, base + a.nsem)

    def wait(ins, outs, send, recv, base):
        a.wait(ins[:ai], outs[:ao], send, recv, base)
        b.wait(ins[ai:], outs[ao:], send, recv, base + a.nsem)

    aliases = dict(a.aliases)
    aliases.update({ai + i: ao + o for i, o in b.aliases.items()})
    return Comm(a.ins + b.ins, a.out_shapes + b.out_shapes, a.nsem + b.nsem, start, wait, aliases)


def run_comm(comm, *, name):
    ni, no = len(comm.ins), len(comm.out_shapes)

    def body(*refs):
        ins, outs = refs[:ni], refs[ni:ni + no]
        send, recv = refs[ni + no:]
        comm.start(ins, outs, send, recv, 0)
        comm.wait(ins, outs, send, recv, 0)

    return pl.pallas_call(
        body, name=name, in_specs=[ANY] * ni, out_specs=[ANY] * no, out_shape=comm.out_shapes,
        input_output_aliases=comm.aliases,
        scratch_shapes=[pltpu.SemaphoreType.DMA((comm.nsem,)), pltpu.SemaphoreType.DMA((comm.nsem,))])(*comm.ins)


HBM = pl.BlockSpec(memory_space=pltpu.HBM)
SEM = pl.BlockSpec(memory_space=pltpu.SEMAPHORE)


def start_comm(comm, *, name, after=None):
    assert not comm.aliases
    ni, no = len(comm.ins), len(comm.out_shapes)
    tail = [] if after is None else [after]

    def body(*refs):
        srcs, lands = refs[:ni], refs[ni:ni + no]
        send, recv = refs[ni + no + len(tail)], refs[ni + no + len(tail) + 1]
        token = refs[-1]
        comm.start(srcs, lands, send, recv, 0)
        token[...] = jnp.zeros_like(token)

    through = [pltpu.HBM(a.shape, a.dtype) for a in comm.ins] + [pltpu.HBM(s.shape, s.dtype) for s in comm.out_shapes]
    ops = [pltpu.with_memory_space_constraint(a, pltpu.HBM) for a in comm.ins]
    ops += [pltpu.with_memory_space_constraint(lax.empty(s.shape, s.dtype), pltpu.HBM) for s in comm.out_shapes]
    ops += tail
    res = pl.pallas_call(
        body, name=name, in_specs=[HBM] * (ni + no) + [ANY] * len(tail),
        out_shape=[pltpu.SemaphoreType.DMA((comm.nsem,)), pltpu.SemaphoreType.DMA((comm.nsem,))] + through
        + [jax.ShapeDtypeStruct((8, LANE), F32)],
        out_specs=[SEM, SEM] + [HBM] * (ni + no) + [pl.BlockSpec(memory_space=pltpu.VMEM)],
        input_output_aliases={i: 2 + i for i in range(ni + no)},
        compiler_params=pltpu.CompilerParams(has_side_effects=pltpu.SideEffectType.DATAFLOW_SIDE_EFFECTING))(*ops)
    return res[0], res[1], list(res[2:2 + ni]), list(res[2 + ni:2 + ni + no]), res[-1]


def wait_comm(comm, started, after, *, name):
    send, recv, srcs, lands, _ = started
    ni, no = len(srcs), len(lands)
    after = list(after) if isinstance(after, (list, tuple)) else [after]

    def body(*refs):
        comm.wait(refs[:ni], refs[ni:ni + no], refs[ni + no], refs[ni + no + 1], 0)

    res = pl.pallas_call(
        body, name=name, in_specs=[HBM] * (ni + no) + [SEM, SEM] + [ANY] * len(after),
        out_shape=[pltpu.HBM(a.shape, a.dtype) for a in srcs + lands], out_specs=[HBM] * (ni + no),
        input_output_aliases={i: i for i in range(ni + no)},
        compiler_params=pltpu.CompilerParams(has_side_effects=pltpu.SideEffectType.DATAFLOW_SIDE_EFFECTING),
    )(*srcs, *lands, send, recv, *after)
    return list(res[ni:])


def _pcall(body, ops, *, name, grid, in_specs, out_specs, out_shape, sem, scratch_shapes=(), comm=None, behind=None):
    if behind is not None:
        n_real, inner = len(ops), body
        ops, in_specs = list(ops) + [behind], list(in_specs) + [ANY]

        def body(*refs):
            inner(*refs[:n_real], *refs[n_real + 1:])

    if comm is None:
        return pl.pallas_call(body, name=name, grid=grid, in_specs=in_specs, out_specs=out_specs, out_shape=out_shape,
                              scratch_shapes=list(scratch_shapes), compiler_params=_cparams(sem))(*ops)
    multi = isinstance(out_shape, (list, tuple))
    k_out_shape = list(out_shape) if multi else [out_shape]
    k_out_specs = list(out_specs) if multi else [out_specs]
    nki, nko, nks = len(ops), len(k_out_shape), len(scratch_shapes)
    nci, nco = len(comm.ins), len(comm.out_shapes)

    def wrapped(*refs):
        p = 0
        k_in = refs[p:p + nki]; p += nki
        c_in = refs[p:p + nci]; p += nci
        k_out = refs[p:p + nko]; p += nko
        c_out = refs[p:p + nco]; p += nco
        k_scr = refs[p:p + nks]; p += nks
        send, recv = refs[p:]
        first = pl.program_id(0) == 0
        last = pl.program_id(0) == grid[0] - 1
        for a in range(1, len(grid)):
            first = jnp.logical_and(first, pl.program_id(a) == 0)
            last = jnp.logical_and(last, pl.program_id(a) == grid[a] - 1)

        @pl.when(first)
        def _():
            comm.start(c_in, c_out, send, recv, 0)

        body(*k_in, *k_out, *k_scr)

        @pl.when(last)
        def _():
            comm.wait(c_in, c_out, send, recv, 0)

    res = pl.pallas_call(
        wrapped, name=name, grid=grid, in_specs=list(in_specs) + [ANY] * nci, out_specs=k_out_specs + [ANY] * nco,
        out_shape=k_out_shape + comm.out_shapes,
        input_output_aliases={nki + i: nko + o for i, o in comm.aliases.items()},
        scratch_shapes=list(scratch_shapes) + [pltpu.SemaphoreType.DMA((comm.nsem,)), pltpu.SemaphoreType.DMA((comm.nsem,))],
        compiler_params=_cparams(("arbitrary",) * len(grid)))(*ops, *comm.ins)
    k_res = list(res[:nko]) if multi else res[0]
    return k_res, list(res[nko:])


_DIMS = {"nn": (((1,), (0,)), ((), ())), "nt": (((1,), (1,)), ((), ())), "tn": (((0,), (0,)), ((), ()))}


def _blockspec(shape, index, rows_inner):
    return pl.BlockSpec(shape, (lambda j, i, k: index(i, j, k)) if rows_inner else index)


def mm(pairs, mode, out_dtype, *, name, alpha=1.0, res=None, tm=1024, tn=1024, tk=4096, b_slots=False, out_slots=False,
       rows_inner=False, comm=None, behind=None):
    a0, b0 = pairs[0]
    if b_slots:
        b_rows, b_cols = b0.shape[1], N_CHIPS * b0.shape[2]
    else:
        b_rows, b_cols = b0.shape
    (M, K) = a0.shape[::-1] if mode == "tn" else a0.shape
    N = b_rows if mode == "nt" else b_cols
    shard = (b_cols if b_slots else N) // N_CHIPS
    tm = _tile(M, tm)
    tn = _tile(shard if (out_slots or (b_slots and mode != "nt")) else N, tn)
    tk = _tile(shard if (b_slots and mode == "nt") else K, tk)
    nk = K // tk
    npairs = len(pairs)
    dims = _DIMS[mode]
    spec = functools.partial(_blockspec, rows_inner=rows_inner)
    if mode == "tn":
        a_spec = spec((tk, tm), lambda i, j, k: (k, i))
    else:
        a_spec = spec((tm, tk), lambda i, j, k: (i, k))
    per = shard // (tk if mode == "nt" else tn)
    if mode == "nt":
        b_spec = (spec((None, tn, tk), lambda i, j, k: (k // per, j, k % per)) if b_slots else
                  spec((tn, tk), lambda i, j, k: (j, k)))
    else:
        b_spec = (spec((None, tk, tn), lambda i, j, k: (j // per, k, j % per)) if b_slots else
                  spec((tk, tn), lambda i, j, k: (k, j)))
    if out_slots:
        assert res is None and mode != "nt"
        o_spec = spec((None, tm, tn), lambda i, j, k: (j // per, i, j % per))
        out_sds = jax.ShapeDtypeStruct((N_CHIPS, M, shard), out_dtype)
    else:
        o_spec = spec((tm, tn), lambda i, j, k: (i, j))
        out_sds = jax.ShapeDtypeStruct((M, N), out_dtype)
    has_res = res is not None

    def body(*refs):
        ab = refs[:2 * npairs]
        res_ref = refs[2 * npairs] if has_res else None
        o_ref = refs[2 * npairs + int(has_res)]

        def products():
            r = None
            for p in range(npairs):
                d = lax.dot_general(ab[2 * p][...].astype(MXU_DTYPE), ab[2 * p + 1][...].astype(MXU_DTYPE), dims,
                                    preferred_element_type=F32)
                r = d if r is None else r + d
            return r

        def finish(r):
            if alpha != 1.0:
                r = r * alpha
            if has_res:
                r = res_ref[...].astype(F32) + r
            o_ref[...] = r.astype(out_dtype)

        if nk == 1:
            finish(products())
            return
        acc = refs[-1]
        k = pl.program_id(2)

        @pl.when(k == 0)
        def _():
            acc[...] = jnp.zeros_like(acc)

        acc[...] += products()

        @pl.when(k == nk - 1)
        def _():
            finish(acc[...])

    ops, specs = [], []
    for a, b in pairs:
        ops += [a, b]
        specs += [a_spec, b_spec]
    if has_res:
        ops.append(res)
        specs.append(o_spec)
    blocks = (N // tn, M // tm) if rows_inner else (M // tm, N // tn)
    return _pcall(body, ops, name=name, grid=blocks + (nk,), in_specs=specs, out_specs=o_spec, out_shape=out_sds,
                  scratch_shapes=[pltpu.VMEM((tm, tn), F32)] if nk > 1 else [],
                  sem=("parallel", "parallel", "arbitrary"), comm=comm, behind=behind)


def _sigmoid(x):
    return 1.0 / (1.0 + jnp.exp(-x))


def ffn_up(n, wg, wu, *, name, tm=512, tn=1408, comm=None, behind=None):
    M, K = n.shape
    shard = wg.shape[2]
    N = N_CHIPS * shard
    tm, tn = _tile(M, tm), _tile(shard, tn)
    per = shard // tn
    w_spec = pl.BlockSpec((None, K, tn), lambda j, i: (j // per, 0, j % per))

    def body(n_ref, wg_ref, wu_ref, g_ref, u_ref, a_ref):
        nv = n_ref[...].astype(MXU_DTYPE)
        g = jnp.dot(nv, wg_ref[...].astype(MXU_DTYPE), preferred_element_type=F32)
        u = jnp.dot(nv, wu_ref[...].astype(MXU_DTYPE), preferred_element_type=F32)
        g_ref[...] = g.astype(g_ref.dtype)
        u_ref[...] = u.astype(u_ref.dtype)
        a_ref[...] = (g * _sigmoid(g) * u).astype(a_ref.dtype)

    o_spec = pl.BlockSpec((tm, tn), lambda j, i: (i, j))
    sds = jax.ShapeDtypeStruct((M, N), BF16)
    return _pcall(
        body, [n, wg, wu], name=name, grid=(N // tn, M // tm),
        in_specs=[pl.BlockSpec((tm, K), lambda j, i: (i, 0)), w_spec, w_spec],
        out_specs=[o_spec, o_spec, o_spec], out_shape=[sds, sds, sds], sem=("parallel", "parallel"), comm=comm,
        behind=behind)


def ffn_dact(dy, wd, gate, up, alpha, *, name, tm=512, tn=1408, comm=None, behind=None):
    M, K = dy.shape
    N = wd.shape[0]
    tm, tn = _tile(M, tm), _tile(N, tn)

    def body(dy_ref, wd_ref, g_ref, u_ref, dg_ref, du_ref):
        da = lax.dot_general(dy_ref[...].astype(MXU_DTYPE), wd_ref[...].astype(MXU_DTYPE), _DIMS["nt"],
                             preferred_element_type=F32) * alpha
        g = g_ref[...].astype(F32)
        u = u_ref[...].astype(F32)
        s = _sigmoid(g)
        du_ref[...] = (da * (g * s)).astype(du_ref.dtype)
        dg_ref[...] = (da * u * (s * (1.0 + g * (1.0 - s)))).astype(dg_ref.dtype)

    o_spec = pl.BlockSpec((tm, tn), lambda j, i: (i, j))
    sds = jax.ShapeDtypeStruct((M, N), BF16)
    return _pcall(
        body, [dy, wd, gate, up], name=name, grid=(N // tn, M // tm),
        in_specs=[pl.BlockSpec((tm, K), lambda j, i: (i, 0)), pl.BlockSpec((tn, K), lambda j, i: (j, 0)), o_spec, o_spec],
        out_specs=[o_spec, o_spec], out_shape=[sds, sds], sem=("parallel", "parallel"), comm=comm, behind=behind)


def _window(width, off, ext):
    ww = LANE
    while ww < width:
        if ww >= ext and off // ww == (off + ext - 1) // ww and width % ww == 0:
            break
        ww *= 2
    else:
        ww = width
    return ww, off // ww, off - (off // ww) * ww


class V:
    def __init__(self, arr, off=0, w=None, hs=0, diff=True):
        self.arr, self.off, self.hs, self.diff = arr, off, hs, diff
        self.w = arr.shape[1] - off if w is None else w

    def window(self, heads, tr):
        ww, blk, inner = _window(self.arr.shape[1], self.off, (heads - 1) * self.hs + self.w)
        return pl.BlockSpec((tr, ww), lambda i, blk=blk: (i, blk)), inner


def _const_spec(c):
    return pl.BlockSpec(c.shape, lambda i: (0, 0))


def row_fwd(fn, rows, consts, outs, out_map, *, heads=1, tr=256, name):
    S = rows[0].arr.shape[0]
    tr = _tile(S, tr, 8)
    wins = [v.window(heads, tr) for v in rows]
    nr, nc = len(rows), len(consts)

    def body(*refs):
        row_refs, const_refs, out_refs = refs[:nr], refs[nr:nr + nc], refs[nr + nc:]
        cv = [c[...].astype(F32) for c in const_refs]
        for h in range(heads):
            rv = []
            for v, (_, io), r in zip(rows, wins, row_refs):
                lo = io + h * v.hs
                rv.append(r[:, lo:lo + v.w].astype(F32))
            res = fn(*rv, *cv)
            for (ai, off, hs), o in zip(out_map, res):
                lo = off + h * hs
                out_refs[ai][:, lo:lo + o.shape[1]] = o.astype(out_refs[ai].dtype)

    return pl.pallas_call(
        body, name=name, grid=(S // tr,),
        in_specs=[w[0] for w in wins] + [_const_spec(c) for c in consts],
        out_specs=[pl.BlockSpec((tr, w), lambda i: (i, 0)) for w, _ in outs],
        out_shape=[jax.ShapeDtypeStruct((S, w), d) for w, d in outs],
        compiler_params=_cparams(("parallel",)))(*[v.arr for v in rows], *consts)


def row_bwd(fn, rows, consts, cots, *, const_diff, heads=1, tr=256, res=None, row_dtype=F32, pack=None, pack_width=0,
            fills=(), name):
    S = rows[0].arr.shape[0]
    tr = _tile(S, tr, 8)
    pack = dict(pack or {})
    nr, nc, nct, nf = len(rows), len(consts), len(cots), len(fills)
    wins = [v.window(heads, tr) for v in rows]
    cwins = [v.window(heads, tr) for v in cots]
    fwins = [v.window(heads, tr) for v, _, _ in fills]
    drows = [k for k, v in enumerate(rows) if v.diff]
    dconsts = [k for k in range(nc) if const_diff[k]]
    has_res = res is not None
    assert not (has_res and 0 in pack)
    widths = [pack_width] if pack else []
    place = []
    for n, k in enumerate(drows):
        if n in pack:
            place.append((0,) + tuple(pack[n]))
        else:
            place.append((len(widths), 0, rows[k].w))
            widths.append(rows[k].w * (heads if rows[k].hs else 1))

    def body(*refs):
        row_refs = refs[:nr]
        const_refs = refs[nr:nr + nc]
        cot_refs = refs[nr + nc:nr + nc + nct]
        p = nr + nc + nct
        fill_refs = refs[p:p + nf]
        p += nf
        res_ref = refs[p] if has_res else None
        p += int(has_res)
        grow_refs = refs[p:p + len(widths)]
        gconst_refs = refs[p + len(widths):]
        i = pl.program_id(0)
        cv = [c[...].astype(F32) for c in const_refs]
        shared = [None] * len(drows)
        gc_sum = [None] * len(dconsts)
        for h in range(heads):
            rv = []
            for v, (_, io), r in zip(rows, wins, row_refs):
                lo = io + h * v.hs
                rv.append(r[:, lo:lo + v.w].astype(F32))
            ct = []
            for v, (_, io), r in zip(cots, cwins, cot_refs):
                lo = io + h * v.hs
                ct.append(r[:, lo:lo + v.w].astype(F32))

            def closed(*d):
                rr, cc = list(rv), list(cv)
                for k, val in zip(drows, d[:len(drows)]):
                    rr[k] = val
                for k, val in zip(dconsts, d[len(drows):]):
                    cc[k] = val
                return tuple(fn(*rr, *cc))

            _, vjp = jax.vjp(closed, *[rv[k] for k in drows], *[cv[k] for k in dconsts])
            grads = vjp(tuple(ct))
            for n, k in enumerate(drows):
                g = grads[n]
                if rows[k].hs == 0 and heads > 1:
                    shared[n] = g if shared[n] is None else shared[n] + g
                else:
                    if n == 0 and has_res:
                        g = g + res_ref[:, h * rows[k].w:(h + 1) * rows[k].w].astype(F32)
                    out, off, hs = place[n]
                    grow_refs[out][:, off + h * hs:off + h * hs + rows[k].w] = g.astype(row_dtype)
            for (v, off, hs), (_, io), r in zip(fills, fwins, fill_refs):
                lo = io + h * v.hs
                grow_refs[0][:, off + h * hs:off + h * hs + v.w] = r[:, lo:lo + v.w].astype(row_dtype)
            for n in range(len(dconsts)):
                g = grads[len(drows) + n]
                gc_sum[n] = g if gc_sum[n] is None else gc_sum[n] + g
        for n, k in enumerate(drows):
            if shared[n] is not None:
                g = shared[n]
                if n == 0 and has_res:
                    g = g + res_ref[...].astype(F32)
                grow_refs[place[n][0]][...] = g.astype(row_dtype)

        @pl.when(i == 0)
        def _():
            for n in range(len(dconsts)):
                gconst_refs[n][...] = gc_sum[n]

        @pl.when(i > 0)
        def _():
            for n in range(len(dconsts)):
                gconst_refs[n][...] += gc_sum[n]

    in_specs = [w[0] for w in wins] + [_const_spec(c) for c in consts] + [w[0] for w in cwins] + [w[0] for w in fwins]
    ops = [v.arr for v in rows] + list(consts) + [v.arr for v in cots] + [v.arr for v, _, _ in fills]
    if has_res:
        in_specs.append(pl.BlockSpec((tr, widths[0]), lambda i: (i, 0)))
        ops.append(res)
    out_specs = [pl.BlockSpec((tr, w), lambda i: (i, 0)) for w in widths]
    out_shape = [jax.ShapeDtypeStruct((S, w), row_dtype) for w in widths]
    for k in dconsts:
        out_specs.append(_const_spec(consts[k]))
        out_shape.append(jax.ShapeDtypeStruct(consts[k].shape, F32))
    return pl.pallas_call(body, name=name, grid=(S // tr,), in_specs=in_specs, out_specs=out_specs,
                          out_shape=out_shape, compiler_params=_cparams(("arbitrary",)))(*ops)


def _rms(x, g, n=None):
    n = x.shape[-1] if n is None else n
    ms = jnp.sum(x * x, axis=-1, keepdims=True) * (1.0 / n)
    return x * lax.rsqrt(ms + EPS) * g


def rms_fn(x, g):
    return (_rms(x, g),)


def qk_prep_fn(nope, rope, cos, sin, gn, gr, rot):
    ms = (jnp.sum(nope * nope, axis=-1, keepdims=True) + jnp.sum(rope * rope, axis=-1, keepdims=True)) * (1.0 / MLA_QK)
    r = lax.rsqrt(ms + EPS)
    on = nope * r * gn
    orr = rope * r * gr
    turned = jnp.dot(orr, rot, precision=lax.Precision.HIGHEST, preferred_element_type=F32)
    return on, orr * cos + turned * sin


def gla_out_fn(o, zr, g):
    return (_rms(o, g) * (zr * _sigmoid(zr)),)


def gate_fn(pre, b):
    t = pre + b
    return ((jnp.minimum(t, 0.0) - jnp.log(1.0 + jnp.exp(-jnp.abs(t)))) * (1.0 / GLA_TAU),)


def _attn_probs(q_ref, k_ref, scale, q0, kext):
    s = lax.dot_general(q_ref[...].astype(MXU_DTYPE), k_ref[0:kext, :].astype(MXU_DTYPE), _DIMS["nt"],
                        preferred_element_type=F32) * scale
    if q0 is not None:
        qc = (q0 + lax.broadcasted_iota(jnp.int32, s.shape, 0)) // CHUNK
        kc = lax.broadcasted_iota(jnp.int32, s.shape, 1) // CHUNK
        s = jnp.where(kc <= qc, s, -1e30)
    m = jnp.max(s, axis=-1, keepdims=True)
    e = jnp.exp(s - m)
    return e / jnp.sum(e, axis=-1, keepdims=True)


def _per_query_block(one, causal, nq, tq, Sk):
    if not causal:
        one(None, Sk, None)
        return
    assert tq % CHUNK == 0
    for ib in range(nq):
        pl.when(pl.program_id(1) == ib)(functools.partial(one, ib * tq, min(Sk, (ib + 1) * tq), ib))


def attn_fwd(q, k, v, *, heads, dk, dv, v_off, v_hs, scale, causal, name, tq=256, comm=None):
    Sq, Sk = q.shape[0], k.shape[0]
    tq = _tile(Sq, tq, 8)

    def body(q_ref, k_ref, v_ref, o_ref):
        def one(q0, kext, ib):
            p = _attn_probs(q_ref, k_ref, scale, q0, kext)
            o_ref[...] = jnp.dot(p.astype(MXU_DTYPE), v_ref[0:kext, :].astype(MXU_DTYPE),
                                 preferred_element_type=F32).astype(o_ref.dtype)

        _per_query_block(one, causal, Sq // tq, tq, Sk)

    return _pcall(
        body, [q, k, v], name=name, grid=(heads, Sq // tq),
        in_specs=[pl.BlockSpec((tq, dk), lambda h, i: (i, h)), pl.BlockSpec((Sk, dk), lambda h, i: (0, h)),
                  pl.BlockSpec((Sk, dv), lambda h, i: (0, v_off + h * v_hs))],
        out_specs=pl.BlockSpec((tq, dv), lambda h, i: (i, h)),
        out_shape=jax.ShapeDtypeStruct((Sq, heads * dv), BF16), sem=("parallel", "parallel"), comm=comm)


def attn_bwd(q, k, v, do, *, heads, dk, dv, v_off, v_hs, scale, causal, name, tq=256, comm=None):
    Sq, Sk = q.shape[0], k.shape[0]
    tq = _tile(Sq, tq, 8)

    def body(q_ref, k_ref, v_ref, do_ref, dq_ref, dk_ref, dv_ref):
        @pl.when(pl.program_id(1) == 0)
        def _():
            dk_ref[...] = jnp.zeros_like(dk_ref)
            dv_ref[...] = jnp.zeros_like(dv_ref)

        def one(q0, kext, ib):
            p = _attn_probs(q_ref, k_ref, scale, q0, kext)
            dob = do_ref[...].astype(MXU_DTYPE)
            dp = lax.dot_general(dob, v_ref[0:kext, :].astype(MXU_DTYPE), _DIMS["nt"], preferred_element_type=F32)
            delta = jnp.sum(p * dp, axis=-1, keepdims=True)
            ds = (p * (dp - delta) * scale).astype(MXU_DTYPE)
            dq_ref[...] = jnp.dot(ds, k_ref[0:kext, :].astype(MXU_DTYPE), preferred_element_type=F32)
            dk_ref[0:kext, :] += lax.dot_general(ds, q_ref[...].astype(MXU_DTYPE), _DIMS["tn"],
                                                 preferred_element_type=F32)
            dv_ref[0:kext, :] += lax.dot_general(p.astype(MXU_DTYPE), dob, _DIMS["tn"], preferred_element_type=F32)

        _per_query_block(one, causal, Sq // tq, tq, Sk)

    return _pcall(
        body, [q, k, v, do], name=name, grid=(heads, Sq // tq),
        in_specs=[pl.BlockSpec((tq, dk), lambda h, i: (i, h)), pl.BlockSpec((Sk, dk), lambda h, i: (0, h)),
                  pl.BlockSpec((Sk, dv), lambda h, i: (0, v_off + h * v_hs)),
                  pl.BlockSpec((tq, dv), lambda h, i: (i, h))],
        out_specs=[pl.BlockSpec((tq, dk), lambda h, i: (i, h)), pl.BlockSpec((Sk, dk), lambda h, i: (0, h)),
                   pl.BlockSpec((Sk, dv), lambda h, i: (0, h))],
        out_shape=[jax.ShapeDtypeStruct((Sq, heads * dk), F32), jax.ShapeDtypeStruct((Sk, heads * dk), F32),
                   jax.ShapeDtypeStruct((Sk, heads * dv), F32)],
        sem=("parallel", "arbitrary"), comm=comm)


def _gla_chunk(k, g, tri_ref):
    b = jnp.dot(tri_ref[...], g, precision=lax.Precision.HIGHEST, preferred_element_type=F32)
    b_end = jnp.sum(g, axis=0, keepdims=True)
    e = jnp.exp(b_end - b)
    return k * e, e, jnp.exp(b_end)


def _gla_windows(z, q_off, k_off, v_off, rows_of):
    H, DK, DV = GLA_HEADS, GLA_DK, GLA_DV
    specs, inner = [], []
    for off, ext in ((q_off, H * DK), (k_off, H * DK), (v_off, H * DV)):
        ww, blk, io = _window(z.shape[1], off, ext)
        specs.append(pl.BlockSpec((CHUNK, ww), lambda c, blk=blk: (rows_of(c), blk)))
        inner.append(io)
    return specs, inner


def gla_fwd(z, la, tri, *, q_off, k_off, v_off, name, comm=None):
    S = z.shape[0]
    nchunk = S // CHUNK
    H, DK, DV = GLA_HEADS, GLA_DK, GLA_DV
    qscale = DK ** -0.5
    zspecs, (qi, ki, vi) = _gla_windows(z, q_off, k_off, v_off, lambda c: c)

    def body(q_ref, k_ref, v_ref, la_ref, tri_ref, o_ref, st_ref, state):
        @pl.when(pl.program_id(0) == 0)
        def _():
            state[...] = jnp.zeros_like(state)

        for h in range(H):
            dks, dvs = slice(h * DK, (h + 1) * DK), slice(h * DV, (h + 1) * DV)
            k = k_ref[:, ki + h * DK:ki + (h + 1) * DK].astype(F32)
            v = v_ref[:, vi + h * DV:vi + (h + 1) * DV]
            q = q_ref[:, qi + h * DK:qi + (h + 1) * DK].astype(F32)
            kdec, _, decay = _gla_chunk(k, la_ref[:, dks].astype(F32), tri_ref)
            ut = lax.dot_general(v.astype(MXU_DTYPE), kdec.astype(MXU_DTYPE), _DIMS["tn"], preferred_element_type=F32)
            new = state[h] * decay + ut
            state[h] = new
            st_ref[h] = new
            qs = (q * qscale).astype(MXU_DTYPE)
            o_ref[:, dvs] = lax.dot_general(qs, new.astype(MXU_DTYPE), _DIMS["nt"], preferred_element_type=F32)

    return _pcall(
        body, [z, z, z, la, tri], name=name, grid=(nchunk,),
        in_specs=zspecs + [pl.BlockSpec((CHUNK, H * DK), lambda c: (c, 0)), pl.BlockSpec((CHUNK, CHUNK), lambda c: (0, 0))],
        out_specs=[pl.BlockSpec((CHUNK, H * DV), lambda c: (c, 0)),
                   pl.BlockSpec((H, None, DV, DK), lambda c: (0, c, 0, 0))],
        out_shape=[jax.ShapeDtypeStruct((S, H * DV), F32), jax.ShapeDtypeStruct((H, nchunk, DV, DK), F32)],
        scratch_shapes=[pltpu.VMEM((H, DV, DK), F32)], sem=("arbitrary",), comm=comm)


def gla_bwd(z, la, tri, trit, states, do, *, q_off, k_off, v_off, name, comm=None):
    S = z.shape[0]
    nchunk = S // CHUNK
    H, DK, DV = GLA_HEADS, GLA_DK, GLA_DV
    qscale = DK ** -0.5
    last = nchunk - 1
    zspecs, (qi, ki, vi) = _gla_windows(z, q_off, k_off, v_off, lambda c: last - c)

    def body(q_ref, k_ref, v_ref, la_ref, tri_ref, trit_ref, st_ref, sp_ref, do_ref, dq_ref, dk_ref, dv_ref, dla_ref,
             dstate):
        c = pl.program_id(0)
        cc = last - c

        @pl.when(c == 0)
        def _():
            dstate[...] = jnp.zeros_like(dstate)

        for h in range(H):
            dks, dvs = slice(h * DK, (h + 1) * DK), slice(h * DV, (h + 1) * DV)
            kf = k_ref[:, ki + h * DK:ki + (h + 1) * DK].astype(F32)
            vb16 = v_ref[:, vi + h * DV:vi + (h + 1) * DV].astype(MXU_DTYPE)
            q = q_ref[:, qi + h * DK:qi + (h + 1) * DK].astype(F32)
            kdec, e, decay = _gla_chunk(kf, la_ref[:, dks].astype(F32), tri_ref)
            dob = do_ref[:, dvs].astype(MXU_DTYPE)
            stb = st_ref[h].astype(MXU_DTYPE)
            qs = (q * qscale).astype(MXU_DTYPE)
            dq_ref[:, dks] = jnp.dot(dob, stb, preferred_element_type=F32) * qscale
            dst = dstate[h] + lax.dot_general(dob, qs, _DIMS["tn"], preferred_element_type=F32)
            prev = jnp.where(cc > 0, sp_ref[h], 0.0)
            ddecay = jnp.sum(dst * prev, axis=0, keepdims=True)
            dstate[h] = dst * decay
            dub = dst.astype(MXU_DTYPE)
            dv_ref[:, dvs] = lax.dot_general(kdec.astype(MXU_DTYPE), dub, _DIMS["nt"], preferred_element_type=F32)
            dkdec = jnp.dot(vb16, dub, preferred_element_type=F32)
            dk_ref[:, dks] = dkdec * e
            w = dkdec * kf * e
            db_end = jnp.sum(w, axis=0, keepdims=True) + ddecay * decay
            dla_ref[:, dks] = db_end - jnp.dot(trit_ref[...], w, precision=lax.Precision.HIGHEST,
                                               preferred_element_type=F32)

    def rows(width):
        return pl.BlockSpec((CHUNK, width), lambda c: (last - c, 0))

    square = pl.BlockSpec((CHUNK, CHUNK), lambda c: (0, 0))
    return _pcall(
        body, [z, z, z, la, tri, trit, states, states, do], name=name, grid=(nchunk,),
        in_specs=zspecs + [rows(H * DK), square, square,
                           pl.BlockSpec((H, None, DV, DK), lambda c: (0, last - c, 0, 0)),
                           pl.BlockSpec((H, None, DV, DK), lambda c: (0, jnp.maximum(last - c - 1, 0), 0, 0)),
                           rows(H * DV)],
        out_specs=[rows(H * DK), rows(H * DK), rows(H * DV), rows(H * DK)],
        out_shape=[jax.ShapeDtypeStruct((S, H * DK), F32), jax.ShapeDtypeStruct((S, H * DK), F32),
                   jax.ShapeDtypeStruct((S, H * DV), F32), jax.ShapeDtypeStruct((S, H * DK), F32)],
        scratch_shapes=[pltpu.VMEM((H, DV, DK), F32)], sem=("arbitrary",), comm=comm)


def loss_head(y, target, *, name, tr=256):
    S, D = y.shape
    tr = _tile(S, tr, 8)

    def body(y_ref, t_ref, dy_ref, loss_ref):
        i = pl.program_id(0)
        err = y_ref[...] - t_ref[...]
        dy_ref[...] = err * (1.0 / D)
        part = jnp.zeros((1, LANE), F32) + 0.5 * jnp.sum(jnp.sum(err * err, axis=-1, keepdims=True) * (1.0 / D))

        @pl.when(i == 0)
        def _():
            loss_ref[...] = part

        @pl.when(i > 0)
        def _():
            loss_ref[...] += part

    spec = pl.BlockSpec((tr, D), lambda i: (i, 0))
    return pl.pallas_call(
        body, name=name, grid=(S // tr,), in_specs=[spec, spec],
        out_specs=[spec, pl.BlockSpec((1, LANE), lambda i: (0, 0))],
        out_shape=[jax.ShapeDtypeStruct((S, D), F32), jax.ShapeDtypeStruct((1, LANE), F32)],
        compiler_params=_cparams(("arbitrary",)))(y, target)


def _core_index():
    return lax.axis_index("c").astype(jnp.int32).reshape(1)


def _chip_slots():
    x, y, c = lax.axis_index("x"), lax.axis_index("y"), lax.axis_index("c")
    return jnp.stack([2 * x + y, 2 * (1 - x) + y, 2 * x + (1 - y), 2 * (1 - x) + (1 - y), c]).astype(jnp.int32)


def sum_chip_parts(own, parts, *, name, tr=1024):
    _, R, C = own.shape
    tr = _tile(R, tr, 8)

    def body(idx_ref, o_ref, p0_ref, p1_ref, p2_ref, out_ref):
        acc = o_ref[...].astype(F32) + p0_ref[...].astype(F32)
        acc = acc + p1_ref[...].astype(F32)
        out_ref[...] = acc + p2_ref[...].astype(F32)

    def slot(k):
        return pl.BlockSpec((None, tr, C), lambda i, idx: (idx[k], i, 0))

    grid_spec = pltpu.PrefetchScalarGridSpec(num_scalar_prefetch=1, grid=(R // tr,),
                                             in_specs=[slot(0), slot(1), slot(2), slot(3)], out_specs=slot(4))
    return pl.pallas_call(body, name=name, grid_spec=grid_spec, out_shape=jax.ShapeDtypeStruct((2, R, C), F32),
                          compiler_params=_cparams(("parallel",)))(_chip_slots(), own, parts, parts, parts)


def add_own_half(g, got, out_dtype, *, name, tr=1024):
    n, _, R, C = g.shape
    tr = _tile(R, tr, 8)

    def body(c_ref, a_ref, b_ref, o_ref):
        o_ref[...] = (a_ref[...].astype(F32) + b_ref[...].astype(F32)).astype(out_dtype)

    spec = pl.BlockSpec((None, tr, C), lambda s, i, c: (s, i, 0))
    grid_spec = pltpu.PrefetchScalarGridSpec(
        num_scalar_prefetch=1, grid=(n, R // tr),
        in_specs=[pl.BlockSpec((None, None, tr, C), lambda s, i, c: (s, c[0], i, 0)), spec], out_specs=spec)
    return pl.pallas_call(body, name=name, grid_spec=grid_spec, out_shape=jax.ShapeDtypeStruct((n, R, C), out_dtype),
                          compiler_params=_cparams(("parallel", "parallel")))(_core_index(), g, got)


def adamw(items, *, name, max_steps=16, behind=None):
    c1 = 1.0 / (1.0 - ADAM_B1 ** ADAM_STEP)
    c2 = 1.0 / (1.0 - ADAM_B2 ** ADAM_STEP)
    n = len(items)
    steps = max_steps
    while steps > 1 and any(it[0].shape[0] % (8 * steps) for it in items):
        steps //= 2
    tail = [] if behind is None else [behind]

    def body(*refs):
        for a in range(n):
            w_ref, g_ref, m_ref, v_ref = refs[4 * a:4 * a + 4]
            go_ref, d_ref, nm_ref, nv_ref = refs[4 * n + len(tail) + 4 * a:4 * n + len(tail) + 4 * a + 4]
            gv = g_ref[...]
            go_ref[...] = gv
            nm = ADAM_B1 * m_ref[...] + (1.0 - ADAM_B1) * gv
            nv = ADAM_B2 * v_ref[...] + (1.0 - ADAM_B2) * (gv * gv)
            nm_ref[...] = nm
            nv_ref[...] = nv
            d_ref[...] = -ADAM_LR * ((nm * c1) / (jnp.sqrt(nv * c2) + ADAM_EPS) + ADAM_WD * w_ref[...])

    ops, in_specs, out_specs, out_shape = [], [], [], []
    for w, g, m, v in items:
        R, C = w.shape
        spec = pl.BlockSpec((R // steps, C), lambda i: (i, 0))
        ops += [w, g, m, v]
        in_specs += [spec] * 4
        out_specs += [spec] * 4
        out_shape += [jax.ShapeDtypeStruct((R, C), F32)] * 4
    flat = _pcall(body, ops + tail, name=name, grid=(steps,), in_specs=in_specs + [ANY] * len(tail), out_specs=out_specs,
                  out_shape=out_shape, sem=("parallel",))
    return [tuple(flat[4 * a:4 * a + 4]) for a in range(n)]


def _place():
    x, y, c = lax.axis_index("x"), lax.axis_index("y"), lax.axis_index("c")
    chips = [(1 - x, y), (x, 1 - y), (1 - x, 1 - y)]
    return x, y, c, chips


def _rcopy(src, dst, send, recv, j, to):
    return pltpu.make_async_remote_copy(src_ref=src, dst_ref=dst, send_sem=send.at[j], recv_sem=recv.at[j], device_id=to,
                                        device_id_type=MESH)


def gather_stage1(shards, split):
    n = len(shards)
    ins = [s.reshape(2, s.shape[0] // 2, s.shape[1]) if sp else s for s, sp in zip(shards, split)]
    outs = [jax.ShapeDtypeStruct((N_CHIPS,) + a.shape, a.dtype) for a in ins]

    def start(in_refs, out_refs, send, recv, base):
        x, y, c, chips = _place()
        mine = 2 * x + y
        for i in range(n):
            src = in_refs[i].at[c] if split[i] else in_refs[i]
            dst = out_refs[i].at[mine, c] if split[i] else out_refs[i].at[mine]
            for k, (px, py) in enumerate(chips):
                _rcopy(src, dst, send, recv, base + 3 * i + k, (px, py, c)).start()

    def wait(in_refs, out_refs, send, recv, base):
        x, y, c, chips = _place()
        for i in range(n):
            src = in_refs[i].at[c] if split[i] else in_refs[i]
            for k, (px, py) in enumerate(chips):
                dst = out_refs[i].at[2 * px + py, c] if split[i] else out_refs[i].at[2 * px + py]
                _rcopy(src, dst, send, recv, base + 3 * i + k, (px, py, c)).wait()

    return Comm(ins, outs, 3 * n, start, wait)


def gather_stage2(slots, shards, split):
    n = len(slots)
    own = [s.reshape(2, s.shape[0] // 2, s.shape[1]) if sp else s for s, sp in zip(shards, split)]

    def copies(in_refs, out_refs, send, recv, base):
        x, y, c, chips = _place()
        sib = (x, y, 1 - c)
        for i in range(n):
            j = base + 4 * i
            mine = out_refs[i].at[2 * x + y]
            yield _rcopy(in_refs[n + i], mine, send, recv, j + 3, sib), _rcopy(in_refs[n + i], mine, send, recv, j + 3, sib)
            if split[i]:
                for k, (px, py) in enumerate(chips):
                    s = 2 * px + py
                    yield (_rcopy(in_refs[i].at[s, c], out_refs[i].at[s, c], send, recv, j + k, sib),
                           _rcopy(in_refs[i].at[s, c], out_refs[i].at[s, 1 - c], send, recv, j + k, sib))

    def start(*a):
        for out, _ in copies(*a):
            out.start()

    def wait(*a):
        for _, back in copies(*a):
            back.wait()

    return Comm(list(slots) + own, [jax.ShapeDtypeStruct(s.shape, s.dtype) for s in slots], 4 * n, start, wait,
                {i: i for i in range(n)})


def swap_halves(gs):
    n = len(gs)

    def copies(in_refs, out_refs, send, recv, base):
        x, y, c, _ = _place()
        return [_rcopy(in_refs[i].at[s, 1 - c], out_refs[i].at[s], send, recv, base + N_CHIPS * i + s, (x, y, 1 - c))
                for i in range(n) for s in range(N_CHIPS)]

    def start(*a):
        for cp in copies(*a):
            cp.start()

    def wait(*a):
        for cp in copies(*a):
            cp.wait()

    return Comm(gs, [jax.ShapeDtypeStruct((N_CHIPS,) + g.shape[2:], g.dtype) for g in gs], N_CHIPS * n, start, wait)


def exchange_chips(ps):
    n = len(ps)

    def start(in_refs, out_refs, send, recv, base):
        x, y, c, chips = _place()
        for i in range(n):
            for k, (px, py) in enumerate(chips):
                _rcopy(in_refs[i].at[2 * px + py], out_refs[i].at[2 * x + y], send, recv, base + 3 * i + k,
                       (px, py, c)).start()

    def wait(in_refs, out_refs, send, recv, base):
        x, y, c, chips = _place()
        for i in range(n):
            for k, (px, py) in enumerate(chips):
                _rcopy(in_refs[i].at[2 * px + py], out_refs[i].at[2 * px + py], send, recv, base + 3 * i + k,
                       (px, py, c)).wait()

    return Comm(ps, [jax.ShapeDtypeStruct(p.shape, p.dtype) for p in ps], 3 * n, start, wait)


def join_halves(fs):
    n = len(fs)

    def start(in_refs, out_refs, send, recv, base):
        x, y, c, _ = _place()
        for i in range(n):
            _rcopy(in_refs[i].at[c], out_refs[i].at[c], send, recv, base + i, (x, y, 1 - c)).start()

    def wait(in_refs, out_refs, send, recv, base):
        x, y, c, _ = _place()
        for i in range(n):
            _rcopy(in_refs[i].at[c], out_refs[i].at[1 - c], send, recv, base + i, (x, y, 1 - c)).wait()

    return Comm(fs, [jax.ShapeDtypeStruct(f.shape, f.dtype) for f in fs], n, start, wait, {i: i for i in range(n)})


def allreduce_small(v, *, name):
    m_per, n = v.shape

    def body(x_ref, sum_ref, all_ref, send_sems, recv_sems, local_sem):
        x, y, c, chips = _place()
        me, sibling = (x, y, c), (x, y, 1 - c)

        def rows(px, py, pc):
            return all_ref.at[pl.ds((4 * px + 2 * py + pc) * m_per, m_per), :]

        def copy(k, block, to, src=None):
            return pltpu.make_async_remote_copy(src_ref=rows(*block) if src is None else src, dst_ref=rows(*block),
                                                send_sem=send_sems.at[k], recv_sem=recv_sems.at[k], device_id=to,
                                                device_id_type=MESH)

        mine = pltpu.make_async_copy(x_ref, rows(*me), local_sem)
        mine.start()
        first = [copy(0, me, sibling, src=x_ref)]
        first += [copy(1 + j, me, (*chip, c), src=x_ref) for j, chip in enumerate(chips)]
        for cp in first:
            cp.start()
        passed = [copy(4 + j, (*chip, c), sibling) for j, chip in enumerate(chips)]
        for j, chip in enumerate(chips):
            copy(1 + j, (*chip, c), me).wait_recv()
            passed[j].start()
        copy(0, sibling, me).wait_recv()
        for j, chip in enumerate(chips):
            copy(4 + j, (*chip, 1 - c), me).wait_recv()
        for cp in first + passed:
            cp.wait_send()
        mine.wait()
        acc = all_ref[0:m_per, :]
        for d in range(1, N_DEV):
            acc = acc + all_ref[d * m_per:(d + 1) * m_per, :]
        sum_ref[...] = acc

    vm = pl.BlockSpec(memory_space=pltpu.VMEM)
    return pl.pallas_call(
        body, name=name, in_specs=[vm], out_specs=vm, out_shape=jax.ShapeDtypeStruct((m_per, n), F32),
        scratch_shapes=[pltpu.VMEM((N_DEV * m_per, n), F32), pltpu.SemaphoreType.DMA((7,)),
                        pltpu.SemaphoreType.DMA((7,)), pltpu.SemaphoreType.DMA],
    )(v)


def _cols_to_slots(w):
    r, c4 = w.shape
    return w.reshape(r, N_CHIPS, c4 // N_CHIPS).transpose(1, 0, 2)


def _slots_to_cols(w):
    n, r, c = w.shape
    return w.transpose(1, 0, 2).reshape(r, n * c)


def _pad_cols(a, width):
    return jnp.pad(a, ((0, 0), (0, width - a.shape[1])))


class InLayout:
    def __init__(self, q_rank, kv_rank):
        gk = GLA_HEADS * GLA_DK
        gv = GLA_HEADS * GLA_DV
        sizes = [q_rank, kv_rank, MLA_ROPE, gk, gk, gv, GLA_GATE_RANK, gv]
        names = ["zq", "zkv", "zkr", "gq", "gk", "gv", "zg", "zr"]
        starts = np.concatenate([[0], np.cumsum(sizes)[:-1]])
        self.ref = {n: (int(s), int(z)) for n, s, z in zip(names, starts, sizes)}
        self.ref_width = int(sum(sizes))
        self.order = ["gv", "zr", "zq", "gq", "gk", "zkv", "zkr", "zg"]
        self.off, self.size = {}, {}
        pos = 0
        for n in self.order:
            padded = -(-self.ref[n][1] // LANE) * LANE
            self.off[n], self.size[n] = pos, padded
            pos += padded
        self.width = pos
        self.shard = self.ref_width // N_CHIPS
        self.shard_pad = -(-self.shard // LANE) * LANE

    def _pieces(self, lo, hi):
        out = []
        while lo < hi:
            s = lo // self.shard
            end = min(hi, (s + 1) * self.shard)
            out.append((s * self.shard_pad + lo - s * self.shard, s * self.shard_pad + end - s * self.shard))
            lo = end
        return out

    def from_shards(self, zs):
        cols = []
        for n in self.order:
            start, size = self.ref[n]
            cols += [zs[:, a:b] for a, b in self._pieces(start, start + size)]
            if self.size[n] > size:
                cols.append(jnp.zeros((zs.shape[0], self.size[n] - size), zs.dtype))
        return jnp.concatenate(cols, axis=1)

    def to_shards(self, dz):
        names = sorted(self.ref, key=lambda n: self.ref[n][0])
        ref = jnp.concatenate([dz[:, self.off[n]:self.off[n] + self.ref[n][1]] for n in names], axis=1)
        ref = ref.reshape(dz.shape[0], N_CHIPS, self.shard)
        return jnp.pad(ref, ((0, 0), (0, 0), (0, self.shard_pad - self.shard))).reshape(dz.shape[0], -1)


def _pad_q_up(w):
    r = w.shape[0]
    w = w.reshape(r, MLA_HEADS, MLA_QK)
    w = jnp.pad(w, ((0, 0), (0, 0), (0, MLA_HEAD_PAD - MLA_QK)))
    return w.reshape(r, MLA_HEADS * MLA_HEAD_PAD)


def _unpad_q_up(g):
    r = g.shape[0]
    return g.reshape(r, MLA_HEADS, MLA_HEAD_PAD)[:, :, :MLA_QK].reshape(r, MLA_HEADS * MLA_QK)


def _rope_tables(positions):
    half = MLA_ROPE // 2
    inv_freq = ROPE_THETA ** (-jnp.arange(half, dtype=F32) / half)
    ang = positions.astype(F32).reshape(-1, 1) * inv_freq
    cos, sin = jnp.cos(ang), jnp.sin(ang)
    s = ang.shape[0]
    cosf = jnp.concatenate([cos, cos, jnp.ones((s, LANE - MLA_ROPE), F32)], axis=1)
    sinf = jnp.concatenate([sin, sin, jnp.zeros((s, LANE - MLA_ROPE), F32)], axis=1)
    rot = np.zeros((LANE, LANE), np.float32)
    for j in range(half):
        rot[j + half, j] = -1.0
        rot[j, j + half] = 1.0
    return cosf, sinf, jnp.asarray(rot)


SMALL = ["ffn1_norm", "mix_norm", "q_a_norm", "kv_a_norm", "mla_q_norm", "mla_k_norm", "gla_b_gate", "gla_out_norm",
         "mem_attn_norm", "mem_norm", "mem_q_norm", "mem_k_norm", "ffn2_norm"]
BIG = ["ffn1_w_gate", "ffn1_w_up", "ffn1_w_down", "w_in", "w_q_up", "w_kv_up", "w_out", "mem_w_q", "mem_w_k",
       "mem_w_v", "mem_w_o", "ffn2_w_gate", "ffn2_w_up", "ffn2_w_down"]
COL_SHARDED = {"ffn1_w_gate", "ffn1_w_up", "w_in", "w_q_up", "w_kv_up", "gla_w_gate2", "mem_w_o", "ffn2_w_gate", "ffn2_w_up"}
WEIGHTS = ["ffn1_norm", "ffn1_w_gate", "ffn1_w_up", "ffn1_w_down", "mix_norm", "w_in", "q_a_norm", "w_q_up", "kv_a_norm",
           "w_kv_up", "mla_q_norm", "mla_k_norm", "gla_w_gate2", "gla_b_gate", "gla_out_norm", "w_out", "mem_attn_norm",
           "mem_norm", "mem_w_q", "mem_w_k", "mem_w_v", "mem_w_o", "mem_q_norm", "mem_k_norm", "ffn2_norm", "ffn2_w_gate",
           "ffn2_w_up", "ffn2_w_down"]


def _pack_small(vals, rows=8):
    flat = jnp.concatenate([v.reshape(-1).astype(F32) for v in vals])
    n = flat.shape[0]
    per = -(-n // (rows * LANE)) * LANE
    return jnp.pad(flat, (0, rows * per - n)).reshape(rows, per)


def _unpack_small(packed, shapes):
    flat = packed.reshape(-1)
    out, pos = [], 0
    for s in shapes:
        n = int(np.prod(s))
        out.append(flat[pos:pos + n].reshape(s))
        pos += n
    return out


FFN1 = ["ffn1_w_gate", "ffn1_w_up", "ffn1_w_down"]
FFN2 = ["ffn2_w_gate", "ffn2_w_up", "ffn2_w_down"]
SLOT_WEIGHTS = {"ffn1_w_gate", "ffn1_w_up", "ffn2_w_gate", "ffn2_w_up", "w_in"}
MID_A = ["w_in", "w_q_up", "w_kv_up", "gla_w_gate2"]
MID_B = ["w_out", "mem_w_q", "mem_w_k", "mem_w_v", "mem_w_o"]


def _with(res, comm):
    return res if comm is not None else (res, None)


def kernel(x, mem, positions, ffn1_norm, ffn1_w_gate, ffn1_w_up, ffn1_w_down, mix_norm, w_in, q_a_norm, w_q_up, kv_a_norm, w_kv_up, mla_q_norm, mla_k_norm, gla_w_gate2, gla_b_gate, gla_out_norm, w_out, mem_attn_norm, mem_norm, mem_w_q, mem_w_k, mem_w_v, mem_w_o, mem_q_norm, mem_k_norm, ffn2_norm, ffn2_w_gate, ffn2_w_up, ffn2_w_down, loss_target, m_ffn1_norm, m_ffn1_w_gate, m_ffn1_w_up, m_ffn1_w_down, m_mix_norm, m_w_in, m_q_a_norm, m_w_q_up, m_kv_a_norm, m_w_kv_up, m_mla_q_norm, m_mla_k_norm, m_gla_w_gate2, m_gla_b_gate, m_gla_out_norm, m_w_out, m_mem_attn_norm, m_mem_norm, m_mem_w_q, m_mem_w_k, m_mem_w_v, m_mem_w_o, m_mem_q_norm, m_mem_k_norm, m_ffn2_norm, m_ffn2_w_gate, m_ffn2_w_up, m_ffn2_w_down, v_ffn1_norm, v_ffn1_w_gate, v_ffn1_w_up, v_ffn1_w_down, v_mix_norm, v_w_in, v_q_a_norm, v_w_q_up, v_kv_a_norm, v_w_kv_up, v_mla_q_norm, v_mla_k_norm, v_gla_w_gate2, v_gla_b_gate, v_gla_out_norm, v_w_out, v_mem_attn_norm, v_mem_norm, v_mem_w_q, v_mem_w_k, v_mem_w_v, v_mem_w_o, v_mem_q_norm, v_mem_k_norm, v_ffn2_norm, v_ffn2_w_gate, v_ffn2_w_up, v_ffn2_w_down):
    args = dict(locals())
    two_d = lambda a: a[0] if a.ndim == 3 else a
    W = {n: two_d(args[n]) for n in WEIGHTS}
    M1 = {n: two_d(args["m_" + n]) for n in WEIGHTS}
    V2 = {n: two_d(args["v_" + n]) for n in WEIGHTS}
    xs, mems, tgt = x[0], mem[0], loss_target[0]
    S, D = xs.shape
    chip = 2 * lax.axis_index("x") + lax.axis_index("y")

    q_rank, kv_rank = W["w_q_up"].shape[0], W["w_kv_up"].shape[0]
    lay = InLayout(q_rank, kv_rank)
    off = lay.off
    up_names, down_names = FFN1[:2], FFN1[2:]
    shard16 = {n: W[n].astype(BF16) for n in up_names}
    full = {}

    def stage1(names):
        return gather_stage1([shard16[n] for n in names], [n != "gla_w_gate2" for n in names])

    def stage2(names, slots):
        return gather_stage2(slots, [shard16[n] for n in names], [n != "gla_w_gate2" for n in names])

    def finish(names, slots):
        for n, s in zip(names, slots):
            s = s.reshape((N_CHIPS,) + shard16[n].shape)
            if n in SLOT_WEIGHTS:
                full[n] = s
            else:
                full[n] = _slots_to_cols(s) if n in COL_SHARDED else s.reshape(-1, s.shape[2])

    token = {"last": None}

    def begin(comm, name, after=None):
        started = start_comm(comm, name=name, after=token["last"] if after is None else after)
        token["last"] = started[-1]
        return comm, started

    first = [begin(stage1([n]), f"gather_start_{n}") for n in up_names]
    zero = token["last"][0, 0]
    later = [n for n in BIG + ["gla_w_gate2"] if n not in shard16]
    for n in later:
        shard16[n] = (W[n] + zero).astype(BF16)
    shard16["w_in"] = _pad_cols(shard16["w_in"], lay.shard_pad)
    n1 = row_fwd(rms_fn, [V(xs)], [W["ffn1_norm"]], [(D, BF16)], [(0, 0, 0)], name="ffn1_norm")[0]
    cosf, sinf, rot = _rope_tables(positions[0])
    gate_s1 = wait_comm(*first[0], [n1, cosf, sinf] + [shard16[n] for n in later], name=f"gather_wait_{up_names[0]}")
    finish(up_names[:1], run_comm(stage2(up_names[:1], gate_s1), name="pass_ffn1_gate"))
    up_s1 = wait_comm(*first[1], full[up_names[0]], name=f"gather_wait_{up_names[1]}")
    down1 = begin(stage1(down_names), "gather_start_ffn1_down", after=up_s1[0])
    mid_a_s1 = begin(stage1(MID_A), "gather_start_mid_a")
    finish(up_names[1:], run_comm(stage2(up_names[1:], up_s1), name="pass_ffn1_up"))
    tri = jnp.asarray(np.tril(np.ones((CHUNK, CHUNK), np.float32)))
    gqn = W["mla_q_norm"][:, :MLA_NOPE]
    gqr = _pad_cols(W["mla_q_norm"][:, MLA_NOPE:], LANE)
    gkn = W["mla_k_norm"][:, :MLA_NOPE]
    gkr = _pad_cols(W["mla_k_norm"][:, MLA_NOPE:], LANE)
    HP = MLA_HEAD_PAD
    mla_scale = MLA_QK ** -0.5
    mem_scale = MEM_HEAD_DIM ** -0.5
    mla_w = MLA_HEADS * MLA_V
    gla_w = GLA_HEADS * GLA_DV
    mem_w = MEM_HEADS * MEM_HEAD_DIM

    gate1, up1, act1 = ffn_up(n1, full["ffn1_w_gate"], full["ffn1_w_up"], name="ffn1_up", behind=token["last"])
    finish(down_names, run_comm(stage2(down_names, wait_comm(*down1, act1, name="gather_wait_ffn1_down")),
                                name="pass_ffn1_down"))
    mid_a1 = wait_comm(*mid_a_s1, act1, name="gather_wait_mid_a")
    mid_b = begin(stage1(MID_B), "gather_start_mid_b", after=mid_a1[0])
    x1, got = mm([(act1, full["ffn1_w_down"])], "nn", F32, alpha=0.5, res=xs, name="ffn1_down",
                 comm=stage2(MID_A, mid_a1), behind=token["last"])
    ffn1_saved = (n1, gate1, up1, act1)
    finish(MID_A, got)
    ffn2_s1 = [begin(stage1([n]), f"gather_start_{n}", after=x1 if n == FFN2[0] else None) for n in FFN2]
    w_q_up_p = _pad_q_up(full["w_q_up"])
    w_gate2_p = jnp.pad(full["gla_w_gate2"], ((0, LANE - GLA_GATE_RANK), (0, 0)))
    h = row_fwd(rms_fn, [V(x1)], [W["mix_norm"]], [(D, BF16)], [(0, 0, 0)], name="mix_norm")[0]
    mid_b1 = wait_comm(*mid_b, h, name="gather_wait_mid_b")
    z_shards, got = mm([(h, full["w_in"])], "nn", F32, name="in_proj", b_slots=True, comm=stage2(MID_B, mid_b1),
                       behind=token["last"])
    z = lay.from_shards(z_shards)
    finish(MID_B, got)
    qa = row_fwd(rms_fn, [V(z, off["zq"], q_rank)], [W["q_a_norm"]], [(q_rank, BF16)], [(0, 0, 0)], name="q_a_norm")[0]
    kva = row_fwd(rms_fn, [V(z, off["zkv"], kv_rank)], [W["kv_a_norm"]], [(kv_rank, BF16)], [(0, 0, 0)], name="kv_a_norm")[0]
    qraw = mm([(qa, w_q_up_p)], "nn", F32, name="q_up")
    kvraw = mm([(kva, full["w_kv_up"])], "nn", F32, name="kv_up")
    tabs = [V(cosf, diff=False), V(sinf, diff=False)]
    q_rows = [V(qraw, 0, LANE, HP), V(qraw, LANE, LANE, HP)] + tabs
    k_rows = [V(kvraw, 0, LANE, HP), V(z, off["zkr"], LANE, 0)] + tabs
    qh = row_fwd(qk_prep_fn, q_rows, [gqn, gqr, rot], [(MLA_HEADS * HP, BF16)], [(0, 0, HP), (0, LANE, HP)],
                 heads=MLA_HEADS, name="q_prep")[0]
    kh = row_fwd(qk_prep_fn, k_rows, [gkn, gkr, rot], [(MLA_HEADS * HP, BF16)], [(0, 0, HP), (0, LANE, HP)],
                 heads=MLA_HEADS, name="k_prep")[0]
    mla_kw = dict(heads=MLA_HEADS, dk=HP, dv=MLA_V, v_off=1, v_hs=2, scale=mla_scale, causal=True, tq=512)
    o_mla = attn_fwd(qh, kh, kvraw, name="mla_attn", **mla_kw)

    zg = z[:, off["zg"]:off["zg"] + LANE]
    pre = mm([(zg, w_gate2_p)], "nn", F32, name="gla_gate")
    la = row_fwd(gate_fn, [V(pre)], [W["gla_b_gate"]], [(pre.shape[1], F32)], [(0, 0, 0)], name="gla_log_decay")[0]
    gla_kw = dict(q_off=off["gq"], k_off=off["gk"], v_off=off["gv"])
    o_raw, states = gla_fwd(z, la, tri, name="gla_scan", **gla_kw)
    gla_rows = [V(o_raw, 0, GLA_DV, GLA_DV), V(z, off["zr"], GLA_DV, GLA_DV)]
    o_gla = row_fwd(gla_out_fn, gla_rows, [W["gla_out_norm"]], [(gla_w, BF16)], [(0, 0, GLA_DV)], heads=GLA_HEADS,
                    name="gla_out")[0]
    o_cat = jnp.concatenate([o_mla, o_gla], axis=1)
    f2 = [wait_comm(*ffn2_s1[k], o_cat, name=f"gather_wait_{FFN2[k]}")[0] for k in range(2)]
    x2, got = mm([(o_cat, full["w_out"])], "nn", F32, res=x1, name="out_proj", comm=stage2(FFN2[:2], f2))
    finish(FFN2[:2], got)

    hm = row_fwd(rms_fn, [V(x2)], [W["mem_attn_norm"]], [(D, BF16)], [(0, 0, 0)], name="mem_attn_norm")[0]
    mn = row_fwd(rms_fn, [V(mems)], [W["mem_norm"]], [(D, BF16)], [(0, 0, 0)], name="mem_norm")[0]
    qm_raw = mm([(hm, full["mem_w_q"])], "nn", F32, name="mem_q")
    km_raw = mm([(mn, full["mem_w_k"])], "nn", F32, name="mem_k")
    vm = mm([(mn, full["mem_w_v"])], "nn", F32, name="mem_v")
    hd = MEM_HEAD_DIM
    qm = row_fwd(rms_fn, [V(qm_raw, 0, hd, hd)], [W["mem_q_norm"]], [(mem_w, BF16)], [(0, 0, hd)], heads=MEM_HEADS,
                 name="mem_q_norm")[0]
    km = row_fwd(rms_fn, [V(km_raw, 0, hd, hd)], [W["mem_k_norm"]], [(mem_w, BF16)], [(0, 0, hd)], heads=MEM_HEADS,
                 name="mem_k_norm")[0]
    mem_kw = dict(heads=MEM_HEADS, dk=hd, dv=hd, v_off=0, v_hs=1, scale=mem_scale, causal=False)
    om = attn_fwd(qm, km, vm, name="mem_attn", **mem_kw)
    x3 = mm([(om, full["mem_w_o"])], "nn", F32, res=x2, name="mem_o")

    n2 = row_fwd(rms_fn, [V(x3)], [W["ffn2_norm"]], [(D, BF16)], [(0, 0, 0)], name="ffn2_norm")[0]
    f2_down = wait_comm(*ffn2_s1[2], n2, name=f"gather_wait_{FFN2[2]}")
    (gate2, up2, act2), got = ffn_up(n2, full["ffn2_w_gate"], full["ffn2_w_up"], name="ffn2_up",
                                     comm=stage2(FFN2[2:], f2_down))
    finish(FFN2[2:], got)
    y = mm([(act2, full["ffn2_w_down"])], "nn", F32, alpha=0.5, res=x3, name="ffn2_down")
    dy, loss_part = loss_head(y, tgt, name="loss_head")
    G = {"loss": loss_part[:, :1]}

    chip_sum, reduced = {}, {}

    def to_halves(n):
        g = G[n]
        if n in SLOT_WEIGHTS:
            s = g
        else:
            s = _cols_to_slots(g) if n in COL_SHARDED else g.reshape(N_CHIPS, g.shape[0] // N_CHIPS, g.shape[1])
        return s.reshape(N_CHIPS, 2, s.shape[1] // 2, s.shape[2])

    def add2(names, halves, got):
        for n, a, b in zip(names, halves, got):
            chip_sum[n] = add_own_half(a, b, BF16, name=f"rs_add2_{n}")

    to_join = []

    def add4_join(names, parts):
        for n, p in zip(names, parts):
            to_join.append((n, sum_chip_parts(chip_sum[n], p, name=f"rs_add4_{n}")))

    def with_joins(comm):
        names, totals = [n for n, _ in to_join], [t for _, t in to_join]
        to_join.clear()
        if not names:
            return comm, lambda got: got
        own = 0 if comm is None else len(comm.out_shapes)
        joined = join_halves(totals)

        def split(got):
            for n, b in zip(names, got[own:]):
                reduced[n] = b.reshape(-1, b.shape[2])[:, :W[n].shape[1]]
            return got[:own]

        return (joined if comm is None else merge_comms(comm, joined)), split

    def flush_joins():
        comm, split = with_joins(None)
        if comm is not None:
            split(run_comm(comm, name=f"rs_join_{len(reduced)}"))

    in_flight = []

    def xchg_start(names):
        in_flight.append((names,) + begin(exchange_chips([chip_sum[n] for n in names]), f"xchg_start_{names[0]}"))

    def xchg_wait(after, count=1):
        for _ in range(count):
            names, comm, started = in_flight.pop(0)
            add4_join(names, wait_comm(comm, started, after, name=f"xchg_wait_{names[0]}"))

    def ffn_backward(dout, xin, tag, saved, dact_comm=None, after_dact=None):
        n_, gate, up, act = saved
        nd, ng, nu = f"{tag}_w_down", f"{tag}_w_gate", f"{tag}_w_up"
        (dgate, dup), got0 = _with(ffn_dact(dout, full[nd], gate, up, 0.5, name=f"{tag}_dact", comm=dact_comm,
                                            behind=token["last"]), dact_comm)
        if after_dact:
            after_dact(got0)
        G[nd] = mm([(act, dout)], "tn", F32, alpha=0.5, name=f"{tag}_dwd", tm=1408, tn=1024, behind=token["last"])
        hd_ = to_halves(nd)
        comm, split = with_joins(swap_halves([hd_]))
        G[ng], got = mm([(n_, dgate)], "tn", F32, name=f"{tag}_dwg", out_slots=True, tm=1024, tn=1408, rows_inner=True,
                        comm=comm)
        add2([nd], [hd_], split(got))
        xchg_start([nd])
        hg = to_halves(ng)
        G[nu], got_g = mm([(n_, dup)], "tn", F32, name=f"{tag}_dwu", out_slots=True, tm=1024, tn=1408, rows_inner=True,
                          comm=swap_halves([hg]), behind=token["last"])
        add2([ng], [hg], got_g)
        xchg_start([ng])
        hu = to_halves(nu)
        dn, got_u = mm([(dgate, full[ng]), (dup, full[nu])], "nt", F32, name=f"{tag}_dn", b_slots=True, tn=1024, tk=1408,
                       comm=swap_halves([hu]), behind=token["last"])
        add2([nu], [hu], got_u)
        xchg_start([nu])
        dx, G[f"{tag}_norm"] = row_bwd(rms_fn, [V(xin)], [W[f"{tag}_norm"]], [V(dn)], const_diff=[True], res=dout,
                                       name=f"{tag}_dnorm")
        return dx

    g3 = ffn_backward(dy, x3, "ffn2", (n2, gate2, up2, act2))
    xchg_wait(g3)

    d_om = mm([(g3, full["mem_w_o"])], "nt", F32, name="mem_o_dx", behind=token["last"])
    G["mem_w_o"] = mm([(om, g3)], "tn", F32, name="mem_o_dw")
    dqm, dkm, dvm = attn_bwd(qm, km, vm, d_om, name="mem_attn_bwd", **mem_kw)
    dqm_raw, G["mem_q_norm"] = row_bwd(rms_fn, [V(qm_raw, 0, hd, hd)], [W["mem_q_norm"]], [V(dqm, 0, hd, hd)],
                                       const_diff=[True], heads=MEM_HEADS, row_dtype=BF16, name="mem_q_norm_bwd")
    dkm_raw, G["mem_k_norm"] = row_bwd(rms_fn, [V(km_raw, 0, hd, hd)], [W["mem_k_norm"]], [V(dkm, 0, hd, hd)],
                                       const_diff=[True], heads=MEM_HEADS, row_dtype=BF16, name="mem_k_norm_bwd")
    dhm = mm([(dqm_raw, full["mem_w_q"])], "nt", F32, name="mem_q_dx")
    G["mem_w_q"] = mm([(hm, dqm_raw)], "tn", F32, name="mem_q_dw")
    dmn = mm([(dkm_raw, full["mem_w_k"]), (dvm, full["mem_w_v"])], "nt", F32, name="mem_kv_dx")
    G["mem_w_k"] = mm([(mn, dkm_raw)], "tn", F32, name="mem_k_dw")
    G["mem_w_v"] = mm([(mn, dvm)], "tn", F32, name="mem_v_dw")
    _, G["mem_norm"] = row_bwd(rms_fn, [V(mems)], [W["mem_norm"]], [V(dmn)], const_diff=[True], row_dtype=BF16,
                               name="mem_norm_bwd")
    g2, G["mem_attn_norm"] = row_bwd(rms_fn, [V(x2)], [W["mem_attn_norm"]], [V(dhm)], const_diff=[True], res=g3,
                                     name="mem_attn_norm_bwd")

    xchg_wait(g2, 2)

    d_ocat = mm([(g2, full["w_out"])], "nt", F32, name="out_proj_dx")
    G["w_out"] = mm([(o_cat, g2)], "tn", F32, name="out_proj_dw")

    d_oraw, d_zr, G["gla_out_norm"] = row_bwd(gla_out_fn, gla_rows, [W["gla_out_norm"]],
                                              [V(d_ocat, mla_w, GLA_DV, GLA_DV)], const_diff=[True], heads=GLA_HEADS,
                                              name="gla_out_bwd")
    mid_b_halves = [to_halves(n) for n in MID_B]
    comm, split = with_joins(swap_halves(mid_b_halves))
    (d_gq, d_gk, d_gv, d_la), got = gla_bwd(z, la, tri, tri.T, states, d_oraw, name="gla_scan_bwd", comm=comm, **gla_kw)
    add2(MID_B, mid_b_halves, split(got))
    xchg_start(MID_B)
    d_pre, G["gla_b_gate"] = row_bwd(gate_fn, [V(pre)], [W["gla_b_gate"]], [V(d_la)], const_diff=[True], row_dtype=BF16,
                                     name="gla_log_decay_bwd")
    d_zg = mm([(d_pre, w_gate2_p)], "nt", BF16, name="gla_gate_dx", behind=token["last"])
    G["gla_w_gate2"] = mm([(zg, d_pre)], "tn", F32, name="gla_gate_dw")[:GLA_GATE_RANK]

    comm, split = with_joins(None)
    (d_qh, d_kh, d_v), got = _with(attn_bwd(qh, kh, kvraw, d_ocat, name="mla_attn_bwd", comm=comm, **mla_kw), comm)
    split(got)
    cq = [V(d_qh, 0, LANE, HP), V(d_qh, LANE, LANE, HP)]
    ck = [V(d_kh, 0, LANE, HP), V(d_kh, LANE, LANE, HP)]
    d_qraw, d_gqn, d_gqr = row_bwd(qk_prep_fn, q_rows, [gqn, gqr, rot], cq, const_diff=[True, True, False],
                                   heads=MLA_HEADS, row_dtype=BF16, pack={0: (0, HP), 1: (LANE, HP)},
                                   pack_width=MLA_HEADS * HP, name="q_prep_bwd")
    d_kvraw, d_zkr, d_gkn, d_gkr = row_bwd(qk_prep_fn, k_rows, [gkn, gkr, rot], ck, const_diff=[True, True, False],
                                           heads=MLA_HEADS, row_dtype=BF16, pack={0: (0, HP)}, pack_width=MLA_HEADS * HP,
                                           fills=[(V(d_v, 0, MLA_V, MLA_V), LANE, HP)], name="k_prep_bwd")
    G["mla_q_norm"] = jnp.concatenate([d_gqn, d_gqr[:, :MLA_ROPE]], axis=1)
    G["mla_k_norm"] = jnp.concatenate([d_gkn, d_gkr[:, :MLA_ROPE]], axis=1)
    d_qa = mm([(d_qraw, w_q_up_p)], "nt", F32, name="q_up_dx")
    G["w_q_up"] = _unpad_q_up(mm([(qa, d_qraw)], "tn", F32, name="q_up_dw"))
    d_kva = mm([(d_kvraw, full["w_kv_up"])], "nt", F32, name="kv_up_dx")
    G["w_kv_up"] = mm([(kva, d_kvraw)], "tn", F32, name="kv_up_dw")
    d_zq, G["q_a_norm"] = row_bwd(rms_fn, [V(z, off["zq"], q_rank)], [W["q_a_norm"]], [V(d_qa)], const_diff=[True],
                                  row_dtype=BF16, name="q_a_norm_bwd")
    d_zkv, G["kv_a_norm"] = row_bwd(rms_fn, [V(z, off["zkv"], kv_rank)], [W["kv_a_norm"]], [V(d_kva)], const_diff=[True],
                                    row_dtype=BF16, name="kv_a_norm_bwd")

    seg = {"gv": d_gv, "zr": d_zr, "zq": d_zq, "gq": d_gq, "gk": d_gk, "zkv": d_zkv, "zkr": d_zkr, "zg": d_zg}
    dz = jnp.concatenate([_pad_cols(seg[n].astype(BF16), lay.size[n]) for n in lay.order], axis=1)
    xchg_wait(dz)
    comm, split = with_joins(None)
    dz_shards = lay.to_shards(dz)
    dh, got = _with(mm([(dz_shards, full["w_in"])], "nt", F32, name="in_proj_dx", b_slots=True, comm=comm), comm)
    split(got)
    G["w_in"] = mm([(h, dz_shards)], "tn", F32, name="in_proj_dw", out_slots=True)
    g1, G["mix_norm"] = row_bwd(rms_fn, [V(x1)], [W["mix_norm"]], [V(dh)], const_diff=[True], res=g2,
                                name="mix_norm_bwd")

    mid_a = [n for n in MID_A if n != "gla_w_gate2"]
    mid_a_halves = [to_halves(n) for n in mid_a]

    def mid_a_sums(got):
        add2(mid_a, mid_a_halves, got)
        xchg_start(mid_a)

    gx = ffn_backward(g1, xs, "ffn1", ffn1_saved, dact_comm=swap_halves(mid_a_halves), after_dact=mid_a_sums)
    xchg_wait(gx, 2)

    grad, delta, new_m, new_v = {}, {}, {}, {}

    def adam_group(names, tag, behind=None):
        if any(n not in reduced for n in names):
            flush_joins()
        res = adamw([(W[n], reduced[n], M1[n], V2[n]) for n in names], name=f"adamw_{tag}", behind=behind)
        for n, (g_, d_, m_, v_) in zip(names, res):
            grad[n], delta[n], new_m[n], new_v[n] = g_, d_, m_, v_

    adam_group(FFN2, "ffn2", behind=token["last"])
    adam_group(mid_a + MID_B, "mid", behind=token["last"])
    adam_group(FFN1[2:], "ffn1_down", behind=token["last"])
    xchg_wait(delta[FFN1[2]], 2)
    adam_group(FFN1[:2], "ffn1_up")

    small_names = SMALL + ["gla_w_gate2"]
    packed = small_names + ["loss"]
    small_sum = allreduce_small(_pack_small([G[n] for n in packed]), name="allreduce_small")
    small_g = dict(zip(packed, _unpack_small(small_sum, [G[n].shape for n in packed])))
    loss = small_g["loss"][0, 0]
    shard_c = W["gla_w_gate2"].shape[1]
    grad["gla_w_gate2"] = lax.dynamic_slice_in_dim(small_g["gla_w_gate2"], chip * shard_c, shard_c, axis=1)
    pw = _pack_small([W[n] for n in SMALL] + [W["gla_w_gate2"]])
    pg = _pack_small([small_g[n] for n in SMALL] + [grad["gla_w_gate2"]])
    pm = _pack_small([M1[n] for n in SMALL] + [M1["gla_w_gate2"]])
    pv = _pack_small([V2[n] for n in SMALL] + [V2["gla_w_gate2"]])
    (_, pd, pnm, pnv), = adamw([(pw, pg, pm, pv)], name="adamw_small")
    shapes = [W[n].shape for n in small_names]
    for n, d_, m_, v_ in zip(small_names, _unpack_small(pd, shapes), _unpack_small(pnm, shapes), _unpack_small(pnv, shapes)):
        delta[n], new_m[n], new_v[n] = d_, m_, v_
        if n != "gla_w_gate2":
            grad[n] = small_g[n]

    lead = lambda d: [d[n].reshape(args[n].shape) for n in WEIGHTS]
    return (loss, gx[None], *lead(grad), *lead(delta), *lead(new_m), *lead(new_v))
```

```python
import functools

import numpy as np
import jax
import jax.numpy as jnp
from jax import lax
from jax.experimental import pallas as pl
from jax.experimental.pallas import tpu as pltpu

F32 = jnp.float32
BF16 = jnp.bfloat16
MXU_DTYPE = jnp.bfloat16
MESH = pl.DeviceIdType.MESH
ANY = pl.BlockSpec(memory_space=pl.ANY)

LANE = 128
EPS = 1e-6
CHUNK = 64
MLA_HEADS = 8
MLA_NOPE = 128
MLA_ROPE = 64
MLA_QK = MLA_NOPE + MLA_ROPE
MLA_V = 128
MLA_HEAD_PAD = 2 * LANE
ROPE_THETA = 10000.0
GLA_HEADS = 4
GLA_DK = 128
GLA_DV = 256
GLA_GATE_RANK = 16
GLA_TAU = 16.0
MEM_HEADS = 4
MEM_HEAD_DIM = 128
N_CHIPS = 4
N_DEV = 8

ADAM_LR = 0.001
ADAM_B1 = 0.9
ADAM_B2 = 0.999
ADAM_EPS = 1e-08
ADAM_WD = 0.01
ADAM_STEP = 10

VMEM_LIMIT = 56 * 1024 * 1024


def _cparams(sem=None):
    if sem is None:
        return pltpu.CompilerParams(vmem_limit_bytes=VMEM_LIMIT)
    return pltpu.CompilerParams(dimension_semantics=sem, vmem_limit_bytes=VMEM_LIMIT)


def _tile(dim, pref, unit=LANE):
    if dim <= pref:
        return dim
    t = (pref // unit) * unit
    while t > unit and dim % t:
        t -= unit
    assert dim % t == 0, (dim, pref, unit)
    return t


class Comm:
    def __init__(self, ins, out_shapes, nsem, start, wait, aliases=None):
        self.ins, self.out_shapes, self.nsem = list(ins), list(out_shapes), nsem
        self.start, self.wait, self.aliases = start, wait, dict(aliases or {})


def merge_comms(a, b):
    ai, ao = len(a.ins), len(a.out_shapes)

    def start(ins, outs, send, recv, base):
        a.start(ins[:ai], outs[:ao], send, recv, base)
        b.start(ins[ai:], outs[ao:], send, recv, base + a.nsem)

    def wait(ins, outs, send, recv, base):
        a.wait(ins[:ai], outs[:ao], send, recv, base)
        b.wait(ins[ai:], outs[ao:], send, recv, base + a.nsem)

    aliases = dict(a.aliases)
    aliases.update({ai + i: ao + o for i, o in b.aliases.items()})
    return Comm(a.ins + b.ins, a.out_shapes + b.out_shapes, a.nsem + b.nsem, start, wait, aliases)


def run_comm(comm, *, name):
    ni, no = len(comm.ins), len(comm.out_shapes)

    def body(*refs):
        ins, outs = refs[:ni], refs[ni:ni + no]
        send, recv = refs[ni + no:]
        comm.start(ins, outs, send, recv, 0)
        comm.wait(ins, outs, send, recv, 0)

    return pl.pallas_call(
        body, name=name, in_specs=[ANY] * ni, out_specs=[ANY] * no, out_shape=comm.out_shapes,
        input_output_aliases=comm.aliases,
        scratch_shapes=[pltpu.SemaphoreType.DMA((comm.nsem,)), pltpu.SemaphoreType.DMA((comm.nsem,))])(*comm.ins)


HBM = pl.BlockSpec(memory_space=pltpu.HBM)
SEM = pl.BlockSpec(memory_space=pltpu.SEMAPHORE)


def start_comm(comm, *, name, after=None):
    assert not comm.aliases
    ni, no = len(comm.ins), len(comm.out_shapes)
    tail = [] if after is None else [after]

    def body(*refs):
        srcs, lands = refs[:ni], refs[ni:ni + no]
        send, recv = refs[ni + no + len(tail)], refs[ni + no + len(tail) + 1]
        token = refs[-1]
        comm.start(srcs, lands, send, recv, 0)
        token[...] = jnp.zeros_like(token)

    through = [pltpu.HBM(a.shape, a.dtype) for a in comm.ins] + [pltpu.HBM(s.shape, s.dtype) for s in comm.out_shapes]
    ops = [pltpu.with_memory_space_constraint(a, pltpu.HBM) for a in comm.ins]
    ops += [pltpu.with_memory_space_constraint(lax.empty(s.shape, s.dtype), pltpu.HBM) for s in comm.out_shapes]
    ops += tail
    res = pl.pallas_call(
        body, name=name, in_specs=[HBM] * (ni + no) + [ANY] * len(tail),
        out_shape=[pltpu.SemaphoreType.DMA((comm.nsem,)), pltpu.SemaphoreType.DMA((comm.nsem,))] + through
        + [jax.ShapeDtypeStruct((8, LANE), F32)],
        out_specs=[SEM, SEM] + [HBM] * (ni + no) + [pl.BlockSpec(memory_space=pltpu.VMEM)],
        input_output_aliases={i: 2 + i for i in range(ni + no)},
        compiler_params=pltpu.CompilerParams(has_side_effects=pltpu.SideEffectType.DATAFLOW_SIDE_EFFECTING))(*ops)
    return res[0], res[1], list(res[2:2 + ni]), list(res[2 + ni:2 + ni + no]), res[-1]


def wait_comm(comm, started, after, *, name):
    send, recv, srcs, lands, _ = started
    ni, no = len(srcs), len(lands)
    after = list(after) if isinstance(after, (list, tuple)) else [after]

    def body(*refs):
        comm.wait(refs[:ni], refs[ni:ni + no], refs[ni + no], refs[ni + no + 1], 0)

    res = pl.pallas_call(
        body, name=name, in_specs=[HBM] * (ni + no) + [SEM, SEM] + [ANY] * len(after),
        out_shape=[pltpu.HBM(a.shape, a.dtype) for a in srcs + lands], out_specs=[HBM] * (ni + no),
        input_output_aliases={i: i for i in range(ni + no)},
        compiler_params=pltpu.CompilerParams(has_side_effects=pltpu.SideEffectType.DATAFLOW_SIDE_EFFECTING),
    )(*srcs, *lands, send, recv, *after)
    return list(res[ni:])


def _pcall(body, ops, *, name, grid, in_specs, out_specs, out_shape, sem, scratch_shapes=(), comm=None, behind=None):
    if behind is not None:
        n_real, inner = len(ops), body
        ops, in_specs = list(ops) + [behind], list(in_specs) + [ANY]

        def body(*refs):
            inner(*refs[:n_real], *refs[n_real + 1:])

    if comm is None:
        return pl.pallas_call(body, name=name, grid=grid, in_specs=in_specs, out_specs=out_specs, out_shape=out_shape,
                              scratch_shapes=list(scratch_shapes), compiler_params=_cparams(sem))(*ops)
    multi = isinstance(out_shape, (list, tuple))
    k_out_shape = list(out_shape) if multi else [out_shape]
    k_out_specs = list(out_specs) if multi else [out_specs]
    nki, nko, nks = len(ops), len(k_out_shape), len(scratch_shapes)
    nci, nco = len(comm.ins), len(comm.out_shapes)

    def wrapped(*refs):
        p = 0
        k_in = refs[p:p + nki]; p += nki
        c_in = refs[p:p + nci]; p += nci
        k_out = refs[p:p + nko]; p += nko
        c_out = refs[p:p + nco]; p += nco
        k_scr = refs[p:p + nks]; p += nks
        send, recv = refs[p:]
        first = pl.program_id(0) == 0
        last = pl.program_id(0) == grid[0] - 1
        for a in range(1, len(grid)):
            first = jnp.logical_and(first, pl.program_id(a) == 0)
            last = jnp.logical_and(last, pl.program_id(a) == grid[a] - 1)

        @pl.when(first)
        def _():
            comm.start(c_in, c_out, send, recv, 0)

        body(*k_in, *k_out, *k_scr)

        @pl.when(last)
        def _():
            comm.wait(c_in, c_out, send, recv, 0)

    res = pl.pallas_call(
        wrapped, name=name, grid=grid, in_specs=list(in_specs) + [ANY] * nci, out_specs=k_out_specs + [ANY] * nco,
        out_shape=k_out_shape + comm.out_shapes,
        input_output_aliases={nki + i: nko + o for i, o in comm.aliases.items()},
        scratch_shapes=list(scratch_shapes) + [pltpu.SemaphoreType.DMA((comm.nsem,)), pltpu.SemaphoreType.DMA((comm.nsem,))],
        compiler_params=_cparams(("arbitrary",) * len(grid)))(*ops, *comm.ins)
    k_res = list(res[:nko]) if multi else res[0]
    return k_res, list(res[nko:])


_DIMS = {"nn": (((1,), (0,)), ((), ())), "nt": (((1,), (1,)), ((), ())), "tn": (((0,), (0,)), ((), ()))}


def _blockspec(shape, index, rows_inner):
    return pl.BlockSpec(shape, (lambda j, i, k: index(i, j, k)) if rows_inner else index)


def mm(pairs, mode, out_dtype, *, name, alpha=1.0, res=None, tm=1024, tn=1024, tk=4096, b_slots=False, out_slots=False,
       rows_inner=False, comm=None, behind=None):
    a0, b0 = pairs[0]
    if b_slots:
        b_rows, b_cols = b0.shape[1], N_CHIPS * b0.shape[2]
    else:
        b_rows, b_cols = b0.shape
    (M, K) = a0.shape[::-1] if mode == "tn" else a0.shape
    N = b_rows if mode == "nt" else b_cols
    shard = (b_cols if b_slots else N) // N_CHIPS
    tm = _tile(M, tm)
    tn = _tile(shard if (out_slots or (b_slots and mode != "nt")) else N, tn)
    tk = _tile(shard if (b_slots and mode == "nt") else K, tk)
    nk = K // tk
    npairs = len(pairs)
    dims = _DIMS[mode]
    spec = functools.partial(_blockspec, rows_inner=rows_inner)
    if mode == "tn":
        a_spec = spec((tk, tm), lambda i, j, k: (k, i))
    else:
        a_spec = spec((tm, tk), lambda i, j, k: (i, k))
    per = shard // (tk if mode == "nt" else tn)
    if mode == "nt":
        b_spec = (spec((None, tn, tk), lambda i, j, k: (k // per, j, k % per)) if b_slots else
                  spec((tn, tk), lambda i, j, k: (j, k)))
    else:
        b_spec = (spec((None, tk, tn), lambda i, j, k: (j // per, k, j % per)) if b_slots else
                  spec((tk, tn), lambda i, j, k: (k, j)))
    if out_slots:
        assert res is None and mode != "nt"
        o_spec = spec((None, tm, tn), lambda i, j, k: (j // per, i, j % per))
        out_sds = jax.ShapeDtypeStruct((N_CHIPS, M, shard), out_dtype)
    else:
        o_spec = spec((tm, tn), lambda i, j, k: (i, j))
        out_sds = jax.ShapeDtypeStruct((M, N), out_dtype)
    has_res = res is not None

    def body(*refs):
        ab = refs[:2 * npairs]
        res_ref = refs[2 * npairs] if has_res else None
        o_ref = refs[2 * npairs + int(has_res)]

        def products():
            r = None
            for p in range(npairs):
                d = lax.dot_general(ab[2 * p][...].astype(MXU_DTYPE), ab[2 * p + 1][...].astype(MXU_DTYPE), dims,
                                    preferred_element_type=F32)
                r = d if r is None else r + d
            return r

        def finish(r):
            if alpha != 1.0:
                r = r * alpha
            if has_res:
                r = res_ref[...].astype(F32) + r
            o_ref[...] = r.astype(out_dtype)

        if nk == 1:
            finish(products())
            return
        acc = refs[-1]
        k = pl.program_id(2)

        @pl.when(k == 0)
        def _():
            acc[...] = jnp.zeros_like(acc)

        acc[...] += products()

        @pl.when(k == nk - 1)
        def _():
            finish(acc[...])

    ops, specs = [], []
    for a, b in pairs:
        ops += [a, b]
        specs += [a_spec, b_spec]
    if has_res:
        ops.append(res)
        specs.append(o_spec)
    blocks = (N // tn, M // tm) if rows_inner else (M // tm, N // tn)
    return _pcall(body, ops, name=name, grid=blocks + (nk,), in_specs=specs, out_specs=o_spec, out_shape=out_sds,
                  scratch_shapes=[pltpu.VMEM((tm, tn), F32)] if nk > 1 else [],
                  sem=("parallel", "parallel", "arbitrary"), comm=comm, behind=behind)


def _sigmoid(x):
    return 1.0 / (1.0 + jnp.exp(-x))


def ffn_up(n, wg, wu, *, name, tm=512, tn=1408, comm=None, behind=None):
    M, K = n.shape
    shard = wg.shape[2]
    N = N_CHIPS * shard
    tm, tn = _tile(M, tm), _tile(shard, tn)
    per = shard // tn
    w_spec = pl.BlockSpec((None, K, tn), lambda j, i: (j // per, 0, j % per))

    def body(n_ref, wg_ref, wu_ref, g_ref, u_ref, a_ref):
        nv = n_ref[...].astype(MXU_DTYPE)
        g = jnp.dot(nv, wg_ref[...].astype(MXU_DTYPE), preferred_element_type=F32)
        u = jnp.dot(nv, wu_ref[...].astype(MXU_DTYPE), preferred_element_type=F32)
        g_ref[...] = g.astype(g_ref.dtype)
        u_ref[...] = u.astype(u_ref.dtype)
        a_ref[...] = (g * _sigmoid(g) * u).astype(a_ref.dtype)

    o_spec = pl.BlockSpec((tm, tn), lambda j, i: (i, j))
    sds = jax.ShapeDtypeStruct((M, N), BF16)
    return _pcall(
        body, [n, wg, wu], name=name, grid=(N // tn, M // tm),
        in_specs=[pl.BlockSpec((tm, K), lambda j, i: (i, 0)), w_spec, w_spec],
        out_specs=[o_spec, o_spec, o_spec], out_shape=[sds, sds, sds], sem=("parallel", "parallel"), comm=comm,
        behind=behind)


def ffn_dact(dy, wd, gate, up, alpha, *, name, tm=512, tn=1408, comm=None, behind=None):
    M, K = dy.shape
    N = wd.shape[0]
    tm, tn = _tile(M, tm), _tile(N, tn)

    def body(dy_ref, wd_ref, g_ref, u_ref, dg_ref, du_ref):
        da = lax.dot_general(dy_ref[...].astype(MXU_DTYPE), wd_ref[...].astype(MXU_DTYPE), _DIMS["nt"],
                             preferred_element_type=F32) * alpha
        g = g_ref[...].astype(F32)
        u = u_ref[...].astype(F32)
        s = _sigmoid(g)
        du_ref[...] = (da * (g * s)).astype(du_ref.dtype)
        dg_ref[...] = (da * u * (s * (1.0 + g * (1.0 - s)))).astype(dg_ref.dtype)

    o_spec = pl.BlockSpec((tm, tn), lambda j, i: (i, j))
    sds = jax.ShapeDtypeStruct((M, N), BF16)
    return _pcall(
        body, [dy, wd, gate, up], name=name, grid=(N // tn, M // tm),
        in_specs=[pl.BlockSpec((tm, K), lambda j, i: (i, 0)), pl.BlockSpec((tn, K), lambda j, i: (j, 0)), o_spec, o_spec],
        out_specs=[o_spec, o_spec], out_shape=[sds, sds], sem=("parallel", "parallel"), comm=comm, behind=behind)


def _window(width, off, ext):
    ww = LANE
    while ww < width:
        if ww >= ext and off // ww == (off + ext - 1) // ww and width % ww == 0:
            break
        ww *= 2
    else:
        ww = width
    return ww, off // ww, off - (off // ww) * ww


class V:
    def __init__(self, arr, off=0, w=None, hs=0, diff=True):
        self.arr, self.off, self.hs, self.diff = arr, off, hs, diff
        self.w = arr.shape[1] - off if w is None else w

    def window(self, heads, tr):
        ww, blk, inner = _window(self.arr.shape[1], self.off, (heads - 1) * self.hs + self.w)
        return pl.BlockSpec((tr, ww), lambda i, blk=blk: (i, blk)), inner


def _const_spec(c):
    return pl.BlockSpec(c.shape, lambda i: (0, 0))


def row_fwd(fn, rows, consts, outs, out_map, *, heads=1, tr=256, name):
    S = rows[0].arr.shape[0]
    tr = _tile(S, tr, 8)
    wins = [v.window(heads, tr) for v in rows]
    nr, nc = len(rows), len(consts)

    def body(*refs):
        row_refs, const_refs, out_refs = refs[:nr], refs[nr:nr + nc], refs[nr + nc:]
        cv = [c[...].astype(F32) for c in const_refs]
        for h in range(heads):
            rv = []
            for v, (_, io), r in zip(rows, wins, row_refs):
                lo = io + h * v.hs
                rv.append(r[:, lo:lo + v.w].astype(F32))
            res = fn(*rv, *cv)
            for (ai, off, hs), o in zip(out_map, res):
                lo = off + h * hs
                out_refs[ai][:, lo:lo + o.shape[1]] = o.astype(out_refs[ai].dtype)

    return pl.pallas_call(
        body, name=name, grid=(S // tr,),
        in_specs=[w[0] for w in wins] + [_const_spec(c) for c in consts],
        out_specs=[pl.BlockSpec((tr, w), lambda i: (i, 0)) for w, _ in outs],
        out_shape=[jax.ShapeDtypeStruct((S, w), d) for w, d in outs],
        compiler_params=_cparams(("parallel",)))(*[v.arr for v in rows], *consts)


def row_bwd(fn, rows, consts, cots, *, const_diff, heads=1, tr=256, res=None, row_dtype=F32, pack=None, pack_width=0,
            fills=(), name):
    S = rows[0].arr.shape[0]
    tr = _tile(S, tr, 8)
    pack = dict(pack or {})
    nr, nc, nct, nf = len(rows), len(consts), len(cots), len(fills)
    wins = [v.window(heads, tr) for v in rows]
    cwins = [v.window(heads, tr) for v in cots]
    fwins = [v.window(heads, tr) for v, _, _ in fills]
    drows = [k for k, v in enumerate(rows) if v.diff]
    dconsts = [k for k in range(nc) if const_diff[k]]
    has_res = res is not None
    assert not (has_res and 0 in pack)
    widths = [pack_width] if pack else []
    place = []
    for n, k in enumerate(drows):
        if n in pack:
            place.append((0,) + tuple(pack[n]))
        else:
            place.append((len(widths), 0, rows[k].w))
            widths.append(rows[k].w * (heads if rows[k].hs else 1))

    def body(*refs):
        row_refs = refs[:nr]
        const_refs = refs[nr:nr + nc]
        cot_refs = refs[nr + nc:nr + nc + nct]
        p = nr + nc + nct
        fill_refs = refs[p:p + nf]
        p += nf
        res_ref = refs[p] if has_res else None
        p += int(has_res)
        grow_refs = refs[p:p + len(widths)]
        gconst_refs = refs[p + len(widths):]
        i = pl.program_id(0)
        cv = [c[...].astype(F32) for c in const_refs]
        shared = [None] * len(drows)
        gc_sum = [None] * len(dconsts)
        for h in range(heads):
            rv = []
            for v, (_, io), r in zip(rows, wins, row_refs):
                lo = io + h * v.hs
                rv.append(r[:, lo:lo + v.w].astype(F32))
            ct = []
            for v, (_, io), r in zip(cots, cwins, cot_refs):
                lo = io + h * v.hs
                ct.append(r[:, lo:lo + v.w].astype(F32))

            def closed(*d):
                rr, cc = list(rv), list(cv)
                for k, val in zip(drows, d[:len(drows)]):
                    rr[k] = val
                for k, val in zip(dconsts, d[len(drows):]):
                    cc[k] = val
                return tuple(fn(*rr, *cc))

            _, vjp = jax.vjp(closed, *[rv[k] for k in drows], *[cv[k] for k in dconsts])
            grads = vjp(tuple(ct))
            for n, k in enumerate(drows):
                g = grads[n]
                if rows[k].hs == 0 and heads > 1:
                    shared[n] = g if shared[n] is None else shared[n] + g
                else:
                    if n == 0 and has_res:
                        g = g + res_ref[:, h * rows[k].w:(h + 1) * rows[k].w].astype(F32)
                    out, off, hs = place[n]
                    grow_refs[out][:, off + h * hs:off + h * hs + rows[k].w] = g.astype(row_dtype)
            for (v, off, hs), (_, io), r in zip(fills, fwins, fill_refs):
                lo = io + h * v.hs
                grow_refs[0][:, off + h * hs:off + h * hs + v.w] = r[:, lo:lo + v.w].astype(row_dtype)
            for n in range(len(dconsts)):
                g = grads[len(drows) + n]
                gc_sum[n] = g if gc_sum[n] is None else gc_sum[n] + g
        for n, k in enumerate(drows):
            if shared[n] is not None:
                g = shared[n]
                if n == 0 and has_res:
                    g = g + res_ref[...].astype(F32)
                grow_refs[place[n][0]][...] = g.astype(row_dtype)

        @pl.when(i == 0)
        def _():
            for n in range(len(dconsts)):
                gconst_refs[n][...] = gc_sum[n]

        @pl.when(i > 0)
        def _():
            for n in range(len(dconsts)):
                gconst_refs[n][...] += gc_sum[n]

    in_specs = [w[0] for w in wins] + [_const_spec(c) for c in consts] + [w[0] for w in cwins] + [w[0] for w in fwins]
    ops = [v.arr for v in rows] + list(consts) + [v.arr for v in cots] + [v.arr for v, _, _ in fills]
    if has_res:
        in_specs.append(pl.BlockSpec((tr, widths[0]), lambda i: (i, 0)))
        ops.append(res)
    out_specs = [pl.BlockSpec((tr, w), lambda i: (i, 0)) for w in widths]
    out_shape = [jax.ShapeDtypeStruct((S, w), row_dtype) for w in widths]
    for k in dconsts:
        out_specs.append(_const_spec(consts[k]))
        out_shape.append(jax.ShapeDtypeStruct(consts[k].shape, F32))
    return pl.pallas_call(body, name=name, grid=(S // tr,), in_specs=in_specs, out_specs=out_specs,
                          out_shape=out_shape, compiler_params=_cparams(("arbitrary",)))(*ops)


def _rms(x, g, n=None):
    n = x.shape[-1] if n is None else n
    ms = jnp.sum(x * x, axis=-1, keepdims=True) * (1.0 / n)
    return x * lax.rsqrt(ms + EPS) * g


def rms_fn(x, g):
    return (_rms(x, g),)


def qk_prep_fn(nope, rope, cos, sin, gn, gr, rot):
    ms = (jnp.sum(nope * nope, axis=-1, keepdims=True) + jnp.sum(rope * rope, axis=-1, keepdims=True)) * (1.0 / MLA_QK)
    r = lax.rsqrt(ms + EPS)
    on = nope * r * gn
    orr = rope * r * gr
    turned = jnp.dot(orr, rot, precision=lax.Precision.HIGHEST, preferred_element_type=F32)
    return on, orr * cos + turned * sin


def gla_out_fn(o, zr, g):
    return (_rms(o, g) * (zr * _sigmoid(zr)),)


def gate_fn(pre, b):
    t = pre + b
    return ((jnp.minimum(t, 0.0) - jnp.log(1.0 + jnp.exp(-jnp.abs(t)))) * (1.0 / GLA_TAU),)


def _attn_probs(q_ref, k_ref, scale, q0, kext):
    s = lax.dot_general(q_ref[...].astype(MXU_DTYPE), k_ref[0:kext, :].astype(MXU_DTYPE), _DIMS["nt"],
                        preferred_element_type=F32) * scale
    if q0 is not None:
        qc = (q0 + lax.broadcasted_iota(jnp.int32, s.shape, 0)) // CHUNK
        kc = lax.broadcasted_iota(jnp.int32, s.shape, 1) // CHUNK
        s = jnp.where(kc <= qc, s, -1e30)
    m = jnp.max(s, axis=-1, keepdims=True)
    e = jnp.exp(s - m)
    return e / jnp.sum(e, axis=-1, keepdims=True)


def _per_query_block(one, causal, nq, tq, Sk):
    if not causal:
        one(None, Sk, None)
        return
    assert tq % CHUNK == 0
    for ib in range(nq):
        pl.when(pl.program_id(1) == ib)(functools.partial(one, ib * tq, min(Sk, (ib + 1) * tq), ib))


def attn_fwd(q, k, v, *, heads, dk, dv, v_off, v_hs, scale, causal, name, tq=256, comm=None):
    Sq, Sk = q.shape[0], k.shape[0]
    tq = _tile(Sq, tq, 8)

    def body(q_ref, k_ref, v_ref, o_ref):
        def one(q0, kext, ib):
            p = _attn_probs(q_ref, k_ref, scale, q0, kext)
            o_ref[...] = jnp.dot(p.astype(MXU_DTYPE), v_ref[0:kext, :].astype(MXU_DTYPE),
                                 preferred_element_type=F32).astype(o_ref.dtype)

        _per_query_block(one, causal, Sq // tq, tq, Sk)

    return _pcall(
        body, [q, k, v], name=name, grid=(heads, Sq // tq),
        in_specs=[pl.BlockSpec((tq, dk), lambda h, i: (i, h)), pl.BlockSpec((Sk, dk), lambda h, i: (0, h)),
                  pl.BlockSpec((Sk, dv), lambda h, i: (0, v_off + h * v_hs))],
        out_specs=pl.BlockSpec((tq, dv), lambda h, i: (i, h)),
        out_shape=jax.ShapeDtypeStruct((Sq, heads * dv), BF16), sem=("parallel", "parallel"), comm=comm)


def attn_bwd(q, k, v, do, *, heads, dk, dv, v_off, v_hs, scale, causal, name, tq=256, comm=None):
    Sq, Sk = q.shape[0], k.shape[0]
    tq = _tile(Sq, tq, 8)

    def body(q_ref, k_ref, v_ref, do_ref, dq_ref, dk_ref, dv_ref):
        @pl.when(pl.program_id(1) == 0)
        def _():
            dk_ref[...] = jnp.zeros_like(dk_ref)
            dv_ref[...] = jnp.zeros_like(dv_ref)

        def one(q0, kext, ib):
            p = _attn_probs(q_ref, k_ref, scale, q0, kext)
            dob = do_ref[...].astype(MXU_DTYPE)
            dp = lax.dot_general(dob, v_ref[0:kext, :].astype(MXU_DTYPE), _DIMS["nt"], preferred_element_type=F32)
            delta = jnp.sum(p * dp, axis=-1, keepdims=True)
            ds = (p * (dp - delta) * scale).astype(MXU_DTYPE)
            dq_ref[...] = jnp.dot(ds, k_ref[0:kext, :].astype(MXU_DTYPE), preferred_element_type=F32)
            dk_ref[0:kext, :] += lax.dot_general(ds, q_ref[...].astype(MXU_DTYPE), _DIMS["tn"],
                                                 preferred_element_type=F32)
            dv_ref[0:kext, :] += lax.dot_general(p.astype(MXU_DTYPE), dob, _DIMS["tn"], preferred_element_type=F32)

        _per_query_block(one, causal, Sq // tq, tq, Sk)

    return _pcall(
        body, [q, k, v, do], name=name, grid=(heads, Sq // tq),
        in_specs=[pl.BlockSpec((tq, dk), lambda h, i: (i, h)), pl.BlockSpec((Sk, dk), lambda h, i: (0, h)),
                  pl.BlockSpec((Sk, dv), lambda h, i: (0, v_off + h * v_hs)),
                  pl.BlockSpec((tq, dv), lambda h, i: (i, h))],
        out_specs=[pl.BlockSpec((tq, dk), lambda h, i: (i, h)), pl.BlockSpec((Sk, dk), lambda h, i: (0, h)),
                   pl.BlockSpec((Sk, dv), lambda h, i: (0, h))],
        out_shape=[jax.ShapeDtypeStruct((Sq, heads * dk), F32), jax.ShapeDtypeStruct((Sk, heads * dk), F32),
                   jax.ShapeDtypeStruct((Sk, heads * dv), F32)],
        sem=("parallel", "arbitrary"), comm=comm)


def _gla_chunk(k, g, tri_ref):
    b = jnp.dot(tri_ref[...], g, precision=lax.Precision.HIGHEST, preferred_element_type=F32)
    b_end = jnp.sum(g, axis=0, keepdims=True)
    e = jnp.exp(b_end - b)
    return k * e, e, jnp.exp(b_end)


def _gla_windows(z, q_off, k_off, v_off, rows_of):
    H, DK, DV = GLA_HEADS, GLA_DK, GLA_DV
    specs, inner = [], []
    for off, ext in ((q_off, H * DK), (k_off, H * DK), (v_off, H * DV)):
        ww, blk, io = _window(z.shape[1], off, ext)
        specs.append(pl.BlockSpec((CHUNK, ww), lambda c, blk=blk: (rows_of(c), blk)))
        inner.append(io)
    return specs, inner


def gla_fwd(z, la, tri, *, q_off, k_off, v_off, name, comm=None):
    S = z.shape[0]
    nchunk = S // CHUNK
    H, DK, DV = GLA_HEADS, GLA_DK, GLA_DV
    qscale = DK ** -0.5
    zspecs, (qi, ki, vi) = _gla_windows(z, q_off, k_off, v_off, lambda c: c)

    def body(q_ref, k_ref, v_ref, la_ref, tri_ref, o_ref, st_ref, state):
        @pl.when(pl.program_id(0) == 0)
        def _():
            state[...] = jnp.zeros_like(state)

        for h in range(H):
            dks, dvs = slice(h * DK, (h + 1) * DK), slice(h * DV, (h + 1) * DV)
            k = k_ref[:, ki + h * DK:ki + (h + 1) * DK].astype(F32)
            v = v_ref[:, vi + h * DV:vi + (h + 1) * DV]
            q = q_ref[:, qi + h * DK:qi + (h + 1) * DK].astype(F32)
            kdec, _, decay = _gla_chunk(k, la_ref[:, dks].astype(F32), tri_ref)
            ut = lax.dot_general(v.astype(MXU_DTYPE), kdec.astype(MXU_DTYPE), _DIMS["tn"], preferred_element_type=F32)
            new = state[h] * decay + ut
            state[h] = new
            st_ref[h] = new
            qs = (q * qscale).astype(MXU_DTYPE)
            o_ref[:, dvs] = lax.dot_general(qs, new.astype(MXU_DTYPE), _DIMS["nt"], preferred_element_type=F32)

    return _pcall(
        body, [z, z, z, la, tri], name=name, grid=(nchunk,),
        in_specs=zspecs + [pl.BlockSpec((CHUNK, H * DK), lambda c: (c, 0)), pl.BlockSpec((CHUNK, CHUNK), lambda c: (0, 0))],
        out_specs=[pl.BlockSpec((CHUNK, H * DV), lambda c: (c, 0)),
                   pl.BlockSpec((H, None, DV, DK), lambda c: (0, c, 0, 0))],
        out_shape=[jax.ShapeDtypeStruct((S, H * DV), F32), jax.ShapeDtypeStruct((H, nchunk, DV, DK), F32)],
        scratch_shapes=[pltpu.VMEM((H, DV, DK), F32)], sem=("arbitrary",), comm=comm)


def gla_bwd(z, la, tri, trit, states, do, *, q_off, k_off, v_off, name, comm=None):
    S = z.shape[0]
    nchunk = S // CHUNK
    H, DK, DV = GLA_HEADS, GLA_DK, GLA_DV
    qscale = DK ** -0.5
    last = nchunk - 1
    zspecs, (qi, ki, vi) = _gla_windows(z, q_off, k_off, v_off, lambda c: last - c)

    def body(q_ref, k_ref, v_ref, la_ref, tri_ref, trit_ref, st_ref, sp_ref, do_ref, dq_ref, dk_ref, dv_ref, dla_ref,
             dstate):
        c = pl.program_id(0)
        cc = last - c

        @pl.when(c == 0)
        def _():
            dstate[...] = jnp.zeros_like(dstate)

        for h in range(H):
            dks, dvs = slice(h * DK, (h + 1) * DK), slice(h * DV, (h + 1) * DV)
            kf = k_ref[:, ki + h * DK:ki + (h + 1) * DK].astype(F32)
            vb16 = v_ref[:, vi + h * DV:vi + (h + 1) * DV].astype(MXU_DTYPE)
            q = q_ref[:, qi + h * DK:qi + (h + 1) * DK].astype(F32)
            kdec, e, decay = _gla_chunk(kf, la_ref[:, dks].astype(F32), tri_ref)
            dob = do_ref[:, dvs].astype(MXU_DTYPE)
            stb = st_ref[h].astype(MXU_DTYPE)
            qs = (q * qscale).astype(MXU_DTYPE)
            dq_ref[:, dks] = jnp.dot(dob, stb, preferred_element_type=F32) * qscale
            dst = dstate[h] + lax.dot_general(dob, qs, _DIMS["tn"], preferred_element_type=F32)
            prev = jnp.where(cc > 0, sp_ref[h], 0.0)
            ddecay = jnp.sum(dst * prev, axis=0, keepdims=True)
            dstate[h] = dst * decay
            dub = dst.astype(MXU_DTYPE)
            dv_ref[:, dvs] = lax.dot_general(kdec.astype(MXU_DTYPE), dub, _DIMS["nt"], preferred_element_type=F32)
            dkdec = jnp.dot(vb16, dub, preferred_element_type=F32)
            dk_ref[:, dks] = dkdec * e
            w = dkdec * kf * e
            db_end = jnp.sum(w, axis=0, keepdims=True) + ddecay * decay
            dla_ref[:, dks] = db_end - jnp.dot(trit_ref[...], w, precision=lax.Precision.HIGHEST,
                                               preferred_element_type=F32)

    def rows(width):
        return pl.BlockSpec((CHUNK, width), lambda c: (last - c, 0))

    square = pl.BlockSpec((CHUNK, CHUNK), lambda c: (0, 0))
    return _pcall(
        body, [z, z, z, la, tri, trit, states, states, do], name=name, grid=(nchunk,),
        in_specs=zspecs + [rows(H * DK), square, square,
                           pl.BlockSpec((H, None, DV, DK), lambda c: (0, last - c, 0, 0)),
                           pl.BlockSpec((H, None, DV, DK), lambda c: (0, jnp.maximum(last - c - 1, 0), 0, 0)),
                           rows(H * DV)],
        out_specs=[rows(H * DK), rows(H * DK), rows(H * DV), rows(H * DK)],
        out_shape=[jax.ShapeDtypeStruct((S, H * DK), F32), jax.ShapeDtypeStruct((S, H * DK), F32),
                   jax.ShapeDtypeStruct((S, H * DV), F32), jax.ShapeDtypeStruct((S, H * DK), F32)],
        scratch_shapes=[pltpu.VMEM((H, DV, DK), F32)], sem=("arbitrary",), comm=comm)


def loss_head(y, target, *, name, tr=256):
    S, D = y.shape
    tr = _tile(S, tr, 8)

    def body(y_ref, t_ref, dy_ref, loss_ref):
        i = pl.program_id(0)
        err = y_ref[...] - t_ref[...]
        dy_ref[...] = err * (1.0 / D)
        part = jnp.zeros((1, LANE), F32) + 0.5 * jnp.sum(jnp.sum(err * err, axis=-1, keepdims=True) * (1.0 / D))

        @pl.when(i == 0)
        def _():
            loss_ref[...] = part

        @pl.when(i > 0)
        def _():
            loss_ref[...] += part

    spec = pl.BlockSpec((tr, D), lambda i: (i, 0))
    return pl.pallas_call(
        body, name=name, grid=(S // tr,), in_specs=[spec, spec],
        out_specs=[spec, pl.BlockSpec((1, LANE), lambda i: (0, 0))],
        out_shape=[jax.ShapeDtypeStruct((S, D), F32), jax.ShapeDtypeStruct((1, LANE), F32)],
        compiler_params=_cparams(("arbitrary",)))(y, target)


def _core_index():
    return lax.axis_index("c").astype(jnp.int32).reshape(1)


def _chip_slots():
    x, y, c = lax.axis_index("x"), lax.axis_index("y"), lax.axis_index("c")
    return jnp.stack([2 * x + y, 2 * (1 - x) + y, 2 * x + (1 - y), 2 * (1 - x) + (1 - y), c]).astype(jnp.int32)


def sum_chip_parts(own, parts, *, name, tr=1024):
    _, R, C = own.shape
    tr = _tile(R, tr, 8)

    def body(idx_ref, o_ref, p0_ref, p1_ref, p2_ref, out_ref):
        acc = o_ref[...].astype(F32) + p0_ref[...].astype(F32)
        acc = acc + p1_ref[...].astype(F32)
        out_ref[...] = acc + p2_ref[...].astype(F32)

    def slot(k):
        return pl.BlockSpec((None, tr, C), lambda i, idx: (idx[k], i, 0))

    grid_spec = pltpu.PrefetchScalarGridSpec(num_scalar_prefetch=1, grid=(R // tr,),
                                             in_specs=[slot(0), slot(1), slot(2), slot(3)], out_specs=slot(4))
    return pl.pallas_call(body, name=name, grid_spec=grid_spec, out_shape=jax.ShapeDtypeStruct((2, R, C), F32),
                          compiler_params=_cparams(("parallel",)))(_chip_slots(), own, parts, parts, parts)


def add_own_half(g, got, out_dtype, *, name, tr=1024):
    n, _, R, C = g.shape
    tr = _tile(R, tr, 8)

    def body(c_ref, a_ref, b_ref, o_ref):
        o_ref[...] = (a_ref[...].astype(F32) + b_ref[...].astype(F32)).astype(out_dtype)

    spec = pl.BlockSpec((None, tr, C), lambda s, i, c: (s, i, 0))
    grid_spec = pltpu.PrefetchScalarGridSpec(
        num_scalar_prefetch=1, grid=(n, R // tr),
        in_specs=[pl.BlockSpec((None, None, tr, C), lambda s, i, c: (s, c[0], i, 0)), spec], out_specs=spec)
    return pl.pallas_call(body, name=name, grid_spec=grid_spec, out_shape=jax.ShapeDtypeStruct((n, R, C), out_dtype),
                          compiler_params=_cparams(("parallel", "parallel")))(_core_index(), g, got)


def adamw(items, *, name, max_steps=16, behind=None):
    c1 = 1.0 / (1.0 - ADAM_B1 ** ADAM_STEP)
    c2 = 1.0 / (1.0 - ADAM_B2 ** ADAM_STEP)
    n = len(items)
    steps = max_steps
    while steps > 1 and any(it[0].shape[0] % (8 * steps) for it in items):
        steps //= 2
    tail = [] if behind is None else [behind]

    def body(*refs):
        for a in range(n):
            w_ref, g_ref, m_ref, v_ref = refs[4 * a:4 * a + 4]
            go_ref, d_ref, nm_ref, nv_ref = refs[4 * n + len(tail) + 4 * a:4 * n + len(tail) + 4 * a + 4]
            gv = g_ref[...]
            go_ref[...] = gv
            nm = ADAM_B1 * m_ref[...] + (1.0 - ADAM_B1) * gv
            nv = ADAM_B2 * v_ref[...] + (1.0 - ADAM_B2) * (gv * gv)
            nm_ref[...] = nm
            nv_ref[...] = nv
            d_ref[...] = -ADAM_LR * ((nm * c1) / (jnp.sqrt(nv * c2) + ADAM_EPS) + ADAM_WD * w_ref[...])

    ops, in_specs, out_specs, out_shape = [], [], [], []
    for w, g, m, v in items:
        R, C = w.shape
        spec = pl.BlockSpec((R // steps, C), lambda i: (i, 0))
        ops += [w, g, m, v]
        in_specs += [spec] * 4
        out_specs += [spec] * 4
        out_shape += [jax.ShapeDtypeStruct((R, C), F32)] * 4
    flat = _pcall(body, ops + tail, name=name, grid=(steps,), in_specs=in_specs + [ANY] * len(tail), out_specs=out_specs,
                  out_shape=out_shape, sem=("parallel",))
    return [tuple(flat[4 * a:4 * a + 4]) for a in range(n)]


def _place():
    x, y, c = lax.axis_index("x"), lax.axis_index("y"), lax.axis_index("c")
    chips = [(1 - x, y), (x, 1 - y), (1 - x, 1 - y)]
    return x, y, c, chips


def _rcopy(src, dst, send, recv, j, to):
    return pltpu.make_async_remote_copy(src_ref=src, dst_ref=dst, send_sem=send.at[j], recv_sem=recv.at[j], device_id=to,
                                        device_id_type=MESH)


def gather_stage1(shards, split):
    n = len(shards)
    ins = [s.reshape(2, s.shape[0] // 2, s.shape[1]) if sp else s for s, sp in zip(shards, split)]
    outs = [jax.ShapeDtypeStruct((N_CHIPS,) + a.shape, a.dtype) for a in ins]

    def start(in_refs, out_refs, send, recv, base):
        x, y, c, chips = _place()
        mine = 2 * x + y
        for i in range(n):
            src = in_refs[i].at[c] if split[i] else in_refs[i]
            dst = out_refs[i].at[mine, c] if split[i] else out_refs[i].at[mine]
            for k, (px, py) in enumerate(chips):
                _rcopy(src, dst, send, recv, base + 3 * i + k, (px, py, c)).start()

    def wait(in_refs, out_refs, send, recv, base):
        x, y, c, chips = _place()
        for i in range(n):
            src = in_refs[i].at[c] if split[i] else in_refs[i]
            for k, (px, py) in enumerate(chips):
                dst = out_refs[i].at[2 * px + py, c] if split[i] else out_refs[i].at[2 * px + py]
                _rcopy(src, dst, send, recv, base + 3 * i + k, (px, py, c)).wait()

    return Comm(ins, outs, 3 * n, start, wait)


def gather_stage2(slots, shards, split):
    n = len(slots)
    own = [s.reshape(2, s.shape[0] // 2, s.shape[1]) if sp else s for s, sp in zip(shards, split)]

    def copies(in_refs, out_refs, send, recv, base):
        x, y, c, chips = _place()
        sib = (x, y, 1 - c)
        for i in range(n):
            j = base + 4 * i
            mine = out_refs[i].at[2 * x + y]
            yield _rcopy(in_refs[n + i], mine, send, recv, j + 3, sib), _rcopy(in_refs[n + i], mine, send, recv, j + 3, sib)
            if split[i]:
                for k, (px, py) in enumerate(chips):
                    s = 2 * px + py
                    yield (_rcopy(in_refs[i].at[s, c], out_refs[i].at[s, c], send, recv, j + k, sib),
                           _rcopy(in_refs[i].at[s, c], out_refs[i].at[s, 1 - c], send, recv, j + k, sib))

    def start(*a):
        for out, _ in copies(*a):
            out.start()

    def wait(*a):
        for _, back in copies(*a):
            back.wait()

    return Comm(list(slots) + own, [jax.ShapeDtypeStruct(s.shape, s.dtype) for s in slots], 4 * n, start, wait,
                {i: i for i in range(n)})


def swap_halves(gs):
    n = len(gs)

    def copies(in_refs, out_refs, send, recv, base):
        x, y, c, _ = _place()
        return [_rcopy(in_refs[i].at[s, 1 - c], out_refs[i].at[s], send, recv, base + N_CHIPS * i + s, (x, y, 1 - c))
                for i in range(n) for s in range(N_CHIPS)]

    def start(*a):
        for cp in copies(*a):
            cp.start()

    def wait(*a):
        for cp in copies(*a):
            cp.wait()

    return Comm(gs, [jax.ShapeDtypeStruct((N_CHIPS,) + g.shape[2:], g.dtype) for g in gs], N_CHIPS * n, start, wait)


def exchange_chips(ps):
    n = len(ps)

    def start(in_refs, out_refs, send, recv, base):
        x, y, c, chips = _place()
        for i in range(n):
            for k, (px, py) in enumerate(chips):
                _rcopy(in_refs[i].at[2 * px + py], out_refs[i].at[2 * x + y], send, recv, base + 3 * i + k,
                       (px, py, c)).start()

    def wait(in_refs, out_refs, send, recv, base):
        x, y, c, chips = _place()
        for i in range(n):
            for k, (px, py) in enumerate(chips):
                _rcopy(in_refs[i].at[2 * px + py], out_refs[i].at[2 * px + py], send, recv, base + 3 * i + k,
                       (px, py, c)).wait()

    return Comm(ps, [jax.ShapeDtypeStruct(p.shape, p.dtype) for p in ps], 3 * n, start, wait)


def join_halves(fs):
    n = len(fs)

    def start(in_refs, out_refs, send, recv, base):
        x, y, c, _ = _place()
        for i in range(n):
            _rcopy(in_refs[i].at[c], out_refs[i].at[c], send, recv, base + i, (x, y, 1 - c)).start()

    def wait(in_refs, out_refs, send, recv, base):
        x, y, c, _ = _place()
        for i in range(n):
            _rcopy(in_refs[i].at[c], out_refs[i].at[1 - c], send, recv, base + i, (x, y, 1 - c)).wait()

    return Comm(fs, [jax.ShapeDtypeStruct(f.shape, f.dtype) for f in fs], n, start, wait, {i: i for i in range(n)})


def allreduce_small(v, *, name):
    m_per, n = v.shape

    def body(x_ref, sum_ref, all_ref, send_sems, recv_sems, local_sem):
        x, y, c, chips = _place()
        me, sibling = (x, y, c), (x, y, 1 - c)

        def rows(px, py, pc):
            return all_ref.at[pl.ds((4 * px + 2 * py + pc) * m_per, m_per), :]

        def copy(k, block, to, src=None):
            return pltpu.make_async_remote_copy(src_ref=rows(*block) if src is None else src, dst_ref=rows(*block),
                                                send_sem=send_sems.at[k], recv_sem=recv_sems.at[k], device_id=to,
                                                device_id_type=MESH)

        mine = pltpu.make_async_copy(x_ref, rows(*me), local_sem)
        mine.start()
        first = [copy(0, me, sibling, src=x_ref)]
        first += [copy(1 + j, me, (*chip, c), src=x_ref) for j, chip in enumerate(chips)]
        for cp in first:
            cp.start()
        passed = [copy(4 + j, (*chip, c), sibling) for j, chip in enumerate(chips)]
        for j, chip in enumerate(chips):
            copy(1 + j, (*chip, c), me).wait_recv()
            passed[j].start()
        copy(0, sibling, me).wait_recv()
        for j, chip in enumerate(chips):
            copy(4 + j, (*chip, 1 - c), me).wait_recv()
        for cp in first + passed:
            cp.wait_send()
        mine.wait()
        acc = all_ref[0:m_per, :]
        for d in range(1, N_DEV):
            acc = acc + all_ref[d * m_per:(d + 1) * m_per, :]
        sum_ref[...] = acc

    vm = pl.BlockSpec(memory_space=pltpu.VMEM)
    return pl.pallas_call(
        body, name=name, in_specs=[vm], out_specs=vm, out_shape=jax.ShapeDtypeStruct((m_per, n), F32),
        scratch_shapes=[pltpu.VMEM((N_DEV * m_per, n), F32), pltpu.SemaphoreType.DMA((7,)),
                        pltpu.SemaphoreType.DMA((7,)), pltpu.SemaphoreType.DMA],
    )(v)


def _cols_to_slots(w):
    r, c4 = w.shape
    return w.reshape(r, N_CHIPS, c4 // N_CHIPS).transpose(1, 0, 2)


def _slots_to_cols(w):
    n, r, c = w.shape
    return w.transpose(1, 0, 2).reshape(r, n * c)


def _pad_cols(a, width):
    return jnp.pad(a, ((0, 0), (0, width - a.shape[1])))


class InLayout:
    def __init__(self, q_rank, kv_rank):
        gk = GLA_HEADS * GLA_DK
        gv = GLA_HEADS * GLA_DV
        sizes = [q_rank, kv_rank, MLA_ROPE, gk, gk, gv, GLA_GATE_RANK, gv]
        names = ["zq", "zkv", "zkr", "gq", "gk", "gv", "zg", "zr"]
        starts = np.concatenate([[0], np.cumsum(sizes)[:-1]])
        self.ref = {n: (int(s), int(z)) for n, s, z in zip(names, starts, sizes)}
        self.ref_width = int(sum(sizes))
        self.order = ["gv", "zr", "zq", "gq", "gk", "zkv", "zkr", "zg"]
        self.off, self.size = {}, {}
        pos = 0
        for n in self.order:
            padded = -(-self.ref[n][1] // LANE) * LANE
            self.off[n], self.size[n] = pos, padded
            pos += padded
        self.width = pos
        self.shard = self.ref_width // N_CHIPS
        self.shard_pad = -(-self.shard // LANE) * LANE

    def _pieces(self, lo, hi):
        out = []
        while lo < hi:
            s = lo // self.shard
            end = min(hi, (s + 1) * self.shard)
            out.append((s * self.shard_pad + lo - s * self.shard, s * self.shard_pad + end - s * self.shard))
            lo = end
        return out

    def from_shards(self, zs):
        cols = []
        for n in self.order:
            start, size = self.ref[n]
            cols += [zs[:, a:b] for a, b in self._pieces(start, start + size)]
            if self.size[n] > size:
                cols.append(jnp.zeros((zs.shape[0], self.size[n] - size), zs.dtype))
        return jnp.concatenate(cols, axis=1)

    def to_shards(self, dz):
        names = sorted(self.ref, key=lambda n: self.ref[n][0])
        ref = jnp.concatenate([dz[:, self.off[n]:self.off[n] + self.ref[n][1]] for n in names], axis=1)
        ref = ref.reshape(dz.shape[0], N_CHIPS, self.shard)
        return jnp.pad(ref, ((0, 0), (0, 0), (0, self.shard_pad - self.shard))).reshape(dz.shape[0], -1)


def _pad_q_up(w):
    r = w.shape[0]
    w = w.reshape(r, MLA_HEADS, MLA_QK)
    w = jnp.pad(w, ((0, 0), (0, 0), (0, MLA_HEAD_PAD - MLA_QK)))
    return w.reshape(r, MLA_HEADS * MLA_HEAD_PAD)


def _unpad_q_up(g):
    r = g.shape[0]
    return g.reshape(r, MLA_HEADS, MLA_HEAD_PAD)[:, :, :MLA_QK].reshape(r, MLA_HEADS * MLA_QK)


def _rope_tables(positions):
    half = MLA_ROPE // 2
    inv_freq = ROPE_THETA ** (-jnp.arange(half, dtype=F32) / half)
    ang = positions.astype(F32).reshape(-1, 1) * inv_freq
    cos, sin = jnp.cos(ang), jnp.sin(ang)
    s = ang.shape[0]
    cosf = jnp.concatenate([cos, cos, jnp.ones((s, LANE - MLA_ROPE), F32)], axis=1)
    sinf = jnp.concatenate([sin, sin, jnp.zeros((s, LANE - MLA_ROPE), F32)], axis=1)
    rot = np.zeros((LANE, LANE), np.float32)
    for j in range(half):
        rot[j + half, j] = -1.0
        rot[j, j + half] = 1.0
    return cosf, sinf, jnp.asarray(rot)


SMALL = ["ffn1_norm", "mix_norm", "q_a_norm", "kv_a_norm", "mla_q_norm", "mla_k_norm", "gla_b_gate", "gla_out_norm",
         "mem_attn_norm", "mem_norm", "mem_q_norm", "mem_k_norm", "ffn2_norm"]
BIG = ["ffn1_w_gate", "ffn1_w_up", "ffn1_w_down", "w_in", "w_q_up", "w_kv_up", "w_out", "mem_w_q", "mem_w_k",
       "mem_w_v", "mem_w_o", "ffn2_w_gate", "ffn2_w_up", "ffn2_w_down"]
COL_SHARDED = {"ffn1_w_gate", "ffn1_w_up", "w_in", "w_q_up", "w_kv_up", "gla_w_gate2", "mem_w_o", "ffn2_w_gate", "ffn2_w_up"}
WEIGHTS = ["ffn1_norm", "ffn1_w_gate", "ffn1_w_up", "ffn1_w_down", "mix_norm", "w_in", "q_a_norm", "w_q_up", "kv_a_norm",
           "w_kv_up", "mla_q_norm", "mla_k_norm", "gla_w_gate2", "gla_b_gate", "gla_out_norm", "w_out", "mem_attn_norm",
           "mem_norm", "mem_w_q", "mem_w_k", "mem_w_v", "mem_w_o", "mem_q_norm", "mem_k_norm", "ffn2_norm", "ffn2_w_gate",
           "ffn2_w_up", "ffn2_w_down"]


def _pack_small(vals, rows=8):
    flat = jnp.concatenate([v.reshape(-1).astype(F32) for v in vals])
    n = flat.shape[0]
    per = -(-n // (rows * LANE)) * LANE
    return jnp.pad(flat, (0, rows * per - n)).reshape(rows, per)


def _unpack_small(packed, shapes):
    flat = packed.reshape(-1)
    out, pos = [], 0
    for s in shapes:
        n = int(np.prod(s))
        out.append(flat[pos:pos + n].reshape(s))
        pos += n
    return out


FFN1 = ["ffn1_w_gate", "ffn1_w_up", "ffn1_w_down"]
FFN2 = ["ffn2_w_gate", "ffn2_w_up", "ffn2_w_down"]
SLOT_WEIGHTS = {"ffn1_w_gate", "ffn1_w_up", "ffn2_w_gate", "ffn2_w_up", "w_in"}
MID_A = ["w_in", "w_q_up", "w_kv_up", "gla_w_gate2"]
MID_B = ["w_out", "mem_w_q", "mem_w_k", "mem_w_v", "mem_w_o"]


def _with(res, comm):
    return res if comm is not None else (res, None)


def kernel(x, mem, positions, ffn1_norm, ffn1_w_gate, ffn1_w_up, ffn1_w_down, mix_norm, w_in, q_a_norm, w_q_up, kv_a_norm, w_kv_up, mla_q_norm, mla_k_norm, gla_w_gate2, gla_b_gate, gla_out_norm, w_out, mem_attn_norm, mem_norm, mem_w_q, mem_w_k, mem_w_v, mem_w_o, mem_q_norm, mem_k_norm, ffn2_norm, ffn2_w_gate, ffn2_w_up, ffn2_w_down, loss_target, m_ffn1_norm, m_ffn1_w_gate, m_ffn1_w_up, m_ffn1_w_down, m_mix_norm, m_w_in, m_q_a_norm, m_w_q_up, m_kv_a_norm, m_w_kv_up, m_mla_q_norm, m_mla_k_norm, m_gla_w_gate2, m_gla_b_gate, m_gla_out_norm, m_w_out, m_mem_attn_norm, m_mem_norm, m_mem_w_q, m_mem_w_k, m_mem_w_v, m_mem_w_o, m_mem_q_norm, m_mem_k_norm, m_ffn2_norm, m_ffn2_w_gate, m_ffn2_w_up, m_ffn2_w_down, v_ffn1_norm, v_ffn1_w_gate, v_ffn1_w_up, v_ffn1_w_down, v_mix_norm, v_w_in, v_q_a_norm, v_w_q_up, v_kv_a_norm, v_w_kv_up, v_mla_q_norm, v_mla_k_norm, v_gla_w_gate2, v_gla_b_gate, v_gla_out_norm, v_w_out, v_mem_attn_norm, v_mem_norm, v_mem_w_q, v_mem_w_k, v_mem_w_v, v_mem_w_o, v_mem_q_norm, v_mem_k_norm, v_ffn2_norm, v_ffn2_w_gate, v_ffn2_w_up, v_ffn2_w_down):
    args = dict(locals())
    two_d = lambda a: a[0] if a.ndim == 3 else a
    W = {n: two_d(args[n]) for n in WEIGHTS}
    M1 = {n: two_d(args["m_" + n]) for n in WEIGHTS}
    V2 = {n: two_d(args["v_" + n]) for n in WEIGHTS}
    xs, mems, tgt = x[0], mem[0], loss_target[0]
    S, D = xs.shape
    chip = 2 * lax.axis_index("x") + lax.axis_index("y")

    q_rank, kv_rank = W["w_q_up"].shape[0], W["w_kv_up"].shape[0]
    lay = InLayout(q_rank, kv_rank)
    off = lay.off
    up_names, down_names = FFN1[:2], FFN1[2:]
    shard16 = {n: W[n].astype(BF16) for n in up_names}
    full = {}

    def stage1(names):
        return gather_stage1([shard16[n] for n in names], [n != "gla_w_gate2" for n in names])

    def stage2(names, slots):
        return gather_stage2(slots, [shard16[n] for n in names], [n != "gla_w_gate2" for n in names])

    def finish(names, slots):
        for n, s in zip(names, slots):
            s = s.reshape((N_CHIPS,) + shard16[n].shape)
            if n in SLOT_WEIGHTS:
                full[n] = s
            else:
                full[n] = _slots_to_cols(s) if n in COL_SHARDED else s.reshape(-1, s.shape[2])

    token = {"last": None}

    def begin(comm, name, after=None):
        started = start_comm(comm, name=name, after=token["last"] if after is None else after)
        token["last"] = started[-1]
        return comm, started

    first = [begin(stage1([n]), f"gather_start_{n}") for n in up_names]
    zero = token["last"][0, 0]
    later = [n for n in BIG + ["gla_w_gate2"] if n not in shard16]
    for n in later:
        shard16[n] = (W[n] + zero).astype(BF16)
    shard16["w_in"] = _pad_cols(shard16["w_in"], lay.shard_pad)
    n1 = row_fwd(rms_fn, [V(xs)], [W["ffn1_norm"]], [(D, BF16)], [(0, 0, 0)], name="ffn1_norm")[0]
    cosf, sinf, rot = _rope_tables(positions[0])
    gate_s1 = wait_comm(*first[0], [n1, cosf, sinf] + [shard16[n] for n in later], name=f"gather_wait_{up_names[0]}")
    finish(up_names[:1], run_comm(stage2(up_names[:1], gate_s1), name="pass_ffn1_gate"))
    up_s1 = wait_comm(*first[1], full[up_names[0]], name=f"gather_wait_{up_names[1]}")
    down1 = begin(stage1(down_names), "gather_start_ffn1_down", after=up_s1[0])
    mid_a_s1 = begin(stage1(MID_A), "gather_start_mid_a")
    finish(up_names[1:], run_comm(stage2(up_names[1:], up_s1), name="pass_ffn1_up"))
    tri = jnp.asarray(np.tril(np.ones((CHUNK, CHUNK), np.float32)))
    gqn = W["mla_q_norm"][:, :MLA_NOPE]
    gqr = _pad_cols(W["mla_q_norm"][:, MLA_NOPE:], LANE)
    gkn = W["mla_k_norm"][:, :MLA_NOPE]
    gkr = _pad_cols(W["mla_k_norm"][:, MLA_NOPE:], LANE)
    HP = MLA_HEAD_PAD
    mla_scale = MLA_QK ** -0.5
    mem_scale = MEM_HEAD_DIM ** -0.5
    mla_w = MLA_HEADS * MLA_V
    gla_w = GLA_HEADS * GLA_DV
    mem_w = MEM_HEADS * MEM_HEAD_DIM

    gate1, up1, act1 = ffn_up(n1, full["ffn1_w_gate"], full["ffn1_w_up"], name="ffn1_up", behind=token["last"])
    finish(down_names, run_comm(stage2(down_names, wait_comm(*down1, act1, name="gather_wait_ffn1_down")),
                                name="pass_ffn1_down"))
    mid_a1 = wait_comm(*mid_a_s1, act1, name="gather_wait_mid_a")
    mid_b = begin(stage1(MID_B), "gather_start_mid_b", after=mid_a1[0])
    x1, got = mm([(act1, full["ffn1_w_down"])], "nn", F32, alpha=0.5, res=xs, name="ffn1_down",
                 comm=stage2(MID_A, mid_a1), behind=token["last"])
    ffn1_saved = (n1, gate1, up1, act1)
    finish(MID_A, got)
    ffn2_s1 = [begin(stage1([n]), f"gather_start_{n}", after=x1 if n == FFN2[0] else None) for n in FFN2]
    w_q_up_p = _pad_q_up(full["w_q_up"])
    w_gate2_p = jnp.pad(full["gla_w_gate2"], ((0, LANE - GLA_GATE_RANK), (0, 0)))
    h = row_fwd(rms_fn, [V(x1)], [W["mix_norm"]], [(D, BF16)], [(0, 0, 0)], name="mix_norm")[0]
    mid_b1 = wait_comm(*mid_b, h, name="gather_wait_mid_b")
    z_shards, got = mm([(h, full["w_in"])], "nn", F32, name="in_proj", b_slots=True, comm=stage2(MID_B, mid_b1),
                       behind=token["last"])
    z = lay.from_shards(z_shards)
    finish(MID_B, got)
    qa = row_fwd(rms_fn, [V(z, off["zq"], q_rank)], [W["q_a_norm"]], [(q_rank, BF16)], [(0, 0, 0)], name="q_a_norm")[0]
    kva = row_fwd(rms_fn, [V(z, off["zkv"], kv_rank)], [W["kv_a_norm"]], [(kv_rank, BF16)], [(0, 0, 0)], name="kv_a_norm")[0]
    qraw = mm([(qa, w_q_up_p)], "nn", F32, name="q_up")
    kvraw = mm([(kva, full["w_kv_up"])], "nn", F32, name="kv_up")
    tabs = [V(cosf, diff=False), V(sinf, diff=False)]
    q_rows = [V(qraw, 0, LANE, HP), V(qraw, LANE, LANE, HP)] + tabs
    k_rows = [V(kvraw, 0, LANE, HP), V(z, off["zkr"], LANE, 0)] + tabs
    qh = row_fwd(qk_prep_fn, q_rows, [gqn, gqr, rot], [(MLA_HEADS * HP, BF16)], [(0, 0, HP), (0, LANE, HP)],
                 heads=MLA_HEADS, name="q_prep")[0]
    kh = row_fwd(qk_prep_fn, k_rows, [gkn, gkr, rot], [(MLA_HEADS * HP, BF16)], [(0, 0, HP), (0, LANE, HP)],
                 heads=MLA_HEADS, name="k_prep")[0]
    mla_kw = dict(heads=MLA_HEADS, dk=HP, dv=MLA_V, v_off=1, v_hs=2, scale=mla_scale, causal=True, tq=512)
    o_mla = attn_fwd(qh, kh, kvraw, name="mla_attn", **mla_kw)

    zg = z[:, off["zg"]:off["zg"] + LANE]
    pre = mm([(zg, w_gate2_p)], "nn", F32, name="gla_gate")
    la = row_fwd(gate_fn, [V(pre)], [W["gla_b_gate"]], [(pre.shape[1], F32)], [(0, 0, 0)], name="gla_log_decay")[0]
    gla_kw = dict(q_off=off["gq"], k_off=off["gk"], v_off=off["gv"])
    o_raw, states = gla_fwd(z, la, tri, name="gla_scan", **gla_kw)
    gla_rows = [V(o_raw, 0, GLA_DV, GLA_DV), V(z, off["zr"], GLA_DV, GLA_DV)]
    o_gla = row_fwd(gla_out_fn, gla_rows, [W["gla_out_norm"]], [(gla_w, BF16)], [(0, 0, GLA_DV)], heads=GLA_HEADS,
                    name="gla_out")[0]
    o_cat = jnp.concatenate([o_mla, o_gla], axis=1)
    f2 = [wait_comm(*ffn2_s1[k], o_cat, name=f"gather_wait_{FFN2[k]}")[0] for k in range(2)]
    x2, got = mm([(o_cat, full["w_out"])], "nn", F32, res=x1, name="out_proj", comm=stage2(FFN2[:2], f2))
    finish(FFN2[:2], got)

    hm = row_fwd(rms_fn, [V(x2)], [W["mem_attn_norm"]], [(D, BF16)], [(0, 0, 0)], name="mem_attn_norm")[0]
    mn = row_fwd(rms_fn, [V(mems)], [W["mem_norm"]], [(D, BF16)], [(0, 0, 0)], name="mem_norm")[0]
    qm_raw = mm([(hm, full["mem_w_q"])], "nn", F32, name="mem_q")
    km_raw = mm([(mn, full["mem_w_k"])], "nn", F32, name="mem_k")
    vm = mm([(mn, full["mem_w_v"])], "nn", F32, name="mem_v")
    hd = MEM_HEAD_DIM
    qm = row_fwd(rms_fn, [V(qm_raw, 0, hd, hd)], [W["mem_q_norm"]], [(mem_w, BF16)], [(0, 0, hd)], heads=MEM_HEADS,
                 name="mem_q_norm")[0]
    km = row_fwd(rms_fn, [V(km_raw, 0, hd, hd)], [W["mem_k_norm"]], [(mem_w, BF16)], [(0, 0, hd)], heads=MEM_HEADS,
                 name="mem_k_norm")[0]
    mem_kw = dict(heads=MEM_HEADS, dk=hd, dv=hd, v_off=0, v_hs=1, scale=mem_scale, causal=False, tq=1024)
    om = attn_fwd(qm, km, vm, name="mem_attn", **mem_kw)
    x3 = mm([(om, full["mem_w_o"])], "nn", F32, res=x2, name="mem_o")

    n2 = row_fwd(rms_fn, [V(x3)], [W["ffn2_norm"]], [(D, BF16)], [(0, 0, 0)], name="ffn2_norm")[0]
    f2_down = wait_comm(*ffn2_s1[2], n2, name=f"gather_wait_{FFN2[2]}")
    (gate2, up2, act2), got = ffn_up(n2, full["ffn2_w_gate"], full["ffn2_w_up"], name="ffn2_up",
                                     comm=stage2(FFN2[2:], f2_down))
    finish(FFN2[2:], got)
    y = mm([(act2, full["ffn2_w_down"])], "nn", F32, alpha=0.5, res=x3, name="ffn2_down")
    dy, loss_part = loss_head(y, tgt, name="loss_head")
    G = {"loss": loss_part[:, :1]}

    chip_sum, reduced = {}, {}

    def to_halves(n):
        g = G[n]
        if n in SLOT_WEIGHTS:
            s = g
        else:
            s = _cols_to_slots(g) if n in COL_SHARDED else g.reshape(N_CHIPS, g.shape[0] // N_CHIPS, g.shape[1])
        return s.reshape(N_CHIPS, 2, s.shape[1] // 2, s.shape[2])

    def add2(names, halves, got):
        for n, a, b in zip(names, halves, got):
            chip_sum[n] = add_own_half(a, b, BF16, name=f"rs_add2_{n}")

    to_join = []

    def add4_join(names, parts):
        for n, p in zip(names, parts):
            to_join.append((n, sum_chip_parts(chip_sum[n], p, name=f"rs_add4_{n}")))

    def with_joins(comm):
        names, totals = [n for n, _ in to_join], [t for _, t in to_join]
        to_join.clear()
        if not names:
            return comm, lambda got: got
        own = 0 if comm is None else len(comm.out_shapes)
        joined = join_halves(totals)

        def split(got):
            for n, b in zip(names, got[own:]):
                reduced[n] = b.reshape(-1, b.shape[2])[:, :W[n].shape[1]]
            return got[:own]

        return (joined if comm is None else merge_comms(comm, joined)), split

    def flush_joins():
        comm, split = with_joins(None)
        if comm is not None:
            split(run_comm(comm, name=f"rs_join_{len(reduced)}"))

    in_flight = []

    def xchg_start(names):
        in_flight.append((names,) + begin(exchange_chips([chip_sum[n] for n in names]), f"xchg_start_{names[0]}"))

    def xchg_wait(after, count=1):
        for _ in range(count):
            names, comm, started = in_flight.pop(0)
            add4_join(names, wait_comm(comm, started, after, name=f"xchg_wait_{names[0]}"))

    def ffn_backward(dout, xin, tag, saved, dact_comm=None, after_dact=None):
        n_, gate, up, act = saved
        nd, ng, nu = f"{tag}_w_down", f"{tag}_w_gate", f"{tag}_w_up"
        (dgate, dup), got0 = _with(ffn_dact(dout, full[nd], gate, up, 0.5, name=f"{tag}_dact", comm=dact_comm,
                                            behind=token["last"]), dact_comm)
        if after_dact:
            after_dact(got0)
        G[nd] = mm([(act, dout)], "tn", F32, alpha=0.5, name=f"{tag}_dwd", tm=1408, tn=1024, behind=token["last"])
        hd_ = to_halves(nd)
        comm, split = with_joins(swap_halves([hd_]))
        G[ng], got = mm([(n_, dgate)], "tn", F32, name=f"{tag}_dwg", out_slots=True, tm=1024, tn=1408, rows_inner=True,
                        comm=comm)
        add2([nd], [hd_], split(got))
        xchg_start([nd])
        hg = to_halves(ng)
        G[nu], got_g = mm([(n_, dup)], "tn", F32, name=f"{tag}_dwu", out_slots=True, tm=1024, tn=1408, rows_inner=True,
                          comm=swap_halves([hg]), behind=token["last"])
        add2([ng], [hg], got_g)
        xchg_start([ng])
        hu = to_halves(nu)
        dn, got_u = mm([(dgate, full[ng]), (dup, full[nu])], "nt", F32, name=f"{tag}_dn", b_slots=True, tn=1024, tk=1408,
                       comm=swap_halves([hu]), behind=token["last"])
        add2([nu], [hu], got_u)
        xchg_start([nu])
        dx, G[f"{tag}_norm"] = row_bwd(rms_fn, [V(xin)], [W[f"{tag}_norm"]], [V(dn)], const_diff=[True], res=dout,
                                       name=f"{tag}_dnorm")
        return dx

    g3 = ffn_backward(dy, x3, "ffn2", (n2, gate2, up2, act2))
    xchg_wait(g3)

    d_om = mm([(g3, full["mem_w_o"])], "nt", F32, name="mem_o_dx", behind=token["last"])
    G["mem_w_o"] = mm([(om, g3)], "tn", F32, name="mem_o_dw")
    dqm, dkm, dvm = attn_bwd(qm, km, vm, d_om, name="mem_attn_bwd", **mem_kw)
    dqm_raw, G["mem_q_norm"] = row_bwd(rms_fn, [V(qm_raw, 0, hd, hd)], [W["mem_q_norm"]], [V(dqm, 0, hd, hd)],
                                       const_diff=[True], heads=MEM_HEADS, row_dtype=BF16, name="mem_q_norm_bwd")
    dkm_raw, G["mem_k_norm"] = row_bwd(rms_fn, [V(km_raw, 0, hd, hd)], [W["mem_k_norm"]], [V(dkm, 0, hd, hd)],
                                       const_diff=[True], heads=MEM_HEADS, row_dtype=BF16, name="mem_k_norm_bwd")
    dhm = mm([(dqm_raw, full["mem_w_q"])], "nt", F32, name="mem_q_dx")
    G["mem_w_q"] = mm([(hm, dqm_raw)], "tn", F32, name="mem_q_dw")
    dmn = mm([(dkm_raw, full["mem_w_k"]), (dvm, full["mem_w_v"])], "nt", F32, name="mem_kv_dx")
    G["mem_w_k"] = mm([(mn, dkm_raw)], "tn", F32, name="mem_k_dw")
    G["mem_w_v"] = mm([(mn, dvm)], "tn", F32, name="mem_v_dw")
    _, G["mem_norm"] = row_bwd(rms_fn, [V(mems)], [W["mem_norm"]], [V(dmn)], const_diff=[True], row_dtype=BF16,
                               name="mem_norm_bwd")
    g2, G["mem_attn_norm"] = row_bwd(rms_fn, [V(x2)], [W["mem_attn_norm"]], [V(dhm)], const_diff=[True], res=g3,
                                     name="mem_attn_norm_bwd")

    xchg_wait(g2, 2)

    d_ocat = mm([(g2, full["w_out"])], "nt", F32, name="out_proj_dx")
    G["w_out"] = mm([(o_cat, g2)], "tn", F32, name="out_proj_dw")

    d_oraw, d_zr, G["gla_out_norm"] = row_bwd(gla_out_fn, gla_rows, [W["gla_out_norm"]],
                                              [V(d_ocat, mla_w, GLA_DV, GLA_DV)], const_diff=[True], heads=GLA_HEADS,
                                              name="gla_out_bwd")
    mid_b_halves = [to_halves(n) for n in MID_B]
    comm, split = with_joins(swap_halves(mid_b_halves))
    (d_gq, d_gk, d_gv, d_la), got = gla_bwd(z, la, tri, tri.T, states, d_oraw, name="gla_scan_bwd", comm=comm, **gla_kw)
    add2(MID_B, mid_b_halves, split(got))
    xchg_start(MID_B)
    d_pre, G["gla_b_gate"] = row_bwd(gate_fn, [V(pre)], [W["gla_b_gate"]], [V(d_la)], const_diff=[True], row_dtype=BF16,
                                     name="gla_log_decay_bwd")
    d_zg = mm([(d_pre, w_gate2_p)], "nt", BF16, name="gla_gate_dx", behind=token["last"])
    G["gla_w_gate2"] = mm([(zg, d_pre)], "tn", F32, name="gla_gate_dw")[:GLA_GATE_RANK]

    comm, split = with_joins(None)
    (d_qh, d_kh, d_v), got = _with(attn_bwd(qh, kh, kvraw, d_ocat, name="mla_attn_bwd", comm=comm, **mla_kw), comm)
    split(got)
    cq = [V(d_qh, 0, LANE, HP), V(d_qh, LANE, LANE, HP)]
    ck = [V(d_kh, 0, LANE, HP), V(d_kh, LANE, LANE, HP)]
    d_qraw, d_gqn, d_gqr = row_bwd(qk_prep_fn, q_rows, [gqn, gqr, rot], cq, const_diff=[True, True, False],
                                   heads=MLA_HEADS, row_dtype=BF16, pack={0: (0, HP), 1: (LANE, HP)},
                                   pack_width=MLA_HEADS * HP, name="q_prep_bwd")
    d_kvraw, d_zkr, d_gkn, d_gkr = row_bwd(qk_prep_fn, k_rows, [gkn, gkr, rot], ck, const_diff=[True, True, False],
                                           heads=MLA_HEADS, row_dtype=BF16, pack={0: (0, HP)}, pack_width=MLA_HEADS * HP,
                                           fills=[(V(d_v, 0, MLA_V, MLA_V), LANE, HP)], name="k_prep_bwd")
    G["mla_q_norm"] = jnp.concatenate([d_gqn, d_gqr[:, :MLA_ROPE]], axis=1)
    G["mla_k_norm"] = jnp.concatenate([d_gkn, d_gkr[:, :MLA_ROPE]], axis=1)
    d_qa = mm([(d_qraw, w_q_up_p)], "nt", F32, name="q_up_dx")
    G["w_q_up"] = _unpad_q_up(mm([(qa, d_qraw)], "tn", F32, name="q_up_dw"))
    d_kva = mm([(d_kvraw, full["w_kv_up"])], "nt", F32, name="kv_up_dx")
    G["w_kv_up"] = mm([(kva, d_kvraw)], "tn", F32, name="kv_up_dw")
    d_zq, G["q_a_norm"] = row_bwd(rms_fn, [V(z, off["zq"], q_rank)], [W["q_a_norm"]], [V(d_qa)], const_diff=[True],
                                  row_dtype=BF16, name="q_a_norm_bwd")
    d_zkv, G["kv_a_norm"] = row_bwd(rms_fn, [V(z, off["zkv"], kv_rank)], [W["kv_a_norm"]], [V(d_kva)], const_diff=[True],
                                    row_dtype=BF16, name="kv_a_norm_bwd")

    seg = {"gv": d_gv, "zr": d_zr, "zq": d_zq, "gq": d_gq, "gk": d_gk, "zkv": d_zkv, "zkr": d_zkr, "zg": d_zg}
    dz = jnp.concatenate([_pad_cols(seg[n].astype(BF16), lay.size[n]) for n in lay.order], axis=1)
    xchg_wait(dz)
    comm, split = with_joins(None)
    dz_shards = lay.to_shards(dz)
    dh, got = _with(mm([(dz_shards, full["w_in"])], "nt", F32, name="in_proj_dx", b_slots=True, comm=comm), comm)
    split(got)
    G["w_in"] = mm([(h, dz_shards)], "tn", F32, name="in_proj_dw", out_slots=True)
    g1, G["mix_norm"] = row_bwd(rms_fn, [V(x1)], [W["mix_norm"]], [V(dh)], const_diff=[True], res=g2,
                                name="mix_norm_bwd")

    mid_a = [n for n in MID_A if n != "gla_w_gate2"]
    mid_a_halves = [to_halves(n) for n in mid_a]

    def mid_a_sums(got):
        add2(mid_a, mid_a_halves, got)
        xchg_start(mid_a)

    gx = ffn_backward(g1, xs, "ffn1", ffn1_saved, dact_comm=swap_halves(mid_a_halves), after_dact=mid_a_sums)
    xchg_wait(gx, 2)

    grad, delta, new_m, new_v = {}, {}, {}, {}

    def adam_group(names, tag, behind=None):
        if any(n not in reduced for n in names):
            flush_joins()
        res = adamw([(W[n], reduced[n], M1[n], V2[n]) for n in names], name=f"adamw_{tag}", behind=behind)
        for n, (g_, d_, m_, v_) in zip(names, res):
            grad[n], delta[n], new_m[n], new_v[n] = g_, d_, m_, v_

    adam_group(FFN2, "ffn2", behind=token["last"])
    adam_group(mid_a + MID_B, "mid", behind=token["last"])
    adam_group(FFN1[2:], "ffn1_down", behind=token["last"])
    xchg_wait(delta[FFN1[2]], 2)
    adam_group(FFN1[:2], "ffn1_up")

    small_names = SMALL + ["gla_w_gate2"]
    packed = small_names + ["loss"]
    small_sum = allreduce_small(_pack_small([G[n] for n in packed]), name="allreduce_small")
    small_g = dict(zip(packed, _unpack_small(small_sum, [G[n].shape for n in packed])))
    loss = small_g["loss"][0, 0]
    shard_c = W["gla_w_gate2"].shape[1]
    grad["gla_w_gate2"] = lax.dynamic_slice_in_dim(small_g["gla_w_gate2"], chip * shard_c, shard_c, axis=1)
    pw = _pack_small([W[n] for n in SMALL] + [W["gla_w_gate2"]])
    pg = _pack_small([small_g[n] for n in SMALL] + [grad["gla_w_gate2"]])
    pm = _pack_small([M1[n] for n in SMALL] + [M1["gla_w_gate2"]])
    pv = _pack_small([V2[n] for n in SMALL] + [V2["gla_w_gate2"]])
    (_, pd, pnm, pnv), = adamw([(pw, pg, pm, pv)], name="adamw_small")
    shapes = [W[n].shape for n in small_names]
    for n, d_, m_, v_ in zip(small_names, _unpack_small(pd, shapes), _unpack_small(pnm, shapes), _unpack_small(pnv, shapes)):
        delta[n], new_m[n], new_v[n] = d_, m_, v_
        if n != "gla_w_gate2":
            grad[n] = small_g[n]

    lead = lambda d: [d[n].reshape(args[n].shape) for n in WEIGHTS]
    return (loss, gx[None], *lead(grad), *lead(delta), *lead(new_m), *lead(new_v))
```

```python
import functools

import numpy as np
import jax
import jax.numpy as jnp
from jax import lax
from jax.experimental import pallas as pl
from jax.experimental.pallas import tpu as pltpu

F32 = jnp.float32
BF16 = jnp.bfloat16
MXU_DTYPE = jnp.bfloat16
MESH = pl.DeviceIdType.MESH
ANY = pl.BlockSpec(memory_space=pl.ANY)

LANE = 128
EPS = 1e-6
CHUNK = 64
MLA_HEADS = 8
MLA_NOPE = 128
MLA_ROPE = 64
MLA_QK = MLA_NOPE + MLA_ROPE
MLA_V = 128
MLA_HEAD_PAD = 2 * LANE
ROPE_THETA = 10000.0
GLA_HEADS = 4
GLA_DK = 128
GLA_DV = 256
GLA_GATE_RANK = 16
GLA_TAU = 16.0
MEM_HEADS = 4
MEM_HEAD_DIM = 128
N_CHIPS = 4
N_DEV = 8

ADAM_LR = 0.001
ADAM_B1 = 0.9
ADAM_B2 = 0.999
ADAM_EPS = 1e-08
ADAM_WD = 0.01
ADAM_STEP = 10

VMEM_LIMIT = 56 * 1024 * 1024


def _cparams(sem=None):
    if sem is None:
        return pltpu.CompilerParams(vmem_limit_bytes=VMEM_LIMIT)
    return pltpu.CompilerParams(dimension_semantics=sem, vmem_limit_bytes=VMEM_LIMIT)


def _tile(dim, pref, unit=LANE):
    if dim <= pref:
        return dim
    t = (pref // unit) * unit
    while t > unit and dim % t:
        t -= unit
    assert dim % t == 0, (dim, pref, unit)
    return t


class Comm:
    def __init__(self, ins, out_shapes, nsem, start, wait, aliases=None):
        self.ins, self.out_shapes, self.nsem = list(ins), list(out_shapes), nsem
        self.start, self.wait, self.aliases = start, wait, dict(aliases or {})


def merge_comms(a, b):
    ai, ao = len(a.ins), len(a.out_shapes)

    def start(ins, outs, send, recv, base):
        a.start(ins[:ai], outs[:ao], send, recv, base)
        b.start(ins[ai:], outs[ao:], send, recv, base + a.nsem)

    def wait(ins, outs, send, recv, base):
        a.wait(ins[:ai], outs[:ao], send, recv, base)
        b.wait(ins[ai:], outs[ao:], send, recv, base + a.nsem)

    aliases = dict(a.aliases)
    aliases.update({ai + i: ao + o for i, o in b.aliases.items()})
    return Comm(a.ins + b.ins, a.out_shapes + b.out_shapes, a.nsem + b.nsem, start, wait, aliases)


def run_comm(comm, *, name):
    ni, no = len(comm.ins), len(comm.out_shapes)

    def body(*refs):
        ins, outs = refs[:ni], refs[ni:ni + no]
        send, recv = refs[ni + no:]
        comm.start(ins, outs, send, recv, 0)
        comm.wait(ins, outs, send, recv, 0)

    return pl.pallas_call(
        body, name=name, in_specs=[ANY] * ni, out_specs=[ANY] * no, out_shape=comm.out_shapes,
        input_output_aliases=comm.aliases,
        scratch_shapes=[pltpu.SemaphoreType.DMA((comm.nsem,)), pltpu.SemaphoreType.DMA((comm.nsem,))])(*comm.ins)


HBM = pl.BlockSpec(memory_space=pltpu.HBM)
SEM = pl.BlockSpec(memory_space=pltpu.SEMAPHORE)


def start_comm(comm, *, name, after=None):
    assert not comm.aliases
    ni, no = len(comm.ins), len(comm.out_shapes)
    tail = [] if after is None else [after]

    def body(*refs):
        srcs, lands = refs[:ni], refs[ni:ni + no]
        send, recv = refs[ni + no + len(tail)], refs[ni + no + len(tail) + 1]
        token = refs[-1]
        comm.start(srcs, lands, send, recv, 0)
        token[...] = jnp.zeros_like(token)

    through = [pltpu.HBM(a.shape, a.dtype) for a in comm.ins] + [pltpu.HBM(s.shape, s.dtype) for s in comm.out_shapes]
    ops = [pltpu.with_memory_space_constraint(a, pltpu.HBM) for a in comm.ins]
    ops += [pltpu.with_memory_space_constraint(lax.empty(s.shape, s.dtype), pltpu.HBM) for s in comm.out_shapes]
    ops += tail
    res = pl.pallas_call(
        body, name=name, in_specs=[HBM] * (ni + no) + [ANY] * len(tail),
        out_shape=[pltpu.SemaphoreType.DMA((comm.nsem,)), pltpu.SemaphoreType.DMA((comm.nsem,))] + through
        + [jax.ShapeDtypeStruct((8, LANE), F32)],
        out_specs=[SEM, SEM] + [HBM] * (ni + no) + [pl.BlockSpec(memory_space=pltpu.VMEM)],
        input_output_aliases={i: 2 + i for i in range(ni + no)},
        compiler_params=pltpu.CompilerParams(has_side_effects=pltpu.SideEffectType.DATAFLOW_SIDE_EFFECTING))(*ops)
    return res[0], res[1], list(res[2:2 + ni]), list(res[2 + ni:2 + ni + no]), res[-1]


def wait_comm(comm, started, after, *, name):
    send, recv, srcs, lands, _ = started
    ni, no = len(srcs), len(lands)
    after = list(after) if isinstance(after, (list, tuple)) else [after]

    def body(*refs):
        comm.wait(refs[:ni], refs[ni:ni + no], refs[ni + no], refs[ni + no + 1], 0)

    res = pl.pallas_call(
        body, name=name, in_specs=[HBM] * (ni + no) + [SEM, SEM] + [ANY] * len(after),
        out_shape=[pltpu.HBM(a.shape, a.dtype) for a in srcs + lands], out_specs=[HBM] * (ni + no),
        input_output_aliases={i: i for i in range(ni + no)},
        compiler_params=pltpu.CompilerParams(has_side_effects=pltpu.SideEffectType.DATAFLOW_SIDE_EFFECTING),
    )(*srcs, *lands, send, recv, *after)
    return list(res[ni:])


def _pcall(body, ops, *, name, grid, in_specs, out_specs, out_shape, sem, scratch_shapes=(), comm=None, behind=None):
    if behind is not None:
        n_real, inner = len(ops), body
        ops, in_specs = list(ops) + [behind], list(in_specs) + [ANY]

        def body(*refs):
            inner(*refs[:n_real], *refs[n_real + 1:])

    if comm is None:
        return pl.pallas_call(body, name=name, grid=grid, in_specs=in_specs, out_specs=out_specs, out_shape=out_shape,
                              scratch_shapes=list(scratch_shapes), compiler_params=_cparams(sem))(*ops)
    multi = isinstance(out_shape, (list, tuple))
    k_out_shape = list(out_shape) if multi else [out_shape]
    k_out_specs = list(out_specs) if multi else [out_specs]
    nki, nko, nks = len(ops), len(k_out_shape), len(scratch_shapes)
    nci, nco = len(comm.ins), len(comm.out_shapes)

    def wrapped(*refs):
        p = 0
        k_in = refs[p:p + nki]; p += nki
        c_in = refs[p:p + nci]; p += nci
        k_out = refs[p:p + nko]; p += nko
        c_out = refs[p:p + nco]; p += nco
        k_scr = refs[p:p + nks]; p += nks
        send, recv = refs[p:]
        first = pl.program_id(0) == 0
        last = pl.program_id(0) == grid[0] - 1
        for a in range(1, len(grid)):
            first = jnp.logical_and(first, pl.program_id(a) == 0)
            last = jnp.logical_and(last, pl.program_id(a) == grid[a] - 1)

        @pl.when(first)
        def _():
            comm.start(c_in, c_out, send, recv, 0)

        body(*k_in, *k_out, *k_scr)

        @pl.when(last)
        def _():
            comm.wait(c_in, c_out, send, recv, 0)

    res = pl.pallas_call(
        wrapped, name=name, grid=grid, in_specs=list(in_specs) + [ANY] * nci, out_specs=k_out_specs + [ANY] * nco,
        out_shape=k_out_shape + comm.out_shapes,
        input_output_aliases={nki + i: nko + o for i, o in comm.aliases.items()},
        scratch_shapes=list(scratch_shapes) + [pltpu.SemaphoreType.DMA((comm.nsem,)), pltpu.SemaphoreType.DMA((comm.nsem,))],
        compiler_params=_cparams(("arbitrary",) * len(grid)))(*ops, *comm.ins)
    k_res = list(res[:nko]) if multi else res[0]
    return k_res, list(res[nko:])


_DIMS = {"nn": (((1,), (0,)), ((), ())), "nt": (((1,), (1,)), ((), ())), "tn": (((0,), (0,)), ((), ()))}


def _blockspec(shape, index, rows_inner):
    return pl.BlockSpec(shape, (lambda j, i, k: index(i, j, k)) if rows_inner else index)


def mm(pairs, mode, out_dtype, *, name, alpha=1.0, res=None, tm=1024, tn=1024, tk=4096, b_slots=False, out_slots=False,
       rows_inner=False, comm=None, behind=None):
    a0, b0 = pairs[0]
    if b_slots:
        b_rows, b_cols = b0.shape[1], N_CHIPS * b0.shape[2]
    else:
        b_rows, b_cols = b0.shape
    (M, K) = a0.shape[::-1] if mode == "tn" else a0.shape
    N = b_rows if mode == "nt" else b_cols
    shard = (b_cols if b_slots else N) // N_CHIPS
    tm = _tile(M, tm)
    tn = _tile(shard if (out_slots or (b_slots and mode != "nt")) else N, tn)
    tk = _tile(shard if (b_slots and mode == "nt") else K, tk)
    nk = K // tk
    npairs = len(pairs)
    dims = _DIMS[mode]
    spec = functools.partial(_blockspec, rows_inner=rows_inner)
    if mode == "tn":
        a_spec = spec((tk, tm), lambda i, j, k: (k, i))
    else:
        a_spec = spec((tm, tk), lambda i, j, k: (i, k))
    per = shard // (tk if mode == "nt" else tn)
    if mode == "nt":
        b_spec = (spec((None, tn, tk), lambda i, j, k: (k // per, j, k % per)) if b_slots else
                  spec((tn, tk), lambda i, j, k: (j, k)))
    else:
        b_spec = (spec((None, tk, tn), lambda i, j, k: (j // per, k, j % per)) if b_slots else
                  spec((tk, tn), lambda i, j, k: (k, j)))
    if out_slots:
        assert res is None and mode != "nt"
        o_spec = spec((None, tm, tn), lambda i, j, k: (j // per, i, j % per))
        out_sds = jax.ShapeDtypeStruct((N_CHIPS, M, shard), out_dtype)
    else:
        o_spec = spec((tm, tn), lambda i, j, k: (i, j))
        out_sds = jax.ShapeDtypeStruct((M, N), out_dtype)
    has_res = res is not None

    def body(*refs):
        ab = refs[:2 * npairs]
        res_ref = refs[2 * npairs] if has_res else None
        o_ref = refs[2 * npairs + int(has_res)]

        def products():
            r = None
            for p in range(npairs):
                d = lax.dot_general(ab[2 * p][...].astype(MXU_DTYPE), ab[2 * p + 1][...].astype(MXU_DTYPE), dims,
                                    preferred_element_type=F32)
                r = d if r is None else r + d
            return r

        def finish(r):
            if alpha != 1.0:
                r = r * alpha
            if has_res:
                r = res_ref[...].astype(F32) + r
            o_ref[...] = r.astype(out_dtype)

        if nk == 1:
            finish(products())
            return
        acc = refs[-1]
        k = pl.program_id(2)

        @pl.when(k == 0)
        def _():
            acc[...] = jnp.zeros_like(acc)

        acc[...] += products()

        @pl.when(k == nk - 1)
        def _():
            finish(acc[...])

    ops, specs = [], []
    for a, b in pairs:
        ops += [a, b]
        specs += [a_spec, b_spec]
    if has_res:
        ops.append(res)
        specs.append(o_spec)
    blocks = (N // tn, M // tm) if rows_inner else (M // tm, N // tn)
    return _pcall(body, ops, name=name, grid=blocks + (nk,), in_specs=specs, out_specs=o_spec, out_shape=out_sds,
                  scratch_shapes=[pltpu.VMEM((tm, tn), F32)] if nk > 1 else [],
                  sem=("parallel", "parallel", "arbitrary"), comm=comm, behind=behind)


def _sigmoid(x):
    return 1.0 / (1.0 + jnp.exp(-x))


def ffn_up(n, wg, wu, *, name, tm=512, tn=1408, comm=None, behind=None):
    M, K = n.shape
    shard = wg.shape[2]
    N = N_CHIPS * shard
    tm, tn = _tile(M, tm), _tile(shard, tn)
    per = shard // tn
    w_spec = pl.BlockSpec((None, K, tn), lambda j, i: (j // per, 0, j % per))

    def body(n_ref, wg_ref, wu_ref, g_ref, u_ref, a_ref):
        nv = n_ref[...].astype(MXU_DTYPE)
        g = jnp.dot(nv, wg_ref[...].astype(MXU_DTYPE), preferred_element_type=F32)
        u = jnp.dot(nv, wu_ref[...].astype(MXU_DTYPE), preferred_element_type=F32)
        g_ref[...] = g.astype(g_ref.dtype)
        u_ref[...] = u.astype(u_ref.dtype)
        a_ref[...] = (g * _sigmoid(g) * u).astype(a_ref.dtype)

    o_spec = pl.BlockSpec((tm, tn), lambda j, i: (i, j))
    sds = jax.ShapeDtypeStruct((M, N), BF16)
    return _pcall(
        body, [n, wg, wu], name=name, grid=(N // tn, M // tm),
        in_specs=[pl.BlockSpec((tm, K), lambda j, i: (i, 0)), w_spec, w_spec],
        out_specs=[o_spec, o_spec, o_spec], out_shape=[sds, sds, sds], sem=("parallel", "parallel"), comm=comm,
        behind=behind)


def ffn_dact(dy, wd, gate, up, alpha, *, name, tm=512, tn=1408, comm=None, behind=None):
    M, K = dy.shape
    N = wd.shape[0]
    tm, tn = _tile(M, tm), _tile(N, tn)

    def body(dy_ref, wd_ref, g_ref, u_ref, dg_ref, du_ref):
        da = lax.dot_general(dy_ref[...].astype(MXU_DTYPE), wd_ref[...].astype(MXU_DTYPE), _DIMS["nt"],
                             preferred_element_type=F32) * alpha
        g = g_ref[...].astype(F32)
        u = u_ref[...].astype(F32)
        s = _sigmoid(g)
        du_ref[...] = (da * (g * s)).astype(du_ref.dtype)
        dg_ref[...] = (da * u * (s * (1.0 + g * (1.0 - s)))).astype(dg_ref.dtype)

    o_spec = pl.BlockSpec((tm, tn), lambda j, i: (i, j))
    sds = jax.ShapeDtypeStruct((M, N), BF16)
    return _pcall(
        body, [dy, wd, gate, up], name=name, grid=(N // tn, M // tm),
        in_specs=[pl.BlockSpec((tm, K), lambda j, i: (i, 0)), pl.BlockSpec((tn, K), lambda j, i: (j, 0)), o_spec, o_spec],
        out_specs=[o_spec, o_spec], out_shape=[sds, sds], sem=("parallel", "parallel"), comm=comm, behind=behind)


def _window(width, off, ext):
    ww = LANE
    while ww < width:
        if ww >= ext and off // ww == (off + ext - 1) // ww and width % ww == 0:
            break
        ww *= 2
    else:
        ww = width
    return ww, off // ww, off - (off // ww) * ww


class V:
    def __init__(self, arr, off=0, w=None, hs=0, diff=True):
        self.arr, self.off, self.hs, self.diff = arr, off, hs, diff
        self.w = arr.shape[1] - off if w is None else w

    def window(self, heads, tr):
        ww, blk, inner = _window(self.arr.shape[1], self.off, (heads - 1) * self.hs + self.w)
        return pl.BlockSpec((tr, ww), lambda i, blk=blk: (i, blk)), inner


def _const_spec(c):
    return pl.BlockSpec(c.shape, lambda i: (0, 0))


def row_fwd(fn, rows, consts, outs, out_map, *, heads=1, tr=256, name):
    S = rows[0].arr.shape[0]
    tr = _tile(S, tr, 8)
    wins = [v.window(heads, tr) for v in rows]
    nr, nc = len(rows), len(consts)

    def body(*refs):
        row_refs, const_refs, out_refs = refs[:nr], refs[nr:nr + nc], refs[nr + nc:]
        cv = [c[...].astype(F32) for c in const_refs]
        for h in range(heads):
            rv = []
            for v, (_, io), r in zip(rows, wins, row_refs):
                lo = io + h * v.hs
                rv.append(r[:, lo:lo + v.w].astype(F32))
            res = fn(*rv, *cv)
            for (ai, off, hs), o in zip(out_map, res):
                lo = off + h * hs
                out_refs[ai][:, lo:lo + o.shape[1]] = o.astype(out_refs[ai].dtype)

    return pl.pallas_call(
        body, name=name, grid=(S // tr,),
        in_specs=[w[0] for w in wins] + [_const_spec(c) for c in consts],
        out_specs=[pl.BlockSpec((tr, w), lambda i: (i, 0)) for w, _ in outs],
        out_shape=[jax.ShapeDtypeStruct((S, w), d) for w, d in outs],
        compiler_params=_cparams(("parallel",)))(*[v.arr for v in rows], *consts)


def row_bwd(fn, rows, consts, cots, *, const_diff, heads=1, tr=256, res=None, row_dtype=F32, pack=None, pack_width=0,
            fills=(), name):
    S = rows[0].arr.shape[0]
    tr = _tile(S, tr, 8)
    pack = dict(pack or {})
    nr, nc, nct, nf = len(rows), len(consts), len(cots), len(fills)
    wins = [v.window(heads, tr) for v in rows]
    cwins = [v.window(heads, tr) for v in cots]
    fwins = [v.window(heads, tr) for v, _, _ in fills]
    drows = [k for k, v in enumerate(rows) if v.diff]
    dconsts = [k for k in range(nc) if const_diff[k]]
    has_res = res is not None
    assert not (has_res and 0 in pack)
    widths = [pack_width] if pack else []
    place = []
    for n, k in enumerate(drows):
        if n in pack:
            place.append((0,) + tuple(pack[n]))
        else:
            place.append((len(widths), 0, rows[k].w))
            widths.append(rows[k].w * (heads if rows[k].hs else 1))

    def body(*refs):
        row_refs = refs[:nr]
        const_refs = refs[nr:nr + nc]
        cot_refs = refs[nr + nc:nr + nc + nct]
        p = nr + nc + nct
        fill_refs = refs[p:p + nf]
        p += nf
        res_ref = refs[p] if has_res else None
        p += int(has_res)
        grow_refs = refs[p:p + len(widths)]
        gconst_refs = refs[p + len(widths):]
        i = pl.program_id(0)
        cv = [c[...].astype(F32) for c in const_refs]
        shared = [None] * len(drows)
        gc_sum = [None] * len(dconsts)
        for h in range(heads):
            rv = []
            for v, (_, io), r in zip(rows, wins, row_refs):
                lo = io + h * v.hs
                rv.append(r[:, lo:lo + v.w].astype(F32))
            ct = []
            for v, (_, io), r in zip(cots, cwins, cot_refs):
                lo = io + h * v.hs
                ct.append(r[:, lo:lo + v.w].astype(F32))

            def closed(*d):
                rr, cc = list(rv), list(cv)
                for k, val in zip(drows, d[:len(drows)]):
                    rr[k] = val
                for k, val in zip(dconsts, d[len(drows):]):
                    cc[k] = val
                return tuple(fn(*rr, *cc))

            _, vjp = jax.vjp(closed, *[rv[k] for k in drows], *[cv[k] for k in dconsts])
            grads = vjp(tuple(ct))
            for n, k in enumerate(drows):
                g = grads[n]
                if rows[k].hs == 0 and heads > 1:
                    shared[n] = g if shared[n] is None else shared[n] + g
                else:
                    if n == 0 and has_res:
                        g = g + res_ref[:, h * rows[k].w:(h + 1) * rows[k].w].astype(F32)
                    out, off, hs = place[n]
                    grow_refs[out][:, off + h * hs:off + h * hs + rows[k].w] = g.astype(row_dtype)
            for (v, off, hs), (_, io), r in zip(fills, fwins, fill_refs):
                lo = io + h * v.hs
                grow_refs[0][:, off + h * hs:off + h * hs + v.w] = r[:, lo:lo + v.w].astype(row_dtype)
            for n in range(len(dconsts)):
                g = grads[len(drows) + n]
                gc_sum[n] = g if gc_sum[n] is None else gc_sum[n] + g
        for n, k in enumerate(drows):
            if shared[n] is not None:
                g = shared[n]
                if n == 0 and has_res:
                    g = g + res_ref[...].astype(F32)
                grow_refs[place[n][0]][...] = g.astype(row_dtype)

        @pl.when(i == 0)
        def _():
            for n in range(len(dconsts)):
                gconst_refs[n][...] = gc_sum[n]

        @pl.when(i > 0)
        def _():
            for n in range(len(dconsts)):
                gconst_refs[n][...] += gc_sum[n]

    in_specs = [w[0] for w in wins] + [_const_spec(c) for c in consts] + [w[0] for w in cwins] + [w[0] for w in fwins]
    ops = [v.arr for v in rows] + list(consts) + [v.arr for v in cots] + [v.arr for v, _, _ in fills]
    if has_res:
        in_specs.append(pl.BlockSpec((tr, widths[0]), lambda i: (i, 0)))
        ops.append(res)
    out_specs = [pl.BlockSpec((tr, w), lambda i: (i, 0)) for w in widths]
    out_shape = [jax.ShapeDtypeStruct((S, w), row_dtype) for w in widths]
    for k in dconsts:
        out_specs.append(_const_spec(consts[k]))
        out_shape.append(jax.ShapeDtypeStruct(consts[k].shape, F32))
    return pl.pallas_call(body, name=name, grid=(S // tr,), in_specs=in_specs, out_specs=out_specs,
                          out_shape=out_shape, compiler_params=_cparams(("arbitrary",)))(*ops)


def _rms(x, g, n=None):
    n = x.shape[-1] if n is None else n
    ms = jnp.sum(x * x, axis=-1, keepdims=True) * (1.0 / n)
    return x * lax.rsqrt(ms + EPS) * g


def rms_fn(x, g):
    return (_rms(x, g),)


def qk_prep_fn(nope, rope, cos, sin, gn, gr, rot):
    ms = (jnp.sum(nope * nope, axis=-1, keepdims=True) + jnp.sum(rope * rope, axis=-1, keepdims=True)) * (1.0 / MLA_QK)
    r = lax.rsqrt(ms + EPS)
    on = nope * r * gn
    orr = rope * r * gr
    turned = jnp.dot(orr, rot, precision=lax.Precision.HIGHEST, preferred_element_type=F32)
    return on, orr * cos + turned * sin


def gla_out_fn(o, zr, g):
    return (_rms(o, g) * (zr * _sigmoid(zr)),)


def gate_fn(pre, b):
    t = pre + b
    return ((jnp.minimum(t, 0.0) - jnp.log(1.0 + jnp.exp(-jnp.abs(t)))) * (1.0 / GLA_TAU),)


def _attn_probs(q_ref, k_ref, scale, q0, kext):
    s = lax.dot_general(q_ref[...].astype(MXU_DTYPE), k_ref[0:kext, :].astype(MXU_DTYPE), _DIMS["nt"],
                        preferred_element_type=F32) * scale
    if q0 is not None:
        qc = (q0 + lax.broadcasted_iota(jnp.int32, s.shape, 0)) // CHUNK
        kc = lax.broadcasted_iota(jnp.int32, s.shape, 1) // CHUNK
        s = jnp.where(kc <= qc, s, -1e30)
    m = jnp.max(s, axis=-1, keepdims=True)
    e = jnp.exp(s - m)
    return e / jnp.sum(e, axis=-1, keepdims=True)


def _per_query_block(one, causal, nq, tq, Sk):
    if not causal:
        one(None, Sk, None)
        return
    assert tq % CHUNK == 0
    for ib in range(nq):
        pl.when(pl.program_id(1) == ib)(functools.partial(one, ib * tq, min(Sk, (ib + 1) * tq), ib))


def attn_fwd(q, k, v, *, heads, dk, dv, v_off, v_hs, scale, causal, name, tq=256, comm=None):
    Sq, Sk = q.shape[0], k.shape[0]
    tq = _tile(Sq, tq, 8)

    def body(q_ref, k_ref, v_ref, o_ref):
        def one(q0, kext, ib):
            p = _attn_probs(q_ref, k_ref, scale, q0, kext)
            o_ref[...] = jnp.dot(p.astype(MXU_DTYPE), v_ref[0:kext, :].astype(MXU_DTYPE),
                                 preferred_element_type=F32).astype(o_ref.dtype)

        _per_query_block(one, causal, Sq // tq, tq, Sk)

    return _pcall(
        body, [q, k, v], name=name, grid=(heads, Sq // tq),
        in_specs=[pl.BlockSpec((tq, dk), lambda h, i: (i, h)), pl.BlockSpec((Sk, dk), lambda h, i: (0, h)),
                  pl.BlockSpec((Sk, dv), lambda h, i: (0, v_off + h * v_hs))],
        out_specs=pl.BlockSpec((tq, dv), lambda h, i: (i, h)),
        out_shape=jax.ShapeDtypeStruct((Sq, heads * dv), BF16), sem=("parallel", "parallel"), comm=comm)


def attn_bwd(q, k, v, do, *, heads, dk, dv, v_off, v_hs, scale, causal, name, tq=256, comm=None):
    Sq, Sk = q.shape[0], k.shape[0]
    tq = _tile(Sq, tq, 8)

    def body(q_ref, k_ref, v_ref, do_ref, dq_ref, dk_ref, dv_ref):
        @pl.when(pl.program_id(1) == 0)
        def _():
            dk_ref[...] = jnp.zeros_like(dk_ref)
            dv_ref[...] = jnp.zeros_like(dv_ref)

        def one(q0, kext, ib):
            p = _attn_probs(q_ref, k_ref, scale, q0, kext)
            dob = do_ref[...].astype(MXU_DTYPE)
            dp = lax.dot_general(dob, v_ref[0:kext, :].astype(MXU_DTYPE), _DIMS["nt"], preferred_element_type=F32)
            delta = jnp.sum(p * dp, axis=-1, keepdims=True)
            ds = (p * (dp - delta) * scale).astype(MXU_DTYPE)
            dq_ref[...] = jnp.dot(ds, k_ref[0:kext, :].astype(MXU_DTYPE), preferred_element_type=F32)
            dk_ref[0:kext, :] += lax.dot_general(ds, q_ref[...].astype(MXU_DTYPE), _DIMS["tn"],
                                                 preferred_element_type=F32)
            dv_ref[0:kext, :] += lax.dot_general(p.astype(MXU_DTYPE), dob, _DIMS["tn"], preferred_element_type=F32)

        _per_query_block(one, causal, Sq // tq, tq, Sk)

    return _pcall(
        body, [q, k, v, do], name=name, grid=(heads, Sq // tq),
        in_specs=[pl.BlockSpec((tq, dk), lambda h, i: (i, h)), pl.BlockSpec((Sk, dk), lambda h, i: (0, h)),
                  pl.BlockSpec((Sk, dv), lambda h, i: (0, v_off + h * v_hs)),
                  pl.BlockSpec((tq, dv), lambda h, i: (i, h))],
        out_specs=[pl.BlockSpec((tq, dk), lambda h, i: (i, h)), pl.BlockSpec((Sk, dk), lambda h, i: (0, h)),
                   pl.BlockSpec((Sk, dv), lambda h, i: (0, h))],
        out_shape=[jax.ShapeDtypeStruct((Sq, heads * dk), F32), jax.ShapeDtypeStruct((Sk, heads * dk), F32),
                   jax.ShapeDtypeStruct((Sk, heads * dv), F32)],
        sem=("parallel", "arbitrary"), comm=comm)


def _gla_chunk(k, g, tri_ref):
    b = jnp.dot(tri_ref[...], g, precision=lax.Precision.HIGHEST, preferred_element_type=F32)
    b_end = jnp.sum(g, axis=0, keepdims=True)
    e = jnp.exp(b_end - b)
    return k * e, e, jnp.exp(b_end)


def _gla_windows(z, q_off, k_off, v_off, rows_of):
    H, DK, DV = GLA_HEADS, GLA_DK, GLA_DV
    specs, inner = [], []
    for off, ext in ((q_off, H * DK), (k_off, H * DK), (v_off, H * DV)):
        ww, blk, io = _window(z.shape[1], off, ext)
        specs.append(pl.BlockSpec((CHUNK, ww), lambda c, blk=blk: (rows_of(c), blk)))
        inner.append(io)
    return specs, inner


def gla_fwd(z, la, tri, *, q_off, k_off, v_off, name, comm=None):
    S = z.shape[0]
    nchunk = S // CHUNK
    H, DK, DV = GLA_HEADS, GLA_DK, GLA_DV
    qscale = DK ** -0.5
    zspecs, (qi, ki, vi) = _gla_windows(z, q_off, k_off, v_off, lambda c: c)

    def body(q_ref, k_ref, v_ref, la_ref, tri_ref, o_ref, st_ref, state):
        @pl.when(pl.program_id(0) == 0)
        def _():
            state[...] = jnp.zeros_like(state)

        for h in range(H):
            dks, dvs = slice(h * DK, (h + 1) * DK), slice(h * DV, (h + 1) * DV)
            k = k_ref[:, ki + h * DK:ki + (h + 1) * DK].astype(F32)
            v = v_ref[:, vi + h * DV:vi + (h + 1) * DV]
            q = q_ref[:, qi + h * DK:qi + (h + 1) * DK].astype(F32)
            kdec, _, decay = _gla_chunk(k, la_ref[:, dks].astype(F32), tri_ref)
            ut = lax.dot_general(v.astype(MXU_DTYPE), kdec.astype(MXU_DTYPE), _DIMS["tn"], preferred_element_type=F32)
            new = state[h] * decay + ut
            state[h] = new
            st_ref[h] = new
            qs = (q * qscale).astype(MXU_DTYPE)
            o_ref[:, dvs] = lax.dot_general(qs, new.astype(MXU_DTYPE), _DIMS["nt"], preferred_element_type=F32)

    return _pcall(
        body, [z, z, z, la, tri], name=name, grid=(nchunk,),
        in_specs=zspecs + [pl.BlockSpec((CHUNK, H * DK), lambda c: (c, 0)), pl.BlockSpec((CHUNK, CHUNK), lambda c: (0, 0))],
        out_specs=[pl.BlockSpec((CHUNK, H * DV), lambda c: (c, 0)),
                   pl.BlockSpec((H, None, DV, DK), lambda c: (0, c, 0, 0))],
        out_shape=[jax.ShapeDtypeStruct((S, H * DV), F32), jax.ShapeDtypeStruct((H, nchunk, DV, DK), F32)],
        scratch_shapes=[pltpu.VMEM((H, DV, DK), F32)], sem=("arbitrary",), comm=comm)


def gla_bwd(z, la, tri, trit, states, do, *, q_off, k_off, v_off, name, comm=None):
    S = z.shape[0]
    nchunk = S // CHUNK
    H, DK, DV = GLA_HEADS, GLA_DK, GLA_DV
    qscale = DK ** -0.5
    last = nchunk - 1
    zspecs, (qi, ki, vi) = _gla_windows(z, q_off, k_off, v_off, lambda c: last - c)

    def body(q_ref, k_ref, v_ref, la_ref, tri_ref, trit_ref, st_ref, sp_ref, do_ref, dq_ref, dk_ref, dv_ref, dla_ref,
             dstate):
        c = pl.program_id(0)
        cc = last - c

        @pl.when(c == 0)
        def _():
            dstate[...] = jnp.zeros_like(dstate)

        for h in range(H):
            dks, dvs = slice(h * DK, (h + 1) * DK), slice(h * DV, (h + 1) * DV)
            kf = k_ref[:, ki + h * DK:ki + (h + 1) * DK].astype(F32)
            vb16 = v_ref[:, vi + h * DV:vi + (h + 1) * DV].astype(MXU_DTYPE)
            q = q_ref[:, qi + h * DK:qi + (h + 1) * DK].astype(F32)
            kdec, e, decay = _gla_chunk(kf, la_ref[:, dks].astype(F32), tri_ref)
            dob = do_ref[:, dvs].astype(MXU_DTYPE)
            stb = st_ref[h].astype(MXU_DTYPE)
            qs = (q * qscale).astype(MXU_DTYPE)
            dq_ref[:, dks] = jnp.dot(dob, stb, preferred_element_type=F32) * qscale
            dst = dstate[h] + lax.dot_general(dob, qs, _DIMS["tn"], preferred_element_type=F32)
            prev = jnp.where(cc > 0, sp_ref[h], 0.0)
            ddecay = jnp.sum(dst * prev, axis=0, keepdims=True)
            dstate[h] = dst * decay
            dub = dst.astype(MXU_DTYPE)
            dv_ref[:, dvs] = lax.dot_general(kdec.astype(MXU_DTYPE), dub, _DIMS["nt"], preferred_element_type=F32)
            dkdec = jnp.dot(vb16, dub, preferred_element_type=F32)
            dk_ref[:, dks] = dkdec * e
            w = dkdec * kf * e
            db_end = jnp.sum(w, axis=0, keepdims=True) + ddecay * decay
            dla_ref[:, dks] = db_end - jnp.dot(trit_ref[...], w, precision=lax.Precision.HIGHEST,
                                               preferred_element_type=F32)

    def rows(width):
        return pl.BlockSpec((CHUNK, width), lambda c: (last - c, 0))

    square = pl.BlockSpec((CHUNK, CHUNK), lambda c: (0, 0))
    return _pcall(
        body, [z, z, z, la, tri, trit, states, states, do], name=name, grid=(nchunk,),
        in_specs=zspecs + [rows(H * DK), square, square,
                           pl.BlockSpec((H, None, DV, DK), lambda c: (0, last - c, 0, 0)),
                           pl.BlockSpec((H, None, DV, DK), lambda c: (0, jnp.maximum(last - c - 1, 0), 0, 0)),
                           rows(H * DV)],
        out_specs=[rows(H * DK), rows(H * DK), rows(H * DV), rows(H * DK)],
        out_shape=[jax.ShapeDtypeStruct((S, H * DK), F32), jax.ShapeDtypeStruct((S, H * DK), F32),
                   jax.ShapeDtypeStruct((S, H * DV), F32), jax.ShapeDtypeStruct((S, H * DK), F32)],
        scratch_shapes=[pltpu.VMEM((H, DV, DK), F32)], sem=("arbitrary",), comm=comm)


def loss_head(y, target, *, name, tr=256):
    S, D = y.shape
    tr = _tile(S, tr, 8)

    def body(y_ref, t_ref, dy_ref, loss_ref):
        i = pl.program_id(0)
        err = y_ref[...] - t_ref[...]
        dy_ref[...] = err * (1.0 / D)
        part = jnp.zeros((1, LANE), F32) + 0.5 * jnp.sum(jnp.sum(err * err, axis=-1, keepdims=True) * (1.0 / D))

        @pl.when(i == 0)
        def _():
            loss_ref[...] = part

        @pl.when(i > 0)
        def _():
            loss_ref[...] += part

    spec = pl.BlockSpec((tr, D), lambda i: (i, 0))
    return pl.pallas_call(
        body, name=name, grid=(S // tr,), in_specs=[spec, spec],
        out_specs=[spec, pl.BlockSpec((1, LANE), lambda i: (0, 0))],
        out_shape=[jax.ShapeDtypeStruct((S, D), F32), jax.ShapeDtypeStruct((1, LANE), F32)],
        compiler_params=_cparams(("arbitrary",)))(y, target)


def _core_index():
    return lax.axis_index("c").astype(jnp.int32).reshape(1)


def _chip_slots():
    x, y, c = lax.axis_index("x"), lax.axis_index("y"), lax.axis_index("c")
    return jnp.stack([2 * x + y, 2 * (1 - x) + y, 2 * x + (1 - y), 2 * (1 - x) + (1 - y), c]).astype(jnp.int32)


def sum_chip_parts(own, parts, *, name, tr=1024):
    _, R, C = own.shape
    tr = _tile(R, tr, 8)

    def body(idx_ref, o_ref, p0_ref, p1_ref, p2_ref, out_ref):
        acc = o_ref[...].astype(F32) + p0_ref[...].astype(F32)
        acc = acc + p1_ref[...].astype(F32)
        out_ref[...] = acc + p2_ref[...].astype(F32)

    def slot(k):
        return pl.BlockSpec((None, tr, C), lambda i, idx: (idx[k], i, 0))

    grid_spec = pltpu.PrefetchScalarGridSpec(num_scalar_prefetch=1, grid=(R // tr,),
                                             in_specs=[slot(0), slot(1), slot(2), slot(3)], out_specs=slot(4))
    return pl.pallas_call(body, name=name, grid_spec=grid_spec, out_shape=jax.ShapeDtypeStruct((2, R, C), F32),
                          compiler_params=_cparams(("parallel",)))(_chip_slots(), own, parts, parts, parts)


def add_own_half(g, got, out_dtype, *, name, tr=1024):
    n, _, R, C = g.shape
    tr = _tile(R, tr, 8)

    def body(c_ref, a_ref, b_ref, o_ref):
        o_ref[...] = (a_ref[...].astype(F32) + b_ref[...].astype(F32)).astype(out_dtype)

    spec = pl.BlockSpec((None, tr, C), lambda s, i, c: (s, i, 0))
    grid_spec = pltpu.PrefetchScalarGridSpec(
        num_scalar_prefetch=1, grid=(n, R // tr),
        in_specs=[pl.BlockSpec((None, None, tr, C), lambda s, i, c: (s, c[0], i, 0)), spec], out_specs=spec)
    return pl.pallas_call(body, name=name, grid_spec=grid_spec, out_shape=jax.ShapeDtypeStruct((n, R, C), out_dtype),
                          compiler_params=_cparams(("parallel", "parallel")))(_core_index(), g, got)


def adamw(items, *, name, max_steps=16, behind=None):
    c1 = 1.0 / (1.0 - ADAM_B1 ** ADAM_STEP)
    c2 = 1.0 / (1.0 - ADAM_B2 ** ADAM_STEP)
    n = len(items)
    steps = max_steps
    while steps > 1 and any(it[0].shape[0] % (8 * steps) for it in items):
        steps //= 2
    tail = [] if behind is None else [behind]

    def body(*refs):
        for a in range(n):
            w_ref, g_ref, m_ref, v_ref = refs[4 * a:4 * a + 4]
            go_ref, d_ref, nm_ref, nv_ref = refs[4 * n + len(tail) + 4 * a:4 * n + len(tail) + 4 * a + 4]
            gv = g_ref[...]
            go_ref[...] = gv
            nm = ADAM_B1 * m_ref[...] + (1.0 - ADAM_B1) * gv
            nv = ADAM_B2 * v_ref[...] + (1.0 - ADAM_B2) * (gv * gv)
            nm_ref[...] = nm
            nv_ref[...] = nv
            d_ref[...] = -ADAM_LR * ((nm * c1) / (jnp.sqrt(nv * c2) + ADAM_EPS) + ADAM_WD * w_ref[...])

    ops, in_specs, out_specs, out_shape = [], [], [], []
    for w, g, m, v in items:
        R, C = w.shape
        spec = pl.BlockSpec((R // steps, C), lambda i: (i, 0))
        ops += [w, g, m, v]
        in_specs += [spec] * 4
        out_specs += [spec] * 4
        out_shape += [jax.ShapeDtypeStruct((R, C), F32)] * 4
    flat = _pcall(body, ops + tail, name=name, grid=(steps,), in_specs=in_specs + [ANY] * len(tail), out_specs=out_specs,
                  out_shape=out_shape, sem=("parallel",))
    return [tuple(flat[4 * a:4 * a + 4]) for a in range(n)]


def _place():
    x, y, c = lax.axis_index("x"), lax.axis_index("y"), lax.axis_index("c")
    chips = [(1 - x, y), (x, 1 - y), (1 - x, 1 - y)]
    return x, y, c, chips


def _rcopy(src, dst, send, recv, j, to):
    return pltpu.make_async_remote_copy(src_ref=src, dst_ref=dst, send_sem=send.at[j], recv_sem=recv.at[j], device_id=to,
                                        device_id_type=MESH)


def gather_stage1(shards, split):
    n = len(shards)
    ins = [s.reshape(2, s.shape[0] // 2, s.shape[1]) if sp else s for s, sp in zip(shards, split)]
    outs = [jax.ShapeDtypeStruct((N_CHIPS,) + a.shape, a.dtype) for a in ins]

    def start(in_refs, out_refs, send, recv, base):
        x, y, c, chips = _place()
        mine = 2 * x + y
        for i in range(n):
            src = in_refs[i].at[c] if split[i] else in_refs[i]
            dst = out_refs[i].at[mine, c] if split[i] else out_refs[i].at[mine]
            for k, (px, py) in enumerate(chips):
                _rcopy(src, dst, send, recv, base + 3 * i + k, (px, py, c)).start()

    def wait(in_refs, out_refs, send, recv, base):
        x, y, c, chips = _place()
        for i in range(n):
            src = in_refs[i].at[c] if split[i] else in_refs[i]
            for k, (px, py) in enumerate(chips):
                dst = out_refs[i].at[2 * px + py, c] if split[i] else out_refs[i].at[2 * px + py]
                _rcopy(src, dst, send, recv, base + 3 * i + k, (px, py, c)).wait()

    return Comm(ins, outs, 3 * n, start, wait)


def gather_stage2(slots, shards, split):
    n = len(slots)
    own = [s.reshape(2, s.shape[0] // 2, s.shape[1]) if sp else s for s, sp in zip(shards, split)]

    def copies(in_refs, out_refs, send, recv, base):
        x, y, c, chips = _place()
        sib = (x, y, 1 - c)
        for i in range(n):
            j = base + 4 * i
            mine = out_refs[i].at[2 * x + y]
            yield _rcopy(in_refs[n + i], mine, send, recv, j + 3, sib), _rcopy(in_refs[n + i], mine, send, recv, j + 3, sib)
            if split[i]:
                for k, (px, py) in enumerate(chips):
                    s = 2 * px + py
                    yield (_rcopy(in_refs[i].at[s, c], out_refs[i].at[s, c], send, recv, j + k, sib),
                           _rcopy(in_refs[i].at[s, c], out_refs[i].at[s, 1 - c], send, recv, j + k, sib))

    def start(*a):
        for out, _ in copies(*a):
            out.start()

    def wait(*a):
        for _, back in copies(*a):
            back.wait()

    return Comm(list(slots) + own, [jax.ShapeDtypeStruct(s.shape, s.dtype) for s in slots], 4 * n, start, wait,
                {i: i for i in range(n)})


def swap_halves(gs):
    n = len(gs)

    def copies(in_refs, out_refs, send, recv, base):
        x, y, c, _ = _place()
        return [_rcopy(in_refs[i].at[s, 1 - c], out_refs[i].at[s], send, recv, base + N_CHIPS * i + s, (x, y, 1 - c))
                for i in range(n) for s in range(N_CHIPS)]

    def start(*a):
        for cp in copies(*a):
            cp.start()

    def wait(*a):
        for cp in copies(*a):
            cp.wait()

    return Comm(gs, [jax.ShapeDtypeStruct((N_CHIPS,) + g.shape[2:], g.dtype) for g in gs], N_CHIPS * n, start, wait)


def exchange_chips(ps):
    n = len(ps)

    def start(in_refs, out_refs, send, recv, base):
        x, y, c, chips = _place()
        for i in range(n):
            for k, (px, py) in enumerate(chips):
                _rcopy(in_refs[i].at[2 * px + py], out_refs[i].at[2 * x + y], send, recv, base + 3 * i + k,
                       (px, py, c)).start()

    def wait(in_refs, out_refs, send, recv, base):
        x, y, c, chips = _place()
        for i in range(n):
            for k, (px, py) in enumerate(chips):
                _rcopy(in_refs[i].at[2 * px + py], out_refs[i].at[2 * px + py], send, recv, base + 3 * i + k,
                       (px, py, c)).wait()

    return Comm(ps, [jax.ShapeDtypeStruct(p.shape, p.dtype) for p in ps], 3 * n, start, wait)


def join_halves(fs):
    n = len(fs)

    def start(in_refs, out_refs, send, recv, base):
        x, y, c, _ = _place()
        for i in range(n):
            _rcopy(in_refs[i].at[c], out_refs[i].at[c], send, recv, base + i, (x, y, 1 - c)).start()

    def wait(in_refs, out_refs, send, recv, base):
        x, y, c, _ = _place()
        for i in range(n):
            _rcopy(in_refs[i].at[c], out_refs[i].at[1 - c], send, recv, base + i, (x, y, 1 - c)).wait()

    return Comm(fs, [jax.ShapeDtypeStruct(f.shape, f.dtype) for f in fs], n, start, wait, {i: i for i in range(n)})


def allreduce_small(v, *, name):
    m_per, n = v.shape

    def body(x_ref, sum_ref, all_ref, send_sems, recv_sems, local_sem):
        x, y, c, chips = _place()
        me, sibling = (x, y, c), (x, y, 1 - c)

        def rows(px, py, pc):
            return all_ref.at[pl.ds((4 * px + 2 * py + pc) * m_per, m_per), :]

        def copy(k, block, to, src=None):
            return pltpu.make_async_remote_copy(src_ref=rows(*block) if src is None else src, dst_ref=rows(*block),
                                                send_sem=send_sems.at[k], recv_sem=recv_sems.at[k], device_id=to,
                                                device_id_type=MESH)

        mine = pltpu.make_async_copy(x_ref, rows(*me), local_sem)
        mine.start()
        first = [copy(0, me, sibling, src=x_ref)]
        first += [copy(1 + j, me, (*chip, c), src=x_ref) for j, chip in enumerate(chips)]
        for cp in first:
            cp.start()
        passed = [copy(4 + j, (*chip, c), sibling) for j, chip in enumerate(chips)]
        for j, chip in enumerate(chips):
            copy(1 + j, (*chip, c), me).wait_recv()
            passed[j].start()
        copy(0, sibling, me).wait_recv()
        for j, chip in enumerate(chips):
            copy(4 + j, (*chip, 1 - c), me).wait_recv()
        for cp in first + passed:
            cp.wait_send()
        mine.wait()
        acc = all_ref[0:m_per, :]
        for d in range(1, N_DEV):
            acc = acc + all_ref[d * m_per:(d + 1) * m_per, :]
        sum_ref[...] = acc

    vm = pl.BlockSpec(memory_space=pltpu.VMEM)
    return pl.pallas_call(
        body, name=name, in_specs=[vm], out_specs=vm, out_shape=jax.ShapeDtypeStruct((m_per, n), F32),
        scratch_shapes=[pltpu.VMEM((N_DEV * m_per, n), F32), pltpu.SemaphoreType.DMA((7,)),
                        pltpu.SemaphoreType.DMA((7,)), pltpu.SemaphoreType.DMA],
    )(v)


def _cols_to_slots(w):
    r, c4 = w.shape
    return w.reshape(r, N_CHIPS, c4 // N_CHIPS).transpose(1, 0, 2)


def _slots_to_cols(w):
    n, r, c = w.shape
    return w.transpose(1, 0, 2).reshape(r, n * c)


def _pad_cols(a, width):
    return jnp.pad(a, ((0, 0), (0, width - a.shape[1])))


class InLayout:
    def __init__(self, q_rank, kv_rank):
        gk = GLA_HEADS * GLA_DK
        gv = GLA_HEADS * GLA_DV
        sizes = [q_rank, kv_rank, MLA_ROPE, gk, gk, gv, GLA_GATE_RANK, gv]
        names = ["zq", "zkv", "zkr", "gq", "gk", "gv", "zg", "zr"]
        starts = np.concatenate([[0], np.cumsum(sizes)[:-1]])
        self.ref = {n: (int(s), int(z)) for n, s, z in zip(names, starts, sizes)}
        self.ref_width = int(sum(sizes))
        self.order = ["gv", "zr", "zq", "gq", "gk", "zkv", "zkr", "zg"]
        self.off, self.size = {}, {}
        pos = 0
        for n in self.order:
            padded = -(-self.ref[n][1] // LANE) * LANE
            self.off[n], self.size[n] = pos, padded
            pos += padded
        self.width = pos
        self.shard = self.ref_width // N_CHIPS
        self.shard_pad = -(-self.shard // LANE) * LANE

    def _pieces(self, lo, hi):
        out = []
        while lo < hi:
            s = lo // self.shard
            end = min(hi, (s + 1) * self.shard)
            out.append((s * self.shard_pad + lo - s * self.shard, s * self.shard_pad + end - s * self.shard))
            lo = end
        return out

    def from_shards(self, zs):
        cols = []
        for n in self.order:
            start, size = self.ref[n]
            cols += [zs[:, a:b] for a, b in self._pieces(start, start + size)]
            if self.size[n] > size:
                cols.append(jnp.zeros((zs.shape[0], self.size[n] - size), zs.dtype))
        return jnp.concatenate(cols, axis=1)

    def to_shards(self, dz):
        names = sorted(self.ref, key=lambda n: self.ref[n][0])
        ref = jnp.concatenate([dz[:, self.off[n]:self.off[n] + self.ref[n][1]] for n in names], axis=1)
        ref = ref.reshape(dz.shape[0], N_CHIPS, self.shard)
        return jnp.pad(ref, ((0, 0), (0, 0), (0, self.shard_pad - self.shard))).reshape(dz.shape[0], -1)


def _pad_q_up(w):
    r = w.shape[0]
    w = w.reshape(r, MLA_HEADS, MLA_QK)
    w = jnp.pad(w, ((0, 0), (0, 0), (0, MLA_HEAD_PAD - MLA_QK)))
    return w.reshape(r, MLA_HEADS * MLA_HEAD_PAD)


def _unpad_q_up(g):
    r = g.shape[0]
    return g.reshape(r, MLA_HEADS, MLA_HEAD_PAD)[:, :, :MLA_QK].reshape(r, MLA_HEADS * MLA_QK)


def _rope_tables(positions):
    half = MLA_ROPE // 2
    inv_freq = ROPE_THETA ** (-jnp.arange(half, dtype=F32) / half)
    ang = positions.astype(F32).reshape(-1, 1) * inv_freq
    cos, sin = jnp.cos(ang), jnp.sin(ang)
    s = ang.shape[0]
    cosf = jnp.concatenate([cos, cos, jnp.ones((s, LANE - MLA_ROPE), F32)], axis=1)
    sinf = jnp.concatenate([sin, sin, jnp.zeros((s, LANE - MLA_ROPE), F32)], axis=1)
    rot = np.zeros((LANE, LANE), np.float32)
    for j in range(half):
        rot[j + half, j] = -1.0
        rot[j, j + half] = 1.0
    return cosf, sinf, jnp.asarray(rot)


SMALL = ["ffn1_norm", "mix_norm", "q_a_norm", "kv_a_norm", "mla_q_norm", "mla_k_norm", "gla_b_gate", "gla_out_norm",
         "mem_attn_norm", "mem_norm", "mem_q_norm", "mem_k_norm", "ffn2_norm"]
BIG = ["ffn1_w_gate", "ffn1_w_up", "ffn1_w_down", "w_in", "w_q_up", "w_kv_up", "w_out", "mem_w_q", "mem_w_k",
       "mem_w_v", "mem_w_o", "ffn2_w_gate", "ffn2_w_up", "ffn2_w_down"]
COL_SHARDED = {"ffn1_w_gate", "ffn1_w_up", "w_in", "w_q_up", "w_kv_up", "gla_w_gate2", "mem_w_o", "ffn2_w_gate", "ffn2_w_up"}
WEIGHTS = ["ffn1_norm", "ffn1_w_gate", "ffn1_w_up", "ffn1_w_down", "mix_norm", "w_in", "q_a_norm", "w_q_up", "kv_a_norm",
           "w_kv_up", "mla_q_norm", "mla_k_norm", "gla_w_gate2", "gla_b_gate", "gla_out_norm", "w_out", "mem_attn_norm",
           "mem_norm", "mem_w_q", "mem_w_k", "mem_w_v", "mem_w_o", "mem_q_norm", "mem_k_norm", "ffn2_norm", "ffn2_w_gate",
           "ffn2_w_up", "ffn2_w_down"]


def _pack_small(vals, rows=8):
    flat = jnp.concatenate([v.reshape(-1).astype(F32) for v in vals])
    n = flat.shape[0]
    per = -(-n // (rows * LANE)) * LANE
    return jnp.pad(flat, (0, rows * per - n)).reshape(rows, per)


def _unpack_small(packed, shapes):
    flat = packed.reshape(-1)
    out, pos = [], 0
    for s in shapes:
        n = int(np.prod(s))
        out.append(flat[pos:pos + n].reshape(s))
        pos += n
    return out


FFN1 = ["ffn1_w_gate", "ffn1_w_up", "ffn1_w_down"]
FFN2 = ["ffn2_w_gate", "ffn2_w_up", "ffn2_w_down"]
SLOT_WEIGHTS = {"ffn1_w_gate", "ffn1_w_up", "ffn2_w_gate", "ffn2_w_up", "w_in"}
MID_A = ["w_in", "w_q_up", "w_kv_up", "gla_w_gate2"]
MID_B = ["w_out", "mem_w_q", "mem_w_k", "mem_w_v", "mem_w_o"]


def _with(res, comm):
    return res if comm is not None else (res, None)


def kernel(x, mem, positions, ffn1_norm, ffn1_w_gate, ffn1_w_up, ffn1_w_down, mix_norm, w_in, q_a_norm, w_q_up, kv_a_norm, w_kv_up, mla_q_norm, mla_k_norm, gla_w_gate2, gla_b_gate, gla_out_norm, w_out, mem_attn_norm, mem_norm, mem_w_q, mem_w_k, mem_w_v, mem_w_o, mem_q_norm, mem_k_norm, ffn2_norm, ffn2_w_gate, ffn2_w_up, ffn2_w_down, loss_target, m_ffn1_norm, m_ffn1_w_gate, m_ffn1_w_up, m_ffn1_w_down, m_mix_norm, m_w_in, m_q_a_norm, m_w_q_up, m_kv_a_norm, m_w_kv_up, m_mla_q_norm, m_mla_k_norm, m_gla_w_gate2, m_gla_b_gate, m_gla_out_norm, m_w_out, m_mem_attn_norm, m_mem_norm, m_mem_w_q, m_mem_w_k, m_mem_w_v, m_mem_w_o, m_mem_q_norm, m_mem_k_norm, m_ffn2_norm, m_ffn2_w_gate, m_ffn2_w_up, m_ffn2_w_down, v_ffn1_norm, v_ffn1_w_gate, v_ffn1_w_up, v_ffn1_w_down, v_mix_norm, v_w_in, v_q_a_norm, v_w_q_up, v_kv_a_norm, v_w_kv_up, v_mla_q_norm, v_mla_k_norm, v_gla_w_gate2, v_gla_b_gate, v_gla_out_norm, v_w_out, v_mem_attn_norm, v_mem_norm, v_mem_w_q, v_mem_w_k, v_mem_w_v, v_mem_w_o, v_mem_q_norm, v_mem_k_norm, v_ffn2_norm, v_ffn2_w_gate, v_ffn2_w_up, v_ffn2_w_down):
    args = dict(locals())
    two_d = lambda a: a[0] if a.ndim == 3 else a
    W = {n: two_d(args[n]) for n in WEIGHTS}
    M1 = {n: two_d(args["m_" + n]) for n in WEIGHTS}
    V2 = {n: two_d(args["v_" + n]) for n in WEIGHTS}
    xs, mems, tgt = x[0], mem[0], loss_target[0]
    S, D = xs.shape
    chip = 2 * lax.axis_index("x") + lax.axis_index("y")

    q_rank, kv_rank = W["w_q_up"].shape[0], W["w_kv_up"].shape[0]
    lay = InLayout(q_rank, kv_rank)
    off = lay.off
    up_names, down_names = FFN1[:2], FFN1[2:]
    shard16 = {n: W[n].astype(BF16) for n in up_names}
    full = {}

    def stage1(names):
        return gather_stage1([shard16[n] for n in names], [n != "gla_w_gate2" for n in names])

    def stage2(names, slots):
        return gather_stage2(slots, [shard16[n] for n in names], [n != "gla_w_gate2" for n in names])

    def finish(names, slots):
        for n, s in zip(names, slots):
            s = s.reshape((N_CHIPS,) + shard16[n].shape)
            if n in SLOT_WEIGHTS:
                full[n] = s
            else:
                full[n] = _slots_to_cols(s) if n in COL_SHARDED else s.reshape(-1, s.shape[2])

    token = {"last": None}

    def begin(comm, name, after=None):
        started = start_comm(comm, name=name, after=token["last"] if after is None else after)
        token["last"] = started[-1]
        return comm, started

    first = [begin(stage1([n]), f"gather_start_{n}") for n in up_names]
    zero = token["last"][0, 0]
    later = [n for n in BIG + ["gla_w_gate2"] if n not in shard16]
    for n in later:
        shard16[n] = (W[n] + zero).astype(BF16)
    shard16["w_in"] = _pad_cols(shard16["w_in"], lay.shard_pad)
    n1 = row_fwd(rms_fn, [V(xs)], [W["ffn1_norm"]], [(D, BF16)], [(0, 0, 0)], name="ffn1_norm")[0]
    cosf, sinf, rot = _rope_tables(positions[0])
    gate_s1 = wait_comm(*first[0], [n1, cosf, sinf] + [shard16[n] for n in later], name=f"gather_wait_{up_names[0]}")
    finish(up_names[:1], run_comm(stage2(up_names[:1], gate_s1), name="pass_ffn1_gate"))
    up_s1 = wait_comm(*first[1], full[up_names[0]], name=f"gather_wait_{up_names[1]}")
    down1 = begin(stage1(down_names), "gather_start_ffn1_down", after=up_s1[0])
    mid_a_s1 = begin(stage1(MID_A), "gather_start_mid_a")
    finish(up_names[1:], run_comm(stage2(up_names[1:], up_s1), name="pass_ffn1_up"))
    tri = jnp.asarray(np.tril(np.ones((CHUNK, CHUNK), np.float32)))
    gqn = W["mla_q_norm"][:, :MLA_NOPE]
    gqr = _pad_cols(W["mla_q_norm"][:, MLA_NOPE:], LANE)
    gkn = W["mla_k_norm"][:, :MLA_NOPE]
    gkr = _pad_cols(W["mla_k_norm"][:, MLA_NOPE:], LANE)
    HP = MLA_HEAD_PAD
    mla_scale = MLA_QK ** -0.5
    mem_scale = MEM_HEAD_DIM ** -0.5
    mla_w = MLA_HEADS * MLA_V
    gla_w = GLA_HEADS * GLA_DV
    mem_w = MEM_HEADS * MEM_HEAD_DIM

    gate1, up1, act1 = ffn_up(n1, full["ffn1_w_gate"], full["ffn1_w_up"], name="ffn1_up", behind=token["last"])
    finish(down_names, run_comm(stage2(down_names, wait_comm(*down1, act1, name="gather_wait_ffn1_down")),
                                name="pass_ffn1_down"))
    mid_a1 = wait_comm(*mid_a_s1, act1, name="gather_wait_mid_a")
    mid_b = begin(stage1(MID_B), "gather_start_mid_b", after=mid_a1[0])
    x1, got = mm([(act1, full["ffn1_w_down"])], "nn", F32, alpha=0.5, res=xs, name="ffn1_down",
                 comm=stage2(MID_A, mid_a1), behind=token["last"])
    ffn1_saved = (n1, gate1, up1, act1)
    finish(MID_A, got)
    ffn2_s1 = [begin(stage1([n]), f"gather_start_{n}", after=x1 if n == FFN2[0] else None) for n in FFN2]
    w_q_up_p = _pad_q_up(full["w_q_up"])
    w_gate2_p = jnp.pad(full["gla_w_gate2"], ((0, LANE - GLA_GATE_RANK), (0, 0)))
    h = row_fwd(rms_fn, [V(x1)], [W["mix_norm"]], [(D, BF16)], [(0, 0, 0)], name="mix_norm")[0]
    mid_b1 = wait_comm(*mid_b, h, name="gather_wait_mid_b")
    z_shards, got = mm([(h, full["w_in"])], "nn", F32, name="in_proj", b_slots=True, comm=stage2(MID_B, mid_b1),
                       behind=token["last"])
    z = lay.from_shards(z_shards)
    finish(MID_B, got)
    qa = row_fwd(rms_fn, [V(z, off["zq"], q_rank)], [W["q_a_norm"]], [(q_rank, BF16)], [(0, 0, 0)], name="q_a_norm")[0]
    kva = row_fwd(rms_fn, [V(z, off["zkv"], kv_rank)], [W["kv_a_norm"]], [(kv_rank, BF16)], [(0, 0, 0)], name="kv_a_norm")[0]
    qraw = mm([(qa, w_q_up_p)], "nn", F32, name="q_up")
    kvraw = mm([(kva, full["w_kv_up"])], "nn", F32, name="kv_up")
    tabs = [V(cosf, diff=False), V(sinf, diff=False)]
    q_rows = [V(qraw, 0, LANE, HP), V(qraw, LANE, LANE, HP)] + tabs
    k_rows = [V(kvraw, 0, LANE, HP), V(z, off["zkr"], LANE, 0)] + tabs
    qh = row_fwd(qk_prep_fn, q_rows, [gqn, gqr, rot], [(MLA_HEADS * HP, BF16)], [(0, 0, HP), (0, LANE, HP)],
                 heads=MLA_HEADS, name="q_prep")[0]
    kh = row_fwd(qk_prep_fn, k_rows, [gkn, gkr, rot], [(MLA_HEADS * HP, BF16)], [(0, 0, HP), (0, LANE, HP)],
                 heads=MLA_HEADS, name="k_prep")[0]
    mla_kw = dict(heads=MLA_HEADS, dk=HP, dv=MLA_V, v_off=1, v_hs=2, scale=mla_scale, causal=True, tq=512)
    o_mla = attn_fwd(qh, kh, kvraw, name="mla_attn", **mla_kw)

    zg = z[:, off["zg"]:off["zg"] + LANE]
    pre = mm([(zg, w_gate2_p)], "nn", F32, name="gla_gate")
    la = row_fwd(gate_fn, [V(pre)], [W["gla_b_gate"]], [(pre.shape[1], F32)], [(0, 0, 0)], name="gla_log_decay")[0]
    gla_kw = dict(q_off=off["gq"], k_off=off["gk"], v_off=off["gv"])
    f2_gate = wait_comm(*ffn2_s1[0], la, name=f"gather_wait_{FFN2[0]}")
    (o_raw, states), got = gla_fwd(z, la, tri, name="gla_scan", comm=stage2(FFN2[:1], f2_gate), **gla_kw)
    finish(FFN2[:1], got)
    gla_rows = [V(o_raw, 0, GLA_DV, GLA_DV), V(z, off["zr"], GLA_DV, GLA_DV)]
    o_gla = row_fwd(gla_out_fn, gla_rows, [W["gla_out_norm"]], [(gla_w, BF16)], [(0, 0, GLA_DV)], heads=GLA_HEADS,
                    name="gla_out")[0]
    o_cat = jnp.concatenate([o_mla, o_gla], axis=1)
    f2_up = wait_comm(*ffn2_s1[1], o_cat, name=f"gather_wait_{FFN2[1]}")
    x2, got = mm([(o_cat, full["w_out"])], "nn", F32, res=x1, name="out_proj", comm=stage2(FFN2[1:2], f2_up))
    finish(FFN2[1:2], got)

    hm = row_fwd(rms_fn, [V(x2)], [W["mem_attn_norm"]], [(D, BF16)], [(0, 0, 0)], name="mem_attn_norm")[0]
    mn = row_fwd(rms_fn, [V(mems)], [W["mem_norm"]], [(D, BF16)], [(0, 0, 0)], name="mem_norm")[0]
    qm_raw = mm([(hm, full["mem_w_q"])], "nn", F32, name="mem_q")
    km_raw = mm([(mn, full["mem_w_k"])], "nn", F32, name="mem_k")
    vm = mm([(mn, full["mem_w_v"])], "nn", F32, name="mem_v")
    hd = MEM_HEAD_DIM
    qm = row_fwd(rms_fn, [V(qm_raw, 0, hd, hd)], [W["mem_q_norm"]], [(mem_w, BF16)], [(0, 0, hd)], heads=MEM_HEADS,
                 name="mem_q_norm")[0]
    km = row_fwd(rms_fn, [V(km_raw, 0, hd, hd)], [W["mem_k_norm"]], [(mem_w, BF16)], [(0, 0, hd)], heads=MEM_HEADS,
                 name="mem_k_norm")[0]
    mem_kw = dict(heads=MEM_HEADS, dk=hd, dv=hd, v_off=0, v_hs=1, scale=mem_scale, causal=False, tq=1024)
    om = attn_fwd(qm, km, vm, name="mem_attn", **mem_kw)
    x3 = mm([(om, full["mem_w_o"])], "nn", F32, res=x2, name="mem_o")

    n2 = row_fwd(rms_fn, [V(x3)], [W["ffn2_norm"]], [(D, BF16)], [(0, 0, 0)], name="ffn2_norm")[0]
    f2_down = wait_comm(*ffn2_s1[2], n2, name=f"gather_wait_{FFN2[2]}")
    (gate2, up2, act2), got = ffn_up(n2, full["ffn2_w_gate"], full["ffn2_w_up"], name="ffn2_up",
                                     comm=stage2(FFN2[2:], f2_down))
    finish(FFN2[2:], got)
    y = mm([(act2, full["ffn2_w_down"])], "nn", F32, alpha=0.5, res=x3, name="ffn2_down")
    dy, loss_part = loss_head(y, tgt, name="loss_head")
    G = {"loss": loss_part[:, :1]}

    chip_sum, reduced = {}, {}

    def to_halves(n):
        g = G[n]
        if n in SLOT_WEIGHTS:
            s = g
        else:
            s = _cols_to_slots(g) if n in COL_SHARDED else g.reshape(N_CHIPS, g.shape[0] // N_CHIPS, g.shape[1])
        return s.reshape(N_CHIPS, 2, s.shape[1] // 2, s.shape[2])

    def add2(names, halves, got):
        for n, a, b in zip(names, halves, got):
            chip_sum[n] = add_own_half(a, b, BF16, name=f"rs_add2_{n}")

    to_join = []

    def add4_join(names, parts):
        for n, p in zip(names, parts):
            to_join.append((n, sum_chip_parts(chip_sum[n], p, name=f"rs_add4_{n}")))

    def with_joins(comm):
        names, totals = [n for n, _ in to_join], [t for _, t in to_join]
        to_join.clear()
        if not names:
            return comm, lambda got: got
        own = 0 if comm is None else len(comm.out_shapes)
        joined = join_halves(totals)

        def split(got):
            for n, b in zip(names, got[own:]):
                reduced[n] = b.reshape(-1, b.shape[2])[:, :W[n].shape[1]]
            return got[:own]

        return (joined if comm is None else merge_comms(comm, joined)), split

    def flush_joins():
        comm, split = with_joins(None)
        if comm is not None:
            split(run_comm(comm, name=f"rs_join_{len(reduced)}"))

    in_flight = []

    def xchg_start(names):
        in_flight.append((names,) + begin(exchange_chips([chip_sum[n] for n in names]), f"xchg_start_{names[0]}"))

    def xchg_wait(after, count=1):
        for _ in range(count):
            names, comm, started = in_flight.pop(0)
            add4_join(names, wait_comm(comm, started, after, name=f"xchg_wait_{names[0]}"))

    def ffn_backward(dout, xin, tag, saved, dact_comm=None, after_dact=None):
        n_, gate, up, act = saved
        nd, ng, nu = f"{tag}_w_down", f"{tag}_w_gate", f"{tag}_w_up"
        (dgate, dup), got0 = _with(ffn_dact(dout, full[nd], gate, up, 0.5, name=f"{tag}_dact", comm=dact_comm,
                                            behind=token["last"]), dact_comm)
        if after_dact:
            after_dact(got0)
        G[nd] = mm([(act, dout)], "tn", F32, alpha=0.5, name=f"{tag}_dwd", tm=1408, tn=1024, behind=token["last"])
        hd_ = to_halves(nd)
        comm, split = with_joins(swap_halves([hd_]))
        G[ng], got = mm([(n_, dgate)], "tn", F32, name=f"{tag}_dwg", out_slots=True, tm=1024, tn=1408, rows_inner=True,
                        comm=comm)
        add2([nd], [hd_], split(got))
        xchg_start([nd])
        hg = to_halves(ng)
        G[nu], got_g = mm([(n_, dup)], "tn", F32, name=f"{tag}_dwu", out_slots=True, tm=1024, tn=1408, rows_inner=True,
                          comm=swap_halves([hg]), behind=token["last"])
        add2([ng], [hg], got_g)
        xchg_start([ng])
        hu = to_halves(nu)
        dn, got_u = mm([(dgate, full[ng]), (dup, full[nu])], "nt", F32, name=f"{tag}_dn", b_slots=True, tn=1024, tk=1408,
                       comm=swap_halves([hu]), behind=token["last"])
        add2([nu], [hu], got_u)
        xchg_start([nu])
        dx, G[f"{tag}_norm"] = row_bwd(rms_fn, [V(xin)], [W[f"{tag}_norm"]], [V(dn)], const_diff=[True], res=dout,
                                       name=f"{tag}_dnorm")
        return dx

    g3 = ffn_backward(dy, x3, "ffn2", (n2, gate2, up2, act2))
    xchg_wait(g3)

    d_om = mm([(g3, full["mem_w_o"])], "nt", F32, name="mem_o_dx", behind=token["last"])
    G["mem_w_o"] = mm([(om, g3)], "tn", F32, name="mem_o_dw")
    dqm, dkm, dvm = attn_bwd(qm, km, vm, d_om, name="mem_attn_bwd", **mem_kw)
    dqm_raw, G["mem_q_norm"] = row_bwd(rms_fn, [V(qm_raw, 0, hd, hd)], [W["mem_q_norm"]], [V(dqm, 0, hd, hd)],
                                       const_diff=[True], heads=MEM_HEADS, row_dtype=BF16, name="mem_q_norm_bwd")
    dkm_raw, G["mem_k_norm"] = row_bwd(rms_fn, [V(km_raw, 0, hd, hd)], [W["mem_k_norm"]], [V(dkm, 0, hd, hd)],
                                       const_diff=[True], heads=MEM_HEADS, row_dtype=BF16, name="mem_k_norm_bwd")
    dhm = mm([(dqm_raw, full["mem_w_q"])], "nt", F32, name="mem_q_dx")
    G["mem_w_q"] = mm([(hm, dqm_raw)], "tn", F32, name="mem_q_dw")
    dmn = mm([(dkm_raw, full["mem_w_k"]), (dvm, full["mem_w_v"])], "nt", F32, name="mem_kv_dx")
    G["mem_w_k"] = mm([(mn, dkm_raw)], "tn", F32, name="mem_k_dw")
    G["mem_w_v"] = mm([(mn, dvm)], "tn", F32, name="mem_v_dw")
    _, G["mem_norm"] = row_bwd(rms_fn, [V(mems)], [W["mem_norm"]], [V(dmn)], const_diff=[True], row_dtype=BF16,
                               name="mem_norm_bwd")
    g2, G["mem_attn_norm"] = row_bwd(rms_fn, [V(x2)], [W["mem_attn_norm"]], [V(dhm)], const_diff=[True], res=g3,
                                     name="mem_attn_norm_bwd")

    xchg_wait(g2, 2)

    d_ocat = mm([(g2, full["w_out"])], "nt", F32, name="out_proj_dx")
    G["w_out"] = mm([(o_cat, g2)], "tn", F32, name="out_proj_dw")

    d_oraw, d_zr, G["gla_out_norm"] = row_bwd(gla_out_fn, gla_rows, [W["gla_out_norm"]],
                                              [V(d_ocat, mla_w, GLA_DV, GLA_DV)], const_diff=[True], heads=GLA_HEADS,
                                              name="gla_out_bwd")
    mid_b_halves = [to_halves(n) for n in MID_B]
    comm, split = with_joins(swap_halves(mid_b_halves))
    (d_gq, d_gk, d_gv, d_la), got = gla_bwd(z, la, tri, tri.T, states, d_oraw, name="gla_scan_bwd", comm=comm, **gla_kw)
    add2(MID_B, mid_b_halves, split(got))
    xchg_start(MID_B)
    d_pre, G["gla_b_gate"] = row_bwd(gate_fn, [V(pre)], [W["gla_b_gate"]], [V(d_la)], const_diff=[True], row_dtype=BF16,
                                     name="gla_log_decay_bwd")
    d_zg = mm([(d_pre, w_gate2_p)], "nt", BF16, name="gla_gate_dx", behind=token["last"])
    G["gla_w_gate2"] = mm([(zg, d_pre)], "tn", F32, name="gla_gate_dw")[:GLA_GATE_RANK]

    comm, split = with_joins(None)
    (d_qh, d_kh, d_v), got = _with(attn_bwd(qh, kh, kvraw, d_ocat, name="mla_attn_bwd", comm=comm, **mla_kw), comm)
    split(got)
    cq = [V(d_qh, 0, LANE, HP), V(d_qh, LANE, LANE, HP)]
    ck = [V(d_kh, 0, LANE, HP), V(d_kh, LANE, LANE, HP)]
    d_qraw, d_gqn, d_gqr = row_bwd(qk_prep_fn, q_rows, [gqn, gqr, rot], cq, const_diff=[True, True, False],
                                   heads=MLA_HEADS, row_dtype=BF16, pack={0: (0, HP), 1: (LANE, HP)},
                                   pack_width=MLA_HEADS * HP, name="q_prep_bwd")
    d_kvraw, d_zkr, d_gkn, d_gkr = row_bwd(qk_prep_fn, k_rows, [gkn, gkr, rot], ck, const_diff=[True, True, False],
                                           heads=MLA_HEADS, row_dtype=BF16, pack={0: (0, HP)}, pack_width=MLA_HEADS * HP,
                                           fills=[(V(d_v, 0, MLA_V, MLA_V), LANE, HP)], name="k_prep_bwd")
    G["mla_q_norm"] = jnp.concatenate([d_gqn, d_gqr[:, :MLA_ROPE]], axis=1)
    G["mla_k_norm"] = jnp.concatenate([d_gkn, d_gkr[:, :MLA_ROPE]], axis=1)
    d_qa = mm([(d_qraw, w_q_up_p)], "nt", F32, name="q_up_dx")
    G["w_q_up"] = _unpad_q_up(mm([(qa, d_qraw)], "tn", F32, name="q_up_dw"))
    d_kva = mm([(d_kvraw, full["w_kv_up"])], "nt", F32, name="kv_up_dx")
    G["w_kv_up"] = mm([(kva, d_kvraw)], "tn", F32, name="kv_up_dw")
    d_zq, G["q_a_norm"] = row_bwd(rms_fn, [V(z, off["zq"], q_rank)], [W["q_a_norm"]], [V(d_qa)], const_diff=[True],
                                  row_dtype=BF16, name="q_a_norm_bwd")
    d_zkv, G["kv_a_norm"] = row_bwd(rms_fn, [V(z, off["zkv"], kv_rank)], [W["kv_a_norm"]], [V(d_kva)], const_diff=[True],
                                    row_dtype=BF16, name="kv_a_norm_bwd")

    seg = {"gv": d_gv, "zr": d_zr, "zq": d_zq, "gq": d_gq, "gk": d_gk, "zkv": d_zkv, "zkr": d_zkr, "zg": d_zg}
    dz = jnp.concatenate([_pad_cols(seg[n].astype(BF16), lay.size[n]) for n in lay.order], axis=1)
    xchg_wait(dz)
    comm, split = with_joins(None)
    dz_shards = lay.to_shards(dz)
    dh, got = _with(mm([(dz_shards, full["w_in"])], "nt", F32, name="in_proj_dx", b_slots=True, comm=comm), comm)
    split(got)
    G["w_in"] = mm([(h, dz_shards)], "tn", F32, name="in_proj_dw", out_slots=True)
    g1, G["mix_norm"] = row_bwd(rms_fn, [V(x1)], [W["mix_norm"]], [V(dh)], const_diff=[True], res=g2,
                                name="mix_norm_bwd")

    mid_a = [n for n in MID_A if n != "gla_w_gate2"]
    mid_a_halves = [to_halves(n) for n in mid_a]

    def mid_a_sums(got):
        add2(mid_a, mid_a_halves, got)
        xchg_start(mid_a)

    gx = ffn_backward(g1, xs, "ffn1", ffn1_saved, dact_comm=swap_halves(mid_a_halves), after_dact=mid_a_sums)
    xchg_wait(gx, 2)

    grad, delta, new_m, new_v = {}, {}, {}, {}

    def adam_group(names, tag, behind=None):
        if any(n not in reduced for n in names):
            flush_joins()
        res = adamw([(W[n], reduced[n], M1[n], V2[n]) for n in names], name=f"adamw_{tag}", behind=behind)
        for n, (g_, d_, m_, v_) in zip(names, res):
            grad[n], delta[n], new_m[n], new_v[n] = g_, d_, m_, v_

    adam_group(FFN2, "ffn2", behind=token["last"])
    adam_group(mid_a + MID_B, "mid", behind=token["last"])
    adam_group(FFN1[2:], "ffn1_down", behind=token["last"])
    xchg_wait(delta[FFN1[2]], 2)
    adam_group(FFN1[:2], "ffn1_up")

    small_names = SMALL + ["gla_w_gate2"]
    packed = small_names + ["loss"]
    small_sum = allreduce_small(_pack_small([G[n] for n in packed]), name="allreduce_small")
    small_g = dict(zip(packed, _unpack_small(small_sum, [G[n].shape for n in packed])))
    loss = small_g["loss"][0, 0]
    shard_c = W["gla_w_gate2"].shape[1]
    grad["gla_w_gate2"] = lax.dynamic_slice_in_dim(small_g["gla_w_gate2"], chip * shard_c, shard_c, axis=1)
    pw = _pack_small([W[n] for n in SMALL] + [W["gla_w_gate2"]])
    pg = _pack_small([small_g[n] for n in SMALL] + [grad["gla_w_gate2"]])
    pm = _pack_small([M1[n] for n in SMALL] + [M1["gla_w_gate2"]])
    pv = _pack_small([V2[n] for n in SMALL] + [V2["gla_w_gate2"]])
    (_, pd, pnm, pnv), = adamw([(pw, pg, pm, pv)], name="adamw_small")
    shapes = [W[n].shape for n in small_names]
    for n, d_, m_, v_ in zip(small_names, _unpack_small(pd, shapes), _unpack_small(pnm, shapes), _unpack_small(pnv, shapes)):
        delta[n], new_m[n], new_v[n] = d_, m_, v_
        if n != "gla_w_gate2":
            grad[n] = small_g[n]

    lead = lambda d: [d[n].reshape(args[n].shape) for n in WEIGHTS]
    return (loss, gx[None], *lead(grad), *lead(delta), *lead(new_m), *lead(new_v))
```

```python
import functools

import numpy as np
import jax
import jax.numpy as jnp
from jax import lax
from jax.experimental import pallas as pl
from jax.experimental.pallas import tpu as pltpu

F32 = jnp.float32
BF16 = jnp.bfloat16
MXU_DTYPE = jnp.bfloat16
MESH = pl.DeviceIdType.MESH
ANY = pl.BlockSpec(memory_space=pl.ANY)

LANE = 128
EPS = 1e-6
CHUNK = 64
MLA_HEADS = 8
MLA_NOPE = 128
MLA_ROPE = 64
MLA_QK = MLA_NOPE + MLA_ROPE
MLA_V = 128
MLA_HEAD_PAD = 2 * LANE
ROPE_THETA = 10000.0
GLA_HEADS = 4
GLA_DK = 128
GLA_DV = 256
GLA_GATE_RANK = 16
GLA_TAU = 16.0
MEM_HEADS = 4
MEM_HEAD_DIM = 128
N_CHIPS = 4
N_DEV = 8

ADAM_LR = 0.001
ADAM_B1 = 0.9
ADAM_B2 = 0.999
ADAM_EPS = 1e-08
ADAM_WD = 0.01
ADAM_STEP = 10

VMEM_LIMIT = 56 * 1024 * 1024


def _cparams(sem=None):
    if sem is None:
        return pltpu.CompilerParams(vmem_limit_bytes=VMEM_LIMIT)
    return pltpu.CompilerParams(dimension_semantics=sem, vmem_limit_bytes=VMEM_LIMIT)


def _tile(dim, pref, unit=LANE):
    if dim <= pref:
        return dim
    t = (pref // unit) * unit
    while t > unit and dim % t:
        t -= unit
    assert dim % t == 0, (dim, pref, unit)
    return t


class Comm:
    def __init__(self, ins, out_shapes, nsem, start, wait, aliases=None):
        self.ins, self.out_shapes, self.nsem = list(ins), list(out_shapes), nsem
        self.start, self.wait, self.aliases = start, wait, dict(aliases or {})


def merge_comms(a, b):
    ai, ao = len(a.ins), len(a.out_shapes)

    def start(ins, outs, send, recv, base):
        a.start(ins[:ai], outs[:ao], send, recv, base)
        b.start(ins[ai:], outs[ao:], send, recv, base + a.nsem)

    def wait(ins, outs, send, recv, base):
        a.wait(ins[:ai], outs[:ao], send, recv, base)
        b.wait(ins[ai:], outs[ao:], send, recv, base + a.nsem)

    aliases = dict(a.aliases)
    aliases.update({ai + i: ao + o for i, o in b.aliases.items()})
    return Comm(a.ins + b.ins, a.out_shapes + b.out_shapes, a.nsem + b.nsem, start, wait, aliases)


def run_comm(comm, *, name):
    ni, no = len(comm.ins), len(comm.out_shapes)

    def body(*refs):
        ins, outs = refs[:ni], refs[ni:ni + no]
        send, recv = refs[ni + no:]
        comm.start(ins, outs, send, recv, 0)
        comm.wait(ins, outs, send, recv, 0)

    return pl.pallas_call(
        body, name=name, in_specs=[ANY] * ni, out_specs=[ANY] * no, out_shape=comm.out_shapes,
        input_output_aliases=comm.aliases,
        scratch_shapes=[pltpu.SemaphoreType.DMA((comm.nsem,)), pltpu.SemaphoreType.DMA((comm.nsem,))])(*comm.ins)


HBM = pl.BlockSpec(memory_space=pltpu.HBM)
SEM = pl.BlockSpec(memory_space=pltpu.SEMAPHORE)


def start_comm(comm, *, name, after=None):
    assert not comm.aliases
    ni, no = len(comm.ins), len(comm.out_shapes)
    tail = [] if after is None else [after]

    def body(*refs):
        srcs, lands = refs[:ni], refs[ni:ni + no]
        send, recv = refs[ni + no + len(tail)], refs[ni + no + len(tail) + 1]
        token = refs[-1]
        comm.start(srcs, lands, send, recv, 0)
        token[...] = jnp.zeros_like(token)

    through = [pltpu.HBM(a.shape, a.dtype) for a in comm.ins] + [pltpu.HBM(s.shape, s.dtype) for s in comm.out_shapes]
    ops = [pltpu.with_memory_space_constraint(a, pltpu.HBM) for a in comm.ins]
    ops += [pltpu.with_memory_space_constraint(lax.empty(s.shape, s.dtype), pltpu.HBM) for s in comm.out_shapes]
    ops += tail
    res = pl.pallas_call(
        body, name=name, in_specs=[HBM] * (ni + no) + [ANY] * len(tail),
        out_shape=[pltpu.SemaphoreType.DMA((comm.nsem,)), pltpu.SemaphoreType.DMA((comm.nsem,))] + through
        + [jax.ShapeDtypeStruct((8, LANE), F32)],
        out_specs=[SEM, SEM] + [HBM] * (ni + no) + [pl.BlockSpec(memory_space=pltpu.VMEM)],
        input_output_aliases={i: 2 + i for i in range(ni + no)},
        compiler_params=pltpu.CompilerParams(has_side_effects=pltpu.SideEffectType.DATAFLOW_SIDE_EFFECTING))(*ops)
    return res[0], res[1], list(res[2:2 + ni]), list(res[2 + ni:2 + ni + no]), res[-1]


def wait_comm(comm, started, after, *, name):
    send, recv, srcs, lands, _ = started
    ni, no = len(srcs), len(lands)
    after = list(after) if isinstance(after, (list, tuple)) else [after]

    def body(*refs):
        comm.wait(refs[:ni], refs[ni:ni + no], refs[ni + no], refs[ni + no + 1], 0)

    res = pl.pallas_call(
        body, name=name, in_specs=[HBM] * (ni + no) + [SEM, SEM] + [ANY] * len(after),
        out_shape=[pltpu.HBM(a.shape, a.dtype) for a in srcs + lands], out_specs=[HBM] * (ni + no),
        input_output_aliases={i: i for i in range(ni + no)},
        compiler_params=pltpu.CompilerParams(has_side_effects=pltpu.SideEffectType.DATAFLOW_SIDE_EFFECTING),
    )(*srcs, *lands, send, recv, *after)
    return list(res[ni:])


def _pcall(body, ops, *, name, grid, in_specs, out_specs, out_shape, sem, scratch_shapes=(), comm=None, behind=None):
    if behind is not None:
        n_real, inner = len(ops), body
        ops, in_specs = list(ops) + [behind], list(in_specs) + [ANY]

        def body(*refs):
            inner(*refs[:n_real], *refs[n_real + 1:])

    if comm is None:
        return pl.pallas_call(body, name=name, grid=grid, in_specs=in_specs, out_specs=out_specs, out_shape=out_shape,
                              scratch_shapes=list(scratch_shapes), compiler_params=_cparams(sem))(*ops)
    multi = isinstance(out_shape, (list, tuple))
    k_out_shape = list(out_shape) if multi else [out_shape]
    k_out_specs = list(out_specs) if multi else [out_specs]
    nki, nko, nks = len(ops), len(k_out_shape), len(scratch_shapes)
    nci, nco = len(comm.ins), len(comm.out_shapes)

    def wrapped(*refs):
        p = 0
        k_in = refs[p:p + nki]; p += nki
        c_in = refs[p:p + nci]; p += nci
        k_out = refs[p:p + nko]; p += nko
        c_out = refs[p:p + nco]; p += nco
        k_scr = refs[p:p + nks]; p += nks
        send, recv = refs[p:]
        first = pl.program_id(0) == 0
        last = pl.program_id(0) == grid[0] - 1
        for a in range(1, len(grid)):
            first = jnp.logical_and(first, pl.program_id(a) == 0)
            last = jnp.logical_and(last, pl.program_id(a) == grid[a] - 1)

        @pl.when(first)
        def _():
            comm.start(c_in, c_out, send, recv, 0)

        body(*k_in, *k_out, *k_scr)

        @pl.when(last)
        def _():
            comm.wait(c_in, c_out, send, recv, 0)

    res = pl.pallas_call(
        wrapped, name=name, grid=grid, in_specs=list(in_specs) + [ANY] * nci, out_specs=k_out_specs + [ANY] * nco,
        out_shape=k_out_shape + comm.out_shapes,
        input_output_aliases={nki + i: nko + o for i, o in comm.aliases.items()},
        scratch_shapes=list(scratch_shapes) + [pltpu.SemaphoreType.DMA((comm.nsem,)), pltpu.SemaphoreType.DMA((comm.nsem,))],
        compiler_params=_cparams(("arbitrary",) * len(grid)))(*ops, *comm.ins)
    k_res = list(res[:nko]) if multi else res[0]
    return k_res, list(res[nko:])


_DIMS = {"nn": (((1,), (0,)), ((), ())), "nt": (((1,), (1,)), ((), ())), "tn": (((0,), (0,)), ((), ()))}


def _blockspec(shape, index, rows_inner):
    return pl.BlockSpec(shape, (lambda j, i, k: index(i, j, k)) if rows_inner else index)


def mm(pairs, mode, out_dtype, *, name, alpha=1.0, res=None, tm=1024, tn=1024, tk=4096, b_slots=False, out_slots=False,
       rows_inner=False, comm=None, behind=None):
    a0, b0 = pairs[0]
    if b_slots:
        b_rows, b_cols = b0.shape[1], N_CHIPS * b0.shape[2]
    else:
        b_rows, b_cols = b0.shape
    (M, K) = a0.shape[::-1] if mode == "tn" else a0.shape
    N = b_rows if mode == "nt" else b_cols
    shard = (b_cols if b_slots else N) // N_CHIPS
    tm = _tile(M, tm)
    tn = _tile(shard if (out_slots or (b_slots and mode != "nt")) else N, tn)
    tk = _tile(shard if (b_slots and mode == "nt") else K, tk)
    nk = K // tk
    npairs = len(pairs)
    dims = _DIMS[mode]
    spec = functools.partial(_blockspec, rows_inner=rows_inner)
    if mode == "tn":
        a_spec = spec((tk, tm), lambda i, j, k: (k, i))
    else:
        a_spec = spec((tm, tk), lambda i, j, k: (i, k))
    per = shard // (tk if mode == "nt" else tn)
    if mode == "nt":
        b_spec = (spec((None, tn, tk), lambda i, j, k: (k // per, j, k % per)) if b_slots else
                  spec((tn, tk), lambda i, j, k: (j, k)))
    else:
        b_spec = (spec((None, tk, tn), lambda i, j, k: (j // per, k, j % per)) if b_slots else
                  spec((tk, tn), lambda i, j, k: (k, j)))
    if out_slots:
        assert res is None and mode != "nt"
        o_spec = spec((None, tm, tn), lambda i, j, k: (j // per, i, j % per))
        out_sds = jax.ShapeDtypeStruct((N_CHIPS, M, shard), out_dtype)
    else:
        o_spec = spec((tm, tn), lambda i, j, k: (i, j))
        out_sds = jax.ShapeDtypeStruct((M, N), out_dtype)
    has_res = res is not None

    def body(*refs):
        ab = refs[:2 * npairs]
        res_ref = refs[2 * npairs] if has_res else None
        o_ref = refs[2 * npairs + int(has_res)]

        def products():
            r = None
            for p in range(npairs):
                d = lax.dot_general(ab[2 * p][...].astype(MXU_DTYPE), ab[2 * p + 1][...].astype(MXU_DTYPE), dims,
                                    preferred_element_type=F32)
                r = d if r is None else r + d
            return r

        def finish(r):
            if alpha != 1.0:
                r = r * alpha
            if has_res:
                r = res_ref[...].astype(F32) + r
            o_ref[...] = r.astype(out_dtype)

        if nk == 1:
            finish(products())
            return
        acc = refs[-1]
        k = pl.program_id(2)

        @pl.when(k == 0)
        def _():
            acc[...] = jnp.zeros_like(acc)

        acc[...] += products()

        @pl.when(k == nk - 1)
        def _():
            finish(acc[...])

    ops, specs = [], []
    for a, b in pairs:
        ops += [a, b]
        specs += [a_spec, b_spec]
    if has_res:
        ops.append(res)
        specs.append(o_spec)
    blocks = (N // tn, M // tm) if rows_inner else (M // tm, N // tn)
    return _pcall(body, ops, name=name, grid=blocks + (nk,), in_specs=specs, out_specs=o_spec, out_shape=out_sds,
                  scratch_shapes=[pltpu.VMEM((tm, tn), F32)] if nk > 1 else [],
                  sem=("parallel", "parallel", "arbitrary"), comm=comm, behind=behind)


def _sigmoid(x):
    return 1.0 / (1.0 + jnp.exp(-x))


def ffn_up(n, wg, wu, *, name, tm=512, tn=1408, comm=None, behind=None):
    M, K = n.shape
    shard = wg.shape[2]
    N = N_CHIPS * shard
    tm, tn = _tile(M, tm), _tile(shard, tn)
    per = shard // tn
    w_spec = pl.BlockSpec((None, K, tn), lambda j, i: (j // per, 0, j % per))

    def body(n_ref, wg_ref, wu_ref, g_ref, u_ref, a_ref):
        nv = n_ref[...].astype(MXU_DTYPE)
        g = jnp.dot(nv, wg_ref[...].astype(MXU_DTYPE), preferred_element_type=F32)
        u = jnp.dot(nv, wu_ref[...].astype(MXU_DTYPE), preferred_element_type=F32)
        g_ref[...] = g.astype(g_ref.dtype)
        u_ref[...] = u.astype(u_ref.dtype)
        a_ref[...] = (g * _sigmoid(g) * u).astype(a_ref.dtype)

    o_spec = pl.BlockSpec((tm, tn), lambda j, i: (i, j))
    sds = jax.ShapeDtypeStruct((M, N), BF16)
    return _pcall(
        body, [n, wg, wu], name=name, grid=(N // tn, M // tm),
        in_specs=[pl.BlockSpec((tm, K), lambda j, i: (i, 0)), w_spec, w_spec],
        out_specs=[o_spec, o_spec, o_spec], out_shape=[sds, sds, sds], sem=("parallel", "parallel"), comm=comm,
        behind=behind)


def ffn_dact(dy, wd, gate, up, alpha, *, name, tm=512, tn=1408, comm=None, behind=None):
    M, K = dy.shape
    N = wd.shape[0]
    tm, tn = _tile(M, tm), _tile(N, tn)

    def body(dy_ref, wd_ref, g_ref, u_ref, dg_ref, du_ref):
        da = lax.dot_general(dy_ref[...].astype(MXU_DTYPE), wd_ref[...].astype(MXU_DTYPE), _DIMS["nt"],
                             preferred_element_type=F32) * alpha
        g = g_ref[...].astype(F32)
        u = u_ref[...].astype(F32)
        s = _sigmoid(g)
        du_ref[...] = (da * (g * s)).astype(du_ref.dtype)
        dg_ref[...] = (da * u * (s * (1.0 + g * (1.0 - s)))).astype(dg_ref.dtype)

    o_spec = pl.BlockSpec((tm, tn), lambda j, i: (i, j))
    sds = jax.ShapeDtypeStruct((M, N), BF16)
    return _pcall(
        body, [dy, wd, gate, up], name=name, grid=(N // tn, M // tm),
        in_specs=[pl.BlockSpec((tm, K), lambda j, i: (i, 0)), pl.BlockSpec((tn, K), lambda j, i: (j, 0)), o_spec, o_spec],
        out_specs=[o_spec, o_spec], out_shape=[sds, sds], sem=("parallel", "parallel"), comm=comm, behind=behind)


def _window(width, off, ext):
    ww = LANE
    while ww < width:
        if ww >= ext and off // ww == (off + ext - 1) // ww and width % ww == 0:
            break
        ww *= 2
    else:
        ww = width
    return ww, off // ww, off - (off // ww) * ww


class V:
    def __init__(self, arr, off=0, w=None, hs=0, diff=True):
        self.arr, self.off, self.hs, self.diff = arr, off, hs, diff
        self.w = arr.shape[1] - off if w is None else w

    def window(self, heads, tr):
        ww, blk, inner = _window(self.arr.shape[1], self.off, (heads - 1) * self.hs + self.w)
        return pl.BlockSpec((tr, ww), lambda i, blk=blk: (i, blk)), inner


def _const_spec(c):
    return pl.BlockSpec(c.shape, lambda i: (0, 0))


def row_fwd(fn, rows, consts, outs, out_map, *, heads=1, tr=256, name):
    S = rows[0].arr.shape[0]
    tr = _tile(S, tr, 8)
    wins = [v.window(heads, tr) for v in rows]
    nr, nc = len(rows), len(consts)

    def body(*refs):
        row_refs, const_refs, out_refs = refs[:nr], refs[nr:nr + nc], refs[nr + nc:]
        cv = [c[...].astype(F32) for c in const_refs]
        for h in range(heads):
            rv = []
            for v, (_, io), r in zip(rows, wins, row_refs):
                lo = io + h * v.hs
                rv.append(r[:, lo:lo + v.w].astype(F32))
            res = fn(*rv, *cv)
            for (ai, off, hs), o in zip(out_map, res):
                lo = off + h * hs
                out_refs[ai][:, lo:lo + o.shape[1]] = o.astype(out_refs[ai].dtype)

    return pl.pallas_call(
        body, name=name, grid=(S // tr,),
        in_specs=[w[0] for w in wins] + [_const_spec(c) for c in consts],
        out_specs=[pl.BlockSpec((tr, w), lambda i: (i, 0)) for w, _ in outs],
        out_shape=[jax.ShapeDtypeStruct((S, w), d) for w, d in outs],
        compiler_params=_cparams(("parallel",)))(*[v.arr for v in rows], *consts)


def row_bwd(fn, rows, consts, cots, *, const_diff, heads=1, tr=256, res=None, row_dtype=F32, pack=None, pack_width=0,
            fills=(), name):
    S = rows[0].arr.shape[0]
    tr = _tile(S, tr, 8)
    pack = dict(pack or {})
    nr, nc, nct, nf = len(rows), len(consts), len(cots), len(fills)
    wins = [v.window(heads, tr) for v in rows]
    cwins = [v.window(heads, tr) for v in cots]
    fwins = [v.window(heads, tr) for v, _, _ in fills]
    drows = [k for k, v in enumerate(rows) if v.diff]
    dconsts = [k for k in range(nc) if const_diff[k]]
    has_res = res is not None
    assert not (has_res and 0 in pack)
    widths = [pack_width] if pack else []
    place = []
    for n, k in enumerate(drows):
        if n in pack:
            place.append((0,) + tuple(pack[n]))
        else:
            place.append((len(widths), 0, rows[k].w))
            widths.append(rows[k].w * (heads if rows[k].hs else 1))

    def body(*refs):
        row_refs = refs[:nr]
        const_refs = refs[nr:nr + nc]
        cot_refs = refs[nr + nc:nr + nc + nct]
        p = nr + nc + nct
        fill_refs = refs[p:p + nf]
        p += nf
        res_ref = refs[p] if has_res else None
        p += int(has_res)
        grow_refs = refs[p:p + len(widths)]
        gconst_refs = refs[p + len(widths):]
        i = pl.program_id(0)
        cv = [c[...].astype(F32) for c in const_refs]
        shared = [None] * len(drows)
        gc_sum = [None] * len(dconsts)
        for h in range(heads):
            rv = []
            for v, (_, io), r in zip(rows, wins, row_refs):
                lo = io + h * v.hs
                rv.append(r[:, lo:lo + v.w].astype(F32))
            ct = []
            for v, (_, io), r in zip(cots, cwins, cot_refs):
                lo = io + h * v.hs
                ct.append(r[:, lo:lo + v.w].astype(F32))

            def closed(*d):
                rr, cc = list(rv), list(cv)
                for k, val in zip(drows, d[:len(drows)]):
                    rr[k] = val
                for k, val in zip(dconsts, d[len(drows):]):
                    cc[k] = val
                return tuple(fn(*rr, *cc))

            _, vjp = jax.vjp(closed, *[rv[k] for k in drows], *[cv[k] for k in dconsts])
            grads = vjp(tuple(ct))
            for n, k in enumerate(drows):
                g = grads[n]
                if rows[k].hs == 0 and heads > 1:
                    shared[n] = g if shared[n] is None else shared[n] + g
                else:
                    if n == 0 and has_res:
                        g = g + res_ref[:, h * rows[k].w:(h + 1) * rows[k].w].astype(F32)
                    out, off, hs = place[n]
                    grow_refs[out][:, off + h * hs:off + h * hs + rows[k].w] = g.astype(row_dtype)
            for (v, off, hs), (_, io), r in zip(fills, fwins, fill_refs):
                lo = io + h * v.hs
                grow_refs[0][:, off + h * hs:off + h * hs + v.w] = r[:, lo:lo + v.w].astype(row_dtype)
            for n in range(len(dconsts)):
                g = grads[len(drows) + n]
                gc_sum[n] = g if gc_sum[n] is None else gc_sum[n] + g
        for n, k in enumerate(drows):
            if shared[n] is not None:
                g = shared[n]
                if n == 0 and has_res:
                    g = g + res_ref[...].astype(F32)
                grow_refs[place[n][0]][...] = g.astype(row_dtype)

        @pl.when(i == 0)
        def _():
            for n in range(len(dconsts)):
                gconst_refs[n][...] = gc_sum[n]

        @pl.when(i > 0)
        def _():
            for n in range(len(dconsts)):
                gconst_refs[n][...] += gc_sum[n]

    in_specs = [w[0] for w in wins] + [_const_spec(c) for c in consts] + [w[0] for w in cwins] + [w[0] for w in fwins]
    ops = [v.arr for v in rows] + list(consts) + [v.arr for v in cots] + [v.arr for v, _, _ in fills]
    if has_res:
        in_specs.append(pl.BlockSpec((tr, widths[0]), lambda i: (i, 0)))
        ops.append(res)
    out_specs = [pl.BlockSpec((tr, w), lambda i: (i, 0)) for w in widths]
    out_shape = [jax.ShapeDtypeStruct((S, w), row_dtype) for w in widths]
    for k in dconsts:
        out_specs.append(_const_spec(consts[k]))
        out_shape.append(jax.ShapeDtypeStruct(consts[k].shape, F32))
    return pl.pallas_call(body, name=name, grid=(S // tr,), in_specs=in_specs, out_specs=out_specs,
                          out_shape=out_shape, compiler_params=_cparams(("arbitrary",)))(*ops)


def _rms(x, g, n=None):
    n = x.shape[-1] if n is None else n
    ms = jnp.sum(x * x, axis=-1, keepdims=True) * (1.0 / n)
    return x * lax.rsqrt(ms + EPS) * g


def rms_fn(x, g):
    return (_rms(x, g),)


def qk_prep_fn(nope, rope, cos, sin, gn, gr, rot):
    ms = (jnp.sum(nope * nope, axis=-1, keepdims=True) + jnp.sum(rope * rope, axis=-1, keepdims=True)) * (1.0 / MLA_QK)
    r = lax.rsqrt(ms + EPS)
    on = nope * r * gn
    orr = rope * r * gr
    turned = jnp.dot(orr, rot, precision=lax.Precision.HIGHEST, preferred_element_type=F32)
    return on, orr * cos + turned * sin


def gla_out_fn(o, zr, g):
    return (_rms(o, g) * (zr * _sigmoid(zr)),)


def gate_fn(pre, b):
    t = pre + b
    return ((jnp.minimum(t, 0.0) - jnp.log(1.0 + jnp.exp(-jnp.abs(t)))) * (1.0 / GLA_TAU),)


def _attn_probs(q_ref, k_ref, scale, q0, kext):
    s = lax.dot_general(q_ref[...].astype(MXU_DTYPE), k_ref[0:kext, :].astype(MXU_DTYPE), _DIMS["nt"],
                        preferred_element_type=F32) * scale
    if q0 is not None:
        qc = (q0 + lax.broadcasted_iota(jnp.int32, s.shape, 0)) // CHUNK
        kc = lax.broadcasted_iota(jnp.int32, s.shape, 1) // CHUNK
        s = jnp.where(kc <= qc, s, -1e30)
    m = jnp.max(s, axis=-1, keepdims=True)
    e = jnp.exp(s - m)
    return e / jnp.sum(e, axis=-1, keepdims=True)


def _per_query_block(one, causal, nq, tq, Sk):
    if not causal:
        one(None, Sk, None)
        return
    assert tq % CHUNK == 0
    for ib in range(nq):
        pl.when(pl.program_id(1) == ib)(functools.partial(one, ib * tq, min(Sk, (ib + 1) * tq), ib))


def attn_fwd(q, k, v, *, heads, dk, dv, v_off, v_hs, scale, causal, name, tq=256, comm=None):
    Sq, Sk = q.shape[0], k.shape[0]
    tq = _tile(Sq, tq, 8)

    def body(q_ref, k_ref, v_ref, o_ref):
        def one(q0, kext, ib):
            p = _attn_probs(q_ref, k_ref, scale, q0, kext)
            o_ref[...] = jnp.dot(p.astype(MXU_DTYPE), v_ref[0:kext, :].astype(MXU_DTYPE),
                                 preferred_element_type=F32).astype(o_ref.dtype)

        _per_query_block(one, causal, Sq // tq, tq, Sk)

    return _pcall(
        body, [q, k, v], name=name, grid=(heads, Sq // tq),
        in_specs=[pl.BlockSpec((tq, dk), lambda h, i: (i, h)), pl.BlockSpec((Sk, dk), lambda h, i: (0, h)),
                  pl.BlockSpec((Sk, dv), lambda h, i: (0, v_off + h * v_hs))],
        out_specs=pl.BlockSpec((tq, dv), lambda h, i: (i, h)),
        out_shape=jax.ShapeDtypeStruct((Sq, heads * dv), BF16), sem=("parallel", "parallel"), comm=comm)


def attn_bwd(q, k, v, do, *, heads, dk, dv, v_off, v_hs, scale, causal, name, tq=256, comm=None):
    Sq, Sk = q.shape[0], k.shape[0]
    tq = _tile(Sq, tq, 8)

    def body(q_ref, k_ref, v_ref, do_ref, dq_ref, dk_ref, dv_ref):
        @pl.when(pl.program_id(1) == 0)
        def _():
            dk_ref[...] = jnp.zeros_like(dk_ref)
            dv_ref[...] = jnp.zeros_like(dv_ref)

        def one(q0, kext, ib):
            p = _attn_probs(q_ref, k_ref, scale, q0, kext)
            dob = do_ref[...].astype(MXU_DTYPE)
            dp = lax.dot_general(dob, v_ref[0:kext, :].astype(MXU_DTYPE), _DIMS["nt"], preferred_element_type=F32)
            delta = jnp.sum(p * dp, axis=-1, keepdims=True)
            ds = (p * (dp - delta) * scale).astype(MXU_DTYPE)
            dq_ref[...] = jnp.dot(ds, k_ref[0:kext, :].astype(MXU_DTYPE), preferred_element_type=F32)
            dk_ref[0:kext, :] += lax.dot_general(ds, q_ref[...].astype(MXU_DTYPE), _DIMS["tn"],
                                                 preferred_element_type=F32)
            dv_ref[0:kext, :] += lax.dot_general(p.astype(MXU_DTYPE), dob, _DIMS["tn"], preferred_element_type=F32)

        _per_query_block(one, causal, Sq // tq, tq, Sk)

    return _pcall(
        body, [q, k, v, do], name=name, grid=(heads, Sq // tq),
        in_specs=[pl.BlockSpec((tq, dk), lambda h, i: (i, h)), pl.BlockSpec((Sk, dk), lambda h, i: (0, h)),
                  pl.BlockSpec((Sk, dv), lambda h, i: (0, v_off + h * v_hs)),
                  pl.BlockSpec((tq, dv), lambda h, i: (i, h))],
        out_specs=[pl.BlockSpec((tq, dk), lambda h, i: (i, h)), pl.BlockSpec((Sk, dk), lambda h, i: (0, h)),
                   pl.BlockSpec((Sk, dv), lambda h, i: (0, h))],
        out_shape=[jax.ShapeDtypeStruct((Sq, heads * dk), F32), jax.ShapeDtypeStruct((Sk, heads * dk), F32),
                   jax.ShapeDtypeStruct((Sk, heads * dv), F32)],
        sem=("parallel", "arbitrary"), comm=comm)


def _gla_chunk(k, g, tri_ref):
    b = jnp.dot(tri_ref[...], g, precision=lax.Precision.HIGHEST, preferred_element_type=F32)
    b_end = jnp.sum(g, axis=0, keepdims=True)
    e = jnp.exp(b_end - b)
    return k * e, e, jnp.exp(b_end)


def _gla_windows(z, q_off, k_off, v_off, rows_of):
    H, DK, DV = GLA_HEADS, GLA_DK, GLA_DV
    specs, inner = [], []
    for off, ext in ((q_off, H * DK), (k_off, H * DK), (v_off, H * DV)):
        ww, blk, io = _window(z.shape[1], off, ext)
        specs.append(pl.BlockSpec((CHUNK, ww), lambda c, blk=blk: (rows_of(c), blk)))
        inner.append(io)
    return specs, inner


def gla_fwd(z, la, tri, *, q_off, k_off, v_off, name, comm=None):
    S = z.shape[0]
    nchunk = S // CHUNK
    H, DK, DV = GLA_HEADS, GLA_DK, GLA_DV
    qscale = DK ** -0.5
    zspecs, (qi, ki, vi) = _gla_windows(z, q_off, k_off, v_off, lambda c: c)

    def body(q_ref, k_ref, v_ref, la_ref, tri_ref, o_ref, st_ref, state):
        @pl.when(pl.program_id(0) == 0)
        def _():
            state[...] = jnp.zeros_like(state)

        for h in range(H):
            dks, dvs = slice(h * DK, (h + 1) * DK), slice(h * DV, (h + 1) * DV)
            k = k_ref[:, ki + h * DK:ki + (h + 1) * DK].astype(F32)
            v = v_ref[:, vi + h * DV:vi + (h + 1) * DV]
            q = q_ref[:, qi + h * DK:qi + (h + 1) * DK].astype(F32)
            kdec, _, decay = _gla_chunk(k, la_ref[:, dks].astype(F32), tri_ref)
            ut = lax.dot_general(v.astype(MXU_DTYPE), kdec.astype(MXU_DTYPE), _DIMS["tn"], preferred_element_type=F32)
            new = state[h] * decay + ut
            state[h] = new
            st_ref[h] = new
            qs = (q * qscale).astype(MXU_DTYPE)
            o_ref[:, dvs] = lax.dot_general(qs, new.astype(MXU_DTYPE), _DIMS["nt"], preferred_element_type=F32)

    return _pcall(
        body, [z, z, z, la, tri], name=name, grid=(nchunk,),
        in_specs=zspecs + [pl.BlockSpec((CHUNK, H * DK), lambda c: (c, 0)), pl.BlockSpec((CHUNK, CHUNK), lambda c: (0, 0))],
        out_specs=[pl.BlockSpec((CHUNK, H * DV), lambda c: (c, 0)),
                   pl.BlockSpec((H, None, DV, DK), lambda c: (0, c, 0, 0))],
        out_shape=[jax.ShapeDtypeStruct((S, H * DV), F32), jax.ShapeDtypeStruct((H, nchunk, DV, DK), F32)],
        scratch_shapes=[pltpu.VMEM((H, DV, DK), F32)], sem=("arbitrary",), comm=comm)


def gla_bwd(z, la, tri, trit, states, do, *, q_off, k_off, v_off, name, comm=None):
    S = z.shape[0]
    nchunk = S // CHUNK
    H, DK, DV = GLA_HEADS, GLA_DK, GLA_DV
    qscale = DK ** -0.5
    last = nchunk - 1
    zspecs, (qi, ki, vi) = _gla_windows(z, q_off, k_off, v_off, lambda c: last - c)

    def body(q_ref, k_ref, v_ref, la_ref, tri_ref, trit_ref, st_ref, sp_ref, do_ref, dq_ref, dk_ref, dv_ref, dla_ref,
             dstate):
        c = pl.program_id(0)
        cc = last - c

        @pl.when(c == 0)
        def _():
            dstate[...] = jnp.zeros_like(dstate)

        for h in range(H):
            dks, dvs = slice(h * DK, (h + 1) * DK), slice(h * DV, (h + 1) * DV)
            kf = k_ref[:, ki + h * DK:ki + (h + 1) * DK].astype(F32)
            vb16 = v_ref[:, vi + h * DV:vi + (h + 1) * DV].astype(MXU_DTYPE)
            q = q_ref[:, qi + h * DK:qi + (h + 1) * DK].astype(F32)
            kdec, e, decay = _gla_chunk(kf, la_ref[:, dks].astype(F32), tri_ref)
            dob = do_ref[:, dvs].astype(MXU_DTYPE)
            stb = st_ref[h].astype(MXU_DTYPE)
            qs = (q * qscale).astype(MXU_DTYPE)
            dq_ref[:, dks] = jnp.dot(dob, stb, preferred_element_type=F32) * qscale
            dst = dstate[h] + lax.dot_general(dob, qs, _DIMS["tn"], preferred_element_type=F32)
            prev = jnp.where(cc > 0, sp_ref[h], 0.0)
            ddecay = jnp.sum(dst * prev, axis=0, keepdims=True)
            dstate[h] = dst * decay
            dub = dst.astype(MXU_DTYPE)
            dv_ref[:, dvs] = lax.dot_general(kdec.astype(MXU_DTYPE), dub, _DIMS["nt"], preferred_element_type=F32)
            dkdec = jnp.dot(vb16, dub, preferred_element_type=F32)
            dk_ref[:, dks] = dkdec * e
            w = dkdec * kf * e
            db_end = jnp.sum(w, axis=0, keepdims=True) + ddecay * decay
            dla_ref[:, dks] = db_end - jnp.dot(trit_ref[...], w, precision=lax.Precision.HIGHEST,
                                               preferred_element_type=F32)

    def rows(width):
        return pl.BlockSpec((CHUNK, width), lambda c: (last - c, 0))

    square = pl.BlockSpec((CHUNK, CHUNK), lambda c: (0, 0))
    return _pcall(
        body, [z, z, z, la, tri, trit, states, states, do], name=name, grid=(nchunk,),
        in_specs=zspecs + [rows(H * DK), square, square,
                           pl.BlockSpec((H, None, DV, DK), lambda c: (0, last - c, 0, 0)),
                           pl.BlockSpec((H, None, DV, DK), lambda c: (0, jnp.maximum(last - c - 1, 0), 0, 0)),
                           rows(H * DV)],
        out_specs=[rows(H * DK), rows(H * DK), rows(H * DV), rows(H * DK)],
        out_shape=[jax.ShapeDtypeStruct((S, H * DK), F32), jax.ShapeDtypeStruct((S, H * DK), F32),
                   jax.ShapeDtypeStruct((S, H * DV), F32), jax.ShapeDtypeStruct((S, H * DK), F32)],
        scratch_shapes=[pltpu.VMEM((H, DV, DK), F32)], sem=("arbitrary",), comm=comm)


def loss_head(y, target, *, name, tr=256):
    S, D = y.shape
    tr = _tile(S, tr, 8)

    def body(y_ref, t_ref, dy_ref, loss_ref):
        i = pl.program_id(0)
        err = y_ref[...] - t_ref[...]
        dy_ref[...] = err * (1.0 / D)
        part = jnp.zeros((1, LANE), F32) + 0.5 * jnp.sum(jnp.sum(err * err, axis=-1, keepdims=True) * (1.0 / D))

        @pl.when(i == 0)
        def _():
            loss_ref[...] = part

        @pl.when(i > 0)
        def _():
            loss_ref[...] += part

    spec = pl.BlockSpec((tr, D), lambda i: (i, 0))
    return pl.pallas_call(
        body, name=name, grid=(S // tr,), in_specs=[spec, spec],
        out_specs=[spec, pl.BlockSpec((1, LANE), lambda i: (0, 0))],
        out_shape=[jax.ShapeDtypeStruct((S, D), F32), jax.ShapeDtypeStruct((1, LANE), F32)],
        compiler_params=_cparams(("arbitrary",)))(y, target)


def _core_index():
    return lax.axis_index("c").astype(jnp.int32).reshape(1)


def _chip_slots():
    x, y, c = lax.axis_index("x"), lax.axis_index("y"), lax.axis_index("c")
    return jnp.stack([2 * x + y, 2 * (1 - x) + y, 2 * x + (1 - y), 2 * (1 - x) + (1 - y), c]).astype(jnp.int32)


def sum_chip_parts(own, parts, *, name, tr=1024):
    _, R, C = own.shape
    tr = _tile(R, tr, 8)

    def body(idx_ref, o_ref, p0_ref, p1_ref, p2_ref, out_ref):
        acc = o_ref[...].astype(F32) + p0_ref[...].astype(F32)
        acc = acc + p1_ref[...].astype(F32)
        out_ref[...] = acc + p2_ref[...].astype(F32)

    def slot(k):
        return pl.BlockSpec((None, tr, C), lambda i, idx: (idx[k], i, 0))

    grid_spec = pltpu.PrefetchScalarGridSpec(num_scalar_prefetch=1, grid=(R // tr,),
                                             in_specs=[slot(0), slot(1), slot(2), slot(3)], out_specs=slot(4))
    return pl.pallas_call(body, name=name, grid_spec=grid_spec, out_shape=jax.ShapeDtypeStruct((2, R, C), F32),
                          compiler_params=_cparams(("parallel",)))(_chip_slots(), own, parts, parts, parts)


def add_own_half(g, got, out_dtype, *, name, tr=1024):
    n, _, R, C = g.shape
    tr = _tile(R, tr, 8)

    def body(c_ref, a_ref, b_ref, o_ref):
        o_ref[...] = (a_ref[...].astype(F32) + b_ref[...].astype(F32)).astype(out_dtype)

    spec = pl.BlockSpec((None, tr, C), lambda s, i, c: (s, i, 0))
    grid_spec = pltpu.PrefetchScalarGridSpec(
        num_scalar_prefetch=1, grid=(n, R // tr),
        in_specs=[pl.BlockSpec((None, None, tr, C), lambda s, i, c: (s, c[0], i, 0)), spec], out_specs=spec)
    return pl.pallas_call(body, name=name, grid_spec=grid_spec, out_shape=jax.ShapeDtypeStruct((n, R, C), out_dtype),
                          compiler_params=_cparams(("parallel", "parallel")))(_core_index(), g, got)


def adamw(items, *, name, max_steps=16, behind=None):
    c1 = 1.0 / (1.0 - ADAM_B1 ** ADAM_STEP)
    c2 = 1.0 / (1.0 - ADAM_B2 ** ADAM_STEP)
    n = len(items)
    steps = max_steps
    while steps > 1 and any(it[0].shape[0] % (8 * steps) for it in items):
        steps //= 2
    tail = [] if behind is None else [behind]

    def body(*refs):
        for a in range(n):
            w_ref, g_ref, m_ref, v_ref = refs[4 * a:4 * a + 4]
            go_ref, d_ref, nm_ref, nv_ref = refs[4 * n + len(tail) + 4 * a:4 * n + len(tail) + 4 * a + 4]
            gv = g_ref[...]
            go_ref[...] = gv
            nm = ADAM_B1 * m_ref[...] + (1.0 - ADAM_B1) * gv
            nv = ADAM_B2 * v_ref[...] + (1.0 - ADAM_B2) * (gv * gv)
            nm_ref[...] = nm
            nv_ref[...] = nv
            d_ref[...] = -ADAM_LR * ((nm * c1) / (jnp.sqrt(nv * c2) + ADAM_EPS) + ADAM_WD * w_ref[...])

    ops, in_specs, out_specs, out_shape = [], [], [], []
    for w, g, m, v in items:
        R, C = w.shape
        spec = pl.BlockSpec((R // steps, C), lambda i: (i, 0))
        ops += [w, g, m, v]
        in_specs += [spec] * 4
        out_specs += [spec] * 4
        out_shape += [jax.ShapeDtypeStruct((R, C), F32)] * 4
    flat = _pcall(body, ops + tail, name=name, grid=(steps,), in_specs=in_specs + [ANY] * len(tail), out_specs=out_specs,
                  out_shape=out_shape, sem=("parallel",))
    return [tuple(flat[4 * a:4 * a + 4]) for a in range(n)]


def _place():
    x, y, c = lax.axis_index("x"), lax.axis_index("y"), lax.axis_index("c")
    chips = [(1 - x, y), (x, 1 - y), (1 - x, 1 - y)]
    return x, y, c, chips


def _rcopy(src, dst, send, recv, j, to):
    return pltpu.make_async_remote_copy(src_ref=src, dst_ref=dst, send_sem=send.at[j], recv_sem=recv.at[j], device_id=to,
                                        device_id_type=MESH)


def gather_stage1(shards, split):
    n = len(shards)
    ins = [s.reshape(2, s.shape[0] // 2, s.shape[1]) if sp else s for s, sp in zip(shards, split)]
    outs = [jax.ShapeDtypeStruct((N_CHIPS,) + a.shape, a.dtype) for a in ins]

    def start(in_refs, out_refs, send, recv, base):
        x, y, c, chips = _place()
        mine = 2 * x + y
        for i in range(n):
            src = in_refs[i].at[c] if split[i] else in_refs[i]
            dst = out_refs[i].at[mine, c] if split[i] else out_refs[i].at[mine]
            for k, (px, py) in enumerate(chips):
                _rcopy(src, dst, send, recv, base + 3 * i + k, (px, py, c)).start()

    def wait(in_refs, out_refs, send, recv, base):
        x, y, c, chips = _place()
        for i in range(n):
            src = in_refs[i].at[c] if split[i] else in_refs[i]
            for k, (px, py) in enumerate(chips):
                dst = out_refs[i].at[2 * px + py, c] if split[i] else out_refs[i].at[2 * px + py]
                _rcopy(src, dst, send, recv, base + 3 * i + k, (px, py, c)).wait()

    return Comm(ins, outs, 3 * n, start, wait)


def gather_stage2(slots, shards, split):
    n = len(slots)
    own = [s.reshape(2, s.shape[0] // 2, s.shape[1]) if sp else s for s, sp in zip(shards, split)]

    def copies(in_refs, out_refs, send, recv, base):
        x, y, c, chips = _place()
        sib = (x, y, 1 - c)
        for i in range(n):
            j = base + 4 * i
            mine = out_refs[i].at[2 * x + y]
            yield _rcopy(in_refs[n + i], mine, send, recv, j + 3, sib), _rcopy(in_refs[n + i], mine, send, recv, j + 3, sib)
            if split[i]:
                for k, (px, py) in enumerate(chips):
                    s = 2 * px + py
                    yield (_rcopy(in_refs[i].at[s, c], out_refs[i].at[s, c], send, recv, j + k, sib),
                           _rcopy(in_refs[i].at[s, c], out_refs[i].at[s, 1 - c], send, recv, j + k, sib))

    def start(*a):
        for out, _ in copies(*a):
            out.start()

    def wait(*a):
        for _, back in copies(*a):
            back.wait()

    return Comm(list(slots) + own, [jax.ShapeDtypeStruct(s.shape, s.dtype) for s in slots], 4 * n, start, wait,
                {i: i for i in range(n)})


def swap_halves(gs):
    n = len(gs)

    def copies(in_refs, out_refs, send, recv, base):
        x, y, c, _ = _place()
        return [_rcopy(in_refs[i].at[s, 1 - c], out_refs[i].at[s], send, recv, base + N_CHIPS * i + s, (x, y, 1 - c))
                for i in range(n) for s in range(N_CHIPS)]

    def start(*a):
        for cp in copies(*a):
            cp.start()

    def wait(*a):
        for cp in copies(*a):
            cp.wait()

    return Comm(gs, [jax.ShapeDtypeStruct((N_CHIPS,) + g.shape[2:], g.dtype) for g in gs], N_CHIPS * n, start, wait)


def exchange_chips(ps):
    n = len(ps)

    def start(in_refs, out_refs, send, recv, base):
        x, y, c, chips = _place()
        for i in range(n):
            for k, (px, py) in enumerate(chips):
                _rcopy(in_refs[i].at[2 * px + py], out_refs[i].at[2 * x + y], send, recv, base + 3 * i + k,
                       (px, py, c)).start()

    def wait(in_refs, out_refs, send, recv, base):
        x, y, c, chips = _place()
        for i in range(n):
            for k, (px, py) in enumerate(chips):
                _rcopy(in_refs[i].at[2 * px + py], out_refs[i].at[2 * px + py], send, recv, base + 3 * i + k,
                       (px, py, c)).wait()

    return Comm(ps, [jax.ShapeDtypeStruct(p.shape, p.dtype) for p in ps], 3 * n, start, wait)


def join_halves(fs):
    n = len(fs)

    def start(in_refs, out_refs, send, recv, base):
        x, y, c, _ = _place()
        for i in range(n):
            _rcopy(in_refs[i].at[c], out_refs[i].at[c], send, recv, base + i, (x, y, 1 - c)).start()

    def wait(in_refs, out_refs, send, recv, base):
        x, y, c, _ = _place()
        for i in range(n):
            _rcopy(in_refs[i].at[c], out_refs[i].at[1 - c], send, recv, base + i, (x, y, 1 - c)).wait()

    return Comm(fs, [jax.ShapeDtypeStruct(f.shape, f.dtype) for f in fs], n, start, wait, {i: i for i in range(n)})


def allreduce_small(v, *, name):
    m_per, n = v.shape

    def body(x_ref, sum_ref, all_ref, send_sems, recv_sems, local_sem):
        x, y, c, chips = _place()
        me, sibling = (x, y, c), (x, y, 1 - c)

        def rows(px, py, pc):
            return all_ref.at[pl.ds((4 * px + 2 * py + pc) * m_per, m_per), :]

        def copy(k, block, to, src=None):
            return pltpu.make_async_remote_copy(src_ref=rows(*block) if src is None else src, dst_ref=rows(*block),
                                                send_sem=send_sems.at[k], recv_sem=recv_sems.at[k], device_id=to,
                                                device_id_type=MESH)

        mine = pltpu.make_async_copy(x_ref, rows(*me), local_sem)
        mine.start()
        first = [copy(0, me, sibling, src=x_ref)]
        first += [copy(1 + j, me, (*chip, c), src=x_ref) for j, chip in enumerate(chips)]
        for cp in first:
            cp.start()
        passed = [copy(4 + j, (*chip, c), sibling) for j, chip in enumerate(chips)]
        for j, chip in enumerate(chips):
            copy(1 + j, (*chip, c), me).wait_recv()
            passed[j].start()
        copy(0, sibling, me).wait_recv()
        for j, chip in enumerate(chips):
            copy(4 + j, (*chip, 1 - c), me).wait_recv()
        for cp in first + passed:
            cp.wait_send()
        mine.wait()
        acc = all_ref[0:m_per, :]
        for d in range(1, N_DEV):
            acc = acc + all_ref[d * m_per:(d + 1) * m_per, :]
        sum_ref[...] = acc

    vm = pl.BlockSpec(memory_space=pltpu.VMEM)
    return pl.pallas_call(
        body, name=name, in_specs=[vm], out_specs=vm, out_shape=jax.ShapeDtypeStruct((m_per, n), F32),
        scratch_shapes=[pltpu.VMEM((N_DEV * m_per, n), F32), pltpu.SemaphoreType.DMA((7,)),
                        pltpu.SemaphoreType.DMA((7,)), pltpu.SemaphoreType.DMA],
    )(v)


def _cols_to_slots(w):
    r, c4 = w.shape
    return w.reshape(r, N_CHIPS, c4 // N_CHIPS).transpose(1, 0, 2)


def _slots_to_cols(w):
    n, r, c = w.shape
    return w.transpose(1, 0, 2).reshape(r, n * c)


def _pad_cols(a, width):
    return jnp.pad(a, ((0, 0), (0, width - a.shape[1])))


class InLayout:
    def __init__(self, q_rank, kv_rank):
        gk = GLA_HEADS * GLA_DK
        gv = GLA_HEADS * GLA_DV
        sizes = [q_rank, kv_rank, MLA_ROPE, gk, gk, gv, GLA_GATE_RANK, gv]
        names = ["zq", "zkv", "zkr", "gq", "gk", "gv", "zg", "zr"]
        starts = np.concatenate([[0], np.cumsum(sizes)[:-1]])
        self.ref = {n: (int(s), int(z)) for n, s, z in zip(names, starts, sizes)}
        self.ref_width = int(sum(sizes))
        self.order = ["gv", "zr", "zq", "gq", "gk", "zkv", "zkr", "zg"]
        self.off, self.size = {}, {}
        pos = 0
        for n in self.order:
            padded = -(-self.ref[n][1] // LANE) * LANE
            self.off[n], self.size[n] = pos, padded
            pos += padded
        self.width = pos
        self.shard = self.ref_width // N_CHIPS
        self.shard_pad = -(-self.shard // LANE) * LANE

    def _pieces(self, lo, hi):
        out = []
        while lo < hi:
            s = lo // self.shard
            end = min(hi, (s + 1) * self.shard)
            out.append((s * self.shard_pad + lo - s * self.shard, s * self.shard_pad + end - s * self.shard))
            lo = end
        return out

    def from_shards(self, zs):
        cols = []
        for n in self.order:
            start, size = self.ref[n]
            cols += [zs[:, a:b] for a, b in self._pieces(start, start + size)]
            if self.size[n] > size:
                cols.append(jnp.zeros((zs.shape[0], self.size[n] - size), zs.dtype))
        return jnp.concatenate(cols, axis=1)

    def to_shards(self, dz):
        names = sorted(self.ref, key=lambda n: self.ref[n][0])
        ref = jnp.concatenate([dz[:, self.off[n]:self.off[n] + self.ref[n][1]] for n in names], axis=1)
        ref = ref.reshape(dz.shape[0], N_CHIPS, self.shard)
        return jnp.pad(ref, ((0, 0), (0, 0), (0, self.shard_pad - self.shard))).reshape(dz.shape[0], -1)


def _pad_q_up(w):
    r = w.shape[0]
    w = w.reshape(r, MLA_HEADS, MLA_QK)
    w = jnp.pad(w, ((0, 0), (0, 0), (0, MLA_HEAD_PAD - MLA_QK)))
    return w.reshape(r, MLA_HEADS * MLA_HEAD_PAD)


def _unpad_q_up(g):
    r = g.shape[0]
    return g.reshape(r, MLA_HEADS, MLA_HEAD_PAD)[:, :, :MLA_QK].reshape(r, MLA_HEADS * MLA_QK)


def _rope_tables(positions):
    half = MLA_ROPE // 2
    inv_freq = ROPE_THETA ** (-jnp.arange(half, dtype=F32) / half)
    ang = positions.astype(F32).reshape(-1, 1) * inv_freq
    cos, sin = jnp.cos(ang), jnp.sin(ang)
    s = ang.shape[0]
    cosf = jnp.concatenate([cos, cos, jnp.ones((s, LANE - MLA_ROPE), F32)], axis=1)
    sinf = jnp.concatenate([sin, sin, jnp.zeros((s, LANE - MLA_ROPE), F32)], axis=1)
    rot = np.zeros((LANE, LANE), np.float32)
    for j in range(half):
        rot[j + half, j] = -1.0
        rot[j, j + half] = 1.0
    return cosf, sinf, jnp.asarray(rot)


SMALL = ["ffn1_norm", "mix_norm", "q_a_norm", "kv_a_norm", "mla_q_norm", "mla_k_norm", "gla_b_gate", "gla_out_norm",
         "mem_attn_norm", "mem_norm", "mem_q_norm", "mem_k_norm", "ffn2_norm"]
BIG = ["ffn1_w_gate", "ffn1_w_up", "ffn1_w_down", "w_in", "w_q_up", "w_kv_up", "w_out", "mem_w_q", "mem_w_k",
       "mem_w_v", "mem_w_o", "ffn2_w_gate", "ffn2_w_up", "ffn2_w_down"]
COL_SHARDED = {"ffn1_w_gate", "ffn1_w_up", "w_in", "w_q_up", "w_kv_up", "gla_w_gate2", "mem_w_o", "ffn2_w_gate", "ffn2_w_up"}
WEIGHTS = ["ffn1_norm", "ffn1_w_gate", "ffn1_w_up", "ffn1_w_down", "mix_norm", "w_in", "q_a_norm", "w_q_up", "kv_a_norm",
           "w_kv_up", "mla_q_norm", "mla_k_norm", "gla_w_gate2", "gla_b_gate", "gla_out_norm", "w_out", "mem_attn_norm",
           "mem_norm", "mem_w_q", "mem_w_k", "mem_w_v", "mem_w_o", "mem_q_norm", "mem_k_norm", "ffn2_norm", "ffn2_w_gate",
           "ffn2_w_up", "ffn2_w_down"]


def _pack_small(vals, rows=8):
    flat = jnp.concatenate([v.reshape(-1).astype(F32) for v in vals])
    n = flat.shape[0]
    per = -(-n // (rows * LANE)) * LANE
    return jnp.pad(flat, (0, rows * per - n)).reshape(rows, per)


def _unpack_small(packed, shapes):
    flat = packed.reshape(-1)
    out, pos = [], 0
    for s in shapes:
        n = int(np.prod(s))
        out.append(flat[pos:pos + n].reshape(s))
        pos += n
    return out


FFN1 = ["ffn1_w_gate", "ffn1_w_up", "ffn1_w_down"]
FFN2 = ["ffn2_w_gate", "ffn2_w_up", "ffn2_w_down"]
SLOT_WEIGHTS = {"ffn1_w_gate", "ffn1_w_up", "ffn2_w_gate", "ffn2_w_up", "w_in"}
MID_A = ["w_in", "w_q_up", "w_kv_up", "gla_w_gate2"]
MID_B = ["w_out", "mem_w_q", "mem_w_k", "mem_w_v", "mem_w_o"]


def _with(res, comm):
    return res if comm is not None else (res, None)


def kernel(x, mem, positions, ffn1_norm, ffn1_w_gate, ffn1_w_up, ffn1_w_down, mix_norm, w_in, q_a_norm, w_q_up, kv_a_norm, w_kv_up, mla_q_norm, mla_k_norm, gla_w_gate2, gla_b_gate, gla_out_norm, w_out, mem_attn_norm, mem_norm, mem_w_q, mem_w_k, mem_w_v, mem_w_o, mem_q_norm, mem_k_norm, ffn2_norm, ffn2_w_gate, ffn2_w_up, ffn2_w_down, loss_target, m_ffn1_norm, m_ffn1_w_gate, m_ffn1_w_up, m_ffn1_w_down, m_mix_norm, m_w_in, m_q_a_norm, m_w_q_up, m_kv_a_norm, m_w_kv_up, m_mla_q_norm, m_mla_k_norm, m_gla_w_gate2, m_gla_b_gate, m_gla_out_norm, m_w_out, m_mem_attn_norm, m_mem_norm, m_mem_w_q, m_mem_w_k, m_mem_w_v, m_mem_w_o, m_mem_q_norm, m_mem_k_norm, m_ffn2_norm, m_ffn2_w_gate, m_ffn2_w_up, m_ffn2_w_down, v_ffn1_norm, v_ffn1_w_gate, v_ffn1_w_up, v_ffn1_w_down, v_mix_norm, v_w_in, v_q_a_norm, v_w_q_up, v_kv_a_norm, v_w_kv_up, v_mla_q_norm, v_mla_k_norm, v_gla_w_gate2, v_gla_b_gate, v_gla_out_norm, v_w_out, v_mem_attn_norm, v_mem_norm, v_mem_w_q, v_mem_w_k, v_mem_w_v, v_mem_w_o, v_mem_q_norm, v_mem_k_norm, v_ffn2_norm, v_ffn2_w_gate, v_ffn2_w_up, v_ffn2_w_down):
    args = dict(locals())
    two_d = lambda a: a[0] if a.ndim == 3 else a
    W = {n: two_d(args[n]) for n in WEIGHTS}
    M1 = {n: two_d(args["m_" + n]) for n in WEIGHTS}
    V2 = {n: two_d(args["v_" + n]) for n in WEIGHTS}
    xs, mems, tgt = x[0], mem[0], loss_target[0]
    S, D = xs.shape
    chip = 2 * lax.axis_index("x") + lax.axis_index("y")

    q_rank, kv_rank = W["w_q_up"].shape[0], W["w_kv_up"].shape[0]
    lay = InLayout(q_rank, kv_rank)
    off = lay.off
    up_names, down_names = FFN1[:2], FFN1[2:]
    shard16 = {n: W[n].astype(BF16) for n in up_names}
    full = {}

    def stage1(names):
        return gather_stage1([shard16[n] for n in names], [n != "gla_w_gate2" for n in names])

    def stage2(names, slots):
        return gather_stage2(slots, [shard16[n] for n in names], [n != "gla_w_gate2" for n in names])

    def finish(names, slots):
        for n, s in zip(names, slots):
            s = s.reshape((N_CHIPS,) + shard16[n].shape)
            if n in SLOT_WEIGHTS:
                full[n] = s
            else:
                full[n] = _slots_to_cols(s) if n in COL_SHARDED else s.reshape(-1, s.shape[2])

    token = {"last": None}

    def begin(comm, name, after=None):
        started = start_comm(comm, name=name, after=token["last"] if after is None else after)
        token["last"] = started[-1]
        return comm, started

    first = [begin(stage1([n]), f"gather_start_{n}") for n in up_names]
    zero = token["last"][0, 0]
    later = [n for n in BIG + ["gla_w_gate2"] if n not in shard16]
    for n in later:
        shard16[n] = (W[n] + zero).astype(BF16)
    shard16["w_in"] = _pad_cols(shard16["w_in"], lay.shard_pad)
    n1 = row_fwd(rms_fn, [V(xs)], [W["ffn1_norm"]], [(D, BF16)], [(0, 0, 0)], name="ffn1_norm")[0]
    cosf, sinf, rot = _rope_tables(positions[0])
    gate_s1 = wait_comm(*first[0], [n1, cosf, sinf] + [shard16[n] for n in later], name=f"gather_wait_{up_names[0]}")
    finish(up_names[:1], run_comm(stage2(up_names[:1], gate_s1), name="pass_ffn1_gate"))
    up_s1 = wait_comm(*first[1], full[up_names[0]], name=f"gather_wait_{up_names[1]}")
    down1 = begin(stage1(down_names), "gather_start_ffn1_down", after=up_s1[0])
    mid_a_s1 = begin(stage1(MID_A), "gather_start_mid_a")
    finish(up_names[1:], run_comm(stage2(up_names[1:], up_s1), name="pass_ffn1_up"))
    tri = jnp.asarray(np.tril(np.ones((CHUNK, CHUNK), np.float32)))
    gqn = W["mla_q_norm"][:, :MLA_NOPE]
    gqr = _pad_cols(W["mla_q_norm"][:, MLA_NOPE:], LANE)
    gkn = W["mla_k_norm"][:, :MLA_NOPE]
    gkr = _pad_cols(W["mla_k_norm"][:, MLA_NOPE:], LANE)
    HP = MLA_HEAD_PAD
    mla_scale = MLA_QK ** -0.5
    mem_scale = MEM_HEAD_DIM ** -0.5
    mla_w = MLA_HEADS * MLA_V
    gla_w = GLA_HEADS * GLA_DV
    mem_w = MEM_HEADS * MEM_HEAD_DIM

    gate1, up1, act1 = ffn_up(n1, full["ffn1_w_gate"], full["ffn1_w_up"], name="ffn1_up", behind=token["last"])
    finish(down_names, run_comm(stage2(down_names, wait_comm(*down1, act1, name="gather_wait_ffn1_down")),
                                name="pass_ffn1_down"))
    mid_a1 = wait_comm(*mid_a_s1, act1, name="gather_wait_mid_a")
    mid_b = begin(stage1(MID_B), "gather_start_mid_b", after=mid_a1[0])
    x1, got = mm([(act1, full["ffn1_w_down"])], "nn", F32, alpha=0.5, res=xs, name="ffn1_down",
                 comm=stage2(MID_A, mid_a1), behind=token["last"])
    ffn1_saved = (n1, gate1, up1, act1)
    finish(MID_A, got)
    ffn2_s1 = [begin(stage1([n]), f"gather_start_{n}", after=x1 if n == FFN2[0] else None) for n in FFN2]
    w_q_up_p = _pad_q_up(full["w_q_up"])
    w_gate2_p = jnp.pad(full["gla_w_gate2"], ((0, LANE - GLA_GATE_RANK), (0, 0)))
    h = row_fwd(rms_fn, [V(x1)], [W["mix_norm"]], [(D, BF16)], [(0, 0, 0)], name="mix_norm")[0]
    mid_b1 = wait_comm(*mid_b, h, name="gather_wait_mid_b")
    z_shards, got = mm([(h, full["w_in"])], "nn", F32, name="in_proj", b_slots=True, comm=stage2(MID_B, mid_b1),
                       behind=token["last"])
    z = lay.from_shards(z_shards)
    finish(MID_B, got)
    qa = row_fwd(rms_fn, [V(z, off["zq"], q_rank)], [W["q_a_norm"]], [(q_rank, BF16)], [(0, 0, 0)], name="q_a_norm")[0]
    kva = row_fwd(rms_fn, [V(z, off["zkv"], kv_rank)], [W["kv_a_norm"]], [(kv_rank, BF16)], [(0, 0, 0)], name="kv_a_norm")[0]
    qraw = mm([(qa, w_q_up_p)], "nn", F32, name="q_up")
    kvraw = mm([(kva, full["w_kv_up"])], "nn", F32, name="kv_up")
    tabs = [V(cosf, diff=False), V(sinf, diff=False)]
    q_rows = [V(qraw, 0, LANE, HP), V(qraw, LANE, LANE, HP)] + tabs
    k_rows = [V(kvraw, 0, LANE, HP), V(z, off["zkr"], LANE, 0)] + tabs
    qh = row_fwd(qk_prep_fn, q_rows, [gqn, gqr, rot], [(MLA_HEADS * HP, BF16)], [(0, 0, HP), (0, LANE, HP)],
                 heads=MLA_HEADS, name="q_prep")[0]
    kh = row_fwd(qk_prep_fn, k_rows, [gkn, gkr, rot], [(MLA_HEADS * HP, BF16)], [(0, 0, HP), (0, LANE, HP)],
                 heads=MLA_HEADS, name="k_prep")[0]
    mla_kw = dict(heads=MLA_HEADS, dk=HP, dv=MLA_V, v_off=1, v_hs=2, scale=mla_scale, causal=True, tq=512)
    o_mla = attn_fwd(qh, kh, kvraw, name="mla_attn", **mla_kw)

    zg = z[:, off["zg"]:off["zg"] + LANE]
    pre = mm([(zg, w_gate2_p)], "nn", F32, name="gla_gate")
    la = row_fwd(gate_fn, [V(pre)], [W["gla_b_gate"]], [(pre.shape[1], F32)], [(0, 0, 0)], name="gla_log_decay")[0]
    gla_kw = dict(q_off=off["gq"], k_off=off["gk"], v_off=off["gv"])
    f2_gate = wait_comm(*ffn2_s1[0], la, name=f"gather_wait_{FFN2[0]}")
    (o_raw, states), got = gla_fwd(z, la, tri, name="gla_scan", comm=stage2(FFN2[:1], f2_gate), **gla_kw)
    finish(FFN2[:1], got)
    gla_rows = [V(o_raw, 0, GLA_DV, GLA_DV), V(z, off["zr"], GLA_DV, GLA_DV)]
    o_gla = row_fwd(gla_out_fn, gla_rows, [W["gla_out_norm"]], [(gla_w, BF16)], [(0, 0, GLA_DV)], heads=GLA_HEADS,
                    name="gla_out")[0]
    o_cat = jnp.concatenate([o_mla, o_gla], axis=1)
    f2_up = wait_comm(*ffn2_s1[1], o_cat, name=f"gather_wait_{FFN2[1]}")
    x2, got = mm([(o_cat, full["w_out"])], "nn", F32, res=x1, name="out_proj", comm=stage2(FFN2[1:2], f2_up))
    finish(FFN2[1:2], got)

    hm = row_fwd(rms_fn, [V(x2)], [W["mem_attn_norm"]], [(D, BF16)], [(0, 0, 0)], name="mem_attn_norm")[0]
    mn = row_fwd(rms_fn, [V(mems)], [W["mem_norm"]], [(D, BF16)], [(0, 0, 0)], name="mem_norm")[0]
    qm_raw = mm([(hm, full["mem_w_q"])], "nn", F32, name="mem_q")
    km_raw = mm([(mn, full["mem_w_k"])], "nn", F32, name="mem_k")
    vm = mm([(mn, full["mem_w_v"])], "nn", F32, name="mem_v")
    hd = MEM_HEAD_DIM
    qm = row_fwd(rms_fn, [V(qm_raw, 0, hd, hd)], [W["mem_q_norm"]], [(mem_w, BF16)], [(0, 0, hd)], heads=MEM_HEADS,
                 name="mem_q_norm")[0]
    km = row_fwd(rms_fn, [V(km_raw, 0, hd, hd)], [W["mem_k_norm"]], [(mem_w, BF16)], [(0, 0, hd)], heads=MEM_HEADS,
                 name="mem_k_norm")[0]
    mem_kw = dict(heads=MEM_HEADS, dk=hd, dv=hd, v_off=0, v_hs=1, scale=mem_scale, causal=False, tq=1024)
    om = attn_fwd(qm, km, vm, name="mem_attn", **mem_kw)
    x3 = mm([(om, full["mem_w_o"])], "nn", F32, res=x2, name="mem_o")

    n2 = row_fwd(rms_fn, [V(x3)], [W["ffn2_norm"]], [(D, BF16)], [(0, 0, 0)], name="ffn2_norm")[0]
    f2_down = wait_comm(*ffn2_s1[2], n2, name=f"gather_wait_{FFN2[2]}")
    (gate2, up2, act2), got = ffn_up(n2, full["ffn2_w_gate"], full["ffn2_w_up"], name="ffn2_up",
                                     comm=stage2(FFN2[2:], f2_down))
    finish(FFN2[2:], got)
    y = mm([(act2, full["ffn2_w_down"])], "nn", F32, alpha=0.5, res=x3, name="ffn2_down")
    dy, loss_part = loss_head(y, tgt, name="loss_head")
    G = {"loss": loss_part[:, :1]}

    chip_sum, reduced = {}, {}

    def to_halves(n):
        g = G[n]
        if n in SLOT_WEIGHTS:
            s = g
        else:
            s = _cols_to_slots(g) if n in COL_SHARDED else g.reshape(N_CHIPS, g.shape[0] // N_CHIPS, g.shape[1])
        return s.reshape(N_CHIPS, 2, s.shape[1] // 2, s.shape[2])

    def add2(names, halves, got):
        for n, a, b in zip(names, halves, got):
            chip_sum[n] = add_own_half(a, b, BF16, name=f"rs_add2_{n}")

    to_join = []

    def add4_join(names, parts):
        for n, p in zip(names, parts):
            to_join.append((n, sum_chip_parts(chip_sum[n], p, name=f"rs_add4_{n}")))

    def with_joins(comm):
        names, totals = [n for n, _ in to_join], [t for _, t in to_join]
        to_join.clear()
        if not names:
            return comm, lambda got: got
        own = 0 if comm is None else len(comm.out_shapes)
        joined = join_halves(totals)

        def split(got):
            for n, b in zip(names, got[own:]):
                reduced[n] = b.reshape(-1, b.shape[2])[:, :W[n].shape[1]]
            return got[:own]

        return (joined if comm is None else merge_comms(comm, joined)), split

    def flush_joins():
        comm, split = with_joins(None)
        if comm is not None:
            split(run_comm(comm, name=f"rs_join_{len(reduced)}"))

    in_flight = []

    def xchg_start(names):
        in_flight.append((names,) + begin(exchange_chips([chip_sum[n] for n in names]), f"xchg_start_{names[0]}"))

    def xchg_wait(after, count=1):
        for _ in range(count):
            names, comm, started = in_flight.pop(0)
            add4_join(names, wait_comm(comm, started, after, name=f"xchg_wait_{names[0]}"))

    def ffn_backward(dout, xin, tag, saved, dact_comm=None, after_dact=None):
        n_, gate, up, act = saved
        nd, ng, nu = f"{tag}_w_down", f"{tag}_w_gate", f"{tag}_w_up"
        (dgate, dup), got0 = _with(ffn_dact(dout, full[nd], gate, up, 0.5, name=f"{tag}_dact", comm=dact_comm,
                                            behind=token["last"]), dact_comm)
        if after_dact:
            after_dact(got0)
        G[nd] = mm([(act, dout)], "tn", F32, alpha=0.5, name=f"{tag}_dwd", tm=1408, tn=1024, behind=token["last"])
        hd_ = to_halves(nd)
        comm, split = with_joins(swap_halves([hd_]))
        G[ng], got = mm([(n_, dgate)], "tn", F32, name=f"{tag}_dwg", out_slots=True, tm=1024, tn=1408, rows_inner=True,
                        comm=comm)
        add2([nd], [hd_], split(got))
        xchg_start([nd])
        hg = to_halves(ng)
        G[nu], got_g = mm([(n_, dup)], "tn", F32, name=f"{tag}_dwu", out_slots=True, tm=1024, tn=1408, rows_inner=True,
                          comm=swap_halves([hg]), behind=token["last"])
        add2([ng], [hg], got_g)
        xchg_start([ng])
        hu = to_halves(nu)
        dn, got_u = mm([(dgate, full[ng]), (dup, full[nu])], "nt", F32, name=f"{tag}_dn", b_slots=True, tn=1024, tk=1408,
                       comm=swap_halves([hu]), behind=token["last"])
        add2([nu], [hu], got_u)
        xchg_start([nu])
        dx, G[f"{tag}_norm"] = row_bwd(rms_fn, [V(xin)], [W[f"{tag}_norm"]], [V(dn)], const_diff=[True], res=dout,
                                       name=f"{tag}_dnorm")
        return dx

    g3 = ffn_backward(dy, x3, "ffn2", (n2, gate2, up2, act2))
    xchg_wait(g3)

    d_om = mm([(g3, full["mem_w_o"])], "nt", F32, name="mem_o_dx", behind=token["last"])
    G["mem_w_o"] = mm([(om, g3)], "tn", F32, name="mem_o_dw")
    dqm, dkm, dvm = attn_bwd(qm, km, vm, d_om, name="mem_attn_bwd", **mem_kw)
    dqm_raw, G["mem_q_norm"] = row_bwd(rms_fn, [V(qm_raw, 0, hd, hd)], [W["mem_q_norm"]], [V(dqm, 0, hd, hd)],
                                       const_diff=[True], heads=MEM_HEADS, row_dtype=BF16, name="mem_q_norm_bwd")
    dkm_raw, G["mem_k_norm"] = row_bwd(rms_fn, [V(km_raw, 0, hd, hd)], [W["mem_k_norm"]], [V(dkm, 0, hd, hd)],
                                       const_diff=[True], heads=MEM_HEADS, row_dtype=BF16, name="mem_k_norm_bwd")
    dhm = mm([(dqm_raw, full["mem_w_q"])], "nt", F32, name="mem_q_dx")
    G["mem_w_q"] = mm([(hm, dqm_raw)], "tn", F32, name="mem_q_dw")
    dmn = mm([(dkm_raw, full["mem_w_k"]), (dvm, full["mem_w_v"])], "nt", F32, name="mem_kv_dx")
    G["mem_w_k"] = mm([(mn, dkm_raw)], "tn", F32, name="mem_k_dw")
    G["mem_w_v"] = mm([(mn, dvm)], "tn", F32, name="mem_v_dw")
    _, G["mem_norm"] = row_bwd(rms_fn, [V(mems)], [W["mem_norm"]], [V(dmn)], const_diff=[True], row_dtype=BF16,
                               name="mem_norm_bwd")
    g2, G["mem_attn_norm"] = row_bwd(rms_fn, [V(x2)], [W["mem_attn_norm"]], [V(dhm)], const_diff=[True], res=g3,
                                     name="mem_attn_norm_bwd")

    xchg_wait(g2, 2)

    d_ocat = mm([(g2, full["w_out"])], "nt", F32, name="out_proj_dx")
    G["w_out"] = mm([(o_cat, g2)], "tn", F32, name="out_proj_dw")

    d_oraw, d_zr, G["gla_out_norm"] = row_bwd(gla_out_fn, gla_rows, [W["gla_out_norm"]],
                                              [V(d_ocat, mla_w, GLA_DV, GLA_DV)], const_diff=[True], heads=GLA_HEADS,
                                              name="gla_out_bwd")
    mid_b_halves = [to_halves(n) for n in MID_B]
    (d_gq, d_gk, d_gv, d_la), got = gla_bwd(z, la, tri, tri.T, states, d_oraw, name="gla_scan_bwd",
                                            comm=swap_halves(mid_b_halves), **gla_kw)
    add2(MID_B, mid_b_halves, got)
    xchg_start(MID_B)
    d_pre, G["gla_b_gate"] = row_bwd(gate_fn, [V(pre)], [W["gla_b_gate"]], [V(d_la)], const_diff=[True], row_dtype=BF16,
                                     name="gla_log_decay_bwd")
    d_zg = mm([(d_pre, w_gate2_p)], "nt", BF16, name="gla_gate_dx", behind=token["last"])
    G["gla_w_gate2"] = mm([(zg, d_pre)], "tn", F32, name="gla_gate_dw")[:GLA_GATE_RANK]

    comm, split = with_joins(None)
    (d_qh, d_kh, d_v), got = _with(attn_bwd(qh, kh, kvraw, d_ocat, name="mla_attn_bwd", comm=comm, **mla_kw), comm)
    split(got)
    cq = [V(d_qh, 0, LANE, HP), V(d_qh, LANE, LANE, HP)]
    ck = [V(d_kh, 0, LANE, HP), V(d_kh, LANE, LANE, HP)]
    d_qraw, d_gqn, d_gqr = row_bwd(qk_prep_fn, q_rows, [gqn, gqr, rot], cq, const_diff=[True, True, False],
                                   heads=MLA_HEADS, row_dtype=BF16, pack={0: (0, HP), 1: (LANE, HP)},
                                   pack_width=MLA_HEADS * HP, name="q_prep_bwd")
    d_kvraw, d_zkr, d_gkn, d_gkr = row_bwd(qk_prep_fn, k_rows, [gkn, gkr, rot], ck, const_diff=[True, True, False],
                                           heads=MLA_HEADS, row_dtype=BF16, pack={0: (0, HP)}, pack_width=MLA_HEADS * HP,
                                           fills=[(V(d_v, 0, MLA_V, MLA_V), LANE, HP)], name="k_prep_bwd")
    G["mla_q_norm"] = jnp.concatenate([d_gqn, d_gqr[:, :MLA_ROPE]], axis=1)
    G["mla_k_norm"] = jnp.concatenate([d_gkn, d_gkr[:, :MLA_ROPE]], axis=1)
    d_qa = mm([(d_qraw, w_q_up_p)], "nt", F32, name="q_up_dx")
    G["w_q_up"] = _unpad_q_up(mm([(qa, d_qraw)], "tn", F32, name="q_up_dw"))
    d_kva = mm([(d_kvraw, full["w_kv_up"])], "nt", F32, name="kv_up_dx")
    G["w_kv_up"] = mm([(kva, d_kvraw)], "tn", F32, name="kv_up_dw")
    d_zq, G["q_a_norm"] = row_bwd(rms_fn, [V(z, off["zq"], q_rank)], [W["q_a_norm"]], [V(d_qa)], const_diff=[True],
                                  row_dtype=BF16, name="q_a_norm_bwd")
    d_zkv, G["kv_a_norm"] = row_bwd(rms_fn, [V(z, off["zkv"], kv_rank)], [W["kv_a_norm"]], [V(d_kva)], const_diff=[True],
                                    row_dtype=BF16, name="kv_a_norm_bwd")

    seg = {"gv": d_gv, "zr": d_zr, "zq": d_zq, "gq": d_gq, "gk": d_gk, "zkv": d_zkv, "zkr": d_zkr, "zg": d_zg}
    dz = jnp.concatenate([_pad_cols(seg[n].astype(BF16), lay.size[n]) for n in lay.order], axis=1)
    xchg_wait(dz)
    comm, split = with_joins(None)
    dz_shards = lay.to_shards(dz)
    dh, got = _with(mm([(dz_shards, full["w_in"])], "nt", F32, name="in_proj_dx", b_slots=True, comm=comm), comm)
    split(got)
    G["w_in"] = mm([(h, dz_shards)], "tn", F32, name="in_proj_dw", out_slots=True)
    g1, G["mix_norm"] = row_bwd(rms_fn, [V(x1)], [W["mix_norm"]], [V(dh)], const_diff=[True], res=g2,
                                name="mix_norm_bwd")

    mid_a = [n for n in MID_A if n != "gla_w_gate2"]
    mid_a_halves = [to_halves(n) for n in mid_a]

    def mid_a_sums(got):
        add2(mid_a, mid_a_halves, got)
        xchg_start(mid_a)

    gx = ffn_backward(g1, xs, "ffn1", ffn1_saved, dact_comm=swap_halves(mid_a_halves), after_dact=mid_a_sums)
    xchg_wait(gx, 2)

    grad, delta, new_m, new_v = {}, {}, {}, {}

    def adam_group(names, tag, behind=None):
        if any(n not in reduced for n in names):
            flush_joins()
        res = adamw([(W[n], reduced[n], M1[n], V2[n]) for n in names], name=f"adamw_{tag}", behind=behind)
        for n, (g_, d_, m_, v_) in zip(names, res):
            grad[n], delta[n], new_m[n], new_v[n] = g_, d_, m_, v_

    adam_group(FFN2, "ffn2", behind=token["last"])
    adam_group(mid_a + MID_B, "mid", behind=token["last"])
    adam_group(FFN1[2:], "ffn1_down", behind=token["last"])
    xchg_wait(delta[FFN1[2]], 2)
    adam_group(FFN1[:2], "ffn1_up")

    small_names = SMALL + ["gla_w_gate2"]
    packed = small_names + ["loss"]
    small_sum = allreduce_small(_pack_small([G[n] for n in packed]), name="allreduce_small")
    small_g = dict(zip(packed, _unpack_small(small_sum, [G[n].shape for n in packed])))
    loss = small_g["loss"][0, 0]
    shard_c = W["gla_w_gate2"].shape[1]
    grad["gla_w_gate2"] = lax.dynamic_slice_in_dim(small_g["gla_w_gate2"], chip * shard_c, shard_c, axis=1)
    pw = _pack_small([W[n] for n in SMALL] + [W["gla_w_gate2"]])
    pg = _pack_small([small_g[n] for n in SMALL] + [grad["gla_w_gate2"]])
    pm = _pack_small([M1[n] for n in SMALL] + [M1["gla_w_gate2"]])
    pv = _pack_small([V2[n] for n in SMALL] + [V2["gla_w_gate2"]])
    (_, pd, pnm, pnv), = adamw([(pw, pg, pm, pv)], name="adamw_small")
    shapes = [W[n].shape for n in small_names]
    for n, d_, m_, v_ in zip(small_names, _unpack_small(pd, shapes), _unpack_small(pnm, shapes), _unpack_small(pnv, shapes)):
        delta[n], new_m[n], new_v[n] = d_, m_, v_
        if n != "gla_w_gate2":
            grad[n] = small_g[n]

    lead = lambda d: [d[n].reshape(args[n].shape) for n in WEIGHTS]
    return (loss, gx[None], *lead(grad), *lead(delta), *lead(new_m), *lead(new_v))
```

```python
import functools

import numpy as np
import jax
import jax.numpy as jnp
from jax import lax
from jax.experimental import pallas as pl
from jax.experimental.pallas import tpu as pltpu

F32 = jnp.float32
BF16 = jnp.bfloat16
MXU_DTYPE = jnp.bfloat16
MESH = pl.DeviceIdType.MESH
ANY = pl.BlockSpec(memory_space=pl.ANY)

LANE = 128
EPS = 1e-6
CHUNK = 64
MLA_HEADS = 8
MLA_NOPE = 128
MLA_ROPE = 64
MLA_QK = MLA_NOPE + MLA_ROPE
MLA_V = 128
MLA_HEAD_PAD = 2 * LANE
ROPE_THETA = 10000.0
GLA_HEADS = 4
GLA_DK = 128
GLA_DV = 256
GLA_GATE_RANK = 16
GLA_TAU = 16.0
MEM_HEADS = 4
MEM_HEAD_DIM = 128
N_CHIPS = 4
N_DEV = 8

ADAM_LR = 0.001
ADAM_B1 = 0.9
ADAM_B2 = 0.999
ADAM_EPS = 1e-08
ADAM_WD = 0.01
ADAM_STEP = 10

VMEM_LIMIT = 56 * 1024 * 1024


def _cparams(sem=None):
    if sem is None:
        return pltpu.CompilerParams(vmem_limit_bytes=VMEM_LIMIT)
    return pltpu.CompilerParams(dimension_semantics=sem, vmem_limit_bytes=VMEM_LIMIT)


def _tile(dim, pref, unit=LANE):
    if dim <= pref:
        return dim
    t = (pref // unit) * unit
    while t > unit and dim % t:
        t -= unit
    assert dim % t == 0, (dim, pref, unit)
    return t


class Comm:
    def __init__(self, ins, out_shapes, nsem, start, wait, aliases=None):
        self.ins, self.out_shapes, self.nsem = list(ins), list(out_shapes), nsem
        self.start, self.wait, self.aliases = start, wait, dict(aliases or {})


def merge_comms(a, b):
    ai, ao = len(a.ins), len(a.out_shapes)

    def start(ins, outs, send, recv, base):
        a.start(ins[:ai], outs[:ao], send, recv, base)
        b.start(ins[ai:], outs[ao:], send, recv, base + a.nsem)

    def wait(ins, outs, send, recv, base):
        a.wait(ins[:ai], outs[:ao], send, recv, base)
        b.wait(ins[ai:], outs[ao:], send, recv, base + a.nsem)

    aliases = dict(a.aliases)
    aliases.update({ai + i: ao + o for i, o in b.aliases.items()})
    return Comm(a.ins + b.ins, a.out_shapes + b.out_shapes, a.nsem + b.nsem, start, wait, aliases)


def run_comm(comm, *, name):
    ni, no = len(comm.ins), len(comm.out_shapes)

    def body(*refs):
        ins, outs = refs[:ni], refs[ni:ni + no]
        send, recv = refs[ni + no:]
        comm.start(ins, outs, send, recv, 0)
        comm.wait(ins, outs, send, recv, 0)

    return pl.pallas_call(
        body, name=name, in_specs=[ANY] * ni, out_specs=[ANY] * no, out_shape=comm.out_shapes,
        input_output_aliases=comm.aliases,
        scratch_shapes=[pltpu.SemaphoreType.DMA((comm.nsem,)), pltpu.SemaphoreType.DMA((comm.nsem,))])(*comm.ins)


HBM = pl.BlockSpec(memory_space=pltpu.HBM)
SEM = pl.BlockSpec(memory_space=pltpu.SEMAPHORE)


def start_comm(comm, *, name, after=None):
    assert not comm.aliases
    ni, no = len(comm.ins), len(comm.out_shapes)
    tail = [] if after is None else [after]

    def body(*refs):
        srcs, lands = refs[:ni], refs[ni:ni + no]
        send, recv = refs[ni + no + len(tail)], refs[ni + no + len(tail) + 1]
        token = refs[-1]
        comm.start(srcs, lands, send, recv, 0)
        token[...] = jnp.zeros_like(token)

    through = [pltpu.HBM(a.shape, a.dtype) for a in comm.ins] + [pltpu.HBM(s.shape, s.dtype) for s in comm.out_shapes]
    ops = [pltpu.with_memory_space_constraint(a, pltpu.HBM) for a in comm.ins]
    ops += [pltpu.with_memory_space_constraint(lax.empty(s.shape, s.dtype), pltpu.HBM) for s in comm.out_shapes]
    ops += tail
    res = pl.pallas_call(
        body, name=name, in_specs=[HBM] * (ni + no) + [ANY] * len(tail),
        out_shape=[pltpu.SemaphoreType.DMA((comm.nsem,)), pltpu.SemaphoreType.DMA((comm.nsem,))] + through
        + [jax.ShapeDtypeStruct((8, LANE), F32)],
        out_specs=[SEM, SEM] + [HBM] * (ni + no) + [pl.BlockSpec(memory_space=pltpu.VMEM)],
        input_output_aliases={i: 2 + i for i in range(ni + no)},
        compiler_params=pltpu.CompilerParams(has_side_effects=pltpu.SideEffectType.DATAFLOW_SIDE_EFFECTING))(*ops)
    return res[0], res[1], list(res[2:2 + ni]), list(res[2 + ni:2 + ni + no]), res[-1]


def wait_comm(comm, started, after, *, name):
    send, recv, srcs, lands, _ = started
    ni, no = len(srcs), len(lands)
    after = list(after) if isinstance(after, (list, tuple)) else [after]

    def body(*refs):
        comm.wait(refs[:ni], refs[ni:ni + no], refs[ni + no], refs[ni + no + 1], 0)

    res = pl.pallas_call(
        body, name=name, in_specs=[HBM] * (ni + no) + [SEM, SEM] + [ANY] * len(after),
        out_shape=[pltpu.HBM(a.shape, a.dtype) for a in srcs + lands], out_specs=[HBM] * (ni + no),
        input_output_aliases={i: i for i in range(ni + no)},
        compiler_params=pltpu.CompilerParams(has_side_effects=pltpu.SideEffectType.DATAFLOW_SIDE_EFFECTING),
    )(*srcs, *lands, send, recv, *after)
    return list(res[ni:])


def _pcall(body, ops, *, name, grid, in_specs, out_specs, out_shape, sem, scratch_shapes=(), comm=None, behind=None):
    if behind is not None:
        n_real, inner = len(ops), body
        ops, in_specs = list(ops) + [behind], list(in_specs) + [ANY]

        def body(*refs):
            inner(*refs[:n_real], *refs[n_real + 1:])

    if comm is None:
        return pl.pallas_call(body, name=name, grid=grid, in_specs=in_specs, out_specs=out_specs, out_shape=out_shape,
                              scratch_shapes=list(scratch_shapes), compiler_params=_cparams(sem))(*ops)
    multi = isinstance(out_shape, (list, tuple))
    k_out_shape = list(out_shape) if multi else [out_shape]
    k_out_specs = list(out_specs) if multi else [out_specs]
    nki, nko, nks = len(ops), len(k_out_shape), len(scratch_shapes)
    nci, nco = len(comm.ins), len(comm.out_shapes)

    def wrapped(*refs):
        p = 0
        k_in = refs[p:p + nki]; p += nki
        c_in = refs[p:p + nci]; p += nci
        k_out = refs[p:p + nko]; p += nko
        c_out = refs[p:p + nco]; p += nco
        k_scr = refs[p:p + nks]; p += nks
        send, recv = refs[p:]
        first = pl.program_id(0) == 0
        last = pl.program_id(0) == grid[0] - 1
        for a in range(1, len(grid)):
            first = jnp.logical_and(first, pl.program_id(a) == 0)
            last = jnp.logical_and(last, pl.program_id(a) == grid[a] - 1)

        @pl.when(first)
        def _():
            comm.start(c_in, c_out, send, recv, 0)

        body(*k_in, *k_out, *k_scr)

        @pl.when(last)
        def _():
            comm.wait(c_in, c_out, send, recv, 0)

    res = pl.pallas_call(
        wrapped, name=name, grid=grid, in_specs=list(in_specs) + [ANY] * nci, out_specs=k_out_specs + [ANY] * nco,
        out_shape=k_out_shape + comm.out_shapes,
        input_output_aliases={nki + i: nko + o for i, o in comm.aliases.items()},
        scratch_shapes=list(scratch_shapes) + [pltpu.SemaphoreType.DMA((comm.nsem,)), pltpu.SemaphoreType.DMA((comm.nsem,))],
        compiler_params=_cparams(("arbitrary",) * len(grid)))(*ops, *comm.ins)
    k_res = list(res[:nko]) if multi else res[0]
    return k_res, list(res[nko:])


_DIMS = {"nn": (((1,), (0,)), ((), ())), "nt": (((1,), (1,)), ((), ())), "tn": (((0,), (0,)), ((), ()))}


def _blockspec(shape, index, rows_inner):
    return pl.BlockSpec(shape, (lambda j, i, k: index(i, j, k)) if rows_inner else index)


def mm(pairs, mode, out_dtype, *, name, alpha=1.0, res=None, tm=1024, tn=1024, tk=4096, b_slots=False, out_slots=False,
       rows_inner=False, comm=None, behind=None):
    a0, b0 = pairs[0]
    if b_slots:
        b_rows, b_cols = b0.shape[1], N_CHIPS * b0.shape[2]
    else:
        b_rows, b_cols = b0.shape
    (M, K) = a0.shape[::-1] if mode == "tn" else a0.shape
    N = b_rows if mode == "nt" else b_cols
    shard = (b_cols if b_slots else N) // N_CHIPS
    tm = _tile(M, tm)
    tn = _tile(shard if (out_slots or (b_slots and mode != "nt")) else N, tn)
    tk = _tile(shard if (b_slots and mode == "nt") else K, tk)
    nk = K // tk
    npairs = len(pairs)
    dims = _DIMS[mode]
    spec = functools.partial(_blockspec, rows_inner=rows_inner)
    if mode == "tn":
        a_spec = spec((tk, tm), lambda i, j, k: (k, i))
    else:
        a_spec = spec((tm, tk), lambda i, j, k: (i, k))
    per = shard // (tk if mode == "nt" else tn)
    if mode == "nt":
        b_spec = (spec((None, tn, tk), lambda i, j, k: (k // per, j, k % per)) if b_slots else
                  spec((tn, tk), lambda i, j, k: (j, k)))
    else:
        b_spec = (spec((None, tk, tn), lambda i, j, k: (j // per, k, j % per)) if b_slots else
                  spec((tk, tn), lambda i, j, k: (k, j)))
    if out_slots:
        assert res is None and mode != "nt"
        o_spec = spec((None, tm, tn), lambda i, j, k: (j // per, i, j % per))
        out_sds = jax.ShapeDtypeStruct((N_CHIPS, M, shard), out_dtype)
    else:
        o_spec = spec((tm, tn), lambda i, j, k: (i, j))
        out_sds = jax.ShapeDtypeStruct((M, N), out_dtype)
    has_res = res is not None

    def body(*refs):
        ab = refs[:2 * npairs]
        res_ref = refs[2 * npairs] if has_res else None
        o_ref = refs[2 * npairs + int(has_res)]

        def products():
            r = None
            for p in range(npairs):
                d = lax.dot_general(ab[2 * p][...].astype(MXU_DTYPE), ab[2 * p + 1][...].astype(MXU_DTYPE), dims,
                                    preferred_element_type=F32)
                r = d if r is None else r + d
            return r

        def finish(r):
            if alpha != 1.0:
                r = r * alpha
            if has_res:
                r = res_ref[...].astype(F32) + r
            o_ref[...] = r.astype(out_dtype)

        if nk == 1:
            finish(products())
            return
        acc = refs[-1]
        k = pl.program_id(2)

        @pl.when(k == 0)
        def _():
            acc[...] = jnp.zeros_like(acc)

        acc[...] += products()

        @pl.when(k == nk - 1)
        def _():
            finish(acc[...])

    ops, specs = [], []
    for a, b in pairs:
        ops += [a, b]
        specs += [a_spec, b_spec]
    if has_res:
        ops.append(res)
        specs.append(o_spec)
    blocks = (N // tn, M // tm) if rows_inner else (M // tm, N // tn)
    return _pcall(body, ops, name=name, grid=blocks + (nk,), in_specs=specs, out_specs=o_spec, out_shape=out_sds,
                  scratch_shapes=[pltpu.VMEM((tm, tn), F32)] if nk > 1 else [],
                  sem=("parallel", "parallel", "arbitrary"), comm=comm, behind=behind)


def _sigmoid(x):
    return 1.0 / (1.0 + jnp.exp(-x))


def ffn_up(n, wg, wu, *, name, tm=512, tn=1408, comm=None, behind=None):
    M, K = n.shape
    shard = wg.shape[2]
    N = N_CHIPS * shard
    tm, tn = _tile(M, tm), _tile(shard, tn)
    per = shard // tn
    w_spec = pl.BlockSpec((None, K, tn), lambda j, i: (j // per, 0, j % per))

    def body(n_ref, wg_ref, wu_ref, g_ref, u_ref, a_ref):
        nv = n_ref[...].astype(MXU_DTYPE)
        g = jnp.dot(nv, wg_ref[...].astype(MXU_DTYPE), preferred_element_type=F32)
        u = jnp.dot(nv, wu_ref[...].astype(MXU_DTYPE), preferred_element_type=F32)
        g_ref[...] = g.astype(g_ref.dtype)
        u_ref[...] = u.astype(u_ref.dtype)
        a_ref[...] = (g * _sigmoid(g) * u).astype(a_ref.dtype)

    o_spec = pl.BlockSpec((tm, tn), lambda j, i: (i, j))
    sds = jax.ShapeDtypeStruct((M, N), BF16)
    return _pcall(
        body, [n, wg, wu], name=name, grid=(N // tn, M // tm),
        in_specs=[pl.BlockSpec((tm, K), lambda j, i: (i, 0)), w_spec, w_spec],
        out_specs=[o_spec, o_spec, o_spec], out_shape=[sds, sds, sds], sem=("parallel", "parallel"), comm=comm,
        behind=behind)


def ffn_dact(dy, wd, gate, up, alpha, *, name, tm=512, tn=1408, comm=None, behind=None):
    M, K = dy.shape
    N = wd.shape[0]
    tm, tn = _tile(M, tm), _tile(N, tn)

    def body(dy_ref, wd_ref, g_ref, u_ref, dg_ref, du_ref):
        da = lax.dot_general(dy_ref[...].astype(MXU_DTYPE), wd_ref[...].astype(MXU_DTYPE), _DIMS["nt"],
                             preferred_element_type=F32) * alpha
        g = g_ref[...].astype(F32)
        u = u_ref[...].astype(F32)
        s = _sigmoid(g)
        du_ref[...] = (da * (g * s)).astype(du_ref.dtype)
        dg_ref[...] = (da * u * (s * (1.0 + g * (1.0 - s)))).astype(dg_ref.dtype)

    o_spec = pl.BlockSpec((tm, tn), lambda j, i: (i, j))
    sds = jax.ShapeDtypeStruct((M, N), BF16)
    return _pcall(
        body, [dy, wd, gate, up], name=name, grid=(N // tn, M // tm),
        in_specs=[pl.BlockSpec((tm, K), lambda j, i: (i, 0)), pl.BlockSpec((tn, K), lambda j, i: (j, 0)), o_spec, o_spec],
        out_specs=[o_spec, o_spec], out_shape=[sds, sds], sem=("parallel", "parallel"), comm=comm, behind=behind)


def _window(width, off, ext):
    ww = LANE
    while ww < width:
        if ww >= ext and off // ww == (off + ext - 1) // ww and width % ww == 0:
            break
        ww *= 2
    else:
        ww = width
    return ww, off // ww, off - (off // ww) * ww


class V:
    def __init__(self, arr, off=0, w=None, hs=0, diff=True):
        self.arr, self.off, self.hs, self.diff = arr, off, hs, diff
        self.w = arr.shape[1] - off if w is None else w

    def window(self, heads, tr):
        ww, blk, inner = _window(self.arr.shape[1], self.off, (heads - 1) * self.hs + self.w)
        return pl.BlockSpec((tr, ww), lambda i, blk=blk: (i, blk)), inner


def _const_spec(c):
    return pl.BlockSpec(c.shape, lambda i: (0, 0))


def row_fwd(fn, rows, consts, outs, out_map, *, heads=1, tr=256, name):
    S = rows[0].arr.shape[0]
    tr = _tile(S, tr, 8)
    wins = [v.window(heads, tr) for v in rows]
    nr, nc = len(rows), len(consts)

    def body(*refs):
        row_refs, const_refs, out_refs = refs[:nr], refs[nr:nr + nc], refs[nr + nc:]
        cv = [c[...].astype(F32) for c in const_refs]
        for h in range(heads):
            rv = []
            for v, (_, io), r in zip(rows, wins, row_refs):
                lo = io + h * v.hs
                rv.append(r[:, lo:lo + v.w].astype(F32))
            res = fn(*rv, *cv)
            for (ai, off, hs), o in zip(out_map, res):
                lo = off + h * hs
                out_refs[ai][:, lo:lo + o.shape[1]] = o.astype(out_refs[ai].dtype)

    return pl.pallas_call(
        body, name=name, grid=(S // tr,),
        in_specs=[w[0] for w in wins] + [_const_spec(c) for c in consts],
        out_specs=[pl.BlockSpec((tr, w), lambda i: (i, 0)) for w, _ in outs],
        out_shape=[jax.ShapeDtypeStruct((S, w), d) for w, d in outs],
        compiler_params=_cparams(("parallel",)))(*[v.arr for v in rows], *consts)


def row_bwd(fn, rows, consts, cots, *, const_diff, heads=1, tr=256, res=None, row_dtype=F32, pack=None, pack_width=0,
            fills=(), copy16=False, name):
    S = rows[0].arr.shape[0]
    tr = _tile(S, tr, 8)
    pack = dict(pack or {})
    nr, nc, nct, nf = len(rows), len(consts), len(cots), len(fills)
    wins = [v.window(heads, tr) for v in rows]
    cwins = [v.window(heads, tr) for v in cots]
    fwins = [v.window(heads, tr) for v, _, _ in fills]
    drows = [k for k, v in enumerate(rows) if v.diff]
    dconsts = [k for k in range(nc) if const_diff[k]]
    has_res = res is not None
    assert not (has_res and 0 in pack)
    widths = [pack_width] if pack else []
    place = []
    for n, k in enumerate(drows):
        if n in pack:
            place.append((0,) + tuple(pack[n]))
        else:
            place.append((len(widths), 0, rows[k].w))
            widths.append(rows[k].w * (heads if rows[k].hs else 1))

    def body(*refs):
        row_refs = refs[:nr]
        const_refs = refs[nr:nr + nc]
        cot_refs = refs[nr + nc:nr + nc + nct]
        p = nr + nc + nct
        fill_refs = refs[p:p + nf]
        p += nf
        res_ref = refs[p] if has_res else None
        p += int(has_res)
        grow_refs = refs[p:p + len(widths)]
        p += len(widths)
        copy_ref = refs[p] if copy16 else None
        gconst_refs = refs[p + int(copy16):]
        i = pl.program_id(0)
        cv = [c[...].astype(F32) for c in const_refs]
        shared = [None] * len(drows)
        gc_sum = [None] * len(dconsts)
        for h in range(heads):
            rv = []
            for v, (_, io), r in zip(rows, wins, row_refs):
                lo = io + h * v.hs
                rv.append(r[:, lo:lo + v.w].astype(F32))
            ct = []
            for v, (_, io), r in zip(cots, cwins, cot_refs):
                lo = io + h * v.hs
                ct.append(r[:, lo:lo + v.w].astype(F32))

            def closed(*d):
                rr, cc = list(rv), list(cv)
                for k, val in zip(drows, d[:len(drows)]):
                    rr[k] = val
                for k, val in zip(dconsts, d[len(drows):]):
                    cc[k] = val
                return tuple(fn(*rr, *cc))

            _, vjp = jax.vjp(closed, *[rv[k] for k in drows], *[cv[k] for k in dconsts])
            grads = vjp(tuple(ct))
            for n, k in enumerate(drows):
                g = grads[n]
                if rows[k].hs == 0 and heads > 1:
                    shared[n] = g if shared[n] is None else shared[n] + g
                else:
                    if n == 0 and has_res:
                        g = g + res_ref[:, h * rows[k].w:(h + 1) * rows[k].w].astype(F32)
                    out, off, hs = place[n]
                    grow_refs[out][:, off + h * hs:off + h * hs + rows[k].w] = g.astype(row_dtype)
                    if copy16 and n == 0:
                        copy_ref[:, off + h * hs:off + h * hs + rows[k].w] = g.astype(BF16)
            for (v, off, hs), (_, io), r in zip(fills, fwins, fill_refs):
                lo = io + h * v.hs
                grow_refs[0][:, off + h * hs:off + h * hs + v.w] = r[:, lo:lo + v.w].astype(row_dtype)
            for n in range(len(dconsts)):
                g = grads[len(drows) + n]
                gc_sum[n] = g if gc_sum[n] is None else gc_sum[n] + g
        for n, k in enumerate(drows):
            if shared[n] is not None:
                g = shared[n]
                if n == 0 and has_res:
                    g = g + res_ref[...].astype(F32)
                grow_refs[place[n][0]][...] = g.astype(row_dtype)

        @pl.when(i == 0)
        def _():
            for n in range(len(dconsts)):
                gconst_refs[n][...] = gc_sum[n]

        @pl.when(i > 0)
        def _():
            for n in range(len(dconsts)):
                gconst_refs[n][...] += gc_sum[n]

    in_specs = [w[0] for w in wins] + [_const_spec(c) for c in consts] + [w[0] for w in cwins] + [w[0] for w in fwins]
    ops = [v.arr for v in rows] + list(consts) + [v.arr for v in cots] + [v.arr for v, _, _ in fills]
    if has_res:
        in_specs.append(pl.BlockSpec((tr, widths[0]), lambda i: (i, 0)))
        ops.append(res)
    out_specs = [pl.BlockSpec((tr, w), lambda i: (i, 0)) for w in widths]
    out_shape = [jax.ShapeDtypeStruct((S, w), row_dtype) for w in widths]
    if copy16:
        out_specs.append(pl.BlockSpec((tr, widths[0]), lambda i: (i, 0)))
        out_shape.append(jax.ShapeDtypeStruct((S, widths[0]), BF16))
    for k in dconsts:
        out_specs.append(_const_spec(consts[k]))
        out_shape.append(jax.ShapeDtypeStruct(consts[k].shape, F32))
    return pl.pallas_call(body, name=name, grid=(S // tr,), in_specs=in_specs, out_specs=out_specs,
                          out_shape=out_shape, compiler_params=_cparams(("arbitrary",)))(*ops)


def _rms(x, g, n=None):
    n = x.shape[-1] if n is None else n
    ms = jnp.sum(x * x, axis=-1, keepdims=True) * (1.0 / n)
    return x * lax.rsqrt(ms + EPS) * g


def rms_fn(x, g):
    return (_rms(x, g),)


def qk_prep_fn(nope, rope, cos, sin, gn, gr, rot):
    ms = (jnp.sum(nope * nope, axis=-1, keepdims=True) + jnp.sum(rope * rope, axis=-1, keepdims=True)) * (1.0 / MLA_QK)
    r = lax.rsqrt(ms + EPS)
    on = nope * r * gn
    orr = rope * r * gr
    turned = jnp.dot(orr, rot, precision=lax.Precision.HIGHEST, preferred_element_type=F32)
    return on, orr * cos + turned * sin


def gla_out_fn(o, zr, g):
    return (_rms(o, g) * (zr * _sigmoid(zr)),)


def gate_fn(pre, b):
    t = pre + b
    return ((jnp.minimum(t, 0.0) - jnp.log(1.0 + jnp.exp(-jnp.abs(t)))) * (1.0 / GLA_TAU),)


def _attn_probs(q_ref, k_ref, scale, q0, kext):
    s = lax.dot_general(q_ref[...].astype(MXU_DTYPE), k_ref[0:kext, :].astype(MXU_DTYPE), _DIMS["nt"],
                        preferred_element_type=F32) * scale
    if q0 is not None:
        qc = (q0 + lax.broadcasted_iota(jnp.int32, s.shape, 0)) // CHUNK
        kc = lax.broadcasted_iota(jnp.int32, s.shape, 1) // CHUNK
        s = jnp.where(kc <= qc, s, -1e30)
    m = jnp.max(s, axis=-1, keepdims=True)
    e = jnp.exp(s - m)
    return e / jnp.sum(e, axis=-1, keepdims=True)


def _per_query_block(one, causal, nq, tq, Sk):
    if not causal:
        one(None, Sk, None)
        return
    assert tq % CHUNK == 0
    for ib in range(nq):
        pl.when(pl.program_id(1) == ib)(functools.partial(one, ib * tq, min(Sk, (ib + 1) * tq), ib))


def attn_fwd(q, k, v, *, heads, dk, dv, v_off, v_hs, scale, causal, name, tq=256, comm=None):
    Sq, Sk = q.shape[0], k.shape[0]
    tq = _tile(Sq, tq, 8)

    def body(q_ref, k_ref, v_ref, o_ref):
        def one(q0, kext, ib):
            p = _attn_probs(q_ref, k_ref, scale, q0, kext)
            o_ref[...] = jnp.dot(p.astype(MXU_DTYPE), v_ref[0:kext, :].astype(MXU_DTYPE),
                                 preferred_element_type=F32).astype(o_ref.dtype)

        _per_query_block(one, causal, Sq // tq, tq, Sk)

    return _pcall(
        body, [q, k, v], name=name, grid=(heads, Sq // tq),
        in_specs=[pl.BlockSpec((tq, dk), lambda h, i: (i, h)), pl.BlockSpec((Sk, dk), lambda h, i: (0, h)),
                  pl.BlockSpec((Sk, dv), lambda h, i: (0, v_off + h * v_hs))],
        out_specs=pl.BlockSpec((tq, dv), lambda h, i: (i, h)),
        out_shape=jax.ShapeDtypeStruct((Sq, heads * dv), BF16), sem=("parallel", "parallel"), comm=comm)


def attn_bwd(q, k, v, do, *, heads, dk, dv, v_off, v_hs, scale, causal, name, tq=256, comm=None):
    Sq, Sk = q.shape[0], k.shape[0]
    tq = _tile(Sq, tq, 8)

    def body(q_ref, k_ref, v_ref, do_ref, dq_ref, dk_ref, dv_ref):
        @pl.when(pl.program_id(1) == 0)
        def _():
            dk_ref[...] = jnp.zeros_like(dk_ref)
            dv_ref[...] = jnp.zeros_like(dv_ref)

        def one(q0, kext, ib):
            p = _attn_probs(q_ref, k_ref, scale, q0, kext)
            dob = do_ref[...].astype(MXU_DTYPE)
            dp = lax.dot_general(dob, v_ref[0:kext, :].astype(MXU_DTYPE), _DIMS["nt"], preferred_element_type=F32)
            delta = jnp.sum(p * dp, axis=-1, keepdims=True)
            ds = (p * (dp - delta) * scale).astype(MXU_DTYPE)
            dq_ref[...] = jnp.dot(ds, k_ref[0:kext, :].astype(MXU_DTYPE), preferred_element_type=F32)
            dk_ref[0:kext, :] += lax.dot_general(ds, q_ref[...].astype(MXU_DTYPE), _DIMS["tn"],
                                                 preferred_element_type=F32)
            dv_ref[0:kext, :] += lax.dot_general(p.astype(MXU_DTYPE), dob, _DIMS["tn"], preferred_element_type=F32)

        _per_query_block(one, causal, Sq // tq, tq, Sk)

    return _pcall(
        body, [q, k, v, do], name=name, grid=(heads, Sq // tq),
        in_specs=[pl.BlockSpec((tq, dk), lambda h, i: (i, h)), pl.BlockSpec((Sk, dk), lambda h, i: (0, h)),
                  pl.BlockSpec((Sk, dv), lambda h, i: (0, v_off + h * v_hs)),
                  pl.BlockSpec((tq, dv), lambda h, i: (i, h))],
        out_specs=[pl.BlockSpec((tq, dk), lambda h, i: (i, h)), pl.BlockSpec((Sk, dk), lambda h, i: (0, h)),
                   pl.BlockSpec((Sk, dv), lambda h, i: (0, h))],
        out_shape=[jax.ShapeDtypeStruct((Sq, heads * dk), F32), jax.ShapeDtypeStruct((Sk, heads * dk), F32),
                   jax.ShapeDtypeStruct((Sk, heads * dv), F32)],
        sem=("parallel", "arbitrary"), comm=comm)


def _gla_chunk(k, g, tri_ref):
    b = jnp.dot(tri_ref[...], g, precision=lax.Precision.HIGHEST, preferred_element_type=F32)
    b_end = jnp.sum(g, axis=0, keepdims=True)
    e = jnp.exp(b_end - b)
    return k * e, e, jnp.exp(b_end)


def _gla_windows(z, q_off, k_off, v_off, rows_of):
    H, DK, DV = GLA_HEADS, GLA_DK, GLA_DV
    specs, inner = [], []
    for off, ext in ((q_off, H * DK), (k_off, H * DK), (v_off, H * DV)):
        ww, blk, io = _window(z.shape[1], off, ext)
        specs.append(pl.BlockSpec((CHUNK, ww), lambda c, blk=blk: (rows_of(c), blk)))
        inner.append(io)
    return specs, inner


def gla_fwd(z, la, tri, *, q_off, k_off, v_off, name, comm=None):
    S = z.shape[0]
    nchunk = S // CHUNK
    H, DK, DV = GLA_HEADS, GLA_DK, GLA_DV
    qscale = DK ** -0.5
    zspecs, (qi, ki, vi) = _gla_windows(z, q_off, k_off, v_off, lambda c: c)

    def body(q_ref, k_ref, v_ref, la_ref, tri_ref, o_ref, st_ref, state):
        @pl.when(pl.program_id(0) == 0)
        def _():
            state[...] = jnp.zeros_like(state)

        for h in range(H):
            dks, dvs = slice(h * DK, (h + 1) * DK), slice(h * DV, (h + 1) * DV)
            k = k_ref[:, ki + h * DK:ki + (h + 1) * DK].astype(F32)
            v = v_ref[:, vi + h * DV:vi + (h + 1) * DV]
            q = q_ref[:, qi + h * DK:qi + (h + 1) * DK].astype(F32)
            kdec, _, decay = _gla_chunk(k, la_ref[:, dks].astype(F32), tri_ref)
            ut = lax.dot_general(v.astype(MXU_DTYPE), kdec.astype(MXU_DTYPE), _DIMS["tn"], preferred_element_type=F32)
            new = state[h] * decay + ut
            state[h] = new
            st_ref[h] = new
            qs = (q * qscale).astype(MXU_DTYPE)
            o_ref[:, dvs] = lax.dot_general(qs, new.astype(MXU_DTYPE), _DIMS["nt"], preferred_element_type=F32)

    return _pcall(
        body, [z, z, z, la, tri], name=name, grid=(nchunk,),
        in_specs=zspecs + [pl.BlockSpec((CHUNK, H * DK), lambda c: (c, 0)), pl.BlockSpec((CHUNK, CHUNK), lambda c: (0, 0))],
        out_specs=[pl.BlockSpec((CHUNK, H * DV), lambda c: (c, 0)),
                   pl.BlockSpec((H, None, DV, DK), lambda c: (0, c, 0, 0))],
        out_shape=[jax.ShapeDtypeStruct((S, H * DV), F32), jax.ShapeDtypeStruct((H, nchunk, DV, DK), F32)],
        scratch_shapes=[pltpu.VMEM((H, DV, DK), F32)], sem=("arbitrary",), comm=comm)


def gla_bwd(z, la, tri, trit, states, do, *, q_off, k_off, v_off, name, comm=None):
    S = z.shape[0]
    nchunk = S // CHUNK
    H, DK, DV = GLA_HEADS, GLA_DK, GLA_DV
    qscale = DK ** -0.5
    last = nchunk - 1
    zspecs, (qi, ki, vi) = _gla_windows(z, q_off, k_off, v_off, lambda c: last - c)

    def body(q_ref, k_ref, v_ref, la_ref, tri_ref, trit_ref, st_ref, sp_ref, do_ref, dq_ref, dk_ref, dv_ref, dla_ref,
             dstate):
        c = pl.program_id(0)
        cc = last - c

        @pl.when(c == 0)
        def _():
            dstate[...] = jnp.zeros_like(dstate)

        for h in range(H):
            dks, dvs = slice(h * DK, (h + 1) * DK), slice(h * DV, (h + 1) * DV)
            kf = k_ref[:, ki + h * DK:ki + (h + 1) * DK].astype(F32)
            vb16 = v_ref[:, vi + h * DV:vi + (h + 1) * DV].astype(MXU_DTYPE)
            q = q_ref[:, qi + h * DK:qi + (h + 1) * DK].astype(F32)
            kdec, e, decay = _gla_chunk(kf, la_ref[:, dks].astype(F32), tri_ref)
            dob = do_ref[:, dvs].astype(MXU_DTYPE)
            stb = st_ref[h].astype(MXU_DTYPE)
            qs = (q * qscale).astype(MXU_DTYPE)
            dq_ref[:, dks] = jnp.dot(dob, stb, preferred_element_type=F32) * qscale
            dst = dstate[h] + lax.dot_general(dob, qs, _DIMS["tn"], preferred_element_type=F32)
            prev = jnp.where(cc > 0, sp_ref[h], 0.0)
            ddecay = jnp.sum(dst * prev, axis=0, keepdims=True)
            dstate[h] = dst * decay
            dub = dst.astype(MXU_DTYPE)
            dv_ref[:, dvs] = lax.dot_general(kdec.astype(MXU_DTYPE), dub, _DIMS["nt"], preferred_element_type=F32)
            dkdec = jnp.dot(vb16, dub, preferred_element_type=F32)
            dk_ref[:, dks] = dkdec * e
            w = dkdec * kf * e
            db_end = jnp.sum(w, axis=0, keepdims=True) + ddecay * decay
            dla_ref[:, dks] = db_end - jnp.dot(trit_ref[...], w, precision=lax.Precision.HIGHEST,
                                               preferred_element_type=F32)

    def rows(width):
        return pl.BlockSpec((CHUNK, width), lambda c: (last - c, 0))

    square = pl.BlockSpec((CHUNK, CHUNK), lambda c: (0, 0))
    return _pcall(
        body, [z, z, z, la, tri, trit, states, states, do], name=name, grid=(nchunk,),
        in_specs=zspecs + [rows(H * DK), square, square,
                           pl.BlockSpec((H, None, DV, DK), lambda c: (0, last - c, 0, 0)),
                           pl.BlockSpec((H, None, DV, DK), lambda c: (0, jnp.maximum(last - c - 1, 0), 0, 0)),
                           rows(H * DV)],
        out_specs=[rows(H * DK), rows(H * DK), rows(H * DV), rows(H * DK)],
        out_shape=[jax.ShapeDtypeStruct((S, H * DK), F32), jax.ShapeDtypeStruct((S, H * DK), F32),
                   jax.ShapeDtypeStruct((S, H * DV), F32), jax.ShapeDtypeStruct((S, H * DK), F32)],
        scratch_shapes=[pltpu.VMEM((H, DV, DK), F32)], sem=("arbitrary",), comm=comm)


def loss_head(y, target, *, name, tr=256):
    S, D = y.shape
    tr = _tile(S, tr, 8)

    def body(y_ref, t_ref, dy_ref, dy16_ref, loss_ref):
        i = pl.program_id(0)
        err = y_ref[...] - t_ref[...]
        dy_ref[...] = err * (1.0 / D)
        dy16_ref[...] = (err * (1.0 / D)).astype(BF16)
        part = jnp.zeros((1, LANE), F32) + 0.5 * jnp.sum(jnp.sum(err * err, axis=-1, keepdims=True) * (1.0 / D))

        @pl.when(i == 0)
        def _():
            loss_ref[...] = part

        @pl.when(i > 0)
        def _():
            loss_ref[...] += part

    spec = pl.BlockSpec((tr, D), lambda i: (i, 0))
    return pl.pallas_call(
        body, name=name, grid=(S // tr,), in_specs=[spec, spec],
        out_specs=[spec, spec, pl.BlockSpec((1, LANE), lambda i: (0, 0))],
        out_shape=[jax.ShapeDtypeStruct((S, D), F32), jax.ShapeDtypeStruct((S, D), BF16),
                   jax.ShapeDtypeStruct((1, LANE), F32)],
        compiler_params=_cparams(("arbitrary",)))(y, target)


def _core_index():
    return lax.axis_index("c").astype(jnp.int32).reshape(1)


def _chip_slots():
    x, y, c = lax.axis_index("x"), lax.axis_index("y"), lax.axis_index("c")
    return jnp.stack([2 * x + y, 2 * (1 - x) + y, 2 * x + (1 - y), 2 * (1 - x) + (1 - y), c]).astype(jnp.int32)


def sum_chip_parts(own, parts, *, name, tr=1024):
    _, R, C = own.shape
    tr = _tile(R, tr, 8)

    def body(idx_ref, o_ref, p0_ref, p1_ref, p2_ref, out_ref):
        acc = o_ref[...].astype(F32) + p0_ref[...].astype(F32)
        acc = acc + p1_ref[...].astype(F32)
        out_ref[...] = acc + p2_ref[...].astype(F32)

    def slot(k):
        return pl.BlockSpec((None, tr, C), lambda i, idx: (idx[k], i, 0))

    grid_spec = pltpu.PrefetchScalarGridSpec(num_scalar_prefetch=1, grid=(R // tr,),
                                             in_specs=[slot(0), slot(1), slot(2), slot(3)], out_specs=slot(4))
    return pl.pallas_call(body, name=name, grid_spec=grid_spec, out_shape=jax.ShapeDtypeStruct((2, R, C), F32),
                          compiler_params=_cparams(("parallel",)))(_chip_slots(), own, parts, parts, parts)


def add_own_half(g, got, out_dtype, *, name, tr=1024):
    n, _, R, C = g.shape
    tr = _tile(R, tr, 8)

    def body(c_ref, a_ref, b_ref, o_ref):
        o_ref[...] = (a_ref[...].astype(F32) + b_ref[...].astype(F32)).astype(out_dtype)

    spec = pl.BlockSpec((None, tr, C), lambda s, i, c: (s, i, 0))
    grid_spec = pltpu.PrefetchScalarGridSpec(
        num_scalar_prefetch=1, grid=(n, R // tr),
        in_specs=[pl.BlockSpec((None, None, tr, C), lambda s, i, c: (s, c[0], i, 0)), spec], out_specs=spec)
    return pl.pallas_call(body, name=name, grid_spec=grid_spec, out_shape=jax.ShapeDtypeStruct((n, R, C), out_dtype),
                          compiler_params=_cparams(("parallel", "parallel")))(_core_index(), g, got)


def adamw(items, *, name, max_steps=16, behind=None):
    c1 = 1.0 / (1.0 - ADAM_B1 ** ADAM_STEP)
    c2 = 1.0 / (1.0 - ADAM_B2 ** ADAM_STEP)
    n = len(items)
    steps = max_steps
    while steps > 1 and any(it[0].shape[0] % (8 * steps) for it in items):
        steps //= 2
    tail = [] if behind is None else [behind]

    def body(*refs):
        for a in range(n):
            w_ref, g_ref, m_ref, v_ref = refs[4 * a:4 * a + 4]
            go_ref, d_ref, nm_ref, nv_ref = refs[4 * n + len(tail) + 4 * a:4 * n + len(tail) + 4 * a + 4]
            gv = g_ref[...]
            go_ref[...] = gv
            nm = ADAM_B1 * m_ref[...] + (1.0 - ADAM_B1) * gv
            nv = ADAM_B2 * v_ref[...] + (1.0 - ADAM_B2) * (gv * gv)
            nm_ref[...] = nm
            nv_ref[...] = nv
            d_ref[...] = -ADAM_LR * ((nm * c1) / (jnp.sqrt(nv * c2) + ADAM_EPS) + ADAM_WD * w_ref[...])

    ops, in_specs, out_specs, out_shape = [], [], [], []
    for w, g, m, v in items:
        R, C = w.shape
        spec = pl.BlockSpec((R // steps, C), lambda i: (i, 0))
        ops += [w, g, m, v]
        in_specs += [spec] * 4
        out_specs += [spec] * 4
        out_shape += [jax.ShapeDtypeStruct((R, C), F32)] * 4
    flat = _pcall(body, ops + tail, name=name, grid=(steps,), in_specs=in_specs + [ANY] * len(tail), out_specs=out_specs,
                  out_shape=out_shape, sem=("parallel",))
    return [tuple(flat[4 * a:4 * a + 4]) for a in range(n)]


def _place():
    x, y, c = lax.axis_index("x"), lax.axis_index("y"), lax.axis_index("c")
    chips = [(1 - x, y), (x, 1 - y), (1 - x, 1 - y)]
    return x, y, c, chips


def _rcopy(src, dst, send, recv, j, to):
    return pltpu.make_async_remote_copy(src_ref=src, dst_ref=dst, send_sem=send.at[j], recv_sem=recv.at[j], device_id=to,
                                        device_id_type=MESH)


def gather_stage1(shards, split):
    n = len(shards)
    ins = [s.reshape(2, s.shape[0] // 2, s.shape[1]) if sp else s for s, sp in zip(shards, split)]
    outs = [jax.ShapeDtypeStruct((N_CHIPS,) + a.shape, a.dtype) for a in ins]

    def start(in_refs, out_refs, send, recv, base):
        x, y, c, chips = _place()
        mine = 2 * x + y
        for i in range(n):
            src = in_refs[i].at[c] if split[i] else in_refs[i]
            dst = out_refs[i].at[mine, c] if split[i] else out_refs[i].at[mine]
            for k, (px, py) in enumerate(chips):
                _rcopy(src, dst, send, recv, base + 3 * i + k, (px, py, c)).start()

    def wait(in_refs, out_refs, send, recv, base):
        x, y, c, chips = _place()
        for i in range(n):
            src = in_refs[i].at[c] if split[i] else in_refs[i]
            for k, (px, py) in enumerate(chips):
                dst = out_refs[i].at[2 * px + py, c] if split[i] else out_refs[i].at[2 * px + py]
                _rcopy(src, dst, send, recv, base + 3 * i + k, (px, py, c)).wait()

    return Comm(ins, outs, 3 * n, start, wait)


def gather_stage2(slots, shards, split):
    n = len(slots)
    own = [s.reshape(2, s.shape[0] // 2, s.shape[1]) if sp else s for s, sp in zip(shards, split)]

    def copies(in_refs, out_refs, send, recv, base):
        x, y, c, chips = _place()
        sib = (x, y, 1 - c)
        for i in range(n):
            j = base + 4 * i
            mine = out_refs[i].at[2 * x + y]
            yield _rcopy(in_refs[n + i], mine, send, recv, j + 3, sib), _rcopy(in_refs[n + i], mine, send, recv, j + 3, sib)
            if split[i]:
                for k, (px, py) in enumerate(chips):
                    s = 2 * px + py
                    yield (_rcopy(in_refs[i].at[s, c], out_refs[i].at[s, c], send, recv, j + k, sib),
                           _rcopy(in_refs[i].at[s, c], out_refs[i].at[s, 1 - c], send, recv, j + k, sib))

    def start(*a):
        for out, _ in copies(*a):
            out.start()

    def wait(*a):
        for _, back in copies(*a):
            back.wait()

    return Comm(list(slots) + own, [jax.ShapeDtypeStruct(s.shape, s.dtype) for s in slots], 4 * n, start, wait,
                {i: i for i in range(n)})


def swap_halves(gs):
    n = len(gs)

    def copies(in_refs, out_refs, send, recv, base):
        x, y, c, _ = _place()
        return [_rcopy(in_refs[i].at[s, 1 - c], out_refs[i].at[s], send, recv, base + N_CHIPS * i + s, (x, y, 1 - c))
                for i in range(n) for s in range(N_CHIPS)]

    def start(*a):
        for cp in copies(*a):
            cp.start()

    def wait(*a):
        for cp in copies(*a):
            cp.wait()

    return Comm(gs, [jax.ShapeDtypeStruct((N_CHIPS,) + g.shape[2:], g.dtype) for g in gs], N_CHIPS * n, start, wait)


def exchange_chips(ps):
    n = len(ps)

    def start(in_refs, out_refs, send, recv, base):
        x, y, c, chips = _place()
        for i in range(n):
            for k, (px, py) in enumerate(chips):
                _rcopy(in_refs[i].at[2 * px + py], out_refs[i].at[2 * x + y], send, recv, base + 3 * i + k,
                       (px, py, c)).start()

    def wait(in_refs, out_refs, send, recv, base):
        x, y, c, chips = _place()
        for i in range(n):
            for k, (px, py) in enumerate(chips):
                _rcopy(in_refs[i].at[2 * px + py], out_refs[i].at[2 * px + py], send, recv, base + 3 * i + k,
                       (px, py, c)).wait()

    return Comm(ps, [jax.ShapeDtypeStruct(p.shape, p.dtype) for p in ps], 3 * n, start, wait)


def join_halves(fs):
    n = len(fs)

    def start(in_refs, out_refs, send, recv, base):
        x, y, c, _ = _place()
        for i in range(n):
            _rcopy(in_refs[i].at[c], out_refs[i].at[c], send, recv, base + i, (x, y, 1 - c)).start()

    def wait(in_refs, out_refs, send, recv, base):
        x, y, c, _ = _place()
        for i in range(n):
            _rcopy(in_refs[i].at[c], out_refs[i].at[1 - c], send, recv, base + i, (x, y, 1 - c)).wait()

    return Comm(fs, [jax.ShapeDtypeStruct(f.shape, f.dtype) for f in fs], n, start, wait, {i: i for i in range(n)})


def allreduce_small(v, *, name):
    m_per, n = v.shape

    def body(x_ref, sum_ref, all_ref, send_sems, recv_sems, local_sem):
        x, y, c, chips = _place()
        me, sibling = (x, y, c), (x, y, 1 - c)

        def rows(px, py, pc):
            return all_ref.at[pl.ds((4 * px + 2 * py + pc) * m_per, m_per), :]

        def copy(k, block, to, src=None):
            return pltpu.make_async_remote_copy(src_ref=rows(*block) if src is None else src, dst_ref=rows(*block),
                                                send_sem=send_sems.at[k], recv_sem=recv_sems.at[k], device_id=to,
                                                device_id_type=MESH)

        mine = pltpu.make_async_copy(x_ref, rows(*me), local_sem)
        mine.start()
        first = [copy(0, me, sibling, src=x_ref)]
        first += [copy(1 + j, me, (*chip, c), src=x_ref) for j, chip in enumerate(chips)]
        for cp in first:
            cp.start()
        passed = [copy(4 + j, (*chip, c), sibling) for j, chip in enumerate(chips)]
        for j, chip in enumerate(chips):
            copy(1 + j, (*chip, c), me).wait_recv()
            passed[j].start()
        copy(0, sibling, me).wait_recv()
        for j, chip in enumerate(chips):
            copy(4 + j, (*chip, 1 - c), me).wait_recv()
        for cp in first + passed:
            cp.wait_send()
        mine.wait()
        acc = all_ref[0:m_per, :]
        for d in range(1, N_DEV):
            acc = acc + all_ref[d * m_per:(d + 1) * m_per, :]
        sum_ref[...] = acc

    vm = pl.BlockSpec(memory_space=pltpu.VMEM)
    return pl.pallas_call(
        body, name=name, in_specs=[vm], out_specs=vm, out_shape=jax.ShapeDtypeStruct((m_per, n), F32),
        scratch_shapes=[pltpu.VMEM((N_DEV * m_per, n), F32), pltpu.SemaphoreType.DMA((7,)),
                        pltpu.SemaphoreType.DMA((7,)), pltpu.SemaphoreType.DMA],
    )(v)


def _cols_to_slots(w):
    r, c4 = w.shape
    return w.reshape(r, N_CHIPS, c4 // N_CHIPS).transpose(1, 0, 2)


def _slots_to_cols(w):
    n, r, c = w.shape
    return w.transpose(1, 0, 2).reshape(r, n * c)


def _pad_cols(a, width):
    return jnp.pad(a, ((0, 0), (0, width - a.shape[1])))


class InLayout:
    def __init__(self, q_rank, kv_rank):
        gk = GLA_HEADS * GLA_DK
        gv = GLA_HEADS * GLA_DV
        sizes = [q_rank, kv_rank, MLA_ROPE, gk, gk, gv, GLA_GATE_RANK, gv]
        names = ["zq", "zkv", "zkr", "gq", "gk", "gv", "zg", "zr"]
        starts = np.concatenate([[0], np.cumsum(sizes)[:-1]])
        self.ref = {n: (int(s), int(z)) for n, s, z in zip(names, starts, sizes)}
        self.ref_width = int(sum(sizes))
        self.order = ["gv", "zr", "zq", "gq", "gk", "zkv", "zkr", "zg"]
        self.off, self.size = {}, {}
        pos = 0
        for n in self.order:
            padded = -(-self.ref[n][1] // LANE) * LANE
            self.off[n], self.size[n] = pos, padded
            pos += padded
        self.width = pos
        self.shard = self.ref_width // N_CHIPS
        self.shard_pad = -(-self.shard // LANE) * LANE

    def _pieces(self, lo, hi):
        out = []
        while lo < hi:
            s = lo // self.shard
            end = min(hi, (s + 1) * self.shard)
            out.append((s * self.shard_pad + lo - s * self.shard, s * self.shard_pad + end - s * self.shard))
            lo = end
        return out

    def from_shards(self, zs):
        cols = []
        for n in self.order:
            start, size = self.ref[n]
            cols += [zs[:, a:b] for a, b in self._pieces(start, start + size)]
            if self.size[n] > size:
                cols.append(jnp.zeros((zs.shape[0], self.size[n] - size), zs.dtype))
        return jnp.concatenate(cols, axis=1)

    def to_shards(self, dz):
        names = sorted(self.ref, key=lambda n: self.ref[n][0])
        ref = jnp.concatenate([dz[:, self.off[n]:self.off[n] + self.ref[n][1]] for n in names], axis=1)
        ref = ref.reshape(dz.shape[0], N_CHIPS, self.shard)
        return jnp.pad(ref, ((0, 0), (0, 0), (0, self.shard_pad - self.shard))).reshape(dz.shape[0], -1)


def _pad_q_up(w):
    r = w.shape[0]
    w = w.reshape(r, MLA_HEADS, MLA_QK)
    w = jnp.pad(w, ((0, 0), (0, 0), (0, MLA_HEAD_PAD - MLA_QK)))
    return w.reshape(r, MLA_HEADS * MLA_HEAD_PAD)


def _unpad_q_up(g):
    r = g.shape[0]
    return g.reshape(r, MLA_HEADS, MLA_HEAD_PAD)[:, :, :MLA_QK].reshape(r, MLA_HEADS * MLA_QK)


def _rope_tables(positions):
    half = MLA_ROPE // 2
    inv_freq = ROPE_THETA ** (-jnp.arange(half, dtype=F32) / half)
    ang = positions.astype(F32).reshape(-1, 1) * inv_freq
    cos, sin = jnp.cos(ang), jnp.sin(ang)
    s = ang.shape[0]
    cosf = jnp.concatenate([cos, cos, jnp.ones((s, LANE - MLA_ROPE), F32)], axis=1)
    sinf = jnp.concatenate([sin, sin, jnp.zeros((s, LANE - MLA_ROPE), F32)], axis=1)
    rot = np.zeros((LANE, LANE), np.float32)
    for j in range(half):
        rot[j + half, j] = -1.0
        rot[j, j + half] = 1.0
    return cosf, sinf, jnp.asarray(rot)


SMALL = ["ffn1_norm", "mix_norm", "q_a_norm", "kv_a_norm", "mla_q_norm", "mla_k_norm", "gla_b_gate", "gla_out_norm",
         "mem_attn_norm", "mem_norm", "mem_q_norm", "mem_k_norm", "ffn2_norm"]
BIG = ["ffn1_w_gate", "ffn1_w_up", "ffn1_w_down", "w_in", "w_q_up", "w_kv_up", "w_out", "mem_w_q", "mem_w_k",
       "mem_w_v", "mem_w_o", "ffn2_w_gate", "ffn2_w_up", "ffn2_w_down"]
COL_SHARDED = {"ffn1_w_gate", "ffn1_w_up", "w_in", "w_q_up", "w_kv_up", "gla_w_gate2", "mem_w_o", "ffn2_w_gate", "ffn2_w_up"}
WEIGHTS = ["ffn1_norm", "ffn1_w_gate", "ffn1_w_up", "ffn1_w_down", "mix_norm", "w_in", "q_a_norm", "w_q_up", "kv_a_norm",
           "w_kv_up", "mla_q_norm", "mla_k_norm", "gla_w_gate2", "gla_b_gate", "gla_out_norm", "w_out", "mem_attn_norm",
           "mem_norm", "mem_w_q", "mem_w_k", "mem_w_v", "mem_w_o", "mem_q_norm", "mem_k_norm", "ffn2_norm", "ffn2_w_gate",
           "ffn2_w_up", "ffn2_w_down"]


def _pack_small(vals, rows=8):
    flat = jnp.concatenate([v.reshape(-1).astype(F32) for v in vals])
    n = flat.shape[0]
    per = -(-n // (rows * LANE)) * LANE
    return jnp.pad(flat, (0, rows * per - n)).reshape(rows, per)


def _unpack_small(packed, shapes):
    flat = packed.reshape(-1)
    out, pos = [], 0
    for s in shapes:
        n = int(np.prod(s))
        out.append(flat[pos:pos + n].reshape(s))
        pos += n
    return out


FFN1 = ["ffn1_w_gate", "ffn1_w_up", "ffn1_w_down"]
FFN2 = ["ffn2_w_gate", "ffn2_w_up", "ffn2_w_down"]
SLOT_WEIGHTS = {"ffn1_w_gate", "ffn1_w_up", "ffn2_w_gate", "ffn2_w_up", "w_in"}
MID_A = ["w_in", "w_q_up", "w_kv_up", "gla_w_gate2"]
MID_B = ["w_out", "mem_w_q", "mem_w_k", "mem_w_v", "mem_w_o"]


def _with(res, comm):
    return res if comm is not None else (res, None)


def kernel(x, mem, positions, ffn1_norm, ffn1_w_gate, ffn1_w_up, ffn1_w_down, mix_norm, w_in, q_a_norm, w_q_up, kv_a_norm, w_kv_up, mla_q_norm, mla_k_norm, gla_w_gate2, gla_b_gate, gla_out_norm, w_out, mem_attn_norm, mem_norm, mem_w_q, mem_w_k, mem_w_v, mem_w_o, mem_q_norm, mem_k_norm, ffn2_norm, ffn2_w_gate, ffn2_w_up, ffn2_w_down, loss_target, m_ffn1_norm, m_ffn1_w_gate, m_ffn1_w_up, m_ffn1_w_down, m_mix_norm, m_w_in, m_q_a_norm, m_w_q_up, m_kv_a_norm, m_w_kv_up, m_mla_q_norm, m_mla_k_norm, m_gla_w_gate2, m_gla_b_gate, m_gla_out_norm, m_w_out, m_mem_attn_norm, m_mem_norm, m_mem_w_q, m_mem_w_k, m_mem_w_v, m_mem_w_o, m_mem_q_norm, m_mem_k_norm, m_ffn2_norm, m_ffn2_w_gate, m_ffn2_w_up, m_ffn2_w_down, v_ffn1_norm, v_ffn1_w_gate, v_ffn1_w_up, v_ffn1_w_down, v_mix_norm, v_w_in, v_q_a_norm, v_w_q_up, v_kv_a_norm, v_w_kv_up, v_mla_q_norm, v_mla_k_norm, v_gla_w_gate2, v_gla_b_gate, v_gla_out_norm, v_w_out, v_mem_attn_norm, v_mem_norm, v_mem_w_q, v_mem_w_k, v_mem_w_v, v_mem_w_o, v_mem_q_norm, v_mem_k_norm, v_ffn2_norm, v_ffn2_w_gate, v_ffn2_w_up, v_ffn2_w_down):
    args = dict(locals())
    two_d = lambda a: a[0] if a.ndim == 3 else a
    W = {n: two_d(args[n]) for n in WEIGHTS}
    M1 = {n: two_d(args["m_" + n]) for n in WEIGHTS}
    V2 = {n: two_d(args["v_" + n]) for n in WEIGHTS}
    xs, mems, tgt = x[0], mem[0], loss_target[0]
    S, D = xs.shape
    chip = 2 * lax.axis_index("x") + lax.axis_index("y")

    q_rank, kv_rank = W["w_q_up"].shape[0], W["w_kv_up"].shape[0]
    lay = InLayout(q_rank, kv_rank)
    off = lay.off
    up_names, down_names = FFN1[:2], FFN1[2:]
    shard16 = {n: W[n].astype(BF16) for n in up_names}
    full = {}

    def stage1(names):
        return gather_stage1([shard16[n] for n in names], [n != "gla_w_gate2" for n in names])

    def stage2(names, slots):
        return gather_stage2(slots, [shard16[n] for n in names], [n != "gla_w_gate2" for n in names])

    def finish(names, slots):
        for n, s in zip(names, slots):
            s = s.reshape((N_CHIPS,) + shard16[n].shape)
            if n in SLOT_WEIGHTS:
                full[n] = s
            else:
                full[n] = _slots_to_cols(s) if n in COL_SHARDED else s.reshape(-1, s.shape[2])

    token = {"last": None}

    def begin(comm, name, after=None):
        started = start_comm(comm, name=name, after=token["last"] if after is None else after)
        token["last"] = started[-1]
        return comm, started

    first = [begin(stage1([n]), f"gather_start_{n}") for n in up_names]
    zero = token["last"][0, 0]
    later = [n for n in BIG + ["gla_w_gate2"] if n not in shard16]
    for n in later:
        shard16[n] = (W[n] + zero).astype(BF16)
    shard16["w_in"] = _pad_cols(shard16["w_in"], lay.shard_pad)
    n1 = row_fwd(rms_fn, [V(xs)], [W["ffn1_norm"]], [(D, BF16)], [(0, 0, 0)], name="ffn1_norm")[0]
    cosf, sinf, rot = _rope_tables(positions[0])
    gate_s1 = wait_comm(*first[0], [n1, cosf, sinf] + [shard16[n] for n in later], name=f"gather_wait_{up_names[0]}")
    finish(up_names[:1], run_comm(stage2(up_names[:1], gate_s1), name="pass_ffn1_gate"))
    up_s1 = wait_comm(*first[1], full[up_names[0]], name=f"gather_wait_{up_names[1]}")
    down1 = begin(stage1(down_names), "gather_start_ffn1_down", after=up_s1[0])
    mid_a_s1 = begin(stage1(MID_A), "gather_start_mid_a")
    finish(up_names[1:], run_comm(stage2(up_names[1:], up_s1), name="pass_ffn1_up"))
    tri = jnp.asarray(np.tril(np.ones((CHUNK, CHUNK), np.float32)))
    gqn = W["mla_q_norm"][:, :MLA_NOPE]
    gqr = _pad_cols(W["mla_q_norm"][:, MLA_NOPE:], LANE)
    gkn = W["mla_k_norm"][:, :MLA_NOPE]
    gkr = _pad_cols(W["mla_k_norm"][:, MLA_NOPE:], LANE)
    HP = MLA_HEAD_PAD
    mla_scale = MLA_QK ** -0.5
    mem_scale = MEM_HEAD_DIM ** -0.5
    mla_w = MLA_HEADS * MLA_V
    gla_w = GLA_HEADS * GLA_DV
    mem_w = MEM_HEADS * MEM_HEAD_DIM

    gate1, up1, act1 = ffn_up(n1, full["ffn1_w_gate"], full["ffn1_w_up"], name="ffn1_up", behind=token["last"])
    finish(down_names, run_comm(stage2(down_names, wait_comm(*down1, act1, name="gather_wait_ffn1_down")),
                                name="pass_ffn1_down"))
    mid_a1 = wait_comm(*mid_a_s1, act1, name="gather_wait_mid_a")
    mid_b = begin(stage1(MID_B), "gather_start_mid_b", after=mid_a1[0])
    x1, got = mm([(act1, full["ffn1_w_down"])], "nn", F32, alpha=0.5, res=xs, name="ffn1_down",
                 comm=stage2(MID_A, mid_a1), behind=token["last"])
    ffn1_saved = (n1, gate1, up1, act1)
    finish(MID_A, got)
    ffn2_s1 = [begin(stage1([n]), f"gather_start_{n}", after=x1 if n == FFN2[0] else None) for n in FFN2]
    w_q_up_p = _pad_q_up(full["w_q_up"])
    w_gate2_p = jnp.pad(full["gla_w_gate2"], ((0, LANE - GLA_GATE_RANK), (0, 0)))
    h = row_fwd(rms_fn, [V(x1)], [W["mix_norm"]], [(D, BF16)], [(0, 0, 0)], name="mix_norm")[0]
    mid_b1 = wait_comm(*mid_b, h, name="gather_wait_mid_b")
    z_shards, got = mm([(h, full["w_in"])], "nn", F32, name="in_proj", b_slots=True, comm=stage2(MID_B, mid_b1),
                       behind=token["last"])
    z = lay.from_shards(z_shards)
    finish(MID_B, got)
    qa = row_fwd(rms_fn, [V(z, off["zq"], q_rank)], [W["q_a_norm"]], [(q_rank, BF16)], [(0, 0, 0)], name="q_a_norm")[0]
    kva = row_fwd(rms_fn, [V(z, off["zkv"], kv_rank)], [W["kv_a_norm"]], [(kv_rank, BF16)], [(0, 0, 0)], name="kv_a_norm")[0]
    qraw = mm([(qa, w_q_up_p)], "nn", F32, name="q_up")
    kvraw = mm([(kva, full["w_kv_up"])], "nn", F32, name="kv_up")
    tabs = [V(cosf, diff=False), V(sinf, diff=False)]
    q_rows = [V(qraw, 0, LANE, HP), V(qraw, LANE, LANE, HP)] + tabs
    k_rows = [V(kvraw, 0, LANE, HP), V(z, off["zkr"], LANE, 0)] + tabs
    qh = row_fwd(qk_prep_fn, q_rows, [gqn, gqr, rot], [(MLA_HEADS * HP, BF16)], [(0, 0, HP), (0, LANE, HP)],
                 heads=MLA_HEADS, name="q_prep")[0]
    kh = row_fwd(qk_prep_fn, k_rows, [gkn, gkr, rot], [(MLA_HEADS * HP, BF16)], [(0, 0, HP), (0, LANE, HP)],
                 heads=MLA_HEADS, name="k_prep")[0]
    mla_kw = dict(heads=MLA_HEADS, dk=HP, dv=MLA_V, v_off=1, v_hs=2, scale=mla_scale, causal=True, tq=512)
    o_mla = attn_fwd(qh, kh, kvraw, name="mla_attn", **mla_kw)

    zg = z[:, off["zg"]:off["zg"] + LANE]
    pre = mm([(zg, w_gate2_p)], "nn", F32, name="gla_gate")
    la = row_fwd(gate_fn, [V(pre)], [W["gla_b_gate"]], [(pre.shape[1], F32)], [(0, 0, 0)], name="gla_log_decay")[0]
    gla_kw = dict(q_off=off["gq"], k_off=off["gk"], v_off=off["gv"])
    f2_gate = wait_comm(*ffn2_s1[0], la, name=f"gather_wait_{FFN2[0]}")
    (o_raw, states), got = gla_fwd(z, la, tri, name="gla_scan", comm=stage2(FFN2[:1], f2_gate), **gla_kw)
    finish(FFN2[:1], got)
    gla_rows = [V(o_raw, 0, GLA_DV, GLA_DV), V(z, off["zr"], GLA_DV, GLA_DV)]
    o_gla = row_fwd(gla_out_fn, gla_rows, [W["gla_out_norm"]], [(gla_w, BF16)], [(0, 0, GLA_DV)], heads=GLA_HEADS,
                    name="gla_out")[0]
    o_cat = jnp.concatenate([o_mla, o_gla], axis=1)
    f2_up = wait_comm(*ffn2_s1[1], o_cat, name=f"gather_wait_{FFN2[1]}")
    x2, got = mm([(o_cat, full["w_out"])], "nn", F32, res=x1, name="out_proj", comm=stage2(FFN2[1:2], f2_up))
    finish(FFN2[1:2], got)

    hm = row_fwd(rms_fn, [V(x2)], [W["mem_attn_norm"]], [(D, BF16)], [(0, 0, 0)], name="mem_attn_norm")[0]
    mn = row_fwd(rms_fn, [V(mems)], [W["mem_norm"]], [(D, BF16)], [(0, 0, 0)], name="mem_norm")[0]
    qm_raw = mm([(hm, full["mem_w_q"])], "nn", F32, name="mem_q")
    km_raw = mm([(mn, full["mem_w_k"])], "nn", F32, name="mem_k")
    vm = mm([(mn, full["mem_w_v"])], "nn", F32, name="mem_v")
    hd = MEM_HEAD_DIM
    qm = row_fwd(rms_fn, [V(qm_raw, 0, hd, hd)], [W["mem_q_norm"]], [(mem_w, BF16)], [(0, 0, hd)], heads=MEM_HEADS,
                 name="mem_q_norm")[0]
    km = row_fwd(rms_fn, [V(km_raw, 0, hd, hd)], [W["mem_k_norm"]], [(mem_w, BF16)], [(0, 0, hd)], heads=MEM_HEADS,
                 name="mem_k_norm")[0]
    mem_kw = dict(heads=MEM_HEADS, dk=hd, dv=hd, v_off=0, v_hs=1, scale=mem_scale, causal=False, tq=1024)
    om = attn_fwd(qm, km, vm, name="mem_attn", **mem_kw)
    x3 = mm([(om, full["mem_w_o"])], "nn", F32, res=x2, name="mem_o")

    n2 = row_fwd(rms_fn, [V(x3)], [W["ffn2_norm"]], [(D, BF16)], [(0, 0, 0)], name="ffn2_norm")[0]
    f2_down = wait_comm(*ffn2_s1[2], n2, name=f"gather_wait_{FFN2[2]}")
    (gate2, up2, act2), got = ffn_up(n2, full["ffn2_w_gate"], full["ffn2_w_up"], name="ffn2_up",
                                     comm=stage2(FFN2[2:], f2_down))
    finish(FFN2[2:], got)
    y = mm([(act2, full["ffn2_w_down"])], "nn", F32, alpha=0.5, res=x3, name="ffn2_down")
    dy, dy16, loss_part = loss_head(y, tgt, name="loss_head")
    G = {"loss": loss_part[:, :1]}

    chip_sum, reduced = {}, {}

    def to_halves(n):
        g = G[n]
        if n in SLOT_WEIGHTS:
            s = g
        else:
            s = _cols_to_slots(g) if n in COL_SHARDED else g.reshape(N_CHIPS, g.shape[0] // N_CHIPS, g.shape[1])
        return s.reshape(N_CHIPS, 2, s.shape[1] // 2, s.shape[2])

    def add2(names, halves, got):
        for n, a, b in zip(names, halves, got):
            chip_sum[n] = add_own_half(a, b, BF16, name=f"rs_add2_{n}")

    to_join = []

    def add4_join(names, parts):
        for n, p in zip(names, parts):
            to_join.append((n, sum_chip_parts(chip_sum[n], p, name=f"rs_add4_{n}")))

    def with_joins(comm):
        names, totals = [n for n, _ in to_join], [t for _, t in to_join]
        to_join.clear()
        if not names:
            return comm, lambda got: got
        own = 0 if comm is None else len(comm.out_shapes)
        joined = join_halves(totals)

        def split(got):
            for n, b in zip(names, got[own:]):
                reduced[n] = b.reshape(-1, b.shape[2])[:, :W[n].shape[1]]
            return got[:own]

        return (joined if comm is None else merge_comms(comm, joined)), split

    def flush_joins():
        comm, split = with_joins(None)
        if comm is not None:
            split(run_comm(comm, name=f"rs_join_{len(reduced)}"))

    in_flight = []

    def xchg_start(names):
        in_flight.append((names,) + begin(exchange_chips([chip_sum[n] for n in names]), f"xchg_start_{names[0]}"))

    def xchg_wait(after, count=1):
        for _ in range(count):
            names, comm, started = in_flight.pop(0)
            add4_join(names, wait_comm(comm, started, after, name=f"xchg_wait_{names[0]}"))

    def ffn_backward(dout, dout16, xin, tag, saved, dact_comm=None, after_dact=None):
        n_, gate, up, act = saved
        nd, ng, nu = f"{tag}_w_down", f"{tag}_w_gate", f"{tag}_w_up"
        (dgate, dup), got0 = _with(ffn_dact(dout16, full[nd], gate, up, 0.5, name=f"{tag}_dact", comm=dact_comm,
                                            behind=token["last"]), dact_comm)
        if after_dact:
            after_dact(got0)
        G[nd] = mm([(act, dout16)], "tn", F32, alpha=0.5, name=f"{tag}_dwd", tm=1408, tn=1024, behind=token["last"])
        hd_ = to_halves(nd)
        comm, split = with_joins(swap_halves([hd_]))
        G[ng], got = mm([(n_, dgate)], "tn", F32, name=f"{tag}_dwg", out_slots=True, tm=1024, tn=1408, rows_inner=True,
                        comm=comm)
        add2([nd], [hd_], split(got))
        xchg_start([nd])
        hg = to_halves(ng)
        G[nu], got_g = mm([(n_, dup)], "tn", F32, name=f"{tag}_dwu", out_slots=True, tm=1024, tn=1408, rows_inner=True,
                          comm=swap_halves([hg]), behind=token["last"])
        add2([ng], [hg], got_g)
        xchg_start([ng])
        hu = to_halves(nu)
        dn, got_u = mm([(dgate, full[ng]), (dup, full[nu])], "nt", F32, name=f"{tag}_dn", b_slots=True, tn=1024, tk=1408,
                       comm=swap_halves([hu]), behind=token["last"])
        add2([nu], [hu], got_u)
        xchg_start([nu])
        dx, G[f"{tag}_norm"] = row_bwd(rms_fn, [V(xin)], [W[f"{tag}_norm"]], [V(dn)], const_diff=[True], res=dout,
                                       name=f"{tag}_dnorm")
        return dx

    g3 = ffn_backward(dy, dy16, x3, "ffn2", (n2, gate2, up2, act2))
    xchg_wait(g3)

    d_om = mm([(g3, full["mem_w_o"])], "nt", F32, name="mem_o_dx", behind=token["last"])
    G["mem_w_o"] = mm([(om, g3)], "tn", F32, name="mem_o_dw")
    dqm, dkm, dvm = attn_bwd(qm, km, vm, d_om, name="mem_attn_bwd", **mem_kw)
    dqm_raw, G["mem_q_norm"] = row_bwd(rms_fn, [V(qm_raw, 0, hd, hd)], [W["mem_q_norm"]], [V(dqm, 0, hd, hd)],
                                       const_diff=[True], heads=MEM_HEADS, row_dtype=BF16, name="mem_q_norm_bwd")
    dkm_raw, G["mem_k_norm"] = row_bwd(rms_fn, [V(km_raw, 0, hd, hd)], [W["mem_k_norm"]], [V(dkm, 0, hd, hd)],
                                       const_diff=[True], heads=MEM_HEADS, row_dtype=BF16, name="mem_k_norm_bwd")
    dhm = mm([(dqm_raw, full["mem_w_q"])], "nt", F32, name="mem_q_dx")
    G["mem_w_q"] = mm([(hm, dqm_raw)], "tn", F32, name="mem_q_dw")
    dmn = mm([(dkm_raw, full["mem_w_k"]), (dvm, full["mem_w_v"])], "nt", F32, name="mem_kv_dx")
    G["mem_w_k"] = mm([(mn, dkm_raw)], "tn", F32, name="mem_k_dw")
    G["mem_w_v"] = mm([(mn, dvm)], "tn", F32, name="mem_v_dw")
    _, G["mem_norm"] = row_bwd(rms_fn, [V(mems)], [W["mem_norm"]], [V(dmn)], const_diff=[True], row_dtype=BF16,
                               name="mem_norm_bwd")
    g2, G["mem_attn_norm"] = row_bwd(rms_fn, [V(x2)], [W["mem_attn_norm"]], [V(dhm)], const_diff=[True], res=g3,
                                     name="mem_attn_norm_bwd")

    xchg_wait(g2, 2)

    d_ocat = mm([(g2, full["w_out"])], "nt", F32, name="out_proj_dx")
    G["w_out"] = mm([(o_cat, g2)], "tn", F32, name="out_proj_dw")

    d_oraw, d_zr, G["gla_out_norm"] = row_bwd(gla_out_fn, gla_rows, [W["gla_out_norm"]],
                                              [V(d_ocat, mla_w, GLA_DV, GLA_DV)], const_diff=[True], heads=GLA_HEADS,
                                              name="gla_out_bwd")
    mid_b_halves = [to_halves(n) for n in MID_B]
    (d_gq, d_gk, d_gv, d_la), got = gla_bwd(z, la, tri, tri.T, states, d_oraw, name="gla_scan_bwd",
                                            comm=swap_halves(mid_b_halves), **gla_kw)
    add2(MID_B, mid_b_halves, got)
    xchg_start(MID_B)
    d_pre, G["gla_b_gate"] = row_bwd(gate_fn, [V(pre)], [W["gla_b_gate"]], [V(d_la)], const_diff=[True], row_dtype=BF16,
                                     name="gla_log_decay_bwd")
    d_zg = mm([(d_pre, w_gate2_p)], "nt", BF16, name="gla_gate_dx", behind=token["last"])
    G["gla_w_gate2"] = mm([(zg, d_pre)], "tn", F32, name="gla_gate_dw")[:GLA_GATE_RANK]

    comm, split = with_joins(None)
    (d_qh, d_kh, d_v), got = _with(attn_bwd(qh, kh, kvraw, d_ocat, name="mla_attn_bwd", comm=comm, **mla_kw), comm)
    split(got)
    cq = [V(d_qh, 0, LANE, HP), V(d_qh, LANE, LANE, HP)]
    ck = [V(d_kh, 0, LANE, HP), V(d_kh, LANE, LANE, HP)]
    d_qraw, d_gqn, d_gqr = row_bwd(qk_prep_fn, q_rows, [gqn, gqr, rot], cq, const_diff=[True, True, False],
                                   heads=MLA_HEADS, row_dtype=BF16, pack={0: (0, HP), 1: (LANE, HP)},
                                   pack_width=MLA_HEADS * HP, name="q_prep_bwd")
    d_kvraw, d_zkr, d_gkn, d_gkr = row_bwd(qk_prep_fn, k_rows, [gkn, gkr, rot], ck, const_diff=[True, True, False],
                                           heads=MLA_HEADS, row_dtype=BF16, pack={0: (0, HP)}, pack_width=MLA_HEADS * HP,
                                           fills=[(V(d_v, 0, MLA_V, MLA_V), LANE, HP)], name="k_prep_bwd")
    G["mla_q_norm"] = jnp.concatenate([d_gqn, d_gqr[:, :MLA_ROPE]], axis=1)
    G["mla_k_norm"] = jnp.concatenate([d_gkn, d_gkr[:, :MLA_ROPE]], axis=1)
    d_qa = mm([(d_qraw, w_q_up_p)], "nt", F32, name="q_up_dx")
    G["w_q_up"] = _unpad_q_up(mm([(qa, d_qraw)], "tn", F32, name="q_up_dw"))
    d_kva = mm([(d_kvraw, full["w_kv_up"])], "nt", F32, name="kv_up_dx")
    G["w_kv_up"] = mm([(kva, d_kvraw)], "tn", F32, name="kv_up_dw")
    d_zq, G["q_a_norm"] = row_bwd(rms_fn, [V(z, off["zq"], q_rank)], [W["q_a_norm"]], [V(d_qa)], const_diff=[True],
                                  row_dtype=BF16, name="q_a_norm_bwd")
    d_zkv, G["kv_a_norm"] = row_bwd(rms_fn, [V(z, off["zkv"], kv_rank)], [W["kv_a_norm"]], [V(d_kva)], const_diff=[True],
                                    row_dtype=BF16, name="kv_a_norm_bwd")

    seg = {"gv": d_gv, "zr": d_zr, "zq": d_zq, "gq": d_gq, "gk": d_gk, "zkv": d_zkv, "zkr": d_zkr, "zg": d_zg}
    dz = jnp.concatenate([_pad_cols(seg[n].astype(BF16), lay.size[n]) for n in lay.order], axis=1)
    xchg_wait(dz)
    comm, split = with_joins(None)
    dz_shards = lay.to_shards(dz)
    dh, got = _with(mm([(dz_shards, full["w_in"])], "nt", F32, name="in_proj_dx", b_slots=True, comm=comm), comm)
    split(got)
    G["w_in"] = mm([(h, dz_shards)], "tn", F32, name="in_proj_dw", out_slots=True)
    g1, g1_16, G["mix_norm"] = row_bwd(rms_fn, [V(x1)], [W["mix_norm"]], [V(dh)], const_diff=[True], res=g2, copy16=True,
                                       name="mix_norm_bwd")

    mid_a = [n for n in MID_A if n != "gla_w_gate2"]
    mid_a_halves = [to_halves(n) for n in mid_a]

    def mid_a_sums(got):
        add2(mid_a, mid_a_halves, got)
        xchg_start(mid_a)

    gx = ffn_backward(g1, g1_16, xs, "ffn1", ffn1_saved, dact_comm=swap_halves(mid_a_halves), after_dact=mid_a_sums)
    xchg_wait(gx, 2)

    grad, delta, new_m, new_v = {}, {}, {}, {}

    def adam_group(names, tag, behind=None):
        if any(n not in reduced for n in names):
            flush_joins()
        res = adamw([(W[n], reduced[n], M1[n], V2[n]) for n in names], name=f"adamw_{tag}", behind=behind)
        for n, (g_, d_, m_, v_) in zip(names, res):
            grad[n], delta[n], new_m[n], new_v[n] = g_, d_, m_, v_

    adam_group(FFN2, "ffn2", behind=token["last"])
    adam_group(mid_a + MID_B, "mid", behind=token["last"])
    adam_group(FFN1[2:], "ffn1_down", behind=token["last"])
    xchg_wait(delta[FFN1[2]], 2)
    adam_group(FFN1[:2], "ffn1_up")

    small_names = SMALL + ["gla_w_gate2"]
    packed = small_names + ["loss"]
    small_sum = allreduce_small(_pack_small([G[n] for n in packed]), name="allreduce_small")
    small_g = dict(zip(packed, _unpack_small(small_sum, [G[n].shape for n in packed])))
    loss = small_g["loss"][0, 0]
    shard_c = W["gla_w_gate2"].shape[1]
    grad["gla_w_gate2"] = lax.dynamic_slice_in_dim(small_g["gla_w_gate2"], chip * shard_c, shard_c, axis=1)
    pw = _pack_small([W[n] for n in SMALL] + [W["gla_w_gate2"]])
    pg = _pack_small([small_g[n] for n in SMALL] + [grad["gla_w_gate2"]])
    pm = _pack_small([M1[n] for n in SMALL] + [M1["gla_w_gate2"]])
    pv = _pack_small([V2[n] for n in SMALL] + [V2["gla_w_gate2"]])
    (_, pd, pnm, pnv), = adamw([(pw, pg, pm, pv)], name="adamw_small")
    shapes = [W[n].shape for n in small_names]
    for n, d_, m_, v_ in zip(small_names, _unpack_small(pd, shapes), _unpack_small(pnm, shapes), _unpack_small(pnv, shapes)):
        delta[n], new_m[n], new_v[n] = d_, m_, v_
        if n != "gla_w_gate2":
            grad[n] = small_g[n]

    lead = lambda d: [d[n].reshape(args[n].shape) for n in WEIGHTS]
    return (loss, gx[None], *lead(grad), *lead(delta), *lead(new_m), *lead(new_v))
```
